```python
import math
import jax
import jax.numpy as jnp
from jax import lax
import numpy as np

D_MODEL = 1024
BATCH = 2
SEQ = 16384
DEPTH = 1

CHUNK = 64
N_META = 16
D_MIX = D_MODEL
ATT_W = D_MIX // 2
ATT_HEADS = 4
HEAD_DIM = ATT_W // (2 * ATT_HEADS)
ATT_Q = ATT_HEADS * 2 * HEAD_DIM
ATT_V = ATT_HEADS * 2 * HEAD_DIM
SSM_W = D_MIX - ATT_W
SSM_CG = 16
SSM_G = SSM_W // SSM_CG
SSM_N = 64
IN_COLS = 2 * ATT_Q + ATT_V + SSM_W
N_BUCKETS = 32
MAX_DISTANCE = 128
Q_BLOCK = 128
N_EXPERTS = 32
TOP_K = 4
D_FF = D_MODEL
SWIGLU_LIMIT = 7.0
SWIGLU_ALPHA = 1.702
MOE_BLOCK = 256
LN_EPS = 1e-5
NEG_INF = -1e30
DEEPNORM_ALPHA = (2.0 * DEPTH) ** 0.25
DEEPNORM_BETA = (8.0 * DEPTH) ** -0.25

kernel_name = "hymba_diffattn_s5_moe_deepnorm"


def layer_norm(x, g, b):
    xf = x.astype(jnp.float32)
    mu = jnp.mean(xf, axis=-1, keepdims=True)
    var = jnp.mean(jnp.square(xf - mu), axis=-1, keepdims=True)
    y = (xf - mu) * lax.rsqrt(var + LN_EPS) * g.astype(jnp.float32) + b.astype(jnp.float32)
    return y.astype(x.dtype)


def rms_norm(x, g):
    xf = x.astype(jnp.float32)
    y = xf * lax.rsqrt(jnp.mean(jnp.square(xf), axis=-1, keepdims=True) + LN_EPS) * g.astype(jnp.float32)
    return y.astype(x.dtype)


def t5_bucket(rel):
    nb = N_BUCKETS // 2
    max_exact = nb // 2
    ret = jnp.where(rel > 0, nb, 0)
    n = jnp.abs(rel)
    n_f = jnp.maximum(n, 1).astype(jnp.float32)
    large = max_exact + (jnp.log(n_f / max_exact) / math.log(MAX_DISTANCE / max_exact)
                         * (nb - max_exact)).astype(jnp.int32)
    large = jnp.minimum(large, nb - 1)
    return ret + jnp.where(n < max_exact, n, large)


def diff_attention(q, k, v, lam, lam_init, rel_bias, subln_g, pos, q_chunk, key_chunk):
    bsz, lp = q.shape[0], q.shape[1]
    qh = jnp.transpose(q, (0, 2, 3, 1, 4)) * (HEAD_DIM ** -0.5)
    kh = jnp.transpose(k, (0, 2, 3, 1, 4))
    vh = jnp.transpose(v, (0, 2, 1, 3))
    table = rel_bias.astype(jnp.float32)
    n_blocks = lp // Q_BLOCK

    def one_block(i):
        start = i * Q_BLOCK
        qb = lax.dynamic_slice_in_dim(qh, start, Q_BLOCK, axis=3)
        qpos = lax.dynamic_slice_in_dim(pos, start, Q_BLOCK)
        qc = lax.dynamic_slice_in_dim(q_chunk, start, Q_BLOCK)
        s = jnp.einsum('bhmqd,bhmkd->bhmqk', qb, kh).astype(jnp.float32)
        bias = table[t5_bucket(pos[None, :] - qpos[:, None])]
        s = s + jnp.transpose(bias, (2, 0, 1))[None, :, None]
        visible = key_chunk[None, :] <= qc[:, None]
        s = jnp.where(visible[None, None, None], s, NEG_INF)
        p = jax.nn.softmax(s, axis=-1)
        w = p[:, :, 0] - lam * p[:, :, 1]
        return jnp.einsum('bhqk,bhke->bqhe', w.astype(vh.dtype), vh)

    o = lax.map(one_block, jnp.arange(n_blocks))
    o = jnp.transpose(o, (1, 0, 2, 3, 4)).reshape(bsz, lp, ATT_HEADS, 2 * HEAD_DIM)
    o = rms_norm(o, subln_g) * (1.0 - lam_init)
    return o.reshape(bsz, lp, ATT_W)


def s5_ssm(u, a_re, a_im, log_step, b_re, b_im, c_re, c_im, d_skip, w_glu, b_glu):
    f32 = jnp.float32
    bsz, lp, _ = u.shape
    uf = u.astype(f32).reshape(bsz, lp, SSM_G, SSM_CG)
    step = jnp.exp(log_step.astype(f32))[:, None]
    ar = jnp.minimum(a_re.astype(f32), -1e-4)
    ai = a_im.astype(f32)
    mag = jnp.exp(step * ar)
    ph = step * ai
    abar_re = mag * jnp.cos(ph)
    abar_im = mag * jnp.sin(ph)
    den = ar * ar + ai * ai
    e_re = abar_re - 1.0
    e_im = abar_im
    f_re = (e_re * ar + e_im * ai) / den
    f_im = (e_im * ar - e_re * ai) / den
    br = b_re.astype(f32)
    bi = b_im.astype(f32)
    bb_re = f_re[..., None] * br - f_im[..., None] * bi
    bb_im = f_re[..., None] * bi + f_im[..., None] * br
    bu_re = jnp.einsum('blgc,gnc->blgn', uf, bb_re)
    bu_im = jnp.einsum('blgc,gnc->blgn', uf, bb_im)
    a_seq_re = jnp.broadcast_to(abar_re[None, None], (1, lp, SSM_G, SSM_N))
    a_seq_im = jnp.broadcast_to(abar_im[None, None], (1, lp, SSM_G, SSM_N))

    def combine(e1, e2):
        a1r, a1i, b1r, b1i = e1
        a2r, a2i, b2r, b2i = e2
        return (a1r * a2r - a1i * a2i,
                a1r * a2i + a1i * a2r,
                a2r * b1r - a2i * b1i + b2r,
                a2r * b1i + a2i * b1r + b2i)

    _, _, xr, xi = lax.associative_scan(combine, (a_seq_re, a_seq_im, bu_re, bu_im), axis=1)
    y = (jnp.einsum('blgn,gcn->blgc', xr, c_re.astype(f32))
         - jnp.einsum('blgn,gcn->blgc', xi, c_im.astype(f32))
         + d_skip.astype(f32).reshape(SSM_G, SSM_CG) * uf)
    y = jax.nn.gelu(y.reshape(bsz, lp, SSM_W))
    y = y * jax.nn.sigmoid(y @ w_glu.astype(f32) + b_glu.astype(f32))
    return y.astype(u.dtype)


def moe_ffn(t, w_router, b_router, w_gate, b_gate, w_up, b_up, w_down, b_down):
    n_tok, dm = t.shape
    logits = t.astype(jnp.float32) @ w_router.astype(jnp.float32) + b_router.astype(jnp.float32)
    top_val, top_idx = lax.top_k(logits, TOP_K)
    gates = jax.nn.softmax(top_val, axis=-1)
    nk = n_tok * TOP_K
    flat_e = top_idx.reshape(nk).astype(jnp.int32)
    order = jnp.argsort(flat_e)
    sorted_e = flat_e[order]
    counts = jnp.bincount(flat_e, length=N_EXPERTS)
    padded = (counts + MOE_BLOCK - 1) // MOE_BLOCK * MOE_BLOCK
    padded_end = jnp.cumsum(padded)
    padded_start = padded_end - padded
    group_start = jnp.cumsum(counts) - counts
    dest = padded_start[sorted_e] + jnp.arange(nk, dtype=jnp.int32) - group_start[sorted_e]
    n_blocks = -(-nk // MOE_BLOCK) + N_EXPERTS
    cap = n_blocks * MOE_BLOCK
    buf_tok = jnp.full((cap,), n_tok, jnp.int32).at[dest].set((order // TOP_K).astype(jnp.int32))
    buf_gate = jnp.zeros((cap,), jnp.float32).at[dest].set(gates.reshape(nk)[order])
    block_expert = jnp.minimum(
        jnp.searchsorted(padded_end, jnp.arange(n_blocks, dtype=jnp.int32) * MOE_BLOCK, side='right'),
        N_EXPERTS - 1)
    t_pad = jnp.concatenate([t, jnp.zeros((1, dm), t.dtype)], axis=0)

    def expert_block(args):
        tok, g, e = args
        xb = t_pad[tok]
        gate = xb @ w_gate[e] + b_gate[e]
        up = xb @ w_up[e] + b_up[e]
        gate = jnp.minimum(gate, SWIGLU_LIMIT)
        up = jnp.clip(up, -SWIGLU_LIMIT, SWIGLU_LIMIT)
        y = ((up + 1.0) * gate * jax.nn.sigmoid(gate * SWIGLU_ALPHA)) @ w_down[e] + b_down[e]
        return (y * g[:, None]).astype(t.dtype)

    out = lax.map(expert_block, (buf_tok.reshape(n_blocks, MOE_BLOCK),
                                 buf_gate.reshape(n_blocks, MOE_BLOCK), block_expert))
    y = jnp.zeros((n_tok + 1, dm), t.dtype).at[buf_tok].add(out.reshape(cap, dm))
    return y[:n_tok]


def setup_inputs(seed: int = 0) -> dict:
    key = jax.random.key(seed)
    ks = jax.random.split(key, 40)
    f32 = jnp.float32

    def nrm(k, shape, s):
        return jax.random.normal(k, shape, f32) * s

    col_scale = jnp.concatenate([jnp.ones((2 * ATT_Q,), f32),
                                 jnp.full((ATT_V,), DEEPNORM_BETA, f32),
                                 jnp.ones((SSM_W,), f32)])
    return {
        'x': nrm(ks[0], (BATCH, SEQ, D_MODEL), 1.0),
        'meta_tokens': nrm(ks[1], (N_META, D_MODEL), 1.0),
        'ln_in_g': 1.0 + nrm(ks[2], (D_MODEL,), 0.02),
        'ln_in_b': nrm(ks[3], (D_MODEL,), 0.02),
        'rel_bias': nrm(ks[4], (N_BUCKETS, ATT_HEADS), 0.3),
        'w_in': nrm(ks[5], (DEPTH, D_MODEL, IN_COLS), D_MODEL ** -0.5) * col_scale,
        'lambda_q1': nrm(ks[6], (DEPTH, HEAD_DIM), 0.1),
        'lambda_k1': nrm(ks[7], (DEPTH, HEAD_DIM), 0.1),
        'lambda_q2': nrm(ks[8], (DEPTH, HEAD_DIM), 0.1),
        'lambda_k2': nrm(ks[9], (DEPTH, HEAD_DIM), 0.1),
        'subln_g': 1.0 + nrm(ks[10], (DEPTH, 2 * HEAD_DIM), 0.02),
        'a_re': -0.5 + nrm(ks[11], (DEPTH, SSM_G, SSM_N), 0.01),
        'a_im': math.pi * jnp.arange(SSM_N, dtype=f32) + nrm(ks[12], (DEPTH, SSM_G, SSM_N), 0.01),
        'log_step': jax.random.uniform(ks[13], (DEPTH, SSM_G), f32, math.log(1e-3), math.log(1e-1)),
        'b_re': nrm(ks[14], (DEPTH, SSM_G, SSM_N, SSM_CG), (2.0 * SSM_CG) ** -0.5),
        'b_im': nrm(ks[15], (DEPTH, SSM_G, SSM_N, SSM_CG), (2.0 * SSM_CG) ** -0.5),
        'c_re': nrm(ks[16], (DEPTH, SSM_G, SSM_CG, SSM_N), SSM_N ** -0.5),
        'c_im': nrm(ks[17], (DEPTH, SSM_G, SSM_CG, SSM_N), SSM_N ** -0.5),
        'd_skip': nrm(ks[18], (DEPTH, SSM_W), 1.0),
        'w_glu': nrm(ks[19], (DEPTH, SSM_W, SSM_W), SSM_W ** -0.5),
        'b_glu': nrm(ks[20], (DEPTH, SSM_W), 0.02),
        'w_out': nrm(ks[21], (DEPTH, D_MIX, D_MODEL), D_MIX ** -0.5 * DEEPNORM_BETA),
        'ln1_g': 1.0 + nrm(ks[22], (DEPTH, D_MODEL), 0.02),
        'ln1_b': nrm(ks[23], (DEPTH, D_MODEL), 0.02),
        'w_router': nrm(ks[24], (DEPTH, D_MODEL, N_EXPERTS), D_MODEL ** -0.5),
        'b_router': nrm(ks[25], (DEPTH, N_EXPERTS), 0.01),
        'w_gate': nrm(ks[26], (DEPTH, N_EXPERTS, D_MODEL, D_FF), D_MODEL ** -0.5),
        'b_gate': nrm(ks[27], (DEPTH, N_EXPERTS, D_FF), 0.02),
        'w_up': nrm(ks[28], (DEPTH, N_EXPERTS, D_MODEL, D_FF), D_MODEL ** -0.5),
        'b_up': nrm(ks[29], (DEPTH, N_EXPERTS, D_FF), 0.02),
        'w_down': nrm(ks[30], (DEPTH, N_EXPERTS, D_FF, D_MODEL), D_FF ** -0.5 * DEEPNORM_BETA),
        'b_down': nrm(ks[31], (DEPTH, N_EXPERTS, D_MODEL), 0.02),
        'ln2_g': 1.0 + nrm(ks[32], (DEPTH, D_MODEL), 0.02),
        'ln2_b': nrm(ks[33], (DEPTH, D_MODEL), 0.02),
    }


def reference(x, meta_tokens, ln_in_g, ln_in_b, rel_bias, w_in, lambda_q1, lambda_k1,
              lambda_q2, lambda_k2, subln_g, a_re, a_im, log_step, b_re, b_im, c_re, c_im,
              d_skip, w_glu, b_glu, w_out, ln1_g, ln1_b, w_router, b_router, w_gate, b_gate,
              w_up, b_up, w_down, b_down, ln2_g, ln2_b):
    f32 = jnp.float32
    bsz, seq, dm = x.shape
    total = N_META + seq
    lp = -(-total // Q_BLOCK) * Q_BLOCK
    h = jnp.concatenate([jnp.broadcast_to(meta_tokens.astype(x.dtype)[None], (bsz, N_META, dm)), x], axis=1)
    h = layer_norm(h, ln_in_g, ln_in_b)
    h = jnp.pad(h, ((0, 0), (0, lp - total), (0, 0)))
    pos = jnp.arange(lp, dtype=jnp.int32)
    chunk = jnp.where(pos < N_META, 0, (pos - N_META) // CHUNK + 1)
    key_chunk = jnp.where(pos < total, chunk, jnp.int32(2 ** 30))

    for l in range(DEPTH):
        proj = h @ w_in[l]
        q = proj[..., :ATT_Q].reshape(bsz, lp, ATT_HEADS, 2, HEAD_DIM)
        k = proj[..., ATT_Q:2 * ATT_Q].reshape(bsz, lp, ATT_HEADS, 2, HEAD_DIM)
        v = proj[..., 2 * ATT_Q:2 * ATT_Q + ATT_V].reshape(bsz, lp, ATT_HEADS, 2 * HEAD_DIM)
        u = proj[..., 2 * ATT_Q + ATT_V:]
        lam_init = 0.8 - 0.6 * math.exp(-0.3 * l)
        lam = (jnp.exp(jnp.sum(lambda_q1[l].astype(f32) * lambda_k1[l].astype(f32)))
               - jnp.exp(jnp.sum(lambda_q2[l].astype(f32) * lambda_k2[l].astype(f32))) + lam_init)
        att = diff_attention(q, k, v, lam, lam_init, rel_bias, subln_g[l], pos, chunk, key_chunk)
        ssm = s5_ssm(u, a_re[l], a_im[l], log_step[l], b_re[l], b_im[l], c_re[l], c_im[l],
                     d_skip[l], w_glu[l], b_glu[l])
        mix = jnp.concatenate([att, ssm], axis=-1) @ w_out[l]
        h = layer_norm(DEEPNORM_ALPHA * h + mix, ln1_g[l], ln1_b[l])
        ffn = moe_ffn(h.reshape(bsz * lp, dm), w_router[l], b_router[l], w_gate[l], b_gate[l],
                      w_up[l], b_up[l], w_down[l], b_down[l]).reshape(bsz, lp, dm)
        h = layer_norm(DEEPNORM_ALPHA * h + ffn, ln2_g[l], ln2_b[l])

    return h[:, N_META:N_META + seq]
```

```python
import functools
import math

import jax
import jax.numpy as jnp
import numpy as np
from jax import lax
from jax.experimental import pallas as pl
from jax.experimental.pallas import tpu as pltpu

F32 = jnp.float32
BF16 = jnp.bfloat16
I32 = jnp.int32

DEPTH = 1
N_META = 16
CHUNK = 64
ATT_HEADS = 4
HEAD_DIM = 64
HEAD_W = 2 * HEAD_DIM
ATT_W = ATT_HEADS * HEAD_W
SSM_W = 512
SSM_CG = 16
SSM_G = SSM_W // SSM_CG
SSM_N = 64
N_BUCKETS = 32
MAX_DISTANCE = 128
N_EXPERTS = 32
TOP_K = 4
SWIGLU_LIMIT = 7.0
SWIGLU_ALPHA = 1.702
LN_EPS = 1e-5
NEG_INF = -1e30
DEEPNORM_ALPHA = (2.0 * DEPTH) ** 0.25
LOG2E = 1.4426950408889634

Q_TILE = 256
KV_TILE = 256
V_ONES = 16
V_ROWS = HEAD_W + V_ONES
META_PAD = 128
SSM_T = 64
ROW_TILE = 256
MOE_BLOCK = 256
VMEM_LIMIT = 56 * 1024 * 1024


def _layer_norm(x, g, b):
    mu = jnp.mean(x, axis=-1, keepdims=True)
    xc = x - mu
    var = jnp.mean(xc * xc, axis=-1, keepdims=True)
    return xc * lax.rsqrt(var + LN_EPS) * g + b


def _dot(a, b):
    return jnp.dot(a, b, preferred_element_type=F32)


def _dot_nt(a, b, precision=None):
    return lax.dot_general(a, b, (((1,), (1,)), ((), ())), precision=precision,
                           preferred_element_type=F32)


def _inproj_kernel(x_ref, g_ref, b_ref, w_ref, q_ref, k_ref, vt_ref, u_ref, *, kv_tile, q_scale):
    h = _layer_norm(x_ref[0], g_ref[...], b_ref[...]).astype(BF16)
    tm = h.shape[0]
    q_ref[0] = (_dot(h, w_ref[:, 0:ATT_W]) * q_scale).astype(BF16)
    k_ref[0] = _dot(h, w_ref[:, ATT_W:2 * ATT_W]).astype(BF16)
    v = _dot(h, w_ref[:, 2 * ATT_W:3 * ATT_W])
    ones = jnp.ones((V_ONES, kv_tile), BF16)
    for hh in range(ATT_HEADS):
        vt = v[:, hh * HEAD_W:(hh + 1) * HEAD_W].T.astype(BF16)
        for j in range(tm // kv_tile):
            vt_ref[0, hh, j, 0:HEAD_W, :] = vt[:, j * kv_tile:(j + 1) * kv_tile]
            vt_ref[0, hh, j, HEAD_W:V_ROWS, :] = ones
    u_ref[0] = _dot(h, w_ref[:, 3 * ATT_W:]).astype(BF16)


def _inproj(x, g, b, w, *, tm, kv_tile):
    bsz, s, d = x.shape
    q_scale = HEAD_DIM ** -0.5 * LOG2E
    n_cols = w.shape[1]
    row = lambda bi, i: (bi, i, 0)
    return pl.pallas_call(
        functools.partial(_inproj_kernel, kv_tile=kv_tile, q_scale=q_scale),
        grid=(bsz, s // tm),
        in_specs=[
            pl.BlockSpec((1, tm, d), row),
            pl.BlockSpec((1, d), lambda bi, i: (0, 0)),
            pl.BlockSpec((1, d), lambda bi, i: (0, 0)),
            pl.BlockSpec((d, n_cols), lambda bi, i: (0, 0)),
        ],
        out_specs=[
            pl.BlockSpec((1, tm, ATT_W), row),
            pl.BlockSpec((1, tm, ATT_W), row),
            pl.BlockSpec((1, ATT_HEADS, tm // kv_tile, V_ROWS, kv_tile), lambda bi, i: (bi, 0, i, 0, 0)),
            pl.BlockSpec((1, tm, SSM_W), row),
        ],
        out_shape=[
            jax.ShapeDtypeStruct((bsz, s, ATT_W), BF16),
            jax.ShapeDtypeStruct((bsz, s, ATT_W), BF16),
            jax.ShapeDtypeStruct((bsz, ATT_HEADS, s // kv_tile, V_ROWS, kv_tile), BF16),
            jax.ShapeDtypeStruct((bsz, s, SSM_W), BF16),
        ],
        compiler_params=pltpu.CompilerParams(
            dimension_semantics=("parallel", "parallel"), vmem_limit_bytes=VMEM_LIMIT),
        name="inproj",
    )(x, g, b, w)


def _attn_kernel(lam_ref, q_ref, k_ref, vt_ref, km_ref, vtm_ref, bias_ref, g_ref, o_ref,
                 m_ref, acc_ref, *, out_scale):
    i = pl.program_id(2)
    q = q_ref[0]
    lane = lax.broadcasted_iota(I32, q.shape, 1)
    zero = jnp.zeros_like(q)
    qz = jnp.concatenate([jnp.where(lane < HEAD_DIM, q, zero), jnp.where(lane >= HEAD_DIM, q, zero)], axis=0)

    m_ref[...] = jnp.full(m_ref.shape, NEG_INF, F32)
    acc_ref[...] = jnp.zeros(acc_ref.shape, F32)

    def scores(kt):
        return _dot_nt(kt, qz)

    def update(s, pv):
        m_old = m_ref[...]
        m_new = jnp.maximum(m_old, jnp.max(s, axis=0, keepdims=True))
        alpha = jnp.exp2(m_old - m_new)
        p = jnp.exp2(s - m_new).astype(BF16)
        acc_ref[...] = acc_ref[...] * alpha + pv(p)
        m_ref[...] = m_new

    def kv_rows(blk, n):
        return k_ref[0, pl.ds(pl.multiple_of(blk * KV_TILE, KV_TILE), n * KV_TILE), :]

    prev = jnp.maximum(i - 1, 0)
    s_near = jnp.concatenate([scores(km_ref[...]), scores(kv_rows(prev, 1)), scores(kv_rows(i, 1))], axis=0)
    s_near = s_near + bias_ref[0, 0]

    def pv_near(p):
        return (_dot(vtm_ref[0], p[0:META_PAD])
                + _dot(vt_ref[0, 0, prev], p[META_PAD:META_PAD + KV_TILE])
                + _dot(vt_ref[0, 0, i], p[META_PAD + KV_TILE:]))

    update(s_near, pv_near)

    n_far = jnp.maximum(i - 1, 0)

    def far_pair(j, carry):
        def pv(p):
            return _dot(vt_ref[0, 0, 2 * j], p[0:KV_TILE]) + _dot(vt_ref[0, 0, 2 * j + 1], p[KV_TILE:])
        update(scores(kv_rows(2 * j, 2)), pv)
        return carry

    lax.fori_loop(0, n_far // 2, far_pair, 0)

    @pl.when(n_far % 2 == 1)
    def _():
        j = n_far - 1
        update(scores(kv_rows(j, 1)), lambda p: _dot(vt_ref[0, 0, j], p))

    acc = acc_ref[...]
    lam = lam_ref[0]
    o1 = acc[0:HEAD_W, 0:Q_TILE] / acc[HEAD_W:HEAD_W + 1, 0:Q_TILE]
    o2 = acc[0:HEAD_W, Q_TILE:] / acc[HEAD_W:HEAD_W + 1, Q_TILE:]
    o = o1 - lam * o2
    ms = jnp.mean(o * o, axis=0, keepdims=True)
    o = o * lax.rsqrt(ms + LN_EPS) * g_ref[...] * out_scale
    o_ref[0] = o.T.astype(o_ref.dtype)


def _attention(lam, q, k, vt, k_meta, vt_meta, bias, subln_g, *, lam_init):
    bsz, s, _ = q.shape
    nq = s // Q_TILE
    n_near = META_PAD + 2 * KV_TILE
    grid_spec = pltpu.PrefetchScalarGridSpec(
        num_scalar_prefetch=1,
        grid=(bsz, ATT_HEADS, nq),
        in_specs=[
            pl.BlockSpec((1, Q_TILE, HEAD_W), lambda b, h, i, lam: (b, i, h)),
            pl.BlockSpec((1, s, HEAD_W), lambda b, h, i, lam: (b, 0, h)),
            pl.BlockSpec((1, 1, s // KV_TILE, V_ROWS, KV_TILE), lambda b, h, i, lam: (b, h, 0, 0, 0)),
            pl.BlockSpec((META_PAD, HEAD_W), lambda b, h, i, lam: (0, h)),
            pl.BlockSpec((1, V_ROWS, META_PAD), lambda b, h, i, lam: (h, 0, 0)),
            pl.BlockSpec((1, 1, n_near, 2 * Q_TILE), lambda b, h, i, lam: (h, jnp.minimum(i, 1), 0, 0)),
            pl.BlockSpec((HEAD_W, 1), lambda b, h, i, lam: (0, 0)),
        ],
        out_specs=pl.BlockSpec((1, Q_TILE, HEAD_W), lambda b, h, i, lam: (b, i, h)),
        scratch_shapes=[pltpu.VMEM((1, 2 * Q_TILE), F32), pltpu.VMEM((V_ROWS, 2 * Q_TILE), F32)],
    )
    return pl.pallas_call(
        functools.partial(_attn_kernel, out_scale=1.0 - lam_init),
        grid_spec=grid_spec,
        out_shape=jax.ShapeDtypeStruct((bsz, s, ATT_W), BF16),
        compiler_params=pltpu.CompilerParams(
            dimension_semantics=("parallel", "parallel", "arbitrary"), vmem_limit_bytes=VMEM_LIMIT),
        name="diff_attention",
    )(lam, q, k, vt, k_meta, vt_meta, bias, subln_g)


def _t5_bucket(rel):
    nb = N_BUCKETS // 2
    max_exact = nb // 2
    ret = jnp.where(rel > 0, nb, 0)
    n = jnp.abs(rel)
    n_f = jnp.maximum(n, 1).astype(F32)
    large = max_exact + (jnp.log(n_f / max_exact) / math.log(MAX_DISTANCE / max_exact)
                         * (nb - max_exact)).astype(I32)
    large = jnp.minimum(large, nb - 1)
    return ret + jnp.where(n < max_exact, n, large)


def _near_bias(rel_bias):
    table = rel_bias.astype(F32)
    far = table[N_BUCKETS // 2 - 1]
    c = jnp.arange(Q_TILE, dtype=I32)[None, :]
    r = jnp.arange(KV_TILE, dtype=I32)[:, None]

    def bias_of(rel):
        return jnp.transpose(table[_t5_bucket(rel)] - far, (2, 0, 1)) * LOG2E

    own = jnp.where((r // CHUNK <= c // CHUNK)[None], bias_of(r - c), NEG_INF)
    prev = bias_of(r - c - KV_TILE)
    rm = jnp.arange(META_PAD, dtype=I32)[:, None]
    meta_ok = (rm < N_META)[None]
    meta0 = jnp.where(meta_ok, bias_of(rm - (N_META + c)), NEG_INF)
    meta1 = jnp.where(meta_ok, bias_of(rm - (N_META + Q_TILE + c)), NEG_INF)
    v0 = jnp.concatenate([meta0, jnp.full_like(prev, NEG_INF), own], axis=1)
    v1 = jnp.concatenate([meta1, prev, own], axis=1)
    both = jnp.stack([v0, v1], axis=1)
    return jnp.concatenate([both, both], axis=-1)


def _ssm_tables(a_re, a_im, log_step, b_re, b_im, c_re, c_im, d_skip):
    hi = lax.Precision.HIGHEST
    t_len = SSM_T
    step = jnp.exp(log_step.astype(F32))[:, None]
    ar = jnp.minimum(a_re.astype(F32), -1e-4)
    ai = a_im.astype(F32)
    mag = jnp.exp(step * ar)
    ph = step * ai
    abar_re = mag * jnp.cos(ph)
    abar_im = mag * jnp.sin(ph)
    den = ar * ar + ai * ai
    e_re = abar_re - 1.0
    e_im = abar_im
    f_re = (e_re * ar + e_im * ai) / den
    f_im = (e_im * ar - e_re * ai) / den
    br = b_re.astype(F32)
    bi = b_im.astype(F32)
    bb_re = f_re[..., None] * br - f_im[..., None] * bi
    bb_im = f_re[..., None] * bi + f_im[..., None] * br
    tau = jnp.arange(t_len + 1, dtype=F32)[None, :, None]
    pmag = jnp.exp(tau * (step * ar)[:, None, :])
    pph = tau * ph[:, None, :]
    pw_re = pmag * jnp.cos(pph)
    pw_im = pmag * jnp.sin(pph)
    w_re = pw_re[..., None] * bb_re[:, None] - pw_im[..., None] * bb_im[:, None]
    w_im = pw_re[..., None] * bb_im[:, None] + pw_im[..., None] * bb_re[:, None]
    cr = c_re.astype(F32)
    ci = c_im.astype(F32)
    kern = (jnp.einsum('gcn,gtnk->gtck', cr, w_re[:, :t_len], precision=hi)
            - jnp.einsum('gcn,gtnk->gtck', ci, w_im[:, :t_len], precision=hi))
    skip = d_skip.astype(F32).reshape(SSM_G, SSM_CG)
    kern = kern.at[:, 0].add(skip[:, :, None] * jnp.eye(SSM_CG, dtype=F32))
    lag = jnp.arange(t_len)[None, :] - jnp.arange(t_len)[:, None]
    toe = jnp.where((lag >= 0)[None, :, :, None, None], kern[:, jnp.maximum(lag, 0)], 0.0)
    mt = jnp.transpose(toe, (0, 1, 4, 2, 3)).reshape(SSM_G, t_len * SSM_CG, t_len * SSM_CG)
    rev = t_len - 1 - jnp.arange(t_len)
    pt_re = jnp.transpose(w_re[:, rev], (0, 1, 3, 2)).reshape(SSM_G, t_len * SSM_CG, SSM_N)
    pt_im = jnp.transpose(w_im[:, rev], (0, 1, 3, 2)).reshape(SSM_G, t_len * SSM_CG, SSM_N)
    pt = jnp.concatenate([pt_re, pt_im], axis=-1)
    up_re = pw_re[:, 1:]
    up_im = pw_im[:, 1:]
    q_re = cr[:, None] * up_re[:, :, None, :] - ci[:, None] * up_im[:, :, None, :]
    q_im = cr[:, None] * up_im[:, :, None, :] + ci[:, None] * up_re[:, :, None, :]
    qt = jnp.concatenate([jnp.transpose(q_re, (0, 3, 1, 2)), -jnp.transpose(q_im, (0, 3, 1, 2))], axis=1)
    qt = qt.reshape(SSM_G, 2 * SSM_N, t_len * SSM_CG)
    at_re = pw_re[:, t_len]
    at_im = pw_im[:, t_len]
    a1 = jnp.concatenate([at_re, at_re], axis=-1)
    a2 = jnp.concatenate([-at_im, at_im], axis=-1)
    return mt.astype(BF16), pt.astype(BF16), qt.astype(BF16), a1, a2


def _ssm_state_kernel(u_ref, pt_ref, s_ref):
    s_ref[0] = _dot(u_ref[0], pt_ref[0])


def _ssm_scan_kernel(s_ref, a1_ref, a2_ref, x_ref, *, n_chunks, bsz):
    a1 = a1_ref[...][:, None, :]
    a2 = a2_ref[...][:, None, :]

    def body(c, x):
        rows = pl.ds(c * bsz, bsz)
        x_ref[:, rows, :] = x
        return a1 * x + a2 * pltpu.roll(x, SSM_N, 2) + s_ref[:, rows, :]

    lax.fori_loop(0, n_chunks, body, jnp.zeros((SSM_G, bsz, 2 * SSM_N), F32))


def _ssm_out_kernel(u_ref, x_ref, mt_ref, qt_ref, y_ref):
    x = x_ref[0]
    x_hi = x.astype(BF16)
    x_lo = (x - x_hi.astype(F32)).astype(BF16)
    y = _dot(u_ref[0], mt_ref[0]) + _dot(x_hi, qt_ref[0]) + _dot(x_lo, qt_ref[0])
    y_ref[0] = y.astype(y_ref.dtype)


def _ssm(ug, mt, pt, qt, a1, a2, *, n_chunks, bsz):
    g, r, w = ug.shape
    n2 = 2 * SSM_N
    per_g = lambda gi: (gi, 0, 0)
    params = pltpu.CompilerParams(dimension_semantics=("parallel",), vmem_limit_bytes=VMEM_LIMIT)
    s = pl.pallas_call(
        _ssm_state_kernel,
        grid=(g,),
        in_specs=[pl.BlockSpec((1, r, w), per_g), pl.BlockSpec((1, w, n2), per_g)],
        out_specs=pl.BlockSpec((1, r, n2), per_g),
        out_shape=jax.ShapeDtypeStruct((g, r, n2), F32),
        compiler_params=params,
        name="ssm_chunk_state",
    )(ug, pt)
    x = pl.pallas_call(
        functools.partial(_ssm_scan_kernel, n_chunks=n_chunks, bsz=bsz),
        out_shape=jax.ShapeDtypeStruct((g, r, n2), F32),
        compiler_params=pltpu.CompilerParams(vmem_limit_bytes=VMEM_LIMIT),
        name="ssm_chunk_scan",
    )(s, a1, a2)
    return pl.pallas_call(
        _ssm_out_kernel,
        grid=(g,),
        in_specs=[pl.BlockSpec((1, r, w), per_g), pl.BlockSpec((1, r, n2), per_g),
                  pl.BlockSpec((1, w, w), per_g), pl.BlockSpec((1, n2, w), per_g)],
        out_specs=pl.BlockSpec((1, r, w), per_g),
        out_shape=jax.ShapeDtypeStruct((g, r, w), BF16),
        compiler_params=params,
        name="ssm_output",
    )(ug, x, mt, qt)


def _mix_kernel(x_ref, gi_ref, bi_ref, att_ref, y_ref, wglu_ref, bglu_ref, wout_ref, g1_ref, b1_ref,
                wr_ref, br_ref, h1_ref, idx_ref, gate_ref, rank_ref, cnt_ref, carry_ref):
    step = pl.program_id(0)

    @pl.when(step == 0)
    def _():
        carry_ref[...] = jnp.zeros(carry_ref.shape, F32)

    h0 = _layer_norm(x_ref[...], gi_ref[...], bi_ref[...])
    y = y_ref[...].astype(F32)
    y = y * (0.5 * (1.0 + jnp.tanh(math.sqrt(2.0 / math.pi) * (y + 0.044715 * (y * y * y)))))
    y = y * jax.nn.sigmoid(_dot(y.astype(BF16), wglu_ref[...]) + bglu_ref[...])
    mix = _dot(att_ref[...], wout_ref[0:ATT_W, :]) + _dot(y.astype(BF16), wout_ref[ATT_W:, :])
    h1 = _layer_norm(DEEPNORM_ALPHA * h0 + mix, g1_ref[...], b1_ref[...])
    h1_ref[...] = h1

    logits = _dot_nt(wr_ref[...], h1, precision=lax.Precision.HIGHEST) + br_ref[...]
    tm = logits.shape[1]
    eidx = lax.broadcasted_iota(I32, logits.shape, 0)
    vals, hots = [], []
    rest = logits
    for _ in range(TOP_K):
        mx = jnp.max(rest, axis=0, keepdims=True)
        first = jnp.min(jnp.where(rest == mx, eidx, N_EXPERTS), axis=0, keepdims=True)
        hot = eidx == first
        vals.append(mx)
        hots.append(hot)
        rest = jnp.where(hot, -jnp.inf, rest)
    exps = [jnp.exp(v - vals[0]) for v in vals]
    denom = exps[0] + exps[1] + exps[2] + exps[3]
    gate_ref[...] = jnp.concatenate([e / denom for e in exps], axis=0)
    idx_ref[...] = jnp.concatenate(
        [jnp.sum(jnp.where(h, eidx, 0), axis=0, keepdims=True) for h in hots], axis=0)

    hot_all = (hots[0] | hots[1] | hots[2] | hots[3]).astype(F32)
    tri = (lax.broadcasted_iota(I32, (tm, tm), 0) < lax.broadcasted_iota(I32, (tm, tm), 1)).astype(BF16)
    before = _dot(hot_all.astype(BF16), tri) + carry_ref[...]
    rank_ref[...] = jnp.concatenate(
        [jnp.sum(jnp.where(h, before, 0.0), axis=0, keepdims=True) for h in hots], axis=0).astype(I32)
    carry_ref[...] = carry_ref[...] + jnp.sum(hot_all, axis=1, keepdims=True)
    cnt_ref[...] = jnp.broadcast_to(carry_ref[...], cnt_ref.shape)


def _mix(x2, gi, bi, att, y, wglu, bglu, wout, g1, b1, wr_t, br):
    t, d = x2.shape
    tm = ROW_TILE
    row = lambda i: (i, 0)
    col = lambda i: (0, i)
    const = lambda i: (0, 0)
    return pl.pallas_call(
        _mix_kernel,
        grid=(t // tm,),
        in_specs=[
            pl.BlockSpec((tm, d), row), pl.BlockSpec((1, d), const), pl.BlockSpec((1, d), const),
            pl.BlockSpec((tm, ATT_W), row), pl.BlockSpec((tm, SSM_W), row),
            pl.BlockSpec((SSM_W, SSM_W), const), pl.BlockSpec((1, SSM_W), const),
            pl.BlockSpec((d, d), const), pl.BlockSpec((1, d), const), pl.BlockSpec((1, d), const),
            pl.BlockSpec((N_EXPERTS, d), const), pl.BlockSpec((N_EXPERTS, 1), const),
        ],
        out_specs=[
            pl.BlockSpec((tm, d), row),
            pl.BlockSpec((TOP_K, tm), col), pl.BlockSpec((TOP_K, tm), col), pl.BlockSpec((TOP_K, tm), col),
            pl.BlockSpec((N_EXPERTS, 128), const),
        ],
        out_shape=[
            jax.ShapeDtypeStruct((t, d), F32),
            jax.ShapeDtypeStruct((TOP_K, t), I32), jax.ShapeDtypeStruct((TOP_K, t), F32),
            jax.ShapeDtypeStruct((TOP_K, t), I32),
            jax.ShapeDtypeStruct((N_EXPERTS, 128), F32),
        ],
        scratch_shapes=[pltpu.VMEM((N_EXPERTS, 1), F32)],
        compiler_params=pltpu.CompilerParams(dimension_semantics=("arbitrary",), vmem_limit_bytes=VMEM_LIMIT),
        name="mix_ln1_router",
    )(x2, gi, bi, att, y, wglu, bglu, wout, g1, b1, wr_t, br)


def _row_copy(src, src_row, dst, dst_row, sem):
    return pltpu.make_async_copy(src.at[pl.ds(src_row, 1), :], dst.at[pl.ds(dst_row, 1), :], sem)


def _dispatch_kernel(pad_start_ref, pad_count_ref, n_used_ref, dest_ref, h1_hbm, xs_hbm, zero_ref, sems, zsem):
    step = pl.program_id(0)
    n_steps = pl.num_programs(0)
    tm = dest_ref.shape[1]
    slot = step % 2
    n_blocks = xs_hbm.shape[0] // MOE_BLOCK

    def zero_block(blk):
        return pltpu.make_async_copy(zero_ref, xs_hbm.at[pl.ds(blk * MOE_BLOCK, MOE_BLOCK), :], zsem)

    def wait_batch(s):
        for _ in range(TOP_K):
            pltpu.make_async_copy(h1_hbm.at[pl.ds(0, tm), :], xs_hbm.at[pl.ds(0, tm), :], sems.at[s]).wait()

    @pl.when(step == 0)
    def _():
        zero_ref[...] = jnp.zeros(zero_ref.shape, F32)
        for e in range(N_EXPERTS):
            start = pad_start_ref[e]
            count = pad_count_ref[e]

            def fill(r, c):
                _row_copy(zero_ref, 0, xs_hbm, start + r, zsem).start()
                return c

            def drain(r, c):
                _row_copy(zero_ref, 0, xs_hbm, start, zsem).wait()
                return c

            lax.fori_loop(0, count, fill, 0)
            lax.fori_loop(0, count, drain, 0)

        def fill_block(blk, c):
            zero_block(blk).start()
            return c

        def drain_block(blk, c):
            zero_block(blk).wait()
            return c

        lax.fori_loop(n_used_ref[0], n_blocks, fill_block, 0)
        lax.fori_loop(n_used_ref[0], n_blocks, drain_block, 0)

    def issue(r, c):
        for k in range(TOP_K):
            _row_copy(h1_hbm, step * tm + r, xs_hbm, dest_ref[k, r], sems.at[slot]).start()
        return c

    lax.fori_loop(0, tm, issue, 0)

    @pl.when(step > 0)
    def _():
        wait_batch(1 - slot)

    @pl.when(step == n_steps - 1)
    def _():
        wait_batch(slot)


def _dispatch(pad_start, pad_count, n_used, dest_t, h1, n_rows):
    t, d = h1.shape
    tm = ROW_TILE
    grid_spec = pltpu.PrefetchScalarGridSpec(
        num_scalar_prefetch=3,
        grid=(t // tm,),
        in_specs=[
            pl.BlockSpec((TOP_K, tm), lambda i, ps, pc, nu: (0, i), memory_space=pltpu.SMEM),
            pl.BlockSpec(memory_space=pl.ANY),
        ],
        out_specs=pl.BlockSpec(memory_space=pl.ANY),
        scratch_shapes=[pltpu.VMEM((MOE_BLOCK, d), F32), pltpu.SemaphoreType.DMA((2,)), pltpu.SemaphoreType.DMA],
    )
    return pl.pallas_call(
        _dispatch_kernel,
        grid_spec=grid_spec,
        out_shape=jax.ShapeDtypeStruct((n_rows, d), F32),
        compiler_params=pltpu.CompilerParams(dimension_semantics=("arbitrary",), has_side_effects=True),
        name="moe_dispatch",
    )(pad_start, pad_count, n_used, dest_t, h1)


def _expert_kernel(be_ref, nb_ref, x_ref, wg_ref, bg_ref, wu_ref, bu_ref, wd_ref, bd_ref, y_ref,
                   wg_b, wu_b, wd_b):
    i = pl.program_id(0)

    @pl.when(i < nb_ref[0])
    def _():
        @pl.when((i == 0) | (be_ref[i] != be_ref[jnp.maximum(i - 1, 0)]))
        def _():
            wg_b[...] = wg_ref[0].astype(BF16)
            wu_b[...] = wu_ref[0].astype(BF16)
            wd_b[...] = wd_ref[0].astype(BF16)

        x = x_ref[...].astype(BF16)
        gate = _dot(x, wg_b[...]) + bg_ref[0]
        up = _dot(x, wu_b[...]) + bu_ref[0]
        gate = jnp.minimum(gate, SWIGLU_LIMIT)
        up = jnp.clip(up, -SWIGLU_LIMIT, SWIGLU_LIMIT)
        act = (up + 1.0) * gate * jax.nn.sigmoid(gate * SWIGLU_ALPHA)
        y_ref[...] = _dot(act.astype(BF16), wd_b[...]) + bd_ref[0]

    @pl.when(i >= nb_ref[0])
    def _():
        y_ref[...] = jnp.zeros(y_ref.shape, F32)


def _experts(block_expert, n_used, xs, wg, bg, wu, bu, wd, bd):
    n_rows, d = xs.shape
    dff = wg.shape[2]
    nb = n_rows // MOE_BLOCK
    blk = lambda i, be, nu: (jnp.minimum(i, nu[0] - 1), 0)
    wsel = lambda i, be, nu: (be[jnp.minimum(i, nu[0] - 1)], 0, 0)
    grid_spec = pltpu.PrefetchScalarGridSpec(
        num_scalar_prefetch=2,
        grid=(nb,),
        in_specs=[
            pl.BlockSpec((MOE_BLOCK, d), blk),
            pl.BlockSpec((1, d, dff), wsel), pl.BlockSpec((1, 1, dff), wsel),
            pl.BlockSpec((1, d, dff), wsel), pl.BlockSpec((1, 1, dff), wsel),
            pl.BlockSpec((1, dff, d), wsel), pl.BlockSpec((1, 1, d), wsel),
        ],
        out_specs=pl.BlockSpec((MOE_BLOCK, d), lambda i, be, nu: (i, 0)),
        scratch_shapes=[pltpu.VMEM((d, dff), BF16), pltpu.VMEM((d, dff), BF16), pltpu.VMEM((dff, d), BF16)],
    )
    return pl.pallas_call(
        _expert_kernel,
        grid_spec=grid_spec,
        out_shape=jax.ShapeDtypeStruct((n_rows, d), F32),
        compiler_params=pltpu.CompilerParams(dimension_semantics=("arbitrary",), vmem_limit_bytes=VMEM_LIMIT),
        name="moe_experts",
    )(block_expert, n_used, xs, wg, bg, wu, bu, wd, bd)


def _combine_kernel(dest_ref, dest_next_ref, h1_ref, gate_ref, ys_hbm, g2_ref, b2_ref, o_ref, buf, sems):
    step = pl.program_id(0)
    n_steps = pl.num_programs(0)
    tm = h1_ref.shape[0]
    slot = step % 2

    def issue(dref, s):
        def body(r, c):
            for k in range(TOP_K):
                pltpu.make_async_copy(ys_hbm.at[pl.ds(dref[k, r], 1), :], buf.at[s, k, pl.ds(r, 1), :],
                                      sems.at[s]).start()
            return c
        lax.fori_loop(0, tm, body, 0)

    @pl.when(step == 0)
    def _():
        issue(dest_ref, 0)

    @pl.when(step + 1 < n_steps)
    def _():
        issue(dest_next_ref, 1 - slot)

    for k in range(TOP_K):
        pltpu.make_async_copy(ys_hbm.at[pl.ds(0, tm), :], buf.at[slot, k], sems.at[slot]).wait()

    gate = gate_ref[...]
    ffn = buf[slot, 0] * gate[:, 0:1]
    for k in range(1, TOP_K):
        ffn = ffn + buf[slot, k] * gate[:, k:k + 1]
    o_ref[...] = _layer_norm(DEEPNORM_ALPHA * h1_ref[...] + ffn, g2_ref[...], b2_ref[...])


def _combine(dest_t, h1, gates, ys, g2, b2):
    t, d = h1.shape
    tm = ROW_TILE
    n_steps = t // tm
    row = lambda i: (i, 0)
    const = lambda i: (0, 0)
    return pl.pallas_call(
        _combine_kernel,
        grid=(n_steps,),
        in_specs=[
            pl.BlockSpec((TOP_K, tm), lambda i: (0, i), memory_space=pltpu.SMEM),
            pl.BlockSpec((TOP_K, tm), lambda i: (0, jnp.minimum(i + 1, n_steps - 1)), memory_space=pltpu.SMEM),
            pl.BlockSpec((tm, d), row), pl.BlockSpec((tm, TOP_K), row),
            pl.BlockSpec(memory_space=pl.ANY),
            pl.BlockSpec((1, d), const), pl.BlockSpec((1, d), const),
        ],
        out_specs=pl.BlockSpec((tm, d), row),
        out_shape=jax.ShapeDtypeStruct((t, d), F32),
        scratch_shapes=[pltpu.VMEM((2, TOP_K, tm, d), F32), pltpu.SemaphoreType.DMA((2,))],
        compiler_params=pltpu.CompilerParams(dimension_semantics=("arbitrary",), vmem_limit_bytes=VMEM_LIMIT),
        name="moe_combine_ln2",
    )(dest_t, dest_t, h1, gates, ys, g2, b2)


def kernel(x, meta_tokens, ln_in_g, ln_in_b, rel_bias, w_in, lambda_q1, lambda_k1, lambda_q2, lambda_k2,
           subln_g, a_re, a_im, log_step, b_re, b_im, c_re, c_im, d_skip, w_glu, b_glu, w_out, ln1_g, ln1_b,
           w_router, b_router, w_gate, b_gate, w_up, b_up, w_down, b_down, ln2_g, ln2_b):
    bsz, seq, dm = x.shape
    assert seq % 512 == 0 and w_in.shape[0] == DEPTH == 1
    layer = 0
    row2 = lambda v: v.astype(F32).reshape(1, -1)

    w_in_b = w_in[layer].astype(BF16)
    gi, bi = row2(ln_in_g), row2(ln_in_b)
    q, k, vt, u = _inproj(x, gi, bi, w_in_b, tm=512, kv_tile=KV_TILE)
    meta = jnp.zeros((1, META_PAD, dm), x.dtype).at[0, :N_META].set(meta_tokens.astype(x.dtype))
    _, k_meta, vt_meta, u_meta = _inproj(meta, gi, bi, w_in_b, tm=META_PAD, kv_tile=META_PAD)

    lam_init = 0.8 - 0.6 * math.exp(-0.3 * layer)
    lam = (jnp.exp(jnp.sum(lambda_q1[layer].astype(F32) * lambda_k1[layer].astype(F32)))
           - jnp.exp(jnp.sum(lambda_q2[layer].astype(F32) * lambda_k2[layer].astype(F32))) + lam_init)
    att = _attention(lam.reshape(1), q, k, vt, k_meta[0], vt_meta[0, :, 0], _near_bias(rel_bias),
                     subln_g[layer].astype(F32).reshape(HEAD_W, 1), lam_init=lam_init)

    mt, pt, qt, a1, a2 = _ssm_tables(a_re[layer], a_im[layer], log_step[layer], b_re[layer], b_im[layer],
                                     c_re[layer], c_im[layer], d_skip[layer])
    n_chunks = -(-(seq // SSM_T + 1) // 8) * 8
    lead = jnp.zeros((bsz, SSM_T, SSM_W), BF16).at[:, SSM_T - N_META:].set(u_meta[:, :N_META])
    tail = jnp.zeros((bsz, n_chunks * SSM_T - SSM_T - seq, SSM_W), BF16)
    ug = jnp.concatenate([lead, u, tail], axis=1).reshape(bsz, n_chunks, SSM_T, SSM_G, SSM_CG)
    ug = jnp.transpose(ug, (3, 1, 0, 2, 4)).reshape(SSM_G, n_chunks * bsz, SSM_T * SSM_CG)
    yg = _ssm(ug, mt, pt, qt, a1, a2, n_chunks=n_chunks, bsz=bsz)
    yg = jnp.transpose(yg.reshape(SSM_G, n_chunks, bsz, SSM_T, SSM_CG), (2, 1, 3, 0, 4))
    y_ssm = yg.reshape(bsz, n_chunks * SSM_T, SSM_W)[:, SSM_T:SSM_T + seq]

    t = bsz * seq
    h1, idx_t, gate_t, rank_t, cnt = _mix(
        x.reshape(t, dm), gi, bi, att.reshape(t, ATT_W), y_ssm.reshape(t, SSM_W),
        w_glu[layer].astype(BF16), row2(b_glu[layer]), w_out[layer].astype(BF16),
        row2(ln1_g[layer]), row2(ln1_b[layer]),
        jnp.transpose(w_router[layer].astype(F32)), b_router[layer].astype(F32).reshape(N_EXPERTS, 1))

    counts = cnt[:, 0].astype(I32)
    padded = (counts + MOE_BLOCK - 1) // MOE_BLOCK * MOE_BLOCK
    padded_end = jnp.cumsum(padded)
    padded_start = padded_end - padded
    dest_t = padded_start[idx_t] + rank_t
    n_blocks = t * TOP_K // MOE_BLOCK + N_EXPERTS
    block_expert = jnp.minimum(
        jnp.searchsorted(padded_end, jnp.arange(n_blocks, dtype=I32) * MOE_BLOCK, side='right'),
        N_EXPERTS - 1).astype(I32)
    n_used = (padded_end[-1:] // MOE_BLOCK).astype(I32)

    xs = _dispatch((padded_start + counts).astype(I32), (padded - counts).astype(I32), n_used, dest_t, h1,
                   n_blocks * MOE_BLOCK)
    b3 = lambda v: v.astype(F32)[:, None, :]
    ys = _experts(block_expert, n_used, xs, w_gate[layer], b3(b_gate[layer]), w_up[layer], b3(b_up[layer]),
                  w_down[layer], b3(b_down[layer]))
    out = _combine(dest_t, h1, jnp.transpose(gate_t), ys, row2(ln2_g[layer]), row2(ln2_b[layer]))
    return out.reshape(bsz, seq, dm)
```

```python
import functools
import math

import jax
import jax.numpy as jnp
import numpy as np
from jax import lax
from jax.experimental import pallas as pl
from jax.experimental.pallas import tpu as pltpu

F32 = jnp.float32
BF16 = jnp.bfloat16
I32 = jnp.int32

DEPTH = 1
N_META = 16
CHUNK = 64
ATT_HEADS = 4
HEAD_DIM = 64
HEAD_W = 2 * HEAD_DIM
ATT_W = ATT_HEADS * HEAD_W
SSM_W = 512
SSM_CG = 16
SSM_G = SSM_W // SSM_CG
SSM_N = 64
N_BUCKETS = 32
MAX_DISTANCE = 128
N_EXPERTS = 32
TOP_K = 4
SWIGLU_LIMIT = 7.0
SWIGLU_ALPHA = 1.702
LN_EPS = 1e-5
NEG_INF = -1e30
DEEPNORM_ALPHA = (2.0 * DEPTH) ** 0.25
LOG2E = 1.4426950408889634

Q_TILE = 256
KV_TILE = 256
V_ONES = 16
V_ROWS = HEAD_W + V_ONES
META_PAD = 128
SSM_T = 64
ROW_TILE = 256
MOE_BLOCK = 256
VMEM_LIMIT = 56 * 1024 * 1024


def _layer_norm(x, g, b):
    mu = jnp.mean(x, axis=-1, keepdims=True)
    xc = x - mu
    var = jnp.mean(xc * xc, axis=-1, keepdims=True)
    return xc * lax.rsqrt(var + LN_EPS) * g + b


def _dot(a, b):
    return jnp.dot(a, b, preferred_element_type=F32)


def _dot_nt(a, b, precision=None):
    return lax.dot_general(a, b, (((1,), (1,)), ((), ())), precision=precision,
                           preferred_element_type=F32)


def _inproj_kernel(x_ref, g_ref, b_ref, w_ref, q_ref, k_ref, vt_ref, u_ref, *, kv_tile, q_scale):
    h = _layer_norm(x_ref[0], g_ref[...], b_ref[...]).astype(BF16)
    tm = h.shape[0]
    q_ref[0] = (_dot(h, w_ref[:, 0:ATT_W]) * q_scale).astype(BF16)
    k_ref[0] = _dot(h, w_ref[:, ATT_W:2 * ATT_W]).astype(BF16)
    v = _dot(h, w_ref[:, 2 * ATT_W:3 * ATT_W])
    ones = jnp.ones((V_ONES, kv_tile), BF16)
    for hh in range(ATT_HEADS):
        vt = v[:, hh * HEAD_W:(hh + 1) * HEAD_W].T.astype(BF16)
        for j in range(tm // kv_tile):
            vt_ref[0, hh, j, 0:HEAD_W, :] = vt[:, j * kv_tile:(j + 1) * kv_tile]
            vt_ref[0, hh, j, HEAD_W:V_ROWS, :] = ones
    u_ref[0] = _dot(h, w_ref[:, 3 * ATT_W:]).astype(BF16)


def _inproj(x, g, b, w, *, tm, kv_tile):
    bsz, s, d = x.shape
    q_scale = HEAD_DIM ** -0.5 * LOG2E
    n_cols = w.shape[1]
    row = lambda bi, i: (bi, i, 0)
    return pl.pallas_call(
        functools.partial(_inproj_kernel, kv_tile=kv_tile, q_scale=q_scale),
        grid=(bsz, s // tm),
        in_specs=[
            pl.BlockSpec((1, tm, d), row),
            pl.BlockSpec((1, d), lambda bi, i: (0, 0)),
            pl.BlockSpec((1, d), lambda bi, i: (0, 0)),
            pl.BlockSpec((d, n_cols), lambda bi, i: (0, 0)),
        ],
        out_specs=[
            pl.BlockSpec((1, tm, ATT_W), row),
            pl.BlockSpec((1, tm, ATT_W), row),
            pl.BlockSpec((1, ATT_HEADS, tm // kv_tile, V_ROWS, kv_tile), lambda bi, i: (bi, 0, i, 0, 0)),
            pl.BlockSpec((1, tm, SSM_W), row),
        ],
        out_shape=[
            jax.ShapeDtypeStruct((bsz, s, ATT_W), BF16),
            jax.ShapeDtypeStruct((bsz, s, ATT_W), BF16),
            jax.ShapeDtypeStruct((bsz, ATT_HEADS, s // kv_tile, V_ROWS, kv_tile), BF16),
            jax.ShapeDtypeStruct((bsz, s, SSM_W), BF16),
        ],
        compiler_params=pltpu.CompilerParams(
            dimension_semantics=("parallel", "parallel"), vmem_limit_bytes=VMEM_LIMIT),
        name="inproj",
    )(x, g, b, w)


def _attn_kernel(lam_ref, q_ref, k_ref, vt_ref, km_ref, vtm_ref, bias_ref, g_ref, o_ref,
                 m_ref, acc_ref, *, out_scale):
    i = pl.program_id(2)
    q = q_ref[0]
    lane = lax.broadcasted_iota(I32, q.shape, 1)
    zero = jnp.zeros_like(q)
    qz = jnp.concatenate([jnp.where(lane < HEAD_DIM, q, zero), jnp.where(lane >= HEAD_DIM, q, zero)], axis=0)

    m_ref[...] = jnp.full(m_ref.shape, NEG_INF, F32)
    acc_ref[...] = jnp.zeros(acc_ref.shape, F32)

    def scores(kt):
        return _dot_nt(kt, qz)

    def update(s, pv):
        m_old = m_ref[...]
        m_new = jnp.maximum(m_old, jnp.max(s, axis=0, keepdims=True))
        alpha = jnp.exp2(m_old - m_new)
        p = jnp.exp2(s - m_new).astype(BF16)
        acc_ref[...] = acc_ref[...] * alpha + pv(p)
        m_ref[...] = m_new

    def kv_rows(blk, n):
        return k_ref[0, pl.ds(pl.multiple_of(blk * KV_TILE, KV_TILE), n * KV_TILE), :]

    prev = jnp.maximum(i - 1, 0)
    s_near = jnp.concatenate([scores(km_ref[...]), scores(kv_rows(prev, 1)), scores(kv_rows(i, 1))], axis=0)
    s_near = s_near + bias_ref[0, 0]

    def pv_near(p):
        return (_dot(vtm_ref[0], p[0:META_PAD])
                + _dot(vt_ref[0, 0, prev], p[META_PAD:META_PAD + KV_TILE])
                + _dot(vt_ref[0, 0, i], p[META_PAD + KV_TILE:]))

    update(s_near, pv_near)

    n_far = jnp.maximum(i - 1, 0)

    def far_pair(j, carry):
        def pv(p):
            return _dot(vt_ref[0, 0, 2 * j], p[0:KV_TILE]) + _dot(vt_ref[0, 0, 2 * j + 1], p[KV_TILE:])
        update(scores(kv_rows(2 * j, 2)), pv)
        return carry

    lax.fori_loop(0, n_far // 2, far_pair, 0)

    @pl.when(n_far % 2 == 1)
    def _():
        j = n_far - 1
        update(scores(kv_rows(j, 1)), lambda p: _dot(vt_ref[0, 0, j], p))

    acc = acc_ref[...]
    lam = lam_ref[0]
    o1 = acc[0:HEAD_W, 0:Q_TILE] / acc[HEAD_W:HEAD_W + 1, 0:Q_TILE]
    o2 = acc[0:HEAD_W, Q_TILE:] / acc[HEAD_W:HEAD_W + 1, Q_TILE:]
    o = o1 - lam * o2
    ms = jnp.mean(o * o, axis=0, keepdims=True)
    o = o * lax.rsqrt(ms + LN_EPS) * g_ref[...] * out_scale
    o_ref[0] = o.T.astype(o_ref.dtype)


def _attention(lam, q, k, vt, k_meta, vt_meta, bias, subln_g, *, lam_init):
    bsz, s, _ = q.shape
    nq = s // Q_TILE
    n_near = META_PAD + 2 * KV_TILE
    grid_spec = pltpu.PrefetchScalarGridSpec(
        num_scalar_prefetch=1,
        grid=(bsz, ATT_HEADS, nq),
        in_specs=[
            pl.BlockSpec((1, Q_TILE, HEAD_W), lambda b, h, i, lam: (b, i, h)),
            pl.BlockSpec((1, s, HEAD_W), lambda b, h, i, lam: (b, 0, h)),
            pl.BlockSpec((1, 1, s // KV_TILE, V_ROWS, KV_TILE), lambda b, h, i, lam: (b, h, 0, 0, 0)),
            pl.BlockSpec((META_PAD, HEAD_W), lambda b, h, i, lam: (0, h)),
            pl.BlockSpec((1, V_ROWS, META_PAD), lambda b, h, i, lam: (h, 0, 0)),
            pl.BlockSpec((1, 1, n_near, 2 * Q_TILE), lambda b, h, i, lam: (h, jnp.minimum(i, 1), 0, 0)),
            pl.BlockSpec((HEAD_W, 1), lambda b, h, i, lam: (0, 0)),
        ],
        out_specs=pl.BlockSpec((1, Q_TILE, HEAD_W), lambda b, h, i, lam: (b, i, h)),
        scratch_shapes=[pltpu.VMEM((1, 2 * Q_TILE), F32), pltpu.VMEM((V_ROWS, 2 * Q_TILE), F32)],
    )
    return pl.pallas_call(
        functools.partial(_attn_kernel, out_scale=1.0 - lam_init),
        grid_spec=grid_spec,
        out_shape=jax.ShapeDtypeStruct((bsz, s, ATT_W), BF16),
        compiler_params=pltpu.CompilerParams(
            dimension_semantics=("parallel", "parallel", "arbitrary"), vmem_limit_bytes=VMEM_LIMIT),
        name="diff_attention",
    )(lam, q, k, vt, k_meta, vt_meta, bias, subln_g)


def _t5_bucket(rel):
    nb = N_BUCKETS // 2
    max_exact = nb // 2
    ret = jnp.where(rel > 0, nb, 0)
    n = jnp.abs(rel)
    n_f = jnp.maximum(n, 1).astype(F32)
    large = max_exact + (jnp.log(n_f / max_exact) / math.log(MAX_DISTANCE / max_exact)
                         * (nb - max_exact)).astype(I32)
    large = jnp.minimum(large, nb - 1)
    return ret + jnp.where(n < max_exact, n, large)


def _near_bias(rel_bias):
    table = rel_bias.astype(F32)
    far = table[N_BUCKETS // 2 - 1]
    c = jnp.arange(Q_TILE, dtype=I32)[None, :]
    r = jnp.arange(KV_TILE, dtype=I32)[:, None]

    def bias_of(rel):
        hot = _t5_bucket(rel)[None, :, :, None] == jnp.arange(N_BUCKETS, dtype=I32)
        looked_up = jnp.sum(jnp.where(hot, jnp.transpose(table - far)[:, None, None, :], 0.0), axis=-1)
        return looked_up * LOG2E

    own = jnp.where((r // CHUNK <= c // CHUNK)[None], bias_of(r - c), NEG_INF)
    prev = bias_of(r - c - KV_TILE)
    rm = jnp.arange(META_PAD, dtype=I32)[:, None]
    meta_ok = (rm < N_META)[None]
    meta0 = jnp.where(meta_ok, bias_of(rm - (N_META + c)), NEG_INF)
    meta1 = jnp.where(meta_ok, bias_of(rm - (N_META + Q_TILE + c)), NEG_INF)
    v0 = jnp.concatenate([meta0, jnp.full_like(prev, NEG_INF), own], axis=1)
    v1 = jnp.concatenate([meta1, prev, own], axis=1)
    both = jnp.stack([v0, v1], axis=1)
    return jnp.concatenate([both, both], axis=-1)


def _ssm_tables(a_re, a_im, log_step, b_re, b_im, c_re, c_im, d_skip):
    hi = lax.Precision.HIGHEST
    t_len = SSM_T
    step = jnp.exp(log_step.astype(F32))[:, None]
    ar = jnp.minimum(a_re.astype(F32), -1e-4)
    ai = a_im.astype(F32)
    mag = jnp.exp(step * ar)
    ph = step * ai
    abar_re = mag * jnp.cos(ph)
    abar_im = mag * jnp.sin(ph)
    den = ar * ar + ai * ai
    e_re = abar_re - 1.0
    e_im = abar_im
    f_re = (e_re * ar + e_im * ai) / den
    f_im = (e_im * ar - e_re * ai) / den
    br = b_re.astype(F32)
    bi = b_im.astype(F32)
    bb_re = f_re[..., None] * br - f_im[..., None] * bi
    bb_im = f_re[..., None] * bi + f_im[..., None] * br
    tau = jnp.arange(t_len + 1, dtype=F32)[None, :, None]
    pmag = jnp.exp(tau * (step * ar)[:, None, :])
    pph = tau * ph[:, None, :]
    pw_re = pmag * jnp.cos(pph)
    pw_im = pmag * jnp.sin(pph)
    w_re = pw_re[..., None] * bb_re[:, None] - pw_im[..., None] * bb_im[:, None]
    w_im = pw_re[..., None] * bb_im[:, None] + pw_im[..., None] * bb_re[:, None]
    cr = c_re.astype(F32)
    ci = c_im.astype(F32)
    kern = (jnp.einsum('gcn,gtnk->gtck', cr, w_re[:, :t_len], precision=hi)
            - jnp.einsum('gcn,gtnk->gtck', ci, w_im[:, :t_len], precision=hi))
    skip = d_skip.astype(F32).reshape(SSM_G, SSM_CG)
    kern = kern.at[:, 0].add(skip[:, :, None] * jnp.eye(SSM_CG, dtype=F32))
    kflat = jnp.transpose(kern, (0, 3, 1, 2)).reshape(SSM_G, SSM_CG, t_len * SSM_CG)
    kpad = jnp.concatenate([jnp.zeros_like(kflat), kflat], axis=-1)
    pt_re = jnp.transpose(jnp.flip(w_re[:, :t_len], axis=1), (0, 1, 3, 2)).reshape(SSM_G, t_len * SSM_CG, SSM_N)
    pt_im = jnp.transpose(jnp.flip(w_im[:, :t_len], axis=1), (0, 1, 3, 2)).reshape(SSM_G, t_len * SSM_CG, SSM_N)
    pt = jnp.concatenate([pt_re, pt_im], axis=-1)
    up_re = pw_re[:, 1:]
    up_im = pw_im[:, 1:]
    q_re = cr[:, None] * up_re[:, :, None, :] - ci[:, None] * up_im[:, :, None, :]
    q_im = cr[:, None] * up_im[:, :, None, :] + ci[:, None] * up_re[:, :, None, :]
    qt = jnp.concatenate([jnp.transpose(q_re, (0, 3, 1, 2)), -jnp.transpose(q_im, (0, 3, 1, 2))], axis=1)
    qt = qt.reshape(SSM_G, 2 * SSM_N, t_len * SSM_CG)
    at_re = pw_re[:, t_len]
    at_im = pw_im[:, t_len]
    a1 = jnp.concatenate([at_re, at_re], axis=-1)
    a2 = jnp.concatenate([-at_im, at_im], axis=-1)
    return kpad, pt.astype(BF16), qt.astype(BF16), a1, a2


def _ssm_state_kernel(u_ref, pt_ref, s_ref):
    s_ref[0] = _dot(u_ref[0], pt_ref[0])


def _ssm_scan_kernel(s_ref, a1_ref, a2_ref, x_ref, *, n_chunks, bsz):
    a1 = a1_ref[...][:, None, :]
    a2 = a2_ref[...][:, None, :]

    def body(c, x):
        rows = pl.ds(c * bsz, bsz)
        x_ref[:, rows, :] = x
        return a1 * x + a2 * pltpu.roll(x, SSM_N, 2) + s_ref[:, rows, :]

    lax.fori_loop(0, n_chunks, body, jnp.zeros((SSM_G, bsz, 2 * SSM_N), F32))


def _ssm_out_kernel(u_ref, x_ref, kp_ref, qt_ref, y_ref, mt_ref):
    kp = kp_ref[0]
    width = SSM_T * SSM_CG
    lane_tile = 128
    for sub in range(0, lane_tile, SSM_CG):
        shifted = kp if sub == 0 else pltpu.roll(kp, 2 * width - sub, 1)
        for s in range(SSM_T):
            off = (SSM_T - s) * SSM_CG
            if off % lane_tile == sub:
                base = off - sub
                mt_ref[s * SSM_CG:(s + 1) * SSM_CG, :] = shifted[:, base:base + width].astype(BF16)
    x = x_ref[0]
    x_hi = x.astype(BF16)
    x_lo = (x - x_hi.astype(F32)).astype(BF16)
    y = _dot(u_ref[0], mt_ref[...]) + _dot(x_hi, qt_ref[0]) + _dot(x_lo, qt_ref[0])
    y_ref[0] = y.astype(y_ref.dtype)


def _ssm(ug, kpad, pt, qt, a1, a2, *, n_chunks, bsz):
    g, r, w = ug.shape
    n2 = 2 * SSM_N
    per_g = lambda gi: (gi, 0, 0)
    params = pltpu.CompilerParams(dimension_semantics=("parallel",), vmem_limit_bytes=VMEM_LIMIT)
    s = pl.pallas_call(
        _ssm_state_kernel,
        grid=(g,),
        in_specs=[pl.BlockSpec((1, r, w), per_g), pl.BlockSpec((1, w, n2), per_g)],
        out_specs=pl.BlockSpec((1, r, n2), per_g),
        out_shape=jax.ShapeDtypeStruct((g, r, n2), F32),
        compiler_params=params,
        name="ssm_chunk_state",
    )(ug, pt)
    x = pl.pallas_call(
        functools.partial(_ssm_scan_kernel, n_chunks=n_chunks, bsz=bsz),
        out_shape=jax.ShapeDtypeStruct((g, r, n2), F32),
        compiler_params=pltpu.CompilerParams(vmem_limit_bytes=VMEM_LIMIT),
        name="ssm_chunk_scan",
    )(s, a1, a2)
    return pl.pallas_call(
        _ssm_out_kernel,
        grid=(g,),
        in_specs=[pl.BlockSpec((1, r, w), per_g), pl.BlockSpec((1, r, n2), per_g),
                  pl.BlockSpec((1, SSM_CG, 2 * w), per_g), pl.BlockSpec((1, n2, w), per_g)],
        out_specs=pl.BlockSpec((1, r, w), per_g),
        out_shape=jax.ShapeDtypeStruct((g, r, w), BF16),
        scratch_shapes=[pltpu.VMEM((w, w), BF16)],
        compiler_params=params,
        name="ssm_output",
    )(ug, x, kpad, qt)


def _mix_kernel(x_ref, gi_ref, bi_ref, att_ref, y_ref, wglu_ref, bglu_ref, wout_ref, g1_ref, b1_ref,
                wr_ref, br_ref, h1_ref, idx_ref, gate_ref, rank_ref, cnt_ref, carry_ref):
    step = pl.program_id(0)

    @pl.when(step == 0)
    def _():
        carry_ref[...] = jnp.zeros(carry_ref.shape, F32)

    h0 = _layer_norm(x_ref[...], gi_ref[...], bi_ref[...])
    y = y_ref[...].astype(F32)
    y = y * (0.5 * (1.0 + jnp.tanh(math.sqrt(2.0 / math.pi) * (y + 0.044715 * (y * y * y)))))
    y = y * jax.nn.sigmoid(_dot(y.astype(BF16), wglu_ref[...]) + bglu_ref[...])
    mix = _dot(att_ref[...], wout_ref[0:ATT_W, :]) + _dot(y.astype(BF16), wout_ref[ATT_W:, :])
    h1 = _layer_norm(DEEPNORM_ALPHA * h0 + mix, g1_ref[...], b1_ref[...])
    h1_ref[...] = h1

    logits = _dot_nt(wr_ref[...], h1, precision=lax.Precision.HIGHEST) + br_ref[...]
    tm = logits.shape[1]
    eidx = lax.broadcasted_iota(I32, logits.shape, 0)
    vals, hots = [], []
    rest = logits
    for _ in range(TOP_K):
        mx = jnp.max(rest, axis=0, keepdims=True)
        first = jnp.min(jnp.where(rest == mx, eidx, N_EXPERTS), axis=0, keepdims=True)
        hot = eidx == first
        vals.append(mx)
        hots.append(hot)
        rest = jnp.where(hot, -jnp.inf, rest)
    exps = [jnp.exp(v - vals[0]) for v in vals]
    denom = exps[0] + exps[1] + exps[2] + exps[3]
    gate_ref[...] = jnp.concatenate([e / denom for e in exps], axis=0)
    idx_ref[...] = jnp.concatenate(
        [jnp.sum(jnp.where(h, eidx, 0), axis=0, keepdims=True) for h in hots], axis=0)

    hot_all = (hots[0] | hots[1] | hots[2] | hots[3]).astype(F32)
    tri = (lax.broadcasted_iota(I32, (tm, tm), 0) < lax.broadcasted_iota(I32, (tm, tm), 1)).astype(BF16)
    before = _dot(hot_all.astype(BF16), tri) + carry_ref[...]
    rank_ref[...] = jnp.concatenate(
        [jnp.sum(jnp.where(h, before, 0.0), axis=0, keepdims=True) for h in hots], axis=0).astype(I32)
    carry_ref[...] = carry_ref[...] + jnp.sum(hot_all, axis=1, keepdims=True)
    cnt_ref[...] = jnp.broadcast_to(carry_ref[...], cnt_ref.shape)


def _mix(x2, gi, bi, att, y, wglu, bglu, wout, g1, b1, wr_t, br):
    t, d = x2.shape
    tm = ROW_TILE
    row = lambda i: (i, 0)
    col = lambda i: (0, i)
    const = lambda i: (0, 0)
    return pl.pallas_call(
        _mix_kernel,
        grid=(t // tm,),
        in_specs=[
            pl.BlockSpec((tm, d), row), pl.BlockSpec((1, d), const), pl.BlockSpec((1, d), const),
            pl.BlockSpec((tm, ATT_W), row), pl.BlockSpec((tm, SSM_W), row),
            pl.BlockSpec((SSM_W, SSM_W), const), pl.BlockSpec((1, SSM_W), const),
            pl.BlockSpec((d, d), const), pl.BlockSpec((1, d), const), pl.BlockSpec((1, d), const),
            pl.BlockSpec((N_EXPERTS, d), const), pl.BlockSpec((N_EXPERTS, 1), const),
        ],
        out_specs=[
            pl.BlockSpec((tm, d), row),
            pl.BlockSpec((TOP_K, tm), col), pl.BlockSpec((TOP_K, tm), col), pl.BlockSpec((TOP_K, tm), col),
            pl.BlockSpec((N_EXPERTS, 128), const),
        ],
        out_shape=[
            jax.ShapeDtypeStruct((t, d), F32),
            jax.ShapeDtypeStruct((TOP_K, t), I32), jax.ShapeDtypeStruct((TOP_K, t), F32),
            jax.ShapeDtypeStruct((TOP_K, t), I32),
            jax.ShapeDtypeStruct((N_EXPERTS, 128), F32),
        ],
        scratch_shapes=[pltpu.VMEM((N_EXPERTS, 1), F32)],
        compiler_params=pltpu.CompilerParams(dimension_semantics=("arbitrary",), vmem_limit_bytes=VMEM_LIMIT),
        name="mix_ln1_router",
    )(x2, gi, bi, att, y, wglu, bglu, wout, g1, b1, wr_t, br)


def _row_copy(src, src_row, dst, dst_row, sem):
    return pltpu.make_async_copy(src.at[pl.ds(src_row, 1), :], dst.at[pl.ds(dst_row, 1), :], sem)


def _dispatch_kernel(pad_start_ref, pad_count_ref, n_used_ref, dest_ref, h1_ref, xs_hbm, zero_ref, sem, zsem):
    step = pl.program_id(0)
    tm = dest_ref.shape[1]
    n_blocks = xs_hbm.shape[0] // MOE_BLOCK

    def zero_block(blk):
        return pltpu.make_async_copy(zero_ref, xs_hbm.at[pl.ds(blk * MOE_BLOCK, MOE_BLOCK), :], zsem)

    @pl.when(step == 0)
    def _():
        zero_ref[...] = jnp.zeros(zero_ref.shape, F32)
        for e in range(N_EXPERTS):
            start = pad_start_ref[e]
            count = pad_count_ref[e]

            def fill(r, c):
                _row_copy(zero_ref, 0, xs_hbm, start + r, zsem).start()
                return c

            def drain(r, c):
                _row_copy(zero_ref, 0, xs_hbm, start, zsem).wait()
                return c

            lax.fori_loop(0, count, fill, 0)
            lax.fori_loop(0, count, drain, 0)

        def fill_block(blk, c):
            zero_block(blk).start()
            return c

        def drain_block(blk, c):
            zero_block(blk).wait()
            return c

        lax.fori_loop(n_used_ref[0], n_blocks, fill_block, 0)
        lax.fori_loop(n_used_ref[0], n_blocks, drain_block, 0)

    def issue(r, c):
        for k in range(TOP_K):
            _row_copy(h1_ref, r, xs_hbm, dest_ref[k, r], sem).start()
        return c

    lax.fori_loop(0, tm, issue, 0, unroll=8)
    for _ in range(TOP_K):
        pltpu.make_async_copy(h1_ref, xs_hbm.at[pl.ds(0, tm), :], sem).wait()


def _dispatch(pad_start, pad_count, n_used, dest_t, h1, n_rows):
    t, d = h1.shape
    tm = ROW_TILE
    grid_spec = pltpu.PrefetchScalarGridSpec(
        num_scalar_prefetch=3,
        grid=(t // tm,),
        in_specs=[
            pl.BlockSpec((TOP_K, tm), lambda i, ps, pc, nu: (0, i), memory_space=pltpu.SMEM),
            pl.BlockSpec((tm, d), lambda i, ps, pc, nu: (i, 0)),
        ],
        out_specs=pl.BlockSpec(memory_space=pl.ANY),
        scratch_shapes=[pltpu.VMEM((MOE_BLOCK, d), F32), pltpu.SemaphoreType.DMA, pltpu.SemaphoreType.DMA],
    )
    return pl.pallas_call(
        _dispatch_kernel,
        grid_spec=grid_spec,
        out_shape=jax.ShapeDtypeStruct((n_rows, d), F32),
        compiler_params=pltpu.CompilerParams(dimension_semantics=("arbitrary",), has_side_effects=True),
        name="moe_dispatch",
    )(pad_start, pad_count, n_used, dest_t, h1)


def _expert_kernel(be_ref, nb_ref, x_ref, wg_ref, bg_ref, wu_ref, bu_ref, wd_ref, bd_ref, y_ref,
                   wg_b, wu_b, wd_b):
    i = pl.program_id(0)

    @pl.when(i < nb_ref[0])
    def _():
        @pl.when((i == 0) | (be_ref[i] != be_ref[jnp.maximum(i - 1, 0)]))
        def _():
            wg_b[...] = wg_ref[0].astype(BF16)
            wu_b[...] = wu_ref[0].astype(BF16)
            wd_b[...] = wd_ref[0].astype(BF16)

        x = x_ref[...].astype(BF16)
        gate = _dot(x, wg_b[...]) + bg_ref[0]
        up = _dot(x, wu_b[...]) + bu_ref[0]
        gate = jnp.minimum(gate, SWIGLU_LIMIT)
        up = jnp.clip(up, -SWIGLU_LIMIT, SWIGLU_LIMIT)
        act = (up + 1.0) * gate * jax.nn.sigmoid(gate * SWIGLU_ALPHA)
        y_ref[...] = _dot(act.astype(BF16), wd_b[...]) + bd_ref[0]

    @pl.when(i >= nb_ref[0])
    def _():
        y_ref[...] = jnp.zeros(y_ref.shape, F32)


def _experts(block_expert, n_used, xs, wg, bg, wu, bu, wd, bd):
    n_rows, d = xs.shape
    dff = wg.shape[2]
    nb = n_rows // MOE_BLOCK
    blk = lambda i, be, nu: (jnp.minimum(i, nu[0] - 1), 0)
    wsel = lambda i, be, nu: (be[jnp.minimum(i, nu[0] - 1)], 0, 0)
    grid_spec = pltpu.PrefetchScalarGridSpec(
        num_scalar_prefetch=2,
        grid=(nb,),
        in_specs=[
            pl.BlockSpec((MOE_BLOCK, d), blk),
            pl.BlockSpec((1, d, dff), wsel), pl.BlockSpec((1, 1, dff), wsel),
            pl.BlockSpec((1, d, dff), wsel), pl.BlockSpec((1, 1, dff), wsel),
            pl.BlockSpec((1, dff, d), wsel), pl.BlockSpec((1, 1, d), wsel),
        ],
        out_specs=pl.BlockSpec((MOE_BLOCK, d), lambda i, be, nu: (i, 0)),
        scratch_shapes=[pltpu.VMEM((d, dff), BF16), pltpu.VMEM((d, dff), BF16), pltpu.VMEM((dff, d), BF16)],
    )
    return pl.pallas_call(
        _expert_kernel,
        grid_spec=grid_spec,
        out_shape=jax.ShapeDtypeStruct((n_rows, d), F32),
        compiler_params=pltpu.CompilerParams(dimension_semantics=("arbitrary",), vmem_limit_bytes=VMEM_LIMIT),
        name="moe_experts",
    )(block_expert, n_used, xs, wg, bg, wu, bu, wd, bd)


def _combine_kernel(dest_ref, dest_next_ref, h1_ref, gate_ref, ys_hbm, g2_ref, b2_ref, o_ref, buf, sems):
    step = pl.program_id(0)
    n_steps = pl.num_programs(0)
    tm = h1_ref.shape[0]
    slot = step % 2

    def issue(dref, s):
        def body(r, c):
            for k in range(TOP_K):
                pltpu.make_async_copy(ys_hbm.at[pl.ds(dref[k, r], 1), :], buf.at[s, k, pl.ds(r, 1), :],
                                      sems.at[s]).start()
            return c
        lax.fori_loop(0, tm, body, 0, unroll=8)

    @pl.when(step == 0)
    def _():
        issue(dest_ref, 0)

    @pl.when(step + 1 < n_steps)
    def _():
        issue(dest_next_ref, 1 - slot)

    for k in range(TOP_K):
        pltpu.make_async_copy(ys_hbm.at[pl.ds(0, tm), :], buf.at[slot, k], sems.at[slot]).wait()

    gate = gate_ref[...]
    ffn = buf[slot, 0] * gate[:, 0:1]
    for k in range(1, TOP_K):
        ffn = ffn + buf[slot, k] * gate[:, k:k + 1]
    o_ref[...] = _layer_norm(DEEPNORM_ALPHA * h1_ref[...] + ffn, g2_ref[...], b2_ref[...])


def _combine(dest_t, h1, gates, ys, g2, b2):
    t, d = h1.shape
    tm = ROW_TILE
    n_steps = t // tm
    row = lambda i: (i, 0)
    const = lambda i: (0, 0)
    return pl.pallas_call(
        _combine_kernel,
        grid=(n_steps,),
        in_specs=[
            pl.BlockSpec((TOP_K, tm), lambda i: (0, i), memory_space=pltpu.SMEM),
            pl.BlockSpec((TOP_K, tm), lambda i: (0, jnp.minimum(i + 1, n_steps - 1)), memory_space=pltpu.SMEM),
            pl.BlockSpec((tm, d), row), pl.BlockSpec((tm, TOP_K), row),
            pl.BlockSpec(memory_space=pl.ANY),
            pl.BlockSpec((1, d), const), pl.BlockSpec((1, d), const),
        ],
        out_specs=pl.BlockSpec((tm, d), row),
        out_shape=jax.ShapeDtypeStruct((t, d), F32),
        scratch_shapes=[pltpu.VMEM((2, TOP_K, tm, d), F32), pltpu.SemaphoreType.DMA((2,))],
        compiler_params=pltpu.CompilerParams(dimension_semantics=("arbitrary",), vmem_limit_bytes=VMEM_LIMIT),
        name="moe_combine_ln2",
    )(dest_t, dest_t, h1, gates, ys, g2, b2)


def kernel(x, meta_tokens, ln_in_g, ln_in_b, rel_bias, w_in, lambda_q1, lambda_k1, lambda_q2, lambda_k2,
           subln_g, a_re, a_im, log_step, b_re, b_im, c_re, c_im, d_skip, w_glu, b_glu, w_out, ln1_g, ln1_b,
           w_router, b_router, w_gate, b_gate, w_up, b_up, w_down, b_down, ln2_g, ln2_b):
    bsz, seq, dm = x.shape
    assert seq % 512 == 0 and w_in.shape[0] == DEPTH == 1
    layer = 0
    row2 = lambda v: v.astype(F32).reshape(1, -1)

    w_in_b = w_in[layer].astype(BF16)
    gi, bi = row2(ln_in_g), row2(ln_in_b)
    q, k, vt, u = _inproj(x, gi, bi, w_in_b, tm=512, kv_tile=KV_TILE)
    meta = jnp.zeros((1, META_PAD, dm), x.dtype).at[0, :N_META].set(meta_tokens.astype(x.dtype))
    _, k_meta, vt_meta, u_meta = _inproj(meta, gi, bi, w_in_b, tm=META_PAD, kv_tile=META_PAD)

    lam_init = 0.8 - 0.6 * math.exp(-0.3 * layer)
    lam = (jnp.exp(jnp.sum(lambda_q1[layer].astype(F32) * lambda_k1[layer].astype(F32)))
           - jnp.exp(jnp.sum(lambda_q2[layer].astype(F32) * lambda_k2[layer].astype(F32))) + lam_init)
    att = _attention(lam.reshape(1), q, k, vt, k_meta[0], vt_meta[0, :, 0], _near_bias(rel_bias),
                     subln_g[layer].astype(F32).reshape(HEAD_W, 1), lam_init=lam_init)

    kpad, pt, qt, a1, a2 = _ssm_tables(a_re[layer], a_im[layer], log_step[layer], b_re[layer], b_im[layer],
                                     c_re[layer], c_im[layer], d_skip[layer])
    n_chunks = -(-(seq // SSM_T + 1) // 8) * 8
    lead = jnp.zeros((bsz, SSM_T, SSM_W), BF16).at[:, SSM_T - N_META:].set(u_meta[:, :N_META])
    tail = jnp.zeros((bsz, n_chunks * SSM_T - SSM_T - seq, SSM_W), BF16)
    ug = jnp.concatenate([lead, u, tail], axis=1).reshape(bsz, n_chunks, SSM_T, SSM_G, SSM_CG)
    ug = jnp.transpose(ug, (3, 1, 0, 2, 4)).reshape(SSM_G, n_chunks * bsz, SSM_T * SSM_CG)
    yg = _ssm(ug, kpad, pt, qt, a1, a2, n_chunks=n_chunks, bsz=bsz)
    yg = jnp.transpose(yg.reshape(SSM_G, n_chunks, bsz, SSM_T, SSM_CG), (2, 1, 3, 0, 4))
    y_ssm = yg.reshape(bsz, n_chunks * SSM_T, SSM_W)[:, SSM_T:SSM_T + seq]

    t = bsz * seq
    h1, idx_t, gate_t, rank_t, cnt = _mix(
        x.reshape(t, dm), gi, bi, att.reshape(t, ATT_W), y_ssm.reshape(t, SSM_W),
        w_glu[layer].astype(BF16), row2(b_glu[layer]), w_out[layer].astype(BF16),
        row2(ln1_g[layer]), row2(ln1_b[layer]),
        jnp.transpose(w_router[layer].astype(F32)), b_router[layer].astype(F32).reshape(N_EXPERTS, 1))

    counts = cnt[:, 0].astype(I32)
    padded = (counts + MOE_BLOCK - 1) // MOE_BLOCK * MOE_BLOCK
    experts = jnp.arange(N_EXPERTS, dtype=I32)
    padded_end = jnp.sum(jnp.where(experts[:, None] <= experts[None, :], padded[:, None], 0), axis=0)
    padded_start = padded_end - padded
    dest_t = jnp.sum(jnp.where(idx_t[..., None] == experts, padded_start, 0), axis=-1) + rank_t
    n_blocks = t * TOP_K // MOE_BLOCK + N_EXPERTS
    block_row0 = jnp.arange(n_blocks, dtype=I32) * MOE_BLOCK
    block_expert = jnp.minimum(jnp.sum((padded_end[None, :] <= block_row0[:, None]).astype(I32), axis=1),
                               N_EXPERTS - 1)
    n_used = (padded_end[-1:] // MOE_BLOCK).astype(I32)

    xs = _dispatch((padded_start + counts).astype(I32), (padded - counts).astype(I32), n_used, dest_t, h1,
                   n_blocks * MOE_BLOCK)
    b3 = lambda v: v.astype(F32)[:, None, :]
    ys = _experts(block_expert, n_used, xs, w_gate[layer], b3(b_gate[layer]), w_up[layer], b3(b_up[layer]),
                  w_down[layer], b3(b_down[layer]))
    out = _combine(dest_t, h1, jnp.transpose(gate_t), ys, row2(ln2_g[layer]), row2(ln2_b[layer]))
    return out.reshape(bsz, seq, dm)
```

```python
import functools
import math

import jax
import jax.numpy as jnp
import numpy as np
from jax import lax
from jax.experimental import pallas as pl
from jax.experimental.pallas import tpu as pltpu

F32 = jnp.float32
BF16 = jnp.bfloat16
I32 = jnp.int32

DEPTH = 1
N_META = 16
CHUNK = 64
ATT_HEADS = 4
HEAD_DIM = 64
HEAD_W = 2 * HEAD_DIM
ATT_W = ATT_HEADS * HEAD_W
SSM_W = 512
SSM_CG = 16
SSM_G = SSM_W // SSM_CG
SSM_N = 64
N_BUCKETS = 32
MAX_DISTANCE = 128
N_EXPERTS = 32
TOP_K = 4
SWIGLU_LIMIT = 7.0
SWIGLU_ALPHA = 1.702
LN_EPS = 1e-5
NEG_INF = -1e30
DEEPNORM_ALPHA = (2.0 * DEPTH) ** 0.25
LOG2E = 1.4426950408889634

Q_TILE = 512
KV_TILE = 512
V_ONES = 16
V_ROWS = HEAD_W + V_ONES
META_PAD = 128
SSM_T = 64
ROW_TILE = 256
MOE_BLOCK = 256
VMEM_LIMIT = 56 * 1024 * 1024


def _layer_norm(x, g, b):
    mu = jnp.mean(x, axis=-1, keepdims=True)
    xc = x - mu
    var = jnp.mean(xc * xc, axis=-1, keepdims=True)
    return xc * lax.rsqrt(var + LN_EPS) * g + b


def _dot(a, b):
    return jnp.dot(a, b, preferred_element_type=F32)


def _dot_nt(a, b, precision=None):
    return lax.dot_general(a, b, (((1,), (1,)), ((), ())), precision=precision,
                           preferred_element_type=F32)


def _inproj_kernel(x_ref, g_ref, b_ref, w_ref, q_ref, k_ref, vt_ref, u_ref, *, kv_tile, q_scale):
    h = _layer_norm(x_ref[0], g_ref[...], b_ref[...]).astype(BF16)
    tm = h.shape[0]
    q_ref[0] = (_dot(h, w_ref[:, 0:ATT_W]) * q_scale).astype(BF16)
    k_ref[0] = _dot(h, w_ref[:, ATT_W:2 * ATT_W]).astype(BF16)
    v = _dot(h, w_ref[:, 2 * ATT_W:3 * ATT_W])
    ones = jnp.ones((V_ONES, kv_tile), BF16)
    for hh in range(ATT_HEADS):
        vt = v[:, hh * HEAD_W:(hh + 1) * HEAD_W].T.astype(BF16)
        for j in range(tm // kv_tile):
            vt_ref[0, hh, j, 0:HEAD_W, :] = vt[:, j * kv_tile:(j + 1) * kv_tile]
            vt_ref[0, hh, j, HEAD_W:V_ROWS, :] = ones
    u_ref[0] = _dot(h, w_ref[:, 3 * ATT_W:]).astype(BF16)


def _inproj(x, g, b, w, *, tm, kv_tile):
    bsz, s, d = x.shape
    q_scale = HEAD_DIM ** -0.5 * LOG2E
    n_cols = w.shape[1]
    row = lambda bi, i: (bi, i, 0)
    return pl.pallas_call(
        functools.partial(_inproj_kernel, kv_tile=kv_tile, q_scale=q_scale),
        grid=(bsz, s // tm),
        in_specs=[
            pl.BlockSpec((1, tm, d), row),
            pl.BlockSpec((1, d), lambda bi, i: (0, 0)),
            pl.BlockSpec((1, d), lambda bi, i: (0, 0)),
            pl.BlockSpec((d, n_cols), lambda bi, i: (0, 0)),
        ],
        out_specs=[
            pl.BlockSpec((1, tm, ATT_W), row),
            pl.BlockSpec((1, tm, ATT_W), row),
            pl.BlockSpec((1, ATT_HEADS, tm // kv_tile, V_ROWS, kv_tile), lambda bi, i: (bi, 0, i, 0, 0)),
            pl.BlockSpec((1, tm, SSM_W), row),
        ],
        out_shape=[
            jax.ShapeDtypeStruct((bsz, s, ATT_W), BF16),
            jax.ShapeDtypeStruct((bsz, s, ATT_W), BF16),
            jax.ShapeDtypeStruct((bsz, ATT_HEADS, s // kv_tile, V_ROWS, kv_tile), BF16),
            jax.ShapeDtypeStruct((bsz, s, SSM_W), BF16),
        ],
        compiler_params=pltpu.CompilerParams(
            dimension_semantics=("parallel", "parallel"), vmem_limit_bytes=VMEM_LIMIT),
        name="inproj",
    )(x, g, b, w)


def _attn_kernel(lam_ref, q_ref, k_ref, vt_ref, km_ref, vtm_ref, bias_ref, g_ref, o_ref,
                 m_ref, acc_ref, qz_ref, sa_ref, sb_ref, sm_ref, *, out_scale):
    i = pl.program_id(2)
    q = q_ref[0]
    lane = lax.broadcasted_iota(I32, q.shape, 1)
    zero = jnp.zeros_like(q)
    qz_ref[0:Q_TILE, :] = jnp.where(lane < HEAD_DIM, q, zero)
    qz_ref[Q_TILE:, :] = jnp.where(lane >= HEAD_DIM, q, zero)

    m_ref[...] = jnp.full(m_ref.shape, NEG_INF, F32)
    acc_ref[...] = jnp.zeros(acc_ref.shape, F32)

    def scores(kt):
        return _dot_nt(kt, qz_ref[...])

    def k_tile(t):
        return k_ref[0, pl.ds(pl.multiple_of(t * KV_TILE, KV_TILE), KV_TILE), :]

    def absorb(s, vt):
        m_old = m_ref[...]
        m_new = jnp.maximum(m_old, jnp.max(s, axis=0, keepdims=True))
        alpha = jnp.exp2(m_old - m_new)
        p = jnp.exp2(s - m_new).astype(BF16)
        acc_ref[...] = acc_ref[...] * alpha + _dot(vt, p)
        m_ref[...] = m_new

    n_far = jnp.maximum(i - 1, 0)
    peel = n_far % 2

    @pl.when(peel == 1)
    def _():
        absorb(scores(k_tile(0)), vt_ref[0, 0, 0])

    sa_ref[...] = scores(k_tile(peel))

    def far_pair(j, carry):
        t0 = peel + 2 * j
        sb_ref[...] = scores(k_tile(t0 + 1))
        absorb(sa_ref[...], vt_ref[0, 0, t0])
        sa_ref[...] = scores(k_tile(t0 + 2))
        absorb(sb_ref[...], vt_ref[0, 0, t0 + 1])
        return carry

    lax.fori_loop(0, n_far // 2, far_pair, 0)

    sb_ref[...] = scores(k_tile(i))
    absorb(sa_ref[...] + bias_ref[0, 0, META_PAD:META_PAD + KV_TILE, :], vt_ref[0, 0, n_far])
    sm_ref[...] = scores(km_ref[...])
    absorb(sb_ref[...] + bias_ref[0, 0, META_PAD + KV_TILE:, :], vt_ref[0, 0, i])
    absorb(sm_ref[...] + bias_ref[0, 0, 0:META_PAD, :], vtm_ref[0])

    acc = acc_ref[...]
    lam = lam_ref[0]
    o1 = acc[0:HEAD_W, 0:Q_TILE] / acc[HEAD_W:HEAD_W + 1, 0:Q_TILE]
    o2 = acc[0:HEAD_W, Q_TILE:] / acc[HEAD_W:HEAD_W + 1, Q_TILE:]
    o = o1 - lam * o2
    ms = jnp.mean(o * o, axis=0, keepdims=True)
    o = o * lax.rsqrt(ms + LN_EPS) * g_ref[...] * out_scale
    o_ref[0] = o.T.astype(o_ref.dtype)


def _attention(lam, q, k, vt, k_meta, vt_meta, bias, subln_g, *, lam_init):
    bsz, s, _ = q.shape
    nq = s // Q_TILE
    n_near = META_PAD + 2 * KV_TILE
    grid_spec = pltpu.PrefetchScalarGridSpec(
        num_scalar_prefetch=1,
        grid=(bsz, ATT_HEADS, nq),
        in_specs=[
            pl.BlockSpec((1, Q_TILE, HEAD_W), lambda b, h, i, lam: (b, i, h)),
            pl.BlockSpec((1, s, HEAD_W), lambda b, h, i, lam: (b, 0, h)),
            pl.BlockSpec((1, 1, s // KV_TILE, V_ROWS, KV_TILE), lambda b, h, i, lam: (b, h, 0, 0, 0)),
            pl.BlockSpec((META_PAD, HEAD_W), lambda b, h, i, lam: (0, h)),
            pl.BlockSpec((1, V_ROWS, META_PAD), lambda b, h, i, lam: (h, 0, 0)),
            pl.BlockSpec((1, 1, n_near, 2 * Q_TILE), lambda b, h, i, lam: (h, jnp.minimum(i, 1), 0, 0)),
            pl.BlockSpec((HEAD_W, 1), lambda b, h, i, lam: (0, 0)),
        ],
        out_specs=pl.BlockSpec((1, Q_TILE, HEAD_W), lambda b, h, i, lam: (b, i, h)),
        scratch_shapes=[
            pltpu.VMEM((1, 2 * Q_TILE), F32), pltpu.VMEM((V_ROWS, 2 * Q_TILE), F32),
            pltpu.VMEM((2 * Q_TILE, HEAD_W), BF16),
            pltpu.VMEM((KV_TILE, 2 * Q_TILE), F32), pltpu.VMEM((KV_TILE, 2 * Q_TILE), F32),
            pltpu.VMEM((META_PAD, 2 * Q_TILE), F32),
        ],
    )
    return pl.pallas_call(
        functools.partial(_attn_kernel, out_scale=1.0 - lam_init),
        grid_spec=grid_spec,
        out_shape=jax.ShapeDtypeStruct((bsz, s, ATT_W), BF16),
        compiler_params=pltpu.CompilerParams(
            dimension_semantics=("parallel", "parallel", "arbitrary"), vmem_limit_bytes=VMEM_LIMIT),
        name="diff_attention",
    )(lam, q, k, vt, k_meta, vt_meta, bias, subln_g)


def _t5_bucket(rel):
    nb = N_BUCKETS // 2
    max_exact = nb // 2
    ret = jnp.where(rel > 0, nb, 0)
    n = jnp.abs(rel)
    n_f = jnp.maximum(n, 1).astype(F32)
    large = max_exact + (jnp.log(n_f / max_exact) / math.log(MAX_DISTANCE / max_exact)
                         * (nb - max_exact)).astype(I32)
    large = jnp.minimum(large, nb - 1)
    return ret + jnp.where(n < max_exact, n, large)


def _near_bias(rel_bias):
    table = rel_bias.astype(F32)
    far = table[N_BUCKETS // 2 - 1]
    c = jnp.arange(Q_TILE, dtype=I32)[None, :]
    r = jnp.arange(KV_TILE, dtype=I32)[:, None]

    span = 2 * Q_TILE + KV_TILE
    width = 2 * span
    rels = jnp.arange(-span, span, dtype=I32)
    hot = _t5_bucket(rels)[:, None] == jnp.arange(N_BUCKETS, dtype=I32)
    by_rel = jnp.sum(jnp.where(hot[None], jnp.transpose(table - far)[:, None, :], 0.0), axis=-1) * LOG2E
    skew = jnp.broadcast_to(by_rel[:, None, :], (ATT_HEADS, Q_TILE, width)).reshape(ATT_HEADS, Q_TILE * width)
    skew = skew[:, :Q_TILE * (width - 1)].reshape(ATT_HEADS, Q_TILE, width - 1)

    def bias_of(offset, rows):
        return jnp.transpose(skew[:, :, span + offset:span + offset + rows], (0, 2, 1))

    own = jnp.where((r // CHUNK <= c // CHUNK)[None], bias_of(0, KV_TILE), NEG_INF)
    prev = bias_of(-KV_TILE, KV_TILE)
    rm = jnp.arange(META_PAD, dtype=I32)[:, None]
    meta_ok = (rm < N_META)[None]
    meta0 = jnp.where(meta_ok, bias_of(-N_META, META_PAD), NEG_INF)
    meta1 = jnp.where(meta_ok, bias_of(-N_META - Q_TILE, META_PAD), NEG_INF)
    v0 = jnp.concatenate([meta0, jnp.full_like(prev, NEG_INF), own], axis=1)
    v1 = jnp.concatenate([meta1, prev, own], axis=1)
    both = jnp.stack([v0, v1], axis=1)
    return jnp.concatenate([both, both], axis=-1)


def _ssm_tables(a_re, a_im, log_step, b_re, b_im, c_re, c_im, d_skip):
    hi = lax.Precision.HIGHEST
    t_len = SSM_T
    step = jnp.exp(log_step.astype(F32))[:, None]
    ar = jnp.minimum(a_re.astype(F32), -1e-4)
    ai = a_im.astype(F32)
    mag = jnp.exp(step * ar)
    ph = step * ai
    abar_re = mag * jnp.cos(ph)
    abar_im = mag * jnp.sin(ph)
    den = ar * ar + ai * ai
    e_re = abar_re - 1.0
    e_im = abar_im
    f_re = (e_re * ar + e_im * ai) / den
    f_im = (e_im * ar - e_re * ai) / den
    br = b_re.astype(F32)
    bi = b_im.astype(F32)
    bb_re = f_re[..., None] * br - f_im[..., None] * bi
    bb_im = f_re[..., None] * bi + f_im[..., None] * br
    tau = jnp.arange(t_len + 1, dtype=F32)[None, :, None]
    pmag = jnp.exp(tau * (step * ar)[:, None, :])
    pph = tau * ph[:, None, :]
    pw_re = pmag * jnp.cos(pph)
    pw_im = pmag * jnp.sin(pph)
    w_re = pw_re[..., None] * bb_re[:, None] - pw_im[..., None] * bb_im[:, None]
    w_im = pw_re[..., None] * bb_im[:, None] + pw_im[..., None] * bb_re[:, None]
    cr = c_re.astype(F32)
    ci = c_im.astype(F32)
    kern = (jnp.einsum('gcn,gtnk->gtck', cr, w_re[:, :t_len], precision=hi)
            - jnp.einsum('gcn,gtnk->gtck', ci, w_im[:, :t_len], precision=hi))
    skip = d_skip.astype(F32).reshape(SSM_G, SSM_CG)
    kern = kern.at[:, 0].add(skip[:, :, None] * jnp.eye(SSM_CG, dtype=F32))
    kflat = jnp.transpose(kern, (0, 3, 1, 2)).reshape(SSM_G, SSM_CG, t_len * SSM_CG)
    kpad = jnp.concatenate([jnp.zeros_like(kflat), kflat], axis=-1)
    pt_re = jnp.transpose(jnp.flip(w_re[:, :t_len], axis=1), (0, 1, 3, 2)).reshape(SSM_G, t_len * SSM_CG, SSM_N)
    pt_im = jnp.transpose(jnp.flip(w_im[:, :t_len], axis=1), (0, 1, 3, 2)).reshape(SSM_G, t_len * SSM_CG, SSM_N)
    pt = jnp.concatenate([pt_re, pt_im], axis=-1)
    up_re = pw_re[:, 1:]
    up_im = pw_im[:, 1:]
    q_re = cr[:, None] * up_re[:, :, None, :] - ci[:, None] * up_im[:, :, None, :]
    q_im = cr[:, None] * up_im[:, :, None, :] + ci[:, None] * up_re[:, :, None, :]
    qt = jnp.concatenate([jnp.transpose(q_re, (0, 3, 1, 2)), -jnp.transpose(q_im, (0, 3, 1, 2))], axis=1)
    qt = qt.reshape(SSM_G, 2 * SSM_N, t_len * SSM_CG)
    at_re = pw_re[:, t_len]
    at_im = pw_im[:, t_len]
    a1 = jnp.concatenate([at_re, at_re], axis=-1)
    a2 = jnp.concatenate([-at_im, at_im], axis=-1)
    return kpad, pt.astype(BF16), qt.astype(BF16), a1, a2


def _ssm_state_kernel(u_ref, pt_ref, s_ref):
    s_ref[0] = _dot(u_ref[0], pt_ref[0])


def _ssm_scan_kernel(s_ref, a1_ref, a2_ref, x_ref, *, n_chunks, bsz):
    a1 = a1_ref[...][:, None, :]
    a2 = a2_ref[...][:, None, :]

    def body(c, x):
        rows = pl.ds(c * bsz, bsz)
        x_ref[:, rows, :] = x
        return a1 * x + a2 * pltpu.roll(x, SSM_N, 2) + s_ref[:, rows, :]

    lax.fori_loop(0, n_chunks, body, jnp.zeros((SSM_G, bsz, 2 * SSM_N), F32))


def _ssm_out_kernel(u_ref, x_ref, kp_ref, qt_ref, y_ref, mt_ref):
    kp = kp_ref[0]
    width = SSM_T * SSM_CG
    lane_tile = 128
    for sub in range(0, lane_tile, SSM_CG):
        shifted = kp if sub == 0 else pltpu.roll(kp, 2 * width - sub, 1)
        for s in range(SSM_T):
            off = (SSM_T - s) * SSM_CG
            if off % lane_tile == sub:
                base = off - sub
                mt_ref[s * SSM_CG:(s + 1) * SSM_CG, :] = shifted[:, base:base + width].astype(BF16)
    x = x_ref[0]
    x_hi = x.astype(BF16)
    x_lo = (x - x_hi.astype(F32)).astype(BF16)
    y = _dot(u_ref[0], mt_ref[...]) + _dot(x_hi, qt_ref[0]) + _dot(x_lo, qt_ref[0])
    y_ref[0] = y.astype(y_ref.dtype)


def _ssm(ug, kpad, pt, qt, a1, a2, *, n_chunks, bsz):
    g, r, w = ug.shape
    n2 = 2 * SSM_N
    per_g = lambda gi: (gi, 0, 0)
    params = pltpu.CompilerParams(dimension_semantics=("parallel",), vmem_limit_bytes=VMEM_LIMIT)
    s = pl.pallas_call(
        _ssm_state_kernel,
        grid=(g,),
        in_specs=[pl.BlockSpec((1, r, w), per_g), pl.BlockSpec((1, w, n2), per_g)],
        out_specs=pl.BlockSpec((1, r, n2), per_g),
        out_shape=jax.ShapeDtypeStruct((g, r, n2), F32),
        compiler_params=params,
        name="ssm_chunk_state",
    )(ug, pt)
    x = pl.pallas_call(
        functools.partial(_ssm_scan_kernel, n_chunks=n_chunks, bsz=bsz),
        out_shape=jax.ShapeDtypeStruct((g, r, n2), F32),
        compiler_params=pltpu.CompilerParams(vmem_limit_bytes=VMEM_LIMIT),
        name="ssm_chunk_scan",
    )(s, a1, a2)
    return pl.pallas_call(
        _ssm_out_kernel,
        grid=(g,),
        in_specs=[pl.BlockSpec((1, r, w), per_g), pl.BlockSpec((1, r, n2), per_g),
                  pl.BlockSpec((1, SSM_CG, 2 * w), per_g), pl.BlockSpec((1, n2, w), per_g)],
        out_specs=pl.BlockSpec((1, r, w), per_g),
        out_shape=jax.ShapeDtypeStruct((g, r, w), BF16),
        scratch_shapes=[pltpu.VMEM((w, w), BF16)],
        compiler_params=params,
        name="ssm_output",
    )(ug, x, kpad, qt)


def _mix_kernel(x_ref, gi_ref, bi_ref, att_ref, y_ref, wglu_ref, bglu_ref, wout_ref, g1_ref, b1_ref,
                wr_ref, br_ref, h1_ref, idx_ref, gate_ref, rank_ref, cnt_ref, carry_ref):
    step = pl.program_id(0)

    @pl.when(step == 0)
    def _():
        carry_ref[...] = jnp.zeros(carry_ref.shape, F32)

    h0 = _layer_norm(x_ref[...], gi_ref[...], bi_ref[...])
    y = y_ref[...].astype(F32)
    y = y * (0.5 * (1.0 + jnp.tanh(math.sqrt(2.0 / math.pi) * (y + 0.044715 * (y * y * y)))))
    y = y * jax.nn.sigmoid(_dot(y.astype(BF16), wglu_ref[...]) + bglu_ref[...])
    mix = _dot(att_ref[...], wout_ref[0:ATT_W, :]) + _dot(y.astype(BF16), wout_ref[ATT_W:, :])
    h1 = _layer_norm(DEEPNORM_ALPHA * h0 + mix, g1_ref[...], b1_ref[...])
    h1_ref[...] = h1

    logits = _dot_nt(wr_ref[...], h1, precision=lax.Precision.HIGHEST) + br_ref[...]
    tm = logits.shape[1]
    eidx = lax.broadcasted_iota(I32, logits.shape, 0)
    vals, hots = [], []
    rest = logits
    for _ in range(TOP_K):
        mx = jnp.max(rest, axis=0, keepdims=True)
        first = jnp.min(jnp.where(rest == mx, eidx, N_EXPERTS), axis=0, keepdims=True)
        hot = eidx == first
        vals.append(mx)
        hots.append(hot)
        rest = jnp.where(hot, -jnp.inf, rest)
    exps = [jnp.exp(v - vals[0]) for v in vals]
    denom = exps[0] + exps[1] + exps[2] + exps[3]
    gate_ref[...] = jnp.concatenate([e / denom for e in exps], axis=0)
    idx_ref[...] = jnp.concatenate(
        [jnp.sum(jnp.where(h, eidx, 0), axis=0, keepdims=True) for h in hots], axis=0)

    hot_all = (hots[0] | hots[1] | hots[2] | hots[3]).astype(F32)
    tri = (lax.broadcasted_iota(I32, (tm, tm), 0) < lax.broadcasted_iota(I32, (tm, tm), 1)).astype(BF16)
    before = _dot(hot_all.astype(BF16), tri) + carry_ref[...]
    rank_ref[...] = jnp.concatenate(
        [jnp.sum(jnp.where(h, before, 0.0), axis=0, keepdims=True) for h in hots], axis=0).astype(I32)
    carry_ref[...] = carry_ref[...] + jnp.sum(hot_all, axis=1, keepdims=True)
    cnt_ref[...] = jnp.broadcast_to(carry_ref[...], cnt_ref.shape)


def _mix(x2, gi, bi, att, y, wglu, bglu, wout, g1, b1, wr_t, br):
    t, d = x2.shape
    tm = ROW_TILE
    row = lambda i: (i, 0)
    col = lambda i: (0, i)
    const = lambda i: (0, 0)
    return pl.pallas_call(
        _mix_kernel,
        grid=(t // tm,),
        in_specs=[
            pl.BlockSpec((tm, d), row), pl.BlockSpec((1, d), const), pl.BlockSpec((1, d), const),
            pl.BlockSpec((tm, ATT_W), row), pl.BlockSpec((tm, SSM_W), row),
            pl.BlockSpec((SSM_W, SSM_W), const), pl.BlockSpec((1, SSM_W), const),
            pl.BlockSpec((d, d), const), pl.BlockSpec((1, d), const), pl.BlockSpec((1, d), const),
            pl.BlockSpec((N_EXPERTS, d), const), pl.BlockSpec((N_EXPERTS, 1), const),
        ],
        out_specs=[
            pl.BlockSpec((tm, d), row),
            pl.BlockSpec((TOP_K, tm), col), pl.BlockSpec((TOP_K, tm), col), pl.BlockSpec((TOP_K, tm), col),
            pl.BlockSpec((N_EXPERTS, 128), const),
        ],
        out_shape=[
            jax.ShapeDtypeStruct((t, d), F32),
            jax.ShapeDtypeStruct((TOP_K, t), I32), jax.ShapeDtypeStruct((TOP_K, t), F32),
            jax.ShapeDtypeStruct((TOP_K, t), I32),
            jax.ShapeDtypeStruct((N_EXPERTS, 128), F32),
        ],
        scratch_shapes=[pltpu.VMEM((N_EXPERTS, 1), F32)],
        compiler_params=pltpu.CompilerParams(dimension_semantics=("arbitrary",), vmem_limit_bytes=VMEM_LIMIT),
        name="mix_ln1_router",
    )(x2, gi, bi, att, y, wglu, bglu, wout, g1, b1, wr_t, br)


def _row_copy(src, src_row, dst, dst_row, sem):
    return pltpu.make_async_copy(src.at[pl.ds(src_row, 1), :], dst.at[pl.ds(dst_row, 1), :], sem)


def _dispatch_kernel(pad_start_ref, pad_count_ref, n_used_ref, dest_ref, h1_ref, xs_hbm, zero_ref, sem, zsem):
    step = pl.program_id(0)
    tm = dest_ref.shape[1]
    n_blocks = xs_hbm.shape[0] // MOE_BLOCK

    def zero_block(blk):
        return pltpu.make_async_copy(zero_ref, xs_hbm.at[pl.ds(blk * MOE_BLOCK, MOE_BLOCK), :], zsem)

    @pl.when(step == 0)
    def _():
        zero_ref[...] = jnp.zeros(zero_ref.shape, F32)
        for e in range(N_EXPERTS):
            start = pad_start_ref[e]
            count = pad_count_ref[e]

            def fill(r, c):
                _row_copy(zero_ref, 0, xs_hbm, start + r, zsem).start()
                return c

            def drain(r, c):
                _row_copy(zero_ref, 0, xs_hbm, start, zsem).wait()
                return c

            lax.fori_loop(0, count, fill, 0)
            lax.fori_loop(0, count, drain, 0)

        def fill_block(blk, c):
            zero_block(blk).start()
            return c

        def drain_block(blk, c):
            zero_block(blk).wait()
            return c

        lax.fori_loop(n_used_ref[0], n_blocks, fill_block, 0)
        lax.fori_loop(n_used_ref[0], n_blocks, drain_block, 0)

    def issue(r, c):
        for k in range(TOP_K):
            _row_copy(h1_ref, r, xs_hbm, dest_ref[k, r], sem).start()
        return c

    lax.fori_loop(0, tm, issue, 0, unroll=8)
    for _ in range(TOP_K):
        pltpu.make_async_copy(h1_ref, xs_hbm.at[pl.ds(0, tm), :], sem).wait()


def _dispatch(pad_start, pad_count, n_used, dest_t, h1, n_rows):
    t, d = h1.shape
    tm = ROW_TILE
    grid_spec = pltpu.PrefetchScalarGridSpec(
        num_scalar_prefetch=3,
        grid=(t // tm,),
        in_specs=[
            pl.BlockSpec((TOP_K, tm), lambda i, ps, pc, nu: (0, i), memory_space=pltpu.SMEM),
            pl.BlockSpec((tm, d), lambda i, ps, pc, nu: (i, 0)),
        ],
        out_specs=pl.BlockSpec(memory_space=pl.ANY),
        scratch_shapes=[pltpu.VMEM((MOE_BLOCK, d), F32), pltpu.SemaphoreType.DMA, pltpu.SemaphoreType.DMA],
    )
    return pl.pallas_call(
        _dispatch_kernel,
        grid_spec=grid_spec,
        out_shape=jax.ShapeDtypeStruct((n_rows, d), F32),
        compiler_params=pltpu.CompilerParams(dimension_semantics=("arbitrary",), has_side_effects=True),
        name="moe_dispatch",
    )(pad_start, pad_count, n_used, dest_t, h1)


def _expert_kernel(be_ref, nb_ref, x_ref, wg_ref, bg_ref, wu_ref, bu_ref, wd_ref, bd_ref, y_ref,
                   wg_b, wu_b, wd_b):
    i = pl.program_id(0)

    @pl.when(i < nb_ref[0])
    def _():
        @pl.when((i == 0) | (be_ref[i] != be_ref[jnp.maximum(i - 1, 0)]))
        def _():
            wg_b[...] = wg_ref[0].astype(BF16)
            wu_b[...] = wu_ref[0].astype(BF16)
            wd_b[...] = wd_ref[0].astype(BF16)

        x = x_ref[...].astype(BF16)
        gate = _dot(x, wg_b[...]) + bg_ref[0]
        up = _dot(x, wu_b[...]) + bu_ref[0]
        gate = jnp.minimum(gate, SWIGLU_LIMIT)
        up = jnp.clip(up, -SWIGLU_LIMIT, SWIGLU_LIMIT)
        act = (up + 1.0) * gate * jax.nn.sigmoid(gate * SWIGLU_ALPHA)
        y_ref[...] = _dot(act.astype(BF16), wd_b[...]) + bd_ref[0]

    @pl.when(i >= nb_ref[0])
    def _():
        y_ref[...] = jnp.zeros(y_ref.shape, F32)


def _experts(block_expert, n_used, xs, wg, bg, wu, bu, wd, bd):
    n_rows, d = xs.shape
    dff = wg.shape[2]
    nb = n_rows // MOE_BLOCK
    blk = lambda i, be, nu: (jnp.minimum(i, nu[0] - 1), 0)
    wsel = lambda i, be, nu: (be[jnp.minimum(i, nu[0] - 1)], 0, 0)
    grid_spec = pltpu.PrefetchScalarGridSpec(
        num_scalar_prefetch=2,
        grid=(nb,),
        in_specs=[
            pl.BlockSpec((MOE_BLOCK, d), blk),
            pl.BlockSpec((1, d, dff), wsel), pl.BlockSpec((1, 1, dff), wsel),
            pl.BlockSpec((1, d, dff), wsel), pl.BlockSpec((1, 1, dff), wsel),
            pl.BlockSpec((1, dff, d), wsel), pl.BlockSpec((1, 1, d), wsel),
        ],
        out_specs=pl.BlockSpec((MOE_BLOCK, d), lambda i, be, nu: (i, 0)),
        scratch_shapes=[pltpu.VMEM((d, dff), BF16), pltpu.VMEM((d, dff), BF16), pltpu.VMEM((dff, d), BF16)],
    )
    return pl.pallas_call(
        _expert_kernel,
        grid_spec=grid_spec,
        out_shape=jax.ShapeDtypeStruct((n_rows, d), F32),
        compiler_params=pltpu.CompilerParams(dimension_semantics=("arbitrary",), vmem_limit_bytes=VMEM_LIMIT),
        name="moe_experts",
    )(block_expert, n_used, xs, wg, bg, wu, bu, wd, bd)


def _combine_kernel(dest_ref, dest_next_ref, h1_ref, gate_ref, ys_hbm, g2_ref, b2_ref, o_ref, buf, sems):
    step = pl.program_id(0)
    n_steps = pl.num_programs(0)
    tm = h1_ref.shape[0]
    slot = step % 2

    def issue(dref, s):
        def body(r, c):
            for k in range(TOP_K):
                pltpu.make_async_copy(ys_hbm.at[pl.ds(dref[k, r], 1), :], buf.at[s, k, pl.ds(r, 1), :],
                                      sems.at[s]).start()
            return c
        lax.fori_loop(0, tm, body, 0, unroll=8)

    @pl.when(step == 0)
    def _():
        issue(dest_ref, 0)

    @pl.when(step + 1 < n_steps)
    def _():
        issue(dest_next_ref, 1 - slot)

    for k in range(TOP_K):
        pltpu.make_async_copy(ys_hbm.at[pl.ds(0, tm), :], buf.at[slot, k], sems.at[slot]).wait()

    gate = gate_ref[...]
    ffn = buf[slot, 0] * gate[:, 0:1]
    for k in range(1, TOP_K):
        ffn = ffn + buf[slot, k] * gate[:, k:k + 1]
    o_ref[...] = _layer_norm(DEEPNORM_ALPHA * h1_ref[...] + ffn, g2_ref[...], b2_ref[...])


def _combine(dest_t, h1, gates, ys, g2, b2):
    t, d = h1.shape
    tm = ROW_TILE
    n_steps = t // tm
    row = lambda i: (i, 0)
    const = lambda i: (0, 0)
    return pl.pallas_call(
        _combine_kernel,
        grid=(n_steps,),
        in_specs=[
            pl.BlockSpec((TOP_K, tm), lambda i: (0, i), memory_space=pltpu.SMEM),
            pl.BlockSpec((TOP_K, tm), lambda i: (0, jnp.minimum(i + 1, n_steps - 1)), memory_space=pltpu.SMEM),
            pl.BlockSpec((tm, d), row), pl.BlockSpec((tm, TOP_K), row),
            pl.BlockSpec(memory_space=pl.ANY),
            pl.BlockSpec((1, d), const), pl.BlockSpec((1, d), const),
        ],
        out_specs=pl.BlockSpec((tm, d), row),
        out_shape=jax.ShapeDtypeStruct((t, d), F32),
        scratch_shapes=[pltpu.VMEM((2, TOP_K, tm, d), F32), pltpu.SemaphoreType.DMA((2,))],
        compiler_params=pltpu.CompilerParams(dimension_semantics=("arbitrary",), vmem_limit_bytes=VMEM_LIMIT),
        name="moe_combine_ln2",
    )(dest_t, dest_t, h1, gates, ys, g2, b2)


def kernel(x, meta_tokens, ln_in_g, ln_in_b, rel_bias, w_in, lambda_q1, lambda_k1, lambda_q2, lambda_k2,
           subln_g, a_re, a_im, log_step, b_re, b_im, c_re, c_im, d_skip, w_glu, b_glu, w_out, ln1_g, ln1_b,
           w_router, b_router, w_gate, b_gate, w_up, b_up, w_down, b_down, ln2_g, ln2_b):
    bsz, seq, dm = x.shape
    assert seq % 512 == 0 and w_in.shape[0] == DEPTH == 1
    layer = 0
    row2 = lambda v: v.astype(F32).reshape(1, -1)

    w_in_b = w_in[layer].astype(BF16)
    gi, bi = row2(ln_in_g), row2(ln_in_b)
    q, k, vt, u = _inproj(x, gi, bi, w_in_b, tm=512, kv_tile=KV_TILE)
    meta = jnp.zeros((1, META_PAD, dm), x.dtype).at[0, :N_META].set(meta_tokens.astype(x.dtype))
    _, k_meta, vt_meta, u_meta = _inproj(meta, gi, bi, w_in_b, tm=META_PAD, kv_tile=META_PAD)

    lam_init = 0.8 - 0.6 * math.exp(-0.3 * layer)
    lam = (jnp.exp(jnp.sum(lambda_q1[layer].astype(F32) * lambda_k1[layer].astype(F32)))
           - jnp.exp(jnp.sum(lambda_q2[layer].astype(F32) * lambda_k2[layer].astype(F32))) + lam_init)
    att = _attention(lam.reshape(1), q, k, vt, k_meta[0], vt_meta[0, :, 0], _near_bias(rel_bias),
                     subln_g[layer].astype(F32).reshape(HEAD_W, 1), lam_init=lam_init)

    kpad, pt, qt, a1, a2 = _ssm_tables(a_re[layer], a_im[layer], log_step[layer], b_re[layer], b_im[layer],
                                     c_re[layer], c_im[layer], d_skip[layer])
    n_chunks = -(-(seq // SSM_T + 1) // 8) * 8
    lead = jnp.zeros((bsz, SSM_T, SSM_W), BF16).at[:, SSM_T - N_META:].set(u_meta[:, :N_META])
    tail = jnp.zeros((bsz, n_chunks * SSM_T - SSM_T - seq, SSM_W), BF16)
    ug = jnp.concatenate([lead, u, tail], axis=1).reshape(bsz, n_chunks, SSM_T, SSM_G, SSM_CG)
    ug = jnp.transpose(ug, (3, 1, 0, 2, 4)).reshape(SSM_G, n_chunks * bsz, SSM_T * SSM_CG)
    yg = _ssm(ug, kpad, pt, qt, a1, a2, n_chunks=n_chunks, bsz=bsz)
    yg = jnp.transpose(yg.reshape(SSM_G, n_chunks, bsz, SSM_T, SSM_CG), (2, 1, 3, 0, 4))
    y_ssm = yg.reshape(bsz, n_chunks * SSM_T, SSM_W)[:, SSM_T:SSM_T + seq]

    t = bsz * seq
    h1, idx_t, gate_t, rank_t, cnt = _mix(
        x.reshape(t, dm), gi, bi, att.reshape(t, ATT_W), y_ssm.reshape(t, SSM_W),
        w_glu[layer].astype(BF16), row2(b_glu[layer]), w_out[layer].astype(BF16),
        row2(ln1_g[layer]), row2(ln1_b[layer]),
        jnp.transpose(w_router[layer].astype(F32)), b_router[layer].astype(F32).reshape(N_EXPERTS, 1))

    counts = cnt[:, 0].astype(I32)
    padded = (counts + MOE_BLOCK - 1) // MOE_BLOCK * MOE_BLOCK
    experts = jnp.arange(N_EXPERTS, dtype=I32)
    padded_end = jnp.sum(jnp.where(experts[:, None] <= experts[None, :], padded[:, None], 0), axis=0)
    padded_start = padded_end - padded
    dest_t = jnp.sum(jnp.where(idx_t[..., None] == experts, padded_start, 0), axis=-1) + rank_t
    n_blocks = t * TOP_K // MOE_BLOCK + N_EXPERTS
    block_row0 = jnp.arange(n_blocks, dtype=I32) * MOE_BLOCK
    block_expert = jnp.minimum(jnp.sum((padded_end[None, :] <= block_row0[:, None]).astype(I32), axis=1),
                               N_EXPERTS - 1)
    n_used = (padded_end[-1:] // MOE_BLOCK).astype(I32)

    xs = _dispatch((padded_start + counts).astype(I32), (padded - counts).astype(I32), n_used, dest_t, h1,
                   n_blocks * MOE_BLOCK)
    b3 = lambda v: v.astype(F32)[:, None, :]
    ys = _experts(block_expert, n_used, xs, w_gate[layer], b3(b_gate[layer]), w_up[layer], b3(b_up[layer]),
                  w_down[layer], b3(b_down[layer]))
    out = _combine(dest_t, h1, jnp.transpose(gate_t), ys, row2(ln2_g[layer]), row2(ln2_b[layer]))
    return out.reshape(bsz, seq, dm)
```

```python
import functools
import math

import jax
import jax.numpy as jnp
import numpy as np
from jax import lax
from jax.experimental import pallas as pl
from jax.experimental.pallas import tpu as pltpu

F32 = jnp.float32
BF16 = jnp.bfloat16
I32 = jnp.int32

DEPTH = 1
N_META = 16
CHUNK = 64
ATT_HEADS = 4
HEAD_DIM = 64
HEAD_W = 2 * HEAD_DIM
ATT_W = ATT_HEADS * HEAD_W
SSM_W = 512
SSM_CG = 16
SSM_G = SSM_W // SSM_CG
SSM_N = 64
N_BUCKETS = 32
MAX_DISTANCE = 128
N_EXPERTS = 32
TOP_K = 4
SWIGLU_LIMIT = 7.0
SWIGLU_ALPHA = 1.702
LN_EPS = 1e-5
NEG_INF = -1e30
DEEPNORM_ALPHA = (2.0 * DEPTH) ** 0.25
LOG2E = 1.4426950408889634

Q_TILE = 512
KV_TILE = 512
V_ONES = 16
V_ROWS = HEAD_W + V_ONES
META_PAD = 128
SSM_T = 64
MIX_TILE = 512
ROW_TILE = 256
MOE_BLOCK = 512
VMEM_LIMIT = 56 * 1024 * 1024


def _layer_norm(x, g, b):
    mu = jnp.mean(x, axis=-1, keepdims=True)
    xc = x - mu
    var = jnp.mean(xc * xc, axis=-1, keepdims=True)
    return xc * lax.rsqrt(var + LN_EPS) * g + b


def _dot(a, b):
    return jnp.dot(a, b, preferred_element_type=F32)


def _dot_nt(a, b, precision=None):
    return lax.dot_general(a, b, (((1,), (1,)), ((), ())), precision=precision,
                           preferred_element_type=F32)


def _inproj_kernel(x_ref, g_ref, b_ref, w_ref, q_ref, k_ref, vt_ref, u_ref, *, kv_tile, q_scale):
    h = _layer_norm(x_ref[0], g_ref[...], b_ref[...]).astype(BF16)
    tm = h.shape[0]
    q_ref[0] = (_dot(h, w_ref[:, 0:ATT_W]) * q_scale).astype(BF16)
    k_ref[0] = _dot(h, w_ref[:, ATT_W:2 * ATT_W]).astype(BF16)
    v = _dot(h, w_ref[:, 2 * ATT_W:3 * ATT_W])
    ones = jnp.ones((V_ONES, kv_tile), BF16)
    for hh in range(ATT_HEADS):
        vt = v[:, hh * HEAD_W:(hh + 1) * HEAD_W].T.astype(BF16)
        for j in range(tm // kv_tile):
            vt_ref[0, hh, j, 0:HEAD_W, :] = vt[:, j * kv_tile:(j + 1) * kv_tile]
            vt_ref[0, hh, j, HEAD_W:V_ROWS, :] = ones
    u_ref[0] = _dot(h, w_ref[:, 3 * ATT_W:]).astype(BF16)


def _inproj(x, g, b, w, *, tm, kv_tile):
    bsz, s, d = x.shape
    q_scale = HEAD_DIM ** -0.5 * LOG2E
    n_cols = w.shape[1]
    row = lambda bi, i: (bi, i, 0)
    return pl.pallas_call(
        functools.partial(_inproj_kernel, kv_tile=kv_tile, q_scale=q_scale),
        grid=(bsz, s // tm),
        in_specs=[
            pl.BlockSpec((1, tm, d), row),
            pl.BlockSpec((1, d), lambda bi, i: (0, 0)),
            pl.BlockSpec((1, d), lambda bi, i: (0, 0)),
            pl.BlockSpec((d, n_cols), lambda bi, i: (0, 0)),
        ],
        out_specs=[
            pl.BlockSpec((1, tm, ATT_W), row),
            pl.BlockSpec((1, tm, ATT_W), row),
            pl.BlockSpec((1, ATT_HEADS, tm // kv_tile, V_ROWS, kv_tile), lambda bi, i: (bi, 0, i, 0, 0)),
            pl.BlockSpec((1, tm, SSM_W), row),
        ],
        out_shape=[
            jax.ShapeDtypeStruct((bsz, s, ATT_W), BF16),
            jax.ShapeDtypeStruct((bsz, s, ATT_W), BF16),
            jax.ShapeDtypeStruct((bsz, ATT_HEADS, s // kv_tile, V_ROWS, kv_tile), BF16),
            jax.ShapeDtypeStruct((bsz, s, SSM_W), BF16),
        ],
        compiler_params=pltpu.CompilerParams(
            dimension_semantics=("parallel", "parallel"), vmem_limit_bytes=VMEM_LIMIT),
        name="inproj",
    )(x, g, b, w)


def _attn_kernel(lam_ref, q_ref, k_ref, vt_ref, km_ref, vtm_ref, bias_ref, g_ref, o_ref,
                 m_ref, acc_ref, qz_ref, sa_ref, sb_ref, sn_ref, *, out_scale):
    i = pl.program_id(2)
    q = q_ref[0]
    lane = lax.broadcasted_iota(I32, q.shape, 1)
    zero = jnp.zeros_like(q)
    qz_ref[0:Q_TILE, :] = jnp.where(lane < HEAD_DIM, q, zero)
    qz_ref[Q_TILE:, :] = jnp.where(lane >= HEAD_DIM, q, zero)

    m_ref[...] = jnp.full(m_ref.shape, NEG_INF, F32)
    acc_ref[...] = jnp.zeros(acc_ref.shape, F32)

    def scores(kt):
        return _dot_nt(kt, qz_ref[...])

    def k_tile(t):
        return k_ref[0, pl.ds(pl.multiple_of(t * KV_TILE, KV_TILE), KV_TILE), :]

    def absorb(s, pv):
        m_old = m_ref[...]
        m_new = jnp.maximum(m_old, jnp.max(s, axis=0, keepdims=True))
        alpha = jnp.exp2(m_old - m_new)
        p = jnp.exp2(s - m_new).astype(BF16)
        acc_ref[...] = acc_ref[...] * alpha + pv(p)
        m_ref[...] = m_new

    def tile_pv(t):
        return lambda p: _dot(vt_ref[0, 0, t], p)

    n_far = jnp.maximum(i - 1, 0)
    peel = n_far % 2

    @pl.when(peel == 1)
    def _():
        absorb(scores(k_tile(0)), tile_pv(0))

    sa_ref[...] = scores(k_tile(peel))

    def far_pair(j, carry):
        t0 = peel + 2 * j
        sb_ref[...] = scores(k_tile(t0 + 1))
        absorb(sa_ref[...], tile_pv(t0))
        sa_ref[...] = scores(k_tile(t0 + 2))
        absorb(sb_ref[...], tile_pv(t0 + 1))
        return carry

    lax.fori_loop(0, n_far // 2, far_pair, 0)

    sn_ref[0:KV_TILE, :] = scores(k_tile(i))
    sn_ref[KV_TILE:, :] = scores(km_ref[...])
    absorb(sa_ref[...] + bias_ref[0, 0, 0:KV_TILE, :], tile_pv(n_far))
    absorb(sn_ref[...] + bias_ref[0, 0, KV_TILE:, :],
           lambda p: _dot(vt_ref[0, 0, i], p[0:KV_TILE]) + _dot(vtm_ref[0], p[KV_TILE:]))

    acc = acc_ref[...]
    lam = lam_ref[0]
    o1 = acc[0:HEAD_W, 0:Q_TILE] / acc[HEAD_W:HEAD_W + 1, 0:Q_TILE]
    o2 = acc[0:HEAD_W, Q_TILE:] / acc[HEAD_W:HEAD_W + 1, Q_TILE:]
    o = o1 - lam * o2
    ms = jnp.mean(o * o, axis=0, keepdims=True)
    o = o * lax.rsqrt(ms + LN_EPS) * g_ref[...] * out_scale
    o_ref[0] = o.T.astype(o_ref.dtype)


def _attention(lam, q, k, vt, k_meta, vt_meta, bias, subln_g, *, lam_init):
    bsz, s, _ = q.shape
    nq = s // Q_TILE
    n_near = META_PAD + 2 * KV_TILE
    grid_spec = pltpu.PrefetchScalarGridSpec(
        num_scalar_prefetch=1,
        grid=(bsz, ATT_HEADS, nq),
        in_specs=[
            pl.BlockSpec((1, Q_TILE, HEAD_W), lambda b, h, i, lam: (b, i, h)),
            pl.BlockSpec((1, s, HEAD_W), lambda b, h, i, lam: (b, 0, h)),
            pl.BlockSpec((1, 1, s // KV_TILE, V_ROWS, KV_TILE), lambda b, h, i, lam: (b, h, 0, 0, 0)),
            pl.BlockSpec((META_PAD, HEAD_W), lambda b, h, i, lam: (0, h)),
            pl.BlockSpec((1, V_ROWS, META_PAD), lambda b, h, i, lam: (h, 0, 0)),
            pl.BlockSpec((1, 1, n_near, 2 * Q_TILE), lambda b, h, i, lam: (h, jnp.minimum(i, 1), 0, 0)),
            pl.BlockSpec((HEAD_W, 1), lambda b, h, i, lam: (0, 0)),
        ],
        out_specs=pl.BlockSpec((1, Q_TILE, HEAD_W), lambda b, h, i, lam: (b, i, h)),
        scratch_shapes=[
            pltpu.VMEM((1, 2 * Q_TILE), F32), pltpu.VMEM((V_ROWS, 2 * Q_TILE), F32),
            pltpu.VMEM((2 * Q_TILE, HEAD_W), BF16),
            pltpu.VMEM((KV_TILE, 2 * Q_TILE), F32), pltpu.VMEM((KV_TILE, 2 * Q_TILE), F32),
            pltpu.VMEM((KV_TILE + META_PAD, 2 * Q_TILE), F32),
        ],
    )
    return pl.pallas_call(
        functools.partial(_attn_kernel, out_scale=1.0 - lam_init),
        grid_spec=grid_spec,
        out_shape=jax.ShapeDtypeStruct((bsz, s, ATT_W), BF16),
        compiler_params=pltpu.CompilerParams(
            dimension_semantics=("parallel", "parallel", "arbitrary"), vmem_limit_bytes=VMEM_LIMIT),
        name="diff_attention",
    )(lam, q, k, vt, k_meta, vt_meta, bias, subln_g)


def _t5_bucket(rel):
    nb = N_BUCKETS // 2
    max_exact = nb // 2
    ret = jnp.where(rel > 0, nb, 0)
    n = jnp.abs(rel)
    n_f = jnp.maximum(n, 1).astype(F32)
    large = max_exact + (jnp.log(n_f / max_exact) / math.log(MAX_DISTANCE / max_exact)
                         * (nb - max_exact)).astype(I32)
    large = jnp.minimum(large, nb - 1)
    return ret + jnp.where(n < max_exact, n, large)


def _near_bias(rel_bias):
    table = rel_bias.astype(F32)
    far = table[N_BUCKETS // 2 - 1]
    c = jnp.arange(Q_TILE, dtype=I32)[None, :]
    r = jnp.arange(KV_TILE, dtype=I32)[:, None]

    span = 2 * Q_TILE + KV_TILE
    width = 2 * span
    rels = jnp.arange(-span, span, dtype=I32)
    hot = (_t5_bucket(rels)[:, None] == jnp.arange(N_BUCKETS, dtype=I32)).astype(F32)
    by_rel = jnp.einsum('wn,nh->hw', hot, table - far, precision=lax.Precision.HIGHEST) * LOG2E
    skew = jnp.broadcast_to(by_rel[:, None, :], (ATT_HEADS, Q_TILE, width)).reshape(ATT_HEADS, Q_TILE * width)
    skew = skew[:, :Q_TILE * (width - 1)].reshape(ATT_HEADS, Q_TILE, width - 1)

    def bias_of(offset, rows):
        return jnp.transpose(skew[:, :, span + offset:span + offset + rows], (0, 2, 1))

    own = jnp.where((r // CHUNK <= c // CHUNK)[None], bias_of(0, KV_TILE), NEG_INF)
    prev = bias_of(-KV_TILE, KV_TILE)
    rm = jnp.arange(META_PAD, dtype=I32)[:, None]
    meta_ok = (rm < N_META)[None]
    meta0 = jnp.where(meta_ok, bias_of(-N_META, META_PAD), NEG_INF)
    meta1 = jnp.where(meta_ok, bias_of(-N_META - Q_TILE, META_PAD), NEG_INF)
    v0 = jnp.concatenate([jnp.full_like(prev, NEG_INF), own, meta0], axis=1)
    v1 = jnp.concatenate([prev, own, meta1], axis=1)
    both = jnp.stack([v0, v1], axis=1)
    return jnp.concatenate([both, both], axis=-1)


def _ssm_tables(a_re, a_im, log_step, b_re, b_im, c_re, c_im, d_skip):
    hi = lax.Precision.HIGHEST
    t_len = SSM_T
    step = jnp.exp(log_step.astype(F32))[:, None]
    ar = jnp.minimum(a_re.astype(F32), -1e-4)
    ai = a_im.astype(F32)
    mag = jnp.exp(step * ar)
    ph = step * ai
    abar_re = mag * jnp.cos(ph)
    abar_im = mag * jnp.sin(ph)
    den = ar * ar + ai * ai
    e_re = abar_re - 1.0
    e_im = abar_im
    f_re = (e_re * ar + e_im * ai) / den
    f_im = (e_im * ar - e_re * ai) / den
    br = b_re.astype(F32)
    bi = b_im.astype(F32)
    bb_re = f_re[..., None] * br - f_im[..., None] * bi
    bb_im = f_re[..., None] * bi + f_im[..., None] * br
    tau = jnp.arange(t_len + 1, dtype=F32)[None, :, None]
    pmag = jnp.exp(tau * (step * ar)[:, None, :])
    pph = tau * ph[:, None, :]
    pw_re = pmag * jnp.cos(pph)
    pw_im = pmag * jnp.sin(pph)
    w_re = pw_re[..., None] * bb_re[:, None] - pw_im[..., None] * bb_im[:, None]
    w_im = pw_re[..., None] * bb_im[:, None] + pw_im[..., None] * bb_re[:, None]
    cr = c_re.astype(F32)
    ci = c_im.astype(F32)
    kern = (jnp.einsum('gcn,gtnk->gtck', cr, w_re[:, :t_len], precision=hi)
            - jnp.einsum('gcn,gtnk->gtck', ci, w_im[:, :t_len], precision=hi))
    skip = d_skip.astype(F32).reshape(SSM_G, SSM_CG)
    kern = kern.at[:, 0].add(skip[:, :, None] * jnp.eye(SSM_CG, dtype=F32))
    kflat = jnp.transpose(kern, (0, 3, 1, 2)).reshape(SSM_G, SSM_CG, t_len * SSM_CG)
    kpad = jnp.concatenate([jnp.zeros_like(kflat), kflat], axis=-1)
    pt_re = jnp.transpose(jnp.flip(w_re[:, :t_len], axis=1), (0, 1, 3, 2)).reshape(SSM_G, t_len * SSM_CG, SSM_N)
    pt_im = jnp.transpose(jnp.flip(w_im[:, :t_len], axis=1), (0, 1, 3, 2)).reshape(SSM_G, t_len * SSM_CG, SSM_N)
    pt = jnp.concatenate([pt_re, pt_im], axis=-1)
    up_re = pw_re[:, 1:]
    up_im = pw_im[:, 1:]
    q_re = cr[:, None] * up_re[:, :, None, :] - ci[:, None] * up_im[:, :, None, :]
    q_im = cr[:, None] * up_im[:, :, None, :] + ci[:, None] * up_re[:, :, None, :]
    qt = jnp.concatenate([jnp.transpose(q_re, (0, 3, 1, 2)), -jnp.transpose(q_im, (0, 3, 1, 2))], axis=1)
    qt = qt.reshape(SSM_G, 2 * SSM_N, t_len * SSM_CG)
    at_re = pw_re[:, t_len]
    at_im = pw_im[:, t_len]
    a1 = jnp.concatenate([at_re, at_re], axis=-1)
    a2 = jnp.concatenate([-at_im, at_im], axis=-1)
    return kpad, pt.astype(BF16), qt.astype(BF16), a1, a2


def _ssm_state_kernel(u_ref, pt_ref, s_ref):
    s_ref[0] = _dot(u_ref[0], pt_ref[0])


def _ssm_scan_kernel(s_ref, a1_ref, a2_ref, x_ref, *, n_chunks, bsz):
    a1 = a1_ref[...][:, None, :]
    a2 = a2_ref[...][:, None, :]

    def body(c, x):
        rows = pl.ds(c * bsz, bsz)
        x_ref[:, rows, :] = x
        return a1 * x + a2 * pltpu.roll(x, SSM_N, 2) + s_ref[:, rows, :]

    lax.fori_loop(0, n_chunks, body, jnp.zeros((SSM_G, bsz, 2 * SSM_N), F32))


def _ssm_out_kernel(u_ref, x_ref, kp_ref, qt_ref, y_ref, mt_ref):
    kp = kp_ref[0]
    width = SSM_T * SSM_CG
    lane_tile = 128
    for sub in range(0, lane_tile, SSM_CG):
        shifted = kp if sub == 0 else pltpu.roll(kp, 2 * width - sub, 1)
        for s in range(SSM_T):
            off = (SSM_T - s) * SSM_CG
            if off % lane_tile == sub:
                base = off - sub
                mt_ref[s * SSM_CG:(s + 1) * SSM_CG, :] = shifted[:, base:base + width].astype(BF16)
    x = x_ref[0]
    x_hi = x.astype(BF16)
    x_lo = (x - x_hi.astype(F32)).astype(BF16)
    y = _dot(u_ref[0], mt_ref[...]) + _dot(x_hi, qt_ref[0]) + _dot(x_lo, qt_ref[0])
    y_ref[0] = y.astype(y_ref.dtype)


def _ssm(ug, kpad, pt, qt, a1, a2, *, n_chunks, bsz):
    g, r, w = ug.shape
    n2 = 2 * SSM_N
    per_g = lambda gi: (gi, 0, 0)
    params = pltpu.CompilerParams(dimension_semantics=("parallel",), vmem_limit_bytes=VMEM_LIMIT)
    s = pl.pallas_call(
        _ssm_state_kernel,
        grid=(g,),
        in_specs=[pl.BlockSpec((1, r, w), per_g), pl.BlockSpec((1, w, n2), per_g)],
        out_specs=pl.BlockSpec((1, r, n2), per_g),
        out_shape=jax.ShapeDtypeStruct((g, r, n2), F32),
        compiler_params=params,
        name="ssm_chunk_state",
    )(ug, pt)
    x = pl.pallas_call(
        functools.partial(_ssm_scan_kernel, n_chunks=n_chunks, bsz=bsz),
        out_shape=jax.ShapeDtypeStruct((g, r, n2), F32),
        compiler_params=pltpu.CompilerParams(vmem_limit_bytes=VMEM_LIMIT),
        name="ssm_chunk_scan",
    )(s, a1, a2)
    return pl.pallas_call(
        _ssm_out_kernel,
        grid=(g,),
        in_specs=[pl.BlockSpec((1, r, w), per_g), pl.BlockSpec((1, r, n2), per_g),
                  pl.BlockSpec((1, SSM_CG, 2 * w), per_g), pl.BlockSpec((1, n2, w), per_g)],
        out_specs=pl.BlockSpec((1, r, w), per_g),
        out_shape=jax.ShapeDtypeStruct((g, r, w), BF16),
        scratch_shapes=[pltpu.VMEM((w, w), BF16)],
        compiler_params=params,
        name="ssm_output",
    )(ug, x, kpad, qt)


def _mix_kernel(x_ref, gi_ref, bi_ref, att_ref, y_ref, wglu_ref, bglu_ref, wout_ref, g1_ref, b1_ref,
                wr_ref, br_ref, h1_ref, idx_ref, gate_ref, rank_ref, cnt_ref, carry_ref):
    step = pl.program_id(0)

    @pl.when(step == 0)
    def _():
        carry_ref[...] = jnp.zeros(carry_ref.shape, F32)

    h0 = _layer_norm(x_ref[...], gi_ref[...], bi_ref[...])
    y = y_ref[...].astype(F32)
    y = y * (0.5 * (1.0 + jnp.tanh(math.sqrt(2.0 / math.pi) * (y + 0.044715 * (y * y * y)))))
    y = y * jax.nn.sigmoid(_dot(y.astype(BF16), wglu_ref[...]) + bglu_ref[...])
    mix = _dot(att_ref[...], wout_ref[0:ATT_W, :]) + _dot(y.astype(BF16), wout_ref[ATT_W:, :])
    h1 = _layer_norm(DEEPNORM_ALPHA * h0 + mix, g1_ref[...], b1_ref[...])
    h1_ref[...] = h1

    logits = _dot_nt(wr_ref[...], h1, precision=lax.Precision.HIGHEST) + br_ref[...]
    tm = logits.shape[1]
    eidx = lax.broadcasted_iota(I32, logits.shape, 0)
    vals, hots = [], []
    rest = logits
    for _ in range(TOP_K):
        mx = jnp.max(rest, axis=0, keepdims=True)
        first = jnp.min(jnp.where(rest == mx, eidx, N_EXPERTS), axis=0, keepdims=True)
        hot = eidx == first
        vals.append(mx)
        hots.append(hot)
        rest = jnp.where(hot, -jnp.inf, rest)
    exps = [jnp.exp(v - vals[0]) for v in vals]
    denom = exps[0] + exps[1] + exps[2] + exps[3]
    gate_ref[...] = jnp.concatenate([e / denom for e in exps], axis=0)
    idx_ref[...] = jnp.concatenate(
        [jnp.sum(jnp.where(h, eidx, 0), axis=0, keepdims=True) for h in hots], axis=0)

    hot_all = (hots[0] | hots[1] | hots[2] | hots[3]).astype(F32)
    tri = (lax.broadcasted_iota(I32, (tm, tm), 0) < lax.broadcasted_iota(I32, (tm, tm), 1)).astype(BF16)
    before = _dot(hot_all.astype(BF16), tri) + carry_ref[...]
    rank_ref[...] = jnp.concatenate(
        [jnp.sum(jnp.where(h, before, 0.0), axis=0, keepdims=True) for h in hots], axis=0).astype(I32)
    carry_ref[...] = carry_ref[...] + jnp.sum(hot_all, axis=1, keepdims=True)
    cnt_ref[...] = jnp.broadcast_to(carry_ref[...], cnt_ref.shape)


def _mix(x2, gi, bi, att, y, wglu, bglu, wout, g1, b1, wr_t, br):
    t, d = x2.shape
    tm = MIX_TILE
    row = lambda i: (i, 0)
    col = lambda i: (0, i)
    const = lambda i: (0, 0)
    return pl.pallas_call(
        _mix_kernel,
        grid=(t // tm,),
        in_specs=[
            pl.BlockSpec((tm, d), row), pl.BlockSpec((1, d), const), pl.BlockSpec((1, d), const),
            pl.BlockSpec((tm, ATT_W), row), pl.BlockSpec((tm, SSM_W), row),
            pl.BlockSpec((SSM_W, SSM_W), const), pl.BlockSpec((1, SSM_W), const),
            pl.BlockSpec((d, d), const), pl.BlockSpec((1, d), const), pl.BlockSpec((1, d), const),
            pl.BlockSpec((N_EXPERTS, d), const), pl.BlockSpec((N_EXPERTS, 1), const),
        ],
        out_specs=[
            pl.BlockSpec((tm, d), row),
            pl.BlockSpec((TOP_K, tm), col), pl.BlockSpec((TOP_K, tm), col), pl.BlockSpec((TOP_K, tm), col),
            pl.BlockSpec((N_EXPERTS, 128), const),
        ],
        out_shape=[
            jax.ShapeDtypeStruct((t, d), F32),
            jax.ShapeDtypeStruct((TOP_K, t), I32), jax.ShapeDtypeStruct((TOP_K, t), F32),
            jax.ShapeDtypeStruct((TOP_K, t), I32),
            jax.ShapeDtypeStruct((N_EXPERTS, 128), F32),
        ],
        scratch_shapes=[pltpu.VMEM((N_EXPERTS, 1), F32)],
        compiler_params=pltpu.CompilerParams(dimension_semantics=("arbitrary",), vmem_limit_bytes=VMEM_LIMIT),
        name="mix_ln1_router",
    )(x2, gi, bi, att, y, wglu, bglu, wout, g1, b1, wr_t, br)


def _row_copy(src, src_row, dst, dst_row, sem):
    return pltpu.make_async_copy(src.at[pl.ds(src_row, 1), :], dst.at[pl.ds(dst_row, 1), :], sem)


def _dispatch_kernel(pad_start_ref, pad_count_ref, n_used_ref, dest_ref, h1_ref, xs_hbm, zero_ref, sem, zsem):
    step = pl.program_id(0)
    tm = dest_ref.shape[1]
    n_blocks = xs_hbm.shape[0] // MOE_BLOCK

    def zero_block(blk):
        return pltpu.make_async_copy(zero_ref, xs_hbm.at[pl.ds(blk * MOE_BLOCK, MOE_BLOCK), :], zsem)

    @pl.when(step == 0)
    def _():
        zero_ref[...] = jnp.zeros(zero_ref.shape, F32)
        for e in range(N_EXPERTS):
            start = pad_start_ref[e]
            count = pad_count_ref[e]

            def fill(r, c):
                _row_copy(zero_ref, 0, xs_hbm, start + r, zsem).start()
                return c

            def drain(r, c):
                _row_copy(zero_ref, 0, xs_hbm, start, zsem).wait()
                return c

            lax.fori_loop(0, count, fill, 0)
            lax.fori_loop(0, count, drain, 0)

        def fill_block(blk, c):
            zero_block(blk).start()
            return c

        def drain_block(blk, c):
            zero_block(blk).wait()
            return c

        lax.fori_loop(n_used_ref[0], n_blocks, fill_block, 0)
        lax.fori_loop(n_used_ref[0], n_blocks, drain_block, 0)

    def issue(r, c):
        for k in range(TOP_K):
            _row_copy(h1_ref, r, xs_hbm, dest_ref[k, r], sem).start()
        return c

    lax.fori_loop(0, tm, issue, 0, unroll=8)
    for _ in range(TOP_K):
        pltpu.make_async_copy(h1_ref, xs_hbm.at[pl.ds(0, tm), :], sem).wait()


def _dispatch(pad_start, pad_count, n_used, dest_t, h1, n_rows):
    t, d = h1.shape
    tm = ROW_TILE
    grid_spec = pltpu.PrefetchScalarGridSpec(
        num_scalar_prefetch=3,
        grid=(t // tm,),
        in_specs=[
            pl.BlockSpec((TOP_K, tm), lambda i, ps, pc, nu: (0, i), memory_space=pltpu.SMEM),
            pl.BlockSpec((tm, d), lambda i, ps, pc, nu: (i, 0)),
        ],
        out_specs=pl.BlockSpec(memory_space=pl.ANY),
        scratch_shapes=[pltpu.VMEM((MOE_BLOCK, d), F32), pltpu.SemaphoreType.DMA, pltpu.SemaphoreType.DMA],
    )
    return pl.pallas_call(
        _dispatch_kernel,
        grid_spec=grid_spec,
        out_shape=jax.ShapeDtypeStruct((n_rows, d), F32),
        compiler_params=pltpu.CompilerParams(dimension_semantics=("arbitrary",), has_side_effects=True),
        name="moe_dispatch",
    )(pad_start, pad_count, n_used, dest_t, h1)


def _expert_kernel(be_ref, nb_ref, x_ref, wg_ref, bg_ref, wu_ref, bu_ref, wd_ref, bd_ref, y_ref,
                   wg_b, wu_b, wd_b):
    i = pl.program_id(0)

    @pl.when(i < nb_ref[0])
    def _():
        @pl.when((i == 0) | (be_ref[i] != be_ref[jnp.maximum(i - 1, 0)]))
        def _():
            wg_b[...] = wg_ref[0].astype(BF16)
            wu_b[...] = wu_ref[0].astype(BF16)
            wd_b[...] = wd_ref[0].astype(BF16)

        x = x_ref[...].astype(BF16)
        gate = _dot(x, wg_b[...]) + bg_ref[0]
        up = _dot(x, wu_b[...]) + bu_ref[0]
        gate = jnp.minimum(gate, SWIGLU_LIMIT)
        up = jnp.clip(up, -SWIGLU_LIMIT, SWIGLU_LIMIT)
        act = (up + 1.0) * gate * jax.nn.sigmoid(gate * SWIGLU_ALPHA)
        y_ref[...] = _dot(act.astype(BF16), wd_b[...]) + bd_ref[0]

    @pl.when(i >= nb_ref[0])
    def _():
        y_ref[...] = jnp.zeros(y_ref.shape, F32)


def _experts(block_expert, n_used, xs, wg, bg, wu, bu, wd, bd):
    n_rows, d = xs.shape
    dff = wg.shape[2]
    nb = n_rows // MOE_BLOCK
    blk = lambda i, be, nu: (jnp.minimum(i, nu[0] - 1), 0)
    wsel = lambda i, be, nu: (be[jnp.minimum(i, nu[0] - 1)], 0, 0)
    grid_spec = pltpu.PrefetchScalarGridSpec(
        num_scalar_prefetch=2,
        grid=(nb,),
        in_specs=[
            pl.BlockSpec((MOE_BLOCK, d), blk),
            pl.BlockSpec((1, d, dff), wsel), pl.BlockSpec((1, 1, dff), wsel),
            pl.BlockSpec((1, d, dff), wsel), pl.BlockSpec((1, 1, dff), wsel),
            pl.BlockSpec((1, dff, d), wsel), pl.BlockSpec((1, 1, d), wsel),
        ],
        out_specs=pl.BlockSpec((MOE_BLOCK, d), lambda i, be, nu: (i, 0)),
        scratch_shapes=[pltpu.VMEM((d, dff), BF16), pltpu.VMEM((d, dff), BF16), pltpu.VMEM((dff, d), BF16)],
    )
    return pl.pallas_call(
        _expert_kernel,
        grid_spec=grid_spec,
        out_shape=jax.ShapeDtypeStruct((n_rows, d), F32),
        compiler_params=pltpu.CompilerParams(dimension_semantics=("arbitrary",), vmem_limit_bytes=VMEM_LIMIT),
        name="moe_experts",
    )(block_expert, n_used, xs, wg, bg, wu, bu, wd, bd)


def _combine_kernel(dest_ref, dest_next_ref, h1_ref, gate_ref, ys_hbm, g2_ref, b2_ref, o_ref, buf, sems):
    step = pl.program_id(0)
    n_steps = pl.num_programs(0)
    tm = h1_ref.shape[0]
    slot = step % 2

    def issue(dref, s):
        def body(r, c):
            for k in range(TOP_K):
                pltpu.make_async_copy(ys_hbm.at[pl.ds(dref[k, r], 1), :], buf.at[s, k, pl.ds(r, 1), :],
                                      sems.at[s]).start()
            return c
        lax.fori_loop(0, tm, body, 0, unroll=8)

    @pl.when(step == 0)
    def _():
        issue(dest_ref, 0)

    @pl.when(step + 1 < n_steps)
    def _():
        issue(dest_next_ref, 1 - slot)

    for k in range(TOP_K):
        pltpu.make_async_copy(ys_hbm.at[pl.ds(0, tm), :], buf.at[slot, k], sems.at[slot]).wait()

    gate = gate_ref[...]
    ffn = buf[slot, 0] * gate[:, 0:1]
    for k in range(1, TOP_K):
        ffn = ffn + buf[slot, k] * gate[:, k:k + 1]
    o_ref[...] = _layer_norm(DEEPNORM_ALPHA * h1_ref[...] + ffn, g2_ref[...], b2_ref[...])


def _combine(dest_t, h1, gates, ys, g2, b2):
    t, d = h1.shape
    tm = ROW_TILE
    n_steps = t // tm
    row = lambda i: (i, 0)
    const = lambda i: (0, 0)
    return pl.pallas_call(
        _combine_kernel,
        grid=(n_steps,),
        in_specs=[
            pl.BlockSpec((TOP_K, tm), lambda i: (0, i), memory_space=pltpu.SMEM),
            pl.BlockSpec((TOP_K, tm), lambda i: (0, jnp.minimum(i + 1, n_steps - 1)), memory_space=pltpu.SMEM),
            pl.BlockSpec((tm, d), row), pl.BlockSpec((tm, TOP_K), row),
            pl.BlockSpec(memory_space=pl.ANY),
            pl.BlockSpec((1, d), const), pl.BlockSpec((1, d), const),
        ],
        out_specs=pl.BlockSpec((tm, d), row),
        out_shape=jax.ShapeDtypeStruct((t, d), F32),
        scratch_shapes=[pltpu.VMEM((2, TOP_K, tm, d), F32), pltpu.SemaphoreType.DMA((2,))],
        compiler_params=pltpu.CompilerParams(dimension_semantics=("arbitrary",), vmem_limit_bytes=VMEM_LIMIT),
        name="moe_combine_ln2",
    )(dest_t, dest_t, h1, gates, ys, g2, b2)


def kernel(x, meta_tokens, ln_in_g, ln_in_b, rel_bias, w_in, lambda_q1, lambda_k1, lambda_q2, lambda_k2,
           subln_g, a_re, a_im, log_step, b_re, b_im, c_re, c_im, d_skip, w_glu, b_glu, w_out, ln1_g, ln1_b,
           w_router, b_router, w_gate, b_gate, w_up, b_up, w_down, b_down, ln2_g, ln2_b):
    bsz, seq, dm = x.shape
    assert seq % 512 == 0 and w_in.shape[0] == DEPTH == 1
    layer = 0
    row2 = lambda v: v.astype(F32).reshape(1, -1)

    w_in_b = w_in[layer].astype(BF16)
    gi, bi = row2(ln_in_g), row2(ln_in_b)
    q, k, vt, u = _inproj(x, gi, bi, w_in_b, tm=512, kv_tile=KV_TILE)
    meta = jnp.zeros((1, META_PAD, dm), x.dtype).at[0, :N_META].set(meta_tokens.astype(x.dtype))
    _, k_meta, vt_meta, u_meta = _inproj(meta, gi, bi, w_in_b, tm=META_PAD, kv_tile=META_PAD)

    lam_init = 0.8 - 0.6 * math.exp(-0.3 * layer)
    lam = (jnp.exp(jnp.sum(lambda_q1[layer].astype(F32) * lambda_k1[layer].astype(F32)))
           - jnp.exp(jnp.sum(lambda_q2[layer].astype(F32) * lambda_k2[layer].astype(F32))) + lam_init)
    att = _attention(lam.reshape(1), q, k, vt, k_meta[0], vt_meta[0, :, 0], _near_bias(rel_bias),
                     subln_g[layer].astype(F32).reshape(HEAD_W, 1), lam_init=lam_init)

    kpad, pt, qt, a1, a2 = _ssm_tables(a_re[layer], a_im[layer], log_step[layer], b_re[layer], b_im[layer],
                                     c_re[layer], c_im[layer], d_skip[layer])
    n_chunks = -(-(seq // SSM_T + 1) // 8) * 8
    lead = jnp.zeros((bsz, SSM_T, SSM_W), BF16).at[:, SSM_T - N_META:].set(u_meta[:, :N_META])
    tail = jnp.zeros((bsz, n_chunks * SSM_T - SSM_T - seq, SSM_W), BF16)
    ug = jnp.concatenate([lead, u, tail], axis=1).reshape(bsz, n_chunks, SSM_T, SSM_G, SSM_CG)
    ug = jnp.transpose(ug, (3, 1, 0, 2, 4)).reshape(SSM_G, n_chunks * bsz, SSM_T * SSM_CG)
    yg = _ssm(ug, kpad, pt, qt, a1, a2, n_chunks=n_chunks, bsz=bsz)
    yg = jnp.transpose(yg.reshape(SSM_G, n_chunks, bsz, SSM_T, SSM_CG), (2, 1, 3, 0, 4))
    y_ssm = yg.reshape(bsz, n_chunks * SSM_T, SSM_W)[:, SSM_T:SSM_T + seq]

    t = bsz * seq
    h1, idx_t, gate_t, rank_t, cnt = _mix(
        x.reshape(t, dm), gi, bi, att.reshape(t, ATT_W), y_ssm.reshape(t, SSM_W),
        w_glu[layer].astype(BF16), row2(b_glu[layer]), w_out[layer].astype(BF16),
        row2(ln1_g[layer]), row2(ln1_b[layer]),
        jnp.transpose(w_router[layer].astype(F32)), b_router[layer].astype(F32).reshape(N_EXPERTS, 1))

    counts = cnt[:, 0].astype(I32)
    padded = (counts + MOE_BLOCK - 1) // MOE_BLOCK * MOE_BLOCK
    experts = jnp.arange(N_EXPERTS, dtype=I32)
    padded_end = jnp.sum(jnp.where(experts[:, None] <= experts[None, :], padded[:, None], 0), axis=0)
    padded_start = padded_end - padded
    dest_t = jnp.sum(jnp.where(idx_t[..., None] == experts, padded_start, 0), axis=-1) + rank_t
    n_blocks = t * TOP_K // MOE_BLOCK + N_EXPERTS
    block_row0 = jnp.arange(n_blocks, dtype=I32) * MOE_BLOCK
    block_expert = jnp.minimum(jnp.sum((padded_end[None, :] <= block_row0[:, None]).astype(I32), axis=1),
                               N_EXPERTS - 1)
    n_used = (padded_end[-1:] // MOE_BLOCK).astype(I32)

    xs = _dispatch((padded_start + counts).astype(I32), (padded - counts).astype(I32), n_used, dest_t, h1,
                   n_blocks * MOE_BLOCK)
    b3 = lambda v: v.astype(F32)[:, None, :]
    ys = _experts(block_expert, n_used, xs, w_gate[layer], b3(b_gate[layer]), w_up[layer], b3(b_up[layer]),
                  w_down[layer], b3(b_down[layer]))
    out = _combine(dest_t, h1, jnp.transpose(gate_t), ys, row2(ln2_g[layer]), row2(ln2_b[layer]))
    return out.reshape(bsz, seq, dm)
```

```python
import functools
import math

import jax
import jax.numpy as jnp
import numpy as np
from jax import lax
from jax.experimental import pallas as pl
from jax.experimental.pallas import tpu as pltpu

F32 = jnp.float32
BF16 = jnp.bfloat16
I32 = jnp.int32

DEPTH = 1
N_META = 16
CHUNK = 64
ATT_HEADS = 4
HEAD_DIM = 64
HEAD_W = 2 * HEAD_DIM
ATT_W = ATT_HEADS * HEAD_W
SSM_W = 512
SSM_CG = 16
SSM_G = SSM_W // SSM_CG
SSM_N = 64
N_BUCKETS = 32
MAX_DISTANCE = 128
N_EXPERTS = 32
TOP_K = 4
SWIGLU_LIMIT = 7.0
SWIGLU_ALPHA = 1.702
LN_EPS = 1e-5
NEG_INF = -1e30
DEEPNORM_ALPHA = (2.0 * DEPTH) ** 0.25
LOG2E = 1.4426950408889634

Q_TILE = 512
KV_TILE = 512
V_ONES = 16
V_ROWS = HEAD_W + V_ONES
META_PAD = 128
SSM_T = 64
MIX_TILE = 512
ROW_TILE = 256
MOE_BLOCK = 512
VMEM_LIMIT = 56 * 1024 * 1024


def _layer_norm(x, g, b):
    mu = jnp.mean(x, axis=-1, keepdims=True)
    xc = x - mu
    var = jnp.mean(xc * xc, axis=-1, keepdims=True)
    return xc * lax.rsqrt(var + LN_EPS) * g + b


def _dot(a, b):
    return jnp.dot(a, b, preferred_element_type=F32)


def _dot_nt(a, b, precision=None):
    return lax.dot_general(a, b, (((1,), (1,)), ((), ())), precision=precision,
                           preferred_element_type=F32)


def _inproj_kernel(x_ref, g_ref, b_ref, w_ref, q_ref, k_ref, vt_ref, u_ref, *, kv_tile, q_scale):
    h = _layer_norm(x_ref[0], g_ref[...], b_ref[...]).astype(BF16)
    tm = h.shape[0]
    q_ref[0] = (_dot(h, w_ref[:, 0:ATT_W]) * q_scale).astype(BF16)
    k_ref[0] = _dot(h, w_ref[:, ATT_W:2 * ATT_W]).astype(BF16)
    v = _dot(h, w_ref[:, 2 * ATT_W:3 * ATT_W])
    ones = jnp.ones((V_ONES, kv_tile), BF16)
    for hh in range(ATT_HEADS):
        vt = v[:, hh * HEAD_W:(hh + 1) * HEAD_W].T.astype(BF16)
        for j in range(tm // kv_tile):
            vt_ref[0, hh, j, 0:HEAD_W, :] = vt[:, j * kv_tile:(j + 1) * kv_tile]
            vt_ref[0, hh, j, HEAD_W:V_ROWS, :] = ones
    u_ref[0] = _dot(h, w_ref[:, 3 * ATT_W:]).astype(BF16)


def _inproj(x, g, b, w, *, tm, kv_tile):
    bsz, s, d = x.shape
    q_scale = HEAD_DIM ** -0.5 * LOG2E
    n_cols = w.shape[1]
    row = lambda bi, i: (bi, i, 0)
    return pl.pallas_call(
        functools.partial(_inproj_kernel, kv_tile=kv_tile, q_scale=q_scale),
        grid=(bsz, s // tm),
        in_specs=[
            pl.BlockSpec((1, tm, d), row),
            pl.BlockSpec((1, d), lambda bi, i: (0, 0)),
            pl.BlockSpec((1, d), lambda bi, i: (0, 0)),
            pl.BlockSpec((d, n_cols), lambda bi, i: (0, 0)),
        ],
        out_specs=[
            pl.BlockSpec((1, tm, ATT_W), row),
            pl.BlockSpec((1, tm, ATT_W), row),
            pl.BlockSpec((1, ATT_HEADS, tm // kv_tile, V_ROWS, kv_tile), lambda bi, i: (bi, 0, i, 0, 0)),
            pl.BlockSpec((1, tm, SSM_W), row),
        ],
        out_shape=[
            jax.ShapeDtypeStruct((bsz, s, ATT_W), BF16),
            jax.ShapeDtypeStruct((bsz, s, ATT_W), BF16),
            jax.ShapeDtypeStruct((bsz, ATT_HEADS, s // kv_tile, V_ROWS, kv_tile), BF16),
            jax.ShapeDtypeStruct((bsz, s, SSM_W), BF16),
        ],
        compiler_params=pltpu.CompilerParams(
            dimension_semantics=("parallel", "parallel"), vmem_limit_bytes=VMEM_LIMIT),
        name="inproj",
    )(x, g, b, w)


def _attn_kernel(lam_ref, q_ref, k_ref, vt_ref, km_ref, vtm_ref, bias_ref, g_ref, o_ref,
                 m_ref, acc_ref, qz_ref, sa_ref, sb_ref, sn_ref, *, out_scale):
    i = pl.program_id(2)
    q = q_ref[0]
    lane = lax.broadcasted_iota(I32, q.shape, 1)
    zero = jnp.zeros_like(q)
    qz_ref[0:Q_TILE, :] = jnp.where(lane < HEAD_DIM, q, zero)
    qz_ref[Q_TILE:, :] = jnp.where(lane >= HEAD_DIM, q, zero)

    m_ref[...] = jnp.full(m_ref.shape, NEG_INF, F32)
    acc_ref[...] = jnp.zeros(acc_ref.shape, F32)

    def scores(kt):
        return _dot_nt(kt, qz_ref[...])

    def k_tile(t):
        return k_ref[0, pl.ds(pl.multiple_of(t * KV_TILE, KV_TILE), KV_TILE), :]

    def absorb(s, pv):
        m_old = m_ref[...]
        m_new = jnp.maximum(m_old, jnp.max(s, axis=0, keepdims=True))
        alpha = jnp.exp2(m_old - m_new)
        p = jnp.exp2(s - m_new).astype(BF16)
        acc_ref[...] = acc_ref[...] * alpha + pv(p)
        m_ref[...] = m_new

    def tile_pv(t):
        return lambda p: _dot(vt_ref[0, 0, t], p)

    n_far = jnp.maximum(i - 1, 0)
    peel = n_far % 2

    @pl.when(peel == 1)
    def _():
        absorb(scores(k_tile(0)), tile_pv(0))

    sa_ref[...] = scores(k_tile(peel))

    def far_pair(j, carry):
        t0 = peel + 2 * j
        sb_ref[...] = scores(k_tile(t0 + 1))
        absorb(sa_ref[...], tile_pv(t0))
        sa_ref[...] = scores(k_tile(t0 + 2))
        absorb(sb_ref[...], tile_pv(t0 + 1))
        return carry

    lax.fori_loop(0, n_far // 2, far_pair, 0)

    sb_ref[...] = scores(k_tile(i))
    absorb(sa_ref[...] + bias_ref[0, 0, 0:KV_TILE, :], tile_pv(n_far))
    sn_ref[...] = scores(km_ref[...])
    absorb(sb_ref[...] + bias_ref[0, 0, KV_TILE:2 * KV_TILE, :], tile_pv(i))
    absorb(sn_ref[...] + bias_ref[0, 0, 2 * KV_TILE:, :], lambda p: _dot(vtm_ref[0], p))

    acc = acc_ref[...]
    lam = lam_ref[0]
    o1 = acc[0:HEAD_W, 0:Q_TILE] / acc[HEAD_W:HEAD_W + 1, 0:Q_TILE]
    o2 = acc[0:HEAD_W, Q_TILE:] / acc[HEAD_W:HEAD_W + 1, Q_TILE:]
    o = o1 - lam * o2
    ms = jnp.mean(o * o, axis=0, keepdims=True)
    o = o * lax.rsqrt(ms + LN_EPS) * g_ref[...] * out_scale
    o_ref[0] = o.T.astype(o_ref.dtype)


def _attention(lam, q, k, vt, k_meta, vt_meta, bias, subln_g, *, lam_init):
    bsz, s, _ = q.shape
    nq = s // Q_TILE
    n_near = META_PAD + 2 * KV_TILE
    grid_spec = pltpu.PrefetchScalarGridSpec(
        num_scalar_prefetch=1,
        grid=(bsz, ATT_HEADS, nq),
        in_specs=[
            pl.BlockSpec((1, Q_TILE, HEAD_W), lambda b, h, i, lam: (b, i, h)),
            pl.BlockSpec((1, s, HEAD_W), lambda b, h, i, lam: (b, 0, h)),
            pl.BlockSpec((1, 1, s // KV_TILE, V_ROWS, KV_TILE), lambda b, h, i, lam: (b, h, 0, 0, 0)),
            pl.BlockSpec((META_PAD, HEAD_W), lambda b, h, i, lam: (0, h)),
            pl.BlockSpec((1, V_ROWS, META_PAD), lambda b, h, i, lam: (h, 0, 0)),
            pl.BlockSpec((1, 1, n_near, 2 * Q_TILE), lambda b, h, i, lam: (h, jnp.minimum(i, 1), 0, 0)),
            pl.BlockSpec((HEAD_W, 1), lambda b, h, i, lam: (0, 0)),
        ],
        out_specs=pl.BlockSpec((1, Q_TILE, HEAD_W), lambda b, h, i, lam: (b, i, h)),
        scratch_shapes=[
            pltpu.VMEM((1, 2 * Q_TILE), F32), pltpu.VMEM((V_ROWS, 2 * Q_TILE), F32),
            pltpu.VMEM((2 * Q_TILE, HEAD_W), BF16),
            pltpu.VMEM((KV_TILE, 2 * Q_TILE), F32), pltpu.VMEM((KV_TILE, 2 * Q_TILE), F32),
            pltpu.VMEM((META_PAD, 2 * Q_TILE), F32),
        ],
    )
    return pl.pallas_call(
        functools.partial(_attn_kernel, out_scale=1.0 - lam_init),
        grid_spec=grid_spec,
        out_shape=jax.ShapeDtypeStruct((bsz, s, ATT_W), BF16),
        compiler_params=pltpu.CompilerParams(
            dimension_semantics=("parallel", "parallel", "arbitrary"), vmem_limit_bytes=VMEM_LIMIT),
        name="diff_attention",
    )(lam, q, k, vt, k_meta, vt_meta, bias, subln_g)


def _t5_bucket(rel):
    nb = N_BUCKETS // 2
    max_exact = nb // 2
    ret = jnp.where(rel > 0, nb, 0)
    n = jnp.abs(rel)
    n_f = jnp.maximum(n, 1).astype(F32)
    large = max_exact + (jnp.log(n_f / max_exact) / math.log(MAX_DISTANCE / max_exact)
                         * (nb - max_exact)).astype(I32)
    large = jnp.minimum(large, nb - 1)
    return ret + jnp.where(n < max_exact, n, large)


def _near_bias(rel_bias):
    table = rel_bias.astype(F32)
    far = table[N_BUCKETS // 2 - 1]
    c = jnp.arange(Q_TILE, dtype=I32)[None, :]
    r = jnp.arange(KV_TILE, dtype=I32)[:, None]

    shifted = jnp.transpose(table - far) * LOG2E

    def bias_of(offset, rows):
        rel = jnp.arange(rows, dtype=I32)[:, None] - c + offset
        hot = _t5_bucket(rel)[None, :, :, None] == jnp.arange(N_BUCKETS, dtype=I32)
        return jnp.sum(jnp.where(hot, shifted[:, None, None, :], 0.0), axis=-1)

    own = jnp.where((r // CHUNK <= c // CHUNK)[None], bias_of(0, KV_TILE), NEG_INF)
    prev = bias_of(-KV_TILE, KV_TILE)
    rm = jnp.arange(META_PAD, dtype=I32)[:, None]
    meta_ok = (rm < N_META)[None]
    meta0 = jnp.where(meta_ok, bias_of(-N_META, META_PAD), NEG_INF)
    meta1 = jnp.where(meta_ok, bias_of(-N_META - Q_TILE, META_PAD), NEG_INF)
    v0 = jnp.concatenate([jnp.full_like(prev, NEG_INF), own, meta0], axis=1)
    v1 = jnp.concatenate([prev, own, meta1], axis=1)
    both = jnp.stack([v0, v1], axis=1)
    return jnp.concatenate([both, both], axis=-1)


def _ssm_tables(a_re, a_im, log_step, b_re, b_im, c_re, c_im, d_skip):
    hi = lax.Precision.HIGHEST
    t_len = SSM_T
    step = jnp.exp(log_step.astype(F32))[:, None]
    ar = jnp.minimum(a_re.astype(F32), -1e-4)
    ai = a_im.astype(F32)
    mag = jnp.exp(step * ar)
    ph = step * ai
    abar_re = mag * jnp.cos(ph)
    abar_im = mag * jnp.sin(ph)
    den = ar * ar + ai * ai
    e_re = abar_re - 1.0
    e_im = abar_im
    f_re = (e_re * ar + e_im * ai) / den
    f_im = (e_im * ar - e_re * ai) / den
    br = b_re.astype(F32)
    bi = b_im.astype(F32)
    bb_re = f_re[..., None] * br - f_im[..., None] * bi
    bb_im = f_re[..., None] * bi + f_im[..., None] * br
    tau = jnp.arange(t_len + 1, dtype=F32)[None, :, None]
    pmag = jnp.exp(tau * (step * ar)[:, None, :])
    pph = tau * ph[:, None, :]
    pw_re = pmag * jnp.cos(pph)
    pw_im = pmag * jnp.sin(pph)
    w_re = pw_re[..., None] * bb_re[:, None] - pw_im[..., None] * bb_im[:, None]
    w_im = pw_re[..., None] * bb_im[:, None] + pw_im[..., None] * bb_re[:, None]
    cr = c_re.astype(F32)
    ci = c_im.astype(F32)
    kern = (jnp.einsum('gcn,gtnk->gtck', cr, w_re[:, :t_len], precision=hi)
            - jnp.einsum('gcn,gtnk->gtck', ci, w_im[:, :t_len], precision=hi))
    skip = d_skip.astype(F32).reshape(SSM_G, SSM_CG)
    kern = kern.at[:, 0].add(skip[:, :, None] * jnp.eye(SSM_CG, dtype=F32))
    kflat = jnp.transpose(kern, (0, 3, 1, 2)).reshape(SSM_G, SSM_CG, t_len * SSM_CG)
    kpad = jnp.concatenate([jnp.zeros_like(kflat), kflat], axis=-1)
    pt_re = jnp.transpose(jnp.flip(w_re[:, :t_len], axis=1), (0, 1, 3, 2)).reshape(SSM_G, t_len * SSM_CG, SSM_N)
    pt_im = jnp.transpose(jnp.flip(w_im[:, :t_len], axis=1), (0, 1, 3, 2)).reshape(SSM_G, t_len * SSM_CG, SSM_N)
    pt = jnp.concatenate([pt_re, pt_im], axis=-1)
    up_re = pw_re[:, 1:]
    up_im = pw_im[:, 1:]
    q_re = cr[:, None] * up_re[:, :, None, :] - ci[:, None] * up_im[:, :, None, :]
    q_im = cr[:, None] * up_im[:, :, None, :] + ci[:, None] * up_re[:, :, None, :]
    qt = jnp.concatenate([jnp.transpose(q_re, (0, 3, 1, 2)), -jnp.transpose(q_im, (0, 3, 1, 2))], axis=1)
    qt = qt.reshape(SSM_G, 2 * SSM_N, t_len * SSM_CG)
    at_re = pw_re[:, t_len]
    at_im = pw_im[:, t_len]
    a1 = jnp.concatenate([at_re, at_re], axis=-1)
    a2 = jnp.concatenate([-at_im, at_im], axis=-1)
    return kpad, pt.astype(BF16), qt.astype(BF16), a1, a2


def _ssm_state_kernel(u_ref, pt_ref, s_ref):
    s_ref[0] = _dot(u_ref[0], pt_ref[0])


def _ssm_scan_kernel(s_ref, a1_ref, a2_ref, x_ref, *, n_chunks, bsz):
    a1 = a1_ref[...][:, None, :]
    a2 = a2_ref[...][:, None, :]

    def body(c, x):
        rows = pl.ds(c * bsz, bsz)
        x_ref[:, rows, :] = x
        return a1 * x + a2 * pltpu.roll(x, SSM_N, 2) + s_ref[:, rows, :]

    lax.fori_loop(0, n_chunks, body, jnp.zeros((SSM_G, bsz, 2 * SSM_N), F32))


def _ssm_out_kernel(u_ref, x_ref, kp_ref, qt_ref, y_ref, mt_ref):
    kp = kp_ref[0]
    width = SSM_T * SSM_CG
    lane_tile = 128
    for sub in range(0, lane_tile, SSM_CG):
        shifted = kp if sub == 0 else pltpu.roll(kp, 2 * width - sub, 1)
        for s in range(SSM_T):
            off = (SSM_T - s) * SSM_CG
            if off % lane_tile == sub:
                base = off - sub
                mt_ref[s * SSM_CG:(s + 1) * SSM_CG, :] = shifted[:, base:base + width].astype(BF16)
    x = x_ref[0]
    x_hi = x.astype(BF16)
    x_lo = (x - x_hi.astype(F32)).astype(BF16)
    y = _dot(u_ref[0], mt_ref[...]) + _dot(x_hi, qt_ref[0]) + _dot(x_lo, qt_ref[0])
    y_ref[0] = y.astype(y_ref.dtype)


def _ssm(ug, kpad, pt, qt, a1, a2, *, n_chunks, bsz):
    g, r, w = ug.shape
    n2 = 2 * SSM_N
    per_g = lambda gi: (gi, 0, 0)
    params = pltpu.CompilerParams(dimension_semantics=("parallel",), vmem_limit_bytes=VMEM_LIMIT)
    s = pl.pallas_call(
        _ssm_state_kernel,
        grid=(g,),
        in_specs=[pl.BlockSpec((1, r, w), per_g), pl.BlockSpec((1, w, n2), per_g)],
        out_specs=pl.BlockSpec((1, r, n2), per_g),
        out_shape=jax.ShapeDtypeStruct((g, r, n2), F32),
        compiler_params=params,
        name="ssm_chunk_state",
    )(ug, pt)
    x = pl.pallas_call(
        functools.partial(_ssm_scan_kernel, n_chunks=n_chunks, bsz=bsz),
        out_shape=jax.ShapeDtypeStruct((g, r, n2), F32),
        compiler_params=pltpu.CompilerParams(vmem_limit_bytes=VMEM_LIMIT),
        name="ssm_chunk_scan",
    )(s, a1, a2)
    return pl.pallas_call(
        _ssm_out_kernel,
        grid=(g,),
        in_specs=[pl.BlockSpec((1, r, w), per_g), pl.BlockSpec((1, r, n2), per_g),
                  pl.BlockSpec((1, SSM_CG, 2 * w), per_g), pl.BlockSpec((1, n2, w), per_g)],
        out_specs=pl.BlockSpec((1, r, w), per_g),
        out_shape=jax.ShapeDtypeStruct((g, r, w), BF16),
        scratch_shapes=[pltpu.VMEM((w, w), BF16)],
        compiler_params=params,
        name="ssm_output",
    )(ug, x, kpad, qt)


def _mix_kernel(x_ref, gi_ref, bi_ref, att_ref, y_ref, wglu_ref, bglu_ref, wout_ref, g1_ref, b1_ref,
                wr_ref, br_ref, h1_ref, idx_ref, gate_ref, rank_ref, cnt_ref, carry_ref):
    step = pl.program_id(0)

    @pl.when(step == 0)
    def _():
        carry_ref[...] = jnp.zeros(carry_ref.shape, F32)

    h0 = _layer_norm(x_ref[...], gi_ref[...], bi_ref[...])
    y = y_ref[...].astype(F32)
    y = y * (0.5 * (1.0 + jnp.tanh(math.sqrt(2.0 / math.pi) * (y + 0.044715 * (y * y * y)))))
    y = y * jax.nn.sigmoid(_dot(y.astype(BF16), wglu_ref[...]) + bglu_ref[...])
    mix = _dot(att_ref[...], wout_ref[0:ATT_W, :]) + _dot(y.astype(BF16), wout_ref[ATT_W:, :])
    h1 = _layer_norm(DEEPNORM_ALPHA * h0 + mix, g1_ref[...], b1_ref[...])
    h1_ref[...] = h1

    logits = _dot_nt(wr_ref[...], h1, precision=lax.Precision.HIGHEST) + br_ref[...]
    tm = logits.shape[1]
    eidx = lax.broadcasted_iota(I32, logits.shape, 0)
    vals, hots = [], []
    rest = logits
    for _ in range(TOP_K):
        mx = jnp.max(rest, axis=0, keepdims=True)
        first = jnp.min(jnp.where(rest == mx, eidx, N_EXPERTS), axis=0, keepdims=True)
        hot = eidx == first
        vals.append(mx)
        hots.append(hot)
        rest = jnp.where(hot, -jnp.inf, rest)
    exps = [jnp.exp(v - vals[0]) for v in vals]
    denom = exps[0] + exps[1] + exps[2] + exps[3]
    gate_ref[...] = jnp.concatenate([e / denom for e in exps], axis=0)
    idx_ref[...] = jnp.concatenate(
        [jnp.sum(jnp.where(h, eidx, 0), axis=0, keepdims=True) for h in hots], axis=0)

    hot_all = (hots[0] | hots[1] | hots[2] | hots[3]).astype(F32)
    tri = (lax.broadcasted_iota(I32, (tm, tm), 0) < lax.broadcasted_iota(I32, (tm, tm), 1)).astype(BF16)
    before = _dot(hot_all.astype(BF16), tri) + carry_ref[...]
    rank_ref[...] = jnp.concatenate(
        [jnp.sum(jnp.where(h, before, 0.0), axis=0, keepdims=True) for h in hots], axis=0).astype(I32)
    carry_ref[...] = carry_ref[...] + jnp.sum(hot_all, axis=1, keepdims=True)
    cnt_ref[...] = jnp.broadcast_to(carry_ref[...], cnt_ref.shape)


def _mix(x2, gi, bi, att, y, wglu, bglu, wout, g1, b1, wr_t, br):
    t, d = x2.shape
    tm = MIX_TILE
    row = lambda i: (i, 0)
    col = lambda i: (0, i)
    const = lambda i: (0, 0)
    return pl.pallas_call(
        _mix_kernel,
        grid=(t // tm,),
        in_specs=[
            pl.BlockSpec((tm, d), row), pl.BlockSpec((1, d), const), pl.BlockSpec((1, d), const),
            pl.BlockSpec((tm, ATT_W), row), pl.BlockSpec((tm, SSM_W), row),
            pl.BlockSpec((SSM_W, SSM_W), const), pl.BlockSpec((1, SSM_W), const),
            pl.BlockSpec((d, d), const), pl.BlockSpec((1, d), const), pl.BlockSpec((1, d), const),
            pl.BlockSpec((N_EXPERTS, d), const), pl.BlockSpec((N_EXPERTS, 1), const),
        ],
        out_specs=[
            pl.BlockSpec((tm, d), row),
            pl.BlockSpec((TOP_K, tm), col), pl.BlockSpec((TOP_K, tm), col), pl.BlockSpec((TOP_K, tm), col),
            pl.BlockSpec((N_EXPERTS, 128), const),
        ],
        out_shape=[
            jax.ShapeDtypeStruct((t, d), F32),
            jax.ShapeDtypeStruct((TOP_K, t), I32), jax.ShapeDtypeStruct((TOP_K, t), F32),
            jax.ShapeDtypeStruct((TOP_K, t), I32),
            jax.ShapeDtypeStruct((N_EXPERTS, 128), F32),
        ],
        scratch_shapes=[pltpu.VMEM((N_EXPERTS, 1), F32)],
        compiler_params=pltpu.CompilerParams(dimension_semantics=("arbitrary",), vmem_limit_bytes=VMEM_LIMIT),
        name="mix_ln1_router",
    )(x2, gi, bi, att, y, wglu, bglu, wout, g1, b1, wr_t, br)


def _row_copy(src, src_row, dst, dst_row, sem):
    return pltpu.make_async_copy(src.at[pl.ds(src_row, 1), :], dst.at[pl.ds(dst_row, 1), :], sem)


def _dispatch_kernel(pad_start_ref, pad_count_ref, n_used_ref, dest_ref, h1_ref, xs_hbm, zero_ref, sem, zsem):
    step = pl.program_id(0)
    tm = dest_ref.shape[1]
    n_blocks = xs_hbm.shape[0] // MOE_BLOCK

    def zero_block(blk):
        return pltpu.make_async_copy(zero_ref, xs_hbm.at[pl.ds(blk * MOE_BLOCK, MOE_BLOCK), :], zsem)

    @pl.when(step == 0)
    def _():
        zero_ref[...] = jnp.zeros(zero_ref.shape, F32)
        for e in range(N_EXPERTS):
            start = pad_start_ref[e]
            count = pad_count_ref[e]

            def fill(r, c):
                _row_copy(zero_ref, 0, xs_hbm, start + r, zsem).start()
                return c

            def drain(r, c):
                _row_copy(zero_ref, 0, xs_hbm, start, zsem).wait()
                return c

            lax.fori_loop(0, count, fill, 0)
            lax.fori_loop(0, count, drain, 0)

        def fill_block(blk, c):
            zero_block(blk).start()
            return c

        def drain_block(blk, c):
            zero_block(blk).wait()
            return c

        lax.fori_loop(n_used_ref[0], n_blocks, fill_block, 0)
        lax.fori_loop(n_used_ref[0], n_blocks, drain_block, 0)

    def issue(r, c):
        for k in range(TOP_K):
            _row_copy(h1_ref, r, xs_hbm, dest_ref[k, r], sem).start()
        return c

    lax.fori_loop(0, tm, issue, 0, unroll=8)
    for _ in range(TOP_K):
        pltpu.make_async_copy(h1_ref, xs_hbm.at[pl.ds(0, tm), :], sem).wait()


def _dispatch(pad_start, pad_count, n_used, dest_t, h1, n_rows):
    t, d = h1.shape
    tm = ROW_TILE
    grid_spec = pltpu.PrefetchScalarGridSpec(
        num_scalar_prefetch=3,
        grid=(t // tm,),
        in_specs=[
            pl.BlockSpec((TOP_K, tm), lambda i, ps, pc, nu: (0, i), memory_space=pltpu.SMEM),
            pl.BlockSpec((tm, d), lambda i, ps, pc, nu: (i, 0)),
        ],
        out_specs=pl.BlockSpec(memory_space=pl.ANY),
        scratch_shapes=[pltpu.VMEM((MOE_BLOCK, d), F32), pltpu.SemaphoreType.DMA, pltpu.SemaphoreType.DMA],
    )
    return pl.pallas_call(
        _dispatch_kernel,
        grid_spec=grid_spec,
        out_shape=jax.ShapeDtypeStruct((n_rows, d), F32),
        compiler_params=pltpu.CompilerParams(dimension_semantics=("arbitrary",), has_side_effects=True),
        name="moe_dispatch",
    )(pad_start, pad_count, n_used, dest_t, h1)


def _expert_kernel(be_ref, nb_ref, x_ref, wg_ref, bg_ref, wu_ref, bu_ref, wd_ref, bd_ref, y_ref,
                   wg_b, wu_b, wd_b):
    i = pl.program_id(0)

    @pl.when(i < nb_ref[0])
    def _():
        @pl.when((i == 0) | (be_ref[i] != be_ref[jnp.maximum(i - 1, 0)]))
        def _():
            wg_b[...] = wg_ref[0].astype(BF16)
            wu_b[...] = wu_ref[0].astype(BF16)
            wd_b[...] = wd_ref[0].astype(BF16)

        x = x_ref[...].astype(BF16)
        gate = _dot(x, wg_b[...]) + bg_ref[0]
        up = _dot(x, wu_b[...]) + bu_ref[0]
        gate = jnp.minimum(gate, SWIGLU_LIMIT)
        up = jnp.clip(up, -SWIGLU_LIMIT, SWIGLU_LIMIT)
        act = (up + 1.0) * gate * jax.nn.sigmoid(gate * SWIGLU_ALPHA)
        y_ref[...] = _dot(act.astype(BF16), wd_b[...]) + bd_ref[0]

    @pl.when(i >= nb_ref[0])
    def _():
        y_ref[...] = jnp.zeros(y_ref.shape, F32)


def _experts(block_expert, n_used, xs, wg, bg, wu, bu, wd, bd):
    n_rows, d = xs.shape
    dff = wg.shape[2]
    nb = n_rows // MOE_BLOCK
    blk = lambda i, be, nu: (jnp.minimum(i, nu[0] - 1), 0)
    wsel = lambda i, be, nu: (be[jnp.minimum(i, nu[0] - 1)], 0, 0)
    grid_spec = pltpu.PrefetchScalarGridSpec(
        num_scalar_prefetch=2,
        grid=(nb,),
        in_specs=[
            pl.BlockSpec((MOE_BLOCK, d), blk),
            pl.BlockSpec((1, d, dff), wsel), pl.BlockSpec((1, 1, dff), wsel),
            pl.BlockSpec((1, d, dff), wsel), pl.BlockSpec((1, 1, dff), wsel),
            pl.BlockSpec((1, dff, d), wsel), pl.BlockSpec((1, 1, d), wsel),
        ],
        out_specs=pl.BlockSpec((MOE_BLOCK, d), lambda i, be, nu: (i, 0)),
        scratch_shapes=[pltpu.VMEM((d, dff), BF16), pltpu.VMEM((d, dff), BF16), pltpu.VMEM((dff, d), BF16)],
    )
    return pl.pallas_call(
        _expert_kernel,
        grid_spec=grid_spec,
        out_shape=jax.ShapeDtypeStruct((n_rows, d), F32),
        compiler_params=pltpu.CompilerParams(dimension_semantics=("arbitrary",), vmem_limit_bytes=VMEM_LIMIT),
        name="moe_experts",
    )(block_expert, n_used, xs, wg, bg, wu, bu, wd, bd)


def _combine_kernel(dest_ref, dest_next_ref, h1_ref, gate_ref, ys_hbm, g2_ref, b2_ref, o_ref, buf, sems):
    step = pl.program_id(0)
    n_steps = pl.num_programs(0)
    tm = h1_ref.shape[0]
    slot = step % 2

    def issue(dref, s):
        def body(r, c):
            for k in range(TOP_K):
                pltpu.make_async_copy(ys_hbm.at[pl.ds(dref[k, r], 1), :], buf.at[s, k, pl.ds(r, 1), :],
                                      sems.at[s]).start()
            return c
        lax.fori_loop(0, tm, body, 0, unroll=8)

    @pl.when(step == 0)
    def _():
        issue(dest_ref, 0)

    @pl.when(step + 1 < n_steps)
    def _():
        issue(dest_next_ref, 1 - slot)

    for k in range(TOP_K):
        pltpu.make_async_copy(ys_hbm.at[pl.ds(0, tm), :], buf.at[slot, k], sems.at[slot]).wait()

    gate = gate_ref[...]
    ffn = buf[slot, 0] * gate[:, 0:1]
    for k in range(1, TOP_K):
        ffn = ffn + buf[slot, k] * gate[:, k:k + 1]
    o_ref[...] = _layer_norm(DEEPNORM_ALPHA * h1_ref[...] + ffn, g2_ref[...], b2_ref[...])


def _combine(dest_t, h1, gates, ys, g2, b2):
    t, d = h1.shape
    tm = ROW_TILE
    n_steps = t // tm
    row = lambda i: (i, 0)
    const = lambda i: (0, 0)
    return pl.pallas_call(
        _combine_kernel,
        grid=(n_steps,),
        in_specs=[
            pl.BlockSpec((TOP_K, tm), lambda i: (0, i), memory_space=pltpu.SMEM),
            pl.BlockSpec((TOP_K, tm), lambda i: (0, jnp.minimum(i + 1, n_steps - 1)), memory_space=pltpu.SMEM),
            pl.BlockSpec((tm, d), row), pl.BlockSpec((tm, TOP_K), row),
            pl.BlockSpec(memory_space=pl.ANY),
            pl.BlockSpec((1, d), const), pl.BlockSpec((1, d), const),
        ],
        out_specs=pl.BlockSpec((tm, d), row),
        out_shape=jax.ShapeDtypeStruct((t, d), F32),
        scratch_shapes=[pltpu.VMEM((2, TOP_K, tm, d), F32), pltpu.SemaphoreType.DMA((2,))],
        compiler_params=pltpu.CompilerParams(dimension_semantics=("arbitrary",), vmem_limit_bytes=VMEM_LIMIT),
        name="moe_combine_ln2",
    )(dest_t, dest_t, h1, gates, ys, g2, b2)


def kernel(x, meta_tokens, ln_in_g, ln_in_b, rel_bias, w_in, lambda_q1, lambda_k1, lambda_q2, lambda_k2,
           subln_g, a_re, a_im, log_step, b_re, b_im, c_re, c_im, d_skip, w_glu, b_glu, w_out, ln1_g, ln1_b,
           w_router, b_router, w_gate, b_gate, w_up, b_up, w_down, b_down, ln2_g, ln2_b):
    bsz, seq, dm = x.shape
    assert seq % 512 == 0 and w_in.shape[0] == DEPTH == 1
    layer = 0
    row2 = lambda v: v.astype(F32).reshape(1, -1)

    w_in_b = w_in[layer].astype(BF16)
    gi, bi = row2(ln_in_g), row2(ln_in_b)
    q, k, vt, u = _inproj(x, gi, bi, w_in_b, tm=512, kv_tile=KV_TILE)
    meta = jnp.zeros((1, META_PAD, dm), x.dtype).at[0, :N_META].set(meta_tokens.astype(x.dtype))
    _, k_meta, vt_meta, u_meta = _inproj(meta, gi, bi, w_in_b, tm=META_PAD, kv_tile=META_PAD)

    lam_init = 0.8 - 0.6 * math.exp(-0.3 * layer)
    lam = (jnp.exp(jnp.sum(lambda_q1[layer].astype(F32) * lambda_k1[layer].astype(F32)))
           - jnp.exp(jnp.sum(lambda_q2[layer].astype(F32) * lambda_k2[layer].astype(F32))) + lam_init)
    att = _attention(lam.reshape(1), q, k, vt, k_meta[0], vt_meta[0, :, 0], _near_bias(rel_bias),
                     subln_g[layer].astype(F32).reshape(HEAD_W, 1), lam_init=lam_init)

    kpad, pt, qt, a1, a2 = _ssm_tables(a_re[layer], a_im[layer], log_step[layer], b_re[layer], b_im[layer],
                                     c_re[layer], c_im[layer], d_skip[layer])
    n_chunks = -(-(seq // SSM_T + 1) // 8) * 8
    lead = jnp.zeros((bsz, SSM_T, SSM_W), BF16).at[:, SSM_T - N_META:].set(u_meta[:, :N_META])
    tail = jnp.zeros((bsz, n_chunks * SSM_T - SSM_T - seq, SSM_W), BF16)
    ug = jnp.concatenate([lead, u, tail], axis=1).reshape(bsz, n_chunks, SSM_T, SSM_G, SSM_CG)
    ug = jnp.transpose(ug, (3, 1, 0, 2, 4)).reshape(SSM_G, n_chunks * bsz, SSM_T * SSM_CG)
    yg = _ssm(ug, kpad, pt, qt, a1, a2, n_chunks=n_chunks, bsz=bsz)
    yg = jnp.transpose(yg.reshape(SSM_G, n_chunks, bsz, SSM_T, SSM_CG), (2, 1, 3, 0, 4))
    y_ssm = yg.reshape(bsz, n_chunks * SSM_T, SSM_W)[:, SSM_T:SSM_T + seq]

    t = bsz * seq
    h1, idx_t, gate_t, rank_t, cnt = _mix(
        x.reshape(t, dm), gi, bi, att.reshape(t, ATT_W), y_ssm.reshape(t, SSM_W),
        w_glu[layer].astype(BF16), row2(b_glu[layer]), w_out[layer].astype(BF16),
        row2(ln1_g[layer]), row2(ln1_b[layer]),
        jnp.transpose(w_router[layer].astype(F32)), b_router[layer].astype(F32).reshape(N_EXPERTS, 1))

    counts = cnt[:, 0].astype(I32)
    padded = (counts + MOE_BLOCK - 1) // MOE_BLOCK * MOE_BLOCK
    experts = jnp.arange(N_EXPERTS, dtype=I32)
    padded_end = jnp.sum(jnp.where(experts[:, None] <= experts[None, :], padded[:, None], 0), axis=0)
    padded_start = padded_end - padded
    dest_t = jnp.sum(jnp.where(idx_t[..., None] == experts, padded_start, 0), axis=-1) + rank_t
    n_blocks = t * TOP_K // MOE_BLOCK + N_EXPERTS
    block_row0 = jnp.arange(n_blocks, dtype=I32) * MOE_BLOCK
    block_expert = jnp.minimum(jnp.sum((padded_end[None, :] <= block_row0[:, None]).astype(I32), axis=1),
                               N_EXPERTS - 1)
    n_used = (padded_end[-1:] // MOE_BLOCK).astype(I32)

    xs = _dispatch((padded_start + counts).astype(I32), (padded - counts).astype(I32), n_used, dest_t, h1,
                   n_blocks * MOE_BLOCK)
    b3 = lambda v: v.astype(F32)[:, None, :]
    ys = _experts(block_expert, n_used, xs, w_gate[layer], b3(b_gate[layer]), w_up[layer], b3(b_up[layer]),
                  w_down[layer], b3(b_down[layer]))
    out = _combine(dest_t, h1, jnp.transpose(gate_t), ys, row2(ln2_g[layer]), row2(ln2_b[layer]))
    return out.reshape(bsz, seq, dm)
```

```python
import functools
import math

import jax
import jax.numpy as jnp
import numpy as np
from jax import lax
from jax.experimental import pallas as pl
from jax.experimental.pallas import tpu as pltpu

F32 = jnp.float32
BF16 = jnp.bfloat16
I32 = jnp.int32

DEPTH = 1
N_META = 16
CHUNK = 64
ATT_HEADS = 4
HEAD_DIM = 64
HEAD_W = 2 * HEAD_DIM
ATT_W = ATT_HEADS * HEAD_W
SSM_W = 512
SSM_CG = 16
SSM_G = SSM_W // SSM_CG
SSM_N = 64
N_BUCKETS = 32
MAX_DISTANCE = 128
N_EXPERTS = 32
TOP_K = 4
SWIGLU_LIMIT = 7.0
SWIGLU_ALPHA = 1.702
LN_EPS = 1e-5
NEG_INF = -1e30
DEEPNORM_ALPHA = (2.0 * DEPTH) ** 0.25
LOG2E = 1.4426950408889634

Q_TILE = 512
KV_TILE = 512
V_ONES = 16
V_ROWS = HEAD_W + V_ONES
META_PAD = 128
SSM_T = 64
MIX_TILE = 512
ROW_TILE = 256
MOE_BLOCK = 512
VMEM_LIMIT = 56 * 1024 * 1024


def _layer_norm(x, g, b):
    mu = jnp.mean(x, axis=-1, keepdims=True)
    xc = x - mu
    var = jnp.mean(xc * xc, axis=-1, keepdims=True)
    return xc * lax.rsqrt(var + LN_EPS) * g + b


def _dot(a, b):
    return jnp.dot(a, b, preferred_element_type=F32)


def _dot_nt(a, b, precision=None):
    return lax.dot_general(a, b, (((1,), (1,)), ((), ())), precision=precision,
                           preferred_element_type=F32)


def _inproj_kernel(x_ref, g_ref, b_ref, w_ref, q_ref, k_ref, vt_ref, u_ref, *, kv_tile, q_scale):
    h = _layer_norm(x_ref[0], g_ref[...], b_ref[...]).astype(BF16)
    tm = h.shape[0]
    q_ref[0] = (_dot(h, w_ref[:, 0:ATT_W]) * q_scale).astype(BF16)
    k_ref[0] = _dot(h, w_ref[:, ATT_W:2 * ATT_W]).astype(BF16)
    v = _dot(h, w_ref[:, 2 * ATT_W:3 * ATT_W])
    ones = jnp.ones((V_ONES, kv_tile), BF16)
    for hh in range(ATT_HEADS):
        vt = v[:, hh * HEAD_W:(hh + 1) * HEAD_W].T.astype(BF16)
        for j in range(tm // kv_tile):
            vt_ref[0, hh, j, 0:HEAD_W, :] = vt[:, j * kv_tile:(j + 1) * kv_tile]
            vt_ref[0, hh, j, HEAD_W:V_ROWS, :] = ones
    u_ref[0] = _dot(h, w_ref[:, 3 * ATT_W:]).astype(BF16)


def _inproj(x, g, b, w, *, tm, kv_tile):
    bsz, s, d = x.shape
    q_scale = HEAD_DIM ** -0.5 * LOG2E
    n_cols = w.shape[1]
    row = lambda bi, i: (bi, i, 0)
    return pl.pallas_call(
        functools.partial(_inproj_kernel, kv_tile=kv_tile, q_scale=q_scale),
        grid=(bsz, s // tm),
        in_specs=[
            pl.BlockSpec((1, tm, d), row),
            pl.BlockSpec((1, d), lambda bi, i: (0, 0)),
            pl.BlockSpec((1, d), lambda bi, i: (0, 0)),
            pl.BlockSpec((d, n_cols), lambda bi, i: (0, 0)),
        ],
        out_specs=[
            pl.BlockSpec((1, tm, ATT_W), row),
            pl.BlockSpec((1, tm, ATT_W), row),
            pl.BlockSpec((1, ATT_HEADS, tm // kv_tile, V_ROWS, kv_tile), lambda bi, i: (bi, 0, i, 0, 0)),
            pl.BlockSpec((1, tm, SSM_W), row),
        ],
        out_shape=[
            jax.ShapeDtypeStruct((bsz, s, ATT_W), BF16),
            jax.ShapeDtypeStruct((bsz, s, ATT_W), BF16),
            jax.ShapeDtypeStruct((bsz, ATT_HEADS, s // kv_tile, V_ROWS, kv_tile), BF16),
            jax.ShapeDtypeStruct((bsz, s, SSM_W), BF16),
        ],
        compiler_params=pltpu.CompilerParams(
            dimension_semantics=("parallel", "parallel"), vmem_limit_bytes=VMEM_LIMIT),
        name="inproj",
    )(x, g, b, w)


def _attn_kernel(lam_ref, q_ref, k_ref, vt_ref, km_ref, vtm_ref, bias_ref, g_ref, o_ref,
                 m_ref, acc_ref, qz_ref, sa_ref, sb_ref, sn_ref, *, out_scale):
    i = pl.program_id(2)
    q = q_ref[0]
    lane = lax.broadcasted_iota(I32, q.shape, 1)
    zero = jnp.zeros_like(q)
    qz_ref[0:Q_TILE, :] = jnp.where(lane < HEAD_DIM, q, zero)
    qz_ref[Q_TILE:, :] = jnp.where(lane >= HEAD_DIM, q, zero)

    m_ref[...] = jnp.full(m_ref.shape, NEG_INF, F32)
    acc_ref[...] = jnp.zeros(acc_ref.shape, F32)

    def scores(kt):
        return _dot_nt(kt, qz_ref[...])

    def k_tile(t):
        return k_ref[0, pl.ds(pl.multiple_of(t * KV_TILE, KV_TILE), KV_TILE), :]

    def absorb(s, pv):
        m_old = m_ref[...]
        m_new = jnp.maximum(m_old, jnp.max(s, axis=0, keepdims=True))
        alpha = jnp.exp2(m_old - m_new)
        p = jnp.exp2(s - m_new).astype(BF16)
        acc_ref[...] = acc_ref[...] * alpha + pv(p)
        m_ref[...] = m_new

    def tile_pv(t):
        return lambda p: _dot(vt_ref[0, 0, t], p)

    n_far = jnp.maximum(i - 1, 0)
    peel = n_far % 2

    @pl.when(peel == 1)
    def _():
        absorb(scores(k_tile(0)), tile_pv(0))

    sa_ref[...] = scores(k_tile(peel))

    def far_pair(j, carry):
        t0 = peel + 2 * j
        sb_ref[...] = scores(k_tile(t0 + 1))
        absorb(sa_ref[...], tile_pv(t0))
        sa_ref[...] = scores(k_tile(t0 + 2))
        absorb(sb_ref[...], tile_pv(t0 + 1))
        return carry

    lax.fori_loop(0, n_far // 2, far_pair, 0)

    sb_ref[...] = scores(k_tile(i))
    absorb(sa_ref[...] + bias_ref[0, 0, 0:KV_TILE, :], tile_pv(n_far))
    sn_ref[...] = scores(km_ref[...])
    absorb(sb_ref[...] + bias_ref[0, 0, KV_TILE:2 * KV_TILE, :], tile_pv(i))
    absorb(sn_ref[...] + bias_ref[0, 0, 2 * KV_TILE:, :], lambda p: _dot(vtm_ref[0], p))

    acc = acc_ref[...]
    lam = lam_ref[0]
    o1 = acc[0:HEAD_W, 0:Q_TILE] / acc[HEAD_W:HEAD_W + 1, 0:Q_TILE]
    o2 = acc[0:HEAD_W, Q_TILE:] / acc[HEAD_W:HEAD_W + 1, Q_TILE:]
    o = o1 - lam * o2
    ms = jnp.mean(o * o, axis=0, keepdims=True)
    o = o * lax.rsqrt(ms + LN_EPS) * g_ref[...] * out_scale
    o_ref[0] = o.T.astype(o_ref.dtype)


def _attention(lam, q, k, vt, k_meta, vt_meta, bias, subln_g, *, lam_init):
    bsz, s, _ = q.shape
    nq = s // Q_TILE
    n_near = META_PAD + 2 * KV_TILE
    grid_spec = pltpu.PrefetchScalarGridSpec(
        num_scalar_prefetch=1,
        grid=(bsz, ATT_HEADS, nq),
        in_specs=[
            pl.BlockSpec((1, Q_TILE, HEAD_W), lambda b, h, i, lam: (b, i, h)),
            pl.BlockSpec((1, s, HEAD_W), lambda b, h, i, lam: (b, 0, h)),
            pl.BlockSpec((1, 1, s // KV_TILE, V_ROWS, KV_TILE), lambda b, h, i, lam: (b, h, 0, 0, 0)),
            pl.BlockSpec((META_PAD, HEAD_W), lambda b, h, i, lam: (0, h)),
            pl.BlockSpec((1, V_ROWS, META_PAD), lambda b, h, i, lam: (h, 0, 0)),
            pl.BlockSpec((1, 1, n_near, 2 * Q_TILE), lambda b, h, i, lam: (h, jnp.minimum(i, 1), 0, 0)),
            pl.BlockSpec((HEAD_W, 1), lambda b, h, i, lam: (0, 0)),
        ],
        out_specs=pl.BlockSpec((1, Q_TILE, HEAD_W), lambda b, h, i, lam: (b, i, h)),
        scratch_shapes=[
            pltpu.VMEM((1, 2 * Q_TILE), F32), pltpu.VMEM((V_ROWS, 2 * Q_TILE), F32),
            pltpu.VMEM((2 * Q_TILE, HEAD_W), BF16),
            pltpu.VMEM((KV_TILE, 2 * Q_TILE), F32), pltpu.VMEM((KV_TILE, 2 * Q_TILE), F32),
            pltpu.VMEM((META_PAD, 2 * Q_TILE), F32),
        ],
    )
    return pl.pallas_call(
        functools.partial(_attn_kernel, out_scale=1.0 - lam_init),
        grid_spec=grid_spec,
        out_shape=jax.ShapeDtypeStruct((bsz, s, ATT_W), BF16),
        compiler_params=pltpu.CompilerParams(
            dimension_semantics=("parallel", "parallel", "arbitrary"), vmem_limit_bytes=VMEM_LIMIT),
        name="diff_attention",
    )(lam, q, k, vt, k_meta, vt_meta, bias, subln_g)


def _t5_bucket(rel):
    nb = N_BUCKETS // 2
    max_exact = nb // 2
    ret = jnp.where(rel > 0, nb, 0)
    n = jnp.abs(rel)
    n_f = jnp.maximum(n, 1).astype(F32)
    large = max_exact + (jnp.log(n_f / max_exact) / math.log(MAX_DISTANCE / max_exact)
                         * (nb - max_exact)).astype(I32)
    large = jnp.minimum(large, nb - 1)
    return ret + jnp.where(n < max_exact, n, large)


def _near_bias(rel_bias):
    table = rel_bias.astype(F32)
    far = table[N_BUCKETS // 2 - 1]
    c = jnp.arange(Q_TILE, dtype=I32)[None, :]
    r = jnp.arange(KV_TILE, dtype=I32)[:, None]

    shifted = jnp.transpose(table - far) * LOG2E

    def bias_of(offset, rows):
        rel = jnp.arange(rows, dtype=I32)[:, None] - c + offset
        hot = _t5_bucket(rel)[None, :, :, None] == jnp.arange(N_BUCKETS, dtype=I32)
        return jnp.sum(jnp.where(hot, shifted[:, None, None, :], 0.0), axis=-1)

    own = jnp.where((r // CHUNK <= c // CHUNK)[None], bias_of(0, KV_TILE), NEG_INF)
    prev = bias_of(-KV_TILE, KV_TILE)
    rm = jnp.arange(META_PAD, dtype=I32)[:, None]
    meta_ok = (rm < N_META)[None]
    meta0 = jnp.where(meta_ok, bias_of(-N_META, META_PAD), NEG_INF)
    meta1 = jnp.where(meta_ok, bias_of(-N_META - Q_TILE, META_PAD), NEG_INF)
    v0 = jnp.concatenate([jnp.full_like(prev, NEG_INF), own, meta0], axis=1)
    v1 = jnp.concatenate([prev, own, meta1], axis=1)
    both = jnp.stack([v0, v1], axis=1)
    return jnp.concatenate([both, both], axis=-1)


def _ssm_tables(a_re, a_im, log_step, b_re, b_im, c_re, c_im, d_skip):
    hi = lax.Precision.HIGHEST
    t_len = SSM_T
    step = jnp.exp(log_step.astype(F32))[:, None]
    ar = jnp.minimum(a_re.astype(F32), -1e-4)
    ai = a_im.astype(F32)
    mag = jnp.exp(step * ar)
    ph = step * ai
    abar_re = mag * jnp.cos(ph)
    abar_im = mag * jnp.sin(ph)
    den = ar * ar + ai * ai
    e_re = abar_re - 1.0
    e_im = abar_im
    f_re = (e_re * ar + e_im * ai) / den
    f_im = (e_im * ar - e_re * ai) / den
    br = b_re.astype(F32)
    bi = b_im.astype(F32)
    bb_re = f_re[..., None] * br - f_im[..., None] * bi
    bb_im = f_re[..., None] * bi + f_im[..., None] * br
    tau = jnp.arange(t_len + 1, dtype=F32)[None, :, None]
    pmag = jnp.exp(tau * (step * ar)[:, None, :])
    pph = tau * ph[:, None, :]
    pw_re = pmag * jnp.cos(pph)
    pw_im = pmag * jnp.sin(pph)
    w_re = pw_re[..., None] * bb_re[:, None] - pw_im[..., None] * bb_im[:, None]
    w_im = pw_re[..., None] * bb_im[:, None] + pw_im[..., None] * bb_re[:, None]
    cr = c_re.astype(F32)
    ci = c_im.astype(F32)
    kern = (jnp.einsum('gcn,gtnk->gtck', cr, w_re[:, :t_len], precision=hi)
            - jnp.einsum('gcn,gtnk->gtck', ci, w_im[:, :t_len], precision=hi))
    skip = d_skip.astype(F32).reshape(SSM_G, SSM_CG)
    kern = kern.at[:, 0].add(skip[:, :, None] * jnp.eye(SSM_CG, dtype=F32))
    kflat = jnp.transpose(kern, (0, 3, 1, 2)).reshape(SSM_G, SSM_CG, t_len * SSM_CG)
    kpad = jnp.concatenate([jnp.zeros_like(kflat), kflat], axis=-1)
    pt_re = jnp.transpose(jnp.flip(w_re[:, :t_len], axis=1), (0, 1, 3, 2)).reshape(SSM_G, t_len * SSM_CG, SSM_N)
    pt_im = jnp.transpose(jnp.flip(w_im[:, :t_len], axis=1), (0, 1, 3, 2)).reshape(SSM_G, t_len * SSM_CG, SSM_N)
    pt = jnp.concatenate([pt_re, pt_im], axis=-1)
    up_re = pw_re[:, 1:]
    up_im = pw_im[:, 1:]
    q_re = cr[:, None] * up_re[:, :, None, :] - ci[:, None] * up_im[:, :, None, :]
    q_im = cr[:, None] * up_im[:, :, None, :] + ci[:, None] * up_re[:, :, None, :]
    qt = jnp.concatenate([jnp.transpose(q_re, (0, 3, 1, 2)), -jnp.transpose(q_im, (0, 3, 1, 2))], axis=1)
    qt = qt.reshape(SSM_G, 2 * SSM_N, t_len * SSM_CG)
    at_re = pw_re[:, t_len]
    at_im = pw_im[:, t_len]
    a1 = jnp.concatenate([at_re, at_re], axis=-1)
    a2 = jnp.concatenate([-at_im, at_im], axis=-1)
    return kpad, pt.astype(BF16), qt.astype(BF16), a1, a2


def _ssm_state_kernel(u_ref, pt_ref, s_ref):
    s_ref[0] = _dot(u_ref[0], pt_ref[0])


def _ssm_scan_kernel(s_ref, a1_ref, a2_ref, x_ref, *, n_chunks, bsz):
    a1 = a1_ref[...][:, None, :]
    a2 = a2_ref[...][:, None, :]

    def body(c, x):
        rows = pl.ds(c * bsz, bsz)
        x_ref[:, rows, :] = x
        return a1 * x + a2 * pltpu.roll(x, SSM_N, 2) + s_ref[:, rows, :]

    lax.fori_loop(0, n_chunks, body, jnp.zeros((SSM_G, bsz, 2 * SSM_N), F32))


def _ssm_out_kernel(u_ref, x_ref, kp_ref, qt_ref, y_ref, mt_ref):
    kp = kp_ref[0]
    width = SSM_T * SSM_CG
    lane_tile = 128
    for sub in range(0, lane_tile, SSM_CG):
        shifted = kp if sub == 0 else pltpu.roll(kp, 2 * width - sub, 1)
        for s in range(SSM_T):
            off = (SSM_T - s) * SSM_CG
            if off % lane_tile == sub:
                base = off - sub
                mt_ref[s * SSM_CG:(s + 1) * SSM_CG, :] = shifted[:, base:base + width].astype(BF16)
    x = x_ref[0]
    x_hi = x.astype(BF16)
    x_lo = (x - x_hi.astype(F32)).astype(BF16)
    y = _dot(u_ref[0], mt_ref[...]) + _dot(x_hi, qt_ref[0]) + _dot(x_lo, qt_ref[0])
    y_ref[0] = y.astype(y_ref.dtype)


def _ssm(ug, kpad, pt, qt, a1, a2, *, n_chunks, bsz):
    g, r, w = ug.shape
    n2 = 2 * SSM_N
    per_g = lambda gi: (gi, 0, 0)
    params = pltpu.CompilerParams(dimension_semantics=("parallel",), vmem_limit_bytes=VMEM_LIMIT)
    s = pl.pallas_call(
        _ssm_state_kernel,
        grid=(g,),
        in_specs=[pl.BlockSpec((1, r, w), per_g), pl.BlockSpec((1, w, n2), per_g)],
        out_specs=pl.BlockSpec((1, r, n2), per_g),
        out_shape=jax.ShapeDtypeStruct((g, r, n2), F32),
        compiler_params=params,
        name="ssm_chunk_state",
    )(ug, pt)
    x = pl.pallas_call(
        functools.partial(_ssm_scan_kernel, n_chunks=n_chunks, bsz=bsz),
        out_shape=jax.ShapeDtypeStruct((g, r, n2), F32),
        compiler_params=pltpu.CompilerParams(vmem_limit_bytes=VMEM_LIMIT),
        name="ssm_chunk_scan",
    )(s, a1, a2)
    return pl.pallas_call(
        _ssm_out_kernel,
        grid=(g,),
        in_specs=[pl.BlockSpec((1, r, w), per_g), pl.BlockSpec((1, r, n2), per_g),
                  pl.BlockSpec((1, SSM_CG, 2 * w), per_g), pl.BlockSpec((1, n2, w), per_g)],
        out_specs=pl.BlockSpec((1, r, w), per_g),
        out_shape=jax.ShapeDtypeStruct((g, r, w), BF16),
        scratch_shapes=[pltpu.VMEM((w, w), BF16)],
        compiler_params=params,
        name="ssm_output",
    )(ug, x, kpad, qt)


def _mix_kernel(x_ref, gi_ref, bi_ref, att_ref, y_ref, wglu_ref, bglu_ref, wout_ref, g1_ref, b1_ref,
                wr_ref, br_ref, h1_ref, idx_ref, gate_ref, rank_ref, cnt_ref, carry_ref):
    step = pl.program_id(0)

    @pl.when(step == 0)
    def _():
        carry_ref[...] = jnp.zeros(carry_ref.shape, F32)

    h0 = _layer_norm(x_ref[...], gi_ref[...], bi_ref[...])
    y = y_ref[...].astype(F32)
    y = y * (0.5 * (1.0 + jnp.tanh(math.sqrt(2.0 / math.pi) * (y + 0.044715 * (y * y * y)))))
    y = y * jax.nn.sigmoid(_dot(y.astype(BF16), wglu_ref[...]) + bglu_ref[...])
    mix = _dot(att_ref[...], wout_ref[0:ATT_W, :]) + _dot(y.astype(BF16), wout_ref[ATT_W:, :])
    h1 = _layer_norm(DEEPNORM_ALPHA * h0 + mix, g1_ref[...], b1_ref[...])
    h1_ref[...] = h1

    logits = _dot_nt(wr_ref[...], h1, precision=lax.Precision.HIGHEST) + br_ref[...]
    tm = logits.shape[1]
    eidx = lax.broadcasted_iota(I32, logits.shape, 0)
    vals, hots = [], []
    rest = logits
    for _ in range(TOP_K):
        mx = jnp.max(rest, axis=0, keepdims=True)
        first = jnp.min(jnp.where(rest == mx, eidx, N_EXPERTS), axis=0, keepdims=True)
        hot = eidx == first
        vals.append(mx)
        hots.append(hot)
        rest = jnp.where(hot, -jnp.inf, rest)
    exps = [jnp.exp(v - vals[0]) for v in vals]
    denom = exps[0] + exps[1] + exps[2] + exps[3]
    gate_ref[...] = jnp.concatenate([e / denom for e in exps], axis=0)
    idx_ref[...] = jnp.concatenate(
        [jnp.sum(jnp.where(h, eidx, 0), axis=0, keepdims=True) for h in hots], axis=0)

    hot_all = (hots[0] | hots[1] | hots[2] | hots[3]).astype(F32)
    tri = (lax.broadcasted_iota(I32, (tm, tm), 0) < lax.broadcasted_iota(I32, (tm, tm), 1)).astype(BF16)
    before = _dot(hot_all.astype(BF16), tri) + carry_ref[...]
    rank_ref[...] = jnp.concatenate(
        [jnp.sum(jnp.where(h, before, 0.0), axis=0, keepdims=True) for h in hots], axis=0).astype(I32)
    carry_ref[...] = carry_ref[...] + jnp.sum(hot_all, axis=1, keepdims=True)
    cnt_ref[...] = jnp.broadcast_to(carry_ref[...], cnt_ref.shape)


def _mix(x2, gi, bi, att, y, wglu, bglu, wout, g1, b1, wr_t, br):
    t, d = x2.shape
    tm = MIX_TILE
    row = lambda i: (i, 0)
    col = lambda i: (0, i)
    const = lambda i: (0, 0)
    return pl.pallas_call(
        _mix_kernel,
        grid=(t // tm,),
        in_specs=[
            pl.BlockSpec((tm, d), row), pl.BlockSpec((1, d), const), pl.BlockSpec((1, d), const),
            pl.BlockSpec((tm, ATT_W), row), pl.BlockSpec((tm, SSM_W), row),
            pl.BlockSpec((SSM_W, SSM_W), const), pl.BlockSpec((1, SSM_W), const),
            pl.BlockSpec((d, d), const), pl.BlockSpec((1, d), const), pl.BlockSpec((1, d), const),
            pl.BlockSpec((N_EXPERTS, d), const), pl.BlockSpec((N_EXPERTS, 1), const),
        ],
        out_specs=[
            pl.BlockSpec((tm, d), row),
            pl.BlockSpec((TOP_K, tm), col), pl.BlockSpec((TOP_K, tm), col), pl.BlockSpec((TOP_K, tm), col),
            pl.BlockSpec((N_EXPERTS, 128), const),
        ],
        out_shape=[
            jax.ShapeDtypeStruct((t, d), F32),
            jax.ShapeDtypeStruct((TOP_K, t), I32), jax.ShapeDtypeStruct((TOP_K, t), F32),
            jax.ShapeDtypeStruct((TOP_K, t), I32),
            jax.ShapeDtypeStruct((N_EXPERTS, 128), F32),
        ],
        scratch_shapes=[pltpu.VMEM((N_EXPERTS, 1), F32)],
        compiler_params=pltpu.CompilerParams(dimension_semantics=("arbitrary",), vmem_limit_bytes=VMEM_LIMIT),
        name="mix_ln1_router",
    )(x2, gi, bi, att, y, wglu, bglu, wout, g1, b1, wr_t, br)


def _expert_kernel(be_ref, nb_ref, tok_first_ref, tok_next_ref, h1_hbm, wg_ref, bg_ref, wu_ref, bu_ref,
                   wd_ref, bd_ref, y_ref, x_even, x_odd, sems, wg_b, wu_b, wd_b):
    i = pl.program_id(0)
    n_used = nb_ref[0]
    slot = i % 2
    bufs = (x_even, x_odd)

    def gather(tok_ref, s):
        for r in range(MOE_BLOCK):
            pltpu.make_async_copy(h1_hbm.at[pl.ds(tok_ref[0, 0, r], 1), :], bufs[s].at[pl.ds(r, 1), :],
                                  sems.at[s]).start()

    def wait_block(s):
        pltpu.make_async_copy(h1_hbm.at[pl.ds(0, MOE_BLOCK), :], bufs[s], sems.at[s]).wait()

    @pl.when(i == 0)
    def _():
        gather(tok_first_ref, 0)

    @pl.when((i < n_used) & ((i == 0) | (be_ref[i] != be_ref[jnp.maximum(i - 1, 0)])))
    def _():
        wg_b[...] = wg_ref[0].astype(BF16)
        wu_b[...] = wu_ref[0].astype(BF16)
        wd_b[...] = wd_ref[0].astype(BF16)

    def run_block(s):
        wait_block(s)
        gather(tok_next_ref, 1 - s)
        x = bufs[s][...].astype(BF16)
        gate = _dot(x, wg_b[...]) + bg_ref[0]
        up = _dot(x, wu_b[...]) + bu_ref[0]
        gate = jnp.minimum(gate, SWIGLU_LIMIT)
        up = jnp.clip(up, -SWIGLU_LIMIT, SWIGLU_LIMIT)
        act = (up + 1.0) * gate * jax.nn.sigmoid(gate * SWIGLU_ALPHA)
        y_ref[...] = _dot(act.astype(BF16), wd_b[...]) + bd_ref[0]

    for s in range(2):
        @pl.when((i < n_used) & (slot == s))
        def _(s=s):
            run_block(s)

        @pl.when((i == n_used - 1) & (slot == s))
        def _(s=s):
            wait_block(1 - s)

    @pl.when(i >= n_used)
    def _():
        y_ref[...] = jnp.zeros(y_ref.shape, F32)


def _experts(block_expert, n_used, slot_token, h1, wg, bg, wu, bu, wd, bd):
    nb = slot_token.shape[0]
    d = h1.shape[1]
    dff = wg.shape[2]
    wsel = lambda i, be, nu: (be[jnp.minimum(i, nu[0] - 1)], 0, 0)
    grid_spec = pltpu.PrefetchScalarGridSpec(
        num_scalar_prefetch=2,
        grid=(nb,),
        in_specs=[
            pl.BlockSpec((1, 1, MOE_BLOCK), lambda i, be, nu: (0, 0, 0), memory_space=pltpu.SMEM),
            pl.BlockSpec((1, 1, MOE_BLOCK), lambda i, be, nu: (jnp.minimum(i + 1, nu[0] - 1), 0, 0),
                         memory_space=pltpu.SMEM),
            pl.BlockSpec(memory_space=pl.ANY),
            pl.BlockSpec((1, d, dff), wsel), pl.BlockSpec((1, 1, dff), wsel),
            pl.BlockSpec((1, d, dff), wsel), pl.BlockSpec((1, 1, dff), wsel),
            pl.BlockSpec((1, dff, d), wsel), pl.BlockSpec((1, 1, d), wsel),
        ],
        out_specs=pl.BlockSpec((MOE_BLOCK, d), lambda i, be, nu: (i, 0)),
        scratch_shapes=[pltpu.VMEM((MOE_BLOCK, d), F32), pltpu.VMEM((MOE_BLOCK, d), F32),
                        pltpu.SemaphoreType.DMA((2,)),
                        pltpu.VMEM((d, dff), BF16), pltpu.VMEM((d, dff), BF16), pltpu.VMEM((dff, d), BF16)],
    )
    return pl.pallas_call(
        _expert_kernel,
        grid_spec=grid_spec,
        out_shape=jax.ShapeDtypeStruct((nb * MOE_BLOCK, d), F32),
        compiler_params=pltpu.CompilerParams(dimension_semantics=("arbitrary",), vmem_limit_bytes=VMEM_LIMIT),
        name="moe_experts",
    )(block_expert, n_used, slot_token, slot_token, h1, wg, bg, wu, bu, wd, bd)


def _combine_kernel(dest_ref, dest_next_ref, h1_ref, gate_ref, ys_hbm, g2_ref, b2_ref, o_ref, buf, sems):
    step = pl.program_id(0)
    n_steps = pl.num_programs(0)
    tm = h1_ref.shape[0]
    slot = step % 2

    def issue(dref, s):
        def body(r, c):
            for k in range(TOP_K):
                pltpu.make_async_copy(ys_hbm.at[pl.ds(dref[k, r], 1), :], buf.at[s, k, pl.ds(r, 1), :],
                                      sems.at[s]).start()
            return c
        lax.fori_loop(0, tm, body, 0, unroll=8)

    @pl.when(step == 0)
    def _():
        issue(dest_ref, 0)

    @pl.when(step + 1 < n_steps)
    def _():
        issue(dest_next_ref, 1 - slot)

    for k in range(TOP_K):
        pltpu.make_async_copy(ys_hbm.at[pl.ds(0, tm), :], buf.at[slot, k], sems.at[slot]).wait()

    gate = gate_ref[...]
    ffn = buf[slot, 0] * gate[:, 0:1]
    for k in range(1, TOP_K):
        ffn = ffn + buf[slot, k] * gate[:, k:k + 1]
    o_ref[...] = _layer_norm(DEEPNORM_ALPHA * h1_ref[...] + ffn, g2_ref[...], b2_ref[...])


def _combine(dest_t, h1, gates, ys, g2, b2):
    t, d = h1.shape
    tm = ROW_TILE
    n_steps = t // tm
    row = lambda i: (i, 0)
    const = lambda i: (0, 0)
    return pl.pallas_call(
        _combine_kernel,
        grid=(n_steps,),
        in_specs=[
            pl.BlockSpec((TOP_K, tm), lambda i: (0, i), memory_space=pltpu.SMEM),
            pl.BlockSpec((TOP_K, tm), lambda i: (0, jnp.minimum(i + 1, n_steps - 1)), memory_space=pltpu.SMEM),
            pl.BlockSpec((tm, d), row), pl.BlockSpec((tm, TOP_K), row),
            pl.BlockSpec(memory_space=pl.ANY),
            pl.BlockSpec((1, d), const), pl.BlockSpec((1, d), const),
        ],
        out_specs=pl.BlockSpec((tm, d), row),
        out_shape=jax.ShapeDtypeStruct((t, d), F32),
        scratch_shapes=[pltpu.VMEM((2, TOP_K, tm, d), F32), pltpu.SemaphoreType.DMA((2,))],
        compiler_params=pltpu.CompilerParams(dimension_semantics=("arbitrary",), vmem_limit_bytes=VMEM_LIMIT),
        name="moe_combine_ln2",
    )(dest_t, dest_t, h1, gates, ys, g2, b2)


def kernel(x, meta_tokens, ln_in_g, ln_in_b, rel_bias, w_in, lambda_q1, lambda_k1, lambda_q2, lambda_k2,
           subln_g, a_re, a_im, log_step, b_re, b_im, c_re, c_im, d_skip, w_glu, b_glu, w_out, ln1_g, ln1_b,
           w_router, b_router, w_gate, b_gate, w_up, b_up, w_down, b_down, ln2_g, ln2_b):
    bsz, seq, dm = x.shape
    assert seq % 512 == 0 and w_in.shape[0] == DEPTH == 1
    layer = 0
    row2 = lambda v: v.astype(F32).reshape(1, -1)

    w_in_b = w_in[layer].astype(BF16)
    gi, bi = row2(ln_in_g), row2(ln_in_b)
    q, k, vt, u = _inproj(x, gi, bi, w_in_b, tm=512, kv_tile=KV_TILE)
    meta = jnp.zeros((1, META_PAD, dm), x.dtype).at[0, :N_META].set(meta_tokens.astype(x.dtype))
    _, k_meta, vt_meta, u_meta = _inproj(meta, gi, bi, w_in_b, tm=META_PAD, kv_tile=META_PAD)

    lam_init = 0.8 - 0.6 * math.exp(-0.3 * layer)
    lam = (jnp.exp(jnp.sum(lambda_q1[layer].astype(F32) * lambda_k1[layer].astype(F32)))
           - jnp.exp(jnp.sum(lambda_q2[layer].astype(F32) * lambda_k2[layer].astype(F32))) + lam_init)
    att = _attention(lam.reshape(1), q, k, vt, k_meta[0], vt_meta[0, :, 0], _near_bias(rel_bias),
                     subln_g[layer].astype(F32).reshape(HEAD_W, 1), lam_init=lam_init)

    kpad, pt, qt, a1, a2 = _ssm_tables(a_re[layer], a_im[layer], log_step[layer], b_re[layer], b_im[layer],
                                     c_re[layer], c_im[layer], d_skip[layer])
    n_chunks = -(-(seq // SSM_T + 1) // 8) * 8
    lead = jnp.zeros((bsz, SSM_T, SSM_W), BF16).at[:, SSM_T - N_META:].set(u_meta[:, :N_META])
    tail = jnp.zeros((bsz, n_chunks * SSM_T - SSM_T - seq, SSM_W), BF16)
    ug = jnp.concatenate([lead, u, tail], axis=1).reshape(bsz, n_chunks, SSM_T, SSM_G, SSM_CG)
    ug = jnp.transpose(ug, (3, 1, 0, 2, 4)).reshape(SSM_G, n_chunks * bsz, SSM_T * SSM_CG)
    yg = _ssm(ug, kpad, pt, qt, a1, a2, n_chunks=n_chunks, bsz=bsz)
    yg = jnp.transpose(yg.reshape(SSM_G, n_chunks, bsz, SSM_T, SSM_CG), (2, 1, 3, 0, 4))
    y_ssm = yg.reshape(bsz, n_chunks * SSM_T, SSM_W)[:, SSM_T:SSM_T + seq]

    t = bsz * seq
    h1, idx_t, gate_t, rank_t, cnt = _mix(
        x.reshape(t, dm), gi, bi, att.reshape(t, ATT_W), y_ssm.reshape(t, SSM_W),
        w_glu[layer].astype(BF16), row2(b_glu[layer]), w_out[layer].astype(BF16),
        row2(ln1_g[layer]), row2(ln1_b[layer]),
        jnp.transpose(w_router[layer].astype(F32)), b_router[layer].astype(F32).reshape(N_EXPERTS, 1))

    counts = cnt[:, 0].astype(I32)
    padded = (counts + MOE_BLOCK - 1) // MOE_BLOCK * MOE_BLOCK
    experts = jnp.arange(N_EXPERTS, dtype=I32)
    padded_end = jnp.sum(jnp.where(experts[:, None] <= experts[None, :], padded[:, None], 0), axis=0)
    padded_start = padded_end - padded
    dest_t = jnp.sum(jnp.where(idx_t[..., None] == experts, padded_start, 0), axis=-1) + rank_t
    n_blocks = t * TOP_K // MOE_BLOCK + N_EXPERTS
    block_row0 = jnp.arange(n_blocks, dtype=I32) * MOE_BLOCK
    block_expert = jnp.minimum(jnp.sum((padded_end[None, :] <= block_row0[:, None]).astype(I32), axis=1),
                               N_EXPERTS - 1)
    n_used = (padded_end[-1:] // MOE_BLOCK).astype(I32)

    token_of = jnp.broadcast_to(jnp.arange(t, dtype=I32), (TOP_K, t))
    slot_token = jnp.zeros((n_blocks * MOE_BLOCK,), I32).at[dest_t.reshape(-1)].set(
        token_of.reshape(-1), unique_indices=True).reshape(n_blocks, 1, MOE_BLOCK)
    b3 = lambda v: v.astype(F32)[:, None, :]
    ys = _experts(block_expert, n_used, slot_token, h1, w_gate[layer], b3(b_gate[layer]), w_up[layer],
                  b3(b_up[layer]), w_down[layer], b3(b_down[layer]))
    out = _combine(dest_t, h1, jnp.transpose(gate_t), ys, row2(ln2_g[layer]), row2(ln2_b[layer]))
    return out.reshape(bsz, seq, dm)
```

```python
import functools
import math

import jax
import jax.numpy as jnp
import numpy as np
from jax import lax
from jax.experimental import pallas as pl
from jax.experimental.pallas import tpu as pltpu

F32 = jnp.float32
BF16 = jnp.bfloat16
I32 = jnp.int32

DEPTH = 1
N_META = 16
CHUNK = 64
ATT_HEADS = 4
HEAD_DIM = 64
HEAD_W = 2 * HEAD_DIM
ATT_W = ATT_HEADS * HEAD_W
SSM_W = 512
SSM_CG = 16
SSM_G = SSM_W // SSM_CG
SSM_N = 64
N_BUCKETS = 32
MAX_DISTANCE = 128
N_EXPERTS = 32
TOP_K = 4
SWIGLU_LIMIT = 7.0
SWIGLU_ALPHA = 1.702
LN_EPS = 1e-5
NEG_INF = -1e30
DEEPNORM_ALPHA = (2.0 * DEPTH) ** 0.25
LOG2E = 1.4426950408889634

Q_TILE = 512
KV_TILE = 512
V_ONES = 16
V_ROWS = HEAD_W + V_ONES
META_PAD = 128
SSM_T = 64
MIX_TILE = 512
ROW_TILE = 256
MOE_BLOCK = 512
VMEM_LIMIT = 56 * 1024 * 1024


def _layer_norm(x, g, b):
    mu = jnp.mean(x, axis=-1, keepdims=True)
    xc = x - mu
    var = jnp.mean(xc * xc, axis=-1, keepdims=True)
    return xc * lax.rsqrt(var + LN_EPS) * g + b


def _dot(a, b):
    return jnp.dot(a, b, preferred_element_type=F32)


def _dot_nt(a, b, precision=None):
    return lax.dot_general(a, b, (((1,), (1,)), ((), ())), precision=precision,
                           preferred_element_type=F32)


def _inproj_kernel(x_ref, g_ref, b_ref, w_ref, q_ref, k_ref, vt_ref, u_ref, *, kv_tile, q_scale):
    h = _layer_norm(x_ref[0], g_ref[...], b_ref[...]).astype(BF16)
    tm = h.shape[0]
    q_ref[0] = (_dot(h, w_ref[:, 0:ATT_W]) * q_scale).astype(BF16)
    k_ref[0] = _dot(h, w_ref[:, ATT_W:2 * ATT_W]).astype(BF16)
    v = _dot(h, w_ref[:, 2 * ATT_W:3 * ATT_W])
    ones = jnp.ones((V_ONES, kv_tile), BF16)
    for hh in range(ATT_HEADS):
        vt = v[:, hh * HEAD_W:(hh + 1) * HEAD_W].T.astype(BF16)
        for j in range(tm // kv_tile):
            vt_ref[0, hh, j, 0:HEAD_W, :] = vt[:, j * kv_tile:(j + 1) * kv_tile]
            vt_ref[0, hh, j, HEAD_W:V_ROWS, :] = ones
    u_ref[0] = _dot(h, w_ref[:, 3 * ATT_W:]).astype(BF16)


def _inproj(x, g, b, w, *, tm, kv_tile):
    bsz, s, d = x.shape
    q_scale = HEAD_DIM ** -0.5 * LOG2E
    n_cols = w.shape[1]
    row = lambda bi, i: (bi, i, 0)
    return pl.pallas_call(
        functools.partial(_inproj_kernel, kv_tile=kv_tile, q_scale=q_scale),
        grid=(bsz, s // tm),
        in_specs=[
            pl.BlockSpec((1, tm, d), row),
            pl.BlockSpec((1, d), lambda bi, i: (0, 0)),
            pl.BlockSpec((1, d), lambda bi, i: (0, 0)),
            pl.BlockSpec((d, n_cols), lambda bi, i: (0, 0)),
        ],
        out_specs=[
            pl.BlockSpec((1, tm, ATT_W), row),
            pl.BlockSpec((1, tm, ATT_W), row),
            pl.BlockSpec((1, ATT_HEADS, tm // kv_tile, V_ROWS, kv_tile), lambda bi, i: (bi, 0, i, 0, 0)),
            pl.BlockSpec((1, tm, SSM_W), row),
        ],
        out_shape=[
            jax.ShapeDtypeStruct((bsz, s, ATT_W), BF16),
            jax.ShapeDtypeStruct((bsz, s, ATT_W), BF16),
            jax.ShapeDtypeStruct((bsz, ATT_HEADS, s // kv_tile, V_ROWS, kv_tile), BF16),
            jax.ShapeDtypeStruct((bsz, s, SSM_W), BF16),
        ],
        compiler_params=pltpu.CompilerParams(
            dimension_semantics=("parallel", "parallel"), vmem_limit_bytes=VMEM_LIMIT),
        name="inproj",
    )(x, g, b, w)


def _attn_kernel(lam_ref, q_ref, k_ref, vt_ref, km_ref, vtm_ref, bias_ref, g_ref, o_ref,
                 m_ref, acc_ref, qz_ref, sa_ref, sb_ref, sn_ref, *, out_scale):
    i = pl.program_id(2)
    q = q_ref[0]
    lane = lax.broadcasted_iota(I32, q.shape, 1)
    zero = jnp.zeros_like(q)
    qz_ref[0:Q_TILE, :] = jnp.where(lane < HEAD_DIM, q, zero)
    qz_ref[Q_TILE:, :] = jnp.where(lane >= HEAD_DIM, q, zero)

    m_ref[...] = jnp.full(m_ref.shape, NEG_INF, F32)
    acc_ref[...] = jnp.zeros(acc_ref.shape, F32)

    def scores(kt):
        return _dot_nt(kt, qz_ref[...])

    def k_tile(t):
        return k_ref[0, pl.ds(pl.multiple_of(t * KV_TILE, KV_TILE), KV_TILE), :]

    def absorb(s, pv):
        m_old = m_ref[...]
        m_new = jnp.maximum(m_old, jnp.max(s, axis=0, keepdims=True))
        alpha = jnp.exp2(m_old - m_new)
        p = jnp.exp2(s - m_new).astype(BF16)
        acc_ref[...] = acc_ref[...] * alpha + pv(p)
        m_ref[...] = m_new

    def tile_pv(t):
        return lambda p: _dot(vt_ref[0, 0, t], p)

    n_far = jnp.maximum(i - 1, 0)
    peel = n_far % 2

    @pl.when(peel == 1)
    def _():
        absorb(scores(k_tile(0)), tile_pv(0))

    sa_ref[...] = scores(k_tile(peel))

    def far_pair(j, carry):
        t0 = peel + 2 * j
        sb_ref[...] = scores(k_tile(t0 + 1))
        absorb(sa_ref[...], tile_pv(t0))
        sa_ref[...] = scores(k_tile(t0 + 2))
        absorb(sb_ref[...], tile_pv(t0 + 1))
        return carry

    lax.fori_loop(0, n_far // 2, far_pair, 0)

    sb_ref[...] = scores(k_tile(i))
    absorb(sa_ref[...] + bias_ref[0, 0, 0:KV_TILE, :], tile_pv(n_far))
    sn_ref[...] = scores(km_ref[...])
    absorb(sb_ref[...] + bias_ref[0, 0, KV_TILE:2 * KV_TILE, :], tile_pv(i))
    absorb(sn_ref[...] + bias_ref[0, 0, 2 * KV_TILE:, :], lambda p: _dot(vtm_ref[0], p))

    acc = acc_ref[...]
    lam = lam_ref[0]
    o1 = acc[0:HEAD_W, 0:Q_TILE] / acc[HEAD_W:HEAD_W + 1, 0:Q_TILE]
    o2 = acc[0:HEAD_W, Q_TILE:] / acc[HEAD_W:HEAD_W + 1, Q_TILE:]
    o = o1 - lam * o2
    ms = jnp.mean(o * o, axis=0, keepdims=True)
    o = o * lax.rsqrt(ms + LN_EPS) * g_ref[...] * out_scale
    o_ref[0] = o.T.astype(o_ref.dtype)


def _attention(lam, q, k, vt, k_meta, vt_meta, bias, subln_g, *, lam_init):
    bsz, s, _ = q.shape
    nq = s // Q_TILE
    n_near = META_PAD + 2 * KV_TILE
    grid_spec = pltpu.PrefetchScalarGridSpec(
        num_scalar_prefetch=1,
        grid=(bsz, ATT_HEADS, nq),
        in_specs=[
            pl.BlockSpec((1, Q_TILE, HEAD_W), lambda b, h, i, lam: (b, i, h)),
            pl.BlockSpec((1, s, HEAD_W), lambda b, h, i, lam: (b, 0, h)),
            pl.BlockSpec((1, 1, s // KV_TILE, V_ROWS, KV_TILE), lambda b, h, i, lam: (b, h, 0, 0, 0)),
            pl.BlockSpec((META_PAD, HEAD_W), lambda b, h, i, lam: (0, h)),
            pl.BlockSpec((1, V_ROWS, META_PAD), lambda b, h, i, lam: (h, 0, 0)),
            pl.BlockSpec((1, 1, n_near, 2 * Q_TILE), lambda b, h, i, lam: (h, jnp.minimum(i, 1), 0, 0)),
            pl.BlockSpec((HEAD_W, 1), lambda b, h, i, lam: (0, 0)),
        ],
        out_specs=pl.BlockSpec((1, Q_TILE, HEAD_W), lambda b, h, i, lam: (b, i, h)),
        scratch_shapes=[
            pltpu.VMEM((1, 2 * Q_TILE), F32), pltpu.VMEM((V_ROWS, 2 * Q_TILE), F32),
            pltpu.VMEM((2 * Q_TILE, HEAD_W), BF16),
            pltpu.VMEM((KV_TILE, 2 * Q_TILE), F32), pltpu.VMEM((KV_TILE, 2 * Q_TILE), F32),
            pltpu.VMEM((META_PAD, 2 * Q_TILE), F32),
        ],
    )
    return pl.pallas_call(
        functools.partial(_attn_kernel, out_scale=1.0 - lam_init),
        grid_spec=grid_spec,
        out_shape=jax.ShapeDtypeStruct((bsz, s, ATT_W), BF16),
        compiler_params=pltpu.CompilerParams(
            dimension_semantics=("parallel", "parallel", "arbitrary"), vmem_limit_bytes=VMEM_LIMIT),
        name="diff_attention",
    )(lam, q, k, vt, k_meta, vt_meta, bias, subln_g)


def _t5_bucket(rel):
    nb = N_BUCKETS // 2
    max_exact = nb // 2
    ret = jnp.where(rel > 0, nb, 0)
    n = jnp.abs(rel)
    n_f = jnp.maximum(n, 1).astype(F32)
    large = max_exact + (jnp.log(n_f / max_exact) / math.log(MAX_DISTANCE / max_exact)
                         * (nb - max_exact)).astype(I32)
    large = jnp.minimum(large, nb - 1)
    return ret + jnp.where(n < max_exact, n, large)


def _near_bias(rel_bias):
    table = rel_bias.astype(F32)
    far = table[N_BUCKETS // 2 - 1]
    c = jnp.arange(Q_TILE, dtype=I32)[None, :]
    r = jnp.arange(KV_TILE, dtype=I32)[:, None]

    shifted = jnp.transpose(table - far) * LOG2E

    def bias_of(offset, rows):
        rel = jnp.arange(rows, dtype=I32)[:, None] - c + offset
        hot = _t5_bucket(rel)[None, :, :, None] == jnp.arange(N_BUCKETS, dtype=I32)
        return jnp.sum(jnp.where(hot, shifted[:, None, None, :], 0.0), axis=-1)

    own = jnp.where((r // CHUNK <= c // CHUNK)[None], bias_of(0, KV_TILE), NEG_INF)
    prev = bias_of(-KV_TILE, KV_TILE)
    rm = jnp.arange(META_PAD, dtype=I32)[:, None]
    meta_ok = (rm < N_META)[None]
    meta0 = jnp.where(meta_ok, bias_of(-N_META, META_PAD), NEG_INF)
    meta1 = jnp.where(meta_ok, bias_of(-N_META - Q_TILE, META_PAD), NEG_INF)
    v0 = jnp.concatenate([jnp.full_like(prev, NEG_INF), own, meta0], axis=1)
    v1 = jnp.concatenate([prev, own, meta1], axis=1)
    both = jnp.stack([v0, v1], axis=1)
    return jnp.concatenate([both, both], axis=-1)


def _ssm_tables(a_re, a_im, log_step, b_re, b_im, c_re, c_im, d_skip):
    hi = lax.Precision.HIGHEST
    t_len = SSM_T
    step = jnp.exp(log_step.astype(F32))[:, None]
    ar = jnp.minimum(a_re.astype(F32), -1e-4)
    ai = a_im.astype(F32)
    mag = jnp.exp(step * ar)
    ph = step * ai
    abar_re = mag * jnp.cos(ph)
    abar_im = mag * jnp.sin(ph)
    den = ar * ar + ai * ai
    e_re = abar_re - 1.0
    e_im = abar_im
    f_re = (e_re * ar + e_im * ai) / den
    f_im = (e_im * ar - e_re * ai) / den
    br = b_re.astype(F32)
    bi = b_im.astype(F32)
    bb_re = f_re[..., None] * br - f_im[..., None] * bi
    bb_im = f_re[..., None] * bi + f_im[..., None] * br
    tau = jnp.arange(t_len + 1, dtype=F32)[None, :, None]
    pmag = jnp.exp(tau * (step * ar)[:, None, :])
    pph = tau * ph[:, None, :]
    pw_re = pmag * jnp.cos(pph)
    pw_im = pmag * jnp.sin(pph)
    w_re = pw_re[..., None] * bb_re[:, None] - pw_im[..., None] * bb_im[:, None]
    w_im = pw_re[..., None] * bb_im[:, None] + pw_im[..., None] * bb_re[:, None]
    cr = c_re.astype(F32)
    ci = c_im.astype(F32)
    kern = (jnp.einsum('gcn,gtnk->gtck', cr, w_re[:, :t_len], precision=hi)
            - jnp.einsum('gcn,gtnk->gtck', ci, w_im[:, :t_len], precision=hi))
    skip = d_skip.astype(F32).reshape(SSM_G, SSM_CG)
    kern = kern.at[:, 0].add(skip[:, :, None] * jnp.eye(SSM_CG, dtype=F32))
    kflat = jnp.transpose(kern, (0, 3, 1, 2)).reshape(SSM_G, SSM_CG, t_len * SSM_CG)
    kpad = jnp.concatenate([jnp.zeros_like(kflat), kflat], axis=-1)
    pt_re = jnp.transpose(jnp.flip(w_re[:, :t_len], axis=1), (0, 1, 3, 2)).reshape(SSM_G, t_len * SSM_CG, SSM_N)
    pt_im = jnp.transpose(jnp.flip(w_im[:, :t_len], axis=1), (0, 1, 3, 2)).reshape(SSM_G, t_len * SSM_CG, SSM_N)
    pt = jnp.concatenate([pt_re, pt_im], axis=-1)
    up_re = pw_re[:, 1:]
    up_im = pw_im[:, 1:]
    q_re = cr[:, None] * up_re[:, :, None, :] - ci[:, None] * up_im[:, :, None, :]
    q_im = cr[:, None] * up_im[:, :, None, :] + ci[:, None] * up_re[:, :, None, :]
    qt = jnp.concatenate([jnp.transpose(q_re, (0, 3, 1, 2)), -jnp.transpose(q_im, (0, 3, 1, 2))], axis=1)
    qt = qt.reshape(SSM_G, 2 * SSM_N, t_len * SSM_CG)
    at_re = pw_re[:, t_len]
    at_im = pw_im[:, t_len]
    a1 = jnp.concatenate([at_re, at_re], axis=-1)
    a2 = jnp.concatenate([-at_im, at_im], axis=-1)
    return kpad, pt.astype(BF16), qt.astype(BF16), a1, a2


def _ssm_state_kernel(u_ref, pt_ref, s_ref):
    s_ref[0] = _dot(u_ref[0], pt_ref[0])


def _ssm_scan_kernel(s_ref, a1_ref, a2_ref, x_ref, *, n_chunks, bsz):
    a1 = a1_ref[...][:, None, :]
    a2 = a2_ref[...][:, None, :]

    def body(c, x):
        rows = pl.ds(c * bsz, bsz)
        x_ref[:, rows, :] = x
        return a1 * x + a2 * pltpu.roll(x, SSM_N, 2) + s_ref[:, rows, :]

    lax.fori_loop(0, n_chunks, body, jnp.zeros((SSM_G, bsz, 2 * SSM_N), F32))


def _ssm_out_kernel(u_ref, x_ref, kp_ref, qt_ref, y_ref, mt_ref):
    kp = kp_ref[0]
    width = SSM_T * SSM_CG
    lane_tile = 128
    for sub in range(0, lane_tile, SSM_CG):
        shifted = kp if sub == 0 else pltpu.roll(kp, 2 * width - sub, 1)
        for s in range(SSM_T):
            off = (SSM_T - s) * SSM_CG
            if off % lane_tile == sub:
                base = off - sub
                mt_ref[s * SSM_CG:(s + 1) * SSM_CG, :] = shifted[:, base:base + width].astype(BF16)
    x = x_ref[0]
    x_hi = x.astype(BF16)
    x_lo = (x - x_hi.astype(F32)).astype(BF16)
    y = _dot(u_ref[0], mt_ref[...]) + _dot(x_hi, qt_ref[0]) + _dot(x_lo, qt_ref[0])
    y_ref[0] = y.astype(y_ref.dtype)


def _ssm(ug, kpad, pt, qt, a1, a2, *, n_chunks, bsz):
    g, r, w = ug.shape
    n2 = 2 * SSM_N
    per_g = lambda gi: (gi, 0, 0)
    params = pltpu.CompilerParams(dimension_semantics=("parallel",), vmem_limit_bytes=VMEM_LIMIT)
    s = pl.pallas_call(
        _ssm_state_kernel,
        grid=(g,),
        in_specs=[pl.BlockSpec((1, r, w), per_g), pl.BlockSpec((1, w, n2), per_g)],
        out_specs=pl.BlockSpec((1, r, n2), per_g),
        out_shape=jax.ShapeDtypeStruct((g, r, n2), F32),
        compiler_params=params,
        name="ssm_chunk_state",
    )(ug, pt)
    x = pl.pallas_call(
        functools.partial(_ssm_scan_kernel, n_chunks=n_chunks, bsz=bsz),
        out_shape=jax.ShapeDtypeStruct((g, r, n2), F32),
        compiler_params=pltpu.CompilerParams(vmem_limit_bytes=VMEM_LIMIT),
        name="ssm_chunk_scan",
    )(s, a1, a2)
    return pl.pallas_call(
        _ssm_out_kernel,
        grid=(g,),
        in_specs=[pl.BlockSpec((1, r, w), per_g), pl.BlockSpec((1, r, n2), per_g),
                  pl.BlockSpec((1, SSM_CG, 2 * w), per_g), pl.BlockSpec((1, n2, w), per_g)],
        out_specs=pl.BlockSpec((1, r, w), per_g),
        out_shape=jax.ShapeDtypeStruct((g, r, w), BF16),
        scratch_shapes=[pltpu.VMEM((w, w), BF16)],
        compiler_params=params,
        name="ssm_output",
    )(ug, x, kpad, qt)


def _mix_kernel(x_ref, gi_ref, bi_ref, att_ref, y_ref, wglu_ref, bglu_ref, wout_ref, g1_ref, b1_ref,
                wr_ref, br_ref, h1_ref, idx_ref, gate_ref, rank_ref, cnt_ref, carry_ref):
    step = pl.program_id(0)

    @pl.when(step == 0)
    def _():
        carry_ref[...] = jnp.zeros(carry_ref.shape, F32)

    h0 = _layer_norm(x_ref[...], gi_ref[...], bi_ref[...])
    y = y_ref[...].astype(F32)
    y = y * (0.5 * (1.0 + jnp.tanh(math.sqrt(2.0 / math.pi) * (y + 0.044715 * (y * y * y)))))
    y = y * jax.nn.sigmoid(_dot(y.astype(BF16), wglu_ref[...]) + bglu_ref[...])
    mix = _dot(att_ref[...], wout_ref[0:ATT_W, :]) + _dot(y.astype(BF16), wout_ref[ATT_W:, :])
    h1 = _layer_norm(DEEPNORM_ALPHA * h0 + mix, g1_ref[...], b1_ref[...])
    h1_ref[...] = h1

    logits = _dot_nt(wr_ref[...], h1, precision=lax.Precision.HIGHEST) + br_ref[...]
    tm = logits.shape[1]
    eidx = lax.broadcasted_iota(I32, logits.shape, 0)
    vals, hots = [], []
    rest = logits
    for _ in range(TOP_K):
        mx = jnp.max(rest, axis=0, keepdims=True)
        first = jnp.min(jnp.where(rest == mx, eidx, N_EXPERTS), axis=0, keepdims=True)
        hot = eidx == first
        vals.append(mx)
        hots.append(hot)
        rest = jnp.where(hot, -jnp.inf, rest)
    exps = [jnp.exp(v - vals[0]) for v in vals]
    denom = exps[0] + exps[1] + exps[2] + exps[3]
    gate_ref[...] = jnp.concatenate([e / denom for e in exps], axis=0)
    idx_ref[...] = jnp.concatenate(
        [jnp.sum(jnp.where(h, eidx, 0), axis=0, keepdims=True) for h in hots], axis=0)

    hot_all = (hots[0] | hots[1] | hots[2] | hots[3]).astype(F32)
    tri = (lax.broadcasted_iota(I32, (tm, tm), 0) < lax.broadcasted_iota(I32, (tm, tm), 1)).astype(BF16)
    before = _dot(hot_all.astype(BF16), tri) + carry_ref[...]
    rank_ref[...] = jnp.concatenate(
        [jnp.sum(jnp.where(h, before, 0.0), axis=0, keepdims=True) for h in hots], axis=0).astype(I32)
    carry_ref[...] = carry_ref[...] + jnp.sum(hot_all, axis=1, keepdims=True)
    cnt_ref[...] = jnp.broadcast_to(carry_ref[...], cnt_ref.shape)


def _mix(x2, gi, bi, att, y, wglu, bglu, wout, g1, b1, wr_t, br):
    t, d = x2.shape
    tm = MIX_TILE
    row = lambda i: (i, 0)
    col = lambda i: (0, i)
    const = lambda i: (0, 0)
    return pl.pallas_call(
        _mix_kernel,
        grid=(t // tm,),
        in_specs=[
            pl.BlockSpec((tm, d), row), pl.BlockSpec((1, d), const), pl.BlockSpec((1, d), const),
            pl.BlockSpec((tm, ATT_W), row), pl.BlockSpec((tm, SSM_W), row),
            pl.BlockSpec((SSM_W, SSM_W), const), pl.BlockSpec((1, SSM_W), const),
            pl.BlockSpec((d, d), const), pl.BlockSpec((1, d), const), pl.BlockSpec((1, d), const),
            pl.BlockSpec((N_EXPERTS, d), const), pl.BlockSpec((N_EXPERTS, 1), const),
        ],
        out_specs=[
            pl.BlockSpec((tm, d), row),
            pl.BlockSpec((TOP_K, tm), col), pl.BlockSpec((TOP_K, tm), col), pl.BlockSpec((TOP_K, tm), col),
            pl.BlockSpec((N_EXPERTS, 128), const),
        ],
        out_shape=[
            jax.ShapeDtypeStruct((t, d), F32),
            jax.ShapeDtypeStruct((TOP_K, t), I32), jax.ShapeDtypeStruct((TOP_K, t), F32),
            jax.ShapeDtypeStruct((TOP_K, t), I32),
            jax.ShapeDtypeStruct((N_EXPERTS, 128), F32),
        ],
        scratch_shapes=[pltpu.VMEM((N_EXPERTS, 1), F32)],
        compiler_params=pltpu.CompilerParams(dimension_semantics=("arbitrary",), vmem_limit_bytes=VMEM_LIMIT),
        name="mix_ln1_router",
    )(x2, gi, bi, att, y, wglu, bglu, wout, g1, b1, wr_t, br)


def _row_copy(src, src_row, dst, dst_row, sem):
    return pltpu.make_async_copy(src.at[pl.ds(src_row, 1), :], dst.at[pl.ds(dst_row, 1), :], sem)


def _dispatch_kernel(pad_start_ref, pad_count_ref, n_used_ref, dest_ref, h1_ref, xs_hbm, zero_ref, sem, zsem):
    step = pl.program_id(0)
    tm = dest_ref.shape[1]
    n_blocks = xs_hbm.shape[0] // MOE_BLOCK

    def zero_block(blk):
        return pltpu.make_async_copy(zero_ref, xs_hbm.at[pl.ds(blk * MOE_BLOCK, MOE_BLOCK), :], zsem)

    @pl.when(step == 0)
    def _():
        zero_ref[...] = jnp.zeros(zero_ref.shape, F32)
        for e in range(N_EXPERTS):
            start = pad_start_ref[e]
            count = pad_count_ref[e]

            def fill(r, c):
                _row_copy(zero_ref, 0, xs_hbm, start + r, zsem).start()
                return c

            def drain(r, c):
                _row_copy(zero_ref, 0, xs_hbm, start, zsem).wait()
                return c

            lax.fori_loop(0, count, fill, 0)
            lax.fori_loop(0, count, drain, 0)

        def fill_block(blk, c):
            zero_block(blk).start()
            return c

        def drain_block(blk, c):
            zero_block(blk).wait()
            return c

        lax.fori_loop(n_used_ref[0], n_blocks, fill_block, 0)
        lax.fori_loop(n_used_ref[0], n_blocks, drain_block, 0)

    def issue(r, c):
        for k in range(TOP_K):
            _row_copy(h1_ref, r, xs_hbm, dest_ref[k, r], sem).start(priority=k % 2)
        return c

    lax.fori_loop(0, tm, issue, 0, unroll=8)
    for _ in range(TOP_K):
        pltpu.make_async_copy(h1_ref, xs_hbm.at[pl.ds(0, tm), :], sem).wait()


def _dispatch(pad_start, pad_count, n_used, dest_t, h1, n_rows):
    t, d = h1.shape
    tm = ROW_TILE
    grid_spec = pltpu.PrefetchScalarGridSpec(
        num_scalar_prefetch=3,
        grid=(t // tm,),
        in_specs=[
            pl.BlockSpec((TOP_K, tm), lambda i, ps, pc, nu: (0, i), memory_space=pltpu.SMEM),
            pl.BlockSpec((tm, d), lambda i, ps, pc, nu: (i, 0)),
        ],
        out_specs=pl.BlockSpec(memory_space=pl.ANY),
        scratch_shapes=[pltpu.VMEM((MOE_BLOCK, d), F32), pltpu.SemaphoreType.DMA, pltpu.SemaphoreType.DMA],
    )
    return pl.pallas_call(
        _dispatch_kernel,
        grid_spec=grid_spec,
        out_shape=jax.ShapeDtypeStruct((n_rows, d), F32),
        compiler_params=pltpu.CompilerParams(dimension_semantics=("arbitrary",), has_side_effects=True),
        name="moe_dispatch",
    )(pad_start, pad_count, n_used, dest_t, h1)


def _expert_kernel(be_ref, nb_ref, x_ref, wg_ref, bg_ref, wu_ref, bu_ref, wd_ref, bd_ref, y_ref,
                   wg_b, wu_b, wd_b):
    i = pl.program_id(0)

    @pl.when(i < nb_ref[0])
    def _():
        @pl.when((i == 0) | (be_ref[i] != be_ref[jnp.maximum(i - 1, 0)]))
        def _():
            wg_b[...] = wg_ref[0].astype(BF16)
            wu_b[...] = wu_ref[0].astype(BF16)
            wd_b[...] = wd_ref[0].astype(BF16)

        x = x_ref[...].astype(BF16)
        gate = _dot(x, wg_b[...]) + bg_ref[0]
        up = _dot(x, wu_b[...]) + bu_ref[0]
        gate = jnp.minimum(gate, SWIGLU_LIMIT)
        up = jnp.clip(up, -SWIGLU_LIMIT, SWIGLU_LIMIT)
        act = (up + 1.0) * gate * jax.nn.sigmoid(gate * SWIGLU_ALPHA)
        y_ref[...] = _dot(act.astype(BF16), wd_b[...]) + bd_ref[0]

    @pl.when(i >= nb_ref[0])
    def _():
        y_ref[...] = jnp.zeros(y_ref.shape, F32)


def _experts(block_expert, n_used, xs, wg, bg, wu, bu, wd, bd):
    n_rows, d = xs.shape
    dff = wg.shape[2]
    nb = n_rows // MOE_BLOCK
    blk = lambda i, be, nu: (jnp.minimum(i, nu[0] - 1), 0)
    wsel = lambda i, be, nu: (be[jnp.minimum(i, nu[0] - 1)], 0, 0)
    grid_spec = pltpu.PrefetchScalarGridSpec(
        num_scalar_prefetch=2,
        grid=(nb,),
        in_specs=[
            pl.BlockSpec((MOE_BLOCK, d), blk),
            pl.BlockSpec((1, d, dff), wsel), pl.BlockSpec((1, 1, dff), wsel),
            pl.BlockSpec((1, d, dff), wsel), pl.BlockSpec((1, 1, dff), wsel),
            pl.BlockSpec((1, dff, d), wsel), pl.BlockSpec((1, 1, d), wsel),
        ],
        out_specs=pl.BlockSpec((MOE_BLOCK, d), lambda i, be, nu: (i, 0)),
        scratch_shapes=[pltpu.VMEM((d, dff), BF16), pltpu.VMEM((d, dff), BF16), pltpu.VMEM((dff, d), BF16)],
    )
    return pl.pallas_call(
        _expert_kernel,
        grid_spec=grid_spec,
        out_shape=jax.ShapeDtypeStruct((n_rows, d), F32),
        compiler_params=pltpu.CompilerParams(dimension_semantics=("arbitrary",), vmem_limit_bytes=VMEM_LIMIT),
        name="moe_experts",
    )(block_expert, n_used, xs, wg, bg, wu, bu, wd, bd)


def _combine_kernel(dest_ref, dest_next_ref, h1_ref, gate_ref, ys_hbm, g2_ref, b2_ref, o_ref, buf, sems):
    step = pl.program_id(0)
    n_steps = pl.num_programs(0)
    tm = h1_ref.shape[0]
    slot = step % 2

    def issue(dref, s):
        def body(r, c):
            for k in range(TOP_K):
                pltpu.make_async_copy(ys_hbm.at[pl.ds(dref[k, r], 1), :], buf.at[s, k, pl.ds(r, 1), :],
                                      sems.at[s]).start(priority=k % 2)
            return c
        lax.fori_loop(0, tm, body, 0, unroll=8)

    @pl.when(step == 0)
    def _():
        issue(dest_ref, 0)

    @pl.when(step + 1 < n_steps)
    def _():
        issue(dest_next_ref, 1 - slot)

    for k in range(TOP_K):
        pltpu.make_async_copy(ys_hbm.at[pl.ds(0, tm), :], buf.at[slot, k], sems.at[slot]).wait()

    gate = gate_ref[...]
    ffn = buf[slot, 0] * gate[:, 0:1]
    for k in range(1, TOP_K):
        ffn = ffn + buf[slot, k] * gate[:, k:k + 1]
    o_ref[...] = _layer_norm(DEEPNORM_ALPHA * h1_ref[...] + ffn, g2_ref[...], b2_ref[...])


def _combine(dest_t, h1, gates, ys, g2, b2):
    t, d = h1.shape
    tm = ROW_TILE
    n_steps = t // tm
    row = lambda i: (i, 0)
    const = lambda i: (0, 0)
    return pl.pallas_call(
        _combine_kernel,
        grid=(n_steps,),
        in_specs=[
            pl.BlockSpec((TOP_K, tm), lambda i: (0, i), memory_space=pltpu.SMEM),
            pl.BlockSpec((TOP_K, tm), lambda i: (0, jnp.minimum(i + 1, n_steps - 1)), memory_space=pltpu.SMEM),
            pl.BlockSpec((tm, d), row), pl.BlockSpec((tm, TOP_K), row),
            pl.BlockSpec(memory_space=pl.ANY),
            pl.BlockSpec((1, d), const), pl.BlockSpec((1, d), const),
        ],
        out_specs=pl.BlockSpec((tm, d), row),
        out_shape=jax.ShapeDtypeStruct((t, d), F32),
        scratch_shapes=[pltpu.VMEM((2, TOP_K, tm, d), F32), pltpu.SemaphoreType.DMA((2,))],
        compiler_params=pltpu.CompilerParams(dimension_semantics=("arbitrary",), vmem_limit_bytes=VMEM_LIMIT),
        name="moe_combine_ln2",
    )(dest_t, dest_t, h1, gates, ys, g2, b2)


def kernel(x, meta_tokens, ln_in_g, ln_in_b, rel_bias, w_in, lambda_q1, lambda_k1, lambda_q2, lambda_k2,
           subln_g, a_re, a_im, log_step, b_re, b_im, c_re, c_im, d_skip, w_glu, b_glu, w_out, ln1_g, ln1_b,
           w_router, b_router, w_gate, b_gate, w_up, b_up, w_down, b_down, ln2_g, ln2_b):
    bsz, seq, dm = x.shape
    assert seq % 512 == 0 and w_in.shape[0] == DEPTH == 1
    layer = 0
    row2 = lambda v: v.astype(F32).reshape(1, -1)

    w_in_b = w_in[layer].astype(BF16)
    gi, bi = row2(ln_in_g), row2(ln_in_b)
    q, k, vt, u = _inproj(x, gi, bi, w_in_b, tm=512, kv_tile=KV_TILE)
    meta = jnp.zeros((1, META_PAD, dm), x.dtype).at[0, :N_META].set(meta_tokens.astype(x.dtype))
    _, k_meta, vt_meta, u_meta = _inproj(meta, gi, bi, w_in_b, tm=META_PAD, kv_tile=META_PAD)

    lam_init = 0.8 - 0.6 * math.exp(-0.3 * layer)
    lam = (jnp.exp(jnp.sum(lambda_q1[layer].astype(F32) * lambda_k1[layer].astype(F32)))
           - jnp.exp(jnp.sum(lambda_q2[layer].astype(F32) * lambda_k2[layer].astype(F32))) + lam_init)
    att = _attention(lam.reshape(1), q, k, vt, k_meta[0], vt_meta[0, :, 0], _near_bias(rel_bias),
                     subln_g[layer].astype(F32).reshape(HEAD_W, 1), lam_init=lam_init)

    kpad, pt, qt, a1, a2 = _ssm_tables(a_re[layer], a_im[layer], log_step[layer], b_re[layer], b_im[layer],
                                     c_re[layer], c_im[layer], d_skip[layer])
    n_chunks = -(-(seq // SSM_T + 1) // 8) * 8
    lead = jnp.zeros((bsz, SSM_T, SSM_W), BF16).at[:, SSM_T - N_META:].set(u_meta[:, :N_META])
    tail = jnp.zeros((bsz, n_chunks * SSM_T - SSM_T - seq, SSM_W), BF16)
    ug = jnp.concatenate([lead, u, tail], axis=1).reshape(bsz, n_chunks, SSM_T, SSM_G, SSM_CG)
    ug = jnp.transpose(ug, (3, 1, 0, 2, 4)).reshape(SSM_G, n_chunks * bsz, SSM_T * SSM_CG)
    yg = _ssm(ug, kpad, pt, qt, a1, a2, n_chunks=n_chunks, bsz=bsz)
    yg = jnp.transpose(yg.reshape(SSM_G, n_chunks, bsz, SSM_T, SSM_CG), (2, 1, 3, 0, 4))
    y_ssm = yg.reshape(bsz, n_chunks * SSM_T, SSM_W)[:, SSM_T:SSM_T + seq]

    t = bsz * seq
    h1, idx_t, gate_t, rank_t, cnt = _mix(
        x.reshape(t, dm), gi, bi, att.reshape(t, ATT_W), y_ssm.reshape(t, SSM_W),
        w_glu[layer].astype(BF16), row2(b_glu[layer]), w_out[layer].astype(BF16),
        row2(ln1_g[layer]), row2(ln1_b[layer]),
        jnp.transpose(w_router[layer].astype(F32)), b_router[layer].astype(F32).reshape(N_EXPERTS, 1))

    counts = cnt[:, 0].astype(I32)
    padded = (counts + MOE_BLOCK - 1) // MOE_BLOCK * MOE_BLOCK
    experts = jnp.arange(N_EXPERTS, dtype=I32)
    padded_end = jnp.sum(jnp.where(experts[:, None] <= experts[None, :], padded[:, None], 0), axis=0)
    padded_start = padded_end - padded
    dest_t = jnp.sum(jnp.where(idx_t[..., None] == experts, padded_start, 0), axis=-1) + rank_t
    n_blocks = t * TOP_K // MOE_BLOCK + N_EXPERTS
    block_row0 = jnp.arange(n_blocks, dtype=I32) * MOE_BLOCK
    block_expert = jnp.minimum(jnp.sum((padded_end[None, :] <= block_row0[:, None]).astype(I32), axis=1),
                               N_EXPERTS - 1)
    n_used = (padded_end[-1:] // MOE_BLOCK).astype(I32)

    xs = _dispatch((padded_start + counts).astype(I32), (padded - counts).astype(I32), n_used, dest_t, h1,
                   n_blocks * MOE_BLOCK)
    b3 = lambda v: v.astype(F32)[:, None, :]
    ys = _experts(block_expert, n_used, xs, w_gate[layer], b3(b_gate[layer]), w_up[layer], b3(b_up[layer]),
                  w_down[layer], b3(b_down[layer]))
    out = _combine(dest_t, h1, jnp.transpose(gate_t), ys, row2(ln2_g[layer]), row2(ln2_b[layer]))
    return out.reshape(bsz, seq, dm)
```

```python
import functools
import math

import jax
import jax.numpy as jnp
import numpy as np
from jax import lax
from jax.experimental import pallas as pl
from jax.experimental.pallas import tpu as pltpu

F32 = jnp.float32
BF16 = jnp.bfloat16
I32 = jnp.int32

DEPTH = 1
N_META = 16
CHUNK = 64
ATT_HEADS = 4
HEAD_DIM = 64
HEAD_W = 2 * HEAD_DIM
ATT_W = ATT_HEADS * HEAD_W
SSM_W = 512
SSM_CG = 16
SSM_G = SSM_W // SSM_CG
SSM_N = 64
N_BUCKETS = 32
MAX_DISTANCE = 128
N_EXPERTS = 32
TOP_K = 4
SWIGLU_LIMIT = 7.0
SWIGLU_ALPHA = 1.702
LN_EPS = 1e-5
NEG_INF = -1e30
DEEPNORM_ALPHA = (2.0 * DEPTH) ** 0.25
LOG2E = 1.4426950408889634

Q_TILE = 512
KV_TILE = 512
V_ONES = 16
V_ROWS = HEAD_W + V_ONES
META_PAD = 128
SSM_T = 64
MIX_TILE = 512
ROW_TILE = 256
MOE_BLOCK = 512
GROUP = 8
SORT_ROWS = 1280
SORT_GROUPS = SORT_ROWS // GROUP
assert SORT_ROWS >= ROW_TILE * TOP_K + N_EXPERTS * (GROUP - 1)
VMEM_LIMIT = 56 * 1024 * 1024


def _layer_norm(x, g, b):
    mu = jnp.mean(x, axis=-1, keepdims=True)
    xc = x - mu
    var = jnp.mean(xc * xc, axis=-1, keepdims=True)
    return xc * lax.rsqrt(var + LN_EPS) * g + b


def _dot(a, b):
    return jnp.dot(a, b, preferred_element_type=F32)


def _dot_nt(a, b, precision=None):
    return lax.dot_general(a, b, (((1,), (1,)), ((), ())), precision=precision,
                           preferred_element_type=F32)


def _inproj_kernel(x_ref, g_ref, b_ref, w_ref, q_ref, k_ref, vt_ref, u_ref, *, kv_tile, q_scale):
    h = _layer_norm(x_ref[0], g_ref[...], b_ref[...]).astype(BF16)
    tm = h.shape[0]
    q_ref[0] = (_dot(h, w_ref[:, 0:ATT_W]) * q_scale).astype(BF16)
    k_ref[0] = _dot(h, w_ref[:, ATT_W:2 * ATT_W]).astype(BF16)
    v = _dot(h, w_ref[:, 2 * ATT_W:3 * ATT_W])
    ones = jnp.ones((V_ONES, kv_tile), BF16)
    for hh in range(ATT_HEADS):
        vt = v[:, hh * HEAD_W:(hh + 1) * HEAD_W].T.astype(BF16)
        for j in range(tm // kv_tile):
            vt_ref[0, hh, j, 0:HEAD_W, :] = vt[:, j * kv_tile:(j + 1) * kv_tile]
            vt_ref[0, hh, j, HEAD_W:V_ROWS, :] = ones
    u_ref[0] = _dot(h, w_ref[:, 3 * ATT_W:]).astype(BF16)


def _inproj(x, g, b, w, *, tm, kv_tile):
    bsz, s, d = x.shape
    q_scale = HEAD_DIM ** -0.5 * LOG2E
    n_cols = w.shape[1]
    row = lambda bi, i: (bi, i, 0)
    return pl.pallas_call(
        functools.partial(_inproj_kernel, kv_tile=kv_tile, q_scale=q_scale),
        grid=(bsz, s // tm),
        in_specs=[
            pl.BlockSpec((1, tm, d), row),
            pl.BlockSpec((1, d), lambda bi, i: (0, 0)),
            pl.BlockSpec((1, d), lambda bi, i: (0, 0)),
            pl.BlockSpec((d, n_cols), lambda bi, i: (0, 0)),
        ],
        out_specs=[
            pl.BlockSpec((1, tm, ATT_W), row),
            pl.BlockSpec((1, tm, ATT_W), row),
            pl.BlockSpec((1, ATT_HEADS, tm // kv_tile, V_ROWS, kv_tile), lambda bi, i: (bi, 0, i, 0, 0)),
            pl.BlockSpec((1, tm, SSM_W), row),
        ],
        out_shape=[
            jax.ShapeDtypeStruct((bsz, s, ATT_W), BF16),
            jax.ShapeDtypeStruct((bsz, s, ATT_W), BF16),
            jax.ShapeDtypeStruct((bsz, ATT_HEADS, s // kv_tile, V_ROWS, kv_tile), BF16),
            jax.ShapeDtypeStruct((bsz, s, SSM_W), BF16),
        ],
        compiler_params=pltpu.CompilerParams(
            dimension_semantics=("parallel", "parallel"), vmem_limit_bytes=VMEM_LIMIT),
        name="inproj",
    )(x, g, b, w)


def _attn_kernel(lam_ref, q_ref, k_ref, vt_ref, km_ref, vtm_ref, bias_ref, g_ref, o_ref,
                 m_ref, acc_ref, qz_ref, sa_ref, sb_ref, sn_ref, *, out_scale):
    i = pl.program_id(2)
    q = q_ref[0]
    lane = lax.broadcasted_iota(I32, q.shape, 1)
    zero = jnp.zeros_like(q)
    qz_ref[0:Q_TILE, :] = jnp.where(lane < HEAD_DIM, q, zero)
    qz_ref[Q_TILE:, :] = jnp.where(lane >= HEAD_DIM, q, zero)

    m_ref[...] = jnp.full(m_ref.shape, NEG_INF, F32)
    acc_ref[...] = jnp.zeros(acc_ref.shape, F32)

    def scores(kt):
        return _dot_nt(kt, qz_ref[...])

    def k_tile(t):
        return k_ref[0, pl.ds(pl.multiple_of(t * KV_TILE, KV_TILE), KV_TILE), :]

    def absorb(s, pv):
        m_old = m_ref[...]
        m_new = jnp.maximum(m_old, jnp.max(s, axis=0, keepdims=True))
        alpha = jnp.exp2(m_old - m_new)
        p = jnp.exp2(s - m_new).astype(BF16)
        acc_ref[...] = acc_ref[...] * alpha + pv(p)
        m_ref[...] = m_new

    def tile_pv(t):
        return lambda p: _dot(vt_ref[0, 0, t], p)

    n_far = jnp.maximum(i - 1, 0)
    peel = n_far % 2

    @pl.when(peel == 1)
    def _():
        absorb(scores(k_tile(0)), tile_pv(0))

    sa_ref[...] = scores(k_tile(peel))

    def far_pair(j, carry):
        t0 = peel + 2 * j
        sb_ref[...] = scores(k_tile(t0 + 1))
        absorb(sa_ref[...], tile_pv(t0))
        sa_ref[...] = scores(k_tile(t0 + 2))
        absorb(sb_ref[...], tile_pv(t0 + 1))
        return carry

    lax.fori_loop(0, n_far // 2, far_pair, 0)

    sb_ref[...] = scores(k_tile(i))
    absorb(sa_ref[...] + bias_ref[0, 0, 0:KV_TILE, :], tile_pv(n_far))
    sn_ref[...] = scores(km_ref[...])
    absorb(sb_ref[...] + bias_ref[0, 0, KV_TILE:2 * KV_TILE, :], tile_pv(i))
    absorb(sn_ref[...] + bias_ref[0, 0, 2 * KV_TILE:, :], lambda p: _dot(vtm_ref[0], p))

    acc = acc_ref[...]
    lam = lam_ref[0]
    o1 = acc[0:HEAD_W, 0:Q_TILE] / acc[HEAD_W:HEAD_W + 1, 0:Q_TILE]
    o2 = acc[0:HEAD_W, Q_TILE:] / acc[HEAD_W:HEAD_W + 1, Q_TILE:]
    o = o1 - lam * o2
    ms = jnp.mean(o * o, axis=0, keepdims=True)
    o = o * lax.rsqrt(ms + LN_EPS) * g_ref[...] * out_scale
    o_ref[0] = o.T.astype(o_ref.dtype)


def _attention(lam, q, k, vt, k_meta, vt_meta, bias, subln_g, *, lam_init):
    bsz, s, _ = q.shape
    nq = s // Q_TILE
    n_near = META_PAD + 2 * KV_TILE
    grid_spec = pltpu.PrefetchScalarGridSpec(
        num_scalar_prefetch=1,
        grid=(bsz, ATT_HEADS, nq),
        in_specs=[
            pl.BlockSpec((1, Q_TILE, HEAD_W), lambda b, h, i, lam: (b, i, h)),
            pl.BlockSpec((1, s, HEAD_W), lambda b, h, i, lam: (b, 0, h)),
            pl.BlockSpec((1, 1, s // KV_TILE, V_ROWS, KV_TILE), lambda b, h, i, lam: (b, h, 0, 0, 0)),
            pl.BlockSpec((META_PAD, HEAD_W), lambda b, h, i, lam: (0, h)),
            pl.BlockSpec((1, V_ROWS, META_PAD), lambda b, h, i, lam: (h, 0, 0)),
            pl.BlockSpec((1, 1, n_near, 2 * Q_TILE), lambda b, h, i, lam: (h, jnp.minimum(i, 1), 0, 0)),
            pl.BlockSpec((HEAD_W, 1), lambda b, h, i, lam: (0, 0)),
        ],
        out_specs=pl.BlockSpec((1, Q_TILE, HEAD_W), lambda b, h, i, lam: (b, i, h)),
        scratch_shapes=[
            pltpu.VMEM((1, 2 * Q_TILE), F32), pltpu.VMEM((V_ROWS, 2 * Q_TILE), F32),
            pltpu.VMEM((2 * Q_TILE, HEAD_W), BF16),
            pltpu.VMEM((KV_TILE, 2 * Q_TILE), F32), pltpu.VMEM((KV_TILE, 2 * Q_TILE), F32),
            pltpu.VMEM((META_PAD, 2 * Q_TILE), F32),
        ],
    )
    return pl.pallas_call(
        functools.partial(_attn_kernel, out_scale=1.0 - lam_init),
        grid_spec=grid_spec,
        out_shape=jax.ShapeDtypeStruct((bsz, s, ATT_W), BF16),
        compiler_params=pltpu.CompilerParams(
            dimension_semantics=("parallel", "parallel", "arbitrary"), vmem_limit_bytes=VMEM_LIMIT),
        name="diff_attention",
    )(lam, q, k, vt, k_meta, vt_meta, bias, subln_g)


def _t5_bucket(rel):
    nb = N_BUCKETS // 2
    max_exact = nb // 2
    ret = jnp.where(rel > 0, nb, 0)
    n = jnp.abs(rel)
    n_f = jnp.maximum(n, 1).astype(F32)
    large = max_exact + (jnp.log(n_f / max_exact) / math.log(MAX_DISTANCE / max_exact)
                         * (nb - max_exact)).astype(I32)
    large = jnp.minimum(large, nb - 1)
    return ret + jnp.where(n < max_exact, n, large)


def _near_bias(rel_bias):
    table = rel_bias.astype(F32)
    far = table[N_BUCKETS // 2 - 1]
    c = jnp.arange(Q_TILE, dtype=I32)[None, :]
    r = jnp.arange(KV_TILE, dtype=I32)[:, None]

    shifted = jnp.transpose(table - far) * LOG2E

    def bias_of(offset, rows):
        rel = jnp.arange(rows, dtype=I32)[:, None] - c + offset
        hot = _t5_bucket(rel)[None, :, :, None] == jnp.arange(N_BUCKETS, dtype=I32)
        return jnp.sum(jnp.where(hot, shifted[:, None, None, :], 0.0), axis=-1)

    own = jnp.where((r // CHUNK <= c // CHUNK)[None], bias_of(0, KV_TILE), NEG_INF)
    prev = bias_of(-KV_TILE, KV_TILE)
    rm = jnp.arange(META_PAD, dtype=I32)[:, None]
    meta_ok = (rm < N_META)[None]
    meta0 = jnp.where(meta_ok, bias_of(-N_META, META_PAD), NEG_INF)
    meta1 = jnp.where(meta_ok, bias_of(-N_META - Q_TILE, META_PAD), NEG_INF)
    v0 = jnp.concatenate([jnp.full_like(prev, NEG_INF), own, meta0], axis=1)
    v1 = jnp.concatenate([prev, own, meta1], axis=1)
    both = jnp.stack([v0, v1], axis=1)
    return jnp.concatenate([both, both], axis=-1)


def _ssm_tables(a_re, a_im, log_step, b_re, b_im, c_re, c_im, d_skip):
    hi = lax.Precision.HIGHEST
    t_len = SSM_T
    step = jnp.exp(log_step.astype(F32))[:, None]
    ar = jnp.minimum(a_re.astype(F32), -1e-4)
    ai = a_im.astype(F32)
    mag = jnp.exp(step * ar)
    ph = step * ai
    abar_re = mag * jnp.cos(ph)
    abar_im = mag * jnp.sin(ph)
    den = ar * ar + ai * ai
    e_re = abar_re - 1.0
    e_im = abar_im
    f_re = (e_re * ar + e_im * ai) / den
    f_im = (e_im * ar - e_re * ai) / den
    br = b_re.astype(F32)
    bi = b_im.astype(F32)
    bb_re = f_re[..., None] * br - f_im[..., None] * bi
    bb_im = f_re[..., None] * bi + f_im[..., None] * br
    tau = jnp.arange(t_len + 1, dtype=F32)[None, :, None]
    pmag = jnp.exp(tau * (step * ar)[:, None, :])
    pph = tau * ph[:, None, :]
    pw_re = pmag * jnp.cos(pph)
    pw_im = pmag * jnp.sin(pph)
    w_re = pw_re[..., None] * bb_re[:, None] - pw_im[..., None] * bb_im[:, None]
    w_im = pw_re[..., None] * bb_im[:, None] + pw_im[..., None] * bb_re[:, None]
    cr = c_re.astype(F32)
    ci = c_im.astype(F32)
    kern = (jnp.einsum('gcn,gtnk->gtck', cr, w_re[:, :t_len], precision=hi)
            - jnp.einsum('gcn,gtnk->gtck', ci, w_im[:, :t_len], precision=hi))
    skip = d_skip.astype(F32).reshape(SSM_G, SSM_CG)
    kern = kern.at[:, 0].add(skip[:, :, None] * jnp.eye(SSM_CG, dtype=F32))
    kflat = jnp.transpose(kern, (0, 3, 1, 2)).reshape(SSM_G, SSM_CG, t_len * SSM_CG)
    kpad = jnp.concatenate([jnp.zeros_like(kflat), kflat], axis=-1)
    pt_re = jnp.transpose(jnp.flip(w_re[:, :t_len], axis=1), (0, 1, 3, 2)).reshape(SSM_G, t_len * SSM_CG, SSM_N)
    pt_im = jnp.transpose(jnp.flip(w_im[:, :t_len], axis=1), (0, 1, 3, 2)).reshape(SSM_G, t_len * SSM_CG, SSM_N)
    pt = jnp.concatenate([pt_re, pt_im], axis=-1)
    up_re = pw_re[:, 1:]
    up_im = pw_im[:, 1:]
    q_re = cr[:, None] * up_re[:, :, None, :] - ci[:, None] * up_im[:, :, None, :]
    q_im = cr[:, None] * up_im[:, :, None, :] + ci[:, None] * up_re[:, :, None, :]
    qt = jnp.concatenate([jnp.transpose(q_re, (0, 3, 1, 2)), -jnp.transpose(q_im, (0, 3, 1, 2))], axis=1)
    qt = qt.reshape(SSM_G, 2 * SSM_N, t_len * SSM_CG)
    at_re = pw_re[:, t_len]
    at_im = pw_im[:, t_len]
    a1 = jnp.concatenate([at_re, at_re], axis=-1)
    a2 = jnp.concatenate([-at_im, at_im], axis=-1)
    return kpad, pt.astype(BF16), qt.astype(BF16), a1, a2


def _ssm_state_kernel(u_ref, pt_ref, s_ref):
    s_ref[0] = _dot(u_ref[0], pt_ref[0])


def _ssm_scan_kernel(s_ref, a1_ref, a2_ref, x_ref, *, n_chunks, bsz):
    a1 = a1_ref[...][:, None, :]
    a2 = a2_ref[...][:, None, :]

    def body(c, x):
        rows = pl.ds(c * bsz, bsz)
        x_ref[:, rows, :] = x
        return a1 * x + a2 * pltpu.roll(x, SSM_N, 2) + s_ref[:, rows, :]

    lax.fori_loop(0, n_chunks, body, jnp.zeros((SSM_G, bsz, 2 * SSM_N), F32))


def _ssm_out_kernel(u_ref, x_ref, kp_ref, qt_ref, y_ref, mt_ref):
    kp = kp_ref[0]
    width = SSM_T * SSM_CG
    lane_tile = 128
    for sub in range(0, lane_tile, SSM_CG):
        shifted = kp if sub == 0 else pltpu.roll(kp, 2 * width - sub, 1)
        for s in range(SSM_T):
            off = (SSM_T - s) * SSM_CG
            if off % lane_tile == sub:
                base = off - sub
                mt_ref[s * SSM_CG:(s + 1) * SSM_CG, :] = shifted[:, base:base + width].astype(BF16)
    x = x_ref[0]
    x_hi = x.astype(BF16)
    x_lo = (x - x_hi.astype(F32)).astype(BF16)
    y = _dot(u_ref[0], mt_ref[...]) + _dot(x_hi, qt_ref[0]) + _dot(x_lo, qt_ref[0])
    y_ref[0] = y.astype(y_ref.dtype)


def _ssm(ug, kpad, pt, qt, a1, a2, *, n_chunks, bsz):
    g, r, w = ug.shape
    n2 = 2 * SSM_N
    per_g = lambda gi: (gi, 0, 0)
    params = pltpu.CompilerParams(dimension_semantics=("parallel",), vmem_limit_bytes=VMEM_LIMIT)
    s = pl.pallas_call(
        _ssm_state_kernel,
        grid=(g,),
        in_specs=[pl.BlockSpec((1, r, w), per_g), pl.BlockSpec((1, w, n2), per_g)],
        out_specs=pl.BlockSpec((1, r, n2), per_g),
        out_shape=jax.ShapeDtypeStruct((g, r, n2), F32),
        compiler_params=params,
        name="ssm_chunk_state",
    )(ug, pt)
    x = pl.pallas_call(
        functools.partial(_ssm_scan_kernel, n_chunks=n_chunks, bsz=bsz),
        out_shape=jax.ShapeDtypeStruct((g, r, n2), F32),
        compiler_params=pltpu.CompilerParams(vmem_limit_bytes=VMEM_LIMIT),
        name="ssm_chunk_scan",
    )(s, a1, a2)
    return pl.pallas_call(
        _ssm_out_kernel,
        grid=(g,),
        in_specs=[pl.BlockSpec((1, r, w), per_g), pl.BlockSpec((1, r, n2), per_g),
                  pl.BlockSpec((1, SSM_CG, 2 * w), per_g), pl.BlockSpec((1, n2, w), per_g)],
        out_specs=pl.BlockSpec((1, r, w), per_g),
        out_shape=jax.ShapeDtypeStruct((g, r, w), BF16),
        scratch_shapes=[pltpu.VMEM((w, w), BF16)],
        compiler_params=params,
        name="ssm_output",
    )(ug, x, kpad, qt)


def _mix_kernel(x_ref, gi_ref, bi_ref, att_ref, y_ref, wglu_ref, bglu_ref, wout_ref, g1_ref, b1_ref,
                wr_ref, br_ref, h1_ref, idx_ref, gate_ref, rank_ref, cnt_ref):
    h0 = _layer_norm(x_ref[...], gi_ref[...], bi_ref[...])
    y = y_ref[...].astype(F32)
    y = y * (0.5 * (1.0 + jnp.tanh(math.sqrt(2.0 / math.pi) * (y + 0.044715 * (y * y * y)))))
    y = y * jax.nn.sigmoid(_dot(y.astype(BF16), wglu_ref[...]) + bglu_ref[...])
    mix = _dot(att_ref[...], wout_ref[0:ATT_W, :]) + _dot(y.astype(BF16), wout_ref[ATT_W:, :])
    h1 = _layer_norm(DEEPNORM_ALPHA * h0 + mix, g1_ref[...], b1_ref[...])
    h1_ref[...] = h1

    logits = _dot_nt(wr_ref[...], h1, precision=lax.Precision.HIGHEST) + br_ref[...]
    tm = logits.shape[1]
    eidx = lax.broadcasted_iota(I32, logits.shape, 0)
    vals, hots = [], []
    rest = logits
    for _ in range(TOP_K):
        mx = jnp.max(rest, axis=0, keepdims=True)
        first = jnp.min(jnp.where(rest == mx, eidx, N_EXPERTS), axis=0, keepdims=True)
        hot = eidx == first
        vals.append(mx)
        hots.append(hot)
        rest = jnp.where(hot, -jnp.inf, rest)
    exps = [jnp.exp(v - vals[0]) for v in vals]
    denom = exps[0] + exps[1] + exps[2] + exps[3]
    gate_ref[...] = jnp.concatenate([e / denom for e in exps], axis=0)
    idx_ref[...] = jnp.concatenate(
        [jnp.sum(jnp.where(h, eidx, 0), axis=0, keepdims=True) for h in hots], axis=0)

    hot_all = (hots[0] | hots[1] | hots[2] | hots[3]).astype(F32)
    sub = ROW_TILE
    tri = (lax.broadcasted_iota(I32, (sub, sub), 0) < lax.broadcasted_iota(I32, (sub, sub), 1)).astype(BF16)
    for part in range(tm // sub):
        cols = slice(part * sub, (part + 1) * sub)
        before = _dot(hot_all[:, cols].astype(BF16), tri)
        rank_ref[:, cols] = jnp.concatenate(
            [jnp.sum(jnp.where(h[:, cols], before, 0.0), axis=0, keepdims=True) for h in hots], axis=0).astype(I32)
        cnt_ref[part] = jnp.broadcast_to(jnp.sum(hot_all[:, cols], axis=1, keepdims=True), cnt_ref.shape[1:])


def _mix(x2, gi, bi, att, y, wglu, bglu, wout, g1, b1, wr_t, br):
    t, d = x2.shape
    tm = MIX_TILE
    row = lambda i: (i, 0)
    col = lambda i: (0, i)
    const = lambda i: (0, 0)
    return pl.pallas_call(
        _mix_kernel,
        grid=(t // tm,),
        in_specs=[
            pl.BlockSpec((tm, d), row), pl.BlockSpec((1, d), const), pl.BlockSpec((1, d), const),
            pl.BlockSpec((tm, ATT_W), row), pl.BlockSpec((tm, SSM_W), row),
            pl.BlockSpec((SSM_W, SSM_W), const), pl.BlockSpec((1, SSM_W), const),
            pl.BlockSpec((d, d), const), pl.BlockSpec((1, d), const), pl.BlockSpec((1, d), const),
            pl.BlockSpec((N_EXPERTS, d), const), pl.BlockSpec((N_EXPERTS, 1), const),
        ],
        out_specs=[
            pl.BlockSpec((tm, d), row),
            pl.BlockSpec((TOP_K, tm), col), pl.BlockSpec((TOP_K, tm), col), pl.BlockSpec((TOP_K, tm), col),
            pl.BlockSpec((tm // ROW_TILE, N_EXPERTS, 128), lambda i: (i, 0, 0)),
        ],
        out_shape=[
            jax.ShapeDtypeStruct((t, d), F32),
            jax.ShapeDtypeStruct((TOP_K, t), I32), jax.ShapeDtypeStruct((TOP_K, t), F32),
            jax.ShapeDtypeStruct((TOP_K, t), I32),
            jax.ShapeDtypeStruct((t // ROW_TILE, N_EXPERTS, 128), F32),
        ],
        compiler_params=pltpu.CompilerParams(dimension_semantics=("parallel",), vmem_limit_bytes=VMEM_LIMIT),
        name="mix_ln1_router",
    )(x2, gi, bi, att, y, wglu, bglu, wout, g1, b1, wr_t, br)


def _group_copy(src, src_group, dst, dst_group, sem):
    return pltpu.make_async_copy(src.at[pl.ds(pl.multiple_of(src_group * GROUP, GROUP), GROUP), :],
                                 dst.at[pl.ds(pl.multiple_of(dst_group * GROUP, GROUP), GROUP), :], sem)


def _dispatch_kernel(ngroups_ref, pad_start_ref, pad_count_ref, n_used_ref, dstg_ref, pos_ref, h1_ref, xs_hbm,
                     sorted_ref, zero_ref, sems, zsem):
    step = pl.program_id(0)
    n_steps = pl.num_programs(0)
    slot = step % 2
    n_blocks = xs_hbm.shape[0] // MOE_BLOCK

    def zero_block(blk):
        return pltpu.make_async_copy(zero_ref, xs_hbm.at[pl.ds(blk * MOE_BLOCK, MOE_BLOCK), :], zsem)

    @pl.when(step == 0)
    def _():
        zero_ref[...] = jnp.zeros(zero_ref.shape, F32)
        for e in range(N_EXPERTS):
            start = pad_start_ref[e]
            count = pad_count_ref[e]

            def fill(j, c):
                _group_copy(zero_ref, 0, xs_hbm, start + j, zsem).start()
                return c

            def drain(j, c):
                _group_copy(zero_ref, 0, xs_hbm, start, zsem).wait()
                return c

            lax.fori_loop(0, count, fill, 0)
            lax.fori_loop(0, count, drain, 0)

        def fill_block(blk, c):
            zero_block(blk).start()
            return c

        def drain_block(blk, c):
            zero_block(blk).wait()
            return c

        lax.fori_loop(n_used_ref[0], n_blocks, fill_block, 0)
        lax.fori_loop(n_used_ref[0], n_blocks, drain_block, 0)

    pos = pos_ref[...]
    rows = lax.broadcasted_iota(I32, (SORT_ROWS, pos.shape[1]), 0)
    place = rows == pos[0:1]
    for k in range(1, TOP_K):
        place = place | (rows == pos[k:k + 1])
    sorted_ref[slot] = _dot(place.astype(BF16), h1_ref[...].astype(BF16))

    def issue(j, c):
        _group_copy(sorted_ref.at[slot], j, xs_hbm, dstg_ref[0, 0, j], sems.at[slot]).start()
        return c

    lax.fori_loop(0, ngroups_ref[step], issue, 0)

    def wait_tile(s, count):
        def drain(j, c):
            _group_copy(sorted_ref.at[s], 0, xs_hbm, 0, sems.at[s]).wait()
            return c
        lax.fori_loop(0, count, drain, 0)

    @pl.when(step > 0)
    def _():
        wait_tile(1 - slot, ngroups_ref[jnp.maximum(step - 1, 0)])

    @pl.when(step == n_steps - 1)
    def _():
        wait_tile(slot, ngroups_ref[step])


def _dispatch(ngroups, pad_start, pad_count, n_used, dstg, pos_t, h1, n_rows):
    t, d = h1.shape
    tm = ROW_TILE
    grid_spec = pltpu.PrefetchScalarGridSpec(
        num_scalar_prefetch=4,
        grid=(t // tm,),
        in_specs=[
            pl.BlockSpec((1, 1, SORT_GROUPS), lambda i, *_: (i, 0, 0), memory_space=pltpu.SMEM),
            pl.BlockSpec((TOP_K, tm), lambda i, *_: (0, i)),
            pl.BlockSpec((tm, d), lambda i, *_: (i, 0)),
        ],
        out_specs=pl.BlockSpec(memory_space=pl.ANY),
        scratch_shapes=[pltpu.VMEM((2, SORT_ROWS, d), F32), pltpu.VMEM((MOE_BLOCK, d), F32),
                        pltpu.SemaphoreType.DMA((2,)), pltpu.SemaphoreType.DMA],
    )
    return pl.pallas_call(
        _dispatch_kernel,
        grid_spec=grid_spec,
        out_shape=jax.ShapeDtypeStruct((n_rows, d), F32),
        compiler_params=pltpu.CompilerParams(dimension_semantics=("arbitrary",), vmem_limit_bytes=VMEM_LIMIT,
                                             has_side_effects=True),
        name="moe_dispatch",
    )(ngroups, pad_start, pad_count, n_used, dstg, pos_t, h1)


def _expert_kernel(be_ref, nb_ref, x_ref, wg_ref, bg_ref, wu_ref, bu_ref, wd_ref, bd_ref, y_ref,
                   wg_b, wu_b, wd_b):
    i = pl.program_id(0)

    @pl.when(i < nb_ref[0])
    def _():
        @pl.when((i == 0) | (be_ref[i] != be_ref[jnp.maximum(i - 1, 0)]))
        def _():
            wg_b[...] = wg_ref[0].astype(BF16)
            wu_b[...] = wu_ref[0].astype(BF16)
            wd_b[...] = wd_ref[0].astype(BF16)

        x = x_ref[...].astype(BF16)
        gate = _dot(x, wg_b[...]) + bg_ref[0]
        up = _dot(x, wu_b[...]) + bu_ref[0]
        gate = jnp.minimum(gate, SWIGLU_LIMIT)
        up = jnp.clip(up, -SWIGLU_LIMIT, SWIGLU_LIMIT)
        act = (up + 1.0) * gate * jax.nn.sigmoid(gate * SWIGLU_ALPHA)
        y_ref[...] = _dot(act.astype(BF16), wd_b[...]) + bd_ref[0]

    @pl.when(i >= nb_ref[0])
    def _():
        y_ref[...] = jnp.zeros(y_ref.shape, F32)


def _experts(block_expert, n_used, xs, wg, bg, wu, bu, wd, bd):
    n_rows, d = xs.shape
    dff = wg.shape[2]
    nb = n_rows // MOE_BLOCK
    blk = lambda i, be, nu: (jnp.minimum(i, nu[0] - 1), 0)
    wsel = lambda i, be, nu: (be[jnp.minimum(i, nu[0] - 1)], 0, 0)
    grid_spec = pltpu.PrefetchScalarGridSpec(
        num_scalar_prefetch=2,
        grid=(nb,),
        in_specs=[
            pl.BlockSpec((MOE_BLOCK, d), blk),
            pl.BlockSpec((1, d, dff), wsel), pl.BlockSpec((1, 1, dff), wsel),
            pl.BlockSpec((1, d, dff), wsel), pl.BlockSpec((1, 1, dff), wsel),
            pl.BlockSpec((1, dff, d), wsel), pl.BlockSpec((1, 1, d), wsel),
        ],
        out_specs=pl.BlockSpec((MOE_BLOCK, d), lambda i, be, nu: (i, 0)),
        scratch_shapes=[pltpu.VMEM((d, dff), BF16), pltpu.VMEM((d, dff), BF16), pltpu.VMEM((dff, d), BF16)],
    )
    return pl.pallas_call(
        _expert_kernel,
        grid_spec=grid_spec,
        out_shape=jax.ShapeDtypeStruct((n_rows, d), F32),
        compiler_params=pltpu.CompilerParams(dimension_semantics=("arbitrary",), vmem_limit_bytes=VMEM_LIMIT),
        name="moe_experts",
    )(block_expert, n_used, xs, wg, bg, wu, bu, wd, bd)


def _combine_kernel(ngroups_ref, dstg_ref, dstg_next_ref, pos_ref, gate_ref, h1_ref, ys_hbm, g2_ref, b2_ref,
                    o_ref, buf, sems):
    step = pl.program_id(0)
    n_steps = pl.num_programs(0)
    slot = step % 2

    def fetch(dref, s, j):
        return _group_copy(ys_hbm, dref[0, 0, j], buf.at[s], j, sems.at[s])

    def issue(dref, s, count):
        def body(j, c):
            fetch(dref, s, j).start()
            return c
        lax.fori_loop(0, count, body, 0)

    @pl.when(step == 0)
    def _():
        buf[...] = jnp.zeros(buf.shape, F32)
        issue(dstg_ref, 0, ngroups_ref[0])

    @pl.when(step + 1 < n_steps)
    def _():
        issue(dstg_next_ref, 1 - slot, ngroups_ref[jnp.minimum(step + 1, n_steps - 1)])

    def drain(j, c):
        fetch(dstg_ref, slot, 0).wait()
        return c

    lax.fori_loop(0, ngroups_ref[step], drain, 0)

    pos = pos_ref[...]
    gate = gate_ref[...]
    lanes = lax.broadcasted_iota(I32, (pos.shape[0], SORT_ROWS), 1)
    sel = jnp.where(lanes == pos[:, 0:1], gate[:, 0:1], 0.0)
    for k in range(1, TOP_K):
        sel = sel + jnp.where(lanes == pos[:, k:k + 1], gate[:, k:k + 1], 0.0)
    sel_hi = sel.astype(BF16)
    sel_lo = (sel - sel_hi.astype(F32)).astype(BF16)
    y_sorted = buf[slot].astype(BF16)
    ffn = _dot(sel_hi, y_sorted) + _dot(sel_lo, y_sorted)
    o_ref[...] = _layer_norm(DEEPNORM_ALPHA * h1_ref[...] + ffn, g2_ref[...], b2_ref[...])


def _combine(ngroups, dstg, pos_c, gates, h1, ys, g2, b2):
    t, d = h1.shape
    tm = ROW_TILE
    n_steps = t // tm
    row = lambda i, ng: (i, 0)
    const = lambda i, ng: (0, 0)
    grid_spec = pltpu.PrefetchScalarGridSpec(
        num_scalar_prefetch=1,
        grid=(n_steps,),
        in_specs=[
            pl.BlockSpec((1, 1, SORT_GROUPS), lambda i, ng: (i, 0, 0), memory_space=pltpu.SMEM),
            pl.BlockSpec((1, 1, SORT_GROUPS), lambda i, ng: (jnp.minimum(i + 1, n_steps - 1), 0, 0),
                         memory_space=pltpu.SMEM),
            pl.BlockSpec((tm, TOP_K), row), pl.BlockSpec((tm, TOP_K), row), pl.BlockSpec((tm, d), row),
            pl.BlockSpec(memory_space=pl.ANY),
            pl.BlockSpec((1, d), const), pl.BlockSpec((1, d), const),
        ],
        out_specs=pl.BlockSpec((tm, d), row),
        scratch_shapes=[pltpu.VMEM((2, SORT_ROWS, d), F32), pltpu.SemaphoreType.DMA((2,))],
    )
    return pl.pallas_call(
        _combine_kernel,
        grid_spec=grid_spec,
        out_shape=jax.ShapeDtypeStruct((t, d), F32),
        compiler_params=pltpu.CompilerParams(dimension_semantics=("arbitrary",), vmem_limit_bytes=VMEM_LIMIT),
        name="moe_combine_ln2",
    )(ngroups, dstg, dstg, pos_c, gates, h1, ys, g2, b2)


def kernel(x, meta_tokens, ln_in_g, ln_in_b, rel_bias, w_in, lambda_q1, lambda_k1, lambda_q2, lambda_k2,
           subln_g, a_re, a_im, log_step, b_re, b_im, c_re, c_im, d_skip, w_glu, b_glu, w_out, ln1_g, ln1_b,
           w_router, b_router, w_gate, b_gate, w_up, b_up, w_down, b_down, ln2_g, ln2_b):
    bsz, seq, dm = x.shape
    assert seq % 512 == 0 and w_in.shape[0] == DEPTH == 1
    layer = 0
    row2 = lambda v: v.astype(F32).reshape(1, -1)

    w_in_b = w_in[layer].astype(BF16)
    gi, bi = row2(ln_in_g), row2(ln_in_b)
    q, k, vt, u = _inproj(x, gi, bi, w_in_b, tm=512, kv_tile=KV_TILE)
    meta = jnp.zeros((1, META_PAD, dm), x.dtype).at[0, :N_META].set(meta_tokens.astype(x.dtype))
    _, k_meta, vt_meta, u_meta = _inproj(meta, gi, bi, w_in_b, tm=META_PAD, kv_tile=META_PAD)

    lam_init = 0.8 - 0.6 * math.exp(-0.3 * layer)
    lam = (jnp.exp(jnp.sum(lambda_q1[layer].astype(F32) * lambda_k1[layer].astype(F32)))
           - jnp.exp(jnp.sum(lambda_q2[layer].astype(F32) * lambda_k2[layer].astype(F32))) + lam_init)
    att = _attention(lam.reshape(1), q, k, vt, k_meta[0], vt_meta[0, :, 0], _near_bias(rel_bias),
                     subln_g[layer].astype(F32).reshape(HEAD_W, 1), lam_init=lam_init)

    kpad, pt, qt, a1, a2 = _ssm_tables(a_re[layer], a_im[layer], log_step[layer], b_re[layer], b_im[layer],
                                     c_re[layer], c_im[layer], d_skip[layer])
    n_chunks = -(-(seq // SSM_T + 1) // 8) * 8
    lead = jnp.zeros((bsz, SSM_T, SSM_W), BF16).at[:, SSM_T - N_META:].set(u_meta[:, :N_META])
    tail = jnp.zeros((bsz, n_chunks * SSM_T - SSM_T - seq, SSM_W), BF16)
    ug = jnp.concatenate([lead, u, tail], axis=1).reshape(bsz, n_chunks, SSM_T, SSM_G, SSM_CG)
    ug = jnp.transpose(ug, (3, 1, 0, 2, 4)).reshape(SSM_G, n_chunks * bsz, SSM_T * SSM_CG)
    yg = _ssm(ug, kpad, pt, qt, a1, a2, n_chunks=n_chunks, bsz=bsz)
    yg = jnp.transpose(yg.reshape(SSM_G, n_chunks, bsz, SSM_T, SSM_CG), (2, 1, 3, 0, 4))
    y_ssm = yg.reshape(bsz, n_chunks * SSM_T, SSM_W)[:, SSM_T:SSM_T + seq]

    t = bsz * seq
    h1, idx_t, gate_t, rank_t, cnt = _mix(
        x.reshape(t, dm), gi, bi, att.reshape(t, ATT_W), y_ssm.reshape(t, SSM_W),
        w_glu[layer].astype(BF16), row2(b_glu[layer]), w_out[layer].astype(BF16),
        row2(ln1_g[layer]), row2(ln1_b[layer]),
        jnp.transpose(w_router[layer].astype(F32)), b_router[layer].astype(F32).reshape(N_EXPERTS, 1))

    n_tiles = t // ROW_TILE
    experts = jnp.arange(N_EXPERTS, dtype=I32)
    tiles = jnp.arange(n_tiles, dtype=I32)
    run = (cnt[:, :, 0].astype(I32) + GROUP - 1) // GROUP * GROUP
    run_off = jnp.sum(jnp.where((experts[:, None] < experts[None, :])[None], run[:, :, None], 0), axis=1)
    run_before = jnp.sum(jnp.where((tiles[:, None] < tiles[None, :])[:, :, None], run[:, None, :], 0), axis=0)
    counts = jnp.sum(run, axis=0)
    padded = (counts + MOE_BLOCK - 1) // MOE_BLOCK * MOE_BLOCK
    padded_end = jnp.sum(jnp.where(experts[:, None] <= experts[None, :], padded[:, None], 0), axis=0)
    padded_start = padded_end - padded
    n_blocks = (t * TOP_K + n_tiles * N_EXPERTS * (GROUP - 1)) // MOE_BLOCK + N_EXPERTS
    block_row0 = jnp.arange(n_blocks, dtype=I32) * MOE_BLOCK
    block_expert = jnp.minimum(jnp.sum((padded_end[None, :] <= block_row0[:, None]).astype(I32), axis=1),
                               N_EXPERTS - 1)
    n_used = (padded_end[-1:] // MOE_BLOCK).astype(I32)
    ngroups = jnp.sum(run, axis=1) // GROUP
    local_row = jnp.arange(SORT_GROUPS, dtype=I32) * GROUP
    owner = jnp.minimum(jnp.sum(((run_off + run)[:, None, :] <= local_row[None, :, None]).astype(I32), axis=-1),
                        N_EXPERTS - 1)
    shift = (padded_start[None, :] + run_before - run_off) // GROUP
    dstg = jnp.sum(jnp.where(owner[..., None] == experts, shift[:, None, :], 0), axis=-1) + local_row // GROUP
    dstg = jnp.where(local_row[None, :] // GROUP < ngroups[:, None], dstg, 0).reshape(n_tiles, 1, SORT_GROUPS)
    run_off_tok = jnp.repeat(run_off, ROW_TILE, axis=0)
    pos_t = jnp.sum(jnp.where(idx_t[..., None] == experts, run_off_tok[None], 0), axis=-1) + rank_t

    xs = _dispatch(ngroups, ((padded_start + counts) // GROUP).astype(I32),
                   ((padded - counts) // GROUP).astype(I32), n_used, dstg, pos_t, h1, n_blocks * MOE_BLOCK)
    b3 = lambda v: v.astype(F32)[:, None, :]
    ys = _experts(block_expert, n_used, xs, w_gate[layer], b3(b_gate[layer]), w_up[layer], b3(b_up[layer]),
                  w_down[layer], b3(b_down[layer]))
    out = _combine(ngroups, dstg, jnp.transpose(pos_t), jnp.transpose(gate_t), h1, ys,
                   row2(ln2_g[layer]), row2(ln2_b[layer]))
    return out.reshape(bsz, seq, dm)
```

```python
import functools
import math

import jax
import jax.numpy as jnp
import numpy as np
from jax import lax
from jax.experimental import pallas as pl
from jax.experimental.pallas import tpu as pltpu

F32 = jnp.float32
BF16 = jnp.bfloat16
I32 = jnp.int32

DEPTH = 1
N_META = 16
CHUNK = 64
ATT_HEADS = 4
HEAD_DIM = 64
HEAD_W = 2 * HEAD_DIM
ATT_W = ATT_HEADS * HEAD_W
SSM_W = 512
SSM_CG = 16
SSM_G = SSM_W // SSM_CG
SSM_N = 64
N_BUCKETS = 32
MAX_DISTANCE = 128
N_EXPERTS = 32
TOP_K = 4
SWIGLU_LIMIT = 7.0
SWIGLU_ALPHA = 1.702
LN_EPS = 1e-5
NEG_INF = -1e30
DEEPNORM_ALPHA = (2.0 * DEPTH) ** 0.25
LOG2E = 1.4426950408889634

Q_TILE = 512
KV_TILE = 512
V_ONES = 16
V_ROWS = HEAD_W + V_ONES
META_PAD = 128
SSM_T = 64
MIX_TILE = 512
ROW_TILE = 256
MOE_BLOCK = 512
GROUP = 8
SORT_ROWS = 1280
SORT_GROUPS = SORT_ROWS // GROUP
assert SORT_ROWS >= ROW_TILE * TOP_K + N_EXPERTS * (GROUP - 1)
VMEM_LIMIT = 56 * 1024 * 1024


def _layer_norm(x, g, b):
    mu = jnp.mean(x, axis=-1, keepdims=True)
    xc = x - mu
    var = jnp.mean(xc * xc, axis=-1, keepdims=True)
    return xc * lax.rsqrt(var + LN_EPS) * g + b


def _dot(a, b):
    return jnp.dot(a, b, preferred_element_type=F32)


def _dot_nt(a, b, precision=None):
    return lax.dot_general(a, b, (((1,), (1,)), ((), ())), precision=precision,
                           preferred_element_type=F32)


def _inproj_kernel(x_ref, g_ref, b_ref, w_ref, q_ref, k_ref, vt_ref, u_ref, *, kv_tile, q_scale):
    h = _layer_norm(x_ref[0], g_ref[...], b_ref[...]).astype(BF16)
    tm = h.shape[0]
    q_ref[0] = (_dot(h, w_ref[:, 0:ATT_W]) * q_scale).astype(BF16)
    k_ref[0] = _dot(h, w_ref[:, ATT_W:2 * ATT_W]).astype(BF16)
    v = _dot(h, w_ref[:, 2 * ATT_W:3 * ATT_W])
    ones = jnp.ones((V_ONES, kv_tile), BF16)
    for hh in range(ATT_HEADS):
        vt = v[:, hh * HEAD_W:(hh + 1) * HEAD_W].T.astype(BF16)
        for j in range(tm // kv_tile):
            vt_ref[0, hh, j, 0:HEAD_W, :] = vt[:, j * kv_tile:(j + 1) * kv_tile]
            vt_ref[0, hh, j, HEAD_W:V_ROWS, :] = ones
    u_ref[0] = _dot(h, w_ref[:, 3 * ATT_W:]).astype(BF16)


def _inproj(x, g, b, w, *, tm, kv_tile):
    bsz, s, d = x.shape
    q_scale = HEAD_DIM ** -0.5 * LOG2E
    n_cols = w.shape[1]
    row = lambda bi, i: (bi, i, 0)
    return pl.pallas_call(
        functools.partial(_inproj_kernel, kv_tile=kv_tile, q_scale=q_scale),
        grid=(bsz, s // tm),
        in_specs=[
            pl.BlockSpec((1, tm, d), row),
            pl.BlockSpec((1, d), lambda bi, i: (0, 0)),
            pl.BlockSpec((1, d), lambda bi, i: (0, 0)),
            pl.BlockSpec((d, n_cols), lambda bi, i: (0, 0)),
        ],
        out_specs=[
            pl.BlockSpec((1, tm, ATT_W), row),
            pl.BlockSpec((1, tm, ATT_W), row),
            pl.BlockSpec((1, ATT_HEADS, tm // kv_tile, V_ROWS, kv_tile), lambda bi, i: (bi, 0, i, 0, 0)),
            pl.BlockSpec((1, tm, SSM_W), row),
        ],
        out_shape=[
            jax.ShapeDtypeStruct((bsz, s, ATT_W), BF16),
            jax.ShapeDtypeStruct((bsz, s, ATT_W), BF16),
            jax.ShapeDtypeStruct((bsz, ATT_HEADS, s // kv_tile, V_ROWS, kv_tile), BF16),
            jax.ShapeDtypeStruct((bsz, s, SSM_W), BF16),
        ],
        compiler_params=pltpu.CompilerParams(
            dimension_semantics=("parallel", "parallel"), vmem_limit_bytes=VMEM_LIMIT),
        name="inproj",
    )(x, g, b, w)


def _attn_kernel(lam_ref, q_ref, k_ref, vt_ref, km_ref, vtm_ref, bias_ref, g_ref, o_ref,
                 m_ref, acc_ref, qz_ref, sa_ref, sb_ref, sn_ref, *, out_scale):
    i = pl.program_id(2)
    q = q_ref[0]
    lane = lax.broadcasted_iota(I32, q.shape, 1)
    zero = jnp.zeros_like(q)
    qz_ref[0:Q_TILE, :] = jnp.where(lane < HEAD_DIM, q, zero)
    qz_ref[Q_TILE:, :] = jnp.where(lane >= HEAD_DIM, q, zero)

    m_ref[...] = jnp.full(m_ref.shape, NEG_INF, F32)
    acc_ref[...] = jnp.zeros(acc_ref.shape, F32)

    def scores(kt):
        return _dot_nt(kt, qz_ref[...])

    def k_tile(t):
        return k_ref[0, pl.ds(pl.multiple_of(t * KV_TILE, KV_TILE), KV_TILE), :]

    def absorb(s, pv):
        m_old = m_ref[...]
        m_new = jnp.maximum(m_old, jnp.max(s, axis=0, keepdims=True))
        alpha = jnp.exp2(m_old - m_new)
        p = jnp.exp2(s - m_new).astype(BF16)
        acc_ref[...] = acc_ref[...] * alpha + pv(p)
        m_ref[...] = m_new

    def tile_pv(t):
        return lambda p: _dot(vt_ref[0, 0, t], p)

    n_far = jnp.maximum(i - 1, 0)
    peel = n_far % 2

    @pl.when(peel == 1)
    def _():
        absorb(scores(k_tile(0)), tile_pv(0))

    sa_ref[...] = scores(k_tile(peel))

    def far_pair(j, carry):
        t0 = peel + 2 * j
        sb_ref[...] = scores(k_tile(t0 + 1))
        absorb(sa_ref[...], tile_pv(t0))
        sa_ref[...] = scores(k_tile(t0 + 2))
        absorb(sb_ref[...], tile_pv(t0 + 1))
        return carry

    lax.fori_loop(0, n_far // 2, far_pair, 0)

    sb_ref[...] = scores(k_tile(i))
    absorb(sa_ref[...] + bias_ref[0, 0, 0:KV_TILE, :], tile_pv(n_far))
    sn_ref[...] = scores(km_ref[...])
    absorb(sb_ref[...] + bias_ref[0, 0, KV_TILE:2 * KV_TILE, :], tile_pv(i))
    absorb(sn_ref[...] + bias_ref[0, 0, 2 * KV_TILE:, :], lambda p: _dot(vtm_ref[0], p))

    acc = acc_ref[...]
    lam = lam_ref[0]
    o1 = acc[0:HEAD_W, 0:Q_TILE] / acc[HEAD_W:HEAD_W + 1, 0:Q_TILE]
    o2 = acc[0:HEAD_W, Q_TILE:] / acc[HEAD_W:HEAD_W + 1, Q_TILE:]
    o = o1 - lam * o2
    ms = jnp.mean(o * o, axis=0, keepdims=True)
    o = o * lax.rsqrt(ms + LN_EPS) * g_ref[...] * out_scale
    o_ref[0] = o.T.astype(o_ref.dtype)


def _attention(lam, q, k, vt, k_meta, vt_meta, bias, subln_g, *, lam_init):
    bsz, s, _ = q.shape
    nq = s // Q_TILE
    n_near = META_PAD + 2 * KV_TILE
    grid_spec = pltpu.PrefetchScalarGridSpec(
        num_scalar_prefetch=1,
        grid=(bsz, ATT_HEADS, nq),
        in_specs=[
            pl.BlockSpec((1, Q_TILE, HEAD_W), lambda b, h, i, lam: (b, i, h)),
            pl.BlockSpec((1, s, HEAD_W), lambda b, h, i, lam: (b, 0, h)),
            pl.BlockSpec((1, 1, s // KV_TILE, V_ROWS, KV_TILE), lambda b, h, i, lam: (b, h, 0, 0, 0)),
            pl.BlockSpec((META_PAD, HEAD_W), lambda b, h, i, lam: (0, h)),
            pl.BlockSpec((1, V_ROWS, META_PAD), lambda b, h, i, lam: (h, 0, 0)),
            pl.BlockSpec((1, 1, n_near, 2 * Q_TILE), lambda b, h, i, lam: (h, jnp.minimum(i, 1), 0, 0)),
            pl.BlockSpec((HEAD_W, 1), lambda b, h, i, lam: (0, 0)),
        ],
        out_specs=pl.BlockSpec((1, Q_TILE, HEAD_W), lambda b, h, i, lam: (b, i, h)),
        scratch_shapes=[
            pltpu.VMEM((1, 2 * Q_TILE), F32), pltpu.VMEM((V_ROWS, 2 * Q_TILE), F32),
            pltpu.VMEM((2 * Q_TILE, HEAD_W), BF16),
            pltpu.VMEM((KV_TILE, 2 * Q_TILE), F32), pltpu.VMEM((KV_TILE, 2 * Q_TILE), F32),
            pltpu.VMEM((META_PAD, 2 * Q_TILE), F32),
        ],
    )
    return pl.pallas_call(
        functools.partial(_attn_kernel, out_scale=1.0 - lam_init),
        grid_spec=grid_spec,
        out_shape=jax.ShapeDtypeStruct((bsz, s, ATT_W), BF16),
        compiler_params=pltpu.CompilerParams(
            dimension_semantics=("parallel", "parallel", "arbitrary"), vmem_limit_bytes=VMEM_LIMIT),
        name="diff_attention",
    )(lam, q, k, vt, k_meta, vt_meta, bias, subln_g)


def _t5_bucket(rel):
    nb = N_BUCKETS // 2
    max_exact = nb // 2
    ret = jnp.where(rel > 0, nb, 0)
    n = jnp.abs(rel)
    n_f = jnp.maximum(n, 1).astype(F32)
    large = max_exact + (jnp.log(n_f / max_exact) / math.log(MAX_DISTANCE / max_exact)
                         * (nb - max_exact)).astype(I32)
    large = jnp.minimum(large, nb - 1)
    return ret + jnp.where(n < max_exact, n, large)


def _near_bias(rel_bias):
    table = rel_bias.astype(F32)
    far = table[N_BUCKETS // 2 - 1]
    c = jnp.arange(Q_TILE, dtype=I32)[None, :]
    r = jnp.arange(KV_TILE, dtype=I32)[:, None]

    shifted = jnp.transpose(table - far) * LOG2E

    def bias_of(offset, rows):
        rel = jnp.arange(rows, dtype=I32)[:, None] - c + offset
        hot = _t5_bucket(rel)[None, :, :, None] == jnp.arange(N_BUCKETS, dtype=I32)
        return jnp.sum(jnp.where(hot, shifted[:, None, None, :], 0.0), axis=-1)

    own = jnp.where((r // CHUNK <= c // CHUNK)[None], bias_of(0, KV_TILE), NEG_INF)
    prev = bias_of(-KV_TILE, KV_TILE)
    rm = jnp.arange(META_PAD, dtype=I32)[:, None]
    meta_ok = (rm < N_META)[None]
    meta0 = jnp.where(meta_ok, bias_of(-N_META, META_PAD), NEG_INF)
    meta1 = jnp.where(meta_ok, bias_of(-N_META - Q_TILE, META_PAD), NEG_INF)
    v0 = jnp.concatenate([jnp.full_like(prev, NEG_INF), own, meta0], axis=1)
    v1 = jnp.concatenate([prev, own, meta1], axis=1)
    both = jnp.stack([v0, v1], axis=1)
    return jnp.concatenate([both, both], axis=-1)


def _ssm_tables(a_re, a_im, log_step, b_re, b_im, c_re, c_im, d_skip):
    hi = lax.Precision.HIGHEST
    t_len = SSM_T
    step = jnp.exp(log_step.astype(F32))[:, None]
    ar = jnp.minimum(a_re.astype(F32), -1e-4)
    ai = a_im.astype(F32)
    mag = jnp.exp(step * ar)
    ph = step * ai
    abar_re = mag * jnp.cos(ph)
    abar_im = mag * jnp.sin(ph)
    den = ar * ar + ai * ai
    e_re = abar_re - 1.0
    e_im = abar_im
    f_re = (e_re * ar + e_im * ai) / den
    f_im = (e_im * ar - e_re * ai) / den
    br = b_re.astype(F32)
    bi = b_im.astype(F32)
    bb_re = f_re[..., None] * br - f_im[..., None] * bi
    bb_im = f_re[..., None] * bi + f_im[..., None] * br
    tau = jnp.arange(t_len + 1, dtype=F32)[None, :, None]
    pmag = jnp.exp(tau * (step * ar)[:, None, :])
    pph = tau * ph[:, None, :]
    pw_re = pmag * jnp.cos(pph)
    pw_im = pmag * jnp.sin(pph)
    w_re = pw_re[..., None] * bb_re[:, None] - pw_im[..., None] * bb_im[:, None]
    w_im = pw_re[..., None] * bb_im[:, None] + pw_im[..., None] * bb_re[:, None]
    cr = c_re.astype(F32)
    ci = c_im.astype(F32)
    kern = (jnp.einsum('gcn,gtnk->gtck', cr, w_re[:, :t_len], precision=hi)
            - jnp.einsum('gcn,gtnk->gtck', ci, w_im[:, :t_len], precision=hi))
    skip = d_skip.astype(F32).reshape(SSM_G, SSM_CG)
    kern = kern.at[:, 0].add(skip[:, :, None] * jnp.eye(SSM_CG, dtype=F32))
    kflat = jnp.transpose(kern, (0, 3, 1, 2)).reshape(SSM_G, SSM_CG, t_len * SSM_CG)
    kpad = jnp.concatenate([jnp.zeros_like(kflat), kflat], axis=-1)
    pt_re = jnp.transpose(jnp.flip(w_re[:, :t_len], axis=1), (0, 1, 3, 2)).reshape(SSM_G, t_len * SSM_CG, SSM_N)
    pt_im = jnp.transpose(jnp.flip(w_im[:, :t_len], axis=1), (0, 1, 3, 2)).reshape(SSM_G, t_len * SSM_CG, SSM_N)
    pt = jnp.concatenate([pt_re, pt_im], axis=-1)
    up_re = pw_re[:, 1:]
    up_im = pw_im[:, 1:]
    q_re = cr[:, None] * up_re[:, :, None, :] - ci[:, None] * up_im[:, :, None, :]
    q_im = cr[:, None] * up_im[:, :, None, :] + ci[:, None] * up_re[:, :, None, :]
    qt = jnp.concatenate([jnp.transpose(q_re, (0, 3, 1, 2)), -jnp.transpose(q_im, (0, 3, 1, 2))], axis=1)
    qt = qt.reshape(SSM_G, 2 * SSM_N, t_len * SSM_CG)
    at_re = pw_re[:, t_len]
    at_im = pw_im[:, t_len]
    a1 = jnp.concatenate([at_re, at_re], axis=-1)
    a2 = jnp.concatenate([-at_im, at_im], axis=-1)
    return kpad, pt.astype(BF16), qt.astype(BF16), a1, a2


def _ssm_state_kernel(u_ref, pt_ref, s_ref):
    s_ref[0] = _dot(u_ref[0], pt_ref[0])


def _ssm_scan_kernel(s_ref, a1_ref, a2_ref, x_ref, *, n_chunks, bsz):
    a1 = a1_ref[...][:, None, :]
    a2 = a2_ref[...][:, None, :]

    def body(c, x):
        rows = pl.ds(c * bsz, bsz)
        x_ref[:, rows, :] = x
        return a1 * x + a2 * pltpu.roll(x, SSM_N, 2) + s_ref[:, rows, :]

    lax.fori_loop(0, n_chunks, body, jnp.zeros((SSM_G, bsz, 2 * SSM_N), F32))


def _ssm_out_kernel(u_ref, x_ref, kp_ref, qt_ref, y_ref, mt_ref):
    kp = kp_ref[0]
    width = SSM_T * SSM_CG
    lane_tile = 128
    for sub in range(0, lane_tile, SSM_CG):
        shifted = kp if sub == 0 else pltpu.roll(kp, 2 * width - sub, 1)
        for s in range(SSM_T):
            off = (SSM_T - s) * SSM_CG
            if off % lane_tile == sub:
                base = off - sub
                mt_ref[s * SSM_CG:(s + 1) * SSM_CG, :] = shifted[:, base:base + width].astype(BF16)
    x = x_ref[0]
    x_hi = x.astype(BF16)
    x_lo = (x - x_hi.astype(F32)).astype(BF16)
    y = _dot(u_ref[0], mt_ref[...]) + _dot(x_hi, qt_ref[0]) + _dot(x_lo, qt_ref[0])
    y_ref[0] = y.astype(y_ref.dtype)


def _ssm(ug, kpad, pt, qt, a1, a2, *, n_chunks, bsz):
    g, r, w = ug.shape
    n2 = 2 * SSM_N
    per_g = lambda gi: (gi, 0, 0)
    params = pltpu.CompilerParams(dimension_semantics=("parallel",), vmem_limit_bytes=VMEM_LIMIT)
    s = pl.pallas_call(
        _ssm_state_kernel,
        grid=(g,),
        in_specs=[pl.BlockSpec((1, r, w), per_g), pl.BlockSpec((1, w, n2), per_g)],
        out_specs=pl.BlockSpec((1, r, n2), per_g),
        out_shape=jax.ShapeDtypeStruct((g, r, n2), F32),
        compiler_params=params,
        name="ssm_chunk_state",
    )(ug, pt)
    x = pl.pallas_call(
        functools.partial(_ssm_scan_kernel, n_chunks=n_chunks, bsz=bsz),
        out_shape=jax.ShapeDtypeStruct((g, r, n2), F32),
        compiler_params=pltpu.CompilerParams(vmem_limit_bytes=VMEM_LIMIT),
        name="ssm_chunk_scan",
    )(s, a1, a2)
    return pl.pallas_call(
        _ssm_out_kernel,
        grid=(g,),
        in_specs=[pl.BlockSpec((1, r, w), per_g), pl.BlockSpec((1, r, n2), per_g),
                  pl.BlockSpec((1, SSM_CG, 2 * w), per_g), pl.BlockSpec((1, n2, w), per_g)],
        out_specs=pl.BlockSpec((1, r, w), per_g),
        out_shape=jax.ShapeDtypeStruct((g, r, w), BF16),
        scratch_shapes=[pltpu.VMEM((w, w), BF16)],
        compiler_params=params,
        name="ssm_output",
    )(ug, x, kpad, qt)


def _mix_kernel(x_ref, gi_ref, bi_ref, att_ref, y_ref, wglu_ref, bglu_ref, wout_ref, g1_ref, b1_ref,
                wr_ref, br_ref, h1_ref, idx_ref, gate_ref, rank_ref, cnt_ref):
    h0 = _layer_norm(x_ref[...], gi_ref[...], bi_ref[...])
    y = y_ref[...].astype(F32)
    y = y * (0.5 * (1.0 + jnp.tanh(math.sqrt(2.0 / math.pi) * (y + 0.044715 * (y * y * y)))))
    y = y * jax.nn.sigmoid(_dot(y.astype(BF16), wglu_ref[...]) + bglu_ref[...])
    mix = _dot(att_ref[...], wout_ref[0:ATT_W, :]) + _dot(y.astype(BF16), wout_ref[ATT_W:, :])
    h1 = _layer_norm(DEEPNORM_ALPHA * h0 + mix, g1_ref[...], b1_ref[...])
    h1_ref[...] = h1

    logits = _dot_nt(wr_ref[...], h1, precision=lax.Precision.HIGHEST) + br_ref[...]
    tm = logits.shape[1]
    eidx = lax.broadcasted_iota(I32, logits.shape, 0)
    vals, hots = [], []
    rest = logits
    for _ in range(TOP_K):
        mx = jnp.max(rest, axis=0, keepdims=True)
        first = jnp.min(jnp.where(rest == mx, eidx, N_EXPERTS), axis=0, keepdims=True)
        hot = eidx == first
        vals.append(mx)
        hots.append(hot)
        rest = jnp.where(hot, -jnp.inf, rest)
    exps = [jnp.exp(v - vals[0]) for v in vals]
    denom = exps[0] + exps[1] + exps[2] + exps[3]
    gate_ref[...] = jnp.concatenate([e / denom for e in exps], axis=0)
    idx_ref[...] = jnp.concatenate(
        [jnp.sum(jnp.where(h, eidx, 0), axis=0, keepdims=True) for h in hots], axis=0)

    hot_all = (hots[0] | hots[1] | hots[2] | hots[3]).astype(F32)
    sub = ROW_TILE
    tri = (lax.broadcasted_iota(I32, (sub, sub), 0) < lax.broadcasted_iota(I32, (sub, sub), 1)).astype(BF16)
    for part in range(tm // sub):
        cols = slice(part * sub, (part + 1) * sub)
        before = _dot(hot_all[:, cols].astype(BF16), tri)
        rank_ref[:, cols] = jnp.concatenate(
            [jnp.sum(jnp.where(h[:, cols], before, 0.0), axis=0, keepdims=True) for h in hots], axis=0).astype(I32)
        cnt_ref[part] = jnp.broadcast_to(jnp.sum(hot_all[:, cols], axis=1, keepdims=True), cnt_ref.shape[1:])


def _mix(x2, gi, bi, att, y, wglu, bglu, wout, g1, b1, wr_t, br):
    t, d = x2.shape
    tm = MIX_TILE
    row = lambda i: (i, 0)
    col = lambda i: (0, i)
    const = lambda i: (0, 0)
    return pl.pallas_call(
        _mix_kernel,
        grid=(t // tm,),
        in_specs=[
            pl.BlockSpec((tm, d), row), pl.BlockSpec((1, d), const), pl.BlockSpec((1, d), const),
            pl.BlockSpec((tm, ATT_W), row), pl.BlockSpec((tm, SSM_W), row),
            pl.BlockSpec((SSM_W, SSM_W), const), pl.BlockSpec((1, SSM_W), const),
            pl.BlockSpec((d, d), const), pl.BlockSpec((1, d), const), pl.BlockSpec((1, d), const),
            pl.BlockSpec((N_EXPERTS, d), const), pl.BlockSpec((N_EXPERTS, 1), const),
        ],
        out_specs=[
            pl.BlockSpec((tm, d), row),
            pl.BlockSpec((TOP_K, tm), col), pl.BlockSpec((TOP_K, tm), col), pl.BlockSpec((TOP_K, tm), col),
            pl.BlockSpec((tm // ROW_TILE, N_EXPERTS, 128), lambda i: (i, 0, 0)),
        ],
        out_shape=[
            jax.ShapeDtypeStruct((t, d), F32),
            jax.ShapeDtypeStruct((TOP_K, t), I32), jax.ShapeDtypeStruct((TOP_K, t), F32),
            jax.ShapeDtypeStruct((TOP_K, t), I32),
            jax.ShapeDtypeStruct((t // ROW_TILE, N_EXPERTS, 128), F32),
        ],
        compiler_params=pltpu.CompilerParams(dimension_semantics=("parallel",), vmem_limit_bytes=VMEM_LIMIT),
        name="mix_ln1_router",
    )(x2, gi, bi, att, y, wglu, bglu, wout, g1, b1, wr_t, br)


_HI_BITS = -65536


def _pack_halves(x):
    half = x.shape[1] // 2
    lo = lax.bitcast_convert_type(x[:, :half], I32)
    hi = lax.bitcast_convert_type(x[:, half:], I32)
    return lax.shift_right_logical(lo, 16) | (hi & _HI_BITS)


def _unpack_halves(w):
    lo = lax.bitcast_convert_type(lax.shift_left(w, 16), F32)
    hi = lax.bitcast_convert_type(w & _HI_BITS, F32)
    return lo.astype(BF16), hi.astype(BF16)


def _group_copy(src, src_group, dst, dst_group, sem):
    return pltpu.make_async_copy(src.at[pl.ds(pl.multiple_of(src_group * GROUP, GROUP), GROUP), :],
                                 dst.at[pl.ds(pl.multiple_of(dst_group * GROUP, GROUP), GROUP), :], sem)


def _dispatch_kernel(ngroups_ref, pad_start_ref, pad_count_ref, n_used_ref, dstg_ref, pos_ref, h1_ref, xs_hbm,
                     sorted_ref, zero_ref, sems, zsem):
    step = pl.program_id(0)
    n_steps = pl.num_programs(0)
    slot = step % 2
    n_blocks = xs_hbm.shape[0] // MOE_BLOCK

    def zero_block(blk):
        return pltpu.make_async_copy(zero_ref, xs_hbm.at[pl.ds(blk * MOE_BLOCK, MOE_BLOCK), :], zsem)

    @pl.when(step == 0)
    def _():
        zero_ref[...] = jnp.zeros(zero_ref.shape, I32)
        for e in range(N_EXPERTS):
            start = pad_start_ref[e]
            count = pad_count_ref[e]

            def fill(j, c):
                _group_copy(zero_ref, 0, xs_hbm, start + j, zsem).start()
                return c

            def drain(j, c):
                _group_copy(zero_ref, 0, xs_hbm, start, zsem).wait()
                return c

            lax.fori_loop(0, count, fill, 0)
            lax.fori_loop(0, count, drain, 0)

        def fill_block(blk, c):
            zero_block(blk).start()
            return c

        def drain_block(blk, c):
            zero_block(blk).wait()
            return c

        lax.fori_loop(n_used_ref[0], n_blocks, fill_block, 0)
        lax.fori_loop(n_used_ref[0], n_blocks, drain_block, 0)

    pos = pos_ref[...]
    rows = lax.broadcasted_iota(I32, (SORT_ROWS, pos.shape[1]), 0)
    place = rows == pos[0:1]
    for k in range(1, TOP_K):
        place = place | (rows == pos[k:k + 1])
    sorted_ref[slot] = _pack_halves(_dot(place.astype(BF16), h1_ref[...].astype(BF16)))

    def issue(j, c):
        _group_copy(sorted_ref.at[slot], j, xs_hbm, dstg_ref[0, 0, j], sems.at[slot]).start()
        return c

    lax.fori_loop(0, ngroups_ref[step], issue, 0)

    def wait_tile(s, count):
        def drain(j, c):
            _group_copy(sorted_ref.at[s], 0, xs_hbm, 0, sems.at[s]).wait()
            return c
        lax.fori_loop(0, count, drain, 0)

    @pl.when(step > 0)
    def _():
        wait_tile(1 - slot, ngroups_ref[jnp.maximum(step - 1, 0)])

    @pl.when(step == n_steps - 1)
    def _():
        wait_tile(slot, ngroups_ref[step])


def _dispatch(ngroups, pad_start, pad_count, n_used, dstg, pos_t, h1, n_rows):
    t, d = h1.shape
    tm = ROW_TILE
    grid_spec = pltpu.PrefetchScalarGridSpec(
        num_scalar_prefetch=4,
        grid=(t // tm,),
        in_specs=[
            pl.BlockSpec((1, 1, SORT_GROUPS), lambda i, *_: (i, 0, 0), memory_space=pltpu.SMEM),
            pl.BlockSpec((TOP_K, tm), lambda i, *_: (0, i)),
            pl.BlockSpec((tm, d), lambda i, *_: (i, 0)),
        ],
        out_specs=pl.BlockSpec(memory_space=pl.ANY),
        scratch_shapes=[pltpu.VMEM((2, SORT_ROWS, d // 2), I32), pltpu.VMEM((MOE_BLOCK, d // 2), I32),
                        pltpu.SemaphoreType.DMA((2,)), pltpu.SemaphoreType.DMA],
    )
    return pl.pallas_call(
        _dispatch_kernel,
        grid_spec=grid_spec,
        out_shape=jax.ShapeDtypeStruct((n_rows, d // 2), I32),
        compiler_params=pltpu.CompilerParams(dimension_semantics=("arbitrary",), vmem_limit_bytes=VMEM_LIMIT,
                                             has_side_effects=True),
        name="moe_dispatch",
    )(ngroups, pad_start, pad_count, n_used, dstg, pos_t, h1)


def _expert_kernel(be_ref, nb_ref, x_ref, wg_ref, bg_ref, wu_ref, bu_ref, wd_ref, bd_ref, y_ref,
                   wg_b, wu_b, wd_b):
    i = pl.program_id(0)

    @pl.when(i < nb_ref[0])
    def _():
        @pl.when((i == 0) | (be_ref[i] != be_ref[jnp.maximum(i - 1, 0)]))
        def _():
            wg_b[...] = wg_ref[0].astype(BF16)
            wu_b[...] = wu_ref[0].astype(BF16)
            wd_b[...] = wd_ref[0].astype(BF16)

        x_lo, x_hi = _unpack_halves(x_ref[...])
        half = x_lo.shape[1]
        gate = _dot(x_lo, wg_b[0:half, :]) + _dot(x_hi, wg_b[half:, :]) + bg_ref[0]
        up = _dot(x_lo, wu_b[0:half, :]) + _dot(x_hi, wu_b[half:, :]) + bu_ref[0]
        gate = jnp.minimum(gate, SWIGLU_LIMIT)
        up = jnp.clip(up, -SWIGLU_LIMIT, SWIGLU_LIMIT)
        act = (up + 1.0) * gate * jax.nn.sigmoid(gate * SWIGLU_ALPHA)
        y = _dot(act.astype(BF16), wd_b[...]) + bd_ref[0]
        y_ref[...] = _pack_halves(y.astype(BF16).astype(F32))

    @pl.when(i >= nb_ref[0])
    def _():
        y_ref[...] = jnp.zeros(y_ref.shape, I32)


def _experts(block_expert, n_used, xs, wg, bg, wu, bu, wd, bd):
    n_rows, packed = xs.shape
    d, dff = wg.shape[1], wg.shape[2]
    assert packed * 2 == d and wd.shape[2] == d
    nb = n_rows // MOE_BLOCK
    blk = lambda i, be, nu: (jnp.minimum(i, nu[0] - 1), 0)
    wsel = lambda i, be, nu: (be[jnp.minimum(i, nu[0] - 1)], 0, 0)
    grid_spec = pltpu.PrefetchScalarGridSpec(
        num_scalar_prefetch=2,
        grid=(nb,),
        in_specs=[
            pl.BlockSpec((MOE_BLOCK, packed), blk),
            pl.BlockSpec((1, d, dff), wsel), pl.BlockSpec((1, 1, dff), wsel),
            pl.BlockSpec((1, d, dff), wsel), pl.BlockSpec((1, 1, dff), wsel),
            pl.BlockSpec((1, dff, d), wsel), pl.BlockSpec((1, 1, d), wsel),
        ],
        out_specs=pl.BlockSpec((MOE_BLOCK, packed), lambda i, be, nu: (i, 0)),
        scratch_shapes=[pltpu.VMEM((d, dff), BF16), pltpu.VMEM((d, dff), BF16), pltpu.VMEM((dff, d), BF16)],
    )
    return pl.pallas_call(
        _expert_kernel,
        grid_spec=grid_spec,
        out_shape=jax.ShapeDtypeStruct((n_rows, packed), I32),
        compiler_params=pltpu.CompilerParams(dimension_semantics=("arbitrary",), vmem_limit_bytes=VMEM_LIMIT),
        name="moe_experts",
    )(block_expert, n_used, xs, wg, bg, wu, bu, wd, bd)


def _combine_kernel(ngroups_ref, dstg_ref, dstg_next_ref, pos_ref, gate_ref, h1_ref, ys_hbm, g2_ref, b2_ref,
                    o_ref, buf, sems):
    step = pl.program_id(0)
    n_steps = pl.num_programs(0)
    slot = step % 2

    def fetch(dref, s, j):
        return _group_copy(ys_hbm, dref[0, 0, j], buf.at[s], j, sems.at[s])

    def issue(dref, s, count):
        def body(j, c):
            fetch(dref, s, j).start()
            return c
        lax.fori_loop(0, count, body, 0)

    @pl.when(step == 0)
    def _():
        buf[...] = jnp.zeros(buf.shape, I32)
        issue(dstg_ref, 0, ngroups_ref[0])

    @pl.when(step + 1 < n_steps)
    def _():
        issue(dstg_next_ref, 1 - slot, ngroups_ref[jnp.minimum(step + 1, n_steps - 1)])

    def drain(j, c):
        fetch(dstg_ref, slot, 0).wait()
        return c

    lax.fori_loop(0, ngroups_ref[step], drain, 0)

    pos = pos_ref[...]
    gate = gate_ref[...]
    lanes = lax.broadcasted_iota(I32, (pos.shape[0], SORT_ROWS), 1)
    sel = jnp.where(lanes == pos[:, 0:1], gate[:, 0:1], 0.0)
    for k in range(1, TOP_K):
        sel = sel + jnp.where(lanes == pos[:, k:k + 1], gate[:, k:k + 1], 0.0)
    sel_hi = sel.astype(BF16)
    sel_lo = (sel - sel_hi.astype(F32)).astype(BF16)
    y_lo, y_hi = _unpack_halves(buf[slot])
    ffn = jnp.concatenate([_dot(sel_hi, y_lo) + _dot(sel_lo, y_lo), _dot(sel_hi, y_hi) + _dot(sel_lo, y_hi)], axis=1)
    o_ref[...] = _layer_norm(DEEPNORM_ALPHA * h1_ref[...] + ffn, g2_ref[...], b2_ref[...])


def _combine(ngroups, dstg, pos_c, gates, h1, ys, g2, b2):
    t, d = h1.shape
    tm = ROW_TILE
    n_steps = t // tm
    row = lambda i, ng: (i, 0)
    const = lambda i, ng: (0, 0)
    grid_spec = pltpu.PrefetchScalarGridSpec(
        num_scalar_prefetch=1,
        grid=(n_steps,),
        in_specs=[
            pl.BlockSpec((1, 1, SORT_GROUPS), lambda i, ng: (i, 0, 0), memory_space=pltpu.SMEM),
            pl.BlockSpec((1, 1, SORT_GROUPS), lambda i, ng: (jnp.minimum(i + 1, n_steps - 1), 0, 0),
                         memory_space=pltpu.SMEM),
            pl.BlockSpec((tm, TOP_K), row), pl.BlockSpec((tm, TOP_K), row), pl.BlockSpec((tm, d), row),
            pl.BlockSpec(memory_space=pl.ANY),
            pl.BlockSpec((1, d), const), pl.BlockSpec((1, d), const),
        ],
        out_specs=pl.BlockSpec((tm, d), row),
        scratch_shapes=[pltpu.VMEM((2, SORT_ROWS, d // 2), I32), pltpu.SemaphoreType.DMA((2,))],
    )
    return pl.pallas_call(
        _combine_kernel,
        grid_spec=grid_spec,
        out_shape=jax.ShapeDtypeStruct((t, d), F32),
        compiler_params=pltpu.CompilerParams(dimension_semantics=("arbitrary",), vmem_limit_bytes=VMEM_LIMIT),
        name="moe_combine_ln2",
    )(ngroups, dstg, dstg, pos_c, gates, h1, ys, g2, b2)


def kernel(x, meta_tokens, ln_in_g, ln_in_b, rel_bias, w_in, lambda_q1, lambda_k1, lambda_q2, lambda_k2,
           subln_g, a_re, a_im, log_step, b_re, b_im, c_re, c_im, d_skip, w_glu, b_glu, w_out, ln1_g, ln1_b,
           w_router, b_router, w_gate, b_gate, w_up, b_up, w_down, b_down, ln2_g, ln2_b):
    bsz, seq, dm = x.shape
    assert seq % 512 == 0 and w_in.shape[0] == DEPTH == 1
    layer = 0
    row2 = lambda v: v.astype(F32).reshape(1, -1)

    w_in_b = w_in[layer].astype(BF16)
    gi, bi = row2(ln_in_g), row2(ln_in_b)
    q, k, vt, u = _inproj(x, gi, bi, w_in_b, tm=512, kv_tile=KV_TILE)
    meta = jnp.zeros((1, META_PAD, dm), x.dtype).at[0, :N_META].set(meta_tokens.astype(x.dtype))
    _, k_meta, vt_meta, u_meta = _inproj(meta, gi, bi, w_in_b, tm=META_PAD, kv_tile=META_PAD)

    lam_init = 0.8 - 0.6 * math.exp(-0.3 * layer)
    lam = (jnp.exp(jnp.sum(lambda_q1[layer].astype(F32) * lambda_k1[layer].astype(F32)))
           - jnp.exp(jnp.sum(lambda_q2[layer].astype(F32) * lambda_k2[layer].astype(F32))) + lam_init)
    att = _attention(lam.reshape(1), q, k, vt, k_meta[0], vt_meta[0, :, 0], _near_bias(rel_bias),
                     subln_g[layer].astype(F32).reshape(HEAD_W, 1), lam_init=lam_init)

    kpad, pt, qt, a1, a2 = _ssm_tables(a_re[layer], a_im[layer], log_step[layer], b_re[layer], b_im[layer],
                                     c_re[layer], c_im[layer], d_skip[layer])
    n_chunks = -(-(seq // SSM_T + 1) // 8) * 8
    lead = jnp.zeros((bsz, SSM_T, SSM_W), BF16).at[:, SSM_T - N_META:].set(u_meta[:, :N_META])
    tail = jnp.zeros((bsz, n_chunks * SSM_T - SSM_T - seq, SSM_W), BF16)
    ug = jnp.concatenate([lead, u, tail], axis=1).reshape(bsz, n_chunks, SSM_T, SSM_G, SSM_CG)
    ug = jnp.transpose(ug, (3, 1, 0, 2, 4)).reshape(SSM_G, n_chunks * bsz, SSM_T * SSM_CG)
    yg = _ssm(ug, kpad, pt, qt, a1, a2, n_chunks=n_chunks, bsz=bsz)
    yg = jnp.transpose(yg.reshape(SSM_G, n_chunks, bsz, SSM_T, SSM_CG), (2, 1, 3, 0, 4))
    y_ssm = yg.reshape(bsz, n_chunks * SSM_T, SSM_W)[:, SSM_T:SSM_T + seq]

    t = bsz * seq
    h1, idx_t, gate_t, rank_t, cnt = _mix(
        x.reshape(t, dm), gi, bi, att.reshape(t, ATT_W), y_ssm.reshape(t, SSM_W),
        w_glu[layer].astype(BF16), row2(b_glu[layer]), w_out[layer].astype(BF16),
        row2(ln1_g[layer]), row2(ln1_b[layer]),
        jnp.transpose(w_router[layer].astype(F32)), b_router[layer].astype(F32).reshape(N_EXPERTS, 1))

    n_tiles = t // ROW_TILE
    experts = jnp.arange(N_EXPERTS, dtype=I32)
    tiles = jnp.arange(n_tiles, dtype=I32)
    run = (cnt[:, :, 0].astype(I32) + GROUP - 1) // GROUP * GROUP
    run_off = jnp.sum(jnp.where((experts[:, None] < experts[None, :])[None], run[:, :, None], 0), axis=1)
    run_before = jnp.sum(jnp.where((tiles[:, None] < tiles[None, :])[:, :, None], run[:, None, :], 0), axis=0)
    counts = jnp.sum(run, axis=0)
    padded = (counts + MOE_BLOCK - 1) // MOE_BLOCK * MOE_BLOCK
    padded_end = jnp.sum(jnp.where(experts[:, None] <= experts[None, :], padded[:, None], 0), axis=0)
    padded_start = padded_end - padded
    n_blocks = (t * TOP_K + n_tiles * N_EXPERTS * (GROUP - 1)) // MOE_BLOCK + N_EXPERTS
    block_row0 = jnp.arange(n_blocks, dtype=I32) * MOE_BLOCK
    block_expert = jnp.minimum(jnp.sum((padded_end[None, :] <= block_row0[:, None]).astype(I32), axis=1),
                               N_EXPERTS - 1)
    n_used = (padded_end[-1:] // MOE_BLOCK).astype(I32)
    ngroups = jnp.sum(run, axis=1) // GROUP
    local_row = jnp.arange(SORT_GROUPS, dtype=I32) * GROUP
    owner = jnp.minimum(jnp.sum(((run_off + run)[:, None, :] <= local_row[None, :, None]).astype(I32), axis=-1),
                        N_EXPERTS - 1)
    shift = (padded_start[None, :] + run_before - run_off) // GROUP
    dstg = jnp.sum(jnp.where(owner[..., None] == experts, shift[:, None, :], 0), axis=-1) + local_row // GROUP
    dstg = jnp.where(local_row[None, :] // GROUP < ngroups[:, None], dstg, 0).reshape(n_tiles, 1, SORT_GROUPS)
    run_off_tok = jnp.repeat(run_off, ROW_TILE, axis=0)
    pos_t = jnp.sum(jnp.where(idx_t[..., None] == experts, run_off_tok[None], 0), axis=-1) + rank_t

    xs = _dispatch(ngroups, ((padded_start + counts) // GROUP).astype(I32),
                   ((padded - counts) // GROUP).astype(I32), n_used, dstg, pos_t, h1, n_blocks * MOE_BLOCK)
    b3 = lambda v: v.astype(F32)[:, None, :]
    ys = _experts(block_expert, n_used, xs, w_gate[layer], b3(b_gate[layer]), w_up[layer], b3(b_up[layer]),
                  w_down[layer], b3(b_down[layer]))
    out = _combine(ngroups, dstg, jnp.transpose(pos_t), jnp.transpose(gate_t), h1, ys,
                   row2(ln2_g[layer]), row2(ln2_b[layer]))
    return out.reshape(bsz, seq, dm)
```

```python
import functools
import math

import jax
import jax.numpy as jnp
import numpy as np
from jax import lax
from jax.experimental import pallas as pl
from jax.experimental.pallas import tpu as pltpu

F32 = jnp.float32
BF16 = jnp.bfloat16
I32 = jnp.int32

DEPTH = 1
N_META = 16
CHUNK = 64
ATT_HEADS = 4
HEAD_DIM = 64
HEAD_W = 2 * HEAD_DIM
ATT_W = ATT_HEADS * HEAD_W
SSM_W = 512
SSM_CG = 16
SSM_G = SSM_W // SSM_CG
SSM_N = 64
N_BUCKETS = 32
MAX_DISTANCE = 128
N_EXPERTS = 32
TOP_K = 4
SWIGLU_LIMIT = 7.0
SWIGLU_ALPHA = 1.702
LN_EPS = 1e-5
NEG_INF = -1e30
DEEPNORM_ALPHA = (2.0 * DEPTH) ** 0.25
LOG2E = 1.4426950408889634

Q_TILE = 512
KV_TILE = 512
V_ONES = 16
V_ROWS = HEAD_W + V_ONES
META_PAD = 128
SSM_T = 64
MIX_TILE = 512
ROW_TILE = 256
MOE_BLOCK = 512
GROUP = 8
SORT_ROWS = 1280
SORT_GROUPS = SORT_ROWS // GROUP
assert SORT_ROWS >= ROW_TILE * TOP_K + N_EXPERTS * (GROUP - 1)
VMEM_LIMIT = 56 * 1024 * 1024


def _layer_norm(x, g, b):
    mu = jnp.mean(x, axis=-1, keepdims=True)
    xc = x - mu
    var = jnp.mean(xc * xc, axis=-1, keepdims=True)
    return xc * lax.rsqrt(var + LN_EPS) * g + b


def _dot(a, b):
    return jnp.dot(a, b, preferred_element_type=F32)


def _dot_nt(a, b, precision=None):
    return lax.dot_general(a, b, (((1,), (1,)), ((), ())), precision=precision,
                           preferred_element_type=F32)


def _inproj_kernel(x_ref, g_ref, b_ref, w_ref, q_ref, k_ref, vt_ref, u_ref, *, kv_tile, q_scale):
    h = _layer_norm(x_ref[0], g_ref[...], b_ref[...]).astype(BF16)
    tm = h.shape[0]
    q_ref[0] = (_dot(h, w_ref[:, 0:ATT_W]) * q_scale).astype(BF16)
    k_ref[0] = _dot(h, w_ref[:, ATT_W:2 * ATT_W]).astype(BF16)
    v = _dot(h, w_ref[:, 2 * ATT_W:3 * ATT_W])
    ones = jnp.ones((V_ONES, kv_tile), BF16)
    for hh in range(ATT_HEADS):
        vt = v[:, hh * HEAD_W:(hh + 1) * HEAD_W].T.astype(BF16)
        for j in range(tm // kv_tile):
            vt_ref[0, hh, j, 0:HEAD_W, :] = vt[:, j * kv_tile:(j + 1) * kv_tile]
            vt_ref[0, hh, j, HEAD_W:V_ROWS, :] = ones
    u_ref[0] = _dot(h, w_ref[:, 3 * ATT_W:]).astype(BF16)


def _inproj(x, g, b, w, *, tm, kv_tile):
    bsz, s, d = x.shape
    q_scale = HEAD_DIM ** -0.5 * LOG2E
    n_cols = w.shape[1]
    row = lambda bi, i: (bi, i, 0)
    return pl.pallas_call(
        functools.partial(_inproj_kernel, kv_tile=kv_tile, q_scale=q_scale),
        grid=(bsz, s // tm),
        in_specs=[
            pl.BlockSpec((1, tm, d), row),
            pl.BlockSpec((1, d), lambda bi, i: (0, 0)),
            pl.BlockSpec((1, d), lambda bi, i: (0, 0)),
            pl.BlockSpec((d, n_cols), lambda bi, i: (0, 0)),
        ],
        out_specs=[
            pl.BlockSpec((1, tm, ATT_W), row),
            pl.BlockSpec((1, tm, ATT_W), row),
            pl.BlockSpec((1, ATT_HEADS, tm // kv_tile, V_ROWS, kv_tile), lambda bi, i: (bi, 0, i, 0, 0)),
            pl.BlockSpec((1, tm, SSM_W), row),
        ],
        out_shape=[
            jax.ShapeDtypeStruct((bsz, s, ATT_W), BF16),
            jax.ShapeDtypeStruct((bsz, s, ATT_W), BF16),
            jax.ShapeDtypeStruct((bsz, ATT_HEADS, s // kv_tile, V_ROWS, kv_tile), BF16),
            jax.ShapeDtypeStruct((bsz, s, SSM_W), BF16),
        ],
        compiler_params=pltpu.CompilerParams(
            dimension_semantics=("parallel", "parallel"), vmem_limit_bytes=VMEM_LIMIT),
        name="inproj",
    )(x, g, b, w)


def _attn_kernel(lam_ref, q_ref, k_ref, vt_ref, km_ref, vtm_ref, bias_ref, g_ref, o_ref,
                 m_ref, acc_ref, qz_ref, sa_ref, sb_ref, sn_ref, *, out_scale):
    i = pl.program_id(2)
    q = q_ref[0]
    lane = lax.broadcasted_iota(I32, q.shape, 1)
    zero = jnp.zeros_like(q)
    qz_ref[0:Q_TILE, :] = jnp.where(lane < HEAD_DIM, q, zero)
    qz_ref[Q_TILE:, :] = jnp.where(lane >= HEAD_DIM, q, zero)

    m_ref[...] = jnp.full(m_ref.shape, NEG_INF, F32)
    acc_ref[...] = jnp.zeros(acc_ref.shape, F32)

    def scores(kt):
        return _dot_nt(kt, qz_ref[...])

    def k_tile(t):
        return k_ref[0, pl.ds(pl.multiple_of(t * KV_TILE, KV_TILE), KV_TILE), :]

    def absorb(s, pv):
        m_old = m_ref[...]
        m_new = jnp.maximum(m_old, jnp.max(s, axis=0, keepdims=True))
        alpha = jnp.exp2(m_old - m_new)
        p = jnp.exp2(s - m_new).astype(BF16)
        acc_ref[...] = acc_ref[...] * alpha + pv(p)
        m_ref[...] = m_new

    def tile_pv(t):
        return lambda p: _dot(vt_ref[0, 0, t], p)

    n_far = jnp.maximum(i - 1, 0)
    peel = n_far % 2

    sn_ref[...] = scores(km_ref[...])
    sa_ref[...] = scores(k_tile(0))
    absorb(sn_ref[...] + bias_ref[0, 0, 2 * KV_TILE:, :], lambda p: _dot(vtm_ref[0], p))

    @pl.when(peel == 1)
    def _():
        absorb(sa_ref[...], tile_pv(0))
        sa_ref[...] = scores(k_tile(1))

    def far_pair(j, carry):
        t0 = peel + 2 * j
        s_cur = sa_ref[...]
        sb_ref[...] = scores(k_tile(t0 + 1))
        absorb(s_cur, tile_pv(t0))
        s_cur = sb_ref[...]
        sa_ref[...] = scores(k_tile(t0 + 2))
        absorb(s_cur, tile_pv(t0 + 1))
        return carry

    lax.fori_loop(0, n_far // 2, far_pair, 0)

    sb_ref[...] = scores(k_tile(i))
    absorb(sa_ref[...] + bias_ref[0, 0, 0:KV_TILE, :], tile_pv(n_far))
    absorb(sb_ref[...] + bias_ref[0, 0, KV_TILE:2 * KV_TILE, :], tile_pv(i))

    acc = acc_ref[...]
    lam = lam_ref[0]
    o1 = acc[0:HEAD_W, 0:Q_TILE] / acc[HEAD_W:HEAD_W + 1, 0:Q_TILE]
    o2 = acc[0:HEAD_W, Q_TILE:] / acc[HEAD_W:HEAD_W + 1, Q_TILE:]
    o = o1 - lam * o2
    ms = jnp.mean(o * o, axis=0, keepdims=True)
    o = o * lax.rsqrt(ms + LN_EPS) * g_ref[...] * out_scale
    o_ref[0] = o.T.astype(o_ref.dtype)


def _attention(lam, q, k, vt, k_meta, vt_meta, bias, subln_g, *, lam_init):
    bsz, s, _ = q.shape
    nq = s // Q_TILE
    n_near = META_PAD + 2 * KV_TILE
    grid_spec = pltpu.PrefetchScalarGridSpec(
        num_scalar_prefetch=1,
        grid=(bsz, ATT_HEADS, nq),
        in_specs=[
            pl.BlockSpec((1, Q_TILE, HEAD_W), lambda b, h, i, lam: (b, i, h)),
            pl.BlockSpec((1, s, HEAD_W), lambda b, h, i, lam: (b, 0, h)),
            pl.BlockSpec((1, 1, s // KV_TILE, V_ROWS, KV_TILE), lambda b, h, i, lam: (b, h, 0, 0, 0)),
            pl.BlockSpec((META_PAD, HEAD_W), lambda b, h, i, lam: (0, h)),
            pl.BlockSpec((1, V_ROWS, META_PAD), lambda b, h, i, lam: (h, 0, 0)),
            pl.BlockSpec((1, 1, n_near, 2 * Q_TILE), lambda b, h, i, lam: (h, jnp.minimum(i, 1), 0, 0)),
            pl.BlockSpec((HEAD_W, 1), lambda b, h, i, lam: (0, 0)),
        ],
        out_specs=pl.BlockSpec((1, Q_TILE, HEAD_W), lambda b, h, i, lam: (b, i, h)),
        scratch_shapes=[
            pltpu.VMEM((1, 2 * Q_TILE), F32), pltpu.VMEM((V_ROWS, 2 * Q_TILE), F32),
            pltpu.VMEM((2 * Q_TILE, HEAD_W), BF16),
            pltpu.VMEM((KV_TILE, 2 * Q_TILE), F32), pltpu.VMEM((KV_TILE, 2 * Q_TILE), F32),
            pltpu.VMEM((META_PAD, 2 * Q_TILE), F32),
        ],
    )
    return pl.pallas_call(
        functools.partial(_attn_kernel, out_scale=1.0 - lam_init),
        grid_spec=grid_spec,
        out_shape=jax.ShapeDtypeStruct((bsz, s, ATT_W), BF16),
        compiler_params=pltpu.CompilerParams(
            dimension_semantics=("parallel", "parallel", "arbitrary"), vmem_limit_bytes=VMEM_LIMIT),
        name="diff_attention",
    )(lam, q, k, vt, k_meta, vt_meta, bias, subln_g)


def _t5_bucket(rel):
    nb = N_BUCKETS // 2
    max_exact = nb // 2
    ret = jnp.where(rel > 0, nb, 0)
    n = jnp.abs(rel)
    n_f = jnp.maximum(n, 1).astype(F32)
    large = max_exact + (jnp.log(n_f / max_exact) / math.log(MAX_DISTANCE / max_exact)
                         * (nb - max_exact)).astype(I32)
    large = jnp.minimum(large, nb - 1)
    return ret + jnp.where(n < max_exact, n, large)


def _near_bias(rel_bias):
    table = rel_bias.astype(F32)
    far = table[N_BUCKETS // 2 - 1]
    c = jnp.arange(Q_TILE, dtype=I32)[None, :]
    r = jnp.arange(KV_TILE, dtype=I32)[:, None]

    shifted = jnp.transpose(table - far) * LOG2E

    def bias_of(offset, rows, cols=Q_TILE):
        rel = jnp.arange(rows, dtype=I32)[:, None] - c[:, :cols] + offset
        hot = _t5_bucket(rel)[None, :, :, None] == jnp.arange(N_BUCKETS, dtype=I32)
        return jnp.sum(jnp.where(hot, shifted[:, None, None, :], 0.0), axis=-1)

    own = jnp.where((r // CHUNK <= c // CHUNK)[None], bias_of(0, KV_TILE), NEG_INF)
    near = MAX_DISTANCE
    prev = jnp.zeros((ATT_HEADS, KV_TILE, Q_TILE), F32).at[:, KV_TILE - near:, :near].set(bias_of(-near, near, near))
    rm = jnp.arange(META_PAD, dtype=I32)[:, None]
    meta_ok = (rm < N_META)[None]
    meta0 = jnp.where(meta_ok, jnp.pad(bias_of(-N_META, N_META), ((0, 0), (0, META_PAD - N_META), (0, 0))), NEG_INF)
    meta1 = jnp.where(meta_ok, jnp.zeros((ATT_HEADS, META_PAD, Q_TILE), F32), NEG_INF)
    assert N_META + Q_TILE - (N_META - 1) >= MAX_DISTANCE and KV_TILE >= near and Q_TILE >= near
    v0 = jnp.concatenate([jnp.full_like(prev, NEG_INF), own, meta0], axis=1)
    v1 = jnp.concatenate([prev, own, meta1], axis=1)
    both = jnp.stack([v0, v1], axis=1)
    return jnp.concatenate([both, both], axis=-1)


def _ssm_tables(a_re, a_im, log_step, b_re, b_im, c_re, c_im, d_skip):
    hi = lax.Precision.HIGHEST
    t_len = SSM_T
    step = jnp.exp(log_step.astype(F32))[:, None]
    ar = jnp.minimum(a_re.astype(F32), -1e-4)
    ai = a_im.astype(F32)
    mag = jnp.exp(step * ar)
    ph = step * ai
    abar_re = mag * jnp.cos(ph)
    abar_im = mag * jnp.sin(ph)
    den = ar * ar + ai * ai
    e_re = abar_re - 1.0
    e_im = abar_im
    f_re = (e_re * ar + e_im * ai) / den
    f_im = (e_im * ar - e_re * ai) / den
    br = b_re.astype(F32)
    bi = b_im.astype(F32)
    bb_re = f_re[..., None] * br - f_im[..., None] * bi
    bb_im = f_re[..., None] * bi + f_im[..., None] * br
    tau = jnp.arange(t_len + 1, dtype=F32)[None, :, None]
    pmag = jnp.exp(tau * (step * ar)[:, None, :])
    pph = tau * ph[:, None, :]
    pw_re = pmag * jnp.cos(pph)
    pw_im = pmag * jnp.sin(pph)
    w_re = pw_re[..., None] * bb_re[:, None] - pw_im[..., None] * bb_im[:, None]
    w_im = pw_re[..., None] * bb_im[:, None] + pw_im[..., None] * bb_re[:, None]
    cr = c_re.astype(F32)
    ci = c_im.astype(F32)
    kern = (jnp.einsum('gcn,gtnk->gtck', cr, w_re[:, :t_len], precision=hi)
            - jnp.einsum('gcn,gtnk->gtck', ci, w_im[:, :t_len], precision=hi))
    skip = d_skip.astype(F32).reshape(SSM_G, SSM_CG)
    kern = kern.at[:, 0].add(skip[:, :, None] * jnp.eye(SSM_CG, dtype=F32))
    kflat = jnp.transpose(kern, (0, 3, 1, 2)).reshape(SSM_G, SSM_CG, t_len * SSM_CG)
    kpad = jnp.concatenate([jnp.zeros_like(kflat), kflat], axis=-1)
    pt_re = jnp.transpose(jnp.flip(w_re[:, :t_len], axis=1), (0, 1, 3, 2)).reshape(SSM_G, t_len * SSM_CG, SSM_N)
    pt_im = jnp.transpose(jnp.flip(w_im[:, :t_len], axis=1), (0, 1, 3, 2)).reshape(SSM_G, t_len * SSM_CG, SSM_N)
    pt = jnp.concatenate([pt_re, pt_im], axis=-1)
    up_re = pw_re[:, 1:]
    up_im = pw_im[:, 1:]
    q_re = cr[:, None] * up_re[:, :, None, :] - ci[:, None] * up_im[:, :, None, :]
    q_im = cr[:, None] * up_im[:, :, None, :] + ci[:, None] * up_re[:, :, None, :]
    qt = jnp.concatenate([jnp.transpose(q_re, (0, 3, 1, 2)), -jnp.transpose(q_im, (0, 3, 1, 2))], axis=1)
    qt = qt.reshape(SSM_G, 2 * SSM_N, t_len * SSM_CG)
    at_re = pw_re[:, t_len]
    at_im = pw_im[:, t_len]
    a1 = jnp.concatenate([at_re, at_re], axis=-1)
    a2 = jnp.concatenate([-at_im, at_im], axis=-1)
    return kpad, pt.astype(BF16), qt.astype(BF16), a1, a2


def _ssm_state_kernel(u_ref, pt_ref, s_ref):
    s_ref[0] = _dot(u_ref[0], pt_ref[0])


def _ssm_scan_kernel(s_ref, a1_ref, a2_ref, x_ref, *, n_chunks, bsz):
    a1 = a1_ref[...][:, None, :]
    a2 = a2_ref[...][:, None, :]

    def body(c, x):
        rows = pl.ds(c * bsz, bsz)
        x_ref[:, rows, :] = x
        return a1 * x + a2 * pltpu.roll(x, SSM_N, 2) + s_ref[:, rows, :]

    lax.fori_loop(0, n_chunks, body, jnp.zeros((SSM_G, bsz, 2 * SSM_N), F32))


def _ssm_out_kernel(u_ref, x_ref, kp_ref, qt_ref, y_ref, mt_ref):
    kp = kp_ref[0]
    width = SSM_T * SSM_CG
    lane_tile = 128
    for sub in range(0, lane_tile, SSM_CG):
        shifted = kp if sub == 0 else pltpu.roll(kp, 2 * width - sub, 1)
        for s in range(SSM_T):
            off = (SSM_T - s) * SSM_CG
            if off % lane_tile == sub:
                base = off - sub
                mt_ref[s * SSM_CG:(s + 1) * SSM_CG, :] = shifted[:, base:base + width].astype(BF16)
    x = x_ref[0]
    x_hi = x.astype(BF16)
    x_lo = (x - x_hi.astype(F32)).astype(BF16)
    y = _dot(u_ref[0], mt_ref[...]) + _dot(x_hi, qt_ref[0]) + _dot(x_lo, qt_ref[0])
    y_ref[0] = y.astype(y_ref.dtype)


def _ssm(ug, kpad, pt, qt, a1, a2, *, n_chunks, bsz):
    g, r, w = ug.shape
    n2 = 2 * SSM_N
    per_g = lambda gi: (gi, 0, 0)
    params = pltpu.CompilerParams(dimension_semantics=("parallel",), vmem_limit_bytes=VMEM_LIMIT)
    s = pl.pallas_call(
        _ssm_state_kernel,
        grid=(g,),
        in_specs=[pl.BlockSpec((1, r, w), per_g), pl.BlockSpec((1, w, n2), per_g)],
        out_specs=pl.BlockSpec((1, r, n2), per_g),
        out_shape=jax.ShapeDtypeStruct((g, r, n2), F32),
        compiler_params=params,
        name="ssm_chunk_state",
    )(ug, pt)
    x = pl.pallas_call(
        functools.partial(_ssm_scan_kernel, n_chunks=n_chunks, bsz=bsz),
        out_shape=jax.ShapeDtypeStruct((g, r, n2), F32),
        compiler_params=pltpu.CompilerParams(vmem_limit_bytes=VMEM_LIMIT),
        name="ssm_chunk_scan",
    )(s, a1, a2)
    return pl.pallas_call(
        _ssm_out_kernel,
        grid=(g,),
        in_specs=[pl.BlockSpec((1, r, w), per_g), pl.BlockSpec((1, r, n2), per_g),
                  pl.BlockSpec((1, SSM_CG, 2 * w), per_g), pl.BlockSpec((1, n2, w), per_g)],
        out_specs=pl.BlockSpec((1, r, w), per_g),
        out_shape=jax.ShapeDtypeStruct((g, r, w), BF16),
        scratch_shapes=[pltpu.VMEM((w, w), BF16)],
        compiler_params=params,
        name="ssm_output",
    )(ug, x, kpad, qt)


def _mix_kernel(x_ref, gi_ref, bi_ref, att_ref, y_ref, wglu_ref, bglu_ref, wout_ref, g1_ref, b1_ref,
                wr_ref, br_ref, h1_ref, idx_ref, gate_ref, rank_ref, cnt_ref):
    h0 = _layer_norm(x_ref[...], gi_ref[...], bi_ref[...])
    y = y_ref[...].astype(F32)
    y = y * (0.5 * (1.0 + jnp.tanh(math.sqrt(2.0 / math.pi) * (y + 0.044715 * (y * y * y)))))
    y = y * jax.nn.sigmoid(_dot(y.astype(BF16), wglu_ref[...]) + bglu_ref[...])
    mix = _dot(att_ref[...], wout_ref[0:ATT_W, :]) + _dot(y.astype(BF16), wout_ref[ATT_W:, :])
    h1 = _layer_norm(DEEPNORM_ALPHA * h0 + mix, g1_ref[...], b1_ref[...])
    h1_ref[...] = h1

    logits = _dot_nt(wr_ref[...], h1, precision=lax.Precision.HIGHEST) + br_ref[...]
    tm = logits.shape[1]
    eidx = lax.broadcasted_iota(I32, logits.shape, 0)
    vals, hots = [], []
    rest = logits
    for _ in range(TOP_K):
        mx = jnp.max(rest, axis=0, keepdims=True)
        first = jnp.min(jnp.where(rest == mx, eidx, N_EXPERTS), axis=0, keepdims=True)
        hot = eidx == first
        vals.append(mx)
        hots.append(hot)
        rest = jnp.where(hot, -jnp.inf, rest)
    exps = [jnp.exp(v - vals[0]) for v in vals]
    denom = exps[0] + exps[1] + exps[2] + exps[3]
    gate_ref[...] = jnp.concatenate([e / denom for e in exps], axis=0)
    idx_ref[...] = jnp.concatenate(
        [jnp.sum(jnp.where(h, eidx, 0), axis=0, keepdims=True) for h in hots], axis=0)

    hot_all = (hots[0] | hots[1] | hots[2] | hots[3]).astype(F32)
    sub = ROW_TILE
    tri = (lax.broadcasted_iota(I32, (sub, sub), 0) < lax.broadcasted_iota(I32, (sub, sub), 1)).astype(BF16)
    for part in range(tm // sub):
        cols = slice(part * sub, (part + 1) * sub)
        before = _dot(hot_all[:, cols].astype(BF16), tri)
        rank_ref[:, cols] = jnp.concatenate(
            [jnp.sum(jnp.where(h[:, cols], before, 0.0), axis=0, keepdims=True) for h in hots], axis=0).astype(I32)
        cnt_ref[part] = jnp.broadcast_to(jnp.sum(hot_all[:, cols], axis=1, keepdims=True), cnt_ref.shape[1:])


def _mix(x2, gi, bi, att, y, wglu, bglu, wout, g1, b1, wr_t, br):
    t, d = x2.shape
    tm = MIX_TILE
    row = lambda i: (i, 0)
    col = lambda i: (0, i)
    const = lambda i: (0, 0)
    return pl.pallas_call(
        _mix_kernel,
        grid=(t // tm,),
        in_specs=[
            pl.BlockSpec((tm, d), row), pl.BlockSpec((1, d), const), pl.BlockSpec((1, d), const),
            pl.BlockSpec((tm, ATT_W), row), pl.BlockSpec((tm, SSM_W), row),
            pl.BlockSpec((SSM_W, SSM_W), const), pl.BlockSpec((1, SSM_W), const),
            pl.BlockSpec((d, d), const), pl.BlockSpec((1, d), const), pl.BlockSpec((1, d), const),
            pl.BlockSpec((N_EXPERTS, d), const), pl.BlockSpec((N_EXPERTS, 1), const),
        ],
        out_specs=[
            pl.BlockSpec((tm, d), row),
            pl.BlockSpec((TOP_K, tm), col), pl.BlockSpec((TOP_K, tm), col), pl.BlockSpec((TOP_K, tm), col),
            pl.BlockSpec((tm // ROW_TILE, N_EXPERTS, 128), lambda i: (i, 0, 0)),
        ],
        out_shape=[
            jax.ShapeDtypeStruct((t, d), F32),
            jax.ShapeDtypeStruct((TOP_K, t), I32), jax.ShapeDtypeStruct((TOP_K, t), F32),
            jax.ShapeDtypeStruct((TOP_K, t), I32),
            jax.ShapeDtypeStruct((t // ROW_TILE, N_EXPERTS, 128), F32),
        ],
        compiler_params=pltpu.CompilerParams(dimension_semantics=("parallel",), vmem_limit_bytes=VMEM_LIMIT),
        name="mix_ln1_router",
    )(x2, gi, bi, att, y, wglu, bglu, wout, g1, b1, wr_t, br)


_HI_BITS = -65536


def _pack_halves(x):
    half = x.shape[1] // 2
    lo = lax.bitcast_convert_type(x[:, :half], I32)
    hi = lax.bitcast_convert_type(x[:, half:], I32)
    return lax.shift_right_logical(lo, 16) | (hi & _HI_BITS)


def _unpack_halves(w):
    lo = lax.bitcast_convert_type(lax.shift_left(w, 16), F32)
    hi = lax.bitcast_convert_type(w & _HI_BITS, F32)
    return lo.astype(BF16), hi.astype(BF16)


def _group_copy(src, src_group, dst, dst_group, sem):
    return pltpu.make_async_copy(src.at[pl.ds(pl.multiple_of(src_group * GROUP, GROUP), GROUP), :],
                                 dst.at[pl.ds(pl.multiple_of(dst_group * GROUP, GROUP), GROUP), :], sem)


def _dispatch_kernel(ngroups_ref, pad_start_ref, pad_count_ref, n_used_ref, dstg_ref, pos_ref, h1_ref, xs_hbm,
                     sorted_ref, zero_ref, sems, zsem):
    step = pl.program_id(0)
    n_steps = pl.num_programs(0)
    slot = step % 2
    n_blocks = xs_hbm.shape[0] // MOE_BLOCK

    def zero_block(blk):
        return pltpu.make_async_copy(zero_ref, xs_hbm.at[pl.ds(blk * MOE_BLOCK, MOE_BLOCK), :], zsem)

    @pl.when(step == 0)
    def _():
        zero_ref[...] = jnp.zeros(zero_ref.shape, I32)
        for e in range(N_EXPERTS):
            start = pad_start_ref[e]
            count = pad_count_ref[e]

            def fill(j, c):
                _group_copy(zero_ref, 0, xs_hbm, start + j, zsem).start()
                return c

            def drain(j, c):
                _group_copy(zero_ref, 0, xs_hbm, start, zsem).wait()
                return c

            lax.fori_loop(0, count, fill, 0)
            lax.fori_loop(0, count, drain, 0)

        def fill_block(blk, c):
            zero_block(blk).start()
            return c

        def drain_block(blk, c):
            zero_block(blk).wait()
            return c

        lax.fori_loop(n_used_ref[0], n_blocks, fill_block, 0)
        lax.fori_loop(n_used_ref[0], n_blocks, drain_block, 0)

    pos = pos_ref[...]
    rows = lax.broadcasted_iota(I32, (SORT_ROWS, pos.shape[1]), 0)
    place = rows == pos[0:1]
    for k in range(1, TOP_K):
        place = place | (rows == pos[k:k + 1])
    sorted_ref[slot] = _pack_halves(_dot(place.astype(BF16), h1_ref[...].astype(BF16)))

    def issue(j, c):
        _group_copy(sorted_ref.at[slot], j, xs_hbm, dstg_ref[0, 0, j], sems.at[slot]).start()
        return c

    lax.fori_loop(0, ngroups_ref[step], issue, 0)

    def wait_tile(s, count):
        def drain(j, c):
            _group_copy(sorted_ref.at[s], 0, xs_hbm, 0, sems.at[s]).wait()
            return c
        lax.fori_loop(0, count, drain, 0)

    @pl.when(step > 0)
    def _():
        wait_tile(1 - slot, ngroups_ref[jnp.maximum(step - 1, 0)])

    @pl.when(step == n_steps - 1)
    def _():
        wait_tile(slot, ngroups_ref[step])


def _dispatch(ngroups, pad_start, pad_count, n_used, dstg, pos_t, h1, n_rows):
    t, d = h1.shape
    tm = ROW_TILE
    grid_spec = pltpu.PrefetchScalarGridSpec(
        num_scalar_prefetch=4,
        grid=(t // tm,),
        in_specs=[
            pl.BlockSpec((1, 1, SORT_GROUPS), lambda i, *_: (i, 0, 0), memory_space=pltpu.SMEM),
            pl.BlockSpec((TOP_K, tm), lambda i, *_: (0, i)),
            pl.BlockSpec((tm, d), lambda i, *_: (i, 0)),
        ],
        out_specs=pl.BlockSpec(memory_space=pl.ANY),
        scratch_shapes=[pltpu.VMEM((2, SORT_ROWS, d // 2), I32), pltpu.VMEM((MOE_BLOCK, d // 2), I32),
                        pltpu.SemaphoreType.DMA((2,)), pltpu.SemaphoreType.DMA],
    )
    return pl.pallas_call(
        _dispatch_kernel,
        grid_spec=grid_spec,
        out_shape=jax.ShapeDtypeStruct((n_rows, d // 2), I32),
        compiler_params=pltpu.CompilerParams(dimension_semantics=("arbitrary",), vmem_limit_bytes=VMEM_LIMIT,
                                             has_side_effects=True),
        name="moe_dispatch",
    )(ngroups, pad_start, pad_count, n_used, dstg, pos_t, h1)


def _expert_kernel(be_ref, nb_ref, x_ref, wg_ref, bg_ref, wu_ref, bu_ref, wd_ref, bd_ref, y_ref,
                   wg_b, wu_b, wd_b):
    i = pl.program_id(0)

    @pl.when(i < nb_ref[0])
    def _():
        @pl.when((i == 0) | (be_ref[i] != be_ref[jnp.maximum(i - 1, 0)]))
        def _():
            wg_b[...] = wg_ref[0].astype(BF16)
            wu_b[...] = wu_ref[0].astype(BF16)
            wd_b[...] = wd_ref[0].astype(BF16)

        x_lo, x_hi = _unpack_halves(x_ref[...])
        half = x_lo.shape[1]
        gate = _dot(x_lo, wg_b[0:half, :]) + _dot(x_hi, wg_b[half:, :]) + bg_ref[0]
        up = _dot(x_lo, wu_b[0:half, :]) + _dot(x_hi, wu_b[half:, :]) + bu_ref[0]
        gate = jnp.minimum(gate, SWIGLU_LIMIT)
        up = jnp.clip(up, -SWIGLU_LIMIT, SWIGLU_LIMIT)
        act = (up + 1.0) * gate * jax.nn.sigmoid(gate * SWIGLU_ALPHA)
        y = _dot(act.astype(BF16), wd_b[...]) + bd_ref[0]
        y_ref[...] = _pack_halves(y.astype(BF16).astype(F32))

    @pl.when(i >= nb_ref[0])
    def _():
        y_ref[...] = jnp.zeros(y_ref.shape, I32)


def _experts(block_expert, n_used, xs, wg, bg, wu, bu, wd, bd):
    n_rows, packed = xs.shape
    d, dff = wg.shape[1], wg.shape[2]
    assert packed * 2 == d and wd.shape[2] == d
    nb = n_rows // MOE_BLOCK
    blk = lambda i, be, nu: (jnp.minimum(i, nu[0] - 1), 0)
    wsel = lambda i, be, nu: (be[jnp.minimum(i, nu[0] - 1)], 0, 0)
    grid_spec = pltpu.PrefetchScalarGridSpec(
        num_scalar_prefetch=2,
        grid=(nb,),
        in_specs=[
            pl.BlockSpec((MOE_BLOCK, packed), blk),
            pl.BlockSpec((1, d, dff), wsel), pl.BlockSpec((1, 1, dff), wsel),
            pl.BlockSpec((1, d, dff), wsel), pl.BlockSpec((1, 1, dff), wsel),
            pl.BlockSpec((1, dff, d), wsel), pl.BlockSpec((1, 1, d), wsel),
        ],
        out_specs=pl.BlockSpec((MOE_BLOCK, packed), lambda i, be, nu: (i, 0)),
        scratch_shapes=[pltpu.VMEM((d, dff), BF16), pltpu.VMEM((d, dff), BF16), pltpu.VMEM((dff, d), BF16)],
    )
    return pl.pallas_call(
        _expert_kernel,
        grid_spec=grid_spec,
        out_shape=jax.ShapeDtypeStruct((n_rows, packed), I32),
        compiler_params=pltpu.CompilerParams(dimension_semantics=("arbitrary",), vmem_limit_bytes=VMEM_LIMIT),
        name="moe_experts",
    )(block_expert, n_used, xs, wg, bg, wu, bu, wd, bd)


def _combine_kernel(ngroups_ref, dstg_ref, dstg_next_ref, pos_ref, gate_ref, h1_ref, ys_hbm, g2_ref, b2_ref,
                    o_ref, buf, sems):
    step = pl.program_id(0)
    n_steps = pl.num_programs(0)
    slot = step % 2

    def fetch(dref, s, j):
        return _group_copy(ys_hbm, dref[0, 0, j], buf.at[s], j, sems.at[s])

    def issue(dref, s, count):
        def body(j, c):
            fetch(dref, s, j).start()
            return c
        lax.fori_loop(0, count, body, 0)

    @pl.when(step == 0)
    def _():
        buf[...] = jnp.zeros(buf.shape, I32)
        issue(dstg_ref, 0, ngroups_ref[0])

    @pl.when(step + 1 < n_steps)
    def _():
        issue(dstg_next_ref, 1 - slot, ngroups_ref[jnp.minimum(step + 1, n_steps - 1)])

    def drain(j, c):
        fetch(dstg_ref, slot, 0).wait()
        return c

    lax.fori_loop(0, ngroups_ref[step], drain, 0)

    pos = pos_ref[...]
    gate = gate_ref[...]
    lanes = lax.broadcasted_iota(I32, (pos.shape[0], SORT_ROWS), 1)
    sel = jnp.where(lanes == pos[:, 0:1], gate[:, 0:1], 0.0)
    for k in range(1, TOP_K):
        sel = sel + jnp.where(lanes == pos[:, k:k + 1], gate[:, k:k + 1], 0.0)
    sel_hi = sel.astype(BF16)
    sel_lo = (sel - sel_hi.astype(F32)).astype(BF16)
    y_lo, y_hi = _unpack_halves(buf[slot])
    ffn = jnp.concatenate([_dot(sel_hi, y_lo) + _dot(sel_lo, y_lo), _dot(sel_hi, y_hi) + _dot(sel_lo, y_hi)], axis=1)
    o_ref[...] = _layer_norm(DEEPNORM_ALPHA * h1_ref[...] + ffn, g2_ref[...], b2_ref[...])


def _combine(ngroups, dstg, pos_c, gates, h1, ys, g2, b2):
    t, d = h1.shape
    tm = ROW_TILE
    n_steps = t // tm
    row = lambda i, ng: (i, 0)
    const = lambda i, ng: (0, 0)
    grid_spec = pltpu.PrefetchScalarGridSpec(
        num_scalar_prefetch=1,
        grid=(n_steps,),
        in_specs=[
            pl.BlockSpec((1, 1, SORT_GROUPS), lambda i, ng: (i, 0, 0), memory_space=pltpu.SMEM),
            pl.BlockSpec((1, 1, SORT_GROUPS), lambda i, ng: (jnp.minimum(i + 1, n_steps - 1), 0, 0),
                         memory_space=pltpu.SMEM),
            pl.BlockSpec((tm, TOP_K), row), pl.BlockSpec((tm, TOP_K), row), pl.BlockSpec((tm, d), row),
            pl.BlockSpec(memory_space=pl.ANY),
            pl.BlockSpec((1, d), const), pl.BlockSpec((1, d), const),
        ],
        out_specs=pl.BlockSpec((tm, d), row),
        scratch_shapes=[pltpu.VMEM((2, SORT_ROWS, d // 2), I32), pltpu.SemaphoreType.DMA((2,))],
    )
    return pl.pallas_call(
        _combine_kernel,
        grid_spec=grid_spec,
        out_shape=jax.ShapeDtypeStruct((t, d), F32),
        compiler_params=pltpu.CompilerParams(dimension_semantics=("arbitrary",), vmem_limit_bytes=VMEM_LIMIT),
        name="moe_combine_ln2",
    )(ngroups, dstg, dstg, pos_c, gates, h1, ys, g2, b2)


def kernel(x, meta_tokens, ln_in_g, ln_in_b, rel_bias, w_in, lambda_q1, lambda_k1, lambda_q2, lambda_k2,
           subln_g, a_re, a_im, log_step, b_re, b_im, c_re, c_im, d_skip, w_glu, b_glu, w_out, ln1_g, ln1_b,
           w_router, b_router, w_gate, b_gate, w_up, b_up, w_down, b_down, ln2_g, ln2_b):
    bsz, seq, dm = x.shape
    assert seq % 512 == 0 and w_in.shape[0] == DEPTH == 1
    layer = 0
    row2 = lambda v: v.astype(F32).reshape(1, -1)

    w_in_b = w_in[layer].astype(BF16)
    gi, bi = row2(ln_in_g), row2(ln_in_b)
    q, k, vt, u = _inproj(x, gi, bi, w_in_b, tm=512, kv_tile=KV_TILE)
    meta = jnp.zeros((1, META_PAD, dm), x.dtype).at[0, :N_META].set(meta_tokens.astype(x.dtype))
    _, k_meta, vt_meta, u_meta = _inproj(meta, gi, bi, w_in_b, tm=META_PAD, kv_tile=META_PAD)

    lam_init = 0.8 - 0.6 * math.exp(-0.3 * layer)
    lam = (jnp.exp(jnp.sum(lambda_q1[layer].astype(F32) * lambda_k1[layer].astype(F32)))
           - jnp.exp(jnp.sum(lambda_q2[layer].astype(F32) * lambda_k2[layer].astype(F32))) + lam_init)
    att = _attention(lam.reshape(1), q, k, vt, k_meta[0], vt_meta[0, :, 0], _near_bias(rel_bias),
                     subln_g[layer].astype(F32).reshape(HEAD_W, 1), lam_init=lam_init)

    kpad, pt, qt, a1, a2 = _ssm_tables(a_re[layer], a_im[layer], log_step[layer], b_re[layer], b_im[layer],
                                     c_re[layer], c_im[layer], d_skip[layer])
    n_chunks = -(-(seq // SSM_T + 1) // 8) * 8
    lead = jnp.zeros((bsz, SSM_T, SSM_W), BF16).at[:, SSM_T - N_META:].set(u_meta[:, :N_META])
    tail = jnp.zeros((bsz, n_chunks * SSM_T - SSM_T - seq, SSM_W), BF16)
    ug = jnp.concatenate([lead, u, tail], axis=1).reshape(bsz, n_chunks, SSM_T, SSM_G, SSM_CG)
    ug = jnp.transpose(ug, (3, 1, 0, 2, 4)).reshape(SSM_G, n_chunks * bsz, SSM_T * SSM_CG)
    yg = _ssm(ug, kpad, pt, qt, a1, a2, n_chunks=n_chunks, bsz=bsz)
    yg = jnp.transpose(yg.reshape(SSM_G, n_chunks, bsz, SSM_T, SSM_CG), (2, 1, 3, 0, 4))
    y_ssm = yg.reshape(bsz, n_chunks * SSM_T, SSM_W)[:, SSM_T:SSM_T + seq]

    t = bsz * seq
    h1, idx_t, gate_t, rank_t, cnt = _mix(
        x.reshape(t, dm), gi, bi, att.reshape(t, ATT_W), y_ssm.reshape(t, SSM_W),
        w_glu[layer].astype(BF16), row2(b_glu[layer]), w_out[layer].astype(BF16),
        row2(ln1_g[layer]), row2(ln1_b[layer]),
        jnp.transpose(w_router[layer].astype(F32)), b_router[layer].astype(F32).reshape(N_EXPERTS, 1))

    n_tiles = t // ROW_TILE
    experts = jnp.arange(N_EXPERTS, dtype=I32)
    tiles = jnp.arange(n_tiles, dtype=I32)
    run = (cnt[:, :, 0].astype(I32) + GROUP - 1) // GROUP * GROUP
    run_off = jnp.sum(jnp.where((experts[:, None] < experts[None, :])[None], run[:, :, None], 0), axis=1)
    run_before = jnp.sum(jnp.where((tiles[:, None] < tiles[None, :])[:, :, None], run[:, None, :], 0), axis=0)
    counts = jnp.sum(run, axis=0)
    padded = (counts + MOE_BLOCK - 1) // MOE_BLOCK * MOE_BLOCK
    padded_end = jnp.sum(jnp.where(experts[:, None] <= experts[None, :], padded[:, None], 0), axis=0)
    padded_start = padded_end - padded
    n_blocks = (t * TOP_K + n_tiles * N_EXPERTS * (GROUP - 1)) // MOE_BLOCK + N_EXPERTS
    block_row0 = jnp.arange(n_blocks, dtype=I32) * MOE_BLOCK
    block_expert = jnp.minimum(jnp.sum((padded_end[None, :] <= block_row0[:, None]).astype(I32), axis=1),
                               N_EXPERTS - 1)
    n_used = (padded_end[-1:] // MOE_BLOCK).astype(I32)
    ngroups = jnp.sum(run, axis=1) // GROUP
    local_row = jnp.arange(SORT_GROUPS, dtype=I32) * GROUP
    owner = jnp.minimum(jnp.sum(((run_off + run)[:, None, :] <= local_row[None, :, None]).astype(I32), axis=-1),
                        N_EXPERTS - 1)
    shift = (padded_start[None, :] + run_before - run_off) // GROUP
    dstg = jnp.sum(jnp.where(owner[..., None] == experts, shift[:, None, :], 0), axis=-1) + local_row // GROUP
    dstg = jnp.where(local_row[None, :] // GROUP < ngroups[:, None], dstg, 0).reshape(n_tiles, 1, SORT_GROUPS)
    run_off_tok = jnp.repeat(run_off, ROW_TILE, axis=0)
    pos_t = jnp.sum(jnp.where(idx_t[..., None] == experts, run_off_tok[None], 0), axis=-1) + rank_t

    xs = _dispatch(ngroups, ((padded_start + counts) // GROUP).astype(I32),
                   ((padded - counts) // GROUP).astype(I32), n_used, dstg, pos_t, h1, n_blocks * MOE_BLOCK)
    b3 = lambda v: v.astype(F32)[:, None, :]
    ys = _experts(block_expert, n_used, xs, w_gate[layer], b3(b_gate[layer]), w_up[layer], b3(b_up[layer]),
                  w_down[layer], b3(b_down[layer]))
    out = _combine(ngroups, dstg, jnp.transpose(pos_t), jnp.transpose(gate_t), h1, ys,
                   row2(ln2_g[layer]), row2(ln2_b[layer]))
    return out.reshape(bsz, seq, dm)
```

```python
import functools
import math

import jax
import jax.numpy as jnp
import numpy as np
from jax import lax
from jax.experimental import pallas as pl
from jax.experimental.pallas import tpu as pltpu

F32 = jnp.float32
BF16 = jnp.bfloat16
I32 = jnp.int32

DEPTH = 1
N_META = 16
CHUNK = 64
ATT_HEADS = 4
HEAD_DIM = 64
HEAD_W = 2 * HEAD_DIM
ATT_W = ATT_HEADS * HEAD_W
SSM_W = 512
SSM_CG = 16
SSM_G = SSM_W // SSM_CG
SSM_N = 64
N_BUCKETS = 32
MAX_DISTANCE = 128
N_EXPERTS = 32
TOP_K = 4
SWIGLU_LIMIT = 7.0
SWIGLU_ALPHA = 1.702
LN_EPS = 1e-5
NEG_INF = -1e30
DEEPNORM_ALPHA = (2.0 * DEPTH) ** 0.25
LOG2E = 1.4426950408889634

Q_TILE = 512
KV_TILE = 512
V_ONES = 16
V_ROWS = HEAD_W + V_ONES
META_PAD = 128
SSM_T = 64
MIX_TILE = 512
ROW_TILE = 256
MOE_BLOCK = 512
GROUP = 8
SORT_ROWS = 1280
SORT_GROUPS = SORT_ROWS // GROUP
assert SORT_ROWS >= ROW_TILE * TOP_K + N_EXPERTS * (GROUP - 1)
VMEM_LIMIT = 56 * 1024 * 1024


def _layer_norm(x, g, b):
    mu = jnp.mean(x, axis=-1, keepdims=True)
    xc = x - mu
    var = jnp.mean(xc * xc, axis=-1, keepdims=True)
    return xc * lax.rsqrt(var + LN_EPS) * g + b


def _dot(a, b):
    return jnp.dot(a, b, preferred_element_type=F32)


def _dot_nt(a, b, precision=None):
    return lax.dot_general(a, b, (((1,), (1,)), ((), ())), precision=precision,
                           preferred_element_type=F32)


def _inproj_kernel(x_ref, g_ref, b_ref, w_ref, q_ref, k_ref, vt_ref, u_ref, *, kv_tile, q_scale):
    h = _layer_norm(x_ref[0], g_ref[...], b_ref[...]).astype(BF16)
    tm = h.shape[0]
    q_ref[0] = (_dot(h, w_ref[:, 0:ATT_W]) * q_scale).astype(BF16)
    k_ref[0] = _dot(h, w_ref[:, ATT_W:2 * ATT_W]).astype(BF16)
    v = _dot(h, w_ref[:, 2 * ATT_W:3 * ATT_W])
    ones = jnp.ones((V_ONES, kv_tile), BF16)
    for hh in range(ATT_HEADS):
        vt = v[:, hh * HEAD_W:(hh + 1) * HEAD_W].T.astype(BF16)
        for j in range(tm // kv_tile):
            vt_ref[0, hh, j, 0:HEAD_W, :] = vt[:, j * kv_tile:(j + 1) * kv_tile]
            vt_ref[0, hh, j, HEAD_W:V_ROWS, :] = ones
    u_ref[0] = _dot(h, w_ref[:, 3 * ATT_W:]).astype(BF16)


def _inproj(x, g, b, w, *, tm, kv_tile):
    bsz, s, d = x.shape
    q_scale = HEAD_DIM ** -0.5 * LOG2E
    n_cols = w.shape[1]
    row = lambda bi, i: (bi, i, 0)
    return pl.pallas_call(
        functools.partial(_inproj_kernel, kv_tile=kv_tile, q_scale=q_scale),
        grid=(bsz, s // tm),
        in_specs=[
            pl.BlockSpec((1, tm, d), row),
            pl.BlockSpec((1, d), lambda bi, i: (0, 0)),
            pl.BlockSpec((1, d), lambda bi, i: (0, 0)),
            pl.BlockSpec((d, n_cols), lambda bi, i: (0, 0)),
        ],
        out_specs=[
            pl.BlockSpec((1, tm, ATT_W), row),
            pl.BlockSpec((1, tm, ATT_W), row),
            pl.BlockSpec((1, ATT_HEADS, tm // kv_tile, V_ROWS, kv_tile), lambda bi, i: (bi, 0, i, 0, 0)),
            pl.BlockSpec((1, tm, SSM_W), row),
        ],
        out_shape=[
            jax.ShapeDtypeStruct((bsz, s, ATT_W), BF16),
            jax.ShapeDtypeStruct((bsz, s, ATT_W), BF16),
            jax.ShapeDtypeStruct((bsz, ATT_HEADS, s // kv_tile, V_ROWS, kv_tile), BF16),
            jax.ShapeDtypeStruct((bsz, s, SSM_W), BF16),
        ],
        compiler_params=pltpu.CompilerParams(
            dimension_semantics=("parallel", "parallel"), vmem_limit_bytes=VMEM_LIMIT),
        name="inproj",
    )(x, g, b, w)


def _attn_kernel(lam_ref, q_ref, k_ref, vt_ref, km_ref, vtm_ref, bias_ref, g_ref, o_ref,
                 m_ref, acc_ref, qz_ref, sa_ref, sb_ref, sn_ref, *, out_scale):
    i = pl.program_id(2)
    qt = q_ref[0].astype(F32).T
    feat = lax.broadcasted_iota(I32, qt.shape, 0)
    qz_ref[:, 0:Q_TILE] = jnp.where(feat < HEAD_DIM, qt, 0.0).astype(BF16)
    qz_ref[:, Q_TILE:] = jnp.where(feat >= HEAD_DIM, qt, 0.0).astype(BF16)

    m_ref[...] = jnp.full(m_ref.shape, NEG_INF, F32)
    acc_ref[...] = jnp.zeros(acc_ref.shape, F32)

    def scores(kt):
        return _dot(kt, qz_ref[...])

    def k_tile(t):
        return k_ref[0, pl.ds(pl.multiple_of(t * KV_TILE, KV_TILE), KV_TILE), :]

    def absorb(s, pv):
        m_old = m_ref[...]
        m_new = jnp.maximum(m_old, jnp.max(s, axis=0, keepdims=True))
        alpha = jnp.exp2(m_old - m_new)
        p = jnp.exp2(s - m_new).astype(BF16)
        acc_ref[...] = acc_ref[...] * alpha + pv(p)
        m_ref[...] = m_new

    def tile_pv(t):
        return lambda p: _dot(vt_ref[0, 0, t], p)

    n_far = jnp.maximum(i - 1, 0)
    peel = n_far % 2

    sn_ref[...] = scores(km_ref[...])
    sa_ref[...] = scores(k_tile(0))
    absorb(sn_ref[...] + bias_ref[0, 0, 2 * KV_TILE:, :], lambda p: _dot(vtm_ref[0], p))

    @pl.when(peel == 1)
    def _():
        absorb(sa_ref[...], tile_pv(0))
        sa_ref[...] = scores(k_tile(1))

    def far_pair(j, carry):
        t0 = peel + 2 * j
        s_cur = sa_ref[...]
        sb_ref[...] = scores(k_tile(t0 + 1))
        absorb(s_cur, tile_pv(t0))
        s_cur = sb_ref[...]
        sa_ref[...] = scores(k_tile(t0 + 2))
        absorb(s_cur, tile_pv(t0 + 1))
        return carry

    lax.fori_loop(0, n_far // 2, far_pair, 0)

    sb_ref[...] = scores(k_tile(i))
    absorb(sa_ref[...] + bias_ref[0, 0, 0:KV_TILE, :], tile_pv(n_far))
    absorb(sb_ref[...] + bias_ref[0, 0, KV_TILE:2 * KV_TILE, :], tile_pv(i))

    acc = acc_ref[...]
    lam = lam_ref[0]
    o1 = acc[0:HEAD_W, 0:Q_TILE] / acc[HEAD_W:HEAD_W + 1, 0:Q_TILE]
    o2 = acc[0:HEAD_W, Q_TILE:] / acc[HEAD_W:HEAD_W + 1, Q_TILE:]
    o = o1 - lam * o2
    ms = jnp.mean(o * o, axis=0, keepdims=True)
    o = o * lax.rsqrt(ms + LN_EPS) * g_ref[...] * out_scale
    o_ref[0] = o.T.astype(o_ref.dtype)


def _attention(lam, q, k, vt, k_meta, vt_meta, bias, subln_g, *, lam_init):
    bsz, s, _ = q.shape
    nq = s // Q_TILE
    n_near = META_PAD + 2 * KV_TILE
    grid_spec = pltpu.PrefetchScalarGridSpec(
        num_scalar_prefetch=1,
        grid=(bsz, ATT_HEADS, nq),
        in_specs=[
            pl.BlockSpec((1, Q_TILE, HEAD_W), lambda b, h, i, lam: (b, i, h)),
            pl.BlockSpec((1, s, HEAD_W), lambda b, h, i, lam: (b, 0, h)),
            pl.BlockSpec((1, 1, s // KV_TILE, V_ROWS, KV_TILE), lambda b, h, i, lam: (b, h, 0, 0, 0)),
            pl.BlockSpec((META_PAD, HEAD_W), lambda b, h, i, lam: (0, h)),
            pl.BlockSpec((1, V_ROWS, META_PAD), lambda b, h, i, lam: (h, 0, 0)),
            pl.BlockSpec((1, 1, n_near, 2 * Q_TILE), lambda b, h, i, lam: (h, jnp.minimum(i, 1), 0, 0)),
            pl.BlockSpec((HEAD_W, 1), lambda b, h, i, lam: (0, 0)),
        ],
        out_specs=pl.BlockSpec((1, Q_TILE, HEAD_W), lambda b, h, i, lam: (b, i, h)),
        scratch_shapes=[
            pltpu.VMEM((1, 2 * Q_TILE), F32), pltpu.VMEM((V_ROWS, 2 * Q_TILE), F32),
            pltpu.VMEM((HEAD_W, 2 * Q_TILE), BF16),
            pltpu.VMEM((KV_TILE, 2 * Q_TILE), F32), pltpu.VMEM((KV_TILE, 2 * Q_TILE), F32),
            pltpu.VMEM((META_PAD, 2 * Q_TILE), F32),
        ],
    )
    return pl.pallas_call(
        functools.partial(_attn_kernel, out_scale=1.0 - lam_init),
        grid_spec=grid_spec,
        out_shape=jax.ShapeDtypeStruct((bsz, s, ATT_W), BF16),
        compiler_params=pltpu.CompilerParams(
            dimension_semantics=("parallel", "parallel", "arbitrary"), vmem_limit_bytes=VMEM_LIMIT),
        name="diff_attention",
    )(lam, q, k, vt, k_meta, vt_meta, bias, subln_g)


def _t5_bucket(rel):
    nb = N_BUCKETS // 2
    max_exact = nb // 2
    ret = jnp.where(rel > 0, nb, 0)
    n = jnp.abs(rel)
    n_f = jnp.maximum(n, 1).astype(F32)
    large = max_exact + (jnp.log(n_f / max_exact) / math.log(MAX_DISTANCE / max_exact)
                         * (nb - max_exact)).astype(I32)
    large = jnp.minimum(large, nb - 1)
    return ret + jnp.where(n < max_exact, n, large)


def _near_bias(rel_bias):
    table = rel_bias.astype(F32)
    far = table[N_BUCKETS // 2 - 1]
    c = jnp.arange(Q_TILE, dtype=I32)[None, :]
    r = jnp.arange(KV_TILE, dtype=I32)[:, None]

    shifted = jnp.transpose(table - far) * LOG2E

    def bias_of(offset, rows, cols=Q_TILE):
        rel = jnp.arange(rows, dtype=I32)[:, None] - c[:, :cols] + offset
        hot = _t5_bucket(rel)[None, :, :, None] == jnp.arange(N_BUCKETS, dtype=I32)
        return jnp.sum(jnp.where(hot, shifted[:, None, None, :], 0.0), axis=-1)

    own = jnp.where((r // CHUNK <= c // CHUNK)[None], bias_of(0, KV_TILE), NEG_INF)
    near = MAX_DISTANCE
    prev = jnp.zeros((ATT_HEADS, KV_TILE, Q_TILE), F32).at[:, KV_TILE - near:, :near].set(bias_of(-near, near, near))
    rm = jnp.arange(META_PAD, dtype=I32)[:, None]
    meta_ok = (rm < N_META)[None]
    meta0 = jnp.where(meta_ok, jnp.pad(bias_of(-N_META, N_META), ((0, 0), (0, META_PAD - N_META), (0, 0))), NEG_INF)
    meta1 = jnp.where(meta_ok, jnp.zeros((ATT_HEADS, META_PAD, Q_TILE), F32), NEG_INF)
    assert N_META + Q_TILE - (N_META - 1) >= MAX_DISTANCE and KV_TILE >= near and Q_TILE >= near
    v0 = jnp.concatenate([jnp.full_like(prev, NEG_INF), own, meta0], axis=1)
    v1 = jnp.concatenate([prev, own, meta1], axis=1)
    both = jnp.stack([v0, v1], axis=1)
    return jnp.concatenate([both, both], axis=-1)


def _ssm_tables(a_re, a_im, log_step, b_re, b_im, c_re, c_im, d_skip):
    hi = lax.Precision.HIGHEST
    t_len = SSM_T
    step = jnp.exp(log_step.astype(F32))[:, None]
    ar = jnp.minimum(a_re.astype(F32), -1e-4)
    ai = a_im.astype(F32)
    mag = jnp.exp(step * ar)
    ph = step * ai
    abar_re = mag * jnp.cos(ph)
    abar_im = mag * jnp.sin(ph)
    den = ar * ar + ai * ai
    e_re = abar_re - 1.0
    e_im = abar_im
    f_re = (e_re * ar + e_im * ai) / den
    f_im = (e_im * ar - e_re * ai) / den
    br = b_re.astype(F32)
    bi = b_im.astype(F32)
    bb_re = f_re[..., None] * br - f_im[..., None] * bi
    bb_im = f_re[..., None] * bi + f_im[..., None] * br
    tau = jnp.arange(t_len + 1, dtype=F32)[None, :, None]
    pmag = jnp.exp(tau * (step * ar)[:, None, :])
    pph = tau * ph[:, None, :]
    pw_re = pmag * jnp.cos(pph)
    pw_im = pmag * jnp.sin(pph)
    bbt_re = jnp.swapaxes(bb_re, 1, 2)[:, None]
    bbt_im = jnp.swapaxes(bb_im, 1, 2)[:, None]

    def times_bbar(p_re, p_im):
        return (p_re[:, :, None, :] * bbt_re - p_im[:, :, None, :] * bbt_im,
                p_re[:, :, None, :] * bbt_im + p_im[:, :, None, :] * bbt_re)

    w_re, w_im = times_bbar(pw_re[:, :t_len], pw_im[:, :t_len])
    cr = c_re.astype(F32)
    ci = c_im.astype(F32)
    kern = (jnp.einsum('gcn,gtkn->gktc', cr, w_re, precision=hi)
            - jnp.einsum('gcn,gtkn->gktc', ci, w_im, precision=hi))
    skip = d_skip.astype(F32).reshape(SSM_G, SSM_CG)
    kern = kern.at[:, :, 0, :].add(skip[:, None, :] * jnp.eye(SSM_CG, dtype=F32))
    kflat = kern.reshape(SSM_G, SSM_CG, t_len * SSM_CG)
    kpad = jnp.concatenate([jnp.zeros_like(kflat), kflat], axis=-1)
    back = jnp.arange(t_len - 1, -1, -1, dtype=F32)[None, :, None]
    bmag = jnp.exp(back * (step * ar)[:, None, :])
    bph = back * ph[:, None, :]
    pt_re, pt_im = times_bbar(bmag * jnp.cos(bph), bmag * jnp.sin(bph))
    pt = jnp.concatenate([pt_re, pt_im], axis=-1).reshape(SSM_G, t_len * SSM_CG, 2 * SSM_N)
    up_re = jnp.swapaxes(pw_re[:, 1:], 1, 2)[..., None]
    up_im = jnp.swapaxes(pw_im[:, 1:], 1, 2)[..., None]
    crt = jnp.swapaxes(cr, 1, 2)[:, :, None, :]
    cit = jnp.swapaxes(ci, 1, 2)[:, :, None, :]
    q_re = crt * up_re - cit * up_im
    q_im = crt * up_im + cit * up_re
    qt = jnp.concatenate([q_re, -q_im], axis=1).reshape(SSM_G, 2 * SSM_N, t_len * SSM_CG)
    at_re = pw_re[:, t_len]
    at_im = pw_im[:, t_len]
    a1 = jnp.concatenate([at_re, at_re], axis=-1)
    a2 = jnp.concatenate([-at_im, at_im], axis=-1)
    return kpad, pt.astype(BF16), qt.astype(BF16), a1, a2


def _ssm_state_kernel(u_ref, pt_ref, s_ref):
    s_ref[0] = _dot(u_ref[0], pt_ref[0])


def _ssm_scan_kernel(s_ref, a1_ref, a2_ref, x_ref, *, n_chunks, bsz):
    a1 = a1_ref[...][:, None, :]
    a2 = a2_ref[...][:, None, :]

    def body(c, x):
        rows = pl.ds(c * bsz, bsz)
        x_ref[:, rows, :] = x
        return a1 * x + a2 * pltpu.roll(x, SSM_N, 2) + s_ref[:, rows, :]

    lax.fori_loop(0, n_chunks, body, jnp.zeros((SSM_G, bsz, 2 * SSM_N), F32))


def _ssm_out_kernel(u_ref, x_ref, kp_ref, qt_ref, y_ref, mt_ref):
    kp = kp_ref[0]
    width = SSM_T * SSM_CG
    lane_tile = 128
    for sub in range(0, lane_tile, SSM_CG):
        shifted = kp if sub == 0 else pltpu.roll(kp, 2 * width - sub, 1)
        for s in range(SSM_T):
            off = (SSM_T - s) * SSM_CG
            if off % lane_tile == sub:
                base = off - sub
                mt_ref[s * SSM_CG:(s + 1) * SSM_CG, :] = shifted[:, base:base + width].astype(BF16)
    x = x_ref[0]
    x_hi = x.astype(BF16)
    x_lo = (x - x_hi.astype(F32)).astype(BF16)
    y = _dot(u_ref[0], mt_ref[...]) + _dot(x_hi, qt_ref[0]) + _dot(x_lo, qt_ref[0])
    y_ref[0] = y.astype(y_ref.dtype)


def _ssm(ug, kpad, pt, qt, a1, a2, *, n_chunks, bsz):
    g, r, w = ug.shape
    n2 = 2 * SSM_N
    per_g = lambda gi: (gi, 0, 0)
    params = pltpu.CompilerParams(dimension_semantics=("parallel",), vmem_limit_bytes=VMEM_LIMIT)
    s = pl.pallas_call(
        _ssm_state_kernel,
        grid=(g,),
        in_specs=[pl.BlockSpec((1, r, w), per_g), pl.BlockSpec((1, w, n2), per_g)],
        out_specs=pl.BlockSpec((1, r, n2), per_g),
        out_shape=jax.ShapeDtypeStruct((g, r, n2), F32),
        compiler_params=params,
        name="ssm_chunk_state",
    )(ug, pt)
    x = pl.pallas_call(
        functools.partial(_ssm_scan_kernel, n_chunks=n_chunks, bsz=bsz),
        out_shape=jax.ShapeDtypeStruct((g, r, n2), F32),
        compiler_params=pltpu.CompilerParams(vmem_limit_bytes=VMEM_LIMIT),
        name="ssm_chunk_scan",
    )(s, a1, a2)
    return pl.pallas_call(
        _ssm_out_kernel,
        grid=(g,),
        in_specs=[pl.BlockSpec((1, r, w), per_g), pl.BlockSpec((1, r, n2), per_g),
                  pl.BlockSpec((1, SSM_CG, 2 * w), per_g), pl.BlockSpec((1, n2, w), per_g)],
        out_specs=pl.BlockSpec((1, r, w), per_g),
        out_shape=jax.ShapeDtypeStruct((g, r, w), BF16),
        scratch_shapes=[pltpu.VMEM((w, w), BF16)],
        compiler_params=params,
        name="ssm_output",
    )(ug, x, kpad, qt)


def _mix_kernel(x_ref, gi_ref, bi_ref, att_ref, y_ref, wglu_ref, bglu_ref, wout_ref, g1_ref, b1_ref,
                wr_ref, br_ref, h1_ref, idx_ref, gate_ref, rank_ref, cnt_ref):
    h0 = _layer_norm(x_ref[...], gi_ref[...], bi_ref[...])
    y = y_ref[...].astype(F32)
    y = y * (0.5 * (1.0 + jnp.tanh(math.sqrt(2.0 / math.pi) * (y + 0.044715 * (y * y * y)))))
    y = y * jax.nn.sigmoid(_dot(y.astype(BF16), wglu_ref[...]) + bglu_ref[...])
    mix = _dot(att_ref[...], wout_ref[0:ATT_W, :]) + _dot(y.astype(BF16), wout_ref[ATT_W:, :])
    h1 = _layer_norm(DEEPNORM_ALPHA * h0 + mix, g1_ref[...], b1_ref[...])
    h1_ref[...] = h1

    logits = _dot_nt(wr_ref[...], h1, precision=lax.Precision.HIGHEST) + br_ref[...]
    tm = logits.shape[1]
    eidx = lax.broadcasted_iota(I32, logits.shape, 0)
    vals, hots = [], []
    rest = logits
    for _ in range(TOP_K):
        mx = jnp.max(rest, axis=0, keepdims=True)
        first = jnp.min(jnp.where(rest == mx, eidx, N_EXPERTS), axis=0, keepdims=True)
        hot = eidx == first
        vals.append(mx)
        hots.append(hot)
        rest = jnp.where(hot, -jnp.inf, rest)
    exps = [jnp.exp(v - vals[0]) for v in vals]
    denom = exps[0] + exps[1] + exps[2] + exps[3]
    gate_ref[...] = jnp.concatenate([e / denom for e in exps], axis=0)
    idx_ref[...] = jnp.concatenate(
        [jnp.sum(jnp.where(h, eidx, 0), axis=0, keepdims=True) for h in hots], axis=0)

    hot_all = (hots[0] | hots[1] | hots[2] | hots[3]).astype(F32)
    sub = ROW_TILE
    tri = (lax.broadcasted_iota(I32, (sub, sub), 0) < lax.broadcasted_iota(I32, (sub, sub), 1)).astype(BF16)
    for part in range(tm // sub):
        cols = slice(part * sub, (part + 1) * sub)
        before = _dot(hot_all[:, cols].astype(BF16), tri)
        rank_ref[:, cols] = jnp.concatenate(
            [jnp.sum(jnp.where(h[:, cols], before, 0.0), axis=0, keepdims=True) for h in hots], axis=0).astype(I32)
        cnt_ref[part] = jnp.broadcast_to(jnp.sum(hot_all[:, cols], axis=1, keepdims=True), cnt_ref.shape[1:])


def _mix(x2, gi, bi, att, y, wglu, bglu, wout, g1, b1, wr_t, br):
    t, d = x2.shape
    tm = MIX_TILE
    row = lambda i: (i, 0)
    col = lambda i: (0, i)
    const = lambda i: (0, 0)
    return pl.pallas_call(
        _mix_kernel,
        grid=(t // tm,),
        in_specs=[
            pl.BlockSpec((tm, d), row), pl.BlockSpec((1, d), const), pl.BlockSpec((1, d), const),
            pl.BlockSpec((tm, ATT_W), row), pl.BlockSpec((tm, SSM_W), row),
            pl.BlockSpec((SSM_W, SSM_W), const), pl.BlockSpec((1, SSM_W), const),
            pl.BlockSpec((d, d), const), pl.BlockSpec((1, d), const), pl.BlockSpec((1, d), const),
            pl.BlockSpec((N_EXPERTS, d), const), pl.BlockSpec((N_EXPERTS, 1), const),
        ],
        out_specs=[
            pl.BlockSpec((tm, d), row),
            pl.BlockSpec((TOP_K, tm), col), pl.BlockSpec((TOP_K, tm), col), pl.BlockSpec((TOP_K, tm), col),
            pl.BlockSpec((tm // ROW_TILE, N_EXPERTS, 128), lambda i: (i, 0, 0)),
        ],
        out_shape=[
            jax.ShapeDtypeStruct((t, d), F32),
            jax.ShapeDtypeStruct((TOP_K, t), I32), jax.ShapeDtypeStruct((TOP_K, t), F32),
            jax.ShapeDtypeStruct((TOP_K, t), I32),
            jax.ShapeDtypeStruct((t // ROW_TILE, N_EXPERTS, 128), F32),
        ],
        compiler_params=pltpu.CompilerParams(dimension_semantics=("parallel",), vmem_limit_bytes=VMEM_LIMIT),
        name="mix_ln1_router",
    )(x2, gi, bi, att, y, wglu, bglu, wout, g1, b1, wr_t, br)


_HI_BITS = -65536


def _pack_halves(x):
    half = x.shape[1] // 2
    lo = lax.bitcast_convert_type(x[:, :half], I32)
    hi = lax.bitcast_convert_type(x[:, half:], I32)
    return lax.shift_right_logical(lo, 16) | (hi & _HI_BITS)


def _unpack_halves(w):
    lo = lax.bitcast_convert_type(lax.shift_left(w, 16), F32)
    hi = lax.bitcast_convert_type(w & _HI_BITS, F32)
    return lo.astype(BF16), hi.astype(BF16)


def _group_copy(src, src_group, dst, dst_group, sem):
    return pltpu.make_async_copy(src.at[pl.ds(pl.multiple_of(src_group * GROUP, GROUP), GROUP), :],
                                 dst.at[pl.ds(pl.multiple_of(dst_group * GROUP, GROUP), GROUP), :], sem)


def _dispatch_kernel(ngroups_ref, pad_start_ref, pad_count_ref, n_used_ref, dstg_ref, pos_ref, h1_ref, xs_hbm,
                     sorted_ref, zero_ref, sems, zsem):
    step = pl.program_id(0)
    n_steps = pl.num_programs(0)
    slot = step % 2
    n_blocks = xs_hbm.shape[0] // MOE_BLOCK

    def zero_block(blk):
        return pltpu.make_async_copy(zero_ref, xs_hbm.at[pl.ds(blk * MOE_BLOCK, MOE_BLOCK), :], zsem)

    @pl.when(step == 0)
    def _():
        zero_ref[...] = jnp.zeros(zero_ref.shape, I32)
        for e in range(N_EXPERTS):
            start = pad_start_ref[e]
            count = pad_count_ref[e]

            def fill(j, c):
                _group_copy(zero_ref, 0, xs_hbm, start + j, zsem).start()
                return c

            def drain(j, c):
                _group_copy(zero_ref, 0, xs_hbm, start, zsem).wait()
                return c

            lax.fori_loop(0, count, fill, 0)
            lax.fori_loop(0, count, drain, 0)

        def fill_block(blk, c):
            zero_block(blk).start()
            return c

        def drain_block(blk, c):
            zero_block(blk).wait()
            return c

        lax.fori_loop(n_used_ref[0], n_blocks, fill_block, 0)
        lax.fori_loop(n_used_ref[0], n_blocks, drain_block, 0)

    pos = pos_ref[...]
    rows = lax.broadcasted_iota(I32, (SORT_ROWS, pos.shape[1]), 0)
    place = rows == pos[0:1]
    for k in range(1, TOP_K):
        place = place | (rows == pos[k:k + 1])
    sorted_ref[slot] = _pack_halves(_dot(place.astype(BF16), h1_ref[...].astype(BF16)))

    def issue(j, c):
        _group_copy(sorted_ref.at[slot], j, xs_hbm, dstg_ref[0, 0, j], sems.at[slot]).start()
        return c

    lax.fori_loop(0, ngroups_ref[step], issue, 0)

    def wait_tile(s, count):
        def drain(j, c):
            _group_copy(sorted_ref.at[s], 0, xs_hbm, 0, sems.at[s]).wait()
            return c
        lax.fori_loop(0, count, drain, 0)

    @pl.when(step > 0)
    def _():
        wait_tile(1 - slot, ngroups_ref[jnp.maximum(step - 1, 0)])

    @pl.when(step == n_steps - 1)
    def _():
        wait_tile(slot, ngroups_ref[step])


def _dispatch(ngroups, pad_start, pad_count, n_used, dstg, pos_t, h1, n_rows):
    t, d = h1.shape
    tm = ROW_TILE
    grid_spec = pltpu.PrefetchScalarGridSpec(
        num_scalar_prefetch=4,
        grid=(t // tm,),
        in_specs=[
            pl.BlockSpec((1, 1, SORT_GROUPS), lambda i, *_: (i, 0, 0), memory_space=pltpu.SMEM),
            pl.BlockSpec((TOP_K, tm), lambda i, *_: (0, i)),
            pl.BlockSpec((tm, d), lambda i, *_: (i, 0)),
        ],
        out_specs=pl.BlockSpec(memory_space=pl.ANY),
        scratch_shapes=[pltpu.VMEM((2, SORT_ROWS, d // 2), I32), pltpu.VMEM((MOE_BLOCK, d // 2), I32),
                        pltpu.SemaphoreType.DMA((2,)), pltpu.SemaphoreType.DMA],
    )
    return pl.pallas_call(
        _dispatch_kernel,
        grid_spec=grid_spec,
        out_shape=jax.ShapeDtypeStruct((n_rows, d // 2), I32),
        compiler_params=pltpu.CompilerParams(dimension_semantics=("arbitrary",), vmem_limit_bytes=VMEM_LIMIT,
                                             has_side_effects=True),
        name="moe_dispatch",
    )(ngroups, pad_start, pad_count, n_used, dstg, pos_t, h1)


def _expert_kernel(be_ref, nb_ref, x_ref, wg_ref, bg_ref, wu_ref, bu_ref, wd_ref, bd_ref, y_ref,
                   wg_b, wu_b, wd_b):
    i = pl.program_id(0)

    @pl.when(i < nb_ref[0])
    def _():
        @pl.when((i == 0) | (be_ref[i] != be_ref[jnp.maximum(i - 1, 0)]))
        def _():
            wg_b[...] = wg_ref[0].astype(BF16)
            wu_b[...] = wu_ref[0].astype(BF16)
            wd_b[...] = wd_ref[0].astype(BF16)

        x_lo, x_hi = _unpack_halves(x_ref[...])
        half = x_lo.shape[1]
        gate = _dot(x_lo, wg_b[0:half, :]) + _dot(x_hi, wg_b[half:, :]) + bg_ref[0]
        up = _dot(x_lo, wu_b[0:half, :]) + _dot(x_hi, wu_b[half:, :]) + bu_ref[0]
        gate = jnp.minimum(gate, SWIGLU_LIMIT)
        up = jnp.clip(up, -SWIGLU_LIMIT, SWIGLU_LIMIT)
        act = (up + 1.0) * gate * jax.nn.sigmoid(gate * SWIGLU_ALPHA)
        y = _dot(act.astype(BF16), wd_b[...]) + bd_ref[0]
        y_ref[...] = _pack_halves(y.astype(BF16).astype(F32))

    @pl.when(i >= nb_ref[0])
    def _():
        y_ref[...] = jnp.zeros(y_ref.shape, I32)


def _experts(block_expert, n_used, xs, wg, bg, wu, bu, wd, bd):
    n_rows, packed = xs.shape
    d, dff = wg.shape[1], wg.shape[2]
    assert packed * 2 == d and wd.shape[2] == d
    nb = n_rows // MOE_BLOCK
    blk = lambda i, be, nu: (jnp.minimum(i, nu[0] - 1), 0)
    wsel = lambda i, be, nu: (be[jnp.minimum(i, nu[0] - 1)], 0, 0)
    grid_spec = pltpu.PrefetchScalarGridSpec(
        num_scalar_prefetch=2,
        grid=(nb,),
        in_specs=[
            pl.BlockSpec((MOE_BLOCK, packed), blk),
            pl.BlockSpec((1, d, dff), wsel), pl.BlockSpec((1, 1, dff), wsel),
            pl.BlockSpec((1, d, dff), wsel), pl.BlockSpec((1, 1, dff), wsel),
            pl.BlockSpec((1, dff, d), wsel), pl.BlockSpec((1, 1, d), wsel),
        ],
        out_specs=pl.BlockSpec((MOE_BLOCK, packed), lambda i, be, nu: (i, 0)),
        scratch_shapes=[pltpu.VMEM((d, dff), BF16), pltpu.VMEM((d, dff), BF16), pltpu.VMEM((dff, d), BF16)],
    )
    return pl.pallas_call(
        _expert_kernel,
        grid_spec=grid_spec,
        out_shape=jax.ShapeDtypeStruct((n_rows, packed), I32),
        compiler_params=pltpu.CompilerParams(dimension_semantics=("arbitrary",), vmem_limit_bytes=VMEM_LIMIT),
        name="moe_experts",
    )(block_expert, n_used, xs, wg, bg, wu, bu, wd, bd)


def _combine_kernel(ngroups_ref, dstg_ref, dstg_next_ref, pos_ref, gate_ref, h1_ref, ys_hbm, g2_ref, b2_ref,
                    o_ref, buf, sems):
    step = pl.program_id(0)
    n_steps = pl.num_programs(0)
    slot = step % 2

    def fetch(dref, s, j):
        return _group_copy(ys_hbm, dref[0, 0, j], buf.at[s], j, sems.at[s])

    def issue(dref, s, count):
        def body(j, c):
            fetch(dref, s, j).start()
            return c
        lax.fori_loop(0, count, body, 0)

    @pl.when(step == 0)
    def _():
        buf[...] = jnp.zeros(buf.shape, I32)
        issue(dstg_ref, 0, ngroups_ref[0])

    @pl.when(step + 1 < n_steps)
    def _():
        issue(dstg_next_ref, 1 - slot, ngroups_ref[jnp.minimum(step + 1, n_steps - 1)])

    def drain(j, c):
        fetch(dstg_ref, slot, 0).wait()
        return c

    lax.fori_loop(0, ngroups_ref[step], drain, 0)

    pos = pos_ref[...]
    gate = gate_ref[...]
    lanes = lax.broadcasted_iota(I32, (pos.shape[0], SORT_ROWS), 1)
    sel = jnp.where(lanes == pos[:, 0:1], gate[:, 0:1], 0.0)
    for k in range(1, TOP_K):
        sel = sel + jnp.where(lanes == pos[:, k:k + 1], gate[:, k:k + 1], 0.0)
    sel_hi = sel.astype(BF16)
    sel_lo = (sel - sel_hi.astype(F32)).astype(BF16)
    y_lo, y_hi = _unpack_halves(buf[slot])
    ffn = jnp.concatenate([_dot(sel_hi, y_lo) + _dot(sel_lo, y_lo), _dot(sel_hi, y_hi) + _dot(sel_lo, y_hi)], axis=1)
    o_ref[...] = _layer_norm(DEEPNORM_ALPHA * h1_ref[...] + ffn, g2_ref[...], b2_ref[...])


def _combine(ngroups, dstg, pos_c, gates, h1, ys, g2, b2):
    t, d = h1.shape
    tm = ROW_TILE
    n_steps = t // tm
    row = lambda i, ng: (i, 0)
    const = lambda i, ng: (0, 0)
    grid_spec = pltpu.PrefetchScalarGridSpec(
        num_scalar_prefetch=1,
        grid=(n_steps,),
        in_specs=[
            pl.BlockSpec((1, 1, SORT_GROUPS), lambda i, ng: (i, 0, 0), memory_space=pltpu.SMEM),
            pl.BlockSpec((1, 1, SORT_GROUPS), lambda i, ng: (jnp.minimum(i + 1, n_steps - 1), 0, 0),
                         memory_space=pltpu.SMEM),
            pl.BlockSpec((tm, TOP_K), row), pl.BlockSpec((tm, TOP_K), row), pl.BlockSpec((tm, d), row),
            pl.BlockSpec(memory_space=pl.ANY),
            pl.BlockSpec((1, d), const), pl.BlockSpec((1, d), const),
        ],
        out_specs=pl.BlockSpec((tm, d), row),
        scratch_shapes=[pltpu.VMEM((2, SORT_ROWS, d // 2), I32), pltpu.SemaphoreType.DMA((2,))],
    )
    return pl.pallas_call(
        _combine_kernel,
        grid_spec=grid_spec,
        out_shape=jax.ShapeDtypeStruct((t, d), F32),
        compiler_params=pltpu.CompilerParams(dimension_semantics=("arbitrary",), vmem_limit_bytes=VMEM_LIMIT),
        name="moe_combine_ln2",
    )(ngroups, dstg, dstg, pos_c, gates, h1, ys, g2, b2)


def kernel(x, meta_tokens, ln_in_g, ln_in_b, rel_bias, w_in, lambda_q1, lambda_k1, lambda_q2, lambda_k2,
           subln_g, a_re, a_im, log_step, b_re, b_im, c_re, c_im, d_skip, w_glu, b_glu, w_out, ln1_g, ln1_b,
           w_router, b_router, w_gate, b_gate, w_up, b_up, w_down, b_down, ln2_g, ln2_b):
    bsz, seq, dm = x.shape
    assert seq % 512 == 0 and w_in.shape[0] == DEPTH == 1
    layer = 0
    row2 = lambda v: v.astype(F32).reshape(1, -1)

    w_in_b = w_in[layer].astype(BF16)
    gi, bi = row2(ln_in_g), row2(ln_in_b)
    q, k, vt, u = _inproj(x, gi, bi, w_in_b, tm=512, kv_tile=KV_TILE)
    meta = jnp.zeros((1, META_PAD, dm), x.dtype).at[0, :N_META].set(meta_tokens.astype(x.dtype))
    _, k_meta, vt_meta, u_meta = _inproj(meta, gi, bi, w_in_b, tm=META_PAD, kv_tile=META_PAD)

    lam_init = 0.8 - 0.6 * math.exp(-0.3 * layer)
    lam = (jnp.exp(jnp.sum(lambda_q1[layer].astype(F32) * lambda_k1[layer].astype(F32)))
           - jnp.exp(jnp.sum(lambda_q2[layer].astype(F32) * lambda_k2[layer].astype(F32))) + lam_init)
    att = _attention(lam.reshape(1), q, k, vt, k_meta[0], vt_meta[0, :, 0], _near_bias(rel_bias),
                     subln_g[layer].astype(F32).reshape(HEAD_W, 1), lam_init=lam_init)

    kpad, pt, qt, a1, a2 = _ssm_tables(a_re[layer], a_im[layer], log_step[layer], b_re[layer], b_im[layer],
                                     c_re[layer], c_im[layer], d_skip[layer])
    n_chunks = -(-(seq // SSM_T + 1) // 8) * 8
    lead = jnp.zeros((bsz, SSM_T, SSM_W), BF16).at[:, SSM_T - N_META:].set(u_meta[:, :N_META])
    tail = jnp.zeros((bsz, n_chunks * SSM_T - SSM_T - seq, SSM_W), BF16)
    ug = jnp.concatenate([lead, u, tail], axis=1).reshape(bsz, n_chunks, SSM_T, SSM_G, SSM_CG)
    ug = jnp.transpose(ug, (3, 1, 0, 2, 4)).reshape(SSM_G, n_chunks * bsz, SSM_T * SSM_CG)
    yg = _ssm(ug, kpad, pt, qt, a1, a2, n_chunks=n_chunks, bsz=bsz)
    yg = jnp.transpose(yg.reshape(SSM_G, n_chunks, bsz, SSM_T, SSM_CG), (2, 1, 3, 0, 4))
    y_ssm = yg.reshape(bsz, n_chunks * SSM_T, SSM_W)[:, SSM_T:SSM_T + seq]

    t = bsz * seq
    h1, idx_t, gate_t, rank_t, cnt = _mix(
        x.reshape(t, dm), gi, bi, att.reshape(t, ATT_W), y_ssm.reshape(t, SSM_W),
        w_glu[layer].astype(BF16), row2(b_glu[layer]), w_out[layer].astype(BF16),
        row2(ln1_g[layer]), row2(ln1_b[layer]),
        jnp.transpose(w_router[layer].astype(F32)), b_router[layer].astype(F32).reshape(N_EXPERTS, 1))

    n_tiles = t // ROW_TILE
    experts = jnp.arange(N_EXPERTS, dtype=I32)
    tiles = jnp.arange(n_tiles, dtype=I32)
    run = (cnt[:, :, 0].astype(I32) + GROUP - 1) // GROUP * GROUP
    run_off = jnp.sum(jnp.where((experts[:, None] < experts[None, :])[None], run[:, :, None], 0), axis=1)
    run_before = jnp.sum(jnp.where((tiles[:, None] < tiles[None, :])[:, :, None], run[:, None, :], 0), axis=0)
    counts = jnp.sum(run, axis=0)
    padded = (counts + MOE_BLOCK - 1) // MOE_BLOCK * MOE_BLOCK
    padded_end = jnp.sum(jnp.where(experts[:, None] <= experts[None, :], padded[:, None], 0), axis=0)
    padded_start = padded_end - padded
    n_blocks = (t * TOP_K + n_tiles * N_EXPERTS * (GROUP - 1)) // MOE_BLOCK + N_EXPERTS
    block_row0 = jnp.arange(n_blocks, dtype=I32) * MOE_BLOCK
    block_expert = jnp.minimum(jnp.sum((padded_end[None, :] <= block_row0[:, None]).astype(I32), axis=1),
                               N_EXPERTS - 1)
    n_used = (padded_end[-1:] // MOE_BLOCK).astype(I32)
    ngroups = jnp.sum(run, axis=1) // GROUP
    local_row = jnp.arange(SORT_GROUPS, dtype=I32) * GROUP
    owner = jnp.minimum(jnp.sum(((run_off + run)[:, None, :] <= local_row[None, :, None]).astype(I32), axis=-1),
                        N_EXPERTS - 1)
    shift = (padded_start[None, :] + run_before - run_off) // GROUP
    dstg = jnp.sum(jnp.where(owner[..., None] == experts, shift[:, None, :], 0), axis=-1) + local_row // GROUP
    dstg = jnp.where(local_row[None, :] // GROUP < ngroups[:, None], dstg, 0).reshape(n_tiles, 1, SORT_GROUPS)
    run_off_tok = jnp.repeat(run_off, ROW_TILE, axis=0)
    pos_t = jnp.sum(jnp.where(idx_t[..., None] == experts, run_off_tok[None], 0), axis=-1) + rank_t

    xs = _dispatch(ngroups, ((padded_start + counts) // GROUP).astype(I32),
                   ((padded - counts) // GROUP).astype(I32), n_used, dstg, pos_t, h1, n_blocks * MOE_BLOCK)
    b3 = lambda v: v.astype(F32)[:, None, :]
    ys = _experts(block_expert, n_used, xs, w_gate[layer], b3(b_gate[layer]), w_up[layer], b3(b_up[layer]),
                  w_down[layer], b3(b_down[layer]))
    out = _combine(ngroups, dstg, jnp.transpose(pos_t), jnp.transpose(gate_t), h1, ys,
                   row2(ln2_g[layer]), row2(ln2_b[layer]))
    return out.reshape(bsz, seq, dm)
```

```python
import functools
import math

import jax
import jax.numpy as jnp
import numpy as np
from jax import lax
from jax.experimental import pallas as pl
from jax.experimental.pallas import tpu as pltpu

F32 = jnp.float32
BF16 = jnp.bfloat16
I32 = jnp.int32

DEPTH = 1
N_META = 16
CHUNK = 64
ATT_HEADS = 4
HEAD_DIM = 64
HEAD_W = 2 * HEAD_DIM
ATT_W = ATT_HEADS * HEAD_W
SSM_W = 512
SSM_CG = 16
SSM_G = SSM_W // SSM_CG
SSM_N = 64
N_BUCKETS = 32
MAX_DISTANCE = 128
N_EXPERTS = 32
TOP_K = 4
SWIGLU_LIMIT = 7.0
SWIGLU_ALPHA = 1.702
LN_EPS = 1e-5
NEG_INF = -1e30
DEEPNORM_ALPHA = (2.0 * DEPTH) ** 0.25
LOG2E = 1.4426950408889634

Q_TILE = 512
KV_TILE = 512
V_ONES = 16
V_ROWS = HEAD_W + V_ONES
META_PAD = 128
SSM_T = 64
MIX_TILE = 512
ROW_TILE = 256
MOE_BLOCK = 512
GROUP = 8
SORT_ROWS = 1280
SORT_GROUPS = SORT_ROWS // GROUP
assert SORT_ROWS >= ROW_TILE * TOP_K + N_EXPERTS * (GROUP - 1)
VMEM_LIMIT = 56 * 1024 * 1024


def _layer_norm(x, g, b):
    mu = jnp.mean(x, axis=-1, keepdims=True)
    xc = x - mu
    var = jnp.mean(xc * xc, axis=-1, keepdims=True)
    return xc * lax.rsqrt(var + LN_EPS) * g + b


def _dot(a, b):
    return jnp.dot(a, b, preferred_element_type=F32)


def _dot_nt(a, b, precision=None):
    return lax.dot_general(a, b, (((1,), (1,)), ((), ())), precision=precision,
                           preferred_element_type=F32)


def _inproj_kernel(x_ref, g_ref, b_ref, w_ref, q_ref, k_ref, vt_ref, u_ref, *, kv_tile, q_scale):
    h = _layer_norm(x_ref[0], g_ref[...], b_ref[...]).astype(BF16)
    tm = h.shape[0]
    q_ref[0] = (_dot(h, w_ref[:, 0:ATT_W]) * q_scale).astype(BF16)
    k_ref[0] = _dot(h, w_ref[:, ATT_W:2 * ATT_W]).astype(BF16)
    v = _dot(h, w_ref[:, 2 * ATT_W:3 * ATT_W])
    ones = jnp.ones((V_ONES, kv_tile), BF16)
    for hh in range(ATT_HEADS):
        vt = v[:, hh * HEAD_W:(hh + 1) * HEAD_W].T.astype(BF16)
        for j in range(tm // kv_tile):
            vt_ref[0, hh, j, 0:HEAD_W, :] = vt[:, j * kv_tile:(j + 1) * kv_tile]
            vt_ref[0, hh, j, HEAD_W:V_ROWS, :] = ones
    u_ref[0] = _dot(h, w_ref[:, 3 * ATT_W:]).astype(BF16)


def _inproj(x, g, b, w, *, tm, kv_tile):
    bsz, s, d = x.shape
    q_scale = HEAD_DIM ** -0.5 * LOG2E
    n_cols = w.shape[1]
    row = lambda bi, i: (bi, i, 0)
    return pl.pallas_call(
        functools.partial(_inproj_kernel, kv_tile=kv_tile, q_scale=q_scale),
        grid=(bsz, s // tm),
        in_specs=[
            pl.BlockSpec((1, tm, d), row),
            pl.BlockSpec((1, d), lambda bi, i: (0, 0)),
            pl.BlockSpec((1, d), lambda bi, i: (0, 0)),
            pl.BlockSpec((d, n_cols), lambda bi, i: (0, 0)),
        ],
        out_specs=[
            pl.BlockSpec((1, tm, ATT_W), row),
            pl.BlockSpec((1, tm, ATT_W), row),
            pl.BlockSpec((1, ATT_HEADS, tm // kv_tile, V_ROWS, kv_tile), lambda bi, i: (bi, 0, i, 0, 0)),
            pl.BlockSpec((1, tm, SSM_W), row),
        ],
        out_shape=[
            jax.ShapeDtypeStruct((bsz, s, ATT_W), BF16),
            jax.ShapeDtypeStruct((bsz, s, ATT_W), BF16),
            jax.ShapeDtypeStruct((bsz, ATT_HEADS, s // kv_tile, V_ROWS, kv_tile), BF16),
            jax.ShapeDtypeStruct((bsz, s, SSM_W), BF16),
        ],
        compiler_params=pltpu.CompilerParams(
            dimension_semantics=("parallel", "parallel"), vmem_limit_bytes=VMEM_LIMIT),
        name="inproj",
    )(x, g, b, w)


def _attn_kernel(lam_ref, q_ref, k_ref, vt_ref, km_ref, vtm_ref, bias_ref, g_ref, o_ref,
                 m_ref, acc_ref, qz_ref, sa_ref, sb_ref, sn_ref, *, out_scale):
    i = pl.program_id(2)
    qt = q_ref[0].astype(F32).T
    feat = lax.broadcasted_iota(I32, qt.shape, 0)
    qz_ref[:, 0:Q_TILE] = jnp.where(feat < HEAD_DIM, qt, 0.0).astype(BF16)
    qz_ref[:, Q_TILE:] = jnp.where(feat >= HEAD_DIM, qt, 0.0).astype(BF16)

    m_ref[...] = jnp.full(m_ref.shape, NEG_INF, F32)
    acc_ref[...] = jnp.zeros(acc_ref.shape, F32)

    def scores(kt):
        return _dot(kt, qz_ref[...])

    def k_tile(t):
        return k_ref[0, pl.ds(pl.multiple_of(t * KV_TILE, KV_TILE), KV_TILE), :]

    def absorb(s, pv):
        m_old = m_ref[...]
        m_new = jnp.maximum(m_old, jnp.max(s, axis=0, keepdims=True))
        alpha = jnp.exp2(m_old - m_new)
        p = jnp.exp2(s - m_new).astype(BF16)
        acc_ref[...] = acc_ref[...] * alpha + pv(p)
        m_ref[...] = m_new

    def tile_pv(t):
        return lambda p: _dot(vt_ref[0, 0, t], p)

    n_far = jnp.maximum(i - 1, 0)
    peel = n_far % 2

    sn_ref[...] = scores(km_ref[...])
    sa_ref[...] = scores(k_tile(0))
    absorb(sn_ref[...] + bias_ref[0, 0, 2 * KV_TILE:, :], lambda p: _dot(vtm_ref[0], p))

    @pl.when(peel == 1)
    def _():
        absorb(sa_ref[...], tile_pv(0))
        sa_ref[...] = scores(k_tile(1))

    def far_pair(j, carry):
        t0 = peel + 2 * j
        s_cur = sa_ref[...]
        sb_ref[...] = scores(k_tile(t0 + 1))
        absorb(s_cur, tile_pv(t0))
        s_cur = sb_ref[...]
        sa_ref[...] = scores(k_tile(t0 + 2))
        absorb(s_cur, tile_pv(t0 + 1))
        return carry

    lax.fori_loop(0, n_far // 2, far_pair, 0)

    sb_ref[...] = scores(k_tile(i))
    absorb(sa_ref[...] + bias_ref[0, 0, 0:KV_TILE, :], tile_pv(n_far))
    absorb(sb_ref[...] + bias_ref[0, 0, KV_TILE:2 * KV_TILE, :], tile_pv(i))

    acc = acc_ref[...]
    lam = lam_ref[0]
    o1 = acc[0:HEAD_W, 0:Q_TILE] / acc[HEAD_W:HEAD_W + 1, 0:Q_TILE]
    o2 = acc[0:HEAD_W, Q_TILE:] / acc[HEAD_W:HEAD_W + 1, Q_TILE:]
    o = o1 - lam * o2
    ms = jnp.mean(o * o, axis=0, keepdims=True)
    o = o * lax.rsqrt(ms + LN_EPS) * g_ref[...] * out_scale
    o_ref[0] = o.T.astype(o_ref.dtype)


def _attention(lam, q, k, vt, k_meta, vt_meta, bias, subln_g, *, lam_init):
    bsz, s, _ = q.shape
    nq = s // Q_TILE
    n_near = META_PAD + 2 * KV_TILE
    grid_spec = pltpu.PrefetchScalarGridSpec(
        num_scalar_prefetch=1,
        grid=(bsz, ATT_HEADS, nq),
        in_specs=[
            pl.BlockSpec((1, Q_TILE, HEAD_W), lambda b, h, i, lam: (b, i, h)),
            pl.BlockSpec((1, s, HEAD_W), lambda b, h, i, lam: (b, 0, h)),
            pl.BlockSpec((1, 1, s // KV_TILE, V_ROWS, KV_TILE), lambda b, h, i, lam: (b, h, 0, 0, 0)),
            pl.BlockSpec((META_PAD, HEAD_W), lambda b, h, i, lam: (0, h)),
            pl.BlockSpec((1, V_ROWS, META_PAD), lambda b, h, i, lam: (h, 0, 0)),
            pl.BlockSpec((1, 1, n_near, 2 * Q_TILE), lambda b, h, i, lam: (h, jnp.minimum(i, 1), 0, 0)),
            pl.BlockSpec((HEAD_W, 1), lambda b, h, i, lam: (0, 0)),
        ],
        out_specs=pl.BlockSpec((1, Q_TILE, HEAD_W), lambda b, h, i, lam: (b, i, h)),
        scratch_shapes=[
            pltpu.VMEM((1, 2 * Q_TILE), F32), pltpu.VMEM((V_ROWS, 2 * Q_TILE), F32),
            pltpu.VMEM((HEAD_W, 2 * Q_TILE), BF16),
            pltpu.VMEM((KV_TILE, 2 * Q_TILE), F32), pltpu.VMEM((KV_TILE, 2 * Q_TILE), F32),
            pltpu.VMEM((META_PAD, 2 * Q_TILE), F32),
        ],
    )
    return pl.pallas_call(
        functools.partial(_attn_kernel, out_scale=1.0 - lam_init),
        grid_spec=grid_spec,
        out_shape=jax.ShapeDtypeStruct((bsz, s, ATT_W), BF16),
        compiler_params=pltpu.CompilerParams(
            dimension_semantics=("parallel", "parallel", "arbitrary"), vmem_limit_bytes=VMEM_LIMIT),
        name="diff_attention",
    )(lam, q, k, vt, k_meta, vt_meta, bias, subln_g)


def _t5_bucket(rel):
    nb = N_BUCKETS // 2
    max_exact = nb // 2
    ret = jnp.where(rel > 0, nb, 0)
    n = jnp.abs(rel)
    n_f = jnp.maximum(n, 1).astype(F32)
    large = max_exact + (jnp.log(n_f / max_exact) / math.log(MAX_DISTANCE / max_exact)
                         * (nb - max_exact)).astype(I32)
    large = jnp.minimum(large, nb - 1)
    return ret + jnp.where(n < max_exact, n, large)


def _near_bias(rel_bias):
    table = rel_bias.astype(F32)
    far = table[N_BUCKETS // 2 - 1]
    c = jnp.arange(Q_TILE, dtype=I32)[None, :]
    r = jnp.arange(KV_TILE, dtype=I32)[:, None]

    shifted = jnp.transpose(table - far) * LOG2E

    def bias_of(offset, rows, cols=Q_TILE):
        rel = jnp.arange(rows, dtype=I32)[:, None] - c[:, :cols] + offset
        hot = _t5_bucket(rel)[None, :, :, None] == jnp.arange(N_BUCKETS, dtype=I32)
        return jnp.sum(jnp.where(hot, shifted[:, None, None, :], 0.0), axis=-1)

    own = jnp.where((r // CHUNK <= c // CHUNK)[None], bias_of(0, KV_TILE), NEG_INF)
    near = MAX_DISTANCE
    prev = jnp.zeros((ATT_HEADS, KV_TILE, Q_TILE), F32).at[:, KV_TILE - near:, :near].set(bias_of(-near, near, near))
    rm = jnp.arange(META_PAD, dtype=I32)[:, None]
    meta_ok = (rm < N_META)[None]
    meta0 = jnp.where(meta_ok, jnp.pad(bias_of(-N_META, N_META), ((0, 0), (0, META_PAD - N_META), (0, 0))), NEG_INF)
    meta1 = jnp.where(meta_ok, jnp.zeros((ATT_HEADS, META_PAD, Q_TILE), F32), NEG_INF)
    assert N_META + Q_TILE - (N_META - 1) >= MAX_DISTANCE and KV_TILE >= near and Q_TILE >= near
    v0 = jnp.concatenate([jnp.full_like(prev, NEG_INF), own, meta0], axis=1)
    v1 = jnp.concatenate([prev, own, meta1], axis=1)
    both = jnp.stack([v0, v1], axis=1)
    return jnp.concatenate([both, both], axis=-1)


def _ssm_tables(a_re, a_im, log_step, b_re, b_im, c_re, c_im, d_skip):
    hi = lax.Precision.HIGHEST
    t_len = SSM_T
    step = jnp.exp(log_step.astype(F32))[:, None]
    ar = jnp.minimum(a_re.astype(F32), -1e-4)
    ai = a_im.astype(F32)
    mag = jnp.exp(step * ar)
    ph = step * ai
    abar_re = mag * jnp.cos(ph)
    abar_im = mag * jnp.sin(ph)
    den = ar * ar + ai * ai
    e_re = abar_re - 1.0
    e_im = abar_im
    f_re = (e_re * ar + e_im * ai) / den
    f_im = (e_im * ar - e_re * ai) / den
    br = b_re.astype(F32)
    bi = b_im.astype(F32)
    bb_re = f_re[..., None] * br - f_im[..., None] * bi
    bb_im = f_re[..., None] * bi + f_im[..., None] * br
    tau = jnp.arange(t_len + 1, dtype=F32)[None, :, None]
    pmag = jnp.exp(tau * (step * ar)[:, None, :])
    pph = tau * ph[:, None, :]
    pw_re = pmag * jnp.cos(pph)
    pw_im = pmag * jnp.sin(pph)
    bbt_re = jnp.swapaxes(bb_re, 1, 2)[:, None]
    bbt_im = jnp.swapaxes(bb_im, 1, 2)[:, None]

    def times_bbar(p_re, p_im):
        return (p_re[:, :, None, :] * bbt_re - p_im[:, :, None, :] * bbt_im,
                p_re[:, :, None, :] * bbt_im + p_im[:, :, None, :] * bbt_re)

    cr = c_re.astype(F32)
    ci = c_im.astype(F32)
    bi_re = jnp.swapaxes(bb_re, 1, 2)[:, :, None, :]
    bi_im = jnp.swapaxes(bb_im, 1, 2)[:, :, None, :]
    cb_re = cr[:, None] * bi_re - ci[:, None] * bi_im
    cb_im = cr[:, None] * bi_im + ci[:, None] * bi_re
    kern = jnp.einsum('gikm,gtm->gitk', jnp.concatenate([cb_re, -cb_im], axis=-1),
                      jnp.concatenate([pw_re[:, :t_len], pw_im[:, :t_len]], axis=-1), precision=hi)
    skip = d_skip.astype(F32).reshape(SSM_G, SSM_CG)
    kern = kern.at[:, :, 0, :].add(skip[:, None, :] * jnp.eye(SSM_CG, dtype=F32))
    kflat = kern.reshape(SSM_G, SSM_CG, t_len * SSM_CG)
    kpad = jnp.concatenate([jnp.zeros_like(kflat), kflat], axis=-1)
    back = jnp.arange(t_len - 1, -1, -1, dtype=F32)[None, :, None]
    bmag = jnp.exp(back * (step * ar)[:, None, :])
    bph = back * ph[:, None, :]
    pt_re, pt_im = times_bbar(bmag * jnp.cos(bph), bmag * jnp.sin(bph))
    pt = jnp.concatenate([pt_re, pt_im], axis=-1).reshape(SSM_G, t_len * SSM_CG, 2 * SSM_N)
    up_re = jnp.swapaxes(pw_re[:, 1:], 1, 2)[..., None]
    up_im = jnp.swapaxes(pw_im[:, 1:], 1, 2)[..., None]
    crt = jnp.swapaxes(cr, 1, 2)[:, :, None, :]
    cit = jnp.swapaxes(ci, 1, 2)[:, :, None, :]
    q_re = crt * up_re - cit * up_im
    q_im = crt * up_im + cit * up_re
    qt = jnp.concatenate([q_re, -q_im], axis=1).reshape(SSM_G, 2 * SSM_N, t_len * SSM_CG)
    at_re = pw_re[:, t_len]
    at_im = pw_im[:, t_len]
    a1 = jnp.concatenate([at_re, at_re], axis=-1)
    a2 = jnp.concatenate([-at_im, at_im], axis=-1)
    return kpad, pt.astype(BF16), qt.astype(BF16), a1, a2


def _ssm_state_kernel(u_ref, pt_ref, s_ref):
    s_ref[0] = _dot(u_ref[0], pt_ref[0])


def _ssm_scan_kernel(s_ref, x0_ref, a1_ref, a2_ref, x_ref, *, n_chunks, bsz):
    a1 = a1_ref[...][:, None, :]
    a2 = a2_ref[...][:, None, :]

    def body(c, x):
        rows = pl.ds(c * bsz, bsz)
        x_ref[:, rows, :] = x
        return a1 * x + a2 * pltpu.roll(x, SSM_N, 2) + s_ref[:, rows, :]

    x0 = jnp.broadcast_to(x0_ref[:, 0:1, :], (SSM_G, bsz, 2 * SSM_N))
    lax.fori_loop(0, n_chunks, body, x0)


def _ssm_out_kernel(u_ref, x_ref, kp_ref, qt_ref, y_ref, mt_ref):
    kp = kp_ref[0]
    width = SSM_T * SSM_CG
    lane_tile = 128
    for sub in range(0, lane_tile, SSM_CG):
        shifted = kp if sub == 0 else pltpu.roll(kp, 2 * width - sub, 1)
        for s in range(SSM_T):
            off = (SSM_T - s) * SSM_CG
            if off % lane_tile == sub:
                base = off - sub
                mt_ref[s * SSM_CG:(s + 1) * SSM_CG, :] = shifted[:, base:base + width].astype(BF16)
    x = x_ref[0]
    x_hi = x.astype(BF16)
    x_lo = (x - x_hi.astype(F32)).astype(BF16)
    y = _dot(u_ref[0], mt_ref[...]) + _dot(x_hi, qt_ref[0]) + _dot(x_lo, qt_ref[0])
    y_ref[0] = y.astype(y_ref.dtype)


def _ssm_chunk_state(ug, pt):
    g, r, w = ug.shape
    n2 = 2 * SSM_N
    per_g = lambda gi: (gi, 0, 0)
    return pl.pallas_call(
        _ssm_state_kernel,
        grid=(g,),
        in_specs=[pl.BlockSpec((1, r, w), per_g), pl.BlockSpec((1, w, n2), per_g)],
        out_specs=pl.BlockSpec((1, r, n2), per_g),
        out_shape=jax.ShapeDtypeStruct((g, r, n2), F32),
        compiler_params=pltpu.CompilerParams(dimension_semantics=("parallel",), vmem_limit_bytes=VMEM_LIMIT),
        name="ssm_chunk_state",
    )(ug, pt)


def _ssm(ug, ug_meta, kpad, pt, qt, a1, a2, *, n_chunks, bsz):
    g, r, w = ug.shape
    n2 = 2 * SSM_N
    per_g = lambda gi: (gi, 0, 0)
    params = pltpu.CompilerParams(dimension_semantics=("parallel",), vmem_limit_bytes=VMEM_LIMIT)
    s = _ssm_chunk_state(ug, pt)
    x0 = _ssm_chunk_state(ug_meta, pt)
    x = pl.pallas_call(
        functools.partial(_ssm_scan_kernel, n_chunks=n_chunks, bsz=bsz),
        out_shape=jax.ShapeDtypeStruct((g, r, n2), F32),
        compiler_params=pltpu.CompilerParams(vmem_limit_bytes=VMEM_LIMIT),
        name="ssm_chunk_scan",
    )(s, x0, a1, a2)
    return pl.pallas_call(
        _ssm_out_kernel,
        grid=(g,),
        in_specs=[pl.BlockSpec((1, r, w), per_g), pl.BlockSpec((1, r, n2), per_g),
                  pl.BlockSpec((1, SSM_CG, 2 * w), per_g), pl.BlockSpec((1, n2, w), per_g)],
        out_specs=pl.BlockSpec((1, r, w), per_g),
        out_shape=jax.ShapeDtypeStruct((g, r, w), BF16),
        scratch_shapes=[pltpu.VMEM((w, w), BF16)],
        compiler_params=params,
        name="ssm_output",
    )(ug, x, kpad, qt)


def _mix_kernel(x_ref, gi_ref, bi_ref, att_ref, y_ref, wglu_ref, bglu_ref, wout_ref, g1_ref, b1_ref,
                wr_ref, br_ref, h1_ref, idx_ref, gate_ref, rank_ref, cnt_ref):
    h0 = _layer_norm(x_ref[...], gi_ref[...], bi_ref[...])
    y = y_ref[...].astype(F32)
    y = y * (0.5 * (1.0 + jnp.tanh(math.sqrt(2.0 / math.pi) * (y + 0.044715 * (y * y * y)))))
    y = y * jax.nn.sigmoid(_dot(y.astype(BF16), wglu_ref[...]) + bglu_ref[...])
    mix = _dot(att_ref[...], wout_ref[0:ATT_W, :]) + _dot(y.astype(BF16), wout_ref[ATT_W:, :])
    h1 = _layer_norm(DEEPNORM_ALPHA * h0 + mix, g1_ref[...], b1_ref[...])
    h1_ref[...] = h1

    logits = _dot_nt(wr_ref[...], h1, precision=lax.Precision.HIGHEST) + br_ref[...]
    tm = logits.shape[1]
    eidx = lax.broadcasted_iota(I32, logits.shape, 0)
    vals, hots = [], []
    rest = logits
    for _ in range(TOP_K):
        mx = jnp.max(rest, axis=0, keepdims=True)
        first = jnp.min(jnp.where(rest == mx, eidx, N_EXPERTS), axis=0, keepdims=True)
        hot = eidx == first
        vals.append(mx)
        hots.append(hot)
        rest = jnp.where(hot, -jnp.inf, rest)
    exps = [jnp.exp(v - vals[0]) for v in vals]
    denom = exps[0] + exps[1] + exps[2] + exps[3]
    gate_ref[...] = jnp.concatenate([e / denom for e in exps], axis=0)
    idx_ref[...] = jnp.concatenate(
        [jnp.sum(jnp.where(h, eidx, 0), axis=0, keepdims=True) for h in hots], axis=0)

    hot_all = (hots[0] | hots[1] | hots[2] | hots[3]).astype(F32)
    sub = ROW_TILE
    tri = (lax.broadcasted_iota(I32, (sub, sub), 0) < lax.broadcasted_iota(I32, (sub, sub), 1)).astype(BF16)
    for part in range(tm // sub):
        cols = slice(part * sub, (part + 1) * sub)
        before = _dot(hot_all[:, cols].astype(BF16), tri)
        rank_ref[:, cols] = jnp.concatenate(
            [jnp.sum(jnp.where(h[:, cols], before, 0.0), axis=0, keepdims=True) for h in hots], axis=0).astype(I32)
        cnt_ref[part] = jnp.broadcast_to(jnp.sum(hot_all[:, cols], axis=1, keepdims=True), cnt_ref.shape[1:])


def _mix(x2, gi, bi, att, y, wglu, bglu, wout, g1, b1, wr_t, br):
    t, d = x2.shape
    tm = MIX_TILE
    row = lambda i: (i, 0)
    col = lambda i: (0, i)
    const = lambda i: (0, 0)
    return pl.pallas_call(
        _mix_kernel,
        grid=(t // tm,),
        in_specs=[
            pl.BlockSpec((tm, d), row), pl.BlockSpec((1, d), const), pl.BlockSpec((1, d), const),
            pl.BlockSpec((tm, ATT_W), row), pl.BlockSpec((tm, SSM_W), row),
            pl.BlockSpec((SSM_W, SSM_W), const), pl.BlockSpec((1, SSM_W), const),
            pl.BlockSpec((d, d), const), pl.BlockSpec((1, d), const), pl.BlockSpec((1, d), const),
            pl.BlockSpec((N_EXPERTS, d), const), pl.BlockSpec((N_EXPERTS, 1), const),
        ],
        out_specs=[
            pl.BlockSpec((tm, d), row),
            pl.BlockSpec((TOP_K, tm), col), pl.BlockSpec((TOP_K, tm), col), pl.BlockSpec((TOP_K, tm), col),
            pl.BlockSpec((tm // ROW_TILE, N_EXPERTS, 128), lambda i: (i, 0, 0)),
        ],
        out_shape=[
            jax.ShapeDtypeStruct((t, d), F32),
            jax.ShapeDtypeStruct((TOP_K, t), I32), jax.ShapeDtypeStruct((TOP_K, t), F32),
            jax.ShapeDtypeStruct((TOP_K, t), I32),
            jax.ShapeDtypeStruct((t // ROW_TILE, N_EXPERTS, 128), F32),
        ],
        compiler_params=pltpu.CompilerParams(dimension_semantics=("parallel",), vmem_limit_bytes=VMEM_LIMIT),
        name="mix_ln1_router",
    )(x2, gi, bi, att, y, wglu, bglu, wout, g1, b1, wr_t, br)


_HI_BITS = -65536


def _pack_halves(x):
    half = x.shape[1] // 2
    lo = lax.bitcast_convert_type(x[:, :half], I32)
    hi = lax.bitcast_convert_type(x[:, half:], I32)
    return lax.shift_right_logical(lo, 16) | (hi & _HI_BITS)


def _unpack_halves(w):
    lo = lax.bitcast_convert_type(lax.shift_left(w, 16), F32)
    hi = lax.bitcast_convert_type(w & _HI_BITS, F32)
    return lo.astype(BF16), hi.astype(BF16)


def _group_copy(src, src_group, dst, dst_group, sem):
    return pltpu.make_async_copy(src.at[pl.ds(pl.multiple_of(src_group * GROUP, GROUP), GROUP), :],
                                 dst.at[pl.ds(pl.multiple_of(dst_group * GROUP, GROUP), GROUP), :], sem)


def _dispatch_kernel(ngroups_ref, pad_start_ref, pad_count_ref, n_used_ref, dstg_ref, pos_ref, h1_ref, xs_hbm,
                     sorted_ref, zero_ref, sems, zsem):
    step = pl.program_id(0)
    n_steps = pl.num_programs(0)
    slot = step % 2
    n_blocks = xs_hbm.shape[0] // MOE_BLOCK

    def zero_block(blk):
        return pltpu.make_async_copy(zero_ref, xs_hbm.at[pl.ds(blk * MOE_BLOCK, MOE_BLOCK), :], zsem)

    @pl.when(step == 0)
    def _():
        zero_ref[...] = jnp.zeros(zero_ref.shape, I32)
        for e in range(N_EXPERTS):
            start = pad_start_ref[e]
            count = pad_count_ref[e]

            def fill(j, c):
                _group_copy(zero_ref, 0, xs_hbm, start + j, zsem).start()
                return c

            def drain(j, c):
                _group_copy(zero_ref, 0, xs_hbm, start, zsem).wait()
                return c

            lax.fori_loop(0, count, fill, 0)
            lax.fori_loop(0, count, drain, 0)

        def fill_block(blk, c):
            zero_block(blk).start()
            return c

        def drain_block(blk, c):
            zero_block(blk).wait()
            return c

        lax.fori_loop(n_used_ref[0], n_blocks, fill_block, 0)
        lax.fori_loop(n_used_ref[0], n_blocks, drain_block, 0)

    pos = pos_ref[...]
    rows = lax.broadcasted_iota(I32, (SORT_ROWS, pos.shape[1]), 0)
    place = rows == pos[0:1]
    for k in range(1, TOP_K):
        place = place | (rows == pos[k:k + 1])
    sorted_ref[slot] = _pack_halves(_dot(place.astype(BF16), h1_ref[...].astype(BF16)))

    def issue(j, c):
        _group_copy(sorted_ref.at[slot], j, xs_hbm, dstg_ref[0, 0, j], sems.at[slot]).start()
        return c

    lax.fori_loop(0, ngroups_ref[step], issue, 0)

    def wait_tile(s, count):
        def drain(j, c):
            _group_copy(sorted_ref.at[s], 0, xs_hbm, 0, sems.at[s]).wait()
            return c
        lax.fori_loop(0, count, drain, 0)

    @pl.when(step > 0)
    def _():
        wait_tile(1 - slot, ngroups_ref[jnp.maximum(step - 1, 0)])

    @pl.when(step == n_steps - 1)
    def _():
        wait_tile(slot, ngroups_ref[step])


def _dispatch(ngroups, pad_start, pad_count, n_used, dstg, pos_t, h1, n_rows):
    t, d = h1.shape
    tm = ROW_TILE
    grid_spec = pltpu.PrefetchScalarGridSpec(
        num_scalar_prefetch=4,
        grid=(t // tm,),
        in_specs=[
            pl.BlockSpec((1, 1, SORT_GROUPS), lambda i, *_: (i, 0, 0), memory_space=pltpu.SMEM),
            pl.BlockSpec((TOP_K, tm), lambda i, *_: (0, i)),
            pl.BlockSpec((tm, d), lambda i, *_: (i, 0)),
        ],
        out_specs=pl.BlockSpec(memory_space=pl.ANY),
        scratch_shapes=[pltpu.VMEM((2, SORT_ROWS, d // 2), I32), pltpu.VMEM((MOE_BLOCK, d // 2), I32),
                        pltpu.SemaphoreType.DMA((2,)), pltpu.SemaphoreType.DMA],
    )
    return pl.pallas_call(
        _dispatch_kernel,
        grid_spec=grid_spec,
        out_shape=jax.ShapeDtypeStruct((n_rows, d // 2), I32),
        compiler_params=pltpu.CompilerParams(dimension_semantics=("arbitrary",), vmem_limit_bytes=VMEM_LIMIT,
                                             has_side_effects=True),
        name="moe_dispatch",
    )(ngroups, pad_start, pad_count, n_used, dstg, pos_t, h1)


def _expert_kernel(be_ref, nb_ref, x_ref, wg_ref, bg_ref, wu_ref, bu_ref, wd_ref, bd_ref, y_ref,
                   wg_b, wu_b, wd_b):
    i = pl.program_id(0)

    @pl.when(i < nb_ref[0])
    def _():
        @pl.when((i == 0) | (be_ref[i] != be_ref[jnp.maximum(i - 1, 0)]))
        def _():
            wg_b[...] = wg_ref[0].astype(BF16)
            wu_b[...] = wu_ref[0].astype(BF16)
            wd_b[...] = wd_ref[0].astype(BF16)

        x_lo, x_hi = _unpack_halves(x_ref[...])
        half = x_lo.shape[1]
        gate = _dot(x_lo, wg_b[0:half, :]) + _dot(x_hi, wg_b[half:, :]) + bg_ref[0]
        up = _dot(x_lo, wu_b[0:half, :]) + _dot(x_hi, wu_b[half:, :]) + bu_ref[0]
        gate = jnp.minimum(gate, SWIGLU_LIMIT)
        up = jnp.clip(up, -SWIGLU_LIMIT, SWIGLU_LIMIT)
        act = (up + 1.0) * gate * jax.nn.sigmoid(gate * SWIGLU_ALPHA)
        y = _dot(act.astype(BF16), wd_b[...]) + bd_ref[0]
        y_ref[...] = _pack_halves(y.astype(BF16).astype(F32))

    @pl.when(i >= nb_ref[0])
    def _():
        y_ref[...] = jnp.zeros(y_ref.shape, I32)


def _experts(block_expert, n_used, xs, wg, bg, wu, bu, wd, bd):
    n_rows, packed = xs.shape
    d, dff = wg.shape[1], wg.shape[2]
    assert packed * 2 == d and wd.shape[2] == d
    nb = n_rows // MOE_BLOCK
    blk = lambda i, be, nu: (jnp.minimum(i, nu[0] - 1), 0)
    wsel = lambda i, be, nu: (be[jnp.minimum(i, nu[0] - 1)], 0, 0)
    grid_spec = pltpu.PrefetchScalarGridSpec(
        num_scalar_prefetch=2,
        grid=(nb,),
        in_specs=[
            pl.BlockSpec((MOE_BLOCK, packed), blk),
            pl.BlockSpec((1, d, dff), wsel), pl.BlockSpec((1, 1, dff), wsel),
            pl.BlockSpec((1, d, dff), wsel), pl.BlockSpec((1, 1, dff), wsel),
            pl.BlockSpec((1, dff, d), wsel), pl.BlockSpec((1, 1, d), wsel),
        ],
        out_specs=pl.BlockSpec((MOE_BLOCK, packed), lambda i, be, nu: (i, 0)),
        scratch_shapes=[pltpu.VMEM((d, dff), BF16), pltpu.VMEM((d, dff), BF16), pltpu.VMEM((dff, d), BF16)],
    )
    return pl.pallas_call(
        _expert_kernel,
        grid_spec=grid_spec,
        out_shape=jax.ShapeDtypeStruct((n_rows, packed), I32),
        compiler_params=pltpu.CompilerParams(dimension_semantics=("arbitrary",), vmem_limit_bytes=VMEM_LIMIT),
        name="moe_experts",
    )(block_expert, n_used, xs, wg, bg, wu, bu, wd, bd)


def _combine_kernel(ngroups_ref, dstg_ref, dstg_next_ref, pos_ref, gate_ref, h1_ref, ys_hbm, g2_ref, b2_ref,
                    o_ref, buf, sems):
    step = pl.program_id(0)
    n_steps = pl.num_programs(0)
    slot = step % 2

    def fetch(dref, s, j):
        return _group_copy(ys_hbm, dref[0, 0, j], buf.at[s], j, sems.at[s])

    def issue(dref, s, count):
        def body(j, c):
            fetch(dref, s, j).start()
            return c
        lax.fori_loop(0, count, body, 0)

    @pl.when(step == 0)
    def _():
        buf[...] = jnp.zeros(buf.shape, I32)
        issue(dstg_ref, 0, ngroups_ref[0])

    @pl.when(step + 1 < n_steps)
    def _():
        issue(dstg_next_ref, 1 - slot, ngroups_ref[jnp.minimum(step + 1, n_steps - 1)])

    def drain(j, c):
        fetch(dstg_ref, slot, 0).wait()
        return c

    lax.fori_loop(0, ngroups_ref[step], drain, 0)

    pos = pos_ref[...]
    gate = gate_ref[...]
    lanes = lax.broadcasted_iota(I32, (pos.shape[0], SORT_ROWS), 1)
    sel = jnp.where(lanes == pos[:, 0:1], gate[:, 0:1], 0.0)
    for k in range(1, TOP_K):
        sel = sel + jnp.where(lanes == pos[:, k:k + 1], gate[:, k:k + 1], 0.0)
    sel_hi = sel.astype(BF16)
    sel_lo = (sel - sel_hi.astype(F32)).astype(BF16)
    y_lo, y_hi = _unpack_halves(buf[slot])
    ffn = jnp.concatenate([_dot(sel_hi, y_lo) + _dot(sel_lo, y_lo), _dot(sel_hi, y_hi) + _dot(sel_lo, y_hi)], axis=1)
    o_ref[...] = _layer_norm(DEEPNORM_ALPHA * h1_ref[...] + ffn, g2_ref[...], b2_ref[...])


def _combine(ngroups, dstg, pos_c, gates, h1, ys, g2, b2):
    t, d = h1.shape
    tm = ROW_TILE
    n_steps = t // tm
    row = lambda i, ng: (i, 0)
    const = lambda i, ng: (0, 0)
    grid_spec = pltpu.PrefetchScalarGridSpec(
        num_scalar_prefetch=1,
        grid=(n_steps,),
        in_specs=[
            pl.BlockSpec((1, 1, SORT_GROUPS), lambda i, ng: (i, 0, 0), memory_space=pltpu.SMEM),
            pl.BlockSpec((1, 1, SORT_GROUPS), lambda i, ng: (jnp.minimum(i + 1, n_steps - 1), 0, 0),
                         memory_space=pltpu.SMEM),
            pl.BlockSpec((tm, TOP_K), row), pl.BlockSpec((tm, TOP_K), row), pl.BlockSpec((tm, d), row),
            pl.BlockSpec(memory_space=pl.ANY),
            pl.BlockSpec((1, d), const), pl.BlockSpec((1, d), const),
        ],
        out_specs=pl.BlockSpec((tm, d), row),
        scratch_shapes=[pltpu.VMEM((2, SORT_ROWS, d // 2), I32), pltpu.SemaphoreType.DMA((2,))],
    )
    return pl.pallas_call(
        _combine_kernel,
        grid_spec=grid_spec,
        out_shape=jax.ShapeDtypeStruct((t, d), F32),
        compiler_params=pltpu.CompilerParams(dimension_semantics=("arbitrary",), vmem_limit_bytes=VMEM_LIMIT),
        name="moe_combine_ln2",
    )(ngroups, dstg, dstg, pos_c, gates, h1, ys, g2, b2)


def kernel(x, meta_tokens, ln_in_g, ln_in_b, rel_bias, w_in, lambda_q1, lambda_k1, lambda_q2, lambda_k2,
           subln_g, a_re, a_im, log_step, b_re, b_im, c_re, c_im, d_skip, w_glu, b_glu, w_out, ln1_g, ln1_b,
           w_router, b_router, w_gate, b_gate, w_up, b_up, w_down, b_down, ln2_g, ln2_b):
    bsz, seq, dm = x.shape
    assert seq % 512 == 0 and w_in.shape[0] == DEPTH == 1
    layer = 0
    row2 = lambda v: v.astype(F32).reshape(1, -1)

    w_in_b = w_in[layer].astype(BF16)
    gi, bi = row2(ln_in_g), row2(ln_in_b)
    q, k, vt, u = _inproj(x, gi, bi, w_in_b, tm=512, kv_tile=KV_TILE)
    meta = jnp.zeros((1, META_PAD, dm), x.dtype).at[0, :N_META].set(meta_tokens.astype(x.dtype))
    _, k_meta, vt_meta, u_meta = _inproj(meta, gi, bi, w_in_b, tm=META_PAD, kv_tile=META_PAD)

    lam_init = 0.8 - 0.6 * math.exp(-0.3 * layer)
    lam = (jnp.exp(jnp.sum(lambda_q1[layer].astype(F32) * lambda_k1[layer].astype(F32)))
           - jnp.exp(jnp.sum(lambda_q2[layer].astype(F32) * lambda_k2[layer].astype(F32))) + lam_init)
    att = _attention(lam.reshape(1), q, k, vt, k_meta[0], vt_meta[0, :, 0], _near_bias(rel_bias),
                     subln_g[layer].astype(F32).reshape(HEAD_W, 1), lam_init=lam_init)

    kpad, pt, qt, a1, a2 = _ssm_tables(a_re[layer], a_im[layer], log_step[layer], b_re[layer], b_im[layer],
                                     c_re[layer], c_im[layer], d_skip[layer])
    n_chunks = seq // SSM_T
    assert (n_chunks * bsz) % 16 == 0
    ug = jnp.transpose(u.reshape(bsz, n_chunks, SSM_T, SSM_G, SSM_CG), (3, 1, 0, 2, 4))
    ug = ug.reshape(SSM_G, n_chunks * bsz, SSM_T * SSM_CG)
    lead = jnp.zeros((8, SSM_T, SSM_W), BF16).at[0, SSM_T - N_META:].set(u_meta[0, :N_META])
    ug_meta = jnp.transpose(lead.reshape(8, SSM_T, SSM_G, SSM_CG), (2, 0, 1, 3)).reshape(SSM_G, 8, SSM_T * SSM_CG)
    yg = _ssm(ug, ug_meta, kpad, pt, qt, a1, a2, n_chunks=n_chunks, bsz=bsz)
    y_ssm = jnp.transpose(yg.reshape(SSM_G, n_chunks, bsz, SSM_T, SSM_CG), (2, 1, 3, 0, 4))
    y_ssm = y_ssm.reshape(bsz, seq, SSM_W)

    t = bsz * seq
    h1, idx_t, gate_t, rank_t, cnt = _mix(
        x.reshape(t, dm), gi, bi, att.reshape(t, ATT_W), y_ssm.reshape(t, SSM_W),
        w_glu[layer].astype(BF16), row2(b_glu[layer]), w_out[layer].astype(BF16),
        row2(ln1_g[layer]), row2(ln1_b[layer]),
        jnp.transpose(w_router[layer].astype(F32)), b_router[layer].astype(F32).reshape(N_EXPERTS, 1))

    n_tiles = t // ROW_TILE
    experts = jnp.arange(N_EXPERTS, dtype=I32)
    tiles = jnp.arange(n_tiles, dtype=I32)
    run = (cnt[:, :, 0].astype(I32) + GROUP - 1) // GROUP * GROUP
    run_off = jnp.sum(jnp.where((experts[:, None] < experts[None, :])[None], run[:, :, None], 0), axis=1)
    run_before = jnp.sum(jnp.where((tiles[:, None] < tiles[None, :])[:, :, None], run[:, None, :], 0), axis=0)
    counts = jnp.sum(run, axis=0)
    padded = (counts + MOE_BLOCK - 1) // MOE_BLOCK * MOE_BLOCK
    padded_end = jnp.sum(jnp.where(experts[:, None] <= experts[None, :], padded[:, None], 0), axis=0)
    padded_start = padded_end - padded
    n_blocks = (t * TOP_K + n_tiles * N_EXPERTS * (GROUP - 1)) // MOE_BLOCK + N_EXPERTS
    block_row0 = jnp.arange(n_blocks, dtype=I32) * MOE_BLOCK
    block_expert = jnp.minimum(jnp.sum((padded_end[None, :] <= block_row0[:, None]).astype(I32), axis=1),
                               N_EXPERTS - 1)
    n_used = (padded_end[-1:] // MOE_BLOCK).astype(I32)
    ngroups = jnp.sum(run, axis=1) // GROUP
    local_row = jnp.arange(SORT_GROUPS, dtype=I32) * GROUP
    owner = jnp.minimum(jnp.sum(((run_off + run)[:, None, :] <= local_row[None, :, None]).astype(I32), axis=-1),
                        N_EXPERTS - 1)
    shift = (padded_start[None, :] + run_before - run_off) // GROUP
    dstg = jnp.sum(jnp.where(owner[..., None] == experts, shift[:, None, :], 0), axis=-1) + local_row // GROUP
    dstg = jnp.where(local_row[None, :] // GROUP < ngroups[:, None], dstg, 0).reshape(n_tiles, 1, SORT_GROUPS)
    run_off_tok = jnp.repeat(run_off, ROW_TILE, axis=0)
    pos_t = jnp.sum(jnp.where(idx_t[..., None] == experts, run_off_tok[None], 0), axis=-1) + rank_t

    xs = _dispatch(ngroups, ((padded_start + counts) // GROUP).astype(I32),
                   ((padded - counts) // GROUP).astype(I32), n_used, dstg, pos_t, h1, n_blocks * MOE_BLOCK)
    b3 = lambda v: v.astype(F32)[:, None, :]
    ys = _experts(block_expert, n_used, xs, w_gate[layer], b3(b_gate[layer]), w_up[layer], b3(b_up[layer]),
                  w_down[layer], b3(b_down[layer]))
    out = _combine(ngroups, dstg, jnp.transpose(pos_t), jnp.transpose(gate_t), h1, ys,
                   row2(ln2_g[layer]), row2(ln2_b[layer]))
    return out.reshape(bsz, seq, dm)
```

```python
import functools
import math

import jax
import jax.numpy as jnp
import numpy as np
from jax import lax
from jax.experimental import pallas as pl
from jax.experimental.pallas import tpu as pltpu

F32 = jnp.float32
BF16 = jnp.bfloat16
I32 = jnp.int32

DEPTH = 1
N_META = 16
CHUNK = 64
ATT_HEADS = 4
HEAD_DIM = 64
HEAD_W = 2 * HEAD_DIM
ATT_W = ATT_HEADS * HEAD_W
SSM_W = 512
SSM_CG = 16
SSM_G = SSM_W // SSM_CG
SSM_N = 64
N_BUCKETS = 32
MAX_DISTANCE = 128
N_EXPERTS = 32
TOP_K = 4
SWIGLU_LIMIT = 7.0
SWIGLU_ALPHA = 1.702
LN_EPS = 1e-5
NEG_INF = -1e30
DEEPNORM_ALPHA = (2.0 * DEPTH) ** 0.25
LOG2E = 1.4426950408889634

Q_TILE = 512
KV_TILE = 512
V_ONES = 16
V_ROWS = HEAD_W + V_ONES
META_PAD = 128
SSM_T = 64
MIX_TILE = 512
ROW_TILE = 256
MOE_BLOCK = 512
GROUP = 8
SORT_ROWS = 1280
SORT_GROUPS = SORT_ROWS // GROUP
assert SORT_ROWS >= ROW_TILE * TOP_K + N_EXPERTS * (GROUP - 1)
VMEM_LIMIT = 56 * 1024 * 1024


def _layer_norm(x, g, b):
    mu = jnp.mean(x, axis=-1, keepdims=True)
    xc = x - mu
    var = jnp.mean(xc * xc, axis=-1, keepdims=True)
    return xc * lax.rsqrt(var + LN_EPS) * g + b


def _dot(a, b):
    return jnp.dot(a, b, preferred_element_type=F32)


def _dot_nt(a, b, precision=None):
    return lax.dot_general(a, b, (((1,), (1,)), ((), ())), precision=precision,
                           preferred_element_type=F32)


def _inproj_kernel(x_ref, g_ref, b_ref, w_ref, q_ref, k_ref, vt_ref, u_ref, *, kv_tile, q_scale):
    h = _layer_norm(x_ref[0], g_ref[...], b_ref[...]).astype(BF16)
    tm = h.shape[0]
    q_ref[0] = (_dot(h, w_ref[:, 0:ATT_W]) * q_scale).astype(BF16)
    k_ref[0] = _dot(h, w_ref[:, ATT_W:2 * ATT_W]).astype(BF16)
    v = _dot(h, w_ref[:, 2 * ATT_W:3 * ATT_W])
    ones = jnp.ones((V_ONES, kv_tile), BF16)
    for hh in range(ATT_HEADS):
        vt = v[:, hh * HEAD_W:(hh + 1) * HEAD_W].T.astype(BF16)
        for j in range(tm // kv_tile):
            vt_ref[0, hh, j, 0:HEAD_W, :] = vt[:, j * kv_tile:(j + 1) * kv_tile]
            vt_ref[0, hh, j, HEAD_W:V_ROWS, :] = ones
    u_ref[0] = _dot(h, w_ref[:, 3 * ATT_W:]).astype(BF16)


def _inproj(x, g, b, w, *, tm, kv_tile):
    bsz, s, d = x.shape
    q_scale = HEAD_DIM ** -0.5 * LOG2E
    n_cols = w.shape[1]
    row = lambda bi, i: (bi, i, 0)
    return pl.pallas_call(
        functools.partial(_inproj_kernel, kv_tile=kv_tile, q_scale=q_scale),
        grid=(bsz, s // tm),
        in_specs=[
            pl.BlockSpec((1, tm, d), row),
            pl.BlockSpec((1, d), lambda bi, i: (0, 0)),
            pl.BlockSpec((1, d), lambda bi, i: (0, 0)),
            pl.BlockSpec((d, n_cols), lambda bi, i: (0, 0)),
        ],
        out_specs=[
            pl.BlockSpec((1, tm, ATT_W), row),
            pl.BlockSpec((1, tm, ATT_W), row),
            pl.BlockSpec((1, ATT_HEADS, tm // kv_tile, V_ROWS, kv_tile), lambda bi, i: (bi, 0, i, 0, 0)),
            pl.BlockSpec((1, tm, SSM_W), row),
        ],
        out_shape=[
            jax.ShapeDtypeStruct((bsz, s, ATT_W), BF16),
            jax.ShapeDtypeStruct((bsz, s, ATT_W), BF16),
            jax.ShapeDtypeStruct((bsz, ATT_HEADS, s // kv_tile, V_ROWS, kv_tile), BF16),
            jax.ShapeDtypeStruct((bsz, s, SSM_W), BF16),
        ],
        compiler_params=pltpu.CompilerParams(
            dimension_semantics=("parallel", "parallel"), vmem_limit_bytes=VMEM_LIMIT),
        name="inproj",
    )(x, g, b, w)


def _attn_kernel(lam_ref, q_ref, k_ref, vt_ref, km_ref, vtm_ref, bias_ref, g_ref, o_ref,
                 m_ref, acc_ref, qz_ref, sa_ref, sb_ref, sn_ref, *, out_scale):
    i = pl.program_id(2)
    qt = q_ref[0].astype(F32).T
    feat = lax.broadcasted_iota(I32, qt.shape, 0)
    qz_ref[:, 0:Q_TILE] = jnp.where(feat < HEAD_DIM, qt, 0.0).astype(BF16)
    qz_ref[:, Q_TILE:] = jnp.where(feat >= HEAD_DIM, qt, 0.0).astype(BF16)

    m_ref[...] = jnp.full(m_ref.shape, NEG_INF, F32)
    acc_ref[...] = jnp.zeros(acc_ref.shape, F32)

    def scores(kt):
        return _dot(kt, qz_ref[...])

    def k_tile(t):
        return k_ref[0, pl.ds(pl.multiple_of(t * KV_TILE, KV_TILE), KV_TILE), :]

    def absorb(s, pv):
        m_old = m_ref[...]
        m_new = jnp.maximum(m_old, jnp.max(s, axis=0, keepdims=True))
        alpha = jnp.exp2(m_old - m_new)
        p = jnp.exp2(s - m_new).astype(BF16)
        acc_ref[...] = acc_ref[...] * alpha + pv(p)
        m_ref[...] = m_new

    def tile_pv(t):
        return lambda p: _dot(vt_ref[0, 0, t], p)

    n_far = jnp.maximum(i - 1, 0)
    peel = n_far % 2

    sn_ref[...] = scores(km_ref[...])
    sa_ref[...] = scores(k_tile(0))
    absorb(sn_ref[...] + bias_ref[0, 0, 2 * KV_TILE:, :], lambda p: _dot(vtm_ref[0], p))

    @pl.when(peel == 1)
    def _():
        absorb(sa_ref[...], tile_pv(0))
        sa_ref[...] = scores(k_tile(1))

    def far_pair(j, carry):
        t0 = peel + 2 * j
        s_cur = sa_ref[...]
        sb_ref[...] = scores(k_tile(t0 + 1))
        absorb(s_cur, tile_pv(t0))
        s_cur = sb_ref[...]
        sa_ref[...] = scores(k_tile(t0 + 2))
        absorb(s_cur, tile_pv(t0 + 1))
        return carry

    lax.fori_loop(0, n_far // 2, far_pair, 0)

    sb_ref[...] = scores(k_tile(i))
    absorb(sa_ref[...] + bias_ref[0, 0, 0:KV_TILE, :], tile_pv(n_far))
    absorb(sb_ref[...] + bias_ref[0, 0, KV_TILE:2 * KV_TILE, :], tile_pv(i))

    acc = acc_ref[...]
    lam = lam_ref[0]
    o1 = acc[0:HEAD_W, 0:Q_TILE] / acc[HEAD_W:HEAD_W + 1, 0:Q_TILE]
    o2 = acc[0:HEAD_W, Q_TILE:] / acc[HEAD_W:HEAD_W + 1, Q_TILE:]
    o = o1 - lam * o2
    ms = jnp.mean(o * o, axis=0, keepdims=True)
    o = o * lax.rsqrt(ms + LN_EPS) * g_ref[...] * out_scale
    o_ref[0] = o.T.astype(o_ref.dtype)


def _attention(lam, q, k, vt, k_meta, vt_meta, bias, subln_g, *, lam_init):
    bsz, s, _ = q.shape
    nq = s // Q_TILE
    n_near = META_PAD + 2 * KV_TILE
    grid_spec = pltpu.PrefetchScalarGridSpec(
        num_scalar_prefetch=1,
        grid=(bsz, ATT_HEADS, nq),
        in_specs=[
            pl.BlockSpec((1, Q_TILE, HEAD_W), lambda b, h, i, lam: (b, i, h)),
            pl.BlockSpec((1, s, HEAD_W), lambda b, h, i, lam: (b, 0, h)),
            pl.BlockSpec((1, 1, s // KV_TILE, V_ROWS, KV_TILE), lambda b, h, i, lam: (b, h, 0, 0, 0)),
            pl.BlockSpec((META_PAD, HEAD_W), lambda b, h, i, lam: (0, h)),
            pl.BlockSpec((1, V_ROWS, META_PAD), lambda b, h, i, lam: (h, 0, 0)),
            pl.BlockSpec((1, 1, n_near, 2 * Q_TILE), lambda b, h, i, lam: (h, jnp.minimum(i, 1), 0, 0)),
            pl.BlockSpec((HEAD_W, 1), lambda b, h, i, lam: (0, 0)),
        ],
        out_specs=pl.BlockSpec((1, Q_TILE, HEAD_W), lambda b, h, i, lam: (b, i, h)),
        scratch_shapes=[
            pltpu.VMEM((1, 2 * Q_TILE), F32), pltpu.VMEM((V_ROWS, 2 * Q_TILE), F32),
            pltpu.VMEM((HEAD_W, 2 * Q_TILE), BF16),
            pltpu.VMEM((KV_TILE, 2 * Q_TILE), F32), pltpu.VMEM((KV_TILE, 2 * Q_TILE), F32),
            pltpu.VMEM((META_PAD, 2 * Q_TILE), F32),
        ],
    )
    return pl.pallas_call(
        functools.partial(_attn_kernel, out_scale=1.0 - lam_init),
        grid_spec=grid_spec,
        out_shape=jax.ShapeDtypeStruct((bsz, s, ATT_W), BF16),
        compiler_params=pltpu.CompilerParams(
            dimension_semantics=("parallel", "parallel", "arbitrary"), vmem_limit_bytes=VMEM_LIMIT),
        name="diff_attention",
    )(lam, q, k, vt, k_meta, vt_meta, bias, subln_g)


def _t5_bucket(rel):
    nb = N_BUCKETS // 2
    max_exact = nb // 2
    ret = jnp.where(rel > 0, nb, 0)
    n = jnp.abs(rel)
    n_f = jnp.maximum(n, 1).astype(F32)
    large = max_exact + (jnp.log(n_f / max_exact) / math.log(MAX_DISTANCE / max_exact)
                         * (nb - max_exact)).astype(I32)
    large = jnp.minimum(large, nb - 1)
    return ret + jnp.where(n < max_exact, n, large)


def _near_bias(rel_bias):
    table = rel_bias.astype(F32)
    far = table[N_BUCKETS // 2 - 1]
    c = jnp.arange(Q_TILE, dtype=I32)[None, :]
    r = jnp.arange(KV_TILE, dtype=I32)[:, None]

    shifted = jnp.transpose(table - far) * LOG2E

    def bias_of(offset, rows, cols=Q_TILE):
        rel = jnp.arange(rows, dtype=I32)[:, None] - c[:, :cols] + offset
        hot = _t5_bucket(rel)[None, :, :, None] == jnp.arange(N_BUCKETS, dtype=I32)
        return jnp.sum(jnp.where(hot, shifted[:, None, None, :], 0.0), axis=-1)

    own = jnp.where((r // CHUNK <= c // CHUNK)[None], bias_of(0, KV_TILE), NEG_INF)
    near = MAX_DISTANCE
    prev = jnp.zeros((ATT_HEADS, KV_TILE, Q_TILE), F32).at[:, KV_TILE - near:, :near].set(bias_of(-near, near, near))
    rm = jnp.arange(META_PAD, dtype=I32)[:, None]
    meta_ok = (rm < N_META)[None]
    meta0 = jnp.where(meta_ok, jnp.pad(bias_of(-N_META, N_META), ((0, 0), (0, META_PAD - N_META), (0, 0))), NEG_INF)
    meta1 = jnp.where(meta_ok, jnp.zeros((ATT_HEADS, META_PAD, Q_TILE), F32), NEG_INF)
    assert N_META + Q_TILE - (N_META - 1) >= MAX_DISTANCE and KV_TILE >= near and Q_TILE >= near
    v0 = jnp.concatenate([jnp.full_like(prev, NEG_INF), own, meta0], axis=1)
    v1 = jnp.concatenate([prev, own, meta1], axis=1)
    both = jnp.stack([v0, v1], axis=1)
    return jnp.concatenate([both, both], axis=-1)


def _ssm_tables(a_re, a_im, log_step, b_re, b_im, c_re, c_im, d_skip):
    hi = lax.Precision.HIGHEST
    t_len = SSM_T
    step = jnp.exp(log_step.astype(F32))[:, None]
    ar = jnp.minimum(a_re.astype(F32), -1e-4)
    ai = a_im.astype(F32)
    mag = jnp.exp(step * ar)
    ph = step * ai
    abar_re = mag * jnp.cos(ph)
    abar_im = mag * jnp.sin(ph)
    den = ar * ar + ai * ai
    e_re = abar_re - 1.0
    e_im = abar_im
    f_re = (e_re * ar + e_im * ai) / den
    f_im = (e_im * ar - e_re * ai) / den
    br = b_re.astype(F32)
    bi = b_im.astype(F32)
    bb_re = f_re[..., None] * br - f_im[..., None] * bi
    bb_im = f_re[..., None] * bi + f_im[..., None] * br
    tau = jnp.arange(t_len + 1, dtype=F32)[None, :, None]
    pmag = jnp.exp(tau * (step * ar)[:, None, :])
    pph = tau * ph[:, None, :]
    pw_re = pmag * jnp.cos(pph)
    pw_im = pmag * jnp.sin(pph)
    bbt_re = jnp.swapaxes(bb_re, 1, 2)[:, None]
    bbt_im = jnp.swapaxes(bb_im, 1, 2)[:, None]

    def times_bbar(p_re, p_im):
        return (p_re[:, :, None, :] * bbt_re - p_im[:, :, None, :] * bbt_im,
                p_re[:, :, None, :] * bbt_im + p_im[:, :, None, :] * bbt_re)

    cr = c_re.astype(F32)
    ci = c_im.astype(F32)
    bi_re = jnp.swapaxes(bb_re, 1, 2)[:, :, None, :]
    bi_im = jnp.swapaxes(bb_im, 1, 2)[:, :, None, :]
    cb_re = cr[:, None] * bi_re - ci[:, None] * bi_im
    cb_im = cr[:, None] * bi_im + ci[:, None] * bi_re
    kern = jnp.einsum('gikm,gtm->gitk', jnp.concatenate([cb_re, -cb_im], axis=-1),
                      jnp.concatenate([pw_re[:, :t_len], pw_im[:, :t_len]], axis=-1), precision=hi)
    skip = d_skip.astype(F32).reshape(SSM_G, SSM_CG)
    kern = kern.at[:, :, 0, :].add(skip[:, None, :] * jnp.eye(SSM_CG, dtype=F32))
    kflat = kern.reshape(SSM_G, SSM_CG, t_len * SSM_CG)
    kpad = jnp.concatenate([jnp.zeros_like(kflat), kflat], axis=-1)
    back = jnp.arange(t_len - 1, -1, -1, dtype=F32)[None, :, None]
    bmag = jnp.exp(back * (step * ar)[:, None, :])
    bph = back * ph[:, None, :]
    pt_re, pt_im = times_bbar(bmag * jnp.cos(bph), bmag * jnp.sin(bph))
    pt = jnp.concatenate([pt_re, pt_im], axis=-1).reshape(SSM_G, t_len * SSM_CG, 2 * SSM_N)
    up_re = jnp.swapaxes(pw_re[:, 1:], 1, 2)[..., None]
    up_im = jnp.swapaxes(pw_im[:, 1:], 1, 2)[..., None]
    crt = jnp.swapaxes(cr, 1, 2)[:, :, None, :]
    cit = jnp.swapaxes(ci, 1, 2)[:, :, None, :]
    q_re = crt * up_re - cit * up_im
    q_im = crt * up_im + cit * up_re
    qt = jnp.concatenate([q_re, -q_im], axis=1).reshape(SSM_G, 2 * SSM_N, t_len * SSM_CG)
    at_re = pw_re[:, t_len]
    at_im = pw_im[:, t_len]
    a1 = jnp.concatenate([at_re, at_re], axis=-1)
    a2 = jnp.concatenate([-at_im, at_im], axis=-1)
    return kpad, pt.astype(BF16), qt.astype(BF16), a1, a2


def _ssm_state_kernel(u_ref, pt_ref, s_ref):
    s_ref[0] = _dot(u_ref[0], pt_ref[0])


def _ssm_scan_kernel(s_ref, x0_ref, a1_ref, a2_ref, x_ref, *, n_chunks, bsz):
    a1 = a1_ref[...][:, None, :]
    a2 = a2_ref[...][:, None, :]

    def body(c, x):
        rows = pl.ds(c * bsz, bsz)
        x_ref[:, rows, :] = x
        return a1 * x + a2 * pltpu.roll(x, SSM_N, 2) + s_ref[:, rows, :]

    x0 = jnp.broadcast_to(x0_ref[:, 0:1, :], (SSM_G, bsz, 2 * SSM_N))
    lax.fori_loop(0, n_chunks, body, x0)


def _ssm_out_kernel(u_ref, x_ref, kp_ref, qt_ref, y_ref, mt_ref):
    kp = kp_ref[0]
    width = SSM_T * SSM_CG
    lane_tile = 128
    for sub in range(0, lane_tile, SSM_CG):
        shifted = kp if sub == 0 else pltpu.roll(kp, 2 * width - sub, 1)
        for s in range(SSM_T):
            off = (SSM_T - s) * SSM_CG
            if off % lane_tile == sub:
                base = off - sub
                mt_ref[s * SSM_CG:(s + 1) * SSM_CG, :] = shifted[:, base:base + width].astype(BF16)
    x = x_ref[0]
    x_hi = x.astype(BF16)
    x_lo = (x - x_hi.astype(F32)).astype(BF16)
    y = _dot(u_ref[0], mt_ref[...]) + _dot(x_hi, qt_ref[0]) + _dot(x_lo, qt_ref[0])
    y_ref[0] = y.astype(y_ref.dtype)


def _ssm_chunk_state(ug, pt):
    g, r, w = ug.shape
    n2 = 2 * SSM_N
    per_g = lambda gi: (gi, 0, 0)
    return pl.pallas_call(
        _ssm_state_kernel,
        grid=(g,),
        in_specs=[pl.BlockSpec((1, r, w), per_g), pl.BlockSpec((1, w, n2), per_g)],
        out_specs=pl.BlockSpec((1, r, n2), per_g),
        out_shape=jax.ShapeDtypeStruct((g, r, n2), F32),
        compiler_params=pltpu.CompilerParams(dimension_semantics=("parallel",), vmem_limit_bytes=VMEM_LIMIT),
        name="ssm_chunk_state",
    )(ug, pt)


def _ssm(ug, ug_meta, kpad, pt, qt, a1, a2, *, n_chunks, bsz):
    g, r, w = ug.shape
    n2 = 2 * SSM_N
    per_g = lambda gi: (gi, 0, 0)
    params = pltpu.CompilerParams(dimension_semantics=("parallel",), vmem_limit_bytes=VMEM_LIMIT)
    s = _ssm_chunk_state(ug, pt)
    x0 = _ssm_chunk_state(ug_meta, pt[:, w - ug_meta.shape[2]:, :])
    x = pl.pallas_call(
        functools.partial(_ssm_scan_kernel, n_chunks=n_chunks, bsz=bsz),
        out_shape=jax.ShapeDtypeStruct((g, r, n2), F32),
        compiler_params=pltpu.CompilerParams(vmem_limit_bytes=VMEM_LIMIT),
        name="ssm_chunk_scan",
    )(s, x0, a1, a2)
    return pl.pallas_call(
        _ssm_out_kernel,
        grid=(g,),
        in_specs=[pl.BlockSpec((1, r, w), per_g), pl.BlockSpec((1, r, n2), per_g),
                  pl.BlockSpec((1, SSM_CG, 2 * w), per_g), pl.BlockSpec((1, n2, w), per_g)],
        out_specs=pl.BlockSpec((1, r, w), per_g),
        out_shape=jax.ShapeDtypeStruct((g, r, w), BF16),
        scratch_shapes=[pltpu.VMEM((w, w), BF16)],
        compiler_params=params,
        name="ssm_output",
    )(ug, x, kpad, qt)


def _mix_kernel(x_ref, gi_ref, bi_ref, att_ref, y_ref, wglu_ref, bglu_ref, wout_ref, g1_ref, b1_ref,
                wr_ref, br_ref, h1_ref, idx_ref, gate_ref, rank_ref, cnt_ref):
    h0 = _layer_norm(x_ref[...], gi_ref[...], bi_ref[...])
    y = y_ref[...].astype(F32)
    y = y * (0.5 * (1.0 + jnp.tanh(math.sqrt(2.0 / math.pi) * (y + 0.044715 * (y * y * y)))))
    y = y * jax.nn.sigmoid(_dot(y.astype(BF16), wglu_ref[...]) + bglu_ref[...])
    mix = _dot(att_ref[...], wout_ref[0:ATT_W, :]) + _dot(y.astype(BF16), wout_ref[ATT_W:, :])
    h1 = _layer_norm(DEEPNORM_ALPHA * h0 + mix, g1_ref[...], b1_ref[...])
    h1_ref[...] = h1

    logits = _dot_nt(wr_ref[...], h1, precision=lax.Precision.HIGHEST) + br_ref[...]
    tm = logits.shape[1]
    eidx = lax.broadcasted_iota(I32, logits.shape, 0)
    vals, hots = [], []
    rest = logits
    for _ in range(TOP_K):
        mx = jnp.max(rest, axis=0, keepdims=True)
        first = jnp.min(jnp.where(rest == mx, eidx, N_EXPERTS), axis=0, keepdims=True)
        hot = eidx == first
        vals.append(mx)
        hots.append(hot)
        rest = jnp.where(hot, -jnp.inf, rest)
    exps = [jnp.exp(v - vals[0]) for v in vals]
    denom = exps[0] + exps[1] + exps[2] + exps[3]
    gate_ref[...] = jnp.concatenate([e / denom for e in exps], axis=0)
    idx_ref[...] = jnp.concatenate(
        [jnp.sum(jnp.where(h, eidx, 0), axis=0, keepdims=True) for h in hots], axis=0)

    hot_all = (hots[0] | hots[1] | hots[2] | hots[3]).astype(F32)
    sub = ROW_TILE
    tri = (lax.broadcasted_iota(I32, (sub, sub), 0) < lax.broadcasted_iota(I32, (sub, sub), 1)).astype(BF16)
    for part in range(tm // sub):
        cols = slice(part * sub, (part + 1) * sub)
        before = _dot(hot_all[:, cols].astype(BF16), tri)
        rank_ref[:, cols] = jnp.concatenate(
            [jnp.sum(jnp.where(h[:, cols], before, 0.0), axis=0, keepdims=True) for h in hots], axis=0).astype(I32)
        cnt_ref[part] = jnp.broadcast_to(jnp.sum(hot_all[:, cols], axis=1, keepdims=True), cnt_ref.shape[1:])


def _mix(x2, gi, bi, att, y, wglu, bglu, wout, g1, b1, wr_t, br):
    t, d = x2.shape
    tm = MIX_TILE
    row = lambda i: (i, 0)
    col = lambda i: (0, i)
    const = lambda i: (0, 0)
    return pl.pallas_call(
        _mix_kernel,
        grid=(t // tm,),
        in_specs=[
            pl.BlockSpec((tm, d), row), pl.BlockSpec((1, d), const), pl.BlockSpec((1, d), const),
            pl.BlockSpec((tm, ATT_W), row), pl.BlockSpec((tm, SSM_W), row),
            pl.BlockSpec((SSM_W, SSM_W), const), pl.BlockSpec((1, SSM_W), const),
            pl.BlockSpec((d, d), const), pl.BlockSpec((1, d), const), pl.BlockSpec((1, d), const),
            pl.BlockSpec((N_EXPERTS, d), const), pl.BlockSpec((N_EXPERTS, 1), const),
        ],
        out_specs=[
            pl.BlockSpec((tm, d), row),
            pl.BlockSpec((TOP_K, tm), col), pl.BlockSpec((TOP_K, tm), col), pl.BlockSpec((TOP_K, tm), col),
            pl.BlockSpec((tm // ROW_TILE, N_EXPERTS, 128), lambda i: (i, 0, 0)),
        ],
        out_shape=[
            jax.ShapeDtypeStruct((t, d), F32),
            jax.ShapeDtypeStruct((TOP_K, t), I32), jax.ShapeDtypeStruct((TOP_K, t), F32),
            jax.ShapeDtypeStruct((TOP_K, t), I32),
            jax.ShapeDtypeStruct((t // ROW_TILE, N_EXPERTS, 128), F32),
        ],
        compiler_params=pltpu.CompilerParams(dimension_semantics=("parallel",), vmem_limit_bytes=VMEM_LIMIT),
        name="mix_ln1_router",
    )(x2, gi, bi, att, y, wglu, bglu, wout, g1, b1, wr_t, br)


_HI_BITS = -65536


def _pack_halves(x):
    half = x.shape[1] // 2
    lo = lax.bitcast_convert_type(x[:, :half], I32)
    hi = lax.bitcast_convert_type(x[:, half:], I32)
    return lax.shift_right_logical(lo, 16) | (hi & _HI_BITS)


def _unpack_halves(w):
    lo = lax.bitcast_convert_type(lax.shift_left(w, 16), F32)
    hi = lax.bitcast_convert_type(w & _HI_BITS, F32)
    return lo.astype(BF16), hi.astype(BF16)


def _group_copy(src, src_group, dst, dst_group, sem):
    return pltpu.make_async_copy(src.at[pl.ds(pl.multiple_of(src_group * GROUP, GROUP), GROUP), :],
                                 dst.at[pl.ds(pl.multiple_of(dst_group * GROUP, GROUP), GROUP), :], sem)


def _dispatch_kernel(ngroups_ref, pad_start_ref, pad_count_ref, n_used_ref, dstg_ref, pos_ref, h1_ref, xs_hbm,
                     sorted_ref, zero_ref, sems, zsem):
    step = pl.program_id(0)
    n_steps = pl.num_programs(0)
    slot = step % 2
    n_blocks = xs_hbm.shape[0] // MOE_BLOCK

    def zero_block(blk):
        return pltpu.make_async_copy(zero_ref, xs_hbm.at[pl.ds(blk * MOE_BLOCK, MOE_BLOCK), :], zsem)

    @pl.when(step == 0)
    def _():
        zero_ref[...] = jnp.zeros(zero_ref.shape, I32)
        for e in range(N_EXPERTS):
            start = pad_start_ref[e]
            count = pad_count_ref[e]

            def fill(j, c):
                _group_copy(zero_ref, 0, xs_hbm, start + j, zsem).start()
                return c

            def drain(j, c):
                _group_copy(zero_ref, 0, xs_hbm, start, zsem).wait()
                return c

            lax.fori_loop(0, count, fill, 0)
            lax.fori_loop(0, count, drain, 0)

        def fill_block(blk, c):
            zero_block(blk).start()
            return c

        def drain_block(blk, c):
            zero_block(blk).wait()
            return c

        lax.fori_loop(n_used_ref[0], n_blocks, fill_block, 0)
        lax.fori_loop(n_used_ref[0], n_blocks, drain_block, 0)

    pos = pos_ref[...]
    rows = lax.broadcasted_iota(I32, (SORT_ROWS, pos.shape[1]), 0)
    place = rows == pos[0:1]
    for k in range(1, TOP_K):
        place = place | (rows == pos[k:k + 1])
    sorted_ref[slot] = _pack_halves(_dot(place.astype(BF16), h1_ref[...].astype(BF16)))

    def issue(j, c):
        _group_copy(sorted_ref.at[slot], j, xs_hbm, dstg_ref[0, 0, j], sems.at[slot]).start()
        return c

    lax.fori_loop(0, ngroups_ref[step], issue, 0)

    def wait_tile(s, count):
        def drain(j, c):
            _group_copy(sorted_ref.at[s], 0, xs_hbm, 0, sems.at[s]).wait()
            return c
        lax.fori_loop(0, count, drain, 0)

    @pl.when(step > 0)
    def _():
        wait_tile(1 - slot, ngroups_ref[jnp.maximum(step - 1, 0)])

    @pl.when(step == n_steps - 1)
    def _():
        wait_tile(slot, ngroups_ref[step])


def _dispatch(ngroups, pad_start, pad_count, n_used, dstg, pos_t, h1, n_rows):
    t, d = h1.shape
    tm = ROW_TILE
    grid_spec = pltpu.PrefetchScalarGridSpec(
        num_scalar_prefetch=4,
        grid=(t // tm,),
        in_specs=[
            pl.BlockSpec((1, 1, SORT_GROUPS), lambda i, *_: (i, 0, 0), memory_space=pltpu.SMEM),
            pl.BlockSpec((TOP_K, tm), lambda i, *_: (0, i)),
            pl.BlockSpec((tm, d), lambda i, *_: (i, 0)),
        ],
        out_specs=pl.BlockSpec(memory_space=pl.ANY),
        scratch_shapes=[pltpu.VMEM((2, SORT_ROWS, d // 2), I32), pltpu.VMEM((MOE_BLOCK, d // 2), I32),
                        pltpu.SemaphoreType.DMA((2,)), pltpu.SemaphoreType.DMA],
    )
    return pl.pallas_call(
        _dispatch_kernel,
        grid_spec=grid_spec,
        out_shape=jax.ShapeDtypeStruct((n_rows, d // 2), I32),
        compiler_params=pltpu.CompilerParams(dimension_semantics=("arbitrary",), vmem_limit_bytes=VMEM_LIMIT,
                                             has_side_effects=True),
        name="moe_dispatch",
    )(ngroups, pad_start, pad_count, n_used, dstg, pos_t, h1)


def _expert_kernel(be_ref, nb_ref, x_ref, wg_ref, bg_ref, wu_ref, bu_ref, wd_ref, bd_ref, y_ref,
                   wg_b, wu_b, wd_b):
    i = pl.program_id(0)

    @pl.when(i < nb_ref[0])
    def _():
        @pl.when((i == 0) | (be_ref[i] != be_ref[jnp.maximum(i - 1, 0)]))
        def _():
            wg_b[...] = wg_ref[0].astype(BF16)
            wu_b[...] = wu_ref[0].astype(BF16)
            wd_b[...] = wd_ref[0].astype(BF16)

        x_lo, x_hi = _unpack_halves(x_ref[...])
        half = x_lo.shape[1]
        gate = _dot(x_lo, wg_b[0:half, :]) + _dot(x_hi, wg_b[half:, :]) + bg_ref[0]
        up = _dot(x_lo, wu_b[0:half, :]) + _dot(x_hi, wu_b[half:, :]) + bu_ref[0]
        gate = jnp.minimum(gate, SWIGLU_LIMIT)
        up = jnp.clip(up, -SWIGLU_LIMIT, SWIGLU_LIMIT)
        act = (up + 1.0) * gate * jax.nn.sigmoid(gate * SWIGLU_ALPHA)
        y = _dot(act.astype(BF16), wd_b[...]) + bd_ref[0]
        y_ref[...] = _pack_halves(y.astype(BF16).astype(F32))

    @pl.when(i >= nb_ref[0])
    def _():
        y_ref[...] = jnp.zeros(y_ref.shape, I32)


def _experts(block_expert, n_used, xs, wg, bg, wu, bu, wd, bd):
    n_rows, packed = xs.shape
    d, dff = wg.shape[1], wg.shape[2]
    assert packed * 2 == d and wd.shape[2] == d
    nb = n_rows // MOE_BLOCK
    blk = lambda i, be, nu: (jnp.minimum(i, nu[0] - 1), 0)
    wsel = lambda i, be, nu: (be[jnp.minimum(i, nu[0] - 1)], 0, 0)
    grid_spec = pltpu.PrefetchScalarGridSpec(
        num_scalar_prefetch=2,
        grid=(nb,),
        in_specs=[
            pl.BlockSpec((MOE_BLOCK, packed), blk),
            pl.BlockSpec((1, d, dff), wsel), pl.BlockSpec((1, 1, dff), wsel),
            pl.BlockSpec((1, d, dff), wsel), pl.BlockSpec((1, 1, dff), wsel),
            pl.BlockSpec((1, dff, d), wsel), pl.BlockSpec((1, 1, d), wsel),
        ],
        out_specs=pl.BlockSpec((MOE_BLOCK, packed), lambda i, be, nu: (i, 0)),
        scratch_shapes=[pltpu.VMEM((d, dff), BF16), pltpu.VMEM((d, dff), BF16), pltpu.VMEM((dff, d), BF16)],
    )
    return pl.pallas_call(
        _expert_kernel,
        grid_spec=grid_spec,
        out_shape=jax.ShapeDtypeStruct((n_rows, packed), I32),
        compiler_params=pltpu.CompilerParams(dimension_semantics=("arbitrary",), vmem_limit_bytes=VMEM_LIMIT),
        name="moe_experts",
    )(block_expert, n_used, xs, wg, bg, wu, bu, wd, bd)


def _combine_kernel(ngroups_ref, dstg_ref, dstg_next_ref, pos_ref, gate_ref, h1_ref, ys_hbm, g2_ref, b2_ref,
                    o_ref, buf, sems):
    step = pl.program_id(0)
    n_steps = pl.num_programs(0)
    slot = step % 2

    def fetch(dref, s, j):
        return _group_copy(ys_hbm, dref[0, 0, j], buf.at[s], j, sems.at[s])

    def issue(dref, s, count):
        def body(jj, c):
            fetch(dref, s, 2 * jj).start()
            fetch(dref, s, 2 * jj + 1).start()
            return c
        lax.fori_loop(0, count // 2, body, 0)

    @pl.when(step == 0)
    def _():
        buf[...] = jnp.zeros(buf.shape, I32)
        issue(dstg_ref, 0, ngroups_ref[0])

    @pl.when(step + 1 < n_steps)
    def _():
        issue(dstg_next_ref, 1 - slot, ngroups_ref[jnp.minimum(step + 1, n_steps - 1)])

    def drain(jj, c):
        pltpu.make_async_copy(ys_hbm.at[pl.ds(0, 2 * GROUP), :], buf.at[slot, pl.ds(0, 2 * GROUP), :],
                              sems.at[slot]).wait()
        return c

    lax.fori_loop(0, ngroups_ref[step] // 2, drain, 0)

    pos = pos_ref[...]
    gate = gate_ref[...]
    lanes = lax.broadcasted_iota(I32, (pos.shape[0], SORT_ROWS), 1)
    sel = jnp.where(lanes == pos[:, 0:1], gate[:, 0:1], 0.0)
    for k in range(1, TOP_K):
        sel = sel + jnp.where(lanes == pos[:, k:k + 1], gate[:, k:k + 1], 0.0)
    sel_hi = sel.astype(BF16)
    sel_lo = (sel - sel_hi.astype(F32)).astype(BF16)
    y_lo, y_hi = _unpack_halves(buf[slot])
    ffn = jnp.concatenate([_dot(sel_hi, y_lo) + _dot(sel_lo, y_lo), _dot(sel_hi, y_hi) + _dot(sel_lo, y_hi)], axis=1)
    o_ref[...] = _layer_norm(DEEPNORM_ALPHA * h1_ref[...] + ffn, g2_ref[...], b2_ref[...])


def _combine(ngroups, dstg, pos_c, gates, h1, ys, g2, b2):
    t, d = h1.shape
    tm = ROW_TILE
    n_steps = t // tm
    row = lambda i, ng: (i, 0)
    const = lambda i, ng: (0, 0)
    grid_spec = pltpu.PrefetchScalarGridSpec(
        num_scalar_prefetch=1,
        grid=(n_steps,),
        in_specs=[
            pl.BlockSpec((1, 1, SORT_GROUPS), lambda i, ng: (i, 0, 0), memory_space=pltpu.SMEM),
            pl.BlockSpec((1, 1, SORT_GROUPS), lambda i, ng: (jnp.minimum(i + 1, n_steps - 1), 0, 0),
                         memory_space=pltpu.SMEM),
            pl.BlockSpec((tm, TOP_K), row), pl.BlockSpec((tm, TOP_K), row), pl.BlockSpec((tm, d), row),
            pl.BlockSpec(memory_space=pl.ANY),
            pl.BlockSpec((1, d), const), pl.BlockSpec((1, d), const),
        ],
        out_specs=pl.BlockSpec((tm, d), row),
        scratch_shapes=[pltpu.VMEM((2, SORT_ROWS, d // 2), I32), pltpu.SemaphoreType.DMA((2,))],
    )
    return pl.pallas_call(
        _combine_kernel,
        grid_spec=grid_spec,
        out_shape=jax.ShapeDtypeStruct((t, d), F32),
        compiler_params=pltpu.CompilerParams(dimension_semantics=("arbitrary",), vmem_limit_bytes=VMEM_LIMIT),
        name="moe_combine_ln2",
    )(ngroups, dstg, dstg, pos_c, gates, h1, ys, g2, b2)


def kernel(x, meta_tokens, ln_in_g, ln_in_b, rel_bias, w_in, lambda_q1, lambda_k1, lambda_q2, lambda_k2,
           subln_g, a_re, a_im, log_step, b_re, b_im, c_re, c_im, d_skip, w_glu, b_glu, w_out, ln1_g, ln1_b,
           w_router, b_router, w_gate, b_gate, w_up, b_up, w_down, b_down, ln2_g, ln2_b):
    bsz, seq, dm = x.shape
    assert seq % 512 == 0 and w_in.shape[0] == DEPTH == 1
    layer = 0
    row2 = lambda v: v.astype(F32).reshape(1, -1)

    w_in_b = w_in[layer].astype(BF16)
    gi, bi = row2(ln_in_g), row2(ln_in_b)
    q, k, vt, u = _inproj(x, gi, bi, w_in_b, tm=512, kv_tile=KV_TILE)
    meta = jnp.zeros((1, META_PAD, dm), x.dtype).at[0, :N_META].set(meta_tokens.astype(x.dtype))
    _, k_meta, vt_meta, u_meta = _inproj(meta, gi, bi, w_in_b, tm=META_PAD, kv_tile=META_PAD)

    lam_init = 0.8 - 0.6 * math.exp(-0.3 * layer)
    lam = (jnp.exp(jnp.sum(lambda_q1[layer].astype(F32) * lambda_k1[layer].astype(F32)))
           - jnp.exp(jnp.sum(lambda_q2[layer].astype(F32) * lambda_k2[layer].astype(F32))) + lam_init)
    att = _attention(lam.reshape(1), q, k, vt, k_meta[0], vt_meta[0, :, 0], _near_bias(rel_bias),
                     subln_g[layer].astype(F32).reshape(HEAD_W, 1), lam_init=lam_init)

    kpad, pt, qt, a1, a2 = _ssm_tables(a_re[layer], a_im[layer], log_step[layer], b_re[layer], b_im[layer],
                                     c_re[layer], c_im[layer], d_skip[layer])
    n_chunks = seq // SSM_T
    assert (n_chunks * bsz) % 16 == 0
    ug = jnp.transpose(u.reshape(bsz, n_chunks, SSM_T, SSM_G, SSM_CG), (3, 1, 0, 2, 4))
    ug = ug.reshape(SSM_G, n_chunks * bsz, SSM_T * SSM_CG)
    lead = jnp.zeros((8, N_META, SSM_W), BF16).at[0].set(u_meta[0, :N_META])
    ug_meta = jnp.transpose(lead.reshape(8, N_META, SSM_G, SSM_CG), (2, 0, 1, 3)).reshape(SSM_G, 8, N_META * SSM_CG)
    yg = _ssm(ug, ug_meta, kpad, pt, qt, a1, a2, n_chunks=n_chunks, bsz=bsz)
    y_ssm = jnp.transpose(yg.reshape(SSM_G, n_chunks, bsz, SSM_T, SSM_CG), (2, 1, 3, 0, 4))
    y_ssm = y_ssm.reshape(bsz, seq, SSM_W)

    t = bsz * seq
    h1, idx_t, gate_t, rank_t, cnt = _mix(
        x.reshape(t, dm), gi, bi, att.reshape(t, ATT_W), y_ssm.reshape(t, SSM_W),
        w_glu[layer].astype(BF16), row2(b_glu[layer]), w_out[layer].astype(BF16),
        row2(ln1_g[layer]), row2(ln1_b[layer]),
        jnp.transpose(w_router[layer].astype(F32)), b_router[layer].astype(F32).reshape(N_EXPERTS, 1))

    n_tiles = t // ROW_TILE
    experts = jnp.arange(N_EXPERTS, dtype=I32)
    tiles = jnp.arange(n_tiles, dtype=I32)
    run = (cnt[:, :, 0].astype(I32) + GROUP - 1) // GROUP * GROUP
    run_off = jnp.sum(jnp.where((experts[:, None] < experts[None, :])[None], run[:, :, None], 0), axis=1)
    run_before = jnp.sum(jnp.where((tiles[:, None] < tiles[None, :])[:, :, None], run[:, None, :], 0), axis=0)
    counts = jnp.sum(run, axis=0)
    padded = (counts + MOE_BLOCK - 1) // MOE_BLOCK * MOE_BLOCK
    padded_end = jnp.sum(jnp.where(experts[:, None] <= experts[None, :], padded[:, None], 0), axis=0)
    padded_start = padded_end - padded
    n_blocks = (t * TOP_K + n_tiles * N_EXPERTS * (GROUP - 1)) // MOE_BLOCK + N_EXPERTS
    block_row0 = jnp.arange(n_blocks, dtype=I32) * MOE_BLOCK
    block_expert = jnp.minimum(jnp.sum((padded_end[None, :] <= block_row0[:, None]).astype(I32), axis=1),
                               N_EXPERTS - 1)
    n_used = (padded_end[-1:] // MOE_BLOCK).astype(I32)
    ngroups = jnp.sum(run, axis=1) // GROUP
    local_row = jnp.arange(SORT_GROUPS, dtype=I32) * GROUP
    owner = jnp.minimum(jnp.sum(((run_off + run)[:, None, :] <= local_row[None, :, None]).astype(I32), axis=-1),
                        N_EXPERTS - 1)
    shift = (padded_start[None, :] + run_before - run_off) // GROUP
    dstg = jnp.sum(jnp.where(owner[..., None] == experts, shift[:, None, :], 0), axis=-1) + local_row // GROUP
    dstg = jnp.where(local_row[None, :] // GROUP < ngroups[:, None], dstg, 0).reshape(n_tiles, 1, SORT_GROUPS)
    run_off_tok = jnp.repeat(run_off, ROW_TILE, axis=0)
    pos_t = jnp.sum(jnp.where(idx_t[..., None] == experts, run_off_tok[None], 0), axis=-1) + rank_t

    xs = _dispatch(ngroups, ((padded_start + counts) // GROUP).astype(I32),
                   ((padded - counts) // GROUP).astype(I32), n_used, dstg, pos_t, h1, n_blocks * MOE_BLOCK)
    b3 = lambda v: v.astype(F32)[:, None, :]
    ys = _experts(block_expert, n_used, xs, w_gate[layer], b3(b_gate[layer]), w_up[layer], b3(b_up[layer]),
                  w_down[layer], b3(b_down[layer]))
    out = _combine((ngroups + 1) // 2 * 2, dstg, jnp.transpose(pos_t), jnp.transpose(gate_t), h1, ys,
                   row2(ln2_g[layer]), row2(ln2_b[layer]))
    return out.reshape(bsz, seq, dm)
```

```python
import functools
import math

import jax
import jax.numpy as jnp
import numpy as np
from jax import lax
from jax.experimental import pallas as pl
from jax.experimental.pallas import tpu as pltpu

F32 = jnp.float32
BF16 = jnp.bfloat16
I32 = jnp.int32

DEPTH = 1
N_META = 16
CHUNK = 64
ATT_HEADS = 4
HEAD_DIM = 64
HEAD_W = 2 * HEAD_DIM
ATT_W = ATT_HEADS * HEAD_W
SSM_W = 512
SSM_CG = 16
SSM_G = SSM_W // SSM_CG
SSM_N = 64
N_BUCKETS = 32
MAX_DISTANCE = 128
N_EXPERTS = 32
TOP_K = 4
SWIGLU_LIMIT = 7.0
SWIGLU_ALPHA = 1.702
LN_EPS = 1e-5
NEG_INF = -1e30
DEEPNORM_ALPHA = (2.0 * DEPTH) ** 0.25
LOG2E = 1.4426950408889634

Q_TILE = 512
KV_TILE = 512
LANE_CHUNK = 256
V_ONES = 16
V_ROWS = HEAD_W + V_ONES
META_PAD = 128
SSM_T = 64
MIX_TILE = 512
ROW_TILE = 256
MOE_BLOCK = 512
GROUP = 8
SORT_ROWS = 1280
SORT_GROUPS = SORT_ROWS // GROUP
assert SORT_ROWS >= ROW_TILE * TOP_K + N_EXPERTS * (GROUP - 1)
VMEM_LIMIT = 56 * 1024 * 1024


def _layer_norm(x, g, b):
    mu = jnp.mean(x, axis=-1, keepdims=True)
    xc = x - mu
    var = jnp.mean(xc * xc, axis=-1, keepdims=True)
    return xc * lax.rsqrt(var + LN_EPS) * g + b


def _dot(a, b):
    return jnp.dot(a, b, preferred_element_type=F32)


def _dot_nt(a, b, precision=None):
    return lax.dot_general(a, b, (((1,), (1,)), ((), ())), precision=precision,
                           preferred_element_type=F32)


def _inproj_kernel(x_ref, g_ref, b_ref, w_ref, q_ref, k_ref, vt_ref, u_ref, *, kv_tile, q_scale):
    h = _layer_norm(x_ref[0], g_ref[...], b_ref[...]).astype(BF16)
    tm = h.shape[0]
    q_ref[0] = (_dot(h, w_ref[:, 0:ATT_W]) * q_scale).astype(BF16)
    k_ref[0] = _dot(h, w_ref[:, ATT_W:2 * ATT_W]).astype(BF16)
    v = _dot(h, w_ref[:, 2 * ATT_W:3 * ATT_W])
    ones = jnp.ones((V_ONES, kv_tile), BF16)
    for hh in range(ATT_HEADS):
        vt = v[:, hh * HEAD_W:(hh + 1) * HEAD_W].T.astype(BF16)
        for j in range(tm // kv_tile):
            vt_ref[0, hh, j, 0:HEAD_W, :] = vt[:, j * kv_tile:(j + 1) * kv_tile]
            vt_ref[0, hh, j, HEAD_W:V_ROWS, :] = ones
    u_ref[0] = _dot(h, w_ref[:, 3 * ATT_W:]).astype(BF16)


def _inproj(x, g, b, w, *, tm, kv_tile):
    bsz, s, d = x.shape
    q_scale = HEAD_DIM ** -0.5 * LOG2E
    n_cols = w.shape[1]
    row = lambda bi, i: (bi, i, 0)
    return pl.pallas_call(
        functools.partial(_inproj_kernel, kv_tile=kv_tile, q_scale=q_scale),
        grid=(bsz, s // tm),
        in_specs=[
            pl.BlockSpec((1, tm, d), row),
            pl.BlockSpec((1, d), lambda bi, i: (0, 0)),
            pl.BlockSpec((1, d), lambda bi, i: (0, 0)),
            pl.BlockSpec((d, n_cols), lambda bi, i: (0, 0)),
        ],
        out_specs=[
            pl.BlockSpec((1, tm, ATT_W), row),
            pl.BlockSpec((1, tm, ATT_W), row),
            pl.BlockSpec((1, ATT_HEADS, tm // kv_tile, V_ROWS, kv_tile), lambda bi, i: (bi, 0, i, 0, 0)),
            pl.BlockSpec((1, tm, SSM_W), row),
        ],
        out_shape=[
            jax.ShapeDtypeStruct((bsz, s, ATT_W), BF16),
            jax.ShapeDtypeStruct((bsz, s, ATT_W), BF16),
            jax.ShapeDtypeStruct((bsz, ATT_HEADS, s // kv_tile, V_ROWS, kv_tile), BF16),
            jax.ShapeDtypeStruct((bsz, s, SSM_W), BF16),
        ],
        compiler_params=pltpu.CompilerParams(
            dimension_semantics=("parallel", "parallel"), vmem_limit_bytes=VMEM_LIMIT),
        name="inproj",
    )(x, g, b, w)


def _attn_kernel(lam_ref, q_ref, k_ref, vt_ref, km_ref, vtm_ref, bias_ref, g_ref, o_ref,
                 m_ref, acc_ref, qz_ref, sa_ref, sb_ref, sn_ref, *, out_scale):
    i = pl.program_id(2)
    qt = q_ref[0].astype(F32).T
    feat = lax.broadcasted_iota(I32, qt.shape, 0)
    qz_ref[:, 0:Q_TILE] = jnp.where(feat < HEAD_DIM, qt, 0.0).astype(BF16)
    qz_ref[:, Q_TILE:] = jnp.where(feat >= HEAD_DIM, qt, 0.0).astype(BF16)

    m_ref[...] = jnp.full(m_ref.shape, NEG_INF, F32)
    acc_ref[...] = jnp.zeros(acc_ref.shape, F32)

    def scores(kt):
        return _dot(kt, qz_ref[...])

    def k_tile(t):
        return k_ref[0, pl.ds(pl.multiple_of(t * KV_TILE, KV_TILE), KV_TILE), :]

    chunks = [slice(c, c + LANE_CHUNK) for c in range(0, 2 * Q_TILE, LANE_CHUNK)]

    def absorb(s_ref, vt, bias_rows=None):
        for cols in chunks:
            s = s_ref[:, cols]
            if bias_rows is not None:
                s = s + bias_ref[0, 0, bias_rows, cols]
            m_old = m_ref[:, cols]
            m_new = jnp.maximum(m_old, jnp.max(s, axis=0, keepdims=True))
            alpha = jnp.exp2(m_old - m_new)
            p = jnp.exp2(s - m_new).astype(BF16)
            acc_ref[:, cols] = acc_ref[:, cols] * alpha + _dot(vt, p)
            m_ref[:, cols] = m_new

    n_far = jnp.maximum(i - 1, 0)
    peel = n_far % 2

    sn_ref[...] = scores(km_ref[...])
    sa_ref[...] = scores(k_tile(0))
    absorb(sn_ref, vtm_ref[0], slice(2 * KV_TILE, 2 * KV_TILE + META_PAD))

    @pl.when(peel == 1)
    def _():
        absorb(sa_ref, vt_ref[0, 0, 0])
        sa_ref[...] = scores(k_tile(1))

    def far_pair(j, carry):
        t0 = peel + 2 * j
        sb_ref[...] = scores(k_tile(t0 + 1))
        absorb(sa_ref, vt_ref[0, 0, t0])
        sa_ref[...] = scores(k_tile(t0 + 2))
        absorb(sb_ref, vt_ref[0, 0, t0 + 1])
        return carry

    lax.fori_loop(0, n_far // 2, far_pair, 0)

    sb_ref[...] = scores(k_tile(i))
    absorb(sa_ref, vt_ref[0, 0, n_far], slice(0, KV_TILE))
    absorb(sb_ref, vt_ref[0, 0, i], slice(KV_TILE, 2 * KV_TILE))

    acc = acc_ref[...]
    lam = lam_ref[0]
    o1 = acc[0:HEAD_W, 0:Q_TILE] / acc[HEAD_W:HEAD_W + 1, 0:Q_TILE]
    o2 = acc[0:HEAD_W, Q_TILE:] / acc[HEAD_W:HEAD_W + 1, Q_TILE:]
    o = o1 - lam * o2
    ms = jnp.mean(o * o, axis=0, keepdims=True)
    o = o * lax.rsqrt(ms + LN_EPS) * g_ref[...] * out_scale
    o_ref[0] = o.T.astype(o_ref.dtype)


def _attention(lam, q, k, vt, k_meta, vt_meta, bias, subln_g, *, lam_init):
    bsz, s, _ = q.shape
    nq = s // Q_TILE
    n_near = META_PAD + 2 * KV_TILE
    grid_spec = pltpu.PrefetchScalarGridSpec(
        num_scalar_prefetch=1,
        grid=(bsz, ATT_HEADS, nq),
        in_specs=[
            pl.BlockSpec((1, Q_TILE, HEAD_W), lambda b, h, i, lam: (b, i, h)),
            pl.BlockSpec((1, s, HEAD_W), lambda b, h, i, lam: (b, 0, h)),
            pl.BlockSpec((1, 1, s // KV_TILE, V_ROWS, KV_TILE), lambda b, h, i, lam: (b, h, 0, 0, 0)),
            pl.BlockSpec((META_PAD, HEAD_W), lambda b, h, i, lam: (0, h)),
            pl.BlockSpec((1, V_ROWS, META_PAD), lambda b, h, i, lam: (h, 0, 0)),
            pl.BlockSpec((1, 1, n_near, 2 * Q_TILE), lambda b, h, i, lam: (h, jnp.minimum(i, 1), 0, 0)),
            pl.BlockSpec((HEAD_W, 1), lambda b, h, i, lam: (0, 0)),
        ],
        out_specs=pl.BlockSpec((1, Q_TILE, HEAD_W), lambda b, h, i, lam: (b, i, h)),
        scratch_shapes=[
            pltpu.VMEM((1, 2 * Q_TILE), F32), pltpu.VMEM((V_ROWS, 2 * Q_TILE), F32),
            pltpu.VMEM((HEAD_W, 2 * Q_TILE), BF16),
            pltpu.VMEM((KV_TILE, 2 * Q_TILE), F32), pltpu.VMEM((KV_TILE, 2 * Q_TILE), F32),
            pltpu.VMEM((META_PAD, 2 * Q_TILE), F32),
        ],
    )
    return pl.pallas_call(
        functools.partial(_attn_kernel, out_scale=1.0 - lam_init),
        grid_spec=grid_spec,
        out_shape=jax.ShapeDtypeStruct((bsz, s, ATT_W), BF16),
        compiler_params=pltpu.CompilerParams(
            dimension_semantics=("parallel", "parallel", "arbitrary"), vmem_limit_bytes=VMEM_LIMIT),
        name="diff_attention",
    )(lam, q, k, vt, k_meta, vt_meta, bias, subln_g)


def _t5_bucket(rel):
    nb = N_BUCKETS // 2
    max_exact = nb // 2
    ret = jnp.where(rel > 0, nb, 0)
    n = jnp.abs(rel)
    n_f = jnp.maximum(n, 1).astype(F32)
    large = max_exact + (jnp.log(n_f / max_exact) / math.log(MAX_DISTANCE / max_exact)
                         * (nb - max_exact)).astype(I32)
    large = jnp.minimum(large, nb - 1)
    return ret + jnp.where(n < max_exact, n, large)


def _near_bias(rel_bias):
    table = rel_bias.astype(F32)
    far = table[N_BUCKETS // 2 - 1]
    c = jnp.arange(Q_TILE, dtype=I32)[None, :]
    r = jnp.arange(KV_TILE, dtype=I32)[:, None]

    shifted = jnp.transpose(table - far) * LOG2E

    def bias_of(offset, rows, cols=Q_TILE):
        rel = jnp.arange(rows, dtype=I32)[:, None] - c[:, :cols] + offset
        hot = _t5_bucket(rel)[None, :, :, None] == jnp.arange(N_BUCKETS, dtype=I32)
        return jnp.sum(jnp.where(hot, shifted[:, None, None, :], 0.0), axis=-1)

    own = jnp.where((r // CHUNK <= c // CHUNK)[None], bias_of(0, KV_TILE), NEG_INF)
    near = MAX_DISTANCE
    prev = jnp.zeros((ATT_HEADS, KV_TILE, Q_TILE), F32).at[:, KV_TILE - near:, :near].set(bias_of(-near, near, near))
    rm = jnp.arange(META_PAD, dtype=I32)[:, None]
    meta_ok = (rm < N_META)[None]
    meta0 = jnp.where(meta_ok, jnp.pad(bias_of(-N_META, N_META), ((0, 0), (0, META_PAD - N_META), (0, 0))), NEG_INF)
    meta1 = jnp.where(meta_ok, jnp.zeros((ATT_HEADS, META_PAD, Q_TILE), F32), NEG_INF)
    assert N_META + Q_TILE - (N_META - 1) >= MAX_DISTANCE and KV_TILE >= near and Q_TILE >= near
    v0 = jnp.concatenate([jnp.full_like(prev, NEG_INF), own, meta0], axis=1)
    v1 = jnp.concatenate([prev, own, meta1], axis=1)
    both = jnp.stack([v0, v1], axis=1)
    return jnp.concatenate([both, both], axis=-1)


def _ssm_tables(a_re, a_im, log_step, b_re, b_im, c_re, c_im, d_skip):
    hi = lax.Precision.HIGHEST
    t_len = SSM_T
    step = jnp.exp(log_step.astype(F32))[:, None]
    ar = jnp.minimum(a_re.astype(F32), -1e-4)
    ai = a_im.astype(F32)
    mag = jnp.exp(step * ar)
    ph = step * ai
    abar_re = mag * jnp.cos(ph)
    abar_im = mag * jnp.sin(ph)
    den = ar * ar + ai * ai
    e_re = abar_re - 1.0
    e_im = abar_im
    f_re = (e_re * ar + e_im * ai) / den
    f_im = (e_im * ar - e_re * ai) / den
    br = b_re.astype(F32)
    bi = b_im.astype(F32)
    bb_re = f_re[..., None] * br - f_im[..., None] * bi
    bb_im = f_re[..., None] * bi + f_im[..., None] * br
    tau = jnp.arange(t_len + 1, dtype=F32)[None, :, None]
    pmag = jnp.exp(tau * (step * ar)[:, None, :])
    pph = tau * ph[:, None, :]
    pw_re = pmag * jnp.cos(pph)
    pw_im = pmag * jnp.sin(pph)
    bbt_re = jnp.swapaxes(bb_re, 1, 2)[:, None]
    bbt_im = jnp.swapaxes(bb_im, 1, 2)[:, None]

    def times_bbar(p_re, p_im):
        return (p_re[:, :, None, :] * bbt_re - p_im[:, :, None, :] * bbt_im,
                p_re[:, :, None, :] * bbt_im + p_im[:, :, None, :] * bbt_re)

    cr = c_re.astype(F32)
    ci = c_im.astype(F32)
    bi_re = jnp.swapaxes(bb_re, 1, 2)[:, :, None, :]
    bi_im = jnp.swapaxes(bb_im, 1, 2)[:, :, None, :]
    cb_re = cr[:, None] * bi_re - ci[:, None] * bi_im
    cb_im = cr[:, None] * bi_im + ci[:, None] * bi_re
    kern = jnp.einsum('gikm,gtm->gitk', jnp.concatenate([cb_re, -cb_im], axis=-1),
                      jnp.concatenate([pw_re[:, :t_len], pw_im[:, :t_len]], axis=-1), precision=hi)
    skip = d_skip.astype(F32).reshape(SSM_G, SSM_CG)
    kern = kern.at[:, :, 0, :].add(skip[:, None, :] * jnp.eye(SSM_CG, dtype=F32))
    kflat = kern.reshape(SSM_G, SSM_CG, t_len * SSM_CG)
    kpad = jnp.concatenate([jnp.zeros_like(kflat), kflat], axis=-1)
    back = jnp.arange(t_len - 1, -1, -1, dtype=F32)[None, :, None]
    bmag = jnp.exp(back * (step * ar)[:, None, :])
    bph = back * ph[:, None, :]
    pt_re, pt_im = times_bbar(bmag * jnp.cos(bph), bmag * jnp.sin(bph))
    pt = jnp.concatenate([pt_re, pt_im], axis=-1).reshape(SSM_G, t_len * SSM_CG, 2 * SSM_N)
    up_re = jnp.swapaxes(pw_re[:, 1:], 1, 2)[..., None]
    up_im = jnp.swapaxes(pw_im[:, 1:], 1, 2)[..., None]
    crt = jnp.swapaxes(cr, 1, 2)[:, :, None, :]
    cit = jnp.swapaxes(ci, 1, 2)[:, :, None, :]
    q_re = crt * up_re - cit * up_im
    q_im = crt * up_im + cit * up_re
    qt = jnp.concatenate([q_re, -q_im], axis=1).reshape(SSM_G, 2 * SSM_N, t_len * SSM_CG)
    at_re = pw_re[:, t_len]
    at_im = pw_im[:, t_len]
    a1 = jnp.concatenate([at_re, at_re], axis=-1)
    a2 = jnp.concatenate([-at_im, at_im], axis=-1)
    return kpad, pt.astype(BF16), qt.astype(BF16), a1, a2


def _ssm_state_kernel(u_ref, pt_ref, s_ref):
    s_ref[0] = _dot(u_ref[0], pt_ref[0])


def _ssm_scan_kernel(s_ref, x0_ref, a1_ref, a2_ref, x_ref, *, n_chunks, bsz):
    a1 = a1_ref[...][:, None, :]
    a2 = a2_ref[...][:, None, :]

    def body(c, x):
        rows = pl.ds(c * bsz, bsz)
        x_ref[:, rows, :] = x
        return a1 * x + a2 * pltpu.roll(x, SSM_N, 2) + s_ref[:, rows, :]

    x0 = jnp.broadcast_to(x0_ref[:, 0:1, :], (SSM_G, bsz, 2 * SSM_N))
    lax.fori_loop(0, n_chunks, body, x0)


def _ssm_out_kernel(u_ref, x_ref, kp_ref, qt_ref, y_ref, mt_ref):
    kp = kp_ref[0]
    width = SSM_T * SSM_CG
    lane_tile = 128
    for sub in range(0, lane_tile, SSM_CG):
        shifted = kp if sub == 0 else pltpu.roll(kp, 2 * width - sub, 1)
        for s in range(SSM_T):
            off = (SSM_T - s) * SSM_CG
            if off % lane_tile == sub:
                base = off - sub
                mt_ref[s * SSM_CG:(s + 1) * SSM_CG, :] = shifted[:, base:base + width].astype(BF16)
    x = x_ref[0]
    x_hi = x.astype(BF16)
    x_lo = (x - x_hi.astype(F32)).astype(BF16)
    y = _dot(u_ref[0], mt_ref[...]) + _dot(x_hi, qt_ref[0]) + _dot(x_lo, qt_ref[0])
    y_ref[0] = y.astype(y_ref.dtype)


def _ssm_chunk_state(ug, pt):
    g, r, w = ug.shape
    n2 = 2 * SSM_N
    per_g = lambda gi: (gi, 0, 0)
    return pl.pallas_call(
        _ssm_state_kernel,
        grid=(g,),
        in_specs=[pl.BlockSpec((1, r, w), per_g), pl.BlockSpec((1, w, n2), per_g)],
        out_specs=pl.BlockSpec((1, r, n2), per_g),
        out_shape=jax.ShapeDtypeStruct((g, r, n2), F32),
        compiler_params=pltpu.CompilerParams(dimension_semantics=("parallel",), vmem_limit_bytes=VMEM_LIMIT),
        name="ssm_chunk_state",
    )(ug, pt)


def _ssm(ug, ug_meta, kpad, pt, qt, a1, a2, *, n_chunks, bsz):
    g, r, w = ug.shape
    n2 = 2 * SSM_N
    per_g = lambda gi: (gi, 0, 0)
    params = pltpu.CompilerParams(dimension_semantics=("parallel",), vmem_limit_bytes=VMEM_LIMIT)
    s = _ssm_chunk_state(ug, pt)
    x0 = _ssm_chunk_state(ug_meta, pt[:, w - ug_meta.shape[2]:, :])
    x = pl.pallas_call(
        functools.partial(_ssm_scan_kernel, n_chunks=n_chunks, bsz=bsz),
        out_shape=jax.ShapeDtypeStruct((g, r, n2), F32),
        compiler_params=pltpu.CompilerParams(vmem_limit_bytes=VMEM_LIMIT),
        name="ssm_chunk_scan",
    )(s, x0, a1, a2)
    return pl.pallas_call(
        _ssm_out_kernel,
        grid=(g,),
        in_specs=[pl.BlockSpec((1, r, w), per_g), pl.BlockSpec((1, r, n2), per_g),
                  pl.BlockSpec((1, SSM_CG, 2 * w), per_g), pl.BlockSpec((1, n2, w), per_g)],
        out_specs=pl.BlockSpec((1, r, w), per_g),
        out_shape=jax.ShapeDtypeStruct((g, r, w), BF16),
        scratch_shapes=[pltpu.VMEM((w, w), BF16)],
        compiler_params=params,
        name="ssm_output",
    )(ug, x, kpad, qt)


def _mix_kernel(x_ref, gi_ref, bi_ref, att_ref, y_ref, wglu_ref, bglu_ref, wout_ref, g1_ref, b1_ref,
                wr_ref, br_ref, h1_ref, idx_ref, gate_ref, rank_ref, cnt_ref):
    h0 = _layer_norm(x_ref[...], gi_ref[...], bi_ref[...])
    y = y_ref[...].astype(F32)
    y = y * (0.5 * (1.0 + jnp.tanh(math.sqrt(2.0 / math.pi) * (y + 0.044715 * (y * y * y)))))
    y = y * jax.nn.sigmoid(_dot(y.astype(BF16), wglu_ref[...]) + bglu_ref[...])
    mix = _dot(att_ref[...], wout_ref[0:ATT_W, :]) + _dot(y.astype(BF16), wout_ref[ATT_W:, :])
    h1 = _layer_norm(DEEPNORM_ALPHA * h0 + mix, g1_ref[...], b1_ref[...])
    h1_ref[...] = h1

    logits = _dot_nt(wr_ref[...], h1, precision=lax.Precision.HIGHEST) + br_ref[...]
    tm = logits.shape[1]
    eidx = lax.broadcasted_iota(I32, logits.shape, 0)
    vals, hots = [], []
    rest = logits
    for _ in range(TOP_K):
        mx = jnp.max(rest, axis=0, keepdims=True)
        first = jnp.min(jnp.where(rest == mx, eidx, N_EXPERTS), axis=0, keepdims=True)
        hot = eidx == first
        vals.append(mx)
        hots.append(hot)
        rest = jnp.where(hot, -jnp.inf, rest)
    exps = [jnp.exp(v - vals[0]) for v in vals]
    denom = exps[0] + exps[1] + exps[2] + exps[3]
    gate_ref[...] = jnp.concatenate([e / denom for e in exps], axis=0)
    idx_ref[...] = jnp.concatenate(
        [jnp.sum(jnp.where(h, eidx, 0), axis=0, keepdims=True) for h in hots], axis=0)

    hot_all = (hots[0] | hots[1] | hots[2] | hots[3]).astype(F32)
    sub = ROW_TILE
    tri = (lax.broadcasted_iota(I32, (sub, sub), 0) < lax.broadcasted_iota(I32, (sub, sub), 1)).astype(BF16)
    for part in range(tm // sub):
        cols = slice(part * sub, (part + 1) * sub)
        before = _dot(hot_all[:, cols].astype(BF16), tri)
        rank_ref[:, cols] = jnp.concatenate(
            [jnp.sum(jnp.where(h[:, cols], before, 0.0), axis=0, keepdims=True) for h in hots], axis=0).astype(I32)
        cnt_ref[part] = jnp.broadcast_to(jnp.sum(hot_all[:, cols], axis=1, keepdims=True), cnt_ref.shape[1:])


def _mix(x2, gi, bi, att, y, wglu, bglu, wout, g1, b1, wr_t, br):
    t, d = x2.shape
    tm = MIX_TILE
    row = lambda i: (i, 0)
    col = lambda i: (0, i)
    const = lambda i: (0, 0)
    return pl.pallas_call(
        _mix_kernel,
        grid=(t // tm,),
        in_specs=[
            pl.BlockSpec((tm, d), row), pl.BlockSpec((1, d), const), pl.BlockSpec((1, d), const),
            pl.BlockSpec((tm, ATT_W), row), pl.BlockSpec((tm, SSM_W), row),
            pl.BlockSpec((SSM_W, SSM_W), const), pl.BlockSpec((1, SSM_W), const),
            pl.BlockSpec((d, d), const), pl.BlockSpec((1, d), const), pl.BlockSpec((1, d), const),
            pl.BlockSpec((N_EXPERTS, d), const), pl.BlockSpec((N_EXPERTS, 1), const),
        ],
        out_specs=[
            pl.BlockSpec((tm, d), row),
            pl.BlockSpec((TOP_K, tm), col), pl.BlockSpec((TOP_K, tm), col), pl.BlockSpec((TOP_K, tm), col),
            pl.BlockSpec((tm // ROW_TILE, N_EXPERTS, 128), lambda i: (i, 0, 0)),
        ],
        out_shape=[
            jax.ShapeDtypeStruct((t, d), F32),
            jax.ShapeDtypeStruct((TOP_K, t), I32), jax.ShapeDtypeStruct((TOP_K, t), F32),
            jax.ShapeDtypeStruct((TOP_K, t), I32),
            jax.ShapeDtypeStruct((t // ROW_TILE, N_EXPERTS, 128), F32),
        ],
        compiler_params=pltpu.CompilerParams(dimension_semantics=("parallel",), vmem_limit_bytes=VMEM_LIMIT),
        name="mix_ln1_router",
    )(x2, gi, bi, att, y, wglu, bglu, wout, g1, b1, wr_t, br)


_HI_BITS = -65536


def _pack_halves(x):
    half = x.shape[1] // 2
    lo = lax.bitcast_convert_type(x[:, :half], I32)
    hi = lax.bitcast_convert_type(x[:, half:], I32)
    return lax.shift_right_logical(lo, 16) | (hi & _HI_BITS)


def _unpack_halves(w):
    lo = lax.bitcast_convert_type(lax.shift_left(w, 16), F32)
    hi = lax.bitcast_convert_type(w & _HI_BITS, F32)
    return lo.astype(BF16), hi.astype(BF16)


def _group_copy(src, src_group, dst, dst_group, sem):
    return pltpu.make_async_copy(src.at[pl.ds(pl.multiple_of(src_group * GROUP, GROUP), GROUP), :],
                                 dst.at[pl.ds(pl.multiple_of(dst_group * GROUP, GROUP), GROUP), :], sem)


def _dispatch_kernel(ngroups_ref, pad_start_ref, pad_count_ref, n_used_ref, dstg_ref, pos_ref, h1_ref, xs_hbm,
                     sorted_ref, zero_ref, sems, zsem):
    step = pl.program_id(0)
    n_steps = pl.num_programs(0)
    slot = step % 2
    n_blocks = xs_hbm.shape[0] // MOE_BLOCK

    def zero_block(blk):
        return pltpu.make_async_copy(zero_ref, xs_hbm.at[pl.ds(blk * MOE_BLOCK, MOE_BLOCK), :], zsem)

    @pl.when(step == 0)
    def _():
        zero_ref[...] = jnp.zeros(zero_ref.shape, I32)
        for e in range(N_EXPERTS):
            start = pad_start_ref[e]
            count = pad_count_ref[e]

            def fill(j, c):
                _group_copy(zero_ref, 0, xs_hbm, start + j, zsem).start()
                return c

            def drain(j, c):
                _group_copy(zero_ref, 0, xs_hbm, start, zsem).wait()
                return c

            lax.fori_loop(0, count, fill, 0)
            lax.fori_loop(0, count, drain, 0)

        def fill_block(blk, c):
            zero_block(blk).start()
            return c

        def drain_block(blk, c):
            zero_block(blk).wait()
            return c

        lax.fori_loop(n_used_ref[0], n_blocks, fill_block, 0)
        lax.fori_loop(n_used_ref[0], n_blocks, drain_block, 0)

    pos = pos_ref[...]
    rows = lax.broadcasted_iota(I32, (SORT_ROWS, pos.shape[1]), 0)
    place = rows == pos[0:1]
    for k in range(1, TOP_K):
        place = place | (rows == pos[k:k + 1])
    sorted_ref[slot] = _pack_halves(_dot(place.astype(BF16), h1_ref[...].astype(BF16)))

    def issue(j, c):
        _group_copy(sorted_ref.at[slot], j, xs_hbm, dstg_ref[0, 0, j], sems.at[slot]).start()
        return c

    lax.fori_loop(0, ngroups_ref[step], issue, 0)

    def wait_tile(s, count):
        def drain(j, c):
            _group_copy(sorted_ref.at[s], 0, xs_hbm, 0, sems.at[s]).wait()
            return c
        lax.fori_loop(0, count, drain, 0)

    @pl.when(step > 0)
    def _():
        wait_tile(1 - slot, ngroups_ref[jnp.maximum(step - 1, 0)])

    @pl.when(step == n_steps - 1)
    def _():
        wait_tile(slot, ngroups_ref[step])


def _dispatch(ngroups, pad_start, pad_count, n_used, dstg, pos_t, h1, n_rows):
    t, d = h1.shape
    tm = ROW_TILE
    grid_spec = pltpu.PrefetchScalarGridSpec(
        num_scalar_prefetch=4,
        grid=(t // tm,),
        in_specs=[
            pl.BlockSpec((1, 1, SORT_GROUPS), lambda i, *_: (i, 0, 0), memory_space=pltpu.SMEM),
            pl.BlockSpec((TOP_K, tm), lambda i, *_: (0, i)),
            pl.BlockSpec((tm, d), lambda i, *_: (i, 0)),
        ],
        out_specs=pl.BlockSpec(memory_space=pl.ANY),
        scratch_shapes=[pltpu.VMEM((2, SORT_ROWS, d // 2), I32), pltpu.VMEM((MOE_BLOCK, d // 2), I32),
                        pltpu.SemaphoreType.DMA((2,)), pltpu.SemaphoreType.DMA],
    )
    return pl.pallas_call(
        _dispatch_kernel,
        grid_spec=grid_spec,
        out_shape=jax.ShapeDtypeStruct((n_rows, d // 2), I32),
        compiler_params=pltpu.CompilerParams(dimension_semantics=("arbitrary",), vmem_limit_bytes=VMEM_LIMIT,
                                             has_side_effects=True),
        name="moe_dispatch",
    )(ngroups, pad_start, pad_count, n_used, dstg, pos_t, h1)


def _expert_kernel(be_ref, nb_ref, x_ref, wg_ref, bg_ref, wu_ref, bu_ref, wd_ref, bd_ref, y_ref,
                   wg_b, wu_b, wd_b):
    i = pl.program_id(0)

    @pl.when(i < nb_ref[0])
    def _():
        @pl.when((i == 0) | (be_ref[i] != be_ref[jnp.maximum(i - 1, 0)]))
        def _():
            wg_b[...] = wg_ref[0].astype(BF16)
            wu_b[...] = wu_ref[0].astype(BF16)
            wd_b[...] = wd_ref[0].astype(BF16)

        x_lo, x_hi = _unpack_halves(x_ref[...])
        half = x_lo.shape[1]
        gate = _dot(x_lo, wg_b[0:half, :]) + _dot(x_hi, wg_b[half:, :]) + bg_ref[0]
        up = _dot(x_lo, wu_b[0:half, :]) + _dot(x_hi, wu_b[half:, :]) + bu_ref[0]
        gate = jnp.minimum(gate, SWIGLU_LIMIT)
        up = jnp.clip(up, -SWIGLU_LIMIT, SWIGLU_LIMIT)
        act = (up + 1.0) * gate * jax.nn.sigmoid(gate * SWIGLU_ALPHA)
        y = _dot(act.astype(BF16), wd_b[...]) + bd_ref[0]
        y_ref[...] = _pack_halves(y.astype(BF16).astype(F32))

    @pl.when(i >= nb_ref[0])
    def _():
        y_ref[...] = jnp.zeros(y_ref.shape, I32)


def _experts(block_expert, n_used, xs, wg, bg, wu, bu, wd, bd):
    n_rows, packed = xs.shape
    d, dff = wg.shape[1], wg.shape[2]
    assert packed * 2 == d and wd.shape[2] == d
    nb = n_rows // MOE_BLOCK
    blk = lambda i, be, nu: (jnp.minimum(i, nu[0] - 1), 0)
    wsel = lambda i, be, nu: (be[jnp.minimum(i, nu[0] - 1)], 0, 0)
    grid_spec = pltpu.PrefetchScalarGridSpec(
        num_scalar_prefetch=2,
        grid=(nb,),
        in_specs=[
            pl.BlockSpec((MOE_BLOCK, packed), blk),
            pl.BlockSpec((1, d, dff), wsel), pl.BlockSpec((1, 1, dff), wsel),
            pl.BlockSpec((1, d, dff), wsel), pl.BlockSpec((1, 1, dff), wsel),
            pl.BlockSpec((1, dff, d), wsel), pl.BlockSpec((1, 1, d), wsel),
        ],
        out_specs=pl.BlockSpec((MOE_BLOCK, packed), lambda i, be, nu: (i, 0)),
        scratch_shapes=[pltpu.VMEM((d, dff), BF16), pltpu.VMEM((d, dff), BF16), pltpu.VMEM((dff, d), BF16)],
    )
    return pl.pallas_call(
        _expert_kernel,
        grid_spec=grid_spec,
        out_shape=jax.ShapeDtypeStruct((n_rows, packed), I32),
        compiler_params=pltpu.CompilerParams(dimension_semantics=("arbitrary",), vmem_limit_bytes=VMEM_LIMIT),
        name="moe_experts",
    )(block_expert, n_used, xs, wg, bg, wu, bu, wd, bd)


def _combine_kernel(ngroups_ref, dstg_ref, dstg_next_ref, pos_ref, gate_ref, h1_ref, ys_hbm, g2_ref, b2_ref,
                    o_ref, buf, sems):
    step = pl.program_id(0)
    n_steps = pl.num_programs(0)
    slot = step % 2

    def fetch(dref, s, j):
        return _group_copy(ys_hbm, dref[0, 0, j], buf.at[s], j, sems.at[s])

    def issue(dref, s, count):
        def body(jj, c):
            fetch(dref, s, 2 * jj).start()
            fetch(dref, s, 2 * jj + 1).start()
            return c
        lax.fori_loop(0, count // 2, body, 0)

    @pl.when(step == 0)
    def _():
        buf[...] = jnp.zeros(buf.shape, I32)
        issue(dstg_ref, 0, ngroups_ref[0])

    @pl.when(step + 1 < n_steps)
    def _():
        issue(dstg_next_ref, 1 - slot, ngroups_ref[jnp.minimum(step + 1, n_steps - 1)])

    def drain(jj, c):
        pltpu.make_async_copy(ys_hbm.at[pl.ds(0, 2 * GROUP), :], buf.at[slot, pl.ds(0, 2 * GROUP), :],
                              sems.at[slot]).wait()
        return c

    lax.fori_loop(0, ngroups_ref[step] // 2, drain, 0)

    pos = pos_ref[...]
    gate = gate_ref[...]
    lanes = lax.broadcasted_iota(I32, (pos.shape[0], SORT_ROWS), 1)
    sel = jnp.where(lanes == pos[:, 0:1], gate[:, 0:1], 0.0)
    for k in range(1, TOP_K):
        sel = sel + jnp.where(lanes == pos[:, k:k + 1], gate[:, k:k + 1], 0.0)
    sel_hi = sel.astype(BF16)
    sel_lo = (sel - sel_hi.astype(F32)).astype(BF16)
    y_lo, y_hi = _unpack_halves(buf[slot])
    ffn = jnp.concatenate([_dot(sel_hi, y_lo) + _dot(sel_lo, y_lo), _dot(sel_hi, y_hi) + _dot(sel_lo, y_hi)], axis=1)
    o_ref[...] = _layer_norm(DEEPNORM_ALPHA * h1_ref[...] + ffn, g2_ref[...], b2_ref[...])


def _combine(ngroups, dstg, pos_c, gates, h1, ys, g2, b2):
    t, d = h1.shape
    tm = ROW_TILE
    n_steps = t // tm
    row = lambda i, ng: (i, 0)
    const = lambda i, ng: (0, 0)
    grid_spec = pltpu.PrefetchScalarGridSpec(
        num_scalar_prefetch=1,
        grid=(n_steps,),
        in_specs=[
            pl.BlockSpec((1, 1, SORT_GROUPS), lambda i, ng: (i, 0, 0), memory_space=pltpu.SMEM),
            pl.BlockSpec((1, 1, SORT_GROUPS), lambda i, ng: (jnp.minimum(i + 1, n_steps - 1), 0, 0),
                         memory_space=pltpu.SMEM),
            pl.BlockSpec((tm, TOP_K), row), pl.BlockSpec((tm, TOP_K), row), pl.BlockSpec((tm, d), row),
            pl.BlockSpec(memory_space=pl.ANY),
            pl.BlockSpec((1, d), const), pl.BlockSpec((1, d), const),
        ],
        out_specs=pl.BlockSpec((tm, d), row),
        scratch_shapes=[pltpu.VMEM((2, SORT_ROWS, d // 2), I32), pltpu.SemaphoreType.DMA((2,))],
    )
    return pl.pallas_call(
        _combine_kernel,
        grid_spec=grid_spec,
        out_shape=jax.ShapeDtypeStruct((t, d), F32),
        compiler_params=pltpu.CompilerParams(dimension_semantics=("arbitrary",), vmem_limit_bytes=VMEM_LIMIT),
        name="moe_combine_ln2",
    )(ngroups, dstg, dstg, pos_c, gates, h1, ys, g2, b2)


def kernel(x, meta_tokens, ln_in_g, ln_in_b, rel_bias, w_in, lambda_q1, lambda_k1, lambda_q2, lambda_k2,
           subln_g, a_re, a_im, log_step, b_re, b_im, c_re, c_im, d_skip, w_glu, b_glu, w_out, ln1_g, ln1_b,
           w_router, b_router, w_gate, b_gate, w_up, b_up, w_down, b_down, ln2_g, ln2_b):
    bsz, seq, dm = x.shape
    assert seq % 512 == 0 and w_in.shape[0] == DEPTH == 1
    layer = 0
    row2 = lambda v: v.astype(F32).reshape(1, -1)

    w_in_b = w_in[layer].astype(BF16)
    gi, bi = row2(ln_in_g), row2(ln_in_b)
    q, k, vt, u = _inproj(x, gi, bi, w_in_b, tm=512, kv_tile=KV_TILE)
    meta = jnp.zeros((1, META_PAD, dm), x.dtype).at[0, :N_META].set(meta_tokens.astype(x.dtype))
    _, k_meta, vt_meta, u_meta = _inproj(meta, gi, bi, w_in_b, tm=META_PAD, kv_tile=META_PAD)

    lam_init = 0.8 - 0.6 * math.exp(-0.3 * layer)
    lam = (jnp.exp(jnp.sum(lambda_q1[layer].astype(F32) * lambda_k1[layer].astype(F32)))
           - jnp.exp(jnp.sum(lambda_q2[layer].astype(F32) * lambda_k2[layer].astype(F32))) + lam_init)
    att = _attention(lam.reshape(1), q, k, vt, k_meta[0], vt_meta[0, :, 0], _near_bias(rel_bias),
                     subln_g[layer].astype(F32).reshape(HEAD_W, 1), lam_init=lam_init)

    kpad, pt, qt, a1, a2 = _ssm_tables(a_re[layer], a_im[layer], log_step[layer], b_re[layer], b_im[layer],
                                     c_re[layer], c_im[layer], d_skip[layer])
    n_chunks = seq // SSM_T
    assert (n_chunks * bsz) % 16 == 0
    ug = jnp.transpose(u.reshape(bsz, n_chunks, SSM_T, SSM_G, SSM_CG), (3, 1, 0, 2, 4))
    ug = ug.reshape(SSM_G, n_chunks * bsz, SSM_T * SSM_CG)
    lead = jnp.zeros((8, N_META, SSM_W), BF16).at[0].set(u_meta[0, :N_META])
    ug_meta = jnp.transpose(lead.reshape(8, N_META, SSM_G, SSM_CG), (2, 0, 1, 3)).reshape(SSM_G, 8, N_META * SSM_CG)
    yg = _ssm(ug, ug_meta, kpad, pt, qt, a1, a2, n_chunks=n_chunks, bsz=bsz)
    y_ssm = jnp.transpose(yg.reshape(SSM_G, n_chunks, bsz, SSM_T, SSM_CG), (2, 1, 3, 0, 4))
    y_ssm = y_ssm.reshape(bsz, seq, SSM_W)

    t = bsz * seq
    h1, idx_t, gate_t, rank_t, cnt = _mix(
        x.reshape(t, dm), gi, bi, att.reshape(t, ATT_W), y_ssm.reshape(t, SSM_W),
        w_glu[layer].astype(BF16), row2(b_glu[layer]), w_out[layer].astype(BF16),
        row2(ln1_g[layer]), row2(ln1_b[layer]),
        jnp.transpose(w_router[layer].astype(F32)), b_router[layer].astype(F32).reshape(N_EXPERTS, 1))

    n_tiles = t // ROW_TILE
    experts = jnp.arange(N_EXPERTS, dtype=I32)
    tiles = jnp.arange(n_tiles, dtype=I32)
    tile_cnt = jnp.max(cnt, axis=-1).astype(I32)
    run = (tile_cnt + GROUP - 1) // GROUP * GROUP
    run_off = jnp.sum(jnp.where((experts[:, None] < experts[None, :])[None], run[:, :, None], 0), axis=1)
    run_before = jnp.sum(jnp.where((tiles[:, None] < tiles[None, :])[:, :, None], run[:, None, :], 0), axis=0)
    counts = jnp.sum(run, axis=0)
    padded = (counts + MOE_BLOCK - 1) // MOE_BLOCK * MOE_BLOCK
    padded_end = jnp.sum(jnp.where(experts[:, None] <= experts[None, :], padded[:, None], 0), axis=0)
    padded_start = padded_end - padded
    n_blocks = (t * TOP_K + n_tiles * N_EXPERTS * (GROUP - 1)) // MOE_BLOCK + N_EXPERTS
    block_row0 = jnp.arange(n_blocks, dtype=I32) * MOE_BLOCK
    block_expert = jnp.minimum(jnp.sum((padded_end[None, :] <= block_row0[:, None]).astype(I32), axis=1),
                               N_EXPERTS - 1)
    n_used = (padded_end[-1:] // MOE_BLOCK).astype(I32)
    ngroups = jnp.sum(run, axis=1) // GROUP
    local_row = jnp.arange(SORT_GROUPS, dtype=I32) * GROUP
    owner = jnp.minimum(jnp.sum(((run_off + run)[:, None, :] <= local_row[None, :, None]).astype(I32), axis=-1),
                        N_EXPERTS - 1)
    shift = (padded_start[None, :] + run_before - run_off) // GROUP
    dstg = jnp.sum(jnp.where(owner[..., None] == experts, shift[:, None, :], 0), axis=-1) + local_row // GROUP
    dstg = jnp.where(local_row[None, :] // GROUP < ngroups[:, None], dstg, 0).reshape(n_tiles, 1, SORT_GROUPS)
    run_off_tok = jnp.repeat(run_off, ROW_TILE, axis=0)
    pos_t = jnp.sum(jnp.where(idx_t[..., None] == experts, run_off_tok[None], 0), axis=-1) + rank_t

    xs = _dispatch(ngroups, ((padded_start + counts) // GROUP).astype(I32),
                   ((padded - counts) // GROUP).astype(I32), n_used, dstg, pos_t, h1, n_blocks * MOE_BLOCK)
    b3 = lambda v: v.astype(F32)[:, None, :]
    ys = _experts(block_expert, n_used, xs, w_gate[layer], b3(b_gate[layer]), w_up[layer], b3(b_up[layer]),
                  w_down[layer], b3(b_down[layer]))
    out = _combine((ngroups + 1) // 2 * 2, dstg, jnp.transpose(pos_t), jnp.transpose(gate_t), h1, ys,
                   row2(ln2_g[layer]), row2(ln2_b[layer]))
    return out.reshape(bsz, seq, dm)
```

```python
import functools
import math

import jax
import jax.numpy as jnp
import numpy as np
from jax import lax
from jax.experimental import pallas as pl
from jax.experimental.pallas import tpu as pltpu

F32 = jnp.float32
BF16 = jnp.bfloat16
I32 = jnp.int32

DEPTH = 1
N_META = 16
CHUNK = 64
ATT_HEADS = 4
HEAD_DIM = 64
HEAD_W = 2 * HEAD_DIM
ATT_W = ATT_HEADS * HEAD_W
SSM_W = 512
SSM_CG = 16
SSM_G = SSM_W // SSM_CG
SSM_N = 64
N_BUCKETS = 32
MAX_DISTANCE = 128
N_EXPERTS = 32
TOP_K = 4
SWIGLU_LIMIT = 7.0
SWIGLU_ALPHA = 1.702
LN_EPS = 1e-5
NEG_INF = -1e30
DEEPNORM_ALPHA = (2.0 * DEPTH) ** 0.25
LOG2E = 1.4426950408889634

Q_TILE = 512
KV_TILE = 512
V_ONES = 16
V_ROWS = HEAD_W + V_ONES
META_PAD = 128
SSM_T = 64
MIX_TILE = 512
ROW_TILE = 256
MOE_BLOCK = 512
GROUP = 8
SORT_ROWS = 1280
SORT_GROUPS = SORT_ROWS // GROUP
assert SORT_ROWS >= ROW_TILE * TOP_K + N_EXPERTS * (GROUP - 1)
VMEM_LIMIT = 56 * 1024 * 1024


def _layer_norm(x, g, b):
    mu = jnp.mean(x, axis=-1, keepdims=True)
    xc = x - mu
    var = jnp.mean(xc * xc, axis=-1, keepdims=True)
    return xc * lax.rsqrt(var + LN_EPS) * g + b


def _dot(a, b):
    return jnp.dot(a, b, preferred_element_type=F32)


def _dot_nt(a, b, precision=None):
    return lax.dot_general(a, b, (((1,), (1,)), ((), ())), precision=precision,
                           preferred_element_type=F32)


def _inproj_kernel(x_ref, g_ref, b_ref, w_ref, q_ref, k_ref, vt_ref, u_ref, *, kv_tile, q_scale):
    h = _layer_norm(x_ref[0], g_ref[...], b_ref[...]).astype(BF16)
    tm = h.shape[0]
    q_ref[0] = (_dot(h, w_ref[:, 0:ATT_W]) * q_scale).astype(BF16)
    k_ref[0] = _dot(h, w_ref[:, ATT_W:2 * ATT_W]).astype(BF16)
    v = _dot(h, w_ref[:, 2 * ATT_W:3 * ATT_W])
    ones = jnp.ones((V_ONES, kv_tile), BF16)
    for hh in range(ATT_HEADS):
        vt = v[:, hh * HEAD_W:(hh + 1) * HEAD_W].T.astype(BF16)
        for j in range(tm // kv_tile):
            vt_ref[0, hh, j, 0:HEAD_W, :] = vt[:, j * kv_tile:(j + 1) * kv_tile]
            vt_ref[0, hh, j, HEAD_W:V_ROWS, :] = ones
    u_ref[0] = _dot(h, w_ref[:, 3 * ATT_W:]).astype(BF16)


def _inproj(x, g, b, w, *, tm, kv_tile):
    bsz, s, d = x.shape
    q_scale = HEAD_DIM ** -0.5 * LOG2E
    n_cols = w.shape[1]
    row = lambda bi, i: (bi, i, 0)
    return pl.pallas_call(
        functools.partial(_inproj_kernel, kv_tile=kv_tile, q_scale=q_scale),
        grid=(bsz, s // tm),
        in_specs=[
            pl.BlockSpec((1, tm, d), row),
            pl.BlockSpec((1, d), lambda bi, i: (0, 0)),
            pl.BlockSpec((1, d), lambda bi, i: (0, 0)),
            pl.BlockSpec((d, n_cols), lambda bi, i: (0, 0)),
        ],
        out_specs=[
            pl.BlockSpec((1, tm, ATT_W), row),
            pl.BlockSpec((1, tm, ATT_W), row),
            pl.BlockSpec((1, ATT_HEADS, tm // kv_tile, V_ROWS, kv_tile), lambda bi, i: (bi, 0, i, 0, 0)),
            pl.BlockSpec((1, tm, SSM_W), row),
        ],
        out_shape=[
            jax.ShapeDtypeStruct((bsz, s, ATT_W), BF16),
            jax.ShapeDtypeStruct((bsz, s, ATT_W), BF16),
            jax.ShapeDtypeStruct((bsz, ATT_HEADS, s // kv_tile, V_ROWS, kv_tile), BF16),
            jax.ShapeDtypeStruct((bsz, s, SSM_W), BF16),
        ],
        compiler_params=pltpu.CompilerParams(
            dimension_semantics=("parallel", "parallel"), vmem_limit_bytes=VMEM_LIMIT),
        name="inproj",
    )(x, g, b, w)


def _attn_kernel(lam_ref, q_ref, k_ref, vt_ref, km_ref, vtm_ref, bias_ref, g_ref, o_ref,
                 m_ref, acc_ref, qz_ref, sa_ref, sb_ref, sn_ref, *, out_scale):
    i = pl.program_id(2)
    qt = q_ref[0].astype(F32).T
    feat = lax.broadcasted_iota(I32, qt.shape, 0)
    qz_ref[:, 0:Q_TILE] = jnp.where(feat < HEAD_DIM, qt, 0.0).astype(BF16)
    qz_ref[:, Q_TILE:] = jnp.where(feat >= HEAD_DIM, qt, 0.0).astype(BF16)

    m_ref[...] = jnp.full(m_ref.shape, NEG_INF, F32)
    acc_ref[...] = jnp.zeros(acc_ref.shape, F32)

    def scores(kt):
        return _dot(kt, qz_ref[...])

    def k_tile(t):
        return k_ref[0, pl.ds(pl.multiple_of(t * KV_TILE, KV_TILE), KV_TILE), :]

    def absorb(s, pv):
        m_old = m_ref[...]
        m_new = jnp.maximum(m_old, jnp.max(s, axis=0, keepdims=True))
        alpha = jnp.exp2(m_old - m_new)
        p = jnp.exp2(s - m_new).astype(BF16)
        acc_ref[...] = acc_ref[...] * alpha + pv(p)
        m_ref[...] = m_new

    def tile_pv(t):
        return lambda p: _dot(vt_ref[0, 0, t], p)

    n_far = jnp.maximum(i - 1, 0)
    peel = n_far % 2

    sn_ref[...] = scores(km_ref[...])
    sa_ref[...] = scores(k_tile(0))
    absorb(sn_ref[...] + bias_ref[0, 0, 2 * KV_TILE:, :], lambda p: _dot(vtm_ref[0], p))

    @pl.when(peel == 1)
    def _():
        sb_ref[...] = scores(k_tile(1))
        absorb(sa_ref[...], tile_pv(0))
        sa_ref[...] = sb_ref[...]

    def far_pair(j, carry):
        t0 = peel + 2 * j
        s_cur = sa_ref[...]
        sb_ref[...] = scores(k_tile(t0 + 1))
        absorb(s_cur, tile_pv(t0))
        s_cur = sb_ref[...]
        sa_ref[...] = scores(k_tile(t0 + 2))
        absorb(s_cur, tile_pv(t0 + 1))
        return carry

    lax.fori_loop(0, n_far // 2, far_pair, 0)

    sb_ref[...] = scores(k_tile(i))
    absorb(sa_ref[...] + bias_ref[0, 0, 0:KV_TILE, :], tile_pv(n_far))
    absorb(sb_ref[...] + bias_ref[0, 0, KV_TILE:2 * KV_TILE, :], tile_pv(i))

    acc = acc_ref[...]
    lam = lam_ref[0]
    o1 = acc[0:HEAD_W, 0:Q_TILE] / acc[HEAD_W:HEAD_W + 1, 0:Q_TILE]
    o2 = acc[0:HEAD_W, Q_TILE:] / acc[HEAD_W:HEAD_W + 1, Q_TILE:]
    o = o1 - lam * o2
    ms = jnp.mean(o * o, axis=0, keepdims=True)
    o = o * lax.rsqrt(ms + LN_EPS) * g_ref[...] * out_scale
    o_ref[0] = o.T.astype(o_ref.dtype)


def _attention(lam, q, k, vt, k_meta, vt_meta, bias, subln_g, *, lam_init):
    bsz, s, _ = q.shape
    nq = s // Q_TILE
    n_near = META_PAD + 2 * KV_TILE
    grid_spec = pltpu.PrefetchScalarGridSpec(
        num_scalar_prefetch=1,
        grid=(bsz, ATT_HEADS, nq),
        in_specs=[
            pl.BlockSpec((1, Q_TILE, HEAD_W), lambda b, h, i, lam: (b, i, h)),
            pl.BlockSpec((1, s, HEAD_W), lambda b, h, i, lam: (b, 0, h)),
            pl.BlockSpec((1, 1, s // KV_TILE, V_ROWS, KV_TILE), lambda b, h, i, lam: (b, h, 0, 0, 0)),
            pl.BlockSpec((META_PAD, HEAD_W), lambda b, h, i, lam: (0, h)),
            pl.BlockSpec((1, V_ROWS, META_PAD), lambda b, h, i, lam: (h, 0, 0)),
            pl.BlockSpec((1, 1, n_near, 2 * Q_TILE), lambda b, h, i, lam: (h, jnp.minimum(i, 1), 0, 0)),
            pl.BlockSpec((HEAD_W, 1), lambda b, h, i, lam: (0, 0)),
        ],
        out_specs=pl.BlockSpec((1, Q_TILE, HEAD_W), lambda b, h, i, lam: (b, i, h)),
        scratch_shapes=[
            pltpu.VMEM((1, 2 * Q_TILE), F32), pltpu.VMEM((V_ROWS, 2 * Q_TILE), F32),
            pltpu.VMEM((HEAD_W, 2 * Q_TILE), BF16),
            pltpu.VMEM((KV_TILE, 2 * Q_TILE), F32), pltpu.VMEM((KV_TILE, 2 * Q_TILE), F32),
            pltpu.VMEM((META_PAD, 2 * Q_TILE), F32),
        ],
    )
    return pl.pallas_call(
        functools.partial(_attn_kernel, out_scale=1.0 - lam_init),
        grid_spec=grid_spec,
        out_shape=jax.ShapeDtypeStruct((bsz, s, ATT_W), BF16),
        compiler_params=pltpu.CompilerParams(
            dimension_semantics=("parallel", "parallel", "arbitrary"), vmem_limit_bytes=VMEM_LIMIT),
        name="diff_attention",
    )(lam, q, k, vt, k_meta, vt_meta, bias, subln_g)


def _t5_bucket(rel):
    nb = N_BUCKETS // 2
    max_exact = nb // 2
    ret = jnp.where(rel > 0, nb, 0)
    n = jnp.abs(rel)
    n_f = jnp.maximum(n, 1).astype(F32)
    large = max_exact + (jnp.log(n_f / max_exact) / math.log(MAX_DISTANCE / max_exact)
                         * (nb - max_exact)).astype(I32)
    large = jnp.minimum(large, nb - 1)
    return ret + jnp.where(n < max_exact, n, large)


def _near_bias(rel_bias):
    table = rel_bias.astype(F32)
    far = table[N_BUCKETS // 2 - 1]
    c = jnp.arange(Q_TILE, dtype=I32)[None, :]
    r = jnp.arange(KV_TILE, dtype=I32)[:, None]

    shifted = jnp.transpose(table - far) * LOG2E

    def bias_of(offset, rows, cols=Q_TILE):
        rel = jnp.arange(rows, dtype=I32)[:, None] - c[:, :cols] + offset
        hot = _t5_bucket(rel)[None, :, :, None] == jnp.arange(N_BUCKETS, dtype=I32)
        return jnp.sum(jnp.where(hot, shifted[:, None, None, :], 0.0), axis=-1)

    own = jnp.where((r // CHUNK <= c // CHUNK)[None], bias_of(0, KV_TILE), NEG_INF)
    near = MAX_DISTANCE
    prev = jnp.zeros((ATT_HEADS, KV_TILE, Q_TILE), F32).at[:, KV_TILE - near:, :near].set(bias_of(-near, near, near))
    rm = jnp.arange(META_PAD, dtype=I32)[:, None]
    meta_ok = (rm < N_META)[None]
    meta0 = jnp.where(meta_ok, jnp.pad(bias_of(-N_META, N_META), ((0, 0), (0, META_PAD - N_META), (0, 0))), NEG_INF)
    meta1 = jnp.where(meta_ok, jnp.zeros((ATT_HEADS, META_PAD, Q_TILE), F32), NEG_INF)
    assert N_META + Q_TILE - (N_META - 1) >= MAX_DISTANCE and KV_TILE >= near and Q_TILE >= near
    v0 = jnp.concatenate([jnp.full_like(prev, NEG_INF), own, meta0], axis=1)
    v1 = jnp.concatenate([prev, own, meta1], axis=1)
    both = jnp.stack([v0, v1], axis=1)
    return jnp.concatenate([both, both], axis=-1)


def _ssm_tables(a_re, a_im, log_step, b_re, b_im, c_re, c_im, d_skip):
    hi = lax.Precision.HIGHEST
    t_len = SSM_T
    step = jnp.exp(log_step.astype(F32))[:, None]
    ar = jnp.minimum(a_re.astype(F32), -1e-4)
    ai = a_im.astype(F32)
    mag = jnp.exp(step * ar)
    ph = step * ai
    abar_re = mag * jnp.cos(ph)
    abar_im = mag * jnp.sin(ph)
    den = ar * ar + ai * ai
    e_re = abar_re - 1.0
    e_im = abar_im
    f_re = (e_re * ar + e_im * ai) / den
    f_im = (e_im * ar - e_re * ai) / den
    br = b_re.astype(F32)
    bi = b_im.astype(F32)
    bb_re = f_re[..., None] * br - f_im[..., None] * bi
    bb_im = f_re[..., None] * bi + f_im[..., None] * br
    tau = jnp.arange(t_len + 1, dtype=F32)[None, :, None]
    pmag = jnp.exp(tau * (step * ar)[:, None, :])
    pph = tau * ph[:, None, :]
    pw_re = pmag * jnp.cos(pph)
    pw_im = pmag * jnp.sin(pph)
    bbt_re = jnp.swapaxes(bb_re, 1, 2)[:, None]
    bbt_im = jnp.swapaxes(bb_im, 1, 2)[:, None]

    def times_bbar(p_re, p_im):
        return (p_re[:, :, None, :] * bbt_re - p_im[:, :, None, :] * bbt_im,
                p_re[:, :, None, :] * bbt_im + p_im[:, :, None, :] * bbt_re)

    cr = c_re.astype(F32)
    ci = c_im.astype(F32)
    bi_re = jnp.swapaxes(bb_re, 1, 2)[:, :, None, :]
    bi_im = jnp.swapaxes(bb_im, 1, 2)[:, :, None, :]
    cb_re = cr[:, None] * bi_re - ci[:, None] * bi_im
    cb_im = cr[:, None] * bi_im + ci[:, None] * bi_re
    kern = jnp.einsum('gikm,gtm->gitk', jnp.concatenate([cb_re, -cb_im], axis=-1),
                      jnp.concatenate([pw_re[:, :t_len], pw_im[:, :t_len]], axis=-1), precision=hi)
    skip = d_skip.astype(F32).reshape(SSM_G, SSM_CG)
    kern = kern.at[:, :, 0, :].add(skip[:, None, :] * jnp.eye(SSM_CG, dtype=F32))
    kflat = kern.reshape(SSM_G, SSM_CG, t_len * SSM_CG)
    kpad = jnp.concatenate([jnp.zeros_like(kflat), kflat], axis=-1)
    back = jnp.arange(t_len - 1, -1, -1, dtype=F32)[None, :, None]
    bmag = jnp.exp(back * (step * ar)[:, None, :])
    bph = back * ph[:, None, :]
    pt_re, pt_im = times_bbar(bmag * jnp.cos(bph), bmag * jnp.sin(bph))
    pt = jnp.concatenate([pt_re, pt_im], axis=-1).reshape(SSM_G, t_len * SSM_CG, 2 * SSM_N)
    up_re = jnp.swapaxes(pw_re[:, 1:], 1, 2)[..., None]
    up_im = jnp.swapaxes(pw_im[:, 1:], 1, 2)[..., None]
    crt = jnp.swapaxes(cr, 1, 2)[:, :, None, :]
    cit = jnp.swapaxes(ci, 1, 2)[:, :, None, :]
    q_re = crt * up_re - cit * up_im
    q_im = crt * up_im + cit * up_re
    qt = jnp.concatenate([q_re, -q_im], axis=1).reshape(SSM_G, 2 * SSM_N, t_len * SSM_CG)
    at_re = pw_re[:, t_len]
    at_im = pw_im[:, t_len]
    a1 = jnp.concatenate([at_re, at_re], axis=-1)
    a2 = jnp.concatenate([-at_im, at_im], axis=-1)
    return kpad, pt.astype(BF16), qt.astype(BF16), a1, a2


def _ssm_state_kernel(u_ref, pt_ref, s_ref):
    s_ref[0] = _dot(u_ref[0], pt_ref[0])


def _ssm_scan_kernel(s_ref, x0_ref, a1_ref, a2_ref, x_ref, *, n_chunks, bsz):
    a1 = a1_ref[...][:, None, :]
    a2 = a2_ref[...][:, None, :]

    def body(c, x):
        rows = pl.ds(c * bsz, bsz)
        x_ref[:, rows, :] = x
        return a1 * x + a2 * pltpu.roll(x, SSM_N, 2) + s_ref[:, rows, :]

    x0 = jnp.broadcast_to(x0_ref[:, 0:1, :], (SSM_G, bsz, 2 * SSM_N))
    lax.fori_loop(0, n_chunks, body, x0)


def _ssm_out_kernel(u_ref, x_ref, kp_ref, qt_ref, y_ref, mt_ref):
    kp = kp_ref[0]
    width = SSM_T * SSM_CG
    lane_tile = 128
    for sub in range(0, lane_tile, SSM_CG):
        shifted = kp if sub == 0 else pltpu.roll(kp, 2 * width - sub, 1)
        for s in range(SSM_T):
            off = (SSM_T - s) * SSM_CG
            if off % lane_tile == sub:
                base = off - sub
                mt_ref[s * SSM_CG:(s + 1) * SSM_CG, :] = shifted[:, base:base + width].astype(BF16)
    x = x_ref[0]
    x_hi = x.astype(BF16)
    x_lo = (x - x_hi.astype(F32)).astype(BF16)
    y = _dot(u_ref[0], mt_ref[...]) + _dot(x_hi, qt_ref[0]) + _dot(x_lo, qt_ref[0])
    y_ref[0] = y.astype(y_ref.dtype)


def _ssm_chunk_state(ug, pt):
    g, r, w = ug.shape
    n2 = 2 * SSM_N
    per_g = lambda gi: (gi, 0, 0)
    return pl.pallas_call(
        _ssm_state_kernel,
        grid=(g,),
        in_specs=[pl.BlockSpec((1, r, w), per_g), pl.BlockSpec((1, w, n2), per_g)],
        out_specs=pl.BlockSpec((1, r, n2), per_g),
        out_shape=jax.ShapeDtypeStruct((g, r, n2), F32),
        compiler_params=pltpu.CompilerParams(dimension_semantics=("parallel",), vmem_limit_bytes=VMEM_LIMIT),
        name="ssm_chunk_state",
    )(ug, pt)


def _ssm(ug, ug_meta, kpad, pt, qt, a1, a2, *, n_chunks, bsz):
    g, r, w = ug.shape
    n2 = 2 * SSM_N
    per_g = lambda gi: (gi, 0, 0)
    params = pltpu.CompilerParams(dimension_semantics=("parallel",), vmem_limit_bytes=VMEM_LIMIT)
    s = _ssm_chunk_state(ug, pt)
    x0 = _ssm_chunk_state(ug_meta, pt[:, w - ug_meta.shape[2]:, :])
    x = pl.pallas_call(
        functools.partial(_ssm_scan_kernel, n_chunks=n_chunks, bsz=bsz),
        out_shape=jax.ShapeDtypeStruct((g, r, n2), F32),
        compiler_params=pltpu.CompilerParams(vmem_limit_bytes=VMEM_LIMIT),
        name="ssm_chunk_scan",
    )(s, x0, a1, a2)
    return pl.pallas_call(
        _ssm_out_kernel,
        grid=(g,),
        in_specs=[pl.BlockSpec((1, r, w), per_g), pl.BlockSpec((1, r, n2), per_g),
                  pl.BlockSpec((1, SSM_CG, 2 * w), per_g), pl.BlockSpec((1, n2, w), per_g)],
        out_specs=pl.BlockSpec((1, r, w), per_g),
        out_shape=jax.ShapeDtypeStruct((g, r, w), BF16),
        scratch_shapes=[pltpu.VMEM((w, w), BF16)],
        compiler_params=params,
        name="ssm_output",
    )(ug, x, kpad, qt)


def _mix_kernel(x_ref, gi_ref, bi_ref, att_ref, y_ref, wglu_ref, bglu_ref, wout_ref, g1_ref, b1_ref,
                wr_ref, br_ref, h1_ref, idx_ref, gate_ref, rank_ref, cnt_ref):
    h0 = _layer_norm(x_ref[...], gi_ref[...], bi_ref[...])
    y = y_ref[...].astype(F32)
    y = y * (0.5 * (1.0 + jnp.tanh(math.sqrt(2.0 / math.pi) * (y + 0.044715 * (y * y * y)))))
    y = y * jax.nn.sigmoid(_dot(y.astype(BF16), wglu_ref[...]) + bglu_ref[...])
    mix = _dot(att_ref[...], wout_ref[0:ATT_W, :]) + _dot(y.astype(BF16), wout_ref[ATT_W:, :])
    h1 = _layer_norm(DEEPNORM_ALPHA * h0 + mix, g1_ref[...], b1_ref[...])
    h1_ref[...] = h1

    logits = _dot_nt(wr_ref[...], h1, precision=lax.Precision.HIGHEST) + br_ref[...]
    tm = logits.shape[1]
    eidx = lax.broadcasted_iota(I32, logits.shape, 0)
    vals, hots = [], []
    rest = logits
    for _ in range(TOP_K):
        mx = jnp.max(rest, axis=0, keepdims=True)
        first = jnp.min(jnp.where(rest == mx, eidx, N_EXPERTS), axis=0, keepdims=True)
        hot = eidx == first
        vals.append(mx)
        hots.append(hot)
        rest = jnp.where(hot, -jnp.inf, rest)
    exps = [jnp.exp(v - vals[0]) for v in vals]
    denom = exps[0] + exps[1] + exps[2] + exps[3]
    gate_ref[...] = jnp.concatenate([e / denom for e in exps], axis=0)
    idx_ref[...] = jnp.concatenate(
        [jnp.sum(jnp.where(h, eidx, 0), axis=0, keepdims=True) for h in hots], axis=0)

    hot_all = (hots[0] | hots[1] | hots[2] | hots[3]).astype(F32)
    sub = ROW_TILE
    tri = (lax.broadcasted_iota(I32, (sub, sub), 0) < lax.broadcasted_iota(I32, (sub, sub), 1)).astype(BF16)
    for part in range(tm // sub):
        cols = slice(part * sub, (part + 1) * sub)
        before = _dot(hot_all[:, cols].astype(BF16), tri)
        rank_ref[:, cols] = jnp.concatenate(
            [jnp.sum(jnp.where(h[:, cols], before, 0.0), axis=0, keepdims=True) for h in hots], axis=0).astype(I32)
        cnt_ref[part] = jnp.broadcast_to(jnp.sum(hot_all[:, cols], axis=1, keepdims=True), cnt_ref.shape[1:])


def _mix(x2, gi, bi, att, y, wglu, bglu, wout, g1, b1, wr_t, br):
    t, d = x2.shape
    tm = MIX_TILE
    row = lambda i: (i, 0)
    col = lambda i: (0, i)
    const = lambda i: (0, 0)
    return pl.pallas_call(
        _mix_kernel,
        grid=(t // tm,),
        in_specs=[
            pl.BlockSpec((tm, d), row), pl.BlockSpec((1, d), const), pl.BlockSpec((1, d), const),
            pl.BlockSpec((tm, ATT_W), row), pl.BlockSpec((tm, SSM_W), row),
            pl.BlockSpec((SSM_W, SSM_W), const), pl.BlockSpec((1, SSM_W), const),
            pl.BlockSpec((d, d), const), pl.BlockSpec((1, d), const), pl.BlockSpec((1, d), const),
            pl.BlockSpec((N_EXPERTS, d), const), pl.BlockSpec((N_EXPERTS, 1), const),
        ],
        out_specs=[
            pl.BlockSpec((tm, d), row),
            pl.BlockSpec((TOP_K, tm), col), pl.BlockSpec((TOP_K, tm), col), pl.BlockSpec((TOP_K, tm), col),
            pl.BlockSpec((tm // ROW_TILE, N_EXPERTS, 128), lambda i: (i, 0, 0)),
        ],
        out_shape=[
            jax.ShapeDtypeStruct((t, d), F32),
            jax.ShapeDtypeStruct((TOP_K, t), I32), jax.ShapeDtypeStruct((TOP_K, t), F32),
            jax.ShapeDtypeStruct((TOP_K, t), I32),
            jax.ShapeDtypeStruct((t // ROW_TILE, N_EXPERTS, 128), F32),
        ],
        compiler_params=pltpu.CompilerParams(dimension_semantics=("parallel",), vmem_limit_bytes=VMEM_LIMIT),
        name="mix_ln1_router",
    )(x2, gi, bi, att, y, wglu, bglu, wout, g1, b1, wr_t, br)


_HI_BITS = -65536


def _pack_halves(x):
    half = x.shape[1] // 2
    lo = lax.bitcast_convert_type(x[:, :half], I32)
    hi = lax.bitcast_convert_type(x[:, half:], I32)
    return lax.shift_right_logical(lo, 16) | (hi & _HI_BITS)


def _unpack_halves(w):
    lo = lax.bitcast_convert_type(lax.shift_left(w, 16), F32)
    hi = lax.bitcast_convert_type(w & _HI_BITS, F32)
    return lo.astype(BF16), hi.astype(BF16)


def _group_copy(src, src_group, dst, dst_group, sem):
    return pltpu.make_async_copy(src.at[pl.ds(pl.multiple_of(src_group * GROUP, GROUP), GROUP), :],
                                 dst.at[pl.ds(pl.multiple_of(dst_group * GROUP, GROUP), GROUP), :], sem)


def _dispatch_kernel(ngroups_ref, pad_start_ref, pad_count_ref, n_used_ref, dstg_ref, pos_ref, h1_ref, xs_hbm,
                     sorted_ref, zero_ref, sems, zsem):
    step = pl.program_id(0)
    n_steps = pl.num_programs(0)
    slot = step % 2
    n_blocks = xs_hbm.shape[0] // MOE_BLOCK

    def zero_block(blk):
        return pltpu.make_async_copy(zero_ref, xs_hbm.at[pl.ds(blk * MOE_BLOCK, MOE_BLOCK), :], zsem)

    @pl.when(step == 0)
    def _():
        zero_ref[...] = jnp.zeros(zero_ref.shape, I32)
        for e in range(N_EXPERTS):
            start = pad_start_ref[e]
            count = pad_count_ref[e]

            def fill(j, c):
                _group_copy(zero_ref, 0, xs_hbm, start + j, zsem).start()
                return c

            def drain(j, c):
                _group_copy(zero_ref, 0, xs_hbm, start, zsem).wait()
                return c

            lax.fori_loop(0, count, fill, 0)
            lax.fori_loop(0, count, drain, 0)

        def fill_block(blk, c):
            zero_block(blk).start()
            return c

        def drain_block(blk, c):
            zero_block(blk).wait()
            return c

        lax.fori_loop(n_used_ref[0], n_blocks, fill_block, 0)
        lax.fori_loop(n_used_ref[0], n_blocks, drain_block, 0)

    pos = pos_ref[...]
    rows = lax.broadcasted_iota(I32, (SORT_ROWS, pos.shape[1]), 0)
    place = rows == pos[0:1]
    for k in range(1, TOP_K):
        place = place | (rows == pos[k:k + 1])
    sorted_ref[slot] = _pack_halves(_dot(place.astype(BF16), h1_ref[...].astype(BF16)))

    def send(j):
        return _group_copy(sorted_ref.at[slot], j, xs_hbm, dstg_ref[0, 0, j], sems.at[slot])

    def issue_pair(jj, c):
        send(2 * jj).start()
        send(2 * jj + 1).start()
        return c

    n_here = ngroups_ref[step]
    lax.fori_loop(0, n_here // 2, issue_pair, 0)

    @pl.when(n_here % 2 == 1)
    def _():
        send(n_here - 1).start()

    def wait_tile(s, count):
        def drain_pair(jj, c):
            pltpu.make_async_copy(sorted_ref.at[s, pl.ds(0, 2 * GROUP), :], xs_hbm.at[pl.ds(0, 2 * GROUP), :],
                                  sems.at[s]).wait()
            return c
        lax.fori_loop(0, count // 2, drain_pair, 0)

        @pl.when(count % 2 == 1)
        def _():
            _group_copy(sorted_ref.at[s], 0, xs_hbm, 0, sems.at[s]).wait()

    @pl.when(step > 0)
    def _():
        wait_tile(1 - slot, ngroups_ref[jnp.maximum(step - 1, 0)])

    @pl.when(step == n_steps - 1)
    def _():
        wait_tile(slot, ngroups_ref[step])


def _dispatch(ngroups, pad_start, pad_count, n_used, dstg, pos_t, h1, n_rows):
    t, d = h1.shape
    tm = ROW_TILE
    grid_spec = pltpu.PrefetchScalarGridSpec(
        num_scalar_prefetch=4,
        grid=(t // tm,),
        in_specs=[
            pl.BlockSpec((1, 1, SORT_GROUPS), lambda i, *_: (i, 0, 0), memory_space=pltpu.SMEM),
            pl.BlockSpec((TOP_K, tm), lambda i, *_: (0, i)),
            pl.BlockSpec((tm, d), lambda i, *_: (i, 0)),
        ],
        out_specs=pl.BlockSpec(memory_space=pl.ANY),
        scratch_shapes=[pltpu.VMEM((2, SORT_ROWS, d // 2), I32), pltpu.VMEM((MOE_BLOCK, d // 2), I32),
                        pltpu.SemaphoreType.DMA((2,)), pltpu.SemaphoreType.DMA],
    )
    return pl.pallas_call(
        _dispatch_kernel,
        grid_spec=grid_spec,
        out_shape=jax.ShapeDtypeStruct((n_rows, d // 2), I32),
        compiler_params=pltpu.CompilerParams(dimension_semantics=("arbitrary",), vmem_limit_bytes=VMEM_LIMIT,
                                             has_side_effects=True),
        name="moe_dispatch",
    )(ngroups, pad_start, pad_count, n_used, dstg, pos_t, h1)


def _expert_kernel(be_ref, nb_ref, x_ref, wg_ref, bg_ref, wu_ref, bu_ref, wd_ref, bd_ref, y_ref,
                   wg_b, wu_b, wd_b):
    i = pl.program_id(0)

    @pl.when(i < nb_ref[0])
    def _():
        @pl.when((i == 0) | (be_ref[i] != be_ref[jnp.maximum(i - 1, 0)]))
        def _():
            wg_b[...] = wg_ref[0].astype(BF16)
            wu_b[...] = wu_ref[0].astype(BF16)
            wd_b[...] = wd_ref[0].astype(BF16)

        x_lo, x_hi = _unpack_halves(x_ref[...])
        half = x_lo.shape[1]
        gate = _dot(x_lo, wg_b[0:half, :]) + _dot(x_hi, wg_b[half:, :]) + bg_ref[0]
        up = _dot(x_lo, wu_b[0:half, :]) + _dot(x_hi, wu_b[half:, :]) + bu_ref[0]
        gate = jnp.minimum(gate, SWIGLU_LIMIT)
        up = jnp.clip(up, -SWIGLU_LIMIT, SWIGLU_LIMIT)
        act = (up + 1.0) * gate * jax.nn.sigmoid(gate * SWIGLU_ALPHA)
        y = _dot(act.astype(BF16), wd_b[...]) + bd_ref[0]
        y_ref[...] = _pack_halves(y.astype(BF16).astype(F32))

    @pl.when(i >= nb_ref[0])
    def _():
        y_ref[...] = jnp.zeros(y_ref.shape, I32)


def _experts(block_expert, n_used, xs, wg, bg, wu, bu, wd, bd):
    n_rows, packed = xs.shape
    d, dff = wg.shape[1], wg.shape[2]
    assert packed * 2 == d and wd.shape[2] == d
    nb = n_rows // MOE_BLOCK
    blk = lambda i, be, nu: (jnp.minimum(i, nu[0] - 1), 0)
    wsel = lambda i, be, nu: (be[jnp.minimum(i, nu[0] - 1)], 0, 0)
    grid_spec = pltpu.PrefetchScalarGridSpec(
        num_scalar_prefetch=2,
        grid=(nb,),
        in_specs=[
            pl.BlockSpec((MOE_BLOCK, packed), blk),
            pl.BlockSpec((1, d, dff), wsel), pl.BlockSpec((1, 1, dff), wsel),
            pl.BlockSpec((1, d, dff), wsel), pl.BlockSpec((1, 1, dff), wsel),
            pl.BlockSpec((1, dff, d), wsel), pl.BlockSpec((1, 1, d), wsel),
        ],
        out_specs=pl.BlockSpec((MOE_BLOCK, packed), lambda i, be, nu: (i, 0)),
        scratch_shapes=[pltpu.VMEM((d, dff), BF16), pltpu.VMEM((d, dff), BF16), pltpu.VMEM((dff, d), BF16)],
    )
    return pl.pallas_call(
        _expert_kernel,
        grid_spec=grid_spec,
        out_shape=jax.ShapeDtypeStruct((n_rows, packed), I32),
        compiler_params=pltpu.CompilerParams(dimension_semantics=("arbitrary",), vmem_limit_bytes=VMEM_LIMIT),
        name="moe_experts",
    )(block_expert, n_used, xs, wg, bg, wu, bu, wd, bd)


def _combine_kernel(ngroups_ref, dstg_ref, dstg_next_ref, pos_ref, gate_ref, h1_ref, ys_hbm, g2_ref, b2_ref,
                    o_ref, buf, sems):
    step = pl.program_id(0)
    n_steps = pl.num_programs(0)
    slot = step % 2

    def fetch(dref, s, j):
        return _group_copy(ys_hbm, dref[0, 0, j], buf.at[s], j, sems.at[s])

    def issue(dref, s, count):
        def body(jj, c):
            fetch(dref, s, 2 * jj).start()
            fetch(dref, s, 2 * jj + 1).start()
            return c
        lax.fori_loop(0, count // 2, body, 0)

    @pl.when(step == 0)
    def _():
        buf[...] = jnp.zeros(buf.shape, I32)
        issue(dstg_ref, 0, ngroups_ref[0])

    @pl.when(step + 1 < n_steps)
    def _():
        issue(dstg_next_ref, 1 - slot, ngroups_ref[jnp.minimum(step + 1, n_steps - 1)])

    def drain(jj, c):
        pltpu.make_async_copy(ys_hbm.at[pl.ds(0, 2 * GROUP), :], buf.at[slot, pl.ds(0, 2 * GROUP), :],
                              sems.at[slot]).wait()
        return c

    lax.fori_loop(0, ngroups_ref[step] // 2, drain, 0)

    pos = pos_ref[...]
    gate = gate_ref[...]
    lanes = lax.broadcasted_iota(I32, (pos.shape[0], SORT_ROWS), 1)
    sel = jnp.where(lanes == pos[:, 0:1], gate[:, 0:1], 0.0)
    for k in range(1, TOP_K):
        sel = sel + jnp.where(lanes == pos[:, k:k + 1], gate[:, k:k + 1], 0.0)
    sel_hi = sel.astype(BF16)
    sel_lo = (sel - sel_hi.astype(F32)).astype(BF16)
    y_lo, y_hi = _unpack_halves(buf[slot])
    ffn = jnp.concatenate([_dot(sel_hi, y_lo) + _dot(sel_lo, y_lo), _dot(sel_hi, y_hi) + _dot(sel_lo, y_hi)], axis=1)
    o_ref[...] = _layer_norm(DEEPNORM_ALPHA * h1_ref[...] + ffn, g2_ref[...], b2_ref[...])


def _combine(ngroups, dstg, pos_c, gates, h1, ys, g2, b2):
    t, d = h1.shape
    tm = ROW_TILE
    n_steps = t // tm
    row = lambda i, ng: (i, 0)
    const = lambda i, ng: (0, 0)
    grid_spec = pltpu.PrefetchScalarGridSpec(
        num_scalar_prefetch=1,
        grid=(n_steps,),
        in_specs=[
            pl.BlockSpec((1, 1, SORT_GROUPS), lambda i, ng: (i, 0, 0), memory_space=pltpu.SMEM),
            pl.BlockSpec((1, 1, SORT_GROUPS), lambda i, ng: (jnp.minimum(i + 1, n_steps - 1), 0, 0),
                         memory_space=pltpu.SMEM),
            pl.BlockSpec((tm, TOP_K), row), pl.BlockSpec((tm, TOP_K), row), pl.BlockSpec((tm, d), row),
            pl.BlockSpec(memory_space=pl.ANY),
            pl.BlockSpec((1, d), const), pl.BlockSpec((1, d), const),
        ],
        out_specs=pl.BlockSpec((tm, d), row),
        scratch_shapes=[pltpu.VMEM((2, SORT_ROWS, d // 2), I32), pltpu.SemaphoreType.DMA((2,))],
    )
    return pl.pallas_call(
        _combine_kernel,
        grid_spec=grid_spec,
        out_shape=jax.ShapeDtypeStruct((t, d), F32),
        compiler_params=pltpu.CompilerParams(dimension_semantics=("arbitrary",), vmem_limit_bytes=VMEM_LIMIT),
        name="moe_combine_ln2",
    )(ngroups, dstg, dstg, pos_c, gates, h1, ys, g2, b2)


def kernel(x, meta_tokens, ln_in_g, ln_in_b, rel_bias, w_in, lambda_q1, lambda_k1, lambda_q2, lambda_k2,
           subln_g, a_re, a_im, log_step, b_re, b_im, c_re, c_im, d_skip, w_glu, b_glu, w_out, ln1_g, ln1_b,
           w_router, b_router, w_gate, b_gate, w_up, b_up, w_down, b_down, ln2_g, ln2_b):
    bsz, seq, dm = x.shape
    assert seq % 512 == 0 and w_in.shape[0] == DEPTH == 1
    layer = 0
    row2 = lambda v: v.astype(F32).reshape(1, -1)

    w_in_b = w_in[layer].astype(BF16)
    gi, bi = row2(ln_in_g), row2(ln_in_b)
    q, k, vt, u = _inproj(x, gi, bi, w_in_b, tm=512, kv_tile=KV_TILE)
    meta = jnp.zeros((1, META_PAD, dm), x.dtype).at[0, :N_META].set(meta_tokens.astype(x.dtype))
    _, k_meta, vt_meta, u_meta = _inproj(meta, gi, bi, w_in_b, tm=META_PAD, kv_tile=META_PAD)

    lam_init = 0.8 - 0.6 * math.exp(-0.3 * layer)
    lam = (jnp.exp(jnp.sum(lambda_q1[layer].astype(F32) * lambda_k1[layer].astype(F32)))
           - jnp.exp(jnp.sum(lambda_q2[layer].astype(F32) * lambda_k2[layer].astype(F32))) + lam_init)
    att = _attention(lam.reshape(1), q, k, vt, k_meta[0], vt_meta[0, :, 0], _near_bias(rel_bias),
                     subln_g[layer].astype(F32).reshape(HEAD_W, 1), lam_init=lam_init)

    kpad, pt, qt, a1, a2 = _ssm_tables(a_re[layer], a_im[layer], log_step[layer], b_re[layer], b_im[layer],
                                     c_re[layer], c_im[layer], d_skip[layer])
    n_chunks = seq // SSM_T
    assert (n_chunks * bsz) % 16 == 0
    ug = jnp.transpose(u.reshape(bsz, n_chunks, SSM_T, SSM_G, SSM_CG), (3, 1, 0, 2, 4))
    ug = ug.reshape(SSM_G, n_chunks * bsz, SSM_T * SSM_CG)
    lead = jnp.zeros((8, N_META, SSM_W), BF16).at[0].set(u_meta[0, :N_META])
    ug_meta = jnp.transpose(lead.reshape(8, N_META, SSM_G, SSM_CG), (2, 0, 1, 3)).reshape(SSM_G, 8, N_META * SSM_CG)
    yg = _ssm(ug, ug_meta, kpad, pt, qt, a1, a2, n_chunks=n_chunks, bsz=bsz)
    y_ssm = jnp.transpose(yg.reshape(SSM_G, n_chunks, bsz, SSM_T, SSM_CG), (2, 1, 3, 0, 4))
    y_ssm = y_ssm.reshape(bsz, seq, SSM_W)

    t = bsz * seq
    h1, idx_t, gate_t, rank_t, cnt = _mix(
        x.reshape(t, dm), gi, bi, att.reshape(t, ATT_W), y_ssm.reshape(t, SSM_W),
        w_glu[layer].astype(BF16), row2(b_glu[layer]), w_out[layer].astype(BF16),
        row2(ln1_g[layer]), row2(ln1_b[layer]),
        jnp.transpose(w_router[layer].astype(F32)), b_router[layer].astype(F32).reshape(N_EXPERTS, 1))

    n_tiles = t // ROW_TILE
    experts = jnp.arange(N_EXPERTS, dtype=I32)
    tiles = jnp.arange(n_tiles, dtype=I32)
    tile_cnt = jnp.max(cnt, axis=-1).astype(I32)
    run = (tile_cnt + GROUP - 1) // GROUP * GROUP
    run_off = jnp.sum(jnp.where((experts[:, None] < experts[None, :])[None], run[:, :, None], 0), axis=1)
    run_before = jnp.sum(jnp.where((tiles[:, None] < tiles[None, :])[:, :, None], run[:, None, :], 0), axis=0)
    counts = jnp.sum(run, axis=0)
    padded = (counts + MOE_BLOCK - 1) // MOE_BLOCK * MOE_BLOCK
    padded_end = jnp.sum(jnp.where(experts[:, None] <= experts[None, :], padded[:, None], 0), axis=0)
    padded_start = padded_end - padded
    n_blocks = (t * TOP_K + n_tiles * N_EXPERTS * (GROUP - 1)) // MOE_BLOCK + N_EXPERTS
    block_row0 = jnp.arange(n_blocks, dtype=I32) * MOE_BLOCK
    block_expert = jnp.minimum(jnp.sum((padded_end[None, :] <= block_row0[:, None]).astype(I32), axis=1),
                               N_EXPERTS - 1)
    n_used = (padded_end[-1:] // MOE_BLOCK).astype(I32)
    ngroups = jnp.sum(run, axis=1) // GROUP
    local_row = jnp.arange(SORT_GROUPS, dtype=I32) * GROUP
    owner = jnp.minimum(jnp.sum(((run_off + run)[:, None, :] <= local_row[None, :, None]).astype(I32), axis=-1),
                        N_EXPERTS - 1)
    shift = (padded_start[None, :] + run_before - run_off) // GROUP
    dstg = jnp.sum(jnp.where(owner[..., None] == experts, shift[:, None, :], 0), axis=-1) + local_row // GROUP
    dstg = jnp.where(local_row[None, :] // GROUP < ngroups[:, None], dstg, 0).reshape(n_tiles, 1, SORT_GROUPS)
    run_off_tok = jnp.repeat(run_off, ROW_TILE, axis=0)
    pos_t = jnp.sum(jnp.where(idx_t[..., None] == experts, run_off_tok[None], 0), axis=-1) + rank_t

    xs = _dispatch(ngroups, ((padded_start + counts) // GROUP).astype(I32),
                   ((padded - counts) // GROUP).astype(I32), n_used, dstg, pos_t, h1, n_blocks * MOE_BLOCK)
    b3 = lambda v: v.astype(F32)[:, None, :]
    ys = _experts(block_expert, n_used, xs, w_gate[layer], b3(b_gate[layer]), w_up[layer], b3(b_up[layer]),
                  w_down[layer], b3(b_down[layer]))
    out = _combine((ngroups + 1) // 2 * 2, dstg, jnp.transpose(pos_t), jnp.transpose(gate_t), h1, ys,
                   row2(ln2_g[layer]), row2(ln2_b[layer]))
    return out.reshape(bsz, seq, dm)
```

```python
import functools
import math

import jax
import jax.numpy as jnp
import numpy as np
from jax import lax
from jax.experimental import pallas as pl
from jax.experimental.pallas import tpu as pltpu

F32 = jnp.float32
BF16 = jnp.bfloat16
I32 = jnp.int32

DEPTH = 1
N_META = 16
CHUNK = 64
ATT_HEADS = 4
HEAD_DIM = 64
HEAD_W = 2 * HEAD_DIM
ATT_W = ATT_HEADS * HEAD_W
SSM_W = 512
SSM_CG = 16
SSM_G = SSM_W // SSM_CG
SSM_N = 64
N_BUCKETS = 32
MAX_DISTANCE = 128
N_EXPERTS = 32
TOP_K = 4
SWIGLU_LIMIT = 7.0
SWIGLU_ALPHA = 1.702
LN_EPS = 1e-5
NEG_INF = -1e30
DEEPNORM_ALPHA = (2.0 * DEPTH) ** 0.25
LOG2E = 1.4426950408889634

Q_TILE = 512
KV_TILE = 512
V_ONES = 16
V_ROWS = HEAD_W + V_ONES
META_PAD = 128
SSM_T = 64
MIX_TILE = 512
ROW_TILE = 256
MOE_BLOCK = 512
GROUP = 8
COPY_BATCH = 4
SORT_ROWS = 1280
SORT_GROUPS = SORT_ROWS // GROUP
assert SORT_ROWS >= ROW_TILE * TOP_K + N_EXPERTS * (GROUP - 1)
VMEM_LIMIT = 56 * 1024 * 1024


def _layer_norm(x, g, b):
    mu = jnp.mean(x, axis=-1, keepdims=True)
    xc = x - mu
    var = jnp.mean(xc * xc, axis=-1, keepdims=True)
    return xc * lax.rsqrt(var + LN_EPS) * g + b


def _dot(a, b):
    return jnp.dot(a, b, preferred_element_type=F32)


def _dot_nt(a, b, precision=None):
    return lax.dot_general(a, b, (((1,), (1,)), ((), ())), precision=precision,
                           preferred_element_type=F32)


def _inproj_kernel(x_ref, g_ref, b_ref, w_ref, q_ref, k_ref, vt_ref, u_ref, *, kv_tile, q_scale):
    h = _layer_norm(x_ref[0], g_ref[...], b_ref[...]).astype(BF16)
    tm = h.shape[0]
    q_ref[0] = (_dot(h, w_ref[:, 0:ATT_W]) * q_scale).astype(BF16)
    k_ref[0] = _dot(h, w_ref[:, ATT_W:2 * ATT_W]).astype(BF16)
    v = _dot(h, w_ref[:, 2 * ATT_W:3 * ATT_W])
    ones = jnp.ones((V_ONES, kv_tile), BF16)
    for hh in range(ATT_HEADS):
        vt = v[:, hh * HEAD_W:(hh + 1) * HEAD_W].T.astype(BF16)
        for j in range(tm // kv_tile):
            vt_ref[0, hh, j, 0:HEAD_W, :] = vt[:, j * kv_tile:(j + 1) * kv_tile]
            vt_ref[0, hh, j, HEAD_W:V_ROWS, :] = ones
    u_ref[0] = _dot(h, w_ref[:, 3 * ATT_W:]).astype(BF16)


def _inproj(x, g, b, w, *, tm, kv_tile):
    bsz, s, d = x.shape
    q_scale = HEAD_DIM ** -0.5 * LOG2E
    n_cols = w.shape[1]
    row = lambda bi, i: (bi, i, 0)
    return pl.pallas_call(
        functools.partial(_inproj_kernel, kv_tile=kv_tile, q_scale=q_scale),
        grid=(bsz, s // tm),
        in_specs=[
            pl.BlockSpec((1, tm, d), row),
            pl.BlockSpec((1, d), lambda bi, i: (0, 0)),
            pl.BlockSpec((1, d), lambda bi, i: (0, 0)),
            pl.BlockSpec((d, n_cols), lambda bi, i: (0, 0)),
        ],
        out_specs=[
            pl.BlockSpec((1, tm, ATT_W), row),
            pl.BlockSpec((1, tm, ATT_W), row),
            pl.BlockSpec((1, ATT_HEADS, tm // kv_tile, V_ROWS, kv_tile), lambda bi, i: (bi, 0, i, 0, 0)),
            pl.BlockSpec((1, tm, SSM_W), row),
        ],
        out_shape=[
            jax.ShapeDtypeStruct((bsz, s, ATT_W), BF16),
            jax.ShapeDtypeStruct((bsz, s, ATT_W), BF16),
            jax.ShapeDtypeStruct((bsz, ATT_HEADS, s // kv_tile, V_ROWS, kv_tile), BF16),
            jax.ShapeDtypeStruct((bsz, s, SSM_W), BF16),
        ],
        compiler_params=pltpu.CompilerParams(
            dimension_semantics=("parallel", "parallel"), vmem_limit_bytes=VMEM_LIMIT),
        name="inproj",
    )(x, g, b, w)


def _attn_kernel(lam_ref, q_ref, k_ref, vt_ref, km_ref, vtm_ref, bias_ref, g_ref, o_ref,
                 m_ref, acc_ref, qz_ref, sa_ref, sb_ref, sn_ref, *, out_scale):
    i = pl.program_id(2)
    qt = q_ref[0].astype(F32).T
    feat = lax.broadcasted_iota(I32, qt.shape, 0)
    qz_ref[:, 0:Q_TILE] = jnp.where(feat < HEAD_DIM, qt, 0.0).astype(BF16)
    qz_ref[:, Q_TILE:] = jnp.where(feat >= HEAD_DIM, qt, 0.0).astype(BF16)

    m_ref[...] = jnp.full(m_ref.shape, NEG_INF, F32)
    acc_ref[...] = jnp.zeros(acc_ref.shape, F32)

    def scores(kt):
        return _dot(kt, qz_ref[...])

    def k_tile(t):
        return k_ref[0, pl.ds(pl.multiple_of(t * KV_TILE, KV_TILE), KV_TILE), :]

    def absorb(s, pv):
        m_old = m_ref[...]
        m_new = jnp.maximum(m_old, jnp.max(s, axis=0, keepdims=True))
        alpha = jnp.exp2(m_old - m_new)
        p = jnp.exp2(s - m_new).astype(BF16)
        acc_ref[...] = acc_ref[...] * alpha + pv(p)
        m_ref[...] = m_new

    def tile_pv(t):
        return lambda p: _dot(vt_ref[0, 0, t], p)

    n_far = jnp.maximum(i - 1, 0)
    peel = n_far % 2

    sn_ref[...] = scores(km_ref[...])
    sa_ref[...] = scores(k_tile(0))
    absorb(sn_ref[...] + bias_ref[0, 0, 2 * KV_TILE:, :], lambda p: _dot(vtm_ref[0], p))

    @pl.when(peel == 1)
    def _():
        sb_ref[...] = scores(k_tile(1))
        absorb(sa_ref[...], tile_pv(0))
        sa_ref[...] = sb_ref[...]

    def far_pair(j, carry):
        t0 = peel + 2 * j
        s_cur = sa_ref[...]
        sb_ref[...] = scores(k_tile(t0 + 1))
        absorb(s_cur, tile_pv(t0))
        s_cur = sb_ref[...]
        sa_ref[...] = scores(k_tile(t0 + 2))
        absorb(s_cur, tile_pv(t0 + 1))
        return carry

    lax.fori_loop(0, n_far // 2, far_pair, 0)

    sb_ref[...] = scores(k_tile(i))
    absorb(sa_ref[...] + bias_ref[0, 0, 0:KV_TILE, :], tile_pv(n_far))
    absorb(sb_ref[...] + bias_ref[0, 0, KV_TILE:2 * KV_TILE, :], tile_pv(i))

    acc = acc_ref[...]
    lam = lam_ref[0]
    o1 = acc[0:HEAD_W, 0:Q_TILE] / acc[HEAD_W:HEAD_W + 1, 0:Q_TILE]
    o2 = acc[0:HEAD_W, Q_TILE:] / acc[HEAD_W:HEAD_W + 1, Q_TILE:]
    o = o1 - lam * o2
    ms = jnp.mean(o * o, axis=0, keepdims=True)
    o = o * lax.rsqrt(ms + LN_EPS) * g_ref[...] * out_scale
    o_ref[0] = o.T.astype(o_ref.dtype)


def _attention(lam, q, k, vt, k_meta, vt_meta, bias, subln_g, *, lam_init):
    bsz, s, _ = q.shape
    nq = s // Q_TILE
    n_near = META_PAD + 2 * KV_TILE
    grid_spec = pltpu.PrefetchScalarGridSpec(
        num_scalar_prefetch=1,
        grid=(bsz, ATT_HEADS, nq),
        in_specs=[
            pl.BlockSpec((1, Q_TILE, HEAD_W), lambda b, h, i, lam: (b, i, h)),
            pl.BlockSpec((1, s, HEAD_W), lambda b, h, i, lam: (b, 0, h)),
            pl.BlockSpec((1, 1, s // KV_TILE, V_ROWS, KV_TILE), lambda b, h, i, lam: (b, h, 0, 0, 0)),
            pl.BlockSpec((META_PAD, HEAD_W), lambda b, h, i, lam: (0, h)),
            pl.BlockSpec((1, V_ROWS, META_PAD), lambda b, h, i, lam: (h, 0, 0)),
            pl.BlockSpec((1, 1, n_near, 2 * Q_TILE), lambda b, h, i, lam: (h, jnp.minimum(i, 1), 0, 0)),
            pl.BlockSpec((HEAD_W, 1), lambda b, h, i, lam: (0, 0)),
        ],
        out_specs=pl.BlockSpec((1, Q_TILE, HEAD_W), lambda b, h, i, lam: (b, i, h)),
        scratch_shapes=[
            pltpu.VMEM((1, 2 * Q_TILE), F32), pltpu.VMEM((V_ROWS, 2 * Q_TILE), F32),
            pltpu.VMEM((HEAD_W, 2 * Q_TILE), BF16),
            pltpu.VMEM((KV_TILE, 2 * Q_TILE), F32), pltpu.VMEM((KV_TILE, 2 * Q_TILE), F32),
            pltpu.VMEM((META_PAD, 2 * Q_TILE), F32),
        ],
    )
    return pl.pallas_call(
        functools.partial(_attn_kernel, out_scale=1.0 - lam_init),
        grid_spec=grid_spec,
        out_shape=jax.ShapeDtypeStruct((bsz, s, ATT_W), BF16),
        compiler_params=pltpu.CompilerParams(
            dimension_semantics=("parallel", "parallel", "arbitrary"), vmem_limit_bytes=VMEM_LIMIT),
        name="diff_attention",
    )(lam, q, k, vt, k_meta, vt_meta, bias, subln_g)


def _t5_bucket(rel):
    nb = N_BUCKETS // 2
    max_exact = nb // 2
    ret = jnp.where(rel > 0, nb, 0)
    n = jnp.abs(rel)
    n_f = jnp.maximum(n, 1).astype(F32)
    large = max_exact + (jnp.log(n_f / max_exact) / math.log(MAX_DISTANCE / max_exact)
                         * (nb - max_exact)).astype(I32)
    large = jnp.minimum(large, nb - 1)
    return ret + jnp.where(n < max_exact, n, large)


def _near_bias(rel_bias):
    table = rel_bias.astype(F32)
    far = table[N_BUCKETS // 2 - 1]
    c = jnp.arange(Q_TILE, dtype=I32)[None, :]
    r = jnp.arange(KV_TILE, dtype=I32)[:, None]

    shifted = jnp.transpose(table - far) * LOG2E

    def bias_of(offset, rows, cols=Q_TILE):
        rel = jnp.arange(rows, dtype=I32)[:, None] - c[:, :cols] + offset
        hot = _t5_bucket(rel)[None, :, :, None] == jnp.arange(N_BUCKETS, dtype=I32)
        return jnp.sum(jnp.where(hot, shifted[:, None, None, :], 0.0), axis=-1)

    own = jnp.where((r // CHUNK <= c // CHUNK)[None], bias_of(0, KV_TILE), NEG_INF)
    near = MAX_DISTANCE
    prev = jnp.zeros((ATT_HEADS, KV_TILE, Q_TILE), F32).at[:, KV_TILE - near:, :near].set(bias_of(-near, near, near))
    rm = jnp.arange(META_PAD, dtype=I32)[:, None]
    meta_ok = (rm < N_META)[None]
    meta0 = jnp.where(meta_ok, jnp.pad(bias_of(-N_META, N_META), ((0, 0), (0, META_PAD - N_META), (0, 0))), NEG_INF)
    meta1 = jnp.where(meta_ok, jnp.zeros((ATT_HEADS, META_PAD, Q_TILE), F32), NEG_INF)
    assert N_META + Q_TILE - (N_META - 1) >= MAX_DISTANCE and KV_TILE >= near and Q_TILE >= near
    v0 = jnp.concatenate([jnp.full_like(prev, NEG_INF), own, meta0], axis=1)
    v1 = jnp.concatenate([prev, own, meta1], axis=1)
    both = jnp.stack([v0, v1], axis=1)
    return jnp.concatenate([both, both], axis=-1)


def _ssm_tables(a_re, a_im, log_step, b_re, b_im, c_re, c_im, d_skip):
    hi = lax.Precision.HIGHEST
    t_len = SSM_T
    step = jnp.exp(log_step.astype(F32))[:, None]
    ar = jnp.minimum(a_re.astype(F32), -1e-4)
    ai = a_im.astype(F32)
    mag = jnp.exp(step * ar)
    ph = step * ai
    abar_re = mag * jnp.cos(ph)
    abar_im = mag * jnp.sin(ph)
    den = ar * ar + ai * ai
    e_re = abar_re - 1.0
    e_im = abar_im
    f_re = (e_re * ar + e_im * ai) / den
    f_im = (e_im * ar - e_re * ai) / den
    br = b_re.astype(F32)
    bi = b_im.astype(F32)
    bb_re = f_re[..., None] * br - f_im[..., None] * bi
    bb_im = f_re[..., None] * bi + f_im[..., None] * br
    tau = jnp.arange(t_len + 1, dtype=F32)[None, :, None]
    pmag = jnp.exp(tau * (step * ar)[:, None, :])
    pph = tau * ph[:, None, :]
    pw_re = pmag * jnp.cos(pph)
    pw_im = pmag * jnp.sin(pph)
    bbt_re = jnp.swapaxes(bb_re, 1, 2)[:, None]
    bbt_im = jnp.swapaxes(bb_im, 1, 2)[:, None]

    def times_bbar(p_re, p_im):
        return (p_re[:, :, None, :] * bbt_re - p_im[:, :, None, :] * bbt_im,
                p_re[:, :, None, :] * bbt_im + p_im[:, :, None, :] * bbt_re)

    cr = c_re.astype(F32)
    ci = c_im.astype(F32)
    bi_re = jnp.swapaxes(bb_re, 1, 2)[:, :, None, :]
    bi_im = jnp.swapaxes(bb_im, 1, 2)[:, :, None, :]
    cb_re = cr[:, None] * bi_re - ci[:, None] * bi_im
    cb_im = cr[:, None] * bi_im + ci[:, None] * bi_re
    kern = jnp.einsum('gikm,gtm->gitk', jnp.concatenate([cb_re, -cb_im], axis=-1),
                      jnp.concatenate([pw_re[:, :t_len], pw_im[:, :t_len]], axis=-1), precision=hi)
    skip = d_skip.astype(F32).reshape(SSM_G, SSM_CG)
    kern = kern.at[:, :, 0, :].add(skip[:, None, :] * jnp.eye(SSM_CG, dtype=F32))
    kflat = kern.reshape(SSM_G, SSM_CG, t_len * SSM_CG)
    kpad = jnp.concatenate([jnp.zeros_like(kflat), kflat], axis=-1)
    back = jnp.arange(t_len - 1, -1, -1, dtype=F32)[None, :, None]
    bmag = jnp.exp(back * (step * ar)[:, None, :])
    bph = back * ph[:, None, :]
    pt_re, pt_im = times_bbar(bmag * jnp.cos(bph), bmag * jnp.sin(bph))
    pt = jnp.concatenate([pt_re, pt_im], axis=-1).reshape(SSM_G, t_len * SSM_CG, 2 * SSM_N)
    up_re = jnp.swapaxes(pw_re[:, 1:], 1, 2)[..., None]
    up_im = jnp.swapaxes(pw_im[:, 1:], 1, 2)[..., None]
    crt = jnp.swapaxes(cr, 1, 2)[:, :, None, :]
    cit = jnp.swapaxes(ci, 1, 2)[:, :, None, :]
    q_re = crt * up_re - cit * up_im
    q_im = crt * up_im + cit * up_re
    qt = jnp.concatenate([q_re, -q_im], axis=1).reshape(SSM_G, 2 * SSM_N, t_len * SSM_CG)
    at_re = pw_re[:, t_len]
    at_im = pw_im[:, t_len]
    a1 = jnp.concatenate([at_re, at_re], axis=-1)
    a2 = jnp.concatenate([-at_im, at_im], axis=-1)
    return kpad, pt.astype(BF16), qt.astype(BF16), a1, a2


def _ssm_state_kernel(u_ref, pt_ref, s_ref):
    s_ref[0] = _dot(u_ref[0], pt_ref[0])


def _ssm_scan_kernel(s_ref, x0_ref, a1_ref, a2_ref, x_ref, *, n_chunks, bsz):
    a1 = a1_ref[...][:, None, :]
    a2 = a2_ref[...][:, None, :]

    def body(c, x):
        rows = pl.ds(c * bsz, bsz)
        x_ref[:, rows, :] = x
        return a1 * x + a2 * pltpu.roll(x, SSM_N, 2) + s_ref[:, rows, :]

    x0 = jnp.broadcast_to(x0_ref[:, 0:1, :], (SSM_G, bsz, 2 * SSM_N))
    lax.fori_loop(0, n_chunks, body, x0)


def _ssm_out_kernel(u_ref, x_ref, kp_ref, qt_ref, y_ref, mt_ref):
    kp = kp_ref[0]
    width = SSM_T * SSM_CG
    lane_tile = 128
    for sub in range(0, lane_tile, SSM_CG):
        shifted = kp if sub == 0 else pltpu.roll(kp, 2 * width - sub, 1)
        for s in range(SSM_T):
            off = (SSM_T - s) * SSM_CG
            if off % lane_tile == sub:
                base = off - sub
                mt_ref[s * SSM_CG:(s + 1) * SSM_CG, :] = shifted[:, base:base + width].astype(BF16)
    x = x_ref[0]
    x_hi = x.astype(BF16)
    x_lo = (x - x_hi.astype(F32)).astype(BF16)
    y = _dot(u_ref[0], mt_ref[...]) + _dot(x_hi, qt_ref[0]) + _dot(x_lo, qt_ref[0])
    y_ref[0] = y.astype(y_ref.dtype)


def _ssm_chunk_state(ug, pt):
    g, r, w = ug.shape
    n2 = 2 * SSM_N
    per_g = lambda gi: (gi, 0, 0)
    return pl.pallas_call(
        _ssm_state_kernel,
        grid=(g,),
        in_specs=[pl.BlockSpec((1, r, w), per_g), pl.BlockSpec((1, w, n2), per_g)],
        out_specs=pl.BlockSpec((1, r, n2), per_g),
        out_shape=jax.ShapeDtypeStruct((g, r, n2), F32),
        compiler_params=pltpu.CompilerParams(dimension_semantics=("parallel",), vmem_limit_bytes=VMEM_LIMIT),
        name="ssm_chunk_state",
    )(ug, pt)


def _ssm(ug, ug_meta, kpad, pt, qt, a1, a2, *, n_chunks, bsz):
    g, r, w = ug.shape
    n2 = 2 * SSM_N
    per_g = lambda gi: (gi, 0, 0)
    params = pltpu.CompilerParams(dimension_semantics=("parallel",), vmem_limit_bytes=VMEM_LIMIT)
    s = _ssm_chunk_state(ug, pt)
    x0 = _ssm_chunk_state(ug_meta, pt[:, w - ug_meta.shape[2]:, :])
    x = pl.pallas_call(
        functools.partial(_ssm_scan_kernel, n_chunks=n_chunks, bsz=bsz),
        out_shape=jax.ShapeDtypeStruct((g, r, n2), F32),
        compiler_params=pltpu.CompilerParams(vmem_limit_bytes=VMEM_LIMIT),
        name="ssm_chunk_scan",
    )(s, x0, a1, a2)
    return pl.pallas_call(
        _ssm_out_kernel,
        grid=(g,),
        in_specs=[pl.BlockSpec((1, r, w), per_g), pl.BlockSpec((1, r, n2), per_g),
                  pl.BlockSpec((1, SSM_CG, 2 * w), per_g), pl.BlockSpec((1, n2, w), per_g)],
        out_specs=pl.BlockSpec((1, r, w), per_g),
        out_shape=jax.ShapeDtypeStruct((g, r, w), BF16),
        scratch_shapes=[pltpu.VMEM((w, w), BF16)],
        compiler_params=params,
        name="ssm_output",
    )(ug, x, kpad, qt)


def _mix_kernel(x_ref, gi_ref, bi_ref, att_ref, y_ref, wglu_ref, bglu_ref, wout_ref, g1_ref, b1_ref,
                wr_ref, br_ref, h1_ref, idx_ref, gate_ref, rank_ref, cnt_ref):
    h0 = _layer_norm(x_ref[...], gi_ref[...], bi_ref[...])
    y = y_ref[...].astype(F32)
    y = y * (0.5 * (1.0 + jnp.tanh(math.sqrt(2.0 / math.pi) * (y + 0.044715 * (y * y * y)))))
    y = y * jax.nn.sigmoid(_dot(y.astype(BF16), wglu_ref[...]) + bglu_ref[...])
    mix = _dot(att_ref[...], wout_ref[0:ATT_W, :]) + _dot(y.astype(BF16), wout_ref[ATT_W:, :])
    h1 = _layer_norm(DEEPNORM_ALPHA * h0 + mix, g1_ref[...], b1_ref[...])
    h1_ref[...] = h1

    logits = _dot_nt(wr_ref[...], h1, precision=lax.Precision.HIGHEST) + br_ref[...]
    tm = logits.shape[1]
    eidx = lax.broadcasted_iota(I32, logits.shape, 0)
    vals, hots = [], []
    rest = logits
    for _ in range(TOP_K):
        mx = jnp.max(rest, axis=0, keepdims=True)
        first = jnp.min(jnp.where(rest == mx, eidx, N_EXPERTS), axis=0, keepdims=True)
        hot = eidx == first
        vals.append(mx)
        hots.append(hot)
        rest = jnp.where(hot, -jnp.inf, rest)
    exps = [jnp.exp(v - vals[0]) for v in vals]
    denom = exps[0] + exps[1] + exps[2] + exps[3]
    gate_ref[...] = jnp.concatenate([e / denom for e in exps], axis=0)
    idx_ref[...] = jnp.concatenate(
        [jnp.sum(jnp.where(h, eidx, 0), axis=0, keepdims=True) for h in hots], axis=0)

    hot_all = (hots[0] | hots[1] | hots[2] | hots[3]).astype(F32)
    sub = ROW_TILE
    tri = (lax.broadcasted_iota(I32, (sub, sub), 0) < lax.broadcasted_iota(I32, (sub, sub), 1)).astype(BF16)
    for part in range(tm // sub):
        cols = slice(part * sub, (part + 1) * sub)
        before = _dot(hot_all[:, cols].astype(BF16), tri)
        rank_ref[:, cols] = jnp.concatenate(
            [jnp.sum(jnp.where(h[:, cols], before, 0.0), axis=0, keepdims=True) for h in hots], axis=0).astype(I32)
        cnt_ref[part] = jnp.broadcast_to(jnp.sum(hot_all[:, cols], axis=1, keepdims=True), cnt_ref.shape[1:])


def _mix(x2, gi, bi, att, y, wglu, bglu, wout, g1, b1, wr_t, br):
    t, d = x2.shape
    tm = MIX_TILE
    row = lambda i: (i, 0)
    col = lambda i: (0, i)
    const = lambda i: (0, 0)
    return pl.pallas_call(
        _mix_kernel,
        grid=(t // tm,),
        in_specs=[
            pl.BlockSpec((tm, d), row), pl.BlockSpec((1, d), const), pl.BlockSpec((1, d), const),
            pl.BlockSpec((tm, ATT_W), row), pl.BlockSpec((tm, SSM_W), row),
            pl.BlockSpec((SSM_W, SSM_W), const), pl.BlockSpec((1, SSM_W), const),
            pl.BlockSpec((d, d), const), pl.BlockSpec((1, d), const), pl.BlockSpec((1, d), const),
            pl.BlockSpec((N_EXPERTS, d), const), pl.BlockSpec((N_EXPERTS, 1), const),
        ],
        out_specs=[
            pl.BlockSpec((tm, d), row),
            pl.BlockSpec((TOP_K, tm), col), pl.BlockSpec((TOP_K, tm), col), pl.BlockSpec((TOP_K, tm), col),
            pl.BlockSpec((tm // ROW_TILE, N_EXPERTS, 128), lambda i: (i, 0, 0)),
        ],
        out_shape=[
            jax.ShapeDtypeStruct((t, d), F32),
            jax.ShapeDtypeStruct((TOP_K, t), I32), jax.ShapeDtypeStruct((TOP_K, t), F32),
            jax.ShapeDtypeStruct((TOP_K, t), I32),
            jax.ShapeDtypeStruct((t // ROW_TILE, N_EXPERTS, 128), F32),
        ],
        compiler_params=pltpu.CompilerParams(dimension_semantics=("parallel",), vmem_limit_bytes=VMEM_LIMIT),
        name="mix_ln1_router",
    )(x2, gi, bi, att, y, wglu, bglu, wout, g1, b1, wr_t, br)


_HI_BITS = -65536


def _pack_halves(x):
    half = x.shape[1] // 2
    lo = lax.bitcast_convert_type(x[:, :half], I32)
    hi = lax.bitcast_convert_type(x[:, half:], I32)
    return lax.shift_right_logical(lo, 16) | (hi & _HI_BITS)


def _unpack_halves(w):
    lo = lax.bitcast_convert_type(lax.shift_left(w, 16), F32)
    hi = lax.bitcast_convert_type(w & _HI_BITS, F32)
    return lo.astype(BF16), hi.astype(BF16)


def _group_copy(src, src_group, dst, dst_group, sem):
    return pltpu.make_async_copy(src.at[pl.ds(pl.multiple_of(src_group * GROUP, GROUP), GROUP), :],
                                 dst.at[pl.ds(pl.multiple_of(dst_group * GROUP, GROUP), GROUP), :], sem)


def _batched(count, fn):
    def body(q, c):
        fn(q * COPY_BATCH, COPY_BATCH)
        return c

    whole = count // COPY_BATCH
    lax.fori_loop(0, whole, body, 0)
    rest = count - whole * COPY_BATCH
    n, done = COPY_BATCH // 2, whole * COPY_BATCH
    while n >= 1:
        take = (rest // n) % 2

        @pl.when(take == 1)
        def _(n=n, done=done):
            fn(done, n)

        done = done + take * n
        n //= 2


def _dispatch_kernel(ngroups_ref, pad_start_ref, pad_count_ref, n_used_ref, dstg_ref, pos_ref, h1_ref, xs_hbm,
                     sorted_ref, zero_ref, sems, zsem):
    step = pl.program_id(0)
    n_steps = pl.num_programs(0)
    slot = step % 2
    n_blocks = xs_hbm.shape[0] // MOE_BLOCK

    def zero_block(blk):
        return pltpu.make_async_copy(zero_ref, xs_hbm.at[pl.ds(blk * MOE_BLOCK, MOE_BLOCK), :], zsem)

    @pl.when(step == 0)
    def _():
        zero_ref[...] = jnp.zeros(zero_ref.shape, I32)
        for e in range(N_EXPERTS):
            start = pad_start_ref[e]
            count = pad_count_ref[e]

            def fill(j, c):
                _group_copy(zero_ref, 0, xs_hbm, start + j, zsem).start()
                return c

            def drain(j, c):
                _group_copy(zero_ref, 0, xs_hbm, start, zsem).wait()
                return c

            lax.fori_loop(0, count, fill, 0)
            lax.fori_loop(0, count, drain, 0)

        def fill_block(blk, c):
            zero_block(blk).start()
            return c

        def drain_block(blk, c):
            zero_block(blk).wait()
            return c

        lax.fori_loop(n_used_ref[0], n_blocks, fill_block, 0)
        lax.fori_loop(n_used_ref[0], n_blocks, drain_block, 0)

    pos = pos_ref[...]
    rows = lax.broadcasted_iota(I32, (SORT_ROWS, pos.shape[1]), 0)
    place = rows == pos[0:1]
    for k in range(1, TOP_K):
        place = place | (rows == pos[k:k + 1])
    sorted_ref[slot] = _pack_halves(_dot(place.astype(BF16), h1_ref[...].astype(BF16)))

    _batched(ngroups_ref[step],
             lambda j, n: [_group_copy(sorted_ref.at[slot], j + r, xs_hbm, dstg_ref[0, 0, j + r],
                                       sems.at[slot]).start() for r in range(n)])

    def wait_tile(s, count):
        _batched(count, lambda j, n: pltpu.make_async_copy(
            sorted_ref.at[s, pl.ds(0, n * GROUP), :], xs_hbm.at[pl.ds(0, n * GROUP), :], sems.at[s]).wait())

    @pl.when(step > 0)
    def _():
        wait_tile(1 - slot, ngroups_ref[jnp.maximum(step - 1, 0)])

    @pl.when(step == n_steps - 1)
    def _():
        wait_tile(slot, ngroups_ref[step])


def _dispatch(ngroups, pad_start, pad_count, n_used, dstg, pos_t, h1, n_rows):
    t, d = h1.shape
    tm = ROW_TILE
    grid_spec = pltpu.PrefetchScalarGridSpec(
        num_scalar_prefetch=4,
        grid=(t // tm,),
        in_specs=[
            pl.BlockSpec((1, 1, SORT_GROUPS), lambda i, *_: (i, 0, 0), memory_space=pltpu.SMEM),
            pl.BlockSpec((TOP_K, tm), lambda i, *_: (0, i)),
            pl.BlockSpec((tm, d), lambda i, *_: (i, 0)),
        ],
        out_specs=pl.BlockSpec(memory_space=pl.ANY),
        scratch_shapes=[pltpu.VMEM((2, SORT_ROWS, d // 2), I32), pltpu.VMEM((MOE_BLOCK, d // 2), I32),
                        pltpu.SemaphoreType.DMA((2,)), pltpu.SemaphoreType.DMA],
    )
    return pl.pallas_call(
        _dispatch_kernel,
        grid_spec=grid_spec,
        out_shape=jax.ShapeDtypeStruct((n_rows, d // 2), I32),
        compiler_params=pltpu.CompilerParams(dimension_semantics=("arbitrary",), vmem_limit_bytes=VMEM_LIMIT,
                                             has_side_effects=True),
        name="moe_dispatch",
    )(ngroups, pad_start, pad_count, n_used, dstg, pos_t, h1)


def _expert_kernel(be_ref, nb_ref, x_ref, wg_ref, bg_ref, wu_ref, bu_ref, wd_ref, bd_ref, y_ref,
                   wg_b, wu_b, wd_b):
    i = pl.program_id(0)

    @pl.when(i < nb_ref[0])
    def _():
        @pl.when((i == 0) | (be_ref[i] != be_ref[jnp.maximum(i - 1, 0)]))
        def _():
            wg_b[...] = wg_ref[0].astype(BF16)
            wu_b[...] = wu_ref[0].astype(BF16)
            wd_b[...] = wd_ref[0].astype(BF16)

        x_lo, x_hi = _unpack_halves(x_ref[...])
        half = x_lo.shape[1]
        gate = _dot(x_lo, wg_b[0:half, :]) + _dot(x_hi, wg_b[half:, :]) + bg_ref[0]
        up = _dot(x_lo, wu_b[0:half, :]) + _dot(x_hi, wu_b[half:, :]) + bu_ref[0]
        gate = jnp.minimum(gate, SWIGLU_LIMIT)
        up = jnp.clip(up, -SWIGLU_LIMIT, SWIGLU_LIMIT)
        act = (up + 1.0) * gate * jax.nn.sigmoid(gate * SWIGLU_ALPHA)
        y = _dot(act.astype(BF16), wd_b[...]) + bd_ref[0]
        y_ref[...] = _pack_halves(y.astype(BF16).astype(F32))

    @pl.when(i >= nb_ref[0])
    def _():
        y_ref[...] = jnp.zeros(y_ref.shape, I32)


def _experts(block_expert, n_used, xs, wg, bg, wu, bu, wd, bd):
    n_rows, packed = xs.shape
    d, dff = wg.shape[1], wg.shape[2]
    assert packed * 2 == d and wd.shape[2] == d
    nb = n_rows // MOE_BLOCK
    blk = lambda i, be, nu: (jnp.minimum(i, nu[0] - 1), 0)
    wsel = lambda i, be, nu: (be[jnp.minimum(i, nu[0] - 1)], 0, 0)
    grid_spec = pltpu.PrefetchScalarGridSpec(
        num_scalar_prefetch=2,
        grid=(nb,),
        in_specs=[
            pl.BlockSpec((MOE_BLOCK, packed), blk),
            pl.BlockSpec((1, d, dff), wsel), pl.BlockSpec((1, 1, dff), wsel),
            pl.BlockSpec((1, d, dff), wsel), pl.BlockSpec((1, 1, dff), wsel),
            pl.BlockSpec((1, dff, d), wsel), pl.BlockSpec((1, 1, d), wsel),
        ],
        out_specs=pl.BlockSpec((MOE_BLOCK, packed), lambda i, be, nu: (i, 0)),
        scratch_shapes=[pltpu.VMEM((d, dff), BF16), pltpu.VMEM((d, dff), BF16), pltpu.VMEM((dff, d), BF16)],
    )
    return pl.pallas_call(
        _expert_kernel,
        grid_spec=grid_spec,
        out_shape=jax.ShapeDtypeStruct((n_rows, packed), I32),
        compiler_params=pltpu.CompilerParams(dimension_semantics=("arbitrary",), vmem_limit_bytes=VMEM_LIMIT),
        name="moe_experts",
    )(block_expert, n_used, xs, wg, bg, wu, bu, wd, bd)


def _combine_kernel(ngroups_ref, dstg_ref, dstg_next_ref, pos_ref, gate_ref, h1_ref, ys_hbm, g2_ref, b2_ref,
                    o_ref, buf, sems):
    step = pl.program_id(0)
    n_steps = pl.num_programs(0)
    slot = step % 2

    def fetch(dref, s, j):
        return _group_copy(ys_hbm, dref[0, 0, j], buf.at[s], j, sems.at[s])

    def issue(dref, s, count):
        _batched(count, lambda j, n: [fetch(dref, s, j + r).start() for r in range(n)])

    @pl.when(step == 0)
    def _():
        buf[...] = jnp.zeros(buf.shape, I32)
        issue(dstg_ref, 0, ngroups_ref[0])

    @pl.when(step + 1 < n_steps)
    def _():
        issue(dstg_next_ref, 1 - slot, ngroups_ref[jnp.minimum(step + 1, n_steps - 1)])

    _batched(ngroups_ref[step], lambda j, n: pltpu.make_async_copy(
        ys_hbm.at[pl.ds(0, n * GROUP), :], buf.at[slot, pl.ds(0, n * GROUP), :], sems.at[slot]).wait())

    pos = pos_ref[...]
    gate = gate_ref[...]
    lanes = lax.broadcasted_iota(I32, (pos.shape[0], SORT_ROWS), 1)
    sel = jnp.where(lanes == pos[:, 0:1], gate[:, 0:1], 0.0)
    for k in range(1, TOP_K):
        sel = sel + jnp.where(lanes == pos[:, k:k + 1], gate[:, k:k + 1], 0.0)
    sel_hi = sel.astype(BF16)
    sel_lo = (sel - sel_hi.astype(F32)).astype(BF16)
    y_lo, y_hi = _unpack_halves(buf[slot])
    ffn = jnp.concatenate([_dot(sel_hi, y_lo) + _dot(sel_lo, y_lo), _dot(sel_hi, y_hi) + _dot(sel_lo, y_hi)], axis=1)
    o_ref[...] = _layer_norm(DEEPNORM_ALPHA * h1_ref[...] + ffn, g2_ref[...], b2_ref[...])


def _combine(ngroups, dstg, pos_c, gates, h1, ys, g2, b2):
    t, d = h1.shape
    tm = ROW_TILE
    n_steps = t // tm
    row = lambda i, ng: (i, 0)
    const = lambda i, ng: (0, 0)
    grid_spec = pltpu.PrefetchScalarGridSpec(
        num_scalar_prefetch=1,
        grid=(n_steps,),
        in_specs=[
            pl.BlockSpec((1, 1, SORT_GROUPS), lambda i, ng: (i, 0, 0), memory_space=pltpu.SMEM),
            pl.BlockSpec((1, 1, SORT_GROUPS), lambda i, ng: (jnp.minimum(i + 1, n_steps - 1), 0, 0),
                         memory_space=pltpu.SMEM),
            pl.BlockSpec((tm, TOP_K), row), pl.BlockSpec((tm, TOP_K), row), pl.BlockSpec((tm, d), row),
            pl.BlockSpec(memory_space=pl.ANY),
            pl.BlockSpec((1, d), const), pl.BlockSpec((1, d), const),
        ],
        out_specs=pl.BlockSpec((tm, d), row),
        scratch_shapes=[pltpu.VMEM((2, SORT_ROWS, d // 2), I32), pltpu.SemaphoreType.DMA((2,))],
    )
    return pl.pallas_call(
        _combine_kernel,
        grid_spec=grid_spec,
        out_shape=jax.ShapeDtypeStruct((t, d), F32),
        compiler_params=pltpu.CompilerParams(dimension_semantics=("arbitrary",), vmem_limit_bytes=VMEM_LIMIT),
        name="moe_combine_ln2",
    )(ngroups, dstg, dstg, pos_c, gates, h1, ys, g2, b2)


def kernel(x, meta_tokens, ln_in_g, ln_in_b, rel_bias, w_in, lambda_q1, lambda_k1, lambda_q2, lambda_k2,
           subln_g, a_re, a_im, log_step, b_re, b_im, c_re, c_im, d_skip, w_glu, b_glu, w_out, ln1_g, ln1_b,
           w_router, b_router, w_gate, b_gate, w_up, b_up, w_down, b_down, ln2_g, ln2_b):
    bsz, seq, dm = x.shape
    assert seq % 512 == 0 and w_in.shape[0] == DEPTH == 1
    layer = 0
    row2 = lambda v: v.astype(F32).reshape(1, -1)

    w_in_b = w_in[layer].astype(BF16)
    gi, bi = row2(ln_in_g), row2(ln_in_b)
    q, k, vt, u = _inproj(x, gi, bi, w_in_b, tm=512, kv_tile=KV_TILE)
    meta = jnp.zeros((1, META_PAD, dm), x.dtype).at[0, :N_META].set(meta_tokens.astype(x.dtype))
    _, k_meta, vt_meta, u_meta = _inproj(meta, gi, bi, w_in_b, tm=META_PAD, kv_tile=META_PAD)

    lam_init = 0.8 - 0.6 * math.exp(-0.3 * layer)
    lam = (jnp.exp(jnp.sum(lambda_q1[layer].astype(F32) * lambda_k1[layer].astype(F32)))
           - jnp.exp(jnp.sum(lambda_q2[layer].astype(F32) * lambda_k2[layer].astype(F32))) + lam_init)
    att = _attention(lam.reshape(1), q, k, vt, k_meta[0], vt_meta[0, :, 0], _near_bias(rel_bias),
                     subln_g[layer].astype(F32).reshape(HEAD_W, 1), lam_init=lam_init)

    kpad, pt, qt, a1, a2 = _ssm_tables(a_re[layer], a_im[layer], log_step[layer], b_re[layer], b_im[layer],
                                     c_re[layer], c_im[layer], d_skip[layer])
    n_chunks = seq // SSM_T
    assert (n_chunks * bsz) % 16 == 0
    ug = jnp.transpose(u.reshape(bsz, n_chunks, SSM_T, SSM_G, SSM_CG), (3, 1, 0, 2, 4))
    ug = ug.reshape(SSM_G, n_chunks * bsz, SSM_T * SSM_CG)
    lead = jnp.zeros((8, N_META, SSM_W), BF16).at[0].set(u_meta[0, :N_META])
    ug_meta = jnp.transpose(lead.reshape(8, N_META, SSM_G, SSM_CG), (2, 0, 1, 3)).reshape(SSM_G, 8, N_META * SSM_CG)
    yg = _ssm(ug, ug_meta, kpad, pt, qt, a1, a2, n_chunks=n_chunks, bsz=bsz)
    y_ssm = jnp.transpose(yg.reshape(SSM_G, n_chunks, bsz, SSM_T, SSM_CG), (2, 1, 3, 0, 4))
    y_ssm = y_ssm.reshape(bsz, seq, SSM_W)

    t = bsz * seq
    h1, idx_t, gate_t, rank_t, cnt = _mix(
        x.reshape(t, dm), gi, bi, att.reshape(t, ATT_W), y_ssm.reshape(t, SSM_W),
        w_glu[layer].astype(BF16), row2(b_glu[layer]), w_out[layer].astype(BF16),
        row2(ln1_g[layer]), row2(ln1_b[layer]),
        jnp.transpose(w_router[layer].astype(F32)), b_router[layer].astype(F32).reshape(N_EXPERTS, 1))

    n_tiles = t // ROW_TILE
    experts = jnp.arange(N_EXPERTS, dtype=I32)
    tiles = jnp.arange(n_tiles, dtype=I32)
    tile_cnt = jnp.max(cnt, axis=-1).astype(I32)
    run = (tile_cnt + GROUP - 1) // GROUP * GROUP
    run_off = jnp.sum(jnp.where((experts[:, None] < experts[None, :])[None], run[:, :, None], 0), axis=1)
    run_before = jnp.sum(jnp.where((tiles[:, None] < tiles[None, :])[:, :, None], run[:, None, :], 0), axis=0)
    counts = jnp.sum(run, axis=0)
    padded = (counts + MOE_BLOCK - 1) // MOE_BLOCK * MOE_BLOCK
    padded_end = jnp.sum(jnp.where(experts[:, None] <= experts[None, :], padded[:, None], 0), axis=0)
    padded_start = padded_end - padded
    n_blocks = (t * TOP_K + n_tiles * N_EXPERTS * (GROUP - 1)) // MOE_BLOCK + N_EXPERTS
    block_row0 = jnp.arange(n_blocks, dtype=I32) * MOE_BLOCK
    block_expert = jnp.minimum(jnp.sum((padded_end[None, :] <= block_row0[:, None]).astype(I32), axis=1),
                               N_EXPERTS - 1)
    n_used = (padded_end[-1:] // MOE_BLOCK).astype(I32)
    ngroups = jnp.sum(run, axis=1) // GROUP
    local_row = jnp.arange(SORT_GROUPS, dtype=I32) * GROUP
    owner = jnp.minimum(jnp.sum(((run_off + run)[:, None, :] <= local_row[None, :, None]).astype(I32), axis=-1),
                        N_EXPERTS - 1)
    shift = (padded_start[None, :] + run_before - run_off) // GROUP
    dstg = jnp.sum(jnp.where(owner[..., None] == experts, shift[:, None, :], 0), axis=-1) + local_row // GROUP
    dstg = jnp.where(local_row[None, :] // GROUP < ngroups[:, None], dstg, 0).reshape(n_tiles, 1, SORT_GROUPS)
    run_off_tok = jnp.repeat(run_off, ROW_TILE, axis=0)
    pos_t = jnp.sum(jnp.where(idx_t[..., None] == experts, run_off_tok[None], 0), axis=-1) + rank_t

    xs = _dispatch(ngroups, ((padded_start + counts) // GROUP).astype(I32),
                   ((padded - counts) // GROUP).astype(I32), n_used, dstg, pos_t, h1, n_blocks * MOE_BLOCK)
    b3 = lambda v: v.astype(F32)[:, None, :]
    ys = _experts(block_expert, n_used, xs, w_gate[layer], b3(b_gate[layer]), w_up[layer], b3(b_up[layer]),
                  w_down[layer], b3(b_down[layer]))
    out = _combine(ngroups, dstg, jnp.transpose(pos_t), jnp.transpose(gate_t), h1, ys,
                   row2(ln2_g[layer]), row2(ln2_b[layer]))
    return out.reshape(bsz, seq, dm)
```

```python
import functools
import math

import jax
import jax.numpy as jnp
import numpy as np
from jax import lax
from jax.experimental import pallas as pl
from jax.experimental.pallas import tpu as pltpu

F32 = jnp.float32
BF16 = jnp.bfloat16
I32 = jnp.int32

DEPTH = 1
N_META = 16
CHUNK = 64
ATT_HEADS = 4
HEAD_DIM = 64
HEAD_W = 2 * HEAD_DIM
ATT_W = ATT_HEADS * HEAD_W
SSM_W = 512
SSM_CG = 16
SSM_G = SSM_W // SSM_CG
SSM_N = 64
N_BUCKETS = 32
MAX_DISTANCE = 128
N_EXPERTS = 32
TOP_K = 4
SWIGLU_LIMIT = 7.0
SWIGLU_ALPHA = 1.702
LN_EPS = 1e-5
NEG_INF = -1e30
DEEPNORM_ALPHA = (2.0 * DEPTH) ** 0.25
LOG2E = 1.4426950408889634

Q_TILE = 512
KV_TILE = 512
V_ONES = 16
V_ROWS = HEAD_W + V_ONES
META_PAD = 128
SSM_T = 64
MIX_TILE = 512
ROW_TILE = 256
MOE_BLOCK = 512
GROUP = 8
COPY_BATCH = 16
SORT_ROWS = 1280
SORT_GROUPS = SORT_ROWS // GROUP
assert SORT_ROWS >= ROW_TILE * TOP_K + N_EXPERTS * (GROUP - 1)
VMEM_LIMIT = 56 * 1024 * 1024


def _layer_norm(x, g, b):
    mu = jnp.mean(x, axis=-1, keepdims=True)
    xc = x - mu
    var = jnp.mean(xc * xc, axis=-1, keepdims=True)
    return xc * lax.rsqrt(var + LN_EPS) * g + b


def _dot(a, b):
    return jnp.dot(a, b, preferred_element_type=F32)


def _dot_nt(a, b, precision=None):
    return lax.dot_general(a, b, (((1,), (1,)), ((), ())), precision=precision,
                           preferred_element_type=F32)


def _inproj_kernel(x_ref, g_ref, b_ref, w_ref, q_ref, k_ref, vt_ref, u_ref, *, kv_tile, q_scale):
    h = _layer_norm(x_ref[0], g_ref[...], b_ref[...]).astype(BF16)
    tm = h.shape[0]
    q_ref[0] = (_dot(h, w_ref[:, 0:ATT_W]) * q_scale).astype(BF16)
    k_ref[0] = _dot(h, w_ref[:, ATT_W:2 * ATT_W]).astype(BF16)
    v = _dot(h, w_ref[:, 2 * ATT_W:3 * ATT_W])
    ones = jnp.ones((V_ONES, kv_tile), BF16)
    for hh in range(ATT_HEADS):
        vt = v[:, hh * HEAD_W:(hh + 1) * HEAD_W].T.astype(BF16)
        for j in range(tm // kv_tile):
            vt_ref[0, hh, j, 0:HEAD_W, :] = vt[:, j * kv_tile:(j + 1) * kv_tile]
            vt_ref[0, hh, j, HEAD_W:V_ROWS, :] = ones
    u_ref[0] = _dot(h, w_ref[:, 3 * ATT_W:]).astype(BF16)


def _inproj(x, g, b, w, *, tm, kv_tile):
    bsz, s, d = x.shape
    q_scale = HEAD_DIM ** -0.5 * LOG2E
    n_cols = w.shape[1]
    row = lambda bi, i: (bi, i, 0)
    return pl.pallas_call(
        functools.partial(_inproj_kernel, kv_tile=kv_tile, q_scale=q_scale),
        grid=(bsz, s // tm),
        in_specs=[
            pl.BlockSpec((1, tm, d), row),
            pl.BlockSpec((1, d), lambda bi, i: (0, 0)),
            pl.BlockSpec((1, d), lambda bi, i: (0, 0)),
            pl.BlockSpec((d, n_cols), lambda bi, i: (0, 0)),
        ],
        out_specs=[
            pl.BlockSpec((1, tm, ATT_W), row),
            pl.BlockSpec((1, tm, ATT_W), row),
            pl.BlockSpec((1, ATT_HEADS, tm // kv_tile, V_ROWS, kv_tile), lambda bi, i: (bi, 0, i, 0, 0)),
            pl.BlockSpec((1, tm, SSM_W), row),
        ],
        out_shape=[
            jax.ShapeDtypeStruct((bsz, s, ATT_W), BF16),
            jax.ShapeDtypeStruct((bsz, s, ATT_W), BF16),
            jax.ShapeDtypeStruct((bsz, ATT_HEADS, s // kv_tile, V_ROWS, kv_tile), BF16),
            jax.ShapeDtypeStruct((bsz, s, SSM_W), BF16),
        ],
        compiler_params=pltpu.CompilerParams(
            dimension_semantics=("parallel", "parallel"), vmem_limit_bytes=VMEM_LIMIT),
        name="inproj",
    )(x, g, b, w)


def _attn_kernel(lam_ref, q_ref, k_ref, vt_ref, km_ref, vtm_ref, bias_ref, g_ref, o_ref,
                 m_ref, acc_ref, qz_ref, sa_ref, sb_ref, sn_ref, *, out_scale):
    i = pl.program_id(2)
    qt = q_ref[0].astype(F32).T
    feat = lax.broadcasted_iota(I32, qt.shape, 0)
    qz_ref[:, 0:Q_TILE] = jnp.where(feat < HEAD_DIM, qt, 0.0).astype(BF16)
    qz_ref[:, Q_TILE:] = jnp.where(feat >= HEAD_DIM, qt, 0.0).astype(BF16)

    m_ref[...] = jnp.full(m_ref.shape, NEG_INF, F32)
    acc_ref[...] = jnp.zeros(acc_ref.shape, F32)

    def scores(kt):
        return _dot(kt, qz_ref[...])

    def k_tile(t):
        return k_ref[0, pl.ds(pl.multiple_of(t * KV_TILE, KV_TILE), KV_TILE), :]

    def absorb(s, pv):
        m_old = m_ref[...]
        m_new = jnp.maximum(m_old, jnp.max(s, axis=0, keepdims=True))
        alpha = jnp.exp2(m_old - m_new)
        p = jnp.exp2(s - m_new).astype(BF16)
        acc_ref[...] = acc_ref[...] * alpha + pv(p)
        m_ref[...] = m_new

    def tile_pv(t):
        return lambda p: _dot(vt_ref[0, 0, t], p)

    n_far = jnp.maximum(i - 1, 0)
    peel = n_far % 2

    sn_ref[...] = scores(km_ref[...])
    sa_ref[...] = scores(k_tile(0))
    absorb(sn_ref[...] + bias_ref[0, 0, 2 * KV_TILE:, :], lambda p: _dot(vtm_ref[0], p))

    @pl.when(peel == 1)
    def _():
        sb_ref[...] = scores(k_tile(1))
        absorb(sa_ref[...], tile_pv(0))
        sa_ref[...] = sb_ref[...]

    def far_pair(j, carry):
        t0 = peel + 2 * j
        s_cur = sa_ref[...]
        sb_ref[...] = scores(k_tile(t0 + 1))
        absorb(s_cur, tile_pv(t0))
        s_cur = sb_ref[...]
        sa_ref[...] = scores(k_tile(t0 + 2))
        absorb(s_cur, tile_pv(t0 + 1))
        return carry

    lax.fori_loop(0, n_far // 2, far_pair, 0)

    sb_ref[...] = scores(k_tile(i))
    absorb(sa_ref[...] + bias_ref[0, 0, 0:KV_TILE, :], tile_pv(n_far))
    absorb(sb_ref[...] + bias_ref[0, 0, KV_TILE:2 * KV_TILE, :], tile_pv(i))

    acc = acc_ref[...]
    lam = lam_ref[0]
    o1 = acc[0:HEAD_W, 0:Q_TILE] / acc[HEAD_W:HEAD_W + 1, 0:Q_TILE]
    o2 = acc[0:HEAD_W, Q_TILE:] / acc[HEAD_W:HEAD_W + 1, Q_TILE:]
    o = o1 - lam * o2
    ms = jnp.mean(o * o, axis=0, keepdims=True)
    o = o * lax.rsqrt(ms + LN_EPS) * g_ref[...] * out_scale
    o_ref[0] = o.T.astype(o_ref.dtype)


def _attention(lam, q, k, vt, k_meta, vt_meta, bias, subln_g, *, lam_init):
    bsz, s, _ = q.shape
    nq = s // Q_TILE
    n_near = META_PAD + 2 * KV_TILE
    grid_spec = pltpu.PrefetchScalarGridSpec(
        num_scalar_prefetch=1,
        grid=(bsz, ATT_HEADS, nq),
        in_specs=[
            pl.BlockSpec((1, Q_TILE, HEAD_W), lambda b, h, i, lam: (b, i, h)),
            pl.BlockSpec((1, s, HEAD_W), lambda b, h, i, lam: (b, 0, h)),
            pl.BlockSpec((1, 1, s // KV_TILE, V_ROWS, KV_TILE), lambda b, h, i, lam: (b, h, 0, 0, 0)),
            pl.BlockSpec((META_PAD, HEAD_W), lambda b, h, i, lam: (0, h)),
            pl.BlockSpec((1, V_ROWS, META_PAD), lambda b, h, i, lam: (h, 0, 0)),
            pl.BlockSpec((1, 1, n_near, 2 * Q_TILE), lambda b, h, i, lam: (h, jnp.minimum(i, 1), 0, 0)),
            pl.BlockSpec((HEAD_W, 1), lambda b, h, i, lam: (0, 0)),
        ],
        out_specs=pl.BlockSpec((1, Q_TILE, HEAD_W), lambda b, h, i, lam: (b, i, h)),
        scratch_shapes=[
            pltpu.VMEM((1, 2 * Q_TILE), F32), pltpu.VMEM((V_ROWS, 2 * Q_TILE), F32),
            pltpu.VMEM((HEAD_W, 2 * Q_TILE), BF16),
            pltpu.VMEM((KV_TILE, 2 * Q_TILE), F32), pltpu.VMEM((KV_TILE, 2 * Q_TILE), F32),
            pltpu.VMEM((META_PAD, 2 * Q_TILE), F32),
        ],
    )
    return pl.pallas_call(
        functools.partial(_attn_kernel, out_scale=1.0 - lam_init),
        grid_spec=grid_spec,
        out_shape=jax.ShapeDtypeStruct((bsz, s, ATT_W), BF16),
        compiler_params=pltpu.CompilerParams(
            dimension_semantics=("parallel", "parallel", "arbitrary"), vmem_limit_bytes=VMEM_LIMIT),
        name="diff_attention",
    )(lam, q, k, vt, k_meta, vt_meta, bias, subln_g)


def _t5_bucket(rel):
    nb = N_BUCKETS // 2
    max_exact = nb // 2
    ret = jnp.where(rel > 0, nb, 0)
    n = jnp.abs(rel)
    n_f = jnp.maximum(n, 1).astype(F32)
    large = max_exact + (jnp.log(n_f / max_exact) / math.log(MAX_DISTANCE / max_exact)
                         * (nb - max_exact)).astype(I32)
    large = jnp.minimum(large, nb - 1)
    return ret + jnp.where(n < max_exact, n, large)


def _near_bias(rel_bias):
    table = rel_bias.astype(F32)
    far = table[N_BUCKETS // 2 - 1]
    c = jnp.arange(Q_TILE, dtype=I32)[None, :]
    r = jnp.arange(KV_TILE, dtype=I32)[:, None]

    shifted = jnp.transpose(table - far) * LOG2E

    def bias_of(offset, rows, cols=Q_TILE):
        rel = jnp.arange(rows, dtype=I32)[:, None] - c[:, :cols] + offset
        hot = _t5_bucket(rel)[None, :, :, None] == jnp.arange(N_BUCKETS, dtype=I32)
        return jnp.sum(jnp.where(hot, shifted[:, None, None, :], 0.0), axis=-1)

    own = jnp.where((r // CHUNK <= c // CHUNK)[None], bias_of(0, KV_TILE), NEG_INF)
    near = MAX_DISTANCE
    prev = jnp.zeros((ATT_HEADS, KV_TILE, Q_TILE), F32).at[:, KV_TILE - near:, :near].set(bias_of(-near, near, near))
    rm = jnp.arange(META_PAD, dtype=I32)[:, None]
    meta_ok = (rm < N_META)[None]
    meta0 = jnp.where(meta_ok, jnp.pad(bias_of(-N_META, N_META), ((0, 0), (0, META_PAD - N_META), (0, 0))), NEG_INF)
    meta1 = jnp.where(meta_ok, jnp.zeros((ATT_HEADS, META_PAD, Q_TILE), F32), NEG_INF)
    assert N_META + Q_TILE - (N_META - 1) >= MAX_DISTANCE and KV_TILE >= near and Q_TILE >= near
    v0 = jnp.concatenate([jnp.full_like(prev, NEG_INF), own, meta0], axis=1)
    v1 = jnp.concatenate([prev, own, meta1], axis=1)
    both = jnp.stack([v0, v1], axis=1)
    return jnp.concatenate([both, both], axis=-1)


def _ssm_tables(a_re, a_im, log_step, b_re, b_im, c_re, c_im, d_skip):
    hi = lax.Precision.HIGHEST
    t_len = SSM_T
    step = jnp.exp(log_step.astype(F32))[:, None]
    ar = jnp.minimum(a_re.astype(F32), -1e-4)
    ai = a_im.astype(F32)
    mag = jnp.exp(step * ar)
    ph = step * ai
    abar_re = mag * jnp.cos(ph)
    abar_im = mag * jnp.sin(ph)
    den = ar * ar + ai * ai
    e_re = abar_re - 1.0
    e_im = abar_im
    f_re = (e_re * ar + e_im * ai) / den
    f_im = (e_im * ar - e_re * ai) / den
    br = b_re.astype(F32)
    bi = b_im.astype(F32)
    bb_re = f_re[..., None] * br - f_im[..., None] * bi
    bb_im = f_re[..., None] * bi + f_im[..., None] * br
    tau = jnp.arange(t_len + 1, dtype=F32)[None, :, None]
    pmag = jnp.exp(tau * (step * ar)[:, None, :])
    pph = tau * ph[:, None, :]
    pw_re = pmag * jnp.cos(pph)
    pw_im = pmag * jnp.sin(pph)
    bbt_re = jnp.swapaxes(bb_re, 1, 2)[:, None]
    bbt_im = jnp.swapaxes(bb_im, 1, 2)[:, None]

    def times_bbar(p_re, p_im):
        return (p_re[:, :, None, :] * bbt_re - p_im[:, :, None, :] * bbt_im,
                p_re[:, :, None, :] * bbt_im + p_im[:, :, None, :] * bbt_re)

    cr = c_re.astype(F32)
    ci = c_im.astype(F32)
    bi_re = jnp.swapaxes(bb_re, 1, 2)[:, :, None, :]
    bi_im = jnp.swapaxes(bb_im, 1, 2)[:, :, None, :]
    cb_re = cr[:, None] * bi_re - ci[:, None] * bi_im
    cb_im = cr[:, None] * bi_im + ci[:, None] * bi_re
    kern = jnp.einsum('gikm,gtm->gitk', jnp.concatenate([cb_re, -cb_im], axis=-1),
                      jnp.concatenate([pw_re[:, :t_len], pw_im[:, :t_len]], axis=-1), precision=hi)
    skip = d_skip.astype(F32).reshape(SSM_G, SSM_CG)
    kern = kern.at[:, :, 0, :].add(skip[:, None, :] * jnp.eye(SSM_CG, dtype=F32))
    kflat = kern.reshape(SSM_G, SSM_CG, t_len * SSM_CG)
    kpad = jnp.concatenate([jnp.zeros_like(kflat), kflat], axis=-1)
    back = jnp.arange(t_len - 1, -1, -1, dtype=F32)[None, :, None]
    bmag = jnp.exp(back * (step * ar)[:, None, :])
    bph = back * ph[:, None, :]
    pt_re, pt_im = times_bbar(bmag * jnp.cos(bph), bmag * jnp.sin(bph))
    pt = jnp.concatenate([pt_re, pt_im], axis=-1).reshape(SSM_G, t_len * SSM_CG, 2 * SSM_N)
    up_re = jnp.swapaxes(pw_re[:, 1:], 1, 2)[..., None]
    up_im = jnp.swapaxes(pw_im[:, 1:], 1, 2)[..., None]
    crt = jnp.swapaxes(cr, 1, 2)[:, :, None, :]
    cit = jnp.swapaxes(ci, 1, 2)[:, :, None, :]
    q_re = crt * up_re - cit * up_im
    q_im = crt * up_im + cit * up_re
    qt = jnp.concatenate([q_re, -q_im], axis=1).reshape(SSM_G, 2 * SSM_N, t_len * SSM_CG)
    at_re = pw_re[:, t_len]
    at_im = pw_im[:, t_len]
    a1 = jnp.concatenate([at_re, at_re], axis=-1)
    a2 = jnp.concatenate([-at_im, at_im], axis=-1)
    return kpad, pt.astype(BF16), qt.astype(BF16), a1, a2


def _ssm_state_kernel(u_ref, pt_ref, s_ref):
    s_ref[0] = _dot(u_ref[0], pt_ref[0])


def _ssm_scan_kernel(s_ref, x0_ref, a1_ref, a2_ref, x_ref, *, n_chunks, bsz):
    a1 = a1_ref[...][:, None, :]
    a2 = a2_ref[...][:, None, :]

    def body(c, x):
        rows = pl.ds(c * bsz, bsz)
        x_ref[:, rows, :] = x
        return a1 * x + a2 * pltpu.roll(x, SSM_N, 2) + s_ref[:, rows, :]

    x0 = jnp.broadcast_to(x0_ref[:, 0:1, :], (SSM_G, bsz, 2 * SSM_N))
    lax.fori_loop(0, n_chunks, body, x0)


def _ssm_out_kernel(u_ref, x_ref, kp_ref, qt_ref, y_ref, mt_ref):
    kp = kp_ref[0]
    width = SSM_T * SSM_CG
    lane_tile = 128
    for sub in range(0, lane_tile, SSM_CG):
        shifted = kp if sub == 0 else pltpu.roll(kp, 2 * width - sub, 1)
        for s in range(SSM_T):
            off = (SSM_T - s) * SSM_CG
            if off % lane_tile == sub:
                base = off - sub
                mt_ref[s * SSM_CG:(s + 1) * SSM_CG, :] = shifted[:, base:base + width].astype(BF16)
    x = x_ref[0]
    x_hi = x.astype(BF16)
    x_lo = (x - x_hi.astype(F32)).astype(BF16)
    y = _dot(u_ref[0], mt_ref[...]) + _dot(x_hi, qt_ref[0]) + _dot(x_lo, qt_ref[0])
    y_ref[0] = y.astype(y_ref.dtype)


def _ssm_chunk_state(ug, pt):
    g, r, w = ug.shape
    n2 = 2 * SSM_N
    per_g = lambda gi: (gi, 0, 0)
    return pl.pallas_call(
        _ssm_state_kernel,
        grid=(g,),
        in_specs=[pl.BlockSpec((1, r, w), per_g), pl.BlockSpec((1, w, n2), per_g)],
        out_specs=pl.BlockSpec((1, r, n2), per_g),
        out_shape=jax.ShapeDtypeStruct((g, r, n2), F32),
        compiler_params=pltpu.CompilerParams(dimension_semantics=("parallel",), vmem_limit_bytes=VMEM_LIMIT),
        name="ssm_chunk_state",
    )(ug, pt)


def _ssm(ug, ug_meta, kpad, pt, qt, a1, a2, *, n_chunks, bsz):
    g, r, w = ug.shape
    n2 = 2 * SSM_N
    per_g = lambda gi: (gi, 0, 0)
    params = pltpu.CompilerParams(dimension_semantics=("parallel",), vmem_limit_bytes=VMEM_LIMIT)
    s = _ssm_chunk_state(ug, pt)
    x0 = _ssm_chunk_state(ug_meta, pt[:, w - ug_meta.shape[2]:, :])
    x = pl.pallas_call(
        functools.partial(_ssm_scan_kernel, n_chunks=n_chunks, bsz=bsz),
        out_shape=jax.ShapeDtypeStruct((g, r, n2), F32),
        compiler_params=pltpu.CompilerParams(vmem_limit_bytes=VMEM_LIMIT),
        name="ssm_chunk_scan",
    )(s, x0, a1, a2)
    return pl.pallas_call(
        _ssm_out_kernel,
        grid=(g,),
        in_specs=[pl.BlockSpec((1, r, w), per_g), pl.BlockSpec((1, r, n2), per_g),
                  pl.BlockSpec((1, SSM_CG, 2 * w), per_g), pl.BlockSpec((1, n2, w), per_g)],
        out_specs=pl.BlockSpec((1, r, w), per_g),
        out_shape=jax.ShapeDtypeStruct((g, r, w), BF16),
        scratch_shapes=[pltpu.VMEM((w, w), BF16)],
        compiler_params=params,
        name="ssm_output",
    )(ug, x, kpad, qt)


def _mix_kernel(x_ref, gi_ref, bi_ref, att_ref, y_ref, wglu_ref, bglu_ref, wout_ref, g1_ref, b1_ref,
                wr_ref, br_ref, h1_ref, idx_ref, gate_ref, rank_ref, cnt_ref):
    h0 = _layer_norm(x_ref[...], gi_ref[...], bi_ref[...])
    y = y_ref[...].astype(F32)
    y = y * (0.5 * (1.0 + jnp.tanh(math.sqrt(2.0 / math.pi) * (y + 0.044715 * (y * y * y)))))
    y = y * jax.nn.sigmoid(_dot(y.astype(BF16), wglu_ref[...]) + bglu_ref[...])
    mix = _dot(att_ref[...], wout_ref[0:ATT_W, :]) + _dot(y.astype(BF16), wout_ref[ATT_W:, :])
    h1 = _layer_norm(DEEPNORM_ALPHA * h0 + mix, g1_ref[...], b1_ref[...])
    h1_ref[...] = h1

    logits = _dot_nt(wr_ref[...], h1, precision=lax.Precision.HIGHEST) + br_ref[...]
    tm = logits.shape[1]
    eidx = lax.broadcasted_iota(I32, logits.shape, 0)
    vals, hots = [], []
    rest = logits
    for _ in range(TOP_K):
        mx = jnp.max(rest, axis=0, keepdims=True)
        first = jnp.min(jnp.where(rest == mx, eidx, N_EXPERTS), axis=0, keepdims=True)
        hot = eidx == first
        vals.append(mx)
        hots.append(hot)
        rest = jnp.where(hot, -jnp.inf, rest)
    exps = [jnp.exp(v - vals[0]) for v in vals]
    denom = exps[0] + exps[1] + exps[2] + exps[3]
    gate_ref[...] = jnp.concatenate([e / denom for e in exps], axis=0)
    idx_ref[...] = jnp.concatenate(
        [jnp.sum(jnp.where(h, eidx, 0), axis=0, keepdims=True) for h in hots], axis=0)

    hot_all = (hots[0] | hots[1] | hots[2] | hots[3]).astype(F32)
    sub = ROW_TILE
    tri = (lax.broadcasted_iota(I32, (sub, sub), 0) < lax.broadcasted_iota(I32, (sub, sub), 1)).astype(BF16)
    for part in range(tm // sub):
        cols = slice(part * sub, (part + 1) * sub)
        before = _dot(hot_all[:, cols].astype(BF16), tri)
        rank_ref[:, cols] = jnp.concatenate(
            [jnp.sum(jnp.where(h[:, cols], before, 0.0), axis=0, keepdims=True) for h in hots], axis=0).astype(I32)
        cnt_ref[part] = jnp.broadcast_to(jnp.sum(hot_all[:, cols], axis=1, keepdims=True), cnt_ref.shape[1:])


def _mix(x2, gi, bi, att, y, wglu, bglu, wout, g1, b1, wr_t, br):
    t, d = x2.shape
    tm = MIX_TILE
    row = lambda i: (i, 0)
    col = lambda i: (0, i)
    const = lambda i: (0, 0)
    return pl.pallas_call(
        _mix_kernel,
        grid=(t // tm,),
        in_specs=[
            pl.BlockSpec((tm, d), row), pl.BlockSpec((1, d), const), pl.BlockSpec((1, d), const),
            pl.BlockSpec((tm, ATT_W), row), pl.BlockSpec((tm, SSM_W), row),
            pl.BlockSpec((SSM_W, SSM_W), const), pl.BlockSpec((1, SSM_W), const),
            pl.BlockSpec((d, d), const), pl.BlockSpec((1, d), const), pl.BlockSpec((1, d), const),
            pl.BlockSpec((N_EXPERTS, d), const), pl.BlockSpec((N_EXPERTS, 1), const),
        ],
        out_specs=[
            pl.BlockSpec((tm, d), row),
            pl.BlockSpec((TOP_K, tm), col), pl.BlockSpec((TOP_K, tm), col), pl.BlockSpec((TOP_K, tm), col),
            pl.BlockSpec((tm // ROW_TILE, N_EXPERTS, 128), lambda i: (i, 0, 0)),
        ],
        out_shape=[
            jax.ShapeDtypeStruct((t, d), F32),
            jax.ShapeDtypeStruct((TOP_K, t), I32), jax.ShapeDtypeStruct((TOP_K, t), F32),
            jax.ShapeDtypeStruct((TOP_K, t), I32),
            jax.ShapeDtypeStruct((t // ROW_TILE, N_EXPERTS, 128), F32),
        ],
        compiler_params=pltpu.CompilerParams(dimension_semantics=("parallel",), vmem_limit_bytes=VMEM_LIMIT),
        name="mix_ln1_router",
    )(x2, gi, bi, att, y, wglu, bglu, wout, g1, b1, wr_t, br)


_HI_BITS = -65536


def _pack_halves(x):
    half = x.shape[1] // 2
    lo = lax.bitcast_convert_type(x[:, :half], I32)
    hi = lax.bitcast_convert_type(x[:, half:], I32)
    return lax.shift_right_logical(lo, 16) | (hi & _HI_BITS)


def _unpack_halves(w):
    lo = lax.bitcast_convert_type(lax.shift_left(w, 16), F32)
    hi = lax.bitcast_convert_type(w & _HI_BITS, F32)
    return lo.astype(BF16), hi.astype(BF16)


def _group_copy(src, src_group, dst, dst_group, sem):
    return pltpu.make_async_copy(src.at[pl.ds(pl.multiple_of(src_group * GROUP, GROUP), GROUP), :],
                                 dst.at[pl.ds(pl.multiple_of(dst_group * GROUP, GROUP), GROUP), :], sem)


def _batched(count, fn):
    def body(q, c):
        fn(q * COPY_BATCH, COPY_BATCH)
        return c

    whole = count // COPY_BATCH
    lax.fori_loop(0, whole, body, 0)
    rest = count - whole * COPY_BATCH
    n, done = COPY_BATCH // 2, whole * COPY_BATCH
    while n >= 1:
        take = (rest // n) % 2

        @pl.when(take == 1)
        def _(n=n, done=done):
            fn(done, n)

        done = done + take * n
        n //= 2


def _dispatch_kernel(ngroups_ref, pad_start_ref, pad_count_ref, n_used_ref, dstg_ref, pos_ref, h1_ref, xs_hbm,
                     sorted_ref, zero_ref, sems, zsem):
    step = pl.program_id(0)
    n_steps = pl.num_programs(0)
    slot = step % 2
    n_blocks = xs_hbm.shape[0] // MOE_BLOCK

    def zero_block(blk):
        return pltpu.make_async_copy(zero_ref, xs_hbm.at[pl.ds(blk * MOE_BLOCK, MOE_BLOCK), :], zsem)

    @pl.when(step == 0)
    def _():
        zero_ref[...] = jnp.zeros(zero_ref.shape, I32)
        for e in range(N_EXPERTS):
            start = pad_start_ref[e]
            count = pad_count_ref[e]

            def fill(j, c):
                _group_copy(zero_ref, 0, xs_hbm, start + j, zsem).start()
                return c

            def drain(j, c):
                _group_copy(zero_ref, 0, xs_hbm, start, zsem).wait()
                return c

            lax.fori_loop(0, count, fill, 0)
            lax.fori_loop(0, count, drain, 0)

        def fill_block(blk, c):
            zero_block(blk).start()
            return c

        def drain_block(blk, c):
            zero_block(blk).wait()
            return c

        lax.fori_loop(n_used_ref[0], n_blocks, fill_block, 0)
        lax.fori_loop(n_used_ref[0], n_blocks, drain_block, 0)

    pos = pos_ref[...]
    rows = lax.broadcasted_iota(I32, (SORT_ROWS, pos.shape[1]), 0)
    place = rows == pos[0:1]
    for k in range(1, TOP_K):
        place = place | (rows == pos[k:k + 1])
    sorted_ref[slot] = _pack_halves(_dot(place.astype(BF16), h1_ref[...].astype(BF16)))

    _batched(ngroups_ref[step],
             lambda j, n: [_group_copy(sorted_ref.at[slot], j + r, xs_hbm, dstg_ref[0, 0, j + r],
                                       sems.at[slot]).start() for r in range(n)])

    def wait_tile(s, count):
        _batched(count, lambda j, n: pltpu.make_async_copy(
            sorted_ref.at[s, pl.ds(0, n * GROUP), :], xs_hbm.at[pl.ds(0, n * GROUP), :], sems.at[s]).wait())

    @pl.when(step > 0)
    def _():
        wait_tile(1 - slot, ngroups_ref[jnp.maximum(step - 1, 0)])

    @pl.when(step == n_steps - 1)
    def _():
        wait_tile(slot, ngroups_ref[step])


def _dispatch(ngroups, pad_start, pad_count, n_used, dstg, pos_t, h1, n_rows):
    t, d = h1.shape
    tm = ROW_TILE
    grid_spec = pltpu.PrefetchScalarGridSpec(
        num_scalar_prefetch=4,
        grid=(t // tm,),
        in_specs=[
            pl.BlockSpec((1, 1, SORT_GROUPS), lambda i, *_: (i, 0, 0), memory_space=pltpu.SMEM),
            pl.BlockSpec((TOP_K, tm), lambda i, *_: (0, i)),
            pl.BlockSpec((tm, d), lambda i, *_: (i, 0)),
        ],
        out_specs=pl.BlockSpec(memory_space=pl.ANY),
        scratch_shapes=[pltpu.VMEM((2, SORT_ROWS, d // 2), I32), pltpu.VMEM((MOE_BLOCK, d // 2), I32),
                        pltpu.SemaphoreType.DMA((2,)), pltpu.SemaphoreType.DMA],
    )
    return pl.pallas_call(
        _dispatch_kernel,
        grid_spec=grid_spec,
        out_shape=jax.ShapeDtypeStruct((n_rows, d // 2), I32),
        compiler_params=pltpu.CompilerParams(dimension_semantics=("arbitrary",), vmem_limit_bytes=VMEM_LIMIT,
                                             has_side_effects=True),
        name="moe_dispatch",
    )(ngroups, pad_start, pad_count, n_used, dstg, pos_t, h1)


def _expert_kernel(be_ref, nb_ref, x_ref, wg_ref, bg_ref, wu_ref, bu_ref, wd_ref, bd_ref, y_ref,
                   wg_b, wu_b, wd_b):
    i = pl.program_id(0)

    @pl.when(i < nb_ref[0])
    def _():
        @pl.when((i == 0) | (be_ref[i] != be_ref[jnp.maximum(i - 1, 0)]))
        def _():
            wg_b[...] = wg_ref[0].astype(BF16)
            wu_b[...] = wu_ref[0].astype(BF16)
            wd_b[...] = wd_ref[0].astype(BF16)

        x_lo, x_hi = _unpack_halves(x_ref[...])
        half = x_lo.shape[1]
        gate = _dot(x_lo, wg_b[0:half, :]) + _dot(x_hi, wg_b[half:, :]) + bg_ref[0]
        up = _dot(x_lo, wu_b[0:half, :]) + _dot(x_hi, wu_b[half:, :]) + bu_ref[0]
        gate = jnp.minimum(gate, SWIGLU_LIMIT)
        up = jnp.clip(up, -SWIGLU_LIMIT, SWIGLU_LIMIT)
        act = (up + 1.0) * gate * jax.nn.sigmoid(gate * SWIGLU_ALPHA)
        y = _dot(act.astype(BF16), wd_b[...]) + bd_ref[0]
        y_ref[...] = _pack_halves(y.astype(BF16).astype(F32))

    @pl.when(i >= nb_ref[0])
    def _():
        y_ref[...] = jnp.zeros(y_ref.shape, I32)


def _experts(block_expert, n_used, xs, wg, bg, wu, bu, wd, bd):
    n_rows, packed = xs.shape
    d, dff = wg.shape[1], wg.shape[2]
    assert packed * 2 == d and wd.shape[2] == d
    nb = n_rows // MOE_BLOCK
    blk = lambda i, be, nu: (jnp.minimum(i, nu[0] - 1), 0)
    wsel = lambda i, be, nu: (be[jnp.minimum(i, nu[0] - 1)], 0, 0)
    grid_spec = pltpu.PrefetchScalarGridSpec(
        num_scalar_prefetch=2,
        grid=(nb,),
        in_specs=[
            pl.BlockSpec((MOE_BLOCK, packed), blk),
            pl.BlockSpec((1, d, dff), wsel), pl.BlockSpec((1, 1, dff), wsel),
            pl.BlockSpec((1, d, dff), wsel), pl.BlockSpec((1, 1, dff), wsel),
            pl.BlockSpec((1, dff, d), wsel), pl.BlockSpec((1, 1, d), wsel),
        ],
        out_specs=pl.BlockSpec((MOE_BLOCK, packed), lambda i, be, nu: (i, 0)),
        scratch_shapes=[pltpu.VMEM((d, dff), BF16), pltpu.VMEM((d, dff), BF16), pltpu.VMEM((dff, d), BF16)],
    )
    return pl.pallas_call(
        _expert_kernel,
        grid_spec=grid_spec,
        out_shape=jax.ShapeDtypeStruct((n_rows, packed), I32),
        compiler_params=pltpu.CompilerParams(dimension_semantics=("arbitrary",), vmem_limit_bytes=VMEM_LIMIT),
        name="moe_experts",
    )(block_expert, n_used, xs, wg, bg, wu, bu, wd, bd)


def _combine_kernel(ngroups_ref, dstg_ref, dstg_next_ref, pos_ref, gate_ref, h1_ref, ys_hbm, g2_ref, b2_ref,
                    o_ref, buf, sems):
    step = pl.program_id(0)
    n_steps = pl.num_programs(0)
    slot = step % 2

    def fetch(dref, s, j):
        return _group_copy(ys_hbm, dref[0, 0, j], buf.at[s], j, sems.at[s])

    def issue(dref, s, count):
        _batched(count, lambda j, n: [fetch(dref, s, j + r).start() for r in range(n)])

    @pl.when(step == 0)
    def _():
        buf[...] = jnp.zeros(buf.shape, I32)
        issue(dstg_ref, 0, ngroups_ref[0])

    @pl.when(step + 1 < n_steps)
    def _():
        issue(dstg_next_ref, 1 - slot, ngroups_ref[jnp.minimum(step + 1, n_steps - 1)])

    _batched(ngroups_ref[step], lambda j, n: pltpu.make_async_copy(
        ys_hbm.at[pl.ds(0, n * GROUP), :], buf.at[slot, pl.ds(0, n * GROUP), :], sems.at[slot]).wait())

    pos = pos_ref[...]
    gate = gate_ref[...]
    lanes = lax.broadcasted_iota(I32, (pos.shape[0], SORT_ROWS), 1)
    sel = jnp.where(lanes == pos[:, 0:1], gate[:, 0:1], 0.0)
    for k in range(1, TOP_K):
        sel = sel + jnp.where(lanes == pos[:, k:k + 1], gate[:, k:k + 1], 0.0)
    sel_hi = sel.astype(BF16)
    sel_lo = (sel - sel_hi.astype(F32)).astype(BF16)
    y_lo, y_hi = _unpack_halves(buf[slot])
    ffn = jnp.concatenate([_dot(sel_hi, y_lo) + _dot(sel_lo, y_lo), _dot(sel_hi, y_hi) + _dot(sel_lo, y_hi)], axis=1)
    o_ref[...] = _layer_norm(DEEPNORM_ALPHA * h1_ref[...] + ffn, g2_ref[...], b2_ref[...])


def _combine(ngroups, dstg, pos_c, gates, h1, ys, g2, b2):
    t, d = h1.shape
    tm = ROW_TILE
    n_steps = t // tm
    row = lambda i, ng: (i, 0)
    const = lambda i, ng: (0, 0)
    grid_spec = pltpu.PrefetchScalarGridSpec(
        num_scalar_prefetch=1,
        grid=(n_steps,),
        in_specs=[
            pl.BlockSpec((1, 1, SORT_GROUPS), lambda i, ng: (i, 0, 0), memory_space=pltpu.SMEM),
            pl.BlockSpec((1, 1, SORT_GROUPS), lambda i, ng: (jnp.minimum(i + 1, n_steps - 1), 0, 0),
                         memory_space=pltpu.SMEM),
            pl.BlockSpec((tm, TOP_K), row), pl.BlockSpec((tm, TOP_K), row), pl.BlockSpec((tm, d), row),
            pl.BlockSpec(memory_space=pl.ANY),
            pl.BlockSpec((1, d), const), pl.BlockSpec((1, d), const),
        ],
        out_specs=pl.BlockSpec((tm, d), row),
        scratch_shapes=[pltpu.VMEM((2, SORT_ROWS, d // 2), I32), pltpu.SemaphoreType.DMA((2,))],
    )
    return pl.pallas_call(
        _combine_kernel,
        grid_spec=grid_spec,
        out_shape=jax.ShapeDtypeStruct((t, d), F32),
        compiler_params=pltpu.CompilerParams(dimension_semantics=("arbitrary",), vmem_limit_bytes=VMEM_LIMIT),
        name="moe_combine_ln2",
    )(ngroups, dstg, dstg, pos_c, gates, h1, ys, g2, b2)


def kernel(x, meta_tokens, ln_in_g, ln_in_b, rel_bias, w_in, lambda_q1, lambda_k1, lambda_q2, lambda_k2,
           subln_g, a_re, a_im, log_step, b_re, b_im, c_re, c_im, d_skip, w_glu, b_glu, w_out, ln1_g, ln1_b,
           w_router, b_router, w_gate, b_gate, w_up, b_up, w_down, b_down, ln2_g, ln2_b):
    bsz, seq, dm = x.shape
    assert seq % 512 == 0 and w_in.shape[0] == DEPTH == 1
    layer = 0
    row2 = lambda v: v.astype(F32).reshape(1, -1)

    w_in_b = w_in[layer].astype(BF16)
    gi, bi = row2(ln_in_g), row2(ln_in_b)
    q, k, vt, u = _inproj(x, gi, bi, w_in_b, tm=512, kv_tile=KV_TILE)
    meta = jnp.zeros((1, META_PAD, dm), x.dtype).at[0, :N_META].set(meta_tokens.astype(x.dtype))
    _, k_meta, vt_meta, u_meta = _inproj(meta, gi, bi, w_in_b, tm=META_PAD, kv_tile=META_PAD)

    lam_init = 0.8 - 0.6 * math.exp(-0.3 * layer)
    lam = (jnp.exp(jnp.sum(lambda_q1[layer].astype(F32) * lambda_k1[layer].astype(F32)))
           - jnp.exp(jnp.sum(lambda_q2[layer].astype(F32) * lambda_k2[layer].astype(F32))) + lam_init)
    att = _attention(lam.reshape(1), q, k, vt, k_meta[0], vt_meta[0, :, 0], _near_bias(rel_bias),
                     subln_g[layer].astype(F32).reshape(HEAD_W, 1), lam_init=lam_init)

    kpad, pt, qt, a1, a2 = _ssm_tables(a_re[layer], a_im[layer], log_step[layer], b_re[layer], b_im[layer],
                                     c_re[layer], c_im[layer], d_skip[layer])
    n_chunks = seq // SSM_T
    assert (n_chunks * bsz) % 16 == 0
    ug = jnp.transpose(u.reshape(bsz, n_chunks, SSM_T, SSM_G, SSM_CG), (3, 1, 0, 2, 4))
    ug = ug.reshape(SSM_G, n_chunks * bsz, SSM_T * SSM_CG)
    lead = jnp.zeros((8, N_META, SSM_W), BF16).at[0].set(u_meta[0, :N_META])
    ug_meta = jnp.transpose(lead.reshape(8, N_META, SSM_G, SSM_CG), (2, 0, 1, 3)).reshape(SSM_G, 8, N_META * SSM_CG)
    yg = _ssm(ug, ug_meta, kpad, pt, qt, a1, a2, n_chunks=n_chunks, bsz=bsz)
    y_ssm = jnp.transpose(yg.reshape(SSM_G, n_chunks, bsz, SSM_T, SSM_CG), (2, 1, 3, 0, 4))
    y_ssm = y_ssm.reshape(bsz, seq, SSM_W)

    t = bsz * seq
    h1, idx_t, gate_t, rank_t, cnt = _mix(
        x.reshape(t, dm), gi, bi, att.reshape(t, ATT_W), y_ssm.reshape(t, SSM_W),
        w_glu[layer].astype(BF16), row2(b_glu[layer]), w_out[layer].astype(BF16),
        row2(ln1_g[layer]), row2(ln1_b[layer]),
        jnp.transpose(w_router[layer].astype(F32)), b_router[layer].astype(F32).reshape(N_EXPERTS, 1))

    n_tiles = t // ROW_TILE
    experts = jnp.arange(N_EXPERTS, dtype=I32)
    tiles = jnp.arange(n_tiles, dtype=I32)
    tile_cnt = jnp.max(cnt, axis=-1).astype(I32)
    run = (tile_cnt + GROUP - 1) // GROUP * GROUP
    run_off = jnp.sum(jnp.where((experts[:, None] < experts[None, :])[None], run[:, :, None], 0), axis=1)
    run_before = jnp.sum(jnp.where((tiles[:, None] < tiles[None, :])[:, :, None], run[:, None, :], 0), axis=0)
    counts = jnp.sum(run, axis=0)
    padded = (counts + MOE_BLOCK - 1) // MOE_BLOCK * MOE_BLOCK
    padded_end = jnp.sum(jnp.where(experts[:, None] <= experts[None, :], padded[:, None], 0), axis=0)
    padded_start = padded_end - padded
    n_blocks = (t * TOP_K + n_tiles * N_EXPERTS * (GROUP - 1)) // MOE_BLOCK + N_EXPERTS
    block_row0 = jnp.arange(n_blocks, dtype=I32) * MOE_BLOCK
    block_expert = jnp.minimum(jnp.sum((padded_end[None, :] <= block_row0[:, None]).astype(I32), axis=1),
                               N_EXPERTS - 1)
    n_used = (padded_end[-1:] // MOE_BLOCK).astype(I32)
    ngroups = jnp.sum(run, axis=1) // GROUP
    local_row = jnp.arange(SORT_GROUPS, dtype=I32) * GROUP
    owner = jnp.minimum(jnp.sum(((run_off + run)[:, None, :] <= local_row[None, :, None]).astype(I32), axis=-1),
                        N_EXPERTS - 1)
    shift = (padded_start[None, :] + run_before - run_off) // GROUP
    dstg = jnp.sum(jnp.where(owner[..., None] == experts, shift[:, None, :], 0), axis=-1) + local_row // GROUP
    dstg = jnp.where(local_row[None, :] // GROUP < ngroups[:, None], dstg, 0).reshape(n_tiles, 1, SORT_GROUPS)
    run_off_tok = jnp.repeat(run_off, ROW_TILE, axis=0)
    pos_t = jnp.sum(jnp.where(idx_t[..., None] == experts, run_off_tok[None], 0), axis=-1) + rank_t

    xs = _dispatch(ngroups, ((padded_start + counts) // GROUP).astype(I32),
                   ((padded - counts) // GROUP).astype(I32), n_used, dstg, pos_t, h1, n_blocks * MOE_BLOCK)
    b3 = lambda v: v.astype(F32)[:, None, :]
    ys = _experts(block_expert, n_used, xs, w_gate[layer], b3(b_gate[layer]), w_up[layer], b3(b_up[layer]),
                  w_down[layer], b3(b_down[layer]))
    out = _combine(ngroups, dstg, jnp.transpose(pos_t), jnp.transpose(gate_t), h1, ys,
                   row2(ln2_g[layer]), row2(ln2_b[layer]))
    return out.reshape(bsz, seq, dm)
```

```python
import functools
import math

import jax
import jax.numpy as jnp
import numpy as np
from jax import lax
from jax.experimental import pallas as pl
from jax.experimental.pallas import tpu as pltpu

F32 = jnp.float32
BF16 = jnp.bfloat16
I32 = jnp.int32

DEPTH = 1
N_META = 16
CHUNK = 64
ATT_HEADS = 4
HEAD_DIM = 64
HEAD_W = 2 * HEAD_DIM
ATT_W = ATT_HEADS * HEAD_W
SSM_W = 512
SSM_CG = 16
SSM_G = SSM_W // SSM_CG
SSM_N = 64
N_BUCKETS = 32
MAX_DISTANCE = 128
N_EXPERTS = 32
TOP_K = 4
SWIGLU_LIMIT = 7.0
SWIGLU_ALPHA = 1.702
LN_EPS = 1e-5
NEG_INF = -1e30
DEEPNORM_ALPHA = (2.0 * DEPTH) ** 0.25
LOG2E = 1.4426950408889634

Q_TILE = 512
KV_TILE = 512
V_ONES = 16
V_ROWS = HEAD_W + V_ONES
META_PAD = 128
SSM_T = 64
MIX_TILE = 512
ROW_TILE = 256
MOE_BLOCK = 512
GROUP = 8
COPY_BATCH = 32
SORT_ROWS = 1280
SORT_GROUPS = SORT_ROWS // GROUP
assert SORT_ROWS >= ROW_TILE * TOP_K + N_EXPERTS * (GROUP - 1)
VMEM_LIMIT = 56 * 1024 * 1024


def _layer_norm(x, g, b):
    mu = jnp.mean(x, axis=-1, keepdims=True)
    xc = x - mu
    var = jnp.mean(xc * xc, axis=-1, keepdims=True)
    return xc * lax.rsqrt(var + LN_EPS) * g + b


def _dot(a, b):
    return jnp.dot(a, b, preferred_element_type=F32)


def _dot_nt(a, b, precision=None):
    return lax.dot_general(a, b, (((1,), (1,)), ((), ())), precision=precision,
                           preferred_element_type=F32)


def _inproj_kernel(x_ref, g_ref, b_ref, w_ref, q_ref, k_ref, vt_ref, u_ref, *, kv_tile, q_scale):
    h = _layer_norm(x_ref[0], g_ref[...], b_ref[...]).astype(BF16)
    tm = h.shape[0]
    q_ref[0] = (_dot(h, w_ref[:, 0:ATT_W]) * q_scale).astype(BF16)
    k_ref[0] = _dot(h, w_ref[:, ATT_W:2 * ATT_W]).astype(BF16)
    v = _dot(h, w_ref[:, 2 * ATT_W:3 * ATT_W])
    ones = jnp.ones((V_ONES, kv_tile), BF16)
    for hh in range(ATT_HEADS):
        vt = v[:, hh * HEAD_W:(hh + 1) * HEAD_W].T.astype(BF16)
        for j in range(tm // kv_tile):
            vt_ref[0, hh, j, 0:HEAD_W, :] = vt[:, j * kv_tile:(j + 1) * kv_tile]
            vt_ref[0, hh, j, HEAD_W:V_ROWS, :] = ones
    u_ref[0] = _dot(h, w_ref[:, 3 * ATT_W:]).astype(BF16)


def _inproj(x, g, b, w, *, tm, kv_tile):
    bsz, s, d = x.shape
    q_scale = HEAD_DIM ** -0.5 * LOG2E
    n_cols = w.shape[1]
    row = lambda bi, i: (bi, i, 0)
    return pl.pallas_call(
        functools.partial(_inproj_kernel, kv_tile=kv_tile, q_scale=q_scale),
        grid=(bsz, s // tm),
        in_specs=[
            pl.BlockSpec((1, tm, d), row),
            pl.BlockSpec((1, d), lambda bi, i: (0, 0)),
            pl.BlockSpec((1, d), lambda bi, i: (0, 0)),
            pl.BlockSpec((d, n_cols), lambda bi, i: (0, 0)),
        ],
        out_specs=[
            pl.BlockSpec((1, tm, ATT_W), row),
            pl.BlockSpec((1, tm, ATT_W), row),
            pl.BlockSpec((1, ATT_HEADS, tm // kv_tile, V_ROWS, kv_tile), lambda bi, i: (bi, 0, i, 0, 0)),
            pl.BlockSpec((1, tm, SSM_W), row),
        ],
        out_shape=[
            jax.ShapeDtypeStruct((bsz, s, ATT_W), BF16),
            jax.ShapeDtypeStruct((bsz, s, ATT_W), BF16),
            jax.ShapeDtypeStruct((bsz, ATT_HEADS, s // kv_tile, V_ROWS, kv_tile), BF16),
            jax.ShapeDtypeStruct((bsz, s, SSM_W), BF16),
        ],
        compiler_params=pltpu.CompilerParams(
            dimension_semantics=("parallel", "parallel"), vmem_limit_bytes=VMEM_LIMIT),
        name="inproj",
    )(x, g, b, w)


def _attn_kernel(lam_ref, q_ref, k_ref, vt_ref, km_ref, vtm_ref, bias_ref, g_ref, o_ref,
                 m_ref, acc_ref, qz_ref, sa_ref, sb_ref, sn_ref, *, out_scale):
    i = pl.program_id(2)
    qt = q_ref[0].astype(F32).T
    feat = lax.broadcasted_iota(I32, qt.shape, 0)
    qz_ref[:, 0:Q_TILE] = jnp.where(feat < HEAD_DIM, qt, 0.0).astype(BF16)
    qz_ref[:, Q_TILE:] = jnp.where(feat >= HEAD_DIM, qt, 0.0).astype(BF16)

    m_ref[...] = jnp.full(m_ref.shape, NEG_INF, F32)
    acc_ref[...] = jnp.zeros(acc_ref.shape, F32)

    def scores(kt):
        return _dot(kt, qz_ref[...])

    def k_tile(t):
        return k_ref[0, pl.ds(pl.multiple_of(t * KV_TILE, KV_TILE), KV_TILE), :]

    def absorb(s, pv):
        m_old = m_ref[...]
        m_new = jnp.maximum(m_old, jnp.max(s, axis=0, keepdims=True))
        alpha = jnp.exp2(m_old - m_new)
        p = jnp.exp2(s - m_new).astype(BF16)
        acc_ref[...] = acc_ref[...] * alpha + pv(p)
        m_ref[...] = m_new

    def tile_pv(t):
        return lambda p: _dot(vt_ref[0, 0, t], p)

    n_far = jnp.maximum(i - 1, 0)
    peel = n_far % 2

    sn_ref[...] = scores(km_ref[...])
    sa_ref[...] = scores(k_tile(0))
    absorb(sn_ref[...] + bias_ref[0, 0, 2 * KV_TILE:, :], lambda p: _dot(vtm_ref[0], p))

    @pl.when(peel == 1)
    def _():
        sb_ref[...] = scores(k_tile(1))
        absorb(sa_ref[...], tile_pv(0))
        sa_ref[...] = sb_ref[...]

    def far_pair(j, carry):
        t0 = peel + 2 * j
        s_cur = sa_ref[...]
        sb_ref[...] = scores(k_tile(t0 + 1))
        absorb(s_cur, tile_pv(t0))
        s_cur = sb_ref[...]
        sa_ref[...] = scores(k_tile(t0 + 2))
        absorb(s_cur, tile_pv(t0 + 1))
        return carry

    lax.fori_loop(0, n_far // 2, far_pair, 0)

    sb_ref[...] = scores(k_tile(i))
    absorb(sa_ref[...] + bias_ref[0, 0, 0:KV_TILE, :], tile_pv(n_far))
    absorb(sb_ref[...] + bias_ref[0, 0, KV_TILE:2 * KV_TILE, :], tile_pv(i))

    acc = acc_ref[...]
    lam = lam_ref[0]
    o1 = acc[0:HEAD_W, 0:Q_TILE] / acc[HEAD_W:HEAD_W + 1, 0:Q_TILE]
    o2 = acc[0:HEAD_W, Q_TILE:] / acc[HEAD_W:HEAD_W + 1, Q_TILE:]
    o = o1 - lam * o2
    ms = jnp.mean(o * o, axis=0, keepdims=True)
    o = o * lax.rsqrt(ms + LN_EPS) * g_ref[...] * out_scale
    o_ref[0] = o.T.astype(o_ref.dtype)


def _attention(lam, q, k, vt, k_meta, vt_meta, bias, subln_g, *, lam_init):
    bsz, s, _ = q.shape
    nq = s // Q_TILE
    n_near = META_PAD + 2 * KV_TILE
    grid_spec = pltpu.PrefetchScalarGridSpec(
        num_scalar_prefetch=1,
        grid=(bsz, ATT_HEADS, nq),
        in_specs=[
            pl.BlockSpec((1, Q_TILE, HEAD_W), lambda b, h, i, lam: (b, i, h)),
            pl.BlockSpec((1, s, HEAD_W), lambda b, h, i, lam: (b, 0, h)),
            pl.BlockSpec((1, 1, s // KV_TILE, V_ROWS, KV_TILE), lambda b, h, i, lam: (b, h, 0, 0, 0)),
            pl.BlockSpec((META_PAD, HEAD_W), lambda b, h, i, lam: (0, h)),
            pl.BlockSpec((1, V_ROWS, META_PAD), lambda b, h, i, lam: (h, 0, 0)),
            pl.BlockSpec((1, 1, n_near, 2 * Q_TILE), lambda b, h, i, lam: (h, jnp.minimum(i, 1), 0, 0)),
            pl.BlockSpec((HEAD_W, 1), lambda b, h, i, lam: (0, 0)),
        ],
        out_specs=pl.BlockSpec((1, Q_TILE, HEAD_W), lambda b, h, i, lam: (b, i, h)),
        scratch_shapes=[
            pltpu.VMEM((1, 2 * Q_TILE), F32), pltpu.VMEM((V_ROWS, 2 * Q_TILE), F32),
            pltpu.VMEM((HEAD_W, 2 * Q_TILE), BF16),
            pltpu.VMEM((KV_TILE, 2 * Q_TILE), F32), pltpu.VMEM((KV_TILE, 2 * Q_TILE), F32),
            pltpu.VMEM((META_PAD, 2 * Q_TILE), F32),
        ],
    )
    return pl.pallas_call(
        functools.partial(_attn_kernel, out_scale=1.0 - lam_init),
        grid_spec=grid_spec,
        out_shape=jax.ShapeDtypeStruct((bsz, s, ATT_W), BF16),
        compiler_params=pltpu.CompilerParams(
            dimension_semantics=("parallel", "parallel", "arbitrary"), vmem_limit_bytes=VMEM_LIMIT),
        name="diff_attention",
    )(lam, q, k, vt, k_meta, vt_meta, bias, subln_g)


def _t5_bucket(rel):
    nb = N_BUCKETS // 2
    max_exact = nb // 2
    ret = jnp.where(rel > 0, nb, 0)
    n = jnp.abs(rel)
    n_f = jnp.maximum(n, 1).astype(F32)
    large = max_exact + (jnp.log(n_f / max_exact) / math.log(MAX_DISTANCE / max_exact)
                         * (nb - max_exact)).astype(I32)
    large = jnp.minimum(large, nb - 1)
    return ret + jnp.where(n < max_exact, n, large)


def _near_bias(rel_bias):
    table = rel_bias.astype(F32)
    far = table[N_BUCKETS // 2 - 1]
    c = jnp.arange(Q_TILE, dtype=I32)[None, :]
    r = jnp.arange(KV_TILE, dtype=I32)[:, None]

    shifted = jnp.transpose(table - far) * LOG2E

    def bias_of(offset, rows, cols=Q_TILE):
        rel = jnp.arange(rows, dtype=I32)[:, None] - c[:, :cols] + offset
        hot = _t5_bucket(rel)[None, :, :, None] == jnp.arange(N_BUCKETS, dtype=I32)
        return jnp.sum(jnp.where(hot, shifted[:, None, None, :], 0.0), axis=-1)

    nd = KV_TILE // MAX_DISTANCE
    own_bias = sum(
        (jnp.eye(nd, k=-d, dtype=F32)[None, :, None, :, None]
         * bias_of(d * MAX_DISTANCE, MAX_DISTANCE, MAX_DISTANCE)[:, None, :, None, :]).reshape(
             ATT_HEADS, KV_TILE, Q_TILE)
        for d in (-1, 0, 1))
    own = jnp.where((r // CHUNK <= c // CHUNK)[None], own_bias, NEG_INF)
    near = MAX_DISTANCE
    prev = jnp.zeros((ATT_HEADS, KV_TILE, Q_TILE), F32).at[:, KV_TILE - near:, :near].set(bias_of(-near, near, near))
    rm = jnp.arange(META_PAD, dtype=I32)[:, None]
    meta_ok = (rm < N_META)[None]
    meta0 = jnp.where(meta_ok, jnp.pad(bias_of(-N_META, N_META), ((0, 0), (0, META_PAD - N_META), (0, 0))), NEG_INF)
    meta1 = jnp.where(meta_ok, jnp.zeros((ATT_HEADS, META_PAD, Q_TILE), F32), NEG_INF)
    assert N_META + Q_TILE - (N_META - 1) >= MAX_DISTANCE and KV_TILE >= near and Q_TILE >= near
    v0 = jnp.concatenate([jnp.full_like(prev, NEG_INF), own, meta0], axis=1)
    v1 = jnp.concatenate([prev, own, meta1], axis=1)
    both = jnp.stack([v0, v1], axis=1)
    return jnp.concatenate([both, both], axis=-1)


def _ssm_tables(a_re, a_im, log_step, b_re, b_im, c_re, c_im, d_skip):
    hi = lax.Precision.HIGHEST
    t_len = SSM_T
    step = jnp.exp(log_step.astype(F32))[:, None]
    ar = jnp.minimum(a_re.astype(F32), -1e-4)
    ai = a_im.astype(F32)
    mag = jnp.exp(step * ar)
    ph = step * ai
    abar_re = mag * jnp.cos(ph)
    abar_im = mag * jnp.sin(ph)
    den = ar * ar + ai * ai
    e_re = abar_re - 1.0
    e_im = abar_im
    f_re = (e_re * ar + e_im * ai) / den
    f_im = (e_im * ar - e_re * ai) / den
    br = b_re.astype(F32)
    bi = b_im.astype(F32)
    bb_re = f_re[..., None] * br - f_im[..., None] * bi
    bb_im = f_re[..., None] * bi + f_im[..., None] * br
    tau = jnp.arange(t_len + 1, dtype=F32)[None, :, None]
    pmag = jnp.exp(tau * (step * ar)[:, None, :])
    pph = tau * ph[:, None, :]
    pw_re = pmag * jnp.cos(pph)
    pw_im = pmag * jnp.sin(pph)
    bbt_re = jnp.swapaxes(bb_re, 1, 2)[:, None]
    bbt_im = jnp.swapaxes(bb_im, 1, 2)[:, None]

    def times_bbar(p_re, p_im):
        return (p_re[:, :, None, :] * bbt_re - p_im[:, :, None, :] * bbt_im,
                p_re[:, :, None, :] * bbt_im + p_im[:, :, None, :] * bbt_re)

    cr = c_re.astype(F32)
    ci = c_im.astype(F32)
    bi_re = jnp.swapaxes(bb_re, 1, 2)[:, :, None, :]
    bi_im = jnp.swapaxes(bb_im, 1, 2)[:, :, None, :]
    cb_re = cr[:, None] * bi_re - ci[:, None] * bi_im
    cb_im = cr[:, None] * bi_im + ci[:, None] * bi_re
    kern = jnp.einsum('gikm,gtm->gitk', jnp.concatenate([cb_re, -cb_im], axis=-1),
                      jnp.concatenate([pw_re[:, :t_len], pw_im[:, :t_len]], axis=-1), precision=hi)
    skip = d_skip.astype(F32).reshape(SSM_G, SSM_CG)
    kern = kern.at[:, :, 0, :].add(skip[:, None, :] * jnp.eye(SSM_CG, dtype=F32))
    kflat = kern.reshape(SSM_G, SSM_CG, t_len * SSM_CG)
    kpad = jnp.concatenate([jnp.zeros_like(kflat), kflat], axis=-1)
    back = jnp.arange(t_len - 1, -1, -1, dtype=F32)[None, :, None]
    bmag = jnp.exp(back * (step * ar)[:, None, :])
    bph = back * ph[:, None, :]
    pt_re, pt_im = times_bbar(bmag * jnp.cos(bph), bmag * jnp.sin(bph))
    pt = jnp.concatenate([pt_re, pt_im], axis=-1).reshape(SSM_G, t_len * SSM_CG, 2 * SSM_N)
    up_re = jnp.swapaxes(pw_re[:, 1:], 1, 2)[..., None]
    up_im = jnp.swapaxes(pw_im[:, 1:], 1, 2)[..., None]
    crt = jnp.swapaxes(cr, 1, 2)[:, :, None, :]
    cit = jnp.swapaxes(ci, 1, 2)[:, :, None, :]
    q_re = crt * up_re - cit * up_im
    q_im = crt * up_im + cit * up_re
    qt = jnp.concatenate([q_re, -q_im], axis=1).reshape(SSM_G, 2 * SSM_N, t_len * SSM_CG)
    at_re = pw_re[:, t_len]
    at_im = pw_im[:, t_len]
    a1 = jnp.concatenate([at_re, at_re], axis=-1)
    a2 = jnp.concatenate([-at_im, at_im], axis=-1)
    return kpad, pt.astype(BF16), qt.astype(BF16), a1, a2


def _ssm_state_kernel(u_ref, pt_ref, s_ref):
    s_ref[0] = _dot(u_ref[0], pt_ref[0])


def _ssm_scan_kernel(s_ref, x0_ref, a1_ref, a2_ref, x_ref, *, n_chunks, bsz):
    a1 = a1_ref[...][:, None, :]
    a2 = a2_ref[...][:, None, :]

    def body(c, x):
        rows = pl.ds(c * bsz, bsz)
        x_ref[:, rows, :] = x
        return a1 * x + a2 * pltpu.roll(x, SSM_N, 2) + s_ref[:, rows, :]

    x0 = jnp.broadcast_to(x0_ref[:, 0:1, :], (SSM_G, bsz, 2 * SSM_N))
    lax.fori_loop(0, n_chunks, body, x0)


def _ssm_out_kernel(u_ref, x_ref, kp_ref, qt_ref, y_ref, mt_ref):
    kp = kp_ref[0]
    width = SSM_T * SSM_CG
    lane_tile = 128
    for sub in range(0, lane_tile, SSM_CG):
        shifted = kp if sub == 0 else pltpu.roll(kp, 2 * width - sub, 1)
        for s in range(SSM_T):
            off = (SSM_T - s) * SSM_CG
            if off % lane_tile == sub:
                base = off - sub
                mt_ref[s * SSM_CG:(s + 1) * SSM_CG, :] = shifted[:, base:base + width].astype(BF16)
    x = x_ref[0]
    x_hi = x.astype(BF16)
    x_lo = (x - x_hi.astype(F32)).astype(BF16)
    y = _dot(u_ref[0], mt_ref[...]) + _dot(x_hi, qt_ref[0]) + _dot(x_lo, qt_ref[0])
    y_ref[0] = y.astype(y_ref.dtype)


def _ssm_chunk_state(ug, pt):
    g, r, w = ug.shape
    n2 = 2 * SSM_N
    per_g = lambda gi: (gi, 0, 0)
    return pl.pallas_call(
        _ssm_state_kernel,
        grid=(g,),
        in_specs=[pl.BlockSpec((1, r, w), per_g), pl.BlockSpec((1, w, n2), per_g)],
        out_specs=pl.BlockSpec((1, r, n2), per_g),
        out_shape=jax.ShapeDtypeStruct((g, r, n2), F32),
        compiler_params=pltpu.CompilerParams(dimension_semantics=("parallel",), vmem_limit_bytes=VMEM_LIMIT),
        name="ssm_chunk_state",
    )(ug, pt)


def _ssm(ug, ug_meta, kpad, pt, qt, a1, a2, *, n_chunks, bsz):
    g, r, w = ug.shape
    n2 = 2 * SSM_N
    per_g = lambda gi: (gi, 0, 0)
    params = pltpu.CompilerParams(dimension_semantics=("parallel",), vmem_limit_bytes=VMEM_LIMIT)
    s = _ssm_chunk_state(ug, pt)
    x0 = _ssm_chunk_state(ug_meta, pt[:, w - ug_meta.shape[2]:, :])
    x = pl.pallas_call(
        functools.partial(_ssm_scan_kernel, n_chunks=n_chunks, bsz=bsz),
        out_shape=jax.ShapeDtypeStruct((g, r, n2), F32),
        compiler_params=pltpu.CompilerParams(vmem_limit_bytes=VMEM_LIMIT),
        name="ssm_chunk_scan",
    )(s, x0, a1, a2)
    return pl.pallas_call(
        _ssm_out_kernel,
        grid=(g,),
        in_specs=[pl.BlockSpec((1, r, w), per_g), pl.BlockSpec((1, r, n2), per_g),
                  pl.BlockSpec((1, SSM_CG, 2 * w), per_g), pl.BlockSpec((1, n2, w), per_g)],
        out_specs=pl.BlockSpec((1, r, w), per_g),
        out_shape=jax.ShapeDtypeStruct((g, r, w), BF16),
        scratch_shapes=[pltpu.VMEM((w, w), BF16)],
        compiler_params=params,
        name="ssm_output",
    )(ug, x, kpad, qt)


def _mix_kernel(x_ref, gi_ref, bi_ref, att_ref, y_ref, wglu_ref, bglu_ref, wout_ref, g1_ref, b1_ref,
                wr_ref, br_ref, h1_ref, idx_ref, gate_ref, rank_ref, cnt_ref):
    h0 = _layer_norm(x_ref[...], gi_ref[...], bi_ref[...])
    y = y_ref[...].astype(F32)
    y = y * (0.5 * (1.0 + jnp.tanh(math.sqrt(2.0 / math.pi) * (y + 0.044715 * (y * y * y)))))
    y = y * jax.nn.sigmoid(_dot(y.astype(BF16), wglu_ref[...]) + bglu_ref[...])
    mix = _dot(att_ref[...], wout_ref[0:ATT_W, :]) + _dot(y.astype(BF16), wout_ref[ATT_W:, :])
    h1 = _layer_norm(DEEPNORM_ALPHA * h0 + mix, g1_ref[...], b1_ref[...])
    h1_ref[...] = h1

    logits = _dot_nt(wr_ref[...], h1, precision=lax.Precision.HIGHEST) + br_ref[...]
    tm = logits.shape[1]
    eidx = lax.broadcasted_iota(I32, logits.shape, 0)
    vals, hots = [], []
    rest = logits
    for _ in range(TOP_K):
        mx = jnp.max(rest, axis=0, keepdims=True)
        first = jnp.min(jnp.where(rest == mx, eidx, N_EXPERTS), axis=0, keepdims=True)
        hot = eidx == first
        vals.append(mx)
        hots.append(hot)
        rest = jnp.where(hot, -jnp.inf, rest)
    exps = [jnp.exp(v - vals[0]) for v in vals]
    denom = exps[0] + exps[1] + exps[2] + exps[3]
    gate_ref[...] = jnp.concatenate([e / denom for e in exps], axis=0)
    idx_ref[...] = jnp.concatenate(
        [jnp.sum(jnp.where(h, eidx, 0), axis=0, keepdims=True) for h in hots], axis=0)

    hot_all = (hots[0] | hots[1] | hots[2] | hots[3]).astype(F32)
    sub = ROW_TILE
    tri = (lax.broadcasted_iota(I32, (sub, sub), 0) < lax.broadcasted_iota(I32, (sub, sub), 1)).astype(BF16)
    for part in range(tm // sub):
        cols = slice(part * sub, (part + 1) * sub)
        before = _dot(hot_all[:, cols].astype(BF16), tri)
        rank_ref[:, cols] = jnp.concatenate(
            [jnp.sum(jnp.where(h[:, cols], before, 0.0), axis=0, keepdims=True) for h in hots], axis=0).astype(I32)
        cnt_ref[part] = jnp.broadcast_to(jnp.sum(hot_all[:, cols], axis=1, keepdims=True), cnt_ref.shape[1:])


def _mix(x2, gi, bi, att, y, wglu, bglu, wout, g1, b1, wr_t, br):
    t, d = x2.shape
    tm = MIX_TILE
    row = lambda i: (i, 0)
    col = lambda i: (0, i)
    const = lambda i: (0, 0)
    return pl.pallas_call(
        _mix_kernel,
        grid=(t // tm,),
        in_specs=[
            pl.BlockSpec((tm, d), row), pl.BlockSpec((1, d), const), pl.BlockSpec((1, d), const),
            pl.BlockSpec((tm, ATT_W), row), pl.BlockSpec((tm, SSM_W), row),
            pl.BlockSpec((SSM_W, SSM_W), const), pl.BlockSpec((1, SSM_W), const),
            pl.BlockSpec((d, d), const), pl.BlockSpec((1, d), const), pl.BlockSpec((1, d), const),
            pl.BlockSpec((N_EXPERTS, d), const), pl.BlockSpec((N_EXPERTS, 1), const),
        ],
        out_specs=[
            pl.BlockSpec((tm, d), row),
            pl.BlockSpec((TOP_K, tm), col), pl.BlockSpec((TOP_K, tm), col), pl.BlockSpec((TOP_K, tm), col),
            pl.BlockSpec((tm // ROW_TILE, N_EXPERTS, 128), lambda i: (i, 0, 0)),
        ],
        out_shape=[
            jax.ShapeDtypeStruct((t, d), F32),
            jax.ShapeDtypeStruct((TOP_K, t), I32), jax.ShapeDtypeStruct((TOP_K, t), F32),
            jax.ShapeDtypeStruct((TOP_K, t), I32),
            jax.ShapeDtypeStruct((t // ROW_TILE, N_EXPERTS, 128), F32),
        ],
        compiler_params=pltpu.CompilerParams(dimension_semantics=("parallel",), vmem_limit_bytes=VMEM_LIMIT),
        name="mix_ln1_router",
    )(x2, gi, bi, att, y, wglu, bglu, wout, g1, b1, wr_t, br)


_HI_BITS = -65536


def _pack_halves(x):
    half = x.shape[1] // 2
    lo = lax.bitcast_convert_type(x[:, :half], I32)
    hi = lax.bitcast_convert_type(x[:, half:], I32)
    return lax.shift_right_logical(lo, 16) | (hi & _HI_BITS)


def _unpack_halves(w):
    lo = lax.bitcast_convert_type(lax.shift_left(w, 16), F32)
    hi = lax.bitcast_convert_type(w & _HI_BITS, F32)
    return lo.astype(BF16), hi.astype(BF16)


def _group_copy(src, src_group, dst, dst_group, sem):
    return pltpu.make_async_copy(src.at[pl.ds(pl.multiple_of(src_group * GROUP, GROUP), GROUP), :],
                                 dst.at[pl.ds(pl.multiple_of(dst_group * GROUP, GROUP), GROUP), :], sem)


def _batched(count, fn):
    def body(q, c):
        fn(q * COPY_BATCH, COPY_BATCH)
        return c

    whole = count // COPY_BATCH
    lax.fori_loop(0, whole, body, 0)
    rest = count - whole * COPY_BATCH
    n, done = COPY_BATCH // 2, whole * COPY_BATCH
    while n >= 1:
        take = (rest // n) % 2

        @pl.when(take == 1)
        def _(n=n, done=done):
            fn(done, n)

        done = done + take * n
        n //= 2


def _dispatch_kernel(ngroups_ref, pad_start_ref, pad_count_ref, n_used_ref, dstg_ref, pos_ref, h1_ref, xs_hbm,
                     sorted_ref, zero_ref, sems, zsem):
    step = pl.program_id(0)
    n_steps = pl.num_programs(0)
    slot = step % 2
    n_blocks = xs_hbm.shape[0] // MOE_BLOCK

    def zero_block(blk):
        return pltpu.make_async_copy(zero_ref, xs_hbm.at[pl.ds(blk * MOE_BLOCK, MOE_BLOCK), :], zsem)

    @pl.when(step == 0)
    def _():
        zero_ref[...] = jnp.zeros(zero_ref.shape, I32)
        for e in range(N_EXPERTS):
            start = pad_start_ref[e]
            count = pad_count_ref[e]

            def fill(j, c):
                _group_copy(zero_ref, 0, xs_hbm, start + j, zsem).start()
                return c

            def drain(j, c):
                _group_copy(zero_ref, 0, xs_hbm, start, zsem).wait()
                return c

            lax.fori_loop(0, count, fill, 0)
            lax.fori_loop(0, count, drain, 0)

        def fill_block(blk, c):
            zero_block(blk).start()
            return c

        def drain_block(blk, c):
            zero_block(blk).wait()
            return c

        lax.fori_loop(n_used_ref[0], n_blocks, fill_block, 0)
        lax.fori_loop(n_used_ref[0], n_blocks, drain_block, 0)

    pos = pos_ref[...]
    rows = lax.broadcasted_iota(I32, (SORT_ROWS, pos.shape[1]), 0)
    place = rows == pos[0:1]
    for k in range(1, TOP_K):
        place = place | (rows == pos[k:k + 1])
    sorted_ref[slot] = _pack_halves(_dot(place.astype(BF16), h1_ref[...].astype(BF16)))

    _batched(ngroups_ref[step],
             lambda j, n: [_group_copy(sorted_ref.at[slot], j + r, xs_hbm, dstg_ref[0, 0, j + r],
                                       sems.at[slot]).start() for r in range(n)])

    def wait_tile(s, count):
        _batched(count, lambda j, n: pltpu.make_async_copy(
            sorted_ref.at[s, pl.ds(0, n * GROUP), :], xs_hbm.at[pl.ds(0, n * GROUP), :], sems.at[s]).wait())

    @pl.when(step > 0)
    def _():
        wait_tile(1 - slot, ngroups_ref[jnp.maximum(step - 1, 0)])

    @pl.when(step == n_steps - 1)
    def _():
        wait_tile(slot, ngroups_ref[step])


def _dispatch(ngroups, pad_start, pad_count, n_used, dstg, pos_t, h1, n_rows):
    t, d = h1.shape
    tm = ROW_TILE
    grid_spec = pltpu.PrefetchScalarGridSpec(
        num_scalar_prefetch=4,
        grid=(t // tm,),
        in_specs=[
            pl.BlockSpec((1, 1, SORT_GROUPS), lambda i, *_: (i, 0, 0), memory_space=pltpu.SMEM),
            pl.BlockSpec((TOP_K, tm), lambda i, *_: (0, i)),
            pl.BlockSpec((tm, d), lambda i, *_: (i, 0)),
        ],
        out_specs=pl.BlockSpec(memory_space=pl.ANY),
        scratch_shapes=[pltpu.VMEM((2, SORT_ROWS, d // 2), I32), pltpu.VMEM((MOE_BLOCK, d // 2), I32),
                        pltpu.SemaphoreType.DMA((2,)), pltpu.SemaphoreType.DMA],
    )
    return pl.pallas_call(
        _dispatch_kernel,
        grid_spec=grid_spec,
        out_shape=jax.ShapeDtypeStruct((n_rows, d // 2), I32),
        compiler_params=pltpu.CompilerParams(dimension_semantics=("arbitrary",), vmem_limit_bytes=VMEM_LIMIT,
                                             has_side_effects=True),
        name="moe_dispatch",
    )(ngroups, pad_start, pad_count, n_used, dstg, pos_t, h1)


def _expert_kernel(be_ref, nb_ref, x_ref, wg_ref, bg_ref, wu_ref, bu_ref, wd_ref, bd_ref, y_ref,
                   wg_b, wu_b, wd_b):
    i = pl.program_id(0)

    @pl.when(i < nb_ref[0])
    def _():
        @pl.when((i == 0) | (be_ref[i] != be_ref[jnp.maximum(i - 1, 0)]))
        def _():
            wg_b[...] = wg_ref[0].astype(BF16)
            wu_b[...] = wu_ref[0].astype(BF16)
            wd_b[...] = wd_ref[0].astype(BF16)

        x_lo, x_hi = _unpack_halves(x_ref[...])
        half = x_lo.shape[1]
        gate = _dot(x_lo, wg_b[0:half, :]) + _dot(x_hi, wg_b[half:, :]) + bg_ref[0]
        up = _dot(x_lo, wu_b[0:half, :]) + _dot(x_hi, wu_b[half:, :]) + bu_ref[0]
        gate = jnp.minimum(gate, SWIGLU_LIMIT)
        up = jnp.clip(up, -SWIGLU_LIMIT, SWIGLU_LIMIT)
        act = (up + 1.0) * gate * jax.nn.sigmoid(gate * SWIGLU_ALPHA)
        y = _dot(act.astype(BF16), wd_b[...]) + bd_ref[0]
        y_ref[...] = _pack_halves(y.astype(BF16).astype(F32))

    @pl.when(i >= nb_ref[0])
    def _():
        y_ref[...] = jnp.zeros(y_ref.shape, I32)


def _experts(block_expert, n_used, xs, wg, bg, wu, bu, wd, bd):
    n_rows, packed = xs.shape
    d, dff = wg.shape[1], wg.shape[2]
    assert packed * 2 == d and wd.shape[2] == d
    nb = n_rows // MOE_BLOCK
    blk = lambda i, be, nu: (jnp.minimum(i, nu[0] - 1), 0)
    wsel = lambda i, be, nu: (be[jnp.minimum(i, nu[0] - 1)], 0, 0)
    grid_spec = pltpu.PrefetchScalarGridSpec(
        num_scalar_prefetch=2,
        grid=(nb,),
        in_specs=[
            pl.BlockSpec((MOE_BLOCK, packed), blk),
            pl.BlockSpec((1, d, dff), wsel), pl.BlockSpec((1, 1, dff), wsel),
            pl.BlockSpec((1, d, dff), wsel), pl.BlockSpec((1, 1, dff), wsel),
            pl.BlockSpec((1, dff, d), wsel), pl.BlockSpec((1, 1, d), wsel),
        ],
        out_specs=pl.BlockSpec((MOE_BLOCK, packed), lambda i, be, nu: (i, 0)),
        scratch_shapes=[pltpu.VMEM((d, dff), BF16), pltpu.VMEM((d, dff), BF16), pltpu.VMEM((dff, d), BF16)],
    )
    return pl.pallas_call(
        _expert_kernel,
        grid_spec=grid_spec,
        out_shape=jax.ShapeDtypeStruct((n_rows, packed), I32),
        compiler_params=pltpu.CompilerParams(dimension_semantics=("arbitrary",), vmem_limit_bytes=VMEM_LIMIT),
        name="moe_experts",
    )(block_expert, n_used, xs, wg, bg, wu, bu, wd, bd)


def _combine_kernel(ngroups_ref, dstg_ref, dstg_next_ref, pos_ref, gate_ref, h1_ref, ys_hbm, g2_ref, b2_ref,
                    o_ref, buf, sems):
    step = pl.program_id(0)
    n_steps = pl.num_programs(0)
    slot = step % 2

    def fetch(dref, s, j):
        return _group_copy(ys_hbm, dref[0, 0, j], buf.at[s], j, sems.at[s])

    def issue(dref, s, count):
        _batched(count, lambda j, n: [fetch(dref, s, j + r).start() for r in range(n)])

    @pl.when(step == 0)
    def _():
        buf[...] = jnp.zeros(buf.shape, I32)
        issue(dstg_ref, 0, ngroups_ref[0])

    @pl.when(step + 1 < n_steps)
    def _():
        issue(dstg_next_ref, 1 - slot, ngroups_ref[jnp.minimum(step + 1, n_steps - 1)])

    _batched(ngroups_ref[step], lambda j, n: pltpu.make_async_copy(
        ys_hbm.at[pl.ds(0, n * GROUP), :], buf.at[slot, pl.ds(0, n * GROUP), :], sems.at[slot]).wait())

    pos = pos_ref[...]
    gate = gate_ref[...]
    lanes = lax.broadcasted_iota(I32, (pos.shape[0], SORT_ROWS), 1)
    sel = jnp.where(lanes == pos[:, 0:1], gate[:, 0:1], 0.0)
    for k in range(1, TOP_K):
        sel = sel + jnp.where(lanes == pos[:, k:k + 1], gate[:, k:k + 1], 0.0)
    sel_hi = sel.astype(BF16)
    sel_lo = (sel - sel_hi.astype(F32)).astype(BF16)
    y_lo, y_hi = _unpack_halves(buf[slot])
    ffn = jnp.concatenate([_dot(sel_hi, y_lo) + _dot(sel_lo, y_lo), _dot(sel_hi, y_hi) + _dot(sel_lo, y_hi)], axis=1)
    o_ref[...] = _layer_norm(DEEPNORM_ALPHA * h1_ref[...] + ffn, g2_ref[...], b2_ref[...])


def _combine(ngroups, dstg, pos_c, gates, h1, ys, g2, b2):
    t, d = h1.shape
    tm = ROW_TILE
    n_steps = t // tm
    row = lambda i, ng: (i, 0)
    const = lambda i, ng: (0, 0)
    grid_spec = pltpu.PrefetchScalarGridSpec(
        num_scalar_prefetch=1,
        grid=(n_steps,),
        in_specs=[
            pl.BlockSpec((1, 1, SORT_GROUPS), lambda i, ng: (i, 0, 0), memory_space=pltpu.SMEM),
            pl.BlockSpec((1, 1, SORT_GROUPS), lambda i, ng: (jnp.minimum(i + 1, n_steps - 1), 0, 0),
                         memory_space=pltpu.SMEM),
            pl.BlockSpec((tm, TOP_K), row), pl.BlockSpec((tm, TOP_K), row), pl.BlockSpec((tm, d), row),
            pl.BlockSpec(memory_space=pl.ANY),
            pl.BlockSpec((1, d), const), pl.BlockSpec((1, d), const),
        ],
        out_specs=pl.BlockSpec((tm, d), row),
        scratch_shapes=[pltpu.VMEM((2, SORT_ROWS, d // 2), I32), pltpu.SemaphoreType.DMA((2,))],
    )
    return pl.pallas_call(
        _combine_kernel,
        grid_spec=grid_spec,
        out_shape=jax.ShapeDtypeStruct((t, d), F32),
        compiler_params=pltpu.CompilerParams(dimension_semantics=("arbitrary",), vmem_limit_bytes=VMEM_LIMIT),
        name="moe_combine_ln2",
    )(ngroups, dstg, dstg, pos_c, gates, h1, ys, g2, b2)


def kernel(x, meta_tokens, ln_in_g, ln_in_b, rel_bias, w_in, lambda_q1, lambda_k1, lambda_q2, lambda_k2,
           subln_g, a_re, a_im, log_step, b_re, b_im, c_re, c_im, d_skip, w_glu, b_glu, w_out, ln1_g, ln1_b,
           w_router, b_router, w_gate, b_gate, w_up, b_up, w_down, b_down, ln2_g, ln2_b):
    bsz, seq, dm = x.shape
    assert seq % 512 == 0 and w_in.shape[0] == DEPTH == 1
    layer = 0
    row2 = lambda v: v.astype(F32).reshape(1, -1)

    w_in_b = w_in[layer].astype(BF16)
    gi, bi = row2(ln_in_g), row2(ln_in_b)
    q, k, vt, u = _inproj(x, gi, bi, w_in_b, tm=512, kv_tile=KV_TILE)
    meta = jnp.zeros((1, META_PAD, dm), x.dtype).at[0, :N_META].set(meta_tokens.astype(x.dtype))
    _, k_meta, vt_meta, u_meta = _inproj(meta, gi, bi, w_in_b, tm=META_PAD, kv_tile=META_PAD)

    lam_init = 0.8 - 0.6 * math.exp(-0.3 * layer)
    lam = (jnp.exp(jnp.sum(lambda_q1[layer].astype(F32) * lambda_k1[layer].astype(F32)))
           - jnp.exp(jnp.sum(lambda_q2[layer].astype(F32) * lambda_k2[layer].astype(F32))) + lam_init)
    att = _attention(lam.reshape(1), q, k, vt, k_meta[0], vt_meta[0, :, 0], _near_bias(rel_bias),
                     subln_g[layer].astype(F32).reshape(HEAD_W, 1), lam_init=lam_init)

    kpad, pt, qt, a1, a2 = _ssm_tables(a_re[layer], a_im[layer], log_step[layer], b_re[layer], b_im[layer],
                                     c_re[layer], c_im[layer], d_skip[layer])
    n_chunks = seq // SSM_T
    assert (n_chunks * bsz) % 16 == 0
    ug = jnp.transpose(u.reshape(bsz, n_chunks, SSM_T, SSM_G, SSM_CG), (3, 1, 0, 2, 4))
    ug = ug.reshape(SSM_G, n_chunks * bsz, SSM_T * SSM_CG)
    lead = jnp.zeros((8, N_META, SSM_W), BF16).at[0].set(u_meta[0, :N_META])
    ug_meta = jnp.transpose(lead.reshape(8, N_META, SSM_G, SSM_CG), (2, 0, 1, 3)).reshape(SSM_G, 8, N_META * SSM_CG)
    yg = _ssm(ug, ug_meta, kpad, pt, qt, a1, a2, n_chunks=n_chunks, bsz=bsz)
    y_ssm = jnp.transpose(yg.reshape(SSM_G, n_chunks, bsz, SSM_T, SSM_CG), (2, 1, 3, 0, 4))
    y_ssm = y_ssm.reshape(bsz, seq, SSM_W)

    t = bsz * seq
    h1, idx_t, gate_t, rank_t, cnt = _mix(
        x.reshape(t, dm), gi, bi, att.reshape(t, ATT_W), y_ssm.reshape(t, SSM_W),
        w_glu[layer].astype(BF16), row2(b_glu[layer]), w_out[layer].astype(BF16),
        row2(ln1_g[layer]), row2(ln1_b[layer]),
        jnp.transpose(w_router[layer].astype(F32)), b_router[layer].astype(F32).reshape(N_EXPERTS, 1))

    n_tiles = t // ROW_TILE
    experts = jnp.arange(N_EXPERTS, dtype=I32)
    tiles = jnp.arange(n_tiles, dtype=I32)
    tile_cnt = jnp.max(cnt, axis=-1).astype(I32)
    run = (tile_cnt + GROUP - 1) // GROUP * GROUP
    run_off = jnp.sum(jnp.where((experts[:, None] < experts[None, :])[None], run[:, :, None], 0), axis=1)
    run_before = jnp.sum(jnp.where((tiles[:, None] < tiles[None, :])[:, :, None], run[:, None, :], 0), axis=0)
    counts = jnp.sum(run, axis=0)
    padded = (counts + MOE_BLOCK - 1) // MOE_BLOCK * MOE_BLOCK
    padded_end = jnp.sum(jnp.where(experts[:, None] <= experts[None, :], padded[:, None], 0), axis=0)
    padded_start = padded_end - padded
    n_blocks = (t * TOP_K + n_tiles * N_EXPERTS * (GROUP - 1)) // MOE_BLOCK + N_EXPERTS
    block_row0 = jnp.arange(n_blocks, dtype=I32) * MOE_BLOCK
    block_expert = jnp.minimum(jnp.sum((padded_end[None, :] <= block_row0[:, None]).astype(I32), axis=1),
                               N_EXPERTS - 1)
    n_used = (padded_end[-1:] // MOE_BLOCK).astype(I32)
    ngroups = jnp.sum(run, axis=1) // GROUP
    local_row = jnp.arange(SORT_GROUPS, dtype=I32) * GROUP
    owner = jnp.minimum(jnp.sum(((run_off + run)[:, None, :] <= local_row[None, :, None]).astype(I32), axis=-1),
                        N_EXPERTS - 1)
    shift = (padded_start[None, :] + run_before - run_off) // GROUP
    dstg = jnp.sum(jnp.where(owner[..., None] == experts, shift[:, None, :], 0), axis=-1) + local_row // GROUP
    dstg = jnp.where(local_row[None, :] // GROUP < ngroups[:, None], dstg, 0).reshape(n_tiles, 1, SORT_GROUPS)
    run_off_tok = jnp.repeat(run_off, ROW_TILE, axis=0)
    pos_t = jnp.sum(jnp.where(idx_t[..., None] == experts, run_off_tok[None], 0), axis=-1) + rank_t

    xs = _dispatch(ngroups, ((padded_start + counts) // GROUP).astype(I32),
                   ((padded - counts) // GROUP).astype(I32), n_used, dstg, pos_t, h1, n_blocks * MOE_BLOCK)
    b3 = lambda v: v.astype(F32)[:, None, :]
    ys = _experts(block_expert, n_used, xs, w_gate[layer], b3(b_gate[layer]), w_up[layer], b3(b_up[layer]),
                  w_down[layer], b3(b_down[layer]))
    out = _combine(ngroups, dstg, jnp.transpose(pos_t), jnp.transpose(gate_t), h1, ys,
                   row2(ln2_g[layer]), row2(ln2_b[layer]))
    return out.reshape(bsz, seq, dm)
```

```python
import functools
import math

import jax
import jax.numpy as jnp
from jax import lax
from jax.experimental import pallas as pl
from jax.experimental.pallas import tpu as pltpu

F32 = jnp.float32
BF16 = jnp.bfloat16
I32 = jnp.int32

DEPTH = 1
N_META = 16
CHUNK = 64
ATT_HEADS = 4
HEAD_DIM = 64
HEAD_W = 2 * HEAD_DIM
ATT_W = ATT_HEADS * HEAD_W
SSM_W = 512
SSM_CG = 16
SSM_G = SSM_W // SSM_CG
SSM_N = 64
N_BUCKETS = 32
MAX_DISTANCE = 128
N_EXPERTS = 32
TOP_K = 4
SWIGLU_LIMIT = 7.0
SWIGLU_ALPHA = 1.702
LN_EPS = 1e-5
NEG_INF = -1e30
DEEPNORM_ALPHA = (2.0 * DEPTH) ** 0.25
LOG2E = 1.4426950408889634

Q_TILE = 512
KV_TILE = 512
V_ONES = 16
V_ROWS = HEAD_W + V_ONES
META_PAD = 128
SSM_T = 64
MIX_TILE = 512
ROW_TILE = 256
MOE_BLOCK = 512
GROUP = 8
COPY_BATCH = 32
SORT_ROWS = 1280
SORT_GROUPS = SORT_ROWS // GROUP
assert SORT_ROWS >= ROW_TILE * TOP_K + N_EXPERTS * (GROUP - 1)
V7X_VMEM_BYTES = 64 * 1024 * 1024
VMEM_LIMIT = V7X_VMEM_BYTES * 7 // 8


def _layer_norm(x, g, b):
    mu = jnp.mean(x, axis=-1, keepdims=True)
    xc = x - mu
    var = jnp.mean(xc * xc, axis=-1, keepdims=True)
    return xc * lax.rsqrt(var + LN_EPS) * g + b


def _dot(a, b):
    return jnp.dot(a, b, preferred_element_type=F32)


def _dot_nt(a, b, precision=None):
    return lax.dot_general(a, b, (((1,), (1,)), ((), ())), precision=precision,
                           preferred_element_type=F32)


def _inproj_kernel(x_ref, g_ref, b_ref, w_ref, q_ref, k_ref, vt_ref, u_ref, *, kv_tile, q_scale):
    h = _layer_norm(x_ref[0], g_ref[...], b_ref[...]).astype(BF16)
    tm = h.shape[0]
    q_ref[0] = (_dot(h, w_ref[:, 0:ATT_W]) * q_scale).astype(BF16)
    k_ref[0] = _dot(h, w_ref[:, ATT_W:2 * ATT_W]).astype(BF16)
    v = _dot(h, w_ref[:, 2 * ATT_W:3 * ATT_W])
    ones = jnp.ones((V_ONES, kv_tile), BF16)
    for hh in range(ATT_HEADS):
        vt = v[:, hh * HEAD_W:(hh + 1) * HEAD_W].T.astype(BF16)
        for j in range(tm // kv_tile):
            vt_ref[0, hh, j, 0:HEAD_W, :] = vt[:, j * kv_tile:(j + 1) * kv_tile]
            vt_ref[0, hh, j, HEAD_W:V_ROWS, :] = ones
    u_ref[0] = _dot(h, w_ref[:, 3 * ATT_W:]).astype(BF16)


def _inproj(x, g, b, w, *, tm, kv_tile):
    bsz, s, d = x.shape
    q_scale = HEAD_DIM ** -0.5 * LOG2E
    n_cols = w.shape[1]
    row = lambda bi, i: (bi, i, 0)
    return pl.pallas_call(
        functools.partial(_inproj_kernel, kv_tile=kv_tile, q_scale=q_scale),
        grid=(bsz, s // tm),
        in_specs=[
            pl.BlockSpec((1, tm, d), row),
            pl.BlockSpec((1, d), lambda bi, i: (0, 0)),
            pl.BlockSpec((1, d), lambda bi, i: (0, 0)),
            pl.BlockSpec((d, n_cols), lambda bi, i: (0, 0)),
        ],
        out_specs=[
            pl.BlockSpec((1, tm, ATT_W), row),
            pl.BlockSpec((1, tm, ATT_W), row),
            pl.BlockSpec((1, ATT_HEADS, tm // kv_tile, V_ROWS, kv_tile), lambda bi, i: (bi, 0, i, 0, 0)),
            pl.BlockSpec((1, tm, SSM_W), row),
        ],
        out_shape=[
            jax.ShapeDtypeStruct((bsz, s, ATT_W), BF16),
            jax.ShapeDtypeStruct((bsz, s, ATT_W), BF16),
            jax.ShapeDtypeStruct((bsz, ATT_HEADS, s // kv_tile, V_ROWS, kv_tile), BF16),
            jax.ShapeDtypeStruct((bsz, s, SSM_W), BF16),
        ],
        compiler_params=pltpu.CompilerParams(
            dimension_semantics=("parallel", "parallel"), vmem_limit_bytes=VMEM_LIMIT),
        name="inproj",
    )(x, g, b, w)


def _attn_kernel(lam_ref, q_ref, k_ref, vt_ref, km_ref, vtm_ref, bias_ref, g_ref, o_ref,
                 m_ref, acc_ref, qz_ref, sa_ref, sb_ref, sn_ref, *, out_scale):
    i = pl.program_id(2)
    qt = q_ref[0].astype(F32).T
    feat = lax.broadcasted_iota(I32, qt.shape, 0)
    qz_ref[:, 0:Q_TILE] = jnp.where(feat < HEAD_DIM, qt, 0.0).astype(BF16)
    qz_ref[:, Q_TILE:] = jnp.where(feat >= HEAD_DIM, qt, 0.0).astype(BF16)

    m_ref[...] = jnp.full(m_ref.shape, NEG_INF, F32)
    acc_ref[...] = jnp.zeros(acc_ref.shape, F32)

    def scores(kt):
        return _dot(kt, qz_ref[...])

    def k_tile(t):
        return k_ref[0, pl.ds(pl.multiple_of(t * KV_TILE, KV_TILE), KV_TILE), :]

    def absorb(s, pv):
        m_old = m_ref[...]
        m_new = jnp.maximum(m_old, jnp.max(s, axis=0, keepdims=True))
        alpha = jnp.exp2(m_old - m_new)
        p = jnp.exp2(s - m_new).astype(BF16)
        acc_ref[...] = acc_ref[...] * alpha + pv(p)
        m_ref[...] = m_new

    def tile_pv(t):
        return lambda p: _dot(vt_ref[0, 0, t], p)

    n_far = jnp.maximum(i - 1, 0)
    peel = n_far % 2

    sn_ref[...] = scores(km_ref[...])
    sa_ref[...] = scores(k_tile(0))
    absorb(sn_ref[...] + bias_ref[0, 0, 2 * KV_TILE:, :], lambda p: _dot(vtm_ref[0], p))

    @pl.when(peel == 1)
    def _():
        sb_ref[...] = scores(k_tile(1))
        absorb(sa_ref[...], tile_pv(0))
        sa_ref[...] = sb_ref[...]

    def far_pair(j, carry):
        t0 = peel + 2 * j
        s_cur = sa_ref[...]
        sb_ref[...] = scores(k_tile(t0 + 1))
        absorb(s_cur, tile_pv(t0))
        s_cur = sb_ref[...]
        sa_ref[...] = scores(k_tile(t0 + 2))
        absorb(s_cur, tile_pv(t0 + 1))
        return carry

    lax.fori_loop(0, n_far // 2, far_pair, 0)

    sb_ref[...] = scores(k_tile(i))
    absorb(sa_ref[...] + bias_ref[0, 0, 0:KV_TILE, :], tile_pv(n_far))
    absorb(sb_ref[...] + bias_ref[0, 0, KV_TILE:2 * KV_TILE, :], tile_pv(i))

    acc = acc_ref[...]
    lam = lam_ref[0]
    o1 = acc[0:HEAD_W, 0:Q_TILE] / acc[HEAD_W:HEAD_W + 1, 0:Q_TILE]
    o2 = acc[0:HEAD_W, Q_TILE:] / acc[HEAD_W:HEAD_W + 1, Q_TILE:]
    o = o1 - lam * o2
    ms = jnp.mean(o * o, axis=0, keepdims=True)
    o = o * lax.rsqrt(ms + LN_EPS) * g_ref[...] * out_scale
    o_ref[0] = o.T.astype(o_ref.dtype)


def _attention(lam, q, k, vt, k_meta, vt_meta, bias, subln_g, *, lam_init):
    bsz, s, _ = q.shape
    nq = s // Q_TILE
    n_near = META_PAD + 2 * KV_TILE
    grid_spec = pltpu.PrefetchScalarGridSpec(
        num_scalar_prefetch=1,
        grid=(bsz, ATT_HEADS, nq),
        in_specs=[
            pl.BlockSpec((1, Q_TILE, HEAD_W), lambda b, h, i, lam: (b, i, h)),
            pl.BlockSpec((1, s, HEAD_W), lambda b, h, i, lam: (b, 0, h)),
            pl.BlockSpec((1, 1, s // KV_TILE, V_ROWS, KV_TILE), lambda b, h, i, lam: (b, h, 0, 0, 0)),
            pl.BlockSpec((META_PAD, HEAD_W), lambda b, h, i, lam: (0, h)),
            pl.BlockSpec((1, V_ROWS, META_PAD), lambda b, h, i, lam: (h, 0, 0)),
            pl.BlockSpec((1, 1, n_near, 2 * Q_TILE), lambda b, h, i, lam: (h, jnp.minimum(i, 1), 0, 0)),
            pl.BlockSpec((HEAD_W, 1), lambda b, h, i, lam: (0, 0)),
        ],
        out_specs=pl.BlockSpec((1, Q_TILE, HEAD_W), lambda b, h, i, lam: (b, i, h)),
        scratch_shapes=[
            pltpu.VMEM((1, 2 * Q_TILE), F32), pltpu.VMEM((V_ROWS, 2 * Q_TILE), F32),
            pltpu.VMEM((HEAD_W, 2 * Q_TILE), BF16),
            pltpu.VMEM((KV_TILE, 2 * Q_TILE), F32), pltpu.VMEM((KV_TILE, 2 * Q_TILE), F32),
            pltpu.VMEM((META_PAD, 2 * Q_TILE), F32),
        ],
    )
    return pl.pallas_call(
        functools.partial(_attn_kernel, out_scale=1.0 - lam_init),
        grid_spec=grid_spec,
        out_shape=jax.ShapeDtypeStruct((bsz, s, ATT_W), BF16),
        compiler_params=pltpu.CompilerParams(
            dimension_semantics=("parallel", "parallel", "arbitrary"), vmem_limit_bytes=VMEM_LIMIT),
        name="diff_attention",
    )(lam, q, k, vt, k_meta, vt_meta, bias, subln_g)


def _t5_bucket(rel):
    nb = N_BUCKETS // 2
    max_exact = nb // 2
    ret = jnp.where(rel > 0, nb, 0)
    n = jnp.abs(rel)
    n_f = jnp.maximum(n, 1).astype(F32)
    large = max_exact + (jnp.log(n_f / max_exact) / math.log(MAX_DISTANCE / max_exact)
                         * (nb - max_exact)).astype(I32)
    large = jnp.minimum(large, nb - 1)
    return ret + jnp.where(n < max_exact, n, large)


def _near_bias(rel_bias):
    table = rel_bias.astype(F32)
    far = table[N_BUCKETS // 2 - 1]
    c = jnp.arange(Q_TILE, dtype=I32)[None, :]
    r = jnp.arange(KV_TILE, dtype=I32)[:, None]

    shifted = jnp.transpose(table - far) * LOG2E

    def bias_of(offset, rows, cols=Q_TILE):
        rel = jnp.arange(rows, dtype=I32)[:, None] - c[:, :cols] + offset
        hot = _t5_bucket(rel)[None, :, :, None] == jnp.arange(N_BUCKETS, dtype=I32)
        return jnp.sum(jnp.where(hot, shifted[:, None, None, :], 0.0), axis=-1)

    nd = KV_TILE // MAX_DISTANCE
    own_bias = sum(
        (jnp.eye(nd, k=-d, dtype=F32)[None, :, None, :, None]
         * bias_of(d * MAX_DISTANCE, MAX_DISTANCE, MAX_DISTANCE)[:, None, :, None, :]).reshape(
             ATT_HEADS, KV_TILE, Q_TILE)
        for d in (-1, 0, 1))
    own = jnp.where((r // CHUNK <= c // CHUNK)[None], own_bias, NEG_INF)
    near = MAX_DISTANCE
    prev = jnp.zeros((ATT_HEADS, KV_TILE, Q_TILE), F32).at[:, KV_TILE - near:, :near].set(bias_of(-near, near, near))
    rm = jnp.arange(META_PAD, dtype=I32)[:, None]
    meta_ok = (rm < N_META)[None]
    meta0 = jnp.where(meta_ok, jnp.pad(bias_of(-N_META, N_META), ((0, 0), (0, META_PAD - N_META), (0, 0))), NEG_INF)
    meta1 = jnp.where(meta_ok, jnp.zeros((ATT_HEADS, META_PAD, Q_TILE), F32), NEG_INF)
    assert N_META + Q_TILE - (N_META - 1) >= MAX_DISTANCE and KV_TILE >= near and Q_TILE >= near
    v0 = jnp.concatenate([jnp.full_like(prev, NEG_INF), own, meta0], axis=1)
    v1 = jnp.concatenate([prev, own, meta1], axis=1)
    both = jnp.stack([v0, v1], axis=1)
    return jnp.concatenate([both, both], axis=-1)


def _ssm_tables(a_re, a_im, log_step, b_re, b_im, c_re, c_im, d_skip):
    hi = lax.Precision.HIGHEST
    t_len = SSM_T
    step = jnp.exp(log_step.astype(F32))[:, None]
    ar = jnp.minimum(a_re.astype(F32), -1e-4)
    ai = a_im.astype(F32)
    mag = jnp.exp(step * ar)
    ph = step * ai
    abar_re = mag * jnp.cos(ph)
    abar_im = mag * jnp.sin(ph)
    den = ar * ar + ai * ai
    e_re = abar_re - 1.0
    e_im = abar_im
    f_re = (e_re * ar + e_im * ai) / den
    f_im = (e_im * ar - e_re * ai) / den
    br = b_re.astype(F32)
    bi = b_im.astype(F32)
    bb_re = f_re[..., None] * br - f_im[..., None] * bi
    bb_im = f_re[..., None] * bi + f_im[..., None] * br
    tau = jnp.arange(t_len + 1, dtype=F32)[None, :, None]
    pmag = jnp.exp(tau * (step * ar)[:, None, :])
    pph = tau * ph[:, None, :]
    pw_re = pmag * jnp.cos(pph)
    pw_im = pmag * jnp.sin(pph)
    bbt_re = jnp.swapaxes(bb_re, 1, 2)[:, None]
    bbt_im = jnp.swapaxes(bb_im, 1, 2)[:, None]

    def times_bbar(p_re, p_im):
        return (p_re[:, :, None, :] * bbt_re - p_im[:, :, None, :] * bbt_im,
                p_re[:, :, None, :] * bbt_im + p_im[:, :, None, :] * bbt_re)

    cr = c_re.astype(F32)
    ci = c_im.astype(F32)
    bi_re = jnp.swapaxes(bb_re, 1, 2)[:, :, None, :]
    bi_im = jnp.swapaxes(bb_im, 1, 2)[:, :, None, :]
    cb_re = cr[:, None] * bi_re - ci[:, None] * bi_im
    cb_im = cr[:, None] * bi_im + ci[:, None] * bi_re
    kern = jnp.einsum('gikm,gtm->gitk', jnp.concatenate([cb_re, -cb_im], axis=-1),
                      jnp.concatenate([pw_re[:, :t_len], pw_im[:, :t_len]], axis=-1), precision=hi)
    skip = d_skip.astype(F32).reshape(SSM_G, SSM_CG)
    kern = kern.at[:, :, 0, :].add(skip[:, None, :] * jnp.eye(SSM_CG, dtype=F32))
    kflat = kern.reshape(SSM_G, SSM_CG, t_len * SSM_CG)
    kpad = jnp.concatenate([jnp.zeros_like(kflat), kflat], axis=-1)
    back = jnp.arange(t_len - 1, -1, -1, dtype=F32)[None, :, None]
    bmag = jnp.exp(back * (step * ar)[:, None, :])
    bph = back * ph[:, None, :]
    pt_re, pt_im = times_bbar(bmag * jnp.cos(bph), bmag * jnp.sin(bph))
    pt = jnp.concatenate([pt_re, pt_im], axis=-1).reshape(SSM_G, t_len * SSM_CG, 2 * SSM_N)
    up_re = jnp.swapaxes(pw_re[:, 1:], 1, 2)[..., None]
    up_im = jnp.swapaxes(pw_im[:, 1:], 1, 2)[..., None]
    crt = jnp.swapaxes(cr, 1, 2)[:, :, None, :]
    cit = jnp.swapaxes(ci, 1, 2)[:, :, None, :]
    q_re = crt * up_re - cit * up_im
    q_im = crt * up_im + cit * up_re
    qt = jnp.concatenate([q_re, -q_im], axis=1).reshape(SSM_G, 2 * SSM_N, t_len * SSM_CG)
    at_re = pw_re[:, t_len]
    at_im = pw_im[:, t_len]
    a1 = jnp.concatenate([at_re, at_re], axis=-1)
    a2 = jnp.concatenate([-at_im, at_im], axis=-1)
    return kpad, pt.astype(BF16), qt.astype(BF16), a1, a2


def _ssm_state_kernel(u_ref, pt_ref, s_ref):
    s_ref[0] = _dot(u_ref[0], pt_ref[0])


def _ssm_scan_kernel(s_ref, x0_ref, a1_ref, a2_ref, x_ref, *, n_chunks, bsz):
    a1 = a1_ref[...][:, None, :]
    a2 = a2_ref[...][:, None, :]

    def body(c, x):
        rows = pl.ds(c * bsz, bsz)
        x_ref[:, rows, :] = x
        return a1 * x + a2 * pltpu.roll(x, SSM_N, 2) + s_ref[:, rows, :]

    x0 = jnp.broadcast_to(x0_ref[:, 0:1, :], (SSM_G, bsz, 2 * SSM_N))
    lax.fori_loop(0, n_chunks, body, x0)


def _ssm_out_kernel(u_ref, x_ref, kp_ref, qt_ref, y_ref, mt_ref):
    kp = kp_ref[0]
    width = SSM_T * SSM_CG
    lane_tile = 128
    for sub in range(0, lane_tile, SSM_CG):
        shifted = kp if sub == 0 else pltpu.roll(kp, 2 * width - sub, 1)
        for s in range(SSM_T):
            off = (SSM_T - s) * SSM_CG
            if off % lane_tile == sub:
                base = off - sub
                mt_ref[s * SSM_CG:(s + 1) * SSM_CG, :] = shifted[:, base:base + width].astype(BF16)
    x = x_ref[0]
    x_hi = x.astype(BF16)
    x_lo = (x - x_hi.astype(F32)).astype(BF16)
    y = _dot(u_ref[0], mt_ref[...]) + _dot(x_hi, qt_ref[0]) + _dot(x_lo, qt_ref[0])
    y_ref[0] = y.astype(y_ref.dtype)


def _ssm_chunk_state(ug, pt):
    g, r, w = ug.shape
    n2 = 2 * SSM_N
    per_g = lambda gi: (gi, 0, 0)
    return pl.pallas_call(
        _ssm_state_kernel,
        grid=(g,),
        in_specs=[pl.BlockSpec((1, r, w), per_g), pl.BlockSpec((1, w, n2), per_g)],
        out_specs=pl.BlockSpec((1, r, n2), per_g),
        out_shape=jax.ShapeDtypeStruct((g, r, n2), F32),
        compiler_params=pltpu.CompilerParams(dimension_semantics=("parallel",), vmem_limit_bytes=VMEM_LIMIT),
        name="ssm_chunk_state",
    )(ug, pt)


def _ssm(ug, ug_meta, kpad, pt, qt, a1, a2, *, n_chunks, bsz):
    g, r, w = ug.shape
    n2 = 2 * SSM_N
    per_g = lambda gi: (gi, 0, 0)
    params = pltpu.CompilerParams(dimension_semantics=("parallel",), vmem_limit_bytes=VMEM_LIMIT)
    s = _ssm_chunk_state(ug, pt)
    x0 = _ssm_chunk_state(ug_meta, pt[:, w - ug_meta.shape[2]:, :])
    x = pl.pallas_call(
        functools.partial(_ssm_scan_kernel, n_chunks=n_chunks, bsz=bsz),
        out_shape=jax.ShapeDtypeStruct((g, r, n2), F32),
        compiler_params=pltpu.CompilerParams(vmem_limit_bytes=VMEM_LIMIT),
        name="ssm_chunk_scan",
    )(s, x0, a1, a2)
    return pl.pallas_call(
        _ssm_out_kernel,
        grid=(g,),
        in_specs=[pl.BlockSpec((1, r, w), per_g), pl.BlockSpec((1, r, n2), per_g),
                  pl.BlockSpec((1, SSM_CG, 2 * w), per_g), pl.BlockSpec((1, n2, w), per_g)],
        out_specs=pl.BlockSpec((1, r, w), per_g),
        out_shape=jax.ShapeDtypeStruct((g, r, w), BF16),
        scratch_shapes=[pltpu.VMEM((w, w), BF16)],
        compiler_params=params,
        name="ssm_output",
    )(ug, x, kpad, qt)


def _mix_kernel(x_ref, gi_ref, bi_ref, att_ref, y_ref, wglu_ref, bglu_ref, wout_ref, g1_ref, b1_ref,
                wr_ref, br_ref, h1_ref, idx_ref, gate_ref, rank_ref, cnt_ref):
    h0 = _layer_norm(x_ref[...], gi_ref[...], bi_ref[...])
    y = y_ref[...].astype(F32)
    y = y * (0.5 * (1.0 + jnp.tanh(math.sqrt(2.0 / math.pi) * (y + 0.044715 * (y * y * y)))))
    y = y * jax.nn.sigmoid(_dot(y.astype(BF16), wglu_ref[...]) + bglu_ref[...])
    mix = _dot(att_ref[...], wout_ref[0:ATT_W, :]) + _dot(y.astype(BF16), wout_ref[ATT_W:, :])
    h1 = _layer_norm(DEEPNORM_ALPHA * h0 + mix, g1_ref[...], b1_ref[...])
    h1_ref[...] = h1

    logits = _dot_nt(wr_ref[...], h1, precision=lax.Precision.HIGHEST) + br_ref[...]
    tm = logits.shape[1]
    eidx = lax.broadcasted_iota(I32, logits.shape, 0)
    vals, hots = [], []
    rest = logits
    for _ in range(TOP_K):
        mx = jnp.max(rest, axis=0, keepdims=True)
        first = jnp.min(jnp.where(rest == mx, eidx, N_EXPERTS), axis=0, keepdims=True)
        hot = eidx == first
        vals.append(mx)
        hots.append(hot)
        rest = jnp.where(hot, -jnp.inf, rest)
    exps = [jnp.exp(v - vals[0]) for v in vals]
    denom = exps[0] + exps[1] + exps[2] + exps[3]
    gate_ref[...] = jnp.concatenate([e / denom for e in exps], axis=0)
    idx_ref[...] = jnp.concatenate(
        [jnp.sum(jnp.where(h, eidx, 0), axis=0, keepdims=True) for h in hots], axis=0)

    hot_all = (hots[0] | hots[1] | hots[2] | hots[3]).astype(F32)
    sub = ROW_TILE
    tri = (lax.broadcasted_iota(I32, (sub, sub), 0) < lax.broadcasted_iota(I32, (sub, sub), 1)).astype(BF16)
    for part in range(tm // sub):
        cols = slice(part * sub, (part + 1) * sub)
        before = _dot(hot_all[:, cols].astype(BF16), tri)
        rank_ref[:, cols] = jnp.concatenate(
            [jnp.sum(jnp.where(h[:, cols], before, 0.0), axis=0, keepdims=True) for h in hots], axis=0).astype(I32)
        cnt_ref[part] = jnp.broadcast_to(jnp.sum(hot_all[:, cols], axis=1, keepdims=True), cnt_ref.shape[1:])


def _mix(x2, gi, bi, att, y, wglu, bglu, wout, g1, b1, wr_t, br):
    t, d = x2.shape
    tm = MIX_TILE
    row = lambda i: (i, 0)
    col = lambda i: (0, i)
    const = lambda i: (0, 0)
    return pl.pallas_call(
        _mix_kernel,
        grid=(t // tm,),
        in_specs=[
            pl.BlockSpec((tm, d), row), pl.BlockSpec((1, d), const), pl.BlockSpec((1, d), const),
            pl.BlockSpec((tm, ATT_W), row), pl.BlockSpec((tm, SSM_W), row),
            pl.BlockSpec((SSM_W, SSM_W), const), pl.BlockSpec((1, SSM_W), const),
            pl.BlockSpec((d, d), const), pl.BlockSpec((1, d), const), pl.BlockSpec((1, d), const),
            pl.BlockSpec((N_EXPERTS, d), const), pl.BlockSpec((N_EXPERTS, 1), const),
        ],
        out_specs=[
            pl.BlockSpec((tm, d), row),
            pl.BlockSpec((TOP_K, tm), col), pl.BlockSpec((TOP_K, tm), col), pl.BlockSpec((TOP_K, tm), col),
            pl.BlockSpec((tm // ROW_TILE, N_EXPERTS, 128), lambda i: (i, 0, 0)),
        ],
        out_shape=[
            jax.ShapeDtypeStruct((t, d), F32),
            jax.ShapeDtypeStruct((TOP_K, t), I32), jax.ShapeDtypeStruct((TOP_K, t), F32),
            jax.ShapeDtypeStruct((TOP_K, t), I32),
            jax.ShapeDtypeStruct((t // ROW_TILE, N_EXPERTS, 128), F32),
        ],
        compiler_params=pltpu.CompilerParams(dimension_semantics=("parallel",), vmem_limit_bytes=VMEM_LIMIT),
        name="mix_ln1_router",
    )(x2, gi, bi, att, y, wglu, bglu, wout, g1, b1, wr_t, br)


_HI_BITS = -65536


def _pack_halves(x):
    half = x.shape[1] // 2
    lo = lax.bitcast_convert_type(x[:, :half], I32)
    hi = lax.bitcast_convert_type(x[:, half:], I32)
    return lax.shift_right_logical(lo, 16) | (hi & _HI_BITS)


def _unpack_halves(w):
    lo = lax.bitcast_convert_type(lax.shift_left(w, 16), F32)
    hi = lax.bitcast_convert_type(w & _HI_BITS, F32)
    return lo.astype(BF16), hi.astype(BF16)


def _group_copy(src, src_group, dst, dst_group, sem):
    return pltpu.make_async_copy(src.at[pl.ds(pl.multiple_of(src_group * GROUP, GROUP), GROUP), :],
                                 dst.at[pl.ds(pl.multiple_of(dst_group * GROUP, GROUP), GROUP), :], sem)


def _batched(count, fn):
    def body(q, c):
        fn(q * COPY_BATCH, COPY_BATCH)
        return c

    whole = count // COPY_BATCH
    lax.fori_loop(0, whole, body, 0)
    rest = count - whole * COPY_BATCH
    n, done = COPY_BATCH // 2, whole * COPY_BATCH
    while n >= 1:
        take = (rest // n) % 2

        @pl.when(take == 1)
        def _(n=n, done=done):
            fn(done, n)

        done = done + take * n
        n //= 2


def _dispatch_kernel(ngroups_ref, pad_start_ref, pad_count_ref, n_used_ref, dstg_ref, pos_ref, h1_ref, xs_hbm,
                     sorted_ref, zero_ref, sems, zsem):
    step = pl.program_id(0)
    n_steps = pl.num_programs(0)
    slot = step % 2
    n_blocks = xs_hbm.shape[0] // MOE_BLOCK

    def zero_block(blk):
        return pltpu.make_async_copy(zero_ref, xs_hbm.at[pl.ds(blk * MOE_BLOCK, MOE_BLOCK), :], zsem)

    @pl.when(step == 0)
    def _():
        zero_ref[...] = jnp.zeros(zero_ref.shape, I32)
        for e in range(N_EXPERTS):
            start = pad_start_ref[e]
            count = pad_count_ref[e]

            def fill(j, c):
                _group_copy(zero_ref, 0, xs_hbm, start + j, zsem).start()
                return c

            def drain(j, c):
                _group_copy(zero_ref, 0, xs_hbm, start, zsem).wait()
                return c

            lax.fori_loop(0, count, fill, 0)
            lax.fori_loop(0, count, drain, 0)

        def fill_block(blk, c):
            zero_block(blk).start()
            return c

        def drain_block(blk, c):
            zero_block(blk).wait()
            return c

        lax.fori_loop(n_used_ref[0], n_blocks, fill_block, 0)
        lax.fori_loop(n_used_ref[0], n_blocks, drain_block, 0)

    pos = pos_ref[...]
    rows = lax.broadcasted_iota(I32, (SORT_ROWS, pos.shape[1]), 0)
    place = rows == pos[0:1]
    for k in range(1, TOP_K):
        place = place | (rows == pos[k:k + 1])
    sorted_ref[slot] = _pack_halves(_dot(place.astype(BF16), h1_ref[...].astype(BF16)))

    _batched(ngroups_ref[step],
             lambda j, n: [_group_copy(sorted_ref.at[slot], j + r, xs_hbm, dstg_ref[0, 0, j + r],
                                       sems.at[slot]).start() for r in range(n)])

    def wait_tile(s, count):
        _batched(count, lambda j, n: pltpu.make_async_copy(
            sorted_ref.at[s, pl.ds(0, n * GROUP), :], xs_hbm.at[pl.ds(0, n * GROUP), :], sems.at[s]).wait())

    @pl.when(step > 0)
    def _():
        wait_tile(1 - slot, ngroups_ref[jnp.maximum(step - 1, 0)])

    @pl.when(step == n_steps - 1)
    def _():
        wait_tile(slot, ngroups_ref[step])


def _dispatch(ngroups, pad_start, pad_count, n_used, dstg, pos_t, h1, n_rows):
    t, d = h1.shape
    tm = ROW_TILE
    grid_spec = pltpu.PrefetchScalarGridSpec(
        num_scalar_prefetch=4,
        grid=(t // tm,),
        in_specs=[
            pl.BlockSpec((1, 1, SORT_GROUPS), lambda i, *_: (i, 0, 0), memory_space=pltpu.SMEM),
            pl.BlockSpec((TOP_K, tm), lambda i, *_: (0, i)),
            pl.BlockSpec((tm, d), lambda i, *_: (i, 0)),
        ],
        out_specs=pl.BlockSpec(memory_space=pl.ANY),
        scratch_shapes=[pltpu.VMEM((2, SORT_ROWS, d // 2), I32), pltpu.VMEM((MOE_BLOCK, d // 2), I32),
                        pltpu.SemaphoreType.DMA((2,)), pltpu.SemaphoreType.DMA],
    )
    return pl.pallas_call(
        _dispatch_kernel,
        grid_spec=grid_spec,
        out_shape=jax.ShapeDtypeStruct((n_rows, d // 2), I32),
        compiler_params=pltpu.CompilerParams(dimension_semantics=("arbitrary",), vmem_limit_bytes=VMEM_LIMIT,
                                             has_side_effects=True),
        name="moe_dispatch",
    )(ngroups, pad_start, pad_count, n_used, dstg, pos_t, h1)


def _expert_kernel(be_ref, nb_ref, x_ref, wg_ref, bg_ref, wu_ref, bu_ref, wd_ref, bd_ref, y_ref,
                   wg_b, wu_b, wd_b):
    i = pl.program_id(0)

    @pl.when(i < nb_ref[0])
    def _():
        @pl.when((i == 0) | (be_ref[i] != be_ref[jnp.maximum(i - 1, 0)]))
        def _():
            wg_b[...] = wg_ref[0].astype(BF16)
            wu_b[...] = wu_ref[0].astype(BF16)
            wd_b[...] = wd_ref[0].astype(BF16)

        x_lo, x_hi = _unpack_halves(x_ref[...])
        half = x_lo.shape[1]
        gate = _dot(x_lo, wg_b[0:half, :]) + _dot(x_hi, wg_b[half:, :]) + bg_ref[0]
        up = _dot(x_lo, wu_b[0:half, :]) + _dot(x_hi, wu_b[half:, :]) + bu_ref[0]
        gate = jnp.minimum(gate, SWIGLU_LIMIT)
        up = jnp.clip(up, -SWIGLU_LIMIT, SWIGLU_LIMIT)
        act = (up + 1.0) * gate * jax.nn.sigmoid(gate * SWIGLU_ALPHA)
        y = _dot(act.astype(BF16), wd_b[...]) + bd_ref[0]
        y_ref[...] = _pack_halves(y.astype(BF16).astype(F32))

    @pl.when(i >= nb_ref[0])
    def _():
        y_ref[...] = jnp.zeros(y_ref.shape, I32)


def _experts(block_expert, n_used, xs, wg, bg, wu, bu, wd, bd):
    n_rows, packed = xs.shape
    d, dff = wg.shape[1], wg.shape[2]
    assert packed * 2 == d and wd.shape[2] == d
    nb = n_rows // MOE_BLOCK
    blk = lambda i, be, nu: (jnp.minimum(i, nu[0] - 1), 0)
    wsel = lambda i, be, nu: (be[jnp.minimum(i, nu[0] - 1)], 0, 0)
    grid_spec = pltpu.PrefetchScalarGridSpec(
        num_scalar_prefetch=2,
        grid=(nb,),
        in_specs=[
            pl.BlockSpec((MOE_BLOCK, packed), blk),
            pl.BlockSpec((1, d, dff), wsel), pl.BlockSpec((1, 1, dff), wsel),
            pl.BlockSpec((1, d, dff), wsel), pl.BlockSpec((1, 1, dff), wsel),
            pl.BlockSpec((1, dff, d), wsel), pl.BlockSpec((1, 1, d), wsel),
        ],
        out_specs=pl.BlockSpec((MOE_BLOCK, packed), lambda i, be, nu: (i, 0)),
        scratch_shapes=[pltpu.VMEM((d, dff), BF16), pltpu.VMEM((d, dff), BF16), pltpu.VMEM((dff, d), BF16)],
    )
    return pl.pallas_call(
        _expert_kernel,
        grid_spec=grid_spec,
        out_shape=jax.ShapeDtypeStruct((n_rows, packed), I32),
        compiler_params=pltpu.CompilerParams(dimension_semantics=("arbitrary",), vmem_limit_bytes=VMEM_LIMIT),
        name="moe_experts",
    )(block_expert, n_used, xs, wg, bg, wu, bu, wd, bd)


def _combine_kernel(ngroups_ref, dstg_ref, dstg_next_ref, pos_ref, gate_ref, h1_ref, ys_hbm, g2_ref, b2_ref,
                    o_ref, buf, sems):
    step = pl.program_id(0)
    n_steps = pl.num_programs(0)
    slot = step % 2

    def fetch(dref, s, j):
        return _group_copy(ys_hbm, dref[0, 0, j], buf.at[s], j, sems.at[s])

    def issue(dref, s, count):
        _batched(count, lambda j, n: [fetch(dref, s, j + r).start() for r in range(n)])

    @pl.when(step == 0)
    def _():
        buf[...] = jnp.zeros(buf.shape, I32)
        issue(dstg_ref, 0, ngroups_ref[0])

    @pl.when(step + 1 < n_steps)
    def _():
        issue(dstg_next_ref, 1 - slot, ngroups_ref[jnp.minimum(step + 1, n_steps - 1)])

    _batched(ngroups_ref[step], lambda j, n: pltpu.make_async_copy(
        ys_hbm.at[pl.ds(0, n * GROUP), :], buf.at[slot, pl.ds(0, n * GROUP), :], sems.at[slot]).wait())

    pos = pos_ref[...]
    gate = gate_ref[...]
    lanes = lax.broadcasted_iota(I32, (pos.shape[0], SORT_ROWS), 1)
    sel = jnp.where(lanes == pos[:, 0:1], gate[:, 0:1], 0.0)
    for k in range(1, TOP_K):
        sel = sel + jnp.where(lanes == pos[:, k:k + 1], gate[:, k:k + 1], 0.0)
    sel_hi = sel.astype(BF16)
    sel_lo = (sel - sel_hi.astype(F32)).astype(BF16)
    y_lo, y_hi = _unpack_halves(buf[slot])
    ffn = jnp.concatenate([_dot(sel_hi, y_lo) + _dot(sel_lo, y_lo), _dot(sel_hi, y_hi) + _dot(sel_lo, y_hi)], axis=1)
    o_ref[...] = _layer_norm(DEEPNORM_ALPHA * h1_ref[...] + ffn, g2_ref[...], b2_ref[...])


def _combine(ngroups, dstg, pos_c, gates, h1, ys, g2, b2):
    t, d = h1.shape
    tm = ROW_TILE
    n_steps = t // tm
    row = lambda i, ng: (i, 0)
    const = lambda i, ng: (0, 0)
    grid_spec = pltpu.PrefetchScalarGridSpec(
        num_scalar_prefetch=1,
        grid=(n_steps,),
        in_specs=[
            pl.BlockSpec((1, 1, SORT_GROUPS), lambda i, ng: (i, 0, 0), memory_space=pltpu.SMEM),
            pl.BlockSpec((1, 1, SORT_GROUPS), lambda i, ng: (jnp.minimum(i + 1, n_steps - 1), 0, 0),
                         memory_space=pltpu.SMEM),
            pl.BlockSpec((tm, TOP_K), row), pl.BlockSpec((tm, TOP_K), row), pl.BlockSpec((tm, d), row),
            pl.BlockSpec(memory_space=pl.ANY),
            pl.BlockSpec((1, d), const), pl.BlockSpec((1, d), const),
        ],
        out_specs=pl.BlockSpec((tm, d), row),
        scratch_shapes=[pltpu.VMEM((2, SORT_ROWS, d // 2), I32), pltpu.SemaphoreType.DMA((2,))],
    )
    return pl.pallas_call(
        _combine_kernel,
        grid_spec=grid_spec,
        out_shape=jax.ShapeDtypeStruct((t, d), F32),
        compiler_params=pltpu.CompilerParams(dimension_semantics=("arbitrary",), vmem_limit_bytes=VMEM_LIMIT),
        name="moe_combine_ln2",
    )(ngroups, dstg, dstg, pos_c, gates, h1, ys, g2, b2)


def kernel(x, meta_tokens, ln_in_g, ln_in_b, rel_bias, w_in, lambda_q1, lambda_k1, lambda_q2, lambda_k2,
           subln_g, a_re, a_im, log_step, b_re, b_im, c_re, c_im, d_skip, w_glu, b_glu, w_out, ln1_g, ln1_b,
           w_router, b_router, w_gate, b_gate, w_up, b_up, w_down, b_down, ln2_g, ln2_b):
    bsz, seq, dm = x.shape
    assert seq % 512 == 0 and w_in.shape[0] == DEPTH == 1
    layer = 0
    row2 = lambda v: v.astype(F32).reshape(1, -1)

    w_in_b = w_in[layer].astype(BF16)
    gi, bi = row2(ln_in_g), row2(ln_in_b)
    q, k, vt, u = _inproj(x, gi, bi, w_in_b, tm=512, kv_tile=KV_TILE)
    meta = jnp.zeros((1, META_PAD, dm), x.dtype).at[0, :N_META].set(meta_tokens.astype(x.dtype))
    _, k_meta, vt_meta, u_meta = _inproj(meta, gi, bi, w_in_b, tm=META_PAD, kv_tile=META_PAD)

    lam_init = 0.8 - 0.6 * math.exp(-0.3 * layer)
    lam = (jnp.exp(jnp.sum(lambda_q1[layer].astype(F32) * lambda_k1[layer].astype(F32)))
           - jnp.exp(jnp.sum(lambda_q2[layer].astype(F32) * lambda_k2[layer].astype(F32))) + lam_init)
    att = _attention(lam.reshape(1), q, k, vt, k_meta[0], vt_meta[0, :, 0], _near_bias(rel_bias),
                     subln_g[layer].astype(F32).reshape(HEAD_W, 1), lam_init=lam_init)

    kpad, pt, qt, a1, a2 = _ssm_tables(a_re[layer], a_im[layer], log_step[layer], b_re[layer], b_im[layer],
                                     c_re[layer], c_im[layer], d_skip[layer])
    n_chunks = seq // SSM_T
    assert (n_chunks * bsz) % 16 == 0
    ug = jnp.transpose(u.reshape(bsz, n_chunks, SSM_T, SSM_G, SSM_CG), (3, 1, 0, 2, 4))
    ug = ug.reshape(SSM_G, n_chunks * bsz, SSM_T * SSM_CG)
    lead = jnp.zeros((8, N_META, SSM_W), BF16).at[0].set(u_meta[0, :N_META])
    ug_meta = jnp.transpose(lead.reshape(8, N_META, SSM_G, SSM_CG), (2, 0, 1, 3)).reshape(SSM_G, 8, N_META * SSM_CG)
    yg = _ssm(ug, ug_meta, kpad, pt, qt, a1, a2, n_chunks=n_chunks, bsz=bsz)
    y_ssm = jnp.transpose(yg.reshape(SSM_G, n_chunks, bsz, SSM_T, SSM_CG), (2, 1, 3, 0, 4))
    y_ssm = y_ssm.reshape(bsz, seq, SSM_W)

    t = bsz * seq
    h1, idx_t, gate_t, rank_t, cnt = _mix(
        x.reshape(t, dm), gi, bi, att.reshape(t, ATT_W), y_ssm.reshape(t, SSM_W),
        w_glu[layer].astype(BF16), row2(b_glu[layer]), w_out[layer].astype(BF16),
        row2(ln1_g[layer]), row2(ln1_b[layer]),
        jnp.transpose(w_router[layer].astype(F32)), b_router[layer].astype(F32).reshape(N_EXPERTS, 1))

    n_tiles = t // ROW_TILE
    experts = jnp.arange(N_EXPERTS, dtype=I32)
    tiles = jnp.arange(n_tiles, dtype=I32)
    tile_cnt = jnp.max(cnt, axis=-1).astype(I32)
    run = (tile_cnt + GROUP - 1) // GROUP * GROUP
    run_off = jnp.sum(jnp.where((experts[:, None] < experts[None, :])[None], run[:, :, None], 0), axis=1)
    run_before = jnp.sum(jnp.where((tiles[:, None] < tiles[None, :])[:, :, None], run[:, None, :], 0), axis=0)
    counts = jnp.sum(run, axis=0)
    padded = (counts + MOE_BLOCK - 1) // MOE_BLOCK * MOE_BLOCK
    padded_end = jnp.sum(jnp.where(experts[:, None] <= experts[None, :], padded[:, None], 0), axis=0)
    padded_start = padded_end - padded
    n_blocks = (t * TOP_K + n_tiles * N_EXPERTS * (GROUP - 1)) // MOE_BLOCK + N_EXPERTS
    block_row0 = jnp.arange(n_blocks, dtype=I32) * MOE_BLOCK
    block_expert = jnp.minimum(jnp.sum((padded_end[None, :] <= block_row0[:, None]).astype(I32), axis=1),
                               N_EXPERTS - 1)
    n_used = (padded_end[-1:] // MOE_BLOCK).astype(I32)
    ngroups = jnp.sum(run, axis=1) // GROUP
    local_row = jnp.arange(SORT_GROUPS, dtype=I32) * GROUP
    owner = jnp.minimum(jnp.sum(((run_off + run)[:, None, :] <= local_row[None, :, None]).astype(I32), axis=-1),
                        N_EXPERTS - 1)
    shift = (padded_start[None, :] + run_before - run_off) // GROUP
    dstg = jnp.sum(jnp.where(owner[..., None] == experts, shift[:, None, :], 0), axis=-1) + local_row // GROUP
    dstg = jnp.where(local_row[None, :] // GROUP < ngroups[:, None], dstg, 0).reshape(n_tiles, 1, SORT_GROUPS)
    run_off_tok = jnp.repeat(run_off, ROW_TILE, axis=0)
    pos_t = jnp.sum(jnp.where(idx_t[..., None] == experts, run_off_tok[None], 0), axis=-1) + rank_t

    xs = _dispatch(ngroups, ((padded_start + counts) // GROUP).astype(I32),
                   ((padded - counts) // GROUP).astype(I32), n_used, dstg, pos_t, h1, n_blocks * MOE_BLOCK)
    b3 = lambda v: v.astype(F32)[:, None, :]
    ys = _experts(block_expert, n_used, xs, w_gate[layer], b3(b_gate[layer]), w_up[layer], b3(b_up[layer]),
                  w_down[layer], b3(b_down[layer]))
    out = _combine(ngroups, dstg, jnp.transpose(pos_t), jnp.transpose(gate_t), h1, ys,
                   row2(ln2_g[layer]), row2(ln2_b[layer]))
    return out.reshape(bsz, seq, dm)
```

```python
import functools
import math

import jax
import jax.numpy as jnp
from jax import lax
from jax.experimental import pallas as pl
from jax.experimental.pallas import tpu as pltpu

F32 = jnp.float32
BF16 = jnp.bfloat16
I32 = jnp.int32

DEPTH = 1
N_META = 16
CHUNK = 64
ATT_HEADS = 4
HEAD_DIM = 64
HEAD_W = 2 * HEAD_DIM
ATT_W = ATT_HEADS * HEAD_W
SSM_W = 512
SSM_CG = 16
SSM_G = SSM_W // SSM_CG
SSM_N = 64
N_BUCKETS = 32
MAX_DISTANCE = 128
N_EXPERTS = 32
TOP_K = 4
SWIGLU_LIMIT = 7.0
SWIGLU_ALPHA = 1.702
LN_EPS = 1e-5
NEG_INF = -1e30
DEEPNORM_ALPHA = (2.0 * DEPTH) ** 0.25
LOG2E = 1.4426950408889634

Q_TILE = 512
KV_TILE = 512
V_ONES = 16
V_ROWS = HEAD_W + V_ONES
META_PAD = 128
SSM_T = 64
MIX_TILE = 512
ROW_TILE = 256
MOE_BLOCK = 512
GROUP = 8
COPY_BATCH = 32
SORT_ROWS = 1280
SORT_GROUPS = SORT_ROWS // GROUP
assert SORT_ROWS >= ROW_TILE * TOP_K + N_EXPERTS * (GROUP - 1)
V7X_VMEM_BYTES = 64 * 1024 * 1024
VMEM_LIMIT = V7X_VMEM_BYTES * 7 // 8


def _layer_norm(x, g, b):
    mu = jnp.mean(x, axis=-1, keepdims=True)
    xc = x - mu
    var = jnp.mean(xc * xc, axis=-1, keepdims=True)
    return xc * lax.rsqrt(var + LN_EPS) * g + b


def _dot(a, b):
    return jnp.dot(a, b, preferred_element_type=F32)


def _dot_nt(a, b, precision=None):
    return lax.dot_general(a, b, (((1,), (1,)), ((), ())), precision=precision,
                           preferred_element_type=F32)


def _inproj_kernel(x_ref, g_ref, b_ref, w_ref, q_ref, k_ref, vt_ref, u_ref, *, kv_tile, q_scale):
    h = _layer_norm(x_ref[0], g_ref[...], b_ref[...]).astype(BF16)
    tm = h.shape[0]
    q_ref[0] = (_dot(h, w_ref[:, 0:ATT_W]) * q_scale).astype(BF16)
    k_ref[0] = _dot(h, w_ref[:, ATT_W:2 * ATT_W]).astype(BF16)
    v = _dot(h, w_ref[:, 2 * ATT_W:3 * ATT_W])
    ones = jnp.ones((V_ONES, kv_tile), BF16)
    for hh in range(ATT_HEADS):
        vt = v[:, hh * HEAD_W:(hh + 1) * HEAD_W].T.astype(BF16)
        for j in range(tm // kv_tile):
            vt_ref[0, hh, j, 0:HEAD_W, :] = vt[:, j * kv_tile:(j + 1) * kv_tile]
            vt_ref[0, hh, j, HEAD_W:V_ROWS, :] = ones
    u_ref[0] = _dot(h, w_ref[:, 3 * ATT_W:]).astype(BF16)


def _inproj(x, g, b, w, *, tm, kv_tile):
    bsz, s, d = x.shape
    q_scale = HEAD_DIM ** -0.5 * LOG2E
    n_cols = w.shape[1]
    row = lambda bi, i: (bi, i, 0)
    return pl.pallas_call(
        functools.partial(_inproj_kernel, kv_tile=kv_tile, q_scale=q_scale),
        grid=(bsz, s // tm),
        in_specs=[
            pl.BlockSpec((1, tm, d), row),
            pl.BlockSpec((1, d), lambda bi, i: (0, 0)),
            pl.BlockSpec((1, d), lambda bi, i: (0, 0)),
            pl.BlockSpec((d, n_cols), lambda bi, i: (0, 0)),
        ],
        out_specs=[
            pl.BlockSpec((1, tm, ATT_W), row),
            pl.BlockSpec((1, tm, ATT_W), row),
            pl.BlockSpec((1, ATT_HEADS, tm // kv_tile, V_ROWS, kv_tile), lambda bi, i: (bi, 0, i, 0, 0)),
            pl.BlockSpec((1, tm, SSM_W), row),
        ],
        out_shape=[
            jax.ShapeDtypeStruct((bsz, s, ATT_W), BF16),
            jax.ShapeDtypeStruct((bsz, s, ATT_W), BF16),
            jax.ShapeDtypeStruct((bsz, ATT_HEADS, s // kv_tile, V_ROWS, kv_tile), BF16),
            jax.ShapeDtypeStruct((bsz, s, SSM_W), BF16),
        ],
        compiler_params=pltpu.CompilerParams(
            dimension_semantics=("parallel", "parallel"), vmem_limit_bytes=VMEM_LIMIT),
        name="inproj",
    )(x, g, b, w)


def _attn_kernel(lam_ref, q_ref, k_ref, vt_ref, km_ref, vtm_ref, bias_ref, g_ref, o_ref,
                 m_ref, acc_ref, qz_ref, sa_ref, sb_ref, sn_ref, *, out_scale):
    i = pl.program_id(2)
    qt = q_ref[0].astype(F32).T
    feat = lax.broadcasted_iota(I32, qt.shape, 0)
    qz_ref[:, 0:Q_TILE] = jnp.where(feat < HEAD_DIM, qt, 0.0).astype(BF16)
    qz_ref[:, Q_TILE:] = jnp.where(feat >= HEAD_DIM, qt, 0.0).astype(BF16)

    m_ref[...] = jnp.full(m_ref.shape, NEG_INF, F32)
    acc_ref[...] = jnp.zeros(acc_ref.shape, F32)

    def scores(kt):
        return _dot(kt, qz_ref[...])

    def k_tile(t):
        return k_ref[0, pl.ds(pl.multiple_of(t * KV_TILE, KV_TILE), KV_TILE), :]

    def absorb(s, pv):
        m_old = m_ref[...]
        m_new = jnp.maximum(m_old, jnp.max(s, axis=0, keepdims=True))
        alpha = jnp.exp2(m_old - m_new)
        p = jnp.exp2(s - m_new).astype(BF16)
        acc_ref[...] = acc_ref[...] * alpha + pv(p)
        m_ref[...] = m_new

    def tile_pv(t):
        return lambda p: _dot(vt_ref[0, 0, t], p)

    n_far = jnp.maximum(i - 1, 0)
    peel = n_far % 2

    sn_ref[...] = scores(km_ref[...])
    sa_ref[...] = scores(k_tile(0))
    absorb(sn_ref[...] + bias_ref[0, 0, 2 * KV_TILE:, :], lambda p: _dot(vtm_ref[0], p))

    @pl.when(peel == 1)
    def _():
        sb_ref[...] = scores(k_tile(1))
        absorb(sa_ref[...], tile_pv(0))
        sa_ref[...] = sb_ref[...]

    def far_pair(j, carry):
        t0 = peel + 2 * j
        s_cur = sa_ref[...]
        sb_ref[...] = scores(k_tile(t0 + 1))
        absorb(s_cur, tile_pv(t0))
        s_cur = sb_ref[...]
        sa_ref[...] = scores(k_tile(t0 + 2))
        absorb(s_cur, tile_pv(t0 + 1))
        return carry

    lax.fori_loop(0, n_far // 2, far_pair, 0)

    sb_ref[...] = scores(k_tile(i))
    absorb(sa_ref[...] + bias_ref[0, 0, 0:KV_TILE, :], tile_pv(n_far))
    absorb(sb_ref[...] + bias_ref[0, 0, KV_TILE:2 * KV_TILE, :], tile_pv(i))

    acc = acc_ref[...]
    lam = lam_ref[0]
    o1 = acc[0:HEAD_W, 0:Q_TILE] / acc[HEAD_W:HEAD_W + 1, 0:Q_TILE]
    o2 = acc[0:HEAD_W, Q_TILE:] / acc[HEAD_W:HEAD_W + 1, Q_TILE:]
    o = o1 - lam * o2
    ms = jnp.mean(o * o, axis=0, keepdims=True)
    o = o * lax.rsqrt(ms + LN_EPS) * g_ref[...] * out_scale
    o_ref[0] = o.T.astype(o_ref.dtype)


def _attention(lam, q, k, vt, k_meta, vt_meta, bias, subln_g, *, lam_init):
    bsz, s, _ = q.shape
    nq = s // Q_TILE
    n_near = META_PAD + 2 * KV_TILE
    grid_spec = pltpu.PrefetchScalarGridSpec(
        num_scalar_prefetch=1,
        grid=(bsz, ATT_HEADS, nq),
        in_specs=[
            pl.BlockSpec((1, Q_TILE, HEAD_W), lambda b, h, i, lam: (b, i, h)),
            pl.BlockSpec((1, s, HEAD_W), lambda b, h, i, lam: (b, 0, h)),
            pl.BlockSpec((1, 1, s // KV_TILE, V_ROWS, KV_TILE), lambda b, h, i, lam: (b, h, 0, 0, 0)),
            pl.BlockSpec((META_PAD, HEAD_W), lambda b, h, i, lam: (0, h)),
            pl.BlockSpec((1, V_ROWS, META_PAD), lambda b, h, i, lam: (h, 0, 0)),
            pl.BlockSpec((1, 1, n_near, 2 * Q_TILE), lambda b, h, i, lam: (h, jnp.minimum(i, 1), 0, 0)),
            pl.BlockSpec((HEAD_W, 1), lambda b, h, i, lam: (0, 0)),
        ],
        out_specs=pl.BlockSpec((1, Q_TILE, HEAD_W), lambda b, h, i, lam: (b, i, h)),
        scratch_shapes=[
            pltpu.VMEM((1, 2 * Q_TILE), F32), pltpu.VMEM((V_ROWS, 2 * Q_TILE), F32),
            pltpu.VMEM((HEAD_W, 2 * Q_TILE), BF16),
            pltpu.VMEM((KV_TILE, 2 * Q_TILE), F32), pltpu.VMEM((KV_TILE, 2 * Q_TILE), F32),
            pltpu.VMEM((META_PAD, 2 * Q_TILE), F32),
        ],
    )
    return pl.pallas_call(
        functools.partial(_attn_kernel, out_scale=1.0 - lam_init),
        grid_spec=grid_spec,
        out_shape=jax.ShapeDtypeStruct((bsz, s, ATT_W), BF16),
        compiler_params=pltpu.CompilerParams(
            dimension_semantics=("parallel", "parallel", "arbitrary"), vmem_limit_bytes=VMEM_LIMIT),
        name="diff_attention",
    )(lam, q, k, vt, k_meta, vt_meta, bias, subln_g)


def _t5_bucket(rel):
    nb = N_BUCKETS // 2
    max_exact = nb // 2
    ret = jnp.where(rel > 0, nb, 0)
    n = jnp.abs(rel)
    n_f = jnp.maximum(n, 1).astype(F32)
    large = max_exact + (jnp.log(n_f / max_exact) / math.log(MAX_DISTANCE / max_exact)
                         * (nb - max_exact)).astype(I32)
    large = jnp.minimum(large, nb - 1)
    return ret + jnp.where(n < max_exact, n, large)


def _near_bias(rel_bias):
    table = rel_bias.astype(F32)
    far = table[N_BUCKETS // 2 - 1]
    c = jnp.arange(Q_TILE, dtype=I32)[None, :]
    r = jnp.arange(KV_TILE, dtype=I32)[:, None]

    shifted = jnp.transpose(table - far) * LOG2E

    def bias_of(offset, rows, cols=Q_TILE):
        rel = jnp.arange(rows, dtype=I32)[:, None] - c[:, :cols] + offset
        hot = _t5_bucket(rel)[None, :, :, None] == jnp.arange(N_BUCKETS, dtype=I32)
        return jnp.sum(jnp.where(hot, shifted[:, None, None, :], 0.0), axis=-1)

    nd = KV_TILE // MAX_DISTANCE
    own_bias = sum(
        (jnp.eye(nd, k=-d, dtype=F32)[None, :, None, :, None]
         * bias_of(d * MAX_DISTANCE, MAX_DISTANCE, MAX_DISTANCE)[:, None, :, None, :]).reshape(
             ATT_HEADS, KV_TILE, Q_TILE)
        for d in (-1, 0, 1))
    own = jnp.where((r // CHUNK <= c // CHUNK)[None], own_bias, NEG_INF)
    near = MAX_DISTANCE
    prev = jnp.zeros((ATT_HEADS, KV_TILE, Q_TILE), F32).at[:, KV_TILE - near:, :near].set(bias_of(-near, near, near))
    rm = jnp.arange(META_PAD, dtype=I32)[:, None]
    meta_ok = (rm < N_META)[None]
    meta0 = jnp.where(meta_ok, jnp.pad(bias_of(-N_META, N_META), ((0, 0), (0, META_PAD - N_META), (0, 0))), NEG_INF)
    meta1 = jnp.where(meta_ok, jnp.zeros((ATT_HEADS, META_PAD, Q_TILE), F32), NEG_INF)
    assert N_META + Q_TILE - (N_META - 1) >= MAX_DISTANCE and KV_TILE >= near and Q_TILE >= near
    v0 = jnp.concatenate([jnp.full_like(prev, NEG_INF), own, meta0], axis=1)
    v1 = jnp.concatenate([prev, own, meta1], axis=1)
    both = jnp.stack([v0, v1], axis=1)
    return jnp.concatenate([both, both], axis=-1)


def _ssm_tables(a_re, a_im, log_step, b_re, b_im, c_re, c_im, d_skip):
    hi = lax.Precision.HIGHEST
    t_len = SSM_T
    step = jnp.exp(log_step.astype(F32))[:, None]
    ar = jnp.minimum(a_re.astype(F32), -1e-4)
    ai = a_im.astype(F32)
    mag = jnp.exp(step * ar)
    ph = step * ai
    abar_re = mag * jnp.cos(ph)
    abar_im = mag * jnp.sin(ph)
    den = ar * ar + ai * ai
    e_re = abar_re - 1.0
    e_im = abar_im
    f_re = (e_re * ar + e_im * ai) / den
    f_im = (e_im * ar - e_re * ai) / den
    br = b_re.astype(F32)
    bi = b_im.astype(F32)
    bb_re = f_re[..., None] * br - f_im[..., None] * bi
    bb_im = f_re[..., None] * bi + f_im[..., None] * br
    tau = jnp.arange(t_len + 1, dtype=F32)[None, :, None]
    pmag = jnp.exp(tau * (step * ar)[:, None, :])
    pph = tau * ph[:, None, :]
    pw_re = pmag * jnp.cos(pph)
    pw_im = pmag * jnp.sin(pph)
    bbt_re = jnp.swapaxes(bb_re, 1, 2)[:, None]
    bbt_im = jnp.swapaxes(bb_im, 1, 2)[:, None]

    def times_bbar(p_re, p_im):
        return (p_re[:, :, None, :] * bbt_re - p_im[:, :, None, :] * bbt_im,
                p_re[:, :, None, :] * bbt_im + p_im[:, :, None, :] * bbt_re)

    cr = c_re.astype(F32)
    ci = c_im.astype(F32)
    bi_re = jnp.swapaxes(bb_re, 1, 2)[:, :, None, :]
    bi_im = jnp.swapaxes(bb_im, 1, 2)[:, :, None, :]
    cb_re = cr[:, None] * bi_re - ci[:, None] * bi_im
    cb_im = cr[:, None] * bi_im + ci[:, None] * bi_re
    kern = jnp.einsum('gikm,gtm->gitk', jnp.concatenate([cb_re, -cb_im], axis=-1),
                      jnp.concatenate([pw_re[:, :t_len], pw_im[:, :t_len]], axis=-1), precision=hi)
    skip = d_skip.astype(F32).reshape(SSM_G, SSM_CG)
    kern = kern.at[:, :, 0, :].add(skip[:, None, :] * jnp.eye(SSM_CG, dtype=F32))
    kflat = kern.reshape(SSM_G, SSM_CG, t_len * SSM_CG)
    kpad = jnp.concatenate([jnp.zeros_like(kflat), kflat], axis=-1)
    back = jnp.arange(t_len - 1, -1, -1, dtype=F32)[None, :, None]
    bmag = jnp.exp(back * (step * ar)[:, None, :])
    bph = back * ph[:, None, :]
    pt_re, pt_im = times_bbar(bmag * jnp.cos(bph), bmag * jnp.sin(bph))
    pt = jnp.concatenate([pt_re, pt_im], axis=-1).reshape(SSM_G, t_len * SSM_CG, 2 * SSM_N)
    up_re = jnp.swapaxes(pw_re[:, 1:], 1, 2)[..., None]
    up_im = jnp.swapaxes(pw_im[:, 1:], 1, 2)[..., None]
    crt = jnp.swapaxes(cr, 1, 2)[:, :, None, :]
    cit = jnp.swapaxes(ci, 1, 2)[:, :, None, :]
    q_re = crt * up_re - cit * up_im
    q_im = crt * up_im + cit * up_re
    qt = jnp.concatenate([q_re, -q_im], axis=1).reshape(SSM_G, 2 * SSM_N, t_len * SSM_CG)
    at_re = pw_re[:, t_len]
    at_im = pw_im[:, t_len]
    a1 = jnp.concatenate([at_re, at_re], axis=-1)
    a2 = jnp.concatenate([-at_im, at_im], axis=-1)
    return kpad, pt.astype(BF16), qt.astype(BF16), a1, a2


def _ssm_state_kernel(u_ref, pt_ref, s_ref):
    s_ref[0] = _dot(u_ref[0], pt_ref[0])


def _ssm_scan_kernel(s_ref, x0_ref, a1_ref, a2_ref, x_ref, *, n_chunks, bsz):
    a1 = a1_ref[...][:, None, :]
    a2 = a2_ref[...][:, None, :]

    def body(c, x):
        rows = pl.ds(c * bsz, bsz)
        x_ref[:, rows, :] = x
        return a1 * x + a2 * pltpu.roll(x, SSM_N, 2) + s_ref[:, rows, :]

    x0 = jnp.broadcast_to(x0_ref[:, 0:1, :], (SSM_G, bsz, 2 * SSM_N))
    lax.fori_loop(0, n_chunks, body, x0)


def _ssm_out_kernel(u_ref, x_ref, kp_ref, qt_ref, y_ref, mt_ref):
    kp = kp_ref[0]
    width = SSM_T * SSM_CG
    lane_tile = 128
    for sub in range(0, lane_tile, SSM_CG):
        shifted = kp if sub == 0 else pltpu.roll(kp, 2 * width - sub, 1)
        for s in range(SSM_T):
            off = (SSM_T - s) * SSM_CG
            if off % lane_tile == sub:
                base = off - sub
                mt_ref[s * SSM_CG:(s + 1) * SSM_CG, :] = shifted[:, base:base + width].astype(BF16)
    x = x_ref[0]
    x_hi = x.astype(BF16)
    x_lo = (x - x_hi.astype(F32)).astype(BF16)
    y = _dot(u_ref[0], mt_ref[...]) + _dot(x_hi, qt_ref[0]) + _dot(x_lo, qt_ref[0])
    y_ref[0] = y.astype(y_ref.dtype)


def _ssm_chunk_state(ug, pt):
    g, r, w = ug.shape
    n2 = 2 * SSM_N
    per_g = lambda gi: (gi, 0, 0)
    return pl.pallas_call(
        _ssm_state_kernel,
        grid=(g,),
        in_specs=[pl.BlockSpec((1, r, w), per_g), pl.BlockSpec((1, w, n2), per_g)],
        out_specs=pl.BlockSpec((1, r, n2), per_g),
        out_shape=jax.ShapeDtypeStruct((g, r, n2), F32),
        compiler_params=pltpu.CompilerParams(dimension_semantics=("parallel",), vmem_limit_bytes=VMEM_LIMIT),
        name="ssm_chunk_state",
    )(ug, pt)


def _ssm(ug, ug_meta, kpad, pt, qt, a1, a2, *, n_chunks, bsz):
    g, r, w = ug.shape
    n2 = 2 * SSM_N
    per_g = lambda gi: (gi, 0, 0)
    params = pltpu.CompilerParams(dimension_semantics=("parallel",), vmem_limit_bytes=VMEM_LIMIT)
    s = _ssm_chunk_state(ug, pt)
    x0 = _ssm_chunk_state(ug_meta, pt[:, w - ug_meta.shape[2]:, :])
    x = pl.pallas_call(
        functools.partial(_ssm_scan_kernel, n_chunks=n_chunks, bsz=bsz),
        out_shape=jax.ShapeDtypeStruct((g, r, n2), F32),
        compiler_params=pltpu.CompilerParams(vmem_limit_bytes=VMEM_LIMIT),
        name="ssm_chunk_scan",
    )(s, x0, a1, a2)
    return pl.pallas_call(
        _ssm_out_kernel,
        grid=(g,),
        in_specs=[pl.BlockSpec((1, r, w), per_g), pl.BlockSpec((1, r, n2), per_g),
                  pl.BlockSpec((1, SSM_CG, 2 * w), per_g), pl.BlockSpec((1, n2, w), per_g)],
        out_specs=pl.BlockSpec((1, r, w), per_g),
        out_shape=jax.ShapeDtypeStruct((g, r, w), BF16),
        scratch_shapes=[pltpu.VMEM((w, w), BF16)],
        compiler_params=params,
        name="ssm_output",
    )(ug, x, kpad, qt)


def _mix_kernel(x_ref, gi_ref, bi_ref, att_ref, y_ref, wglu_ref, bglu_ref, wout_ref, g1_ref, b1_ref,
                wr_ref, br_ref, h1_ref, idx_ref, gate_ref, rank_ref, cnt_ref):
    h0 = _layer_norm(x_ref[...], gi_ref[...], bi_ref[...])
    y = y_ref[...].astype(F32)
    y = y * (0.5 * (1.0 + jnp.tanh(math.sqrt(2.0 / math.pi) * (y + 0.044715 * (y * y * y)))))
    y = y * jax.nn.sigmoid(_dot(y.astype(BF16), wglu_ref[...]) + bglu_ref[...])
    mix = _dot(att_ref[...], wout_ref[0:ATT_W, :]) + _dot(y.astype(BF16), wout_ref[ATT_W:, :])
    h1 = _layer_norm(DEEPNORM_ALPHA * h0 + mix, g1_ref[...], b1_ref[...])
    h1_ref[...] = h1

    logits = _dot_nt(wr_ref[...], h1, precision=lax.Precision.HIGHEST) + br_ref[...]
    tm = logits.shape[1]
    eidx = lax.broadcasted_iota(I32, logits.shape, 0)
    vals, hots = [], []
    rest = logits
    for _ in range(TOP_K):
        mx = jnp.max(rest, axis=0, keepdims=True)
        first = jnp.min(jnp.where(rest == mx, eidx, N_EXPERTS), axis=0, keepdims=True)
        hot = eidx == first
        vals.append(mx)
        hots.append(hot)
        rest = jnp.where(hot, -jnp.inf, rest)
    exps = [jnp.exp(v - vals[0]) for v in vals]
    denom = exps[0] + exps[1] + exps[2] + exps[3]
    gate_ref[...] = jnp.concatenate([e / denom for e in exps], axis=0)
    idx_ref[...] = jnp.concatenate(
        [jnp.sum(jnp.where(h, eidx, 0), axis=0, keepdims=True) for h in hots], axis=0)

    hot_all = (hots[0] | hots[1] | hots[2] | hots[3]).astype(F32)
    sub = ROW_TILE
    tri = (lax.broadcasted_iota(I32, (sub, sub), 0) < lax.broadcasted_iota(I32, (sub, sub), 1)).astype(BF16)
    for part in range(tm // sub):
        cols = slice(part * sub, (part + 1) * sub)
        before = _dot(hot_all[:, cols].astype(BF16), tri)
        rank_ref[:, cols] = jnp.concatenate(
            [jnp.sum(jnp.where(h[:, cols], before, 0.0), axis=0, keepdims=True) for h in hots], axis=0).astype(I32)
        cnt_ref[part] = jnp.broadcast_to(jnp.sum(hot_all[:, cols], axis=1, keepdims=True), cnt_ref.shape[1:])


def _mix(x2, gi, bi, att, y, wglu, bglu, wout, g1, b1, wr_t, br):
    t, d = x2.shape
    tm = MIX_TILE
    row = lambda i: (i, 0)
    col = lambda i: (0, i)
    const = lambda i: (0, 0)
    return pl.pallas_call(
        _mix_kernel,
        grid=(t // tm,),
        in_specs=[
            pl.BlockSpec((tm, d), row), pl.BlockSpec((1, d), const), pl.BlockSpec((1, d), const),
            pl.BlockSpec((tm, ATT_W), row), pl.BlockSpec((tm, SSM_W), row),
            pl.BlockSpec((SSM_W, SSM_W), const), pl.BlockSpec((1, SSM_W), const),
            pl.BlockSpec((d, d), const), pl.BlockSpec((1, d), const), pl.BlockSpec((1, d), const),
            pl.BlockSpec((N_EXPERTS, d), const), pl.BlockSpec((N_EXPERTS, 1), const),
        ],
        out_specs=[
            pl.BlockSpec((tm, d), row),
            pl.BlockSpec((TOP_K, tm), col), pl.BlockSpec((TOP_K, tm), col), pl.BlockSpec((TOP_K, tm), col),
            pl.BlockSpec((tm // ROW_TILE, N_EXPERTS, 128), lambda i: (i, 0, 0)),
        ],
        out_shape=[
            jax.ShapeDtypeStruct((t, d), F32),
            jax.ShapeDtypeStruct((TOP_K, t), I32), jax.ShapeDtypeStruct((TOP_K, t), F32),
            jax.ShapeDtypeStruct((TOP_K, t), I32),
            jax.ShapeDtypeStruct((t // ROW_TILE, N_EXPERTS, 128), F32),
        ],
        compiler_params=pltpu.CompilerParams(dimension_semantics=("parallel",), vmem_limit_bytes=VMEM_LIMIT),
        name="mix_ln1_router",
    )(x2, gi, bi, att, y, wglu, bglu, wout, g1, b1, wr_t, br)


_HI_BITS = -65536


def _pack_halves(x):
    half = x.shape[1] // 2
    lo = lax.bitcast_convert_type(x[:, :half], I32)
    hi = lax.bitcast_convert_type(x[:, half:], I32)
    return lax.shift_right_logical(lo, 16) | (hi & _HI_BITS)


def _unpack_halves(w):
    lo = lax.bitcast_convert_type(lax.shift_left(w, 16), F32)
    hi = lax.bitcast_convert_type(w & _HI_BITS, F32)
    return lo.astype(BF16), hi.astype(BF16)


def _group_copy(src, src_group, dst, dst_group, sem):
    return pltpu.make_async_copy(src.at[pl.ds(pl.multiple_of(src_group * GROUP, GROUP), GROUP), :],
                                 dst.at[pl.ds(pl.multiple_of(dst_group * GROUP, GROUP), GROUP), :], sem)


def _batched(count, fn):
    def body(q, c):
        fn(q * COPY_BATCH, COPY_BATCH)
        return c

    whole = count // COPY_BATCH
    lax.fori_loop(0, whole, body, 0)
    rest = count - whole * COPY_BATCH
    n, done = COPY_BATCH // 2, whole * COPY_BATCH
    while n >= 1:
        take = (rest // n) % 2

        @pl.when(take == 1)
        def _(n=n, done=done):
            fn(done, n)

        done = done + take * n
        n //= 2


def _dispatch_kernel(ngroups_ref, pad_start_ref, pad_count_ref, n_used_ref, dstg_ref, pos_ref, h1_ref, xs_hbm,
                     sorted_ref, zero_ref, sems, zsem):
    step = pl.program_id(0)
    n_steps = pl.num_programs(0)
    slot = step % 2
    n_blocks = xs_hbm.shape[0] // MOE_BLOCK

    def zero_block(blk):
        return pltpu.make_async_copy(zero_ref, xs_hbm.at[pl.ds(blk * MOE_BLOCK, MOE_BLOCK), :], zsem)

    @pl.when(step == 0)
    def _():
        zero_ref[...] = jnp.zeros(zero_ref.shape, I32)
        for e in range(N_EXPERTS):
            start = pad_start_ref[e]
            count = pad_count_ref[e]

            def fill(j, c):
                _group_copy(zero_ref, 0, xs_hbm, start + j, zsem).start()
                return c

            def drain(j, c):
                _group_copy(zero_ref, 0, xs_hbm, start, zsem).wait()
                return c

            lax.fori_loop(0, count, fill, 0)
            lax.fori_loop(0, count, drain, 0)

        def fill_block(blk, c):
            zero_block(blk).start()
            return c

        def drain_block(blk, c):
            zero_block(blk).wait()
            return c

        lax.fori_loop(n_used_ref[0], n_blocks, fill_block, 0)
        lax.fori_loop(n_used_ref[0], n_blocks, drain_block, 0)

    pos = pos_ref[...]
    rows = lax.broadcasted_iota(I32, (SORT_ROWS, pos.shape[1]), 0)
    place = rows == pos[0:1]
    for k in range(1, TOP_K):
        place = place | (rows == pos[k:k + 1])
    sorted_ref[slot] = _pack_halves(_dot(place.astype(BF16), h1_ref[...].astype(BF16)))

    _batched(ngroups_ref[step],
             lambda j, n: [_group_copy(sorted_ref.at[slot], j + r, xs_hbm, dstg_ref[0, 0, j + r],
                                       sems.at[slot]).start() for r in range(n)])

    def wait_tile(s, count):
        _batched(count, lambda j, n: pltpu.make_async_copy(
            sorted_ref.at[s, pl.ds(0, n * GROUP), :], xs_hbm.at[pl.ds(0, n * GROUP), :], sems.at[s]).wait())

    @pl.when(step > 0)
    def _():
        wait_tile(1 - slot, ngroups_ref[jnp.maximum(step - 1, 0)])

    @pl.when(step == n_steps - 1)
    def _():
        wait_tile(slot, ngroups_ref[step])


def _dispatch(ngroups, pad_start, pad_count, n_used, dstg, pos_t, h1, n_rows):
    t, d = h1.shape
    tm = ROW_TILE
    grid_spec = pltpu.PrefetchScalarGridSpec(
        num_scalar_prefetch=4,
        grid=(t // tm,),
        in_specs=[
            pl.BlockSpec((1, 1, SORT_GROUPS), lambda i, *_: (i, 0, 0), memory_space=pltpu.SMEM),
            pl.BlockSpec((TOP_K, tm), lambda i, *_: (0, i)),
            pl.BlockSpec((tm, d), lambda i, *_: (i, 0)),
        ],
        out_specs=pl.BlockSpec(memory_space=pl.ANY),
        scratch_shapes=[pltpu.VMEM((2, SORT_ROWS, d // 2), I32), pltpu.VMEM((MOE_BLOCK, d // 2), I32),
                        pltpu.SemaphoreType.DMA((2,)), pltpu.SemaphoreType.DMA],
    )
    return pl.pallas_call(
        _dispatch_kernel,
        grid_spec=grid_spec,
        out_shape=jax.ShapeDtypeStruct((n_rows, d // 2), I32),
        compiler_params=pltpu.CompilerParams(dimension_semantics=("arbitrary",), vmem_limit_bytes=VMEM_LIMIT,
                                             has_side_effects=True),
        name="moe_dispatch",
    )(ngroups, pad_start, pad_count, n_used, dstg, pos_t, h1)


def _expert_kernel(be_ref, nb_ref, x_ref, wg_ref, bg_ref, wu_ref, bu_ref, wd_ref, bd_ref, y_ref,
                   wg_b, wu_b, wd_b):
    i = pl.program_id(0)

    @pl.when(i < nb_ref[0])
    def _():
        @pl.when((i == 0) | (be_ref[i] != be_ref[jnp.maximum(i - 1, 0)]))
        def _():
            wg_b[...] = wg_ref[0].astype(BF16)
            wu_b[...] = wu_ref[0].astype(BF16)
            wd_b[...] = wd_ref[0].astype(BF16)

        x_lo, x_hi = _unpack_halves(x_ref[...])
        half = x_lo.shape[1]
        gate = _dot(x_lo, wg_b[0:half, :]) + _dot(x_hi, wg_b[half:, :]) + bg_ref[0]
        up = _dot(x_lo, wu_b[0:half, :]) + _dot(x_hi, wu_b[half:, :]) + bu_ref[0]
        gate = jnp.minimum(gate, SWIGLU_LIMIT)
        up = jnp.clip(up, -SWIGLU_LIMIT, SWIGLU_LIMIT)
        act = (up + 1.0) * gate * jax.nn.sigmoid(gate * SWIGLU_ALPHA)
        y = _dot(act.astype(BF16), wd_b[...]) + bd_ref[0]
        y_ref[...] = _pack_halves(y.astype(BF16).astype(F32))

    @pl.when(i >= nb_ref[0])
    def _():
        y_ref[...] = jnp.zeros(y_ref.shape, I32)


def _experts(block_expert, n_used, xs, wg, bg, wu, bu, wd, bd):
    n_rows, packed = xs.shape
    d, dff = wg.shape[1], wg.shape[2]
    assert packed * 2 == d and wd.shape[2] == d
    nb = n_rows // MOE_BLOCK
    last = lambda nu: jnp.maximum(nu[0] - 1, 0)
    blk = lambda i, be, nu: (jnp.minimum(i, last(nu)), 0)
    wsel = lambda i, be, nu: (be[jnp.minimum(i, last(nu))], 0, 0)
    grid_spec = pltpu.PrefetchScalarGridSpec(
        num_scalar_prefetch=2,
        grid=(nb,),
        in_specs=[
            pl.BlockSpec((MOE_BLOCK, packed), blk),
            pl.BlockSpec((1, d, dff), wsel), pl.BlockSpec((1, 1, dff), wsel),
            pl.BlockSpec((1, d, dff), wsel), pl.BlockSpec((1, 1, dff), wsel),
            pl.BlockSpec((1, dff, d), wsel), pl.BlockSpec((1, 1, d), wsel),
        ],
        out_specs=pl.BlockSpec((MOE_BLOCK, packed), lambda i, be, nu: (i, 0)),
        scratch_shapes=[pltpu.VMEM((d, dff), BF16), pltpu.VMEM((d, dff), BF16), pltpu.VMEM((dff, d), BF16)],
    )
    return pl.pallas_call(
        _expert_kernel,
        grid_spec=grid_spec,
        out_shape=jax.ShapeDtypeStruct((n_rows, packed), I32),
        compiler_params=pltpu.CompilerParams(dimension_semantics=("arbitrary",), vmem_limit_bytes=VMEM_LIMIT),
        name="moe_experts",
    )(block_expert, n_used, xs, wg, bg, wu, bu, wd, bd)


def _combine_kernel(ngroups_ref, dstg_ref, dstg_next_ref, pos_ref, gate_ref, h1_ref, ys_hbm, g2_ref, b2_ref,
                    o_ref, buf, sems):
    step = pl.program_id(0)
    n_steps = pl.num_programs(0)
    slot = step % 2

    def fetch(dref, s, j):
        return _group_copy(ys_hbm, dref[0, 0, j], buf.at[s], j, sems.at[s])

    def issue(dref, s, count):
        _batched(count, lambda j, n: [fetch(dref, s, j + r).start() for r in range(n)])

    @pl.when(step == 0)
    def _():
        buf[...] = jnp.zeros(buf.shape, I32)
        issue(dstg_ref, 0, ngroups_ref[0])

    @pl.when(step + 1 < n_steps)
    def _():
        issue(dstg_next_ref, 1 - slot, ngroups_ref[jnp.minimum(step + 1, n_steps - 1)])

    _batched(ngroups_ref[step], lambda j, n: pltpu.make_async_copy(
        ys_hbm.at[pl.ds(0, n * GROUP), :], buf.at[slot, pl.ds(0, n * GROUP), :], sems.at[slot]).wait())

    pos = pos_ref[...]
    gate = gate_ref[...]
    lanes = lax.broadcasted_iota(I32, (pos.shape[0], SORT_ROWS), 1)
    sel = jnp.where(lanes == pos[:, 0:1], gate[:, 0:1], 0.0)
    for k in range(1, TOP_K):
        sel = sel + jnp.where(lanes == pos[:, k:k + 1], gate[:, k:k + 1], 0.0)
    sel_hi = sel.astype(BF16)
    sel_lo = (sel - sel_hi.astype(F32)).astype(BF16)
    y_lo, y_hi = _unpack_halves(buf[slot])
    ffn = jnp.concatenate([_dot(sel_hi, y_lo) + _dot(sel_lo, y_lo), _dot(sel_hi, y_hi) + _dot(sel_lo, y_hi)], axis=1)
    o_ref[...] = _layer_norm(DEEPNORM_ALPHA * h1_ref[...] + ffn, g2_ref[...], b2_ref[...])


def _combine(ngroups, dstg, pos_c, gates, h1, ys, g2, b2):
    t, d = h1.shape
    tm = ROW_TILE
    n_steps = t // tm
    row = lambda i, ng: (i, 0)
    const = lambda i, ng: (0, 0)
    grid_spec = pltpu.PrefetchScalarGridSpec(
        num_scalar_prefetch=1,
        grid=(n_steps,),
        in_specs=[
            pl.BlockSpec((1, 1, SORT_GROUPS), lambda i, ng: (i, 0, 0), memory_space=pltpu.SMEM),
            pl.BlockSpec((1, 1, SORT_GROUPS), lambda i, ng: (jnp.minimum(i + 1, n_steps - 1), 0, 0),
                         memory_space=pltpu.SMEM),
            pl.BlockSpec((tm, TOP_K), row), pl.BlockSpec((tm, TOP_K), row), pl.BlockSpec((tm, d), row),
            pl.BlockSpec(memory_space=pl.ANY),
            pl.BlockSpec((1, d), const), pl.BlockSpec((1, d), const),
        ],
        out_specs=pl.BlockSpec((tm, d), row),
        scratch_shapes=[pltpu.VMEM((2, SORT_ROWS, d // 2), I32), pltpu.SemaphoreType.DMA((2,))],
    )
    return pl.pallas_call(
        _combine_kernel,
        grid_spec=grid_spec,
        out_shape=jax.ShapeDtypeStruct((t, d), F32),
        compiler_params=pltpu.CompilerParams(dimension_semantics=("arbitrary",), vmem_limit_bytes=VMEM_LIMIT),
        name="moe_combine_ln2",
    )(ngroups, dstg, dstg, pos_c, gates, h1, ys, g2, b2)


def kernel(x, meta_tokens, ln_in_g, ln_in_b, rel_bias, w_in, lambda_q1, lambda_k1, lambda_q2, lambda_k2,
           subln_g, a_re, a_im, log_step, b_re, b_im, c_re, c_im, d_skip, w_glu, b_glu, w_out, ln1_g, ln1_b,
           w_router, b_router, w_gate, b_gate, w_up, b_up, w_down, b_down, ln2_g, ln2_b):
    bsz, seq, dm = x.shape
    assert seq % 512 == 0 and w_in.shape[0] == DEPTH == 1
    layer = 0
    row2 = lambda v: v.astype(F32).reshape(1, -1)

    w_in_b = w_in[layer].astype(BF16)
    gi, bi = row2(ln_in_g), row2(ln_in_b)
    q, k, vt, u = _inproj(x, gi, bi, w_in_b, tm=512, kv_tile=KV_TILE)
    meta = jnp.zeros((1, META_PAD, dm), x.dtype).at[0, :N_META].set(meta_tokens.astype(x.dtype))
    _, k_meta, vt_meta, u_meta = _inproj(meta, gi, bi, w_in_b, tm=META_PAD, kv_tile=META_PAD)

    lam_init = 0.8 - 0.6 * math.exp(-0.3 * layer)
    lam = (jnp.exp(jnp.sum(lambda_q1[layer].astype(F32) * lambda_k1[layer].astype(F32)))
           - jnp.exp(jnp.sum(lambda_q2[layer].astype(F32) * lambda_k2[layer].astype(F32))) + lam_init)
    att = _attention(lam.reshape(1), q, k, vt, k_meta[0], vt_meta[0, :, 0], _near_bias(rel_bias),
                     subln_g[layer].astype(F32).reshape(HEAD_W, 1), lam_init=lam_init)

    kpad, pt, qt, a1, a2 = _ssm_tables(a_re[layer], a_im[layer], log_step[layer], b_re[layer], b_im[layer],
                                     c_re[layer], c_im[layer], d_skip[layer])
    n_chunks = seq // SSM_T
    assert (n_chunks * bsz) % 16 == 0
    ug = jnp.transpose(u.reshape(bsz, n_chunks, SSM_T, SSM_G, SSM_CG), (3, 1, 0, 2, 4))
    ug = ug.reshape(SSM_G, n_chunks * bsz, SSM_T * SSM_CG)
    lead = jnp.zeros((8, N_META, SSM_W), BF16).at[0].set(u_meta[0, :N_META])
    ug_meta = jnp.transpose(lead.reshape(8, N_META, SSM_G, SSM_CG), (2, 0, 1, 3)).reshape(SSM_G, 8, N_META * SSM_CG)
    yg = _ssm(ug, ug_meta, kpad, pt, qt, a1, a2, n_chunks=n_chunks, bsz=bsz)
    y_ssm = jnp.transpose(yg.reshape(SSM_G, n_chunks, bsz, SSM_T, SSM_CG), (2, 1, 3, 0, 4))
    y_ssm = y_ssm.reshape(bsz, seq, SSM_W)

    t = bsz * seq
    h1, idx_t, gate_t, rank_t, cnt = _mix(
        x.reshape(t, dm), gi, bi, att.reshape(t, ATT_W), y_ssm.reshape(t, SSM_W),
        w_glu[layer].astype(BF16), row2(b_glu[layer]), w_out[layer].astype(BF16),
        row2(ln1_g[layer]), row2(ln1_b[layer]),
        jnp.transpose(w_router[layer].astype(F32)), b_router[layer].astype(F32).reshape(N_EXPERTS, 1))

    n_tiles = t // ROW_TILE
    experts = jnp.arange(N_EXPERTS, dtype=I32)
    tiles = jnp.arange(n_tiles, dtype=I32)
    tile_cnt = jnp.max(cnt, axis=-1).astype(I32)
    run = (tile_cnt + GROUP - 1) // GROUP * GROUP
    run_off = jnp.sum(jnp.where((experts[:, None] < experts[None, :])[None], run[:, :, None], 0), axis=1)
    run_before = jnp.sum(jnp.where((tiles[:, None] < tiles[None, :])[:, :, None], run[:, None, :], 0), axis=0)
    counts = jnp.sum(run, axis=0)
    padded = (counts + MOE_BLOCK - 1) // MOE_BLOCK * MOE_BLOCK
    padded_end = jnp.sum(jnp.where(experts[:, None] <= experts[None, :], padded[:, None], 0), axis=0)
    padded_start = padded_end - padded
    n_blocks = (t * TOP_K + n_tiles * N_EXPERTS * (GROUP - 1)) // MOE_BLOCK + N_EXPERTS
    block_row0 = jnp.arange(n_blocks, dtype=I32) * MOE_BLOCK
    block_expert = jnp.minimum(jnp.sum((padded_end[None, :] <= block_row0[:, None]).astype(I32), axis=1),
                               N_EXPERTS - 1)
    n_used = (padded_end[-1:] // MOE_BLOCK).astype(I32)
    ngroups = jnp.sum(run, axis=1) // GROUP
    local_row = jnp.arange(SORT_GROUPS, dtype=I32) * GROUP
    owner = jnp.minimum(jnp.sum(((run_off + run)[:, None, :] <= local_row[None, :, None]).astype(I32), axis=-1),
                        N_EXPERTS - 1)
    shift = (padded_start[None, :] + run_before - run_off) // GROUP
    dstg = jnp.sum(jnp.where(owner[..., None] == experts, shift[:, None, :], 0), axis=-1) + local_row // GROUP
    dstg = jnp.where(local_row[None, :] // GROUP < ngroups[:, None], dstg, 0).reshape(n_tiles, 1, SORT_GROUPS)
    run_off_tok = jnp.repeat(run_off, ROW_TILE, axis=0)
    pos_t = jnp.sum(jnp.where(idx_t[..., None] == experts, run_off_tok[None], 0), axis=-1) + rank_t

    xs = _dispatch(ngroups, ((padded_start + counts) // GROUP).astype(I32),
                   ((padded - counts) // GROUP).astype(I32), n_used, dstg, pos_t, h1, n_blocks * MOE_BLOCK)
    b3 = lambda v: v.astype(F32)[:, None, :]
    ys = _experts(block_expert, n_used, xs, w_gate[layer], b3(b_gate[layer]), w_up[layer], b3(b_up[layer]),
                  w_down[layer], b3(b_down[layer]))
    out = _combine(ngroups, dstg, jnp.transpose(pos_t), jnp.transpose(gate_t), h1, ys,
                   row2(ln2_g[layer]), row2(ln2_b[layer]))
    return out.reshape(bsz, seq, dm)
```

```python
import functools
import math

import jax
import jax.numpy as jnp
from jax import lax
from jax.experimental import pallas as pl
from jax.experimental.pallas import tpu as pltpu

F32 = jnp.float32
BF16 = jnp.bfloat16
I32 = jnp.int32

DEPTH = 1
N_META = 16
CHUNK = 64
ATT_HEADS = 4
HEAD_DIM = 64
HEAD_W = 2 * HEAD_DIM
ATT_W = ATT_HEADS * HEAD_W
SSM_W = 512
SSM_CG = 16
SSM_G = SSM_W // SSM_CG
SSM_N = 64
N_BUCKETS = 32
MAX_DISTANCE = 128
N_EXPERTS = 32
TOP_K = 4
SWIGLU_LIMIT = 7.0
SWIGLU_ALPHA = 1.702
LN_EPS = 1e-5
NEG_INF = -1e30
DEEPNORM_ALPHA = (2.0 * DEPTH) ** 0.25
LOG2E = 1.4426950408889634

Q_TILE = 512
KV_TILE = 512
V_ONES = 16
V_ROWS = HEAD_W + V_ONES
META_PAD = 128
SSM_T = 64
MIX_TILE = 512
ROW_TILE = 256
MOE_BLOCK = 512
GROUP = 8
COPY_BATCH = 32
SORT_ROWS = 1280
SORT_GROUPS = SORT_ROWS // GROUP
assert SORT_ROWS >= ROW_TILE * TOP_K + N_EXPERTS * (GROUP - 1)
V7X_VMEM_BYTES = 64 * 1024 * 1024
VMEM_LIMIT = V7X_VMEM_BYTES * 7 // 8


def _layer_norm(x, g, b):
    mu = jnp.mean(x, axis=-1, keepdims=True)
    xc = x - mu
    var = jnp.mean(xc * xc, axis=-1, keepdims=True)
    return xc * lax.rsqrt(var + LN_EPS) * g + b


def _dot(a, b):
    return jnp.dot(a, b, preferred_element_type=F32)


def _dot_nt(a, b, precision=None):
    return lax.dot_general(a, b, (((1,), (1,)), ((), ())), precision=precision,
                           preferred_element_type=F32)


def _inproj_kernel(x_ref, g_ref, b_ref, w_ref, q_ref, k_ref, vt_ref, u_ref, *, kv_tile, q_scale):
    h = _layer_norm(x_ref[0], g_ref[...], b_ref[...]).astype(BF16)
    tm = h.shape[0]
    q_ref[0] = (_dot(h, w_ref[:, 0:ATT_W]) * q_scale).astype(BF16)
    k_ref[0] = _dot(h, w_ref[:, ATT_W:2 * ATT_W]).astype(BF16)
    v = _dot(h, w_ref[:, 2 * ATT_W:3 * ATT_W])
    ones = jnp.ones((V_ONES, kv_tile), BF16)
    for hh in range(ATT_HEADS):
        vt = v[:, hh * HEAD_W:(hh + 1) * HEAD_W].T.astype(BF16)
        for j in range(tm // kv_tile):
            vt_ref[0, hh, j, 0:HEAD_W, :] = vt[:, j * kv_tile:(j + 1) * kv_tile]
            vt_ref[0, hh, j, HEAD_W:V_ROWS, :] = ones
    u_ref[0] = _dot(h, w_ref[:, 3 * ATT_W:]).astype(BF16)


def _inproj(x, g, b, w, *, tm, kv_tile):
    bsz, s, d = x.shape
    q_scale = HEAD_DIM ** -0.5 * LOG2E
    n_cols = w.shape[1]
    row = lambda bi, i: (bi, i, 0)
    return pl.pallas_call(
        functools.partial(_inproj_kernel, kv_tile=kv_tile, q_scale=q_scale),
        grid=(bsz, s // tm),
        in_specs=[
            pl.BlockSpec((1, tm, d), row),
            pl.BlockSpec((1, d), lambda bi, i: (0, 0)),
            pl.BlockSpec((1, d), lambda bi, i: (0, 0)),
            pl.BlockSpec((d, n_cols), lambda bi, i: (0, 0)),
        ],
        out_specs=[
            pl.BlockSpec((1, tm, ATT_W), row),
            pl.BlockSpec((1, tm, ATT_W), row),
            pl.BlockSpec((1, ATT_HEADS, tm // kv_tile, V_ROWS, kv_tile), lambda bi, i: (bi, 0, i, 0, 0)),
            pl.BlockSpec((1, tm, SSM_W), row),
        ],
        out_shape=[
            jax.ShapeDtypeStruct((bsz, s, ATT_W), BF16),
            jax.ShapeDtypeStruct((bsz, s, ATT_W), BF16),
            jax.ShapeDtypeStruct((bsz, ATT_HEADS, s // kv_tile, V_ROWS, kv_tile), BF16),
            jax.ShapeDtypeStruct((bsz, s, SSM_W), BF16),
        ],
        compiler_params=pltpu.CompilerParams(
            dimension_semantics=("parallel", "parallel"), vmem_limit_bytes=VMEM_LIMIT),
        name="inproj",
    )(x, g, b, w)


def _attn_kernel(lam_ref, q_ref, k_ref, vt_ref, km_ref, vtm_ref, bias_ref, g_ref, o_ref,
                 m_ref, acc_ref, qz_ref, sa_ref, sb_ref, sn_ref, *, out_scale):
    i = pl.program_id(2)
    qt = q_ref[0].astype(F32).T
    feat = lax.broadcasted_iota(I32, qt.shape, 0)
    qz_ref[:, 0:Q_TILE] = jnp.where(feat < HEAD_DIM, qt, 0.0).astype(BF16)
    qz_ref[:, Q_TILE:] = jnp.where(feat >= HEAD_DIM, qt, 0.0).astype(BF16)

    m_ref[...] = jnp.full(m_ref.shape, NEG_INF, F32)
    acc_ref[...] = jnp.zeros(acc_ref.shape, F32)

    def scores(kt):
        return _dot(kt, qz_ref[...])

    def k_tile(t):
        return k_ref[0, pl.ds(pl.multiple_of(t * KV_TILE, KV_TILE), KV_TILE), :]

    def absorb(s, pv):
        m_old = m_ref[...]
        m_new = jnp.maximum(m_old, jnp.max(s, axis=0, keepdims=True))
        alpha = jnp.exp2(m_old - m_new)
        p = jnp.exp2(s - m_new).astype(BF16)
        acc_ref[...] = acc_ref[...] * alpha + pv(p)
        m_ref[...] = m_new

    def tile_pv(t):
        return lambda p: _dot(vt_ref[0, 0, t], p)

    n_far = jnp.maximum(i - 1, 0)
    peel = n_far % 2

    sn_ref[...] = scores(km_ref[...])
    sa_ref[...] = scores(k_tile(0))
    absorb(sn_ref[...] + bias_ref[0, 0, 2 * KV_TILE:, :], lambda p: _dot(vtm_ref[0], p))

    @pl.when(peel == 1)
    def _():
        sb_ref[...] = scores(k_tile(1))
        absorb(sa_ref[...], tile_pv(0))
        sa_ref[...] = sb_ref[...]

    def far_pair(j, carry):
        t0 = peel + 2 * j
        s_cur = sa_ref[...]
        sb_ref[...] = scores(k_tile(t0 + 1))
        absorb(s_cur, tile_pv(t0))
        s_cur = sb_ref[...]
        sa_ref[...] = scores(k_tile(t0 + 2))
        absorb(s_cur, tile_pv(t0 + 1))
        return carry

    lax.fori_loop(0, n_far // 2, far_pair, 0)

    sb_ref[...] = scores(k_tile(i))
    absorb(sa_ref[...] + bias_ref[0, 0, 0:KV_TILE, :], tile_pv(n_far))
    absorb(sb_ref[...] + bias_ref[0, 0, KV_TILE:2 * KV_TILE, :], tile_pv(i))

    acc = acc_ref[...]
    lam = lam_ref[0]
    o1 = acc[0:HEAD_W, 0:Q_TILE] / acc[HEAD_W:HEAD_W + 1, 0:Q_TILE]
    o2 = acc[0:HEAD_W, Q_TILE:] / acc[HEAD_W:HEAD_W + 1, Q_TILE:]
    o = o1 - lam * o2
    ms = jnp.mean(o * o, axis=0, keepdims=True)
    o = o * lax.rsqrt(ms + LN_EPS) * g_ref[...] * out_scale
    o_ref[0] = o.T.astype(o_ref.dtype)


def _attention(lam, q, k, vt, k_meta, vt_meta, bias, subln_g, *, lam_init):
    bsz, s, _ = q.shape
    nq = s // Q_TILE
    n_near = META_PAD + 2 * KV_TILE
    grid_spec = pltpu.PrefetchScalarGridSpec(
        num_scalar_prefetch=1,
        grid=(bsz, ATT_HEADS, nq),
        in_specs=[
            pl.BlockSpec((1, Q_TILE, HEAD_W), lambda b, h, i, lam: (b, i, h)),
            pl.BlockSpec((1, s, HEAD_W), lambda b, h, i, lam: (b, 0, h)),
            pl.BlockSpec((1, 1, s // KV_TILE, V_ROWS, KV_TILE), lambda b, h, i, lam: (b, h, 0, 0, 0)),
            pl.BlockSpec((META_PAD, HEAD_W), lambda b, h, i, lam: (0, h)),
            pl.BlockSpec((1, V_ROWS, META_PAD), lambda b, h, i, lam: (h, 0, 0)),
            pl.BlockSpec((1, 1, n_near, 2 * Q_TILE), lambda b, h, i, lam: (h, jnp.minimum(i, 1), 0, 0)),
            pl.BlockSpec((HEAD_W, 1), lambda b, h, i, lam: (0, 0)),
        ],
        out_specs=pl.BlockSpec((1, Q_TILE, HEAD_W), lambda b, h, i, lam: (b, i, h)),
        scratch_shapes=[
            pltpu.VMEM((1, 2 * Q_TILE), F32), pltpu.VMEM((V_ROWS, 2 * Q_TILE), F32),
            pltpu.VMEM((HEAD_W, 2 * Q_TILE), BF16),
            pltpu.VMEM((KV_TILE, 2 * Q_TILE), F32), pltpu.VMEM((KV_TILE, 2 * Q_TILE), F32),
            pltpu.VMEM((META_PAD, 2 * Q_TILE), F32),
        ],
    )
    return pl.pallas_call(
        functools.partial(_attn_kernel, out_scale=1.0 - lam_init),
        grid_spec=grid_spec,
        out_shape=jax.ShapeDtypeStruct((bsz, s, ATT_W), BF16),
        compiler_params=pltpu.CompilerParams(
            dimension_semantics=("parallel", "parallel", "arbitrary"), vmem_limit_bytes=VMEM_LIMIT),
        name="diff_attention",
    )(lam, q, k, vt, k_meta, vt_meta, bias, subln_g)


def _t5_bucket(rel):
    nb = N_BUCKETS // 2
    max_exact = nb // 2
    ret = jnp.where(rel > 0, nb, 0)
    n = jnp.abs(rel)
    n_f = jnp.maximum(n, 1).astype(F32)
    large = max_exact + (jnp.log(n_f / max_exact) / math.log(MAX_DISTANCE / max_exact)
                         * (nb - max_exact)).astype(I32)
    large = jnp.minimum(large, nb - 1)
    return ret + jnp.where(n < max_exact, n, large)


def _near_bias(rel_bias):
    table = rel_bias.astype(F32)
    far = table[N_BUCKETS // 2 - 1]
    c = jnp.arange(Q_TILE, dtype=I32)[None, :]
    r = jnp.arange(KV_TILE, dtype=I32)[:, None]

    shifted = jnp.transpose(table - far) * LOG2E

    def bias_of(offset, rows, cols=Q_TILE):
        rel = jnp.arange(rows, dtype=I32)[:, None] - c[:, :cols] + offset
        hot = _t5_bucket(rel)[None, :, :, None] == jnp.arange(N_BUCKETS, dtype=I32)
        return jnp.sum(jnp.where(hot, shifted[:, None, None, :], 0.0), axis=-1)

    nd = KV_TILE // MAX_DISTANCE
    own_bias = sum(
        (jnp.eye(nd, k=-d, dtype=F32)[None, :, None, :, None]
         * bias_of(d * MAX_DISTANCE, MAX_DISTANCE, MAX_DISTANCE)[:, None, :, None, :]).reshape(
             ATT_HEADS, KV_TILE, Q_TILE)
        for d in (-1, 0, 1))
    own = jnp.where((r // CHUNK <= c // CHUNK)[None], own_bias, NEG_INF)
    near = MAX_DISTANCE
    prev = jnp.zeros((ATT_HEADS, KV_TILE, Q_TILE), F32).at[:, KV_TILE - near:, :near].set(bias_of(-near, near, near))
    rm = jnp.arange(META_PAD, dtype=I32)[:, None]
    meta_ok = (rm < N_META)[None]
    meta0 = jnp.where(meta_ok, jnp.pad(bias_of(-N_META, N_META), ((0, 0), (0, META_PAD - N_META), (0, 0))), NEG_INF)
    meta1 = jnp.where(meta_ok, jnp.zeros((ATT_HEADS, META_PAD, Q_TILE), F32), NEG_INF)
    assert N_META + Q_TILE - (N_META - 1) >= MAX_DISTANCE and KV_TILE >= near and Q_TILE >= near
    v0 = jnp.concatenate([jnp.full_like(prev, NEG_INF), own, meta0], axis=1)
    v1 = jnp.concatenate([prev, own, meta1], axis=1)
    both = jnp.stack([v0, v1], axis=1)
    return jnp.concatenate([both, both], axis=-1)


def _ssm_tables(a_re, a_im, log_step, b_re, b_im, c_re, c_im, d_skip):
    hi = lax.Precision.HIGHEST
    t_len = SSM_T
    step = jnp.exp(log_step.astype(F32))[:, None]
    ar = jnp.minimum(a_re.astype(F32), -1e-4)
    ai = a_im.astype(F32)
    mag = jnp.exp(step * ar)
    ph = step * ai
    abar_re = mag * jnp.cos(ph)
    abar_im = mag * jnp.sin(ph)
    den = ar * ar + ai * ai
    e_re = abar_re - 1.0
    e_im = abar_im
    f_re = (e_re * ar + e_im * ai) / den
    f_im = (e_im * ar - e_re * ai) / den
    br = b_re.astype(F32)
    bi = b_im.astype(F32)
    bb_re = f_re[..., None] * br - f_im[..., None] * bi
    bb_im = f_re[..., None] * bi + f_im[..., None] * br
    tau = jnp.arange(t_len + 1, dtype=F32)[None, :, None]
    pmag = jnp.exp(tau * (step * ar)[:, None, :])
    pph = tau * ph[:, None, :]
    pw_re = pmag * jnp.cos(pph)
    pw_im = pmag * jnp.sin(pph)
    bbt_re = jnp.swapaxes(bb_re, 1, 2)[:, None]
    bbt_im = jnp.swapaxes(bb_im, 1, 2)[:, None]

    def times_bbar(p_re, p_im):
        return (p_re[:, :, None, :] * bbt_re - p_im[:, :, None, :] * bbt_im,
                p_re[:, :, None, :] * bbt_im + p_im[:, :, None, :] * bbt_re)

    cr = c_re.astype(F32)
    ci = c_im.astype(F32)
    bi_re = jnp.swapaxes(bb_re, 1, 2)[:, :, None, :]
    bi_im = jnp.swapaxes(bb_im, 1, 2)[:, :, None, :]
    cb_re = cr[:, None] * bi_re - ci[:, None] * bi_im
    cb_im = cr[:, None] * bi_im + ci[:, None] * bi_re
    kern = jnp.einsum('gikm,gtm->gitk', jnp.concatenate([cb_re, -cb_im], axis=-1),
                      jnp.concatenate([pw_re[:, :t_len], pw_im[:, :t_len]], axis=-1), precision=hi)
    skip = d_skip.astype(F32).reshape(SSM_G, SSM_CG)
    kern = kern.at[:, :, 0, :].add(skip[:, None, :] * jnp.eye(SSM_CG, dtype=F32))
    kflat = kern.reshape(SSM_G, SSM_CG, t_len * SSM_CG)
    kpad = jnp.concatenate([jnp.zeros_like(kflat), kflat], axis=-1)
    back = jnp.arange(t_len - 1, -1, -1, dtype=F32)[None, :, None]
    bmag = jnp.exp(back * (step * ar)[:, None, :])
    bph = back * ph[:, None, :]
    pt_re, pt_im = times_bbar(bmag * jnp.cos(bph), bmag * jnp.sin(bph))
    pt = jnp.concatenate([pt_re, pt_im], axis=-1).reshape(SSM_G, t_len * SSM_CG, 2 * SSM_N)
    up_re = jnp.swapaxes(pw_re[:, 1:], 1, 2)[..., None]
    up_im = jnp.swapaxes(pw_im[:, 1:], 1, 2)[..., None]
    crt = jnp.swapaxes(cr, 1, 2)[:, :, None, :]
    cit = jnp.swapaxes(ci, 1, 2)[:, :, None, :]
    q_re = crt * up_re - cit * up_im
    q_im = crt * up_im + cit * up_re
    qt = jnp.concatenate([q_re, -q_im], axis=1).reshape(SSM_G, 2 * SSM_N, t_len * SSM_CG)
    at_re = pw_re[:, t_len]
    at_im = pw_im[:, t_len]
    a1 = jnp.concatenate([at_re, at_re], axis=-1)
    a2 = jnp.concatenate([-at_im, at_im], axis=-1)
    return kpad, pt.astype(BF16), qt.astype(BF16), a1, a2


def _ssm_state_kernel(u_ref, pt_ref, s_ref):
    s_ref[0] = _dot(u_ref[0], pt_ref[0])


def _ssm_scan_kernel(s_ref, x0_ref, a1_ref, a2_ref, x_ref, *, n_chunks, bsz):
    a1 = a1_ref[...][:, None, :]
    a2 = a2_ref[...][:, None, :]

    def body(c, x):
        rows = pl.ds(c * bsz, bsz)
        x_ref[:, rows, :] = x
        return a1 * x + a2 * pltpu.roll(x, SSM_N, 2) + s_ref[:, rows, :]

    x0 = jnp.broadcast_to(x0_ref[:, 0:1, :], (SSM_G, bsz, 2 * SSM_N))
    lax.fori_loop(0, n_chunks, body, x0)


def _ssm_out_kernel(u_ref, x_ref, kp_ref, qt_ref, y_ref, mt_ref):
    kp = kp_ref[0]
    width = SSM_T * SSM_CG
    lane_tile = 128
    for sub in range(0, lane_tile, SSM_CG):
        shifted = kp if sub == 0 else pltpu.roll(kp, 2 * width - sub, 1)
        for s in range(SSM_T):
            off = (SSM_T - s) * SSM_CG
            if off % lane_tile == sub:
                base = off - sub
                mt_ref[s * SSM_CG:(s + 1) * SSM_CG, :] = shifted[:, base:base + width].astype(BF16)
    x = x_ref[0]
    x_hi = x.astype(BF16)
    x_lo = (x - x_hi.astype(F32)).astype(BF16)
    y = _dot(u_ref[0], mt_ref[...]) + _dot(x_hi, qt_ref[0]) + _dot(x_lo, qt_ref[0])
    y_ref[0] = y.astype(y_ref.dtype)


def _ssm_chunk_state(ug, pt):
    g, r, w = ug.shape
    n2 = 2 * SSM_N
    per_g = lambda gi: (gi, 0, 0)
    return pl.pallas_call(
        _ssm_state_kernel,
        grid=(g,),
        in_specs=[pl.BlockSpec((1, r, w), per_g), pl.BlockSpec((1, w, n2), per_g)],
        out_specs=pl.BlockSpec((1, r, n2), per_g),
        out_shape=jax.ShapeDtypeStruct((g, r, n2), F32),
        compiler_params=pltpu.CompilerParams(dimension_semantics=("parallel",), vmem_limit_bytes=VMEM_LIMIT),
        name="ssm_chunk_state",
    )(ug, pt)


def _ssm(ug, ug_meta, kpad, pt, qt, a1, a2, *, n_chunks, bsz):
    g, r, w = ug.shape
    n2 = 2 * SSM_N
    per_g = lambda gi: (gi, 0, 0)
    params = pltpu.CompilerParams(dimension_semantics=("parallel",), vmem_limit_bytes=VMEM_LIMIT)
    s = _ssm_chunk_state(ug, pt)
    x0 = _ssm_chunk_state(ug_meta, pt[:, w - ug_meta.shape[2]:, :])
    x = pl.pallas_call(
        functools.partial(_ssm_scan_kernel, n_chunks=n_chunks, bsz=bsz),
        out_shape=jax.ShapeDtypeStruct((g, r, n2), F32),
        compiler_params=pltpu.CompilerParams(vmem_limit_bytes=VMEM_LIMIT),
        name="ssm_chunk_scan",
    )(s, x0, a1, a2)
    return pl.pallas_call(
        _ssm_out_kernel,
        grid=(g,),
        in_specs=[pl.BlockSpec((1, r, w), per_g), pl.BlockSpec((1, r, n2), per_g),
                  pl.BlockSpec((1, SSM_CG, 2 * w), per_g), pl.BlockSpec((1, n2, w), per_g)],
        out_specs=pl.BlockSpec((1, r, w), per_g),
        out_shape=jax.ShapeDtypeStruct((g, r, w), BF16),
        scratch_shapes=[pltpu.VMEM((w, w), BF16)],
        compiler_params=params,
        name="ssm_output",
    )(ug, x, kpad, qt)


def _mix_kernel(x_ref, gi_ref, bi_ref, att_ref, y_ref, wglu_ref, bglu_ref, wout_ref, g1_ref, b1_ref,
                wr_ref, br_ref, h1_ref, idx_ref, gate_ref, rank_ref, cnt_ref):
    h0 = _layer_norm(x_ref[...], gi_ref[...], bi_ref[...])
    y = y_ref[...].astype(F32)
    y = y * (0.5 * (1.0 + jnp.tanh(math.sqrt(2.0 / math.pi) * (y + 0.044715 * (y * y * y)))))
    y = y * jax.nn.sigmoid(_dot(y.astype(BF16), wglu_ref[...]) + bglu_ref[...])
    mix = _dot(att_ref[...], wout_ref[0:ATT_W, :]) + _dot(y.astype(BF16), wout_ref[ATT_W:, :])
    h1 = _layer_norm(DEEPNORM_ALPHA * h0 + mix, g1_ref[...], b1_ref[...])
    h1_ref[...] = h1

    logits = _dot_nt(wr_ref[...], h1, precision=lax.Precision.HIGHEST) + br_ref[...]
    tm = logits.shape[1]
    eidx = lax.broadcasted_iota(I32, logits.shape, 0)
    vals, hots = [], []
    rest = logits
    for _ in range(TOP_K):
        mx = jnp.max(rest, axis=0, keepdims=True)
        first = jnp.min(jnp.where(rest == mx, eidx, N_EXPERTS), axis=0, keepdims=True)
        hot = eidx == first
        vals.append(mx)
        hots.append(hot)
        rest = jnp.where(hot, -jnp.inf, rest)
    exps = [jnp.exp(v - vals[0]) for v in vals]
    denom = exps[0] + exps[1] + exps[2] + exps[3]
    gate_ref[...] = jnp.concatenate([e / denom for e in exps], axis=0)
    idx_ref[...] = jnp.concatenate(
        [jnp.sum(jnp.where(h, eidx, 0), axis=0, keepdims=True) for h in hots], axis=0)

    hot_all = (hots[0] | hots[1] | hots[2] | hots[3]).astype(F32)
    sub = ROW_TILE
    tri = (lax.broadcasted_iota(I32, (sub, sub), 0) < lax.broadcasted_iota(I32, (sub, sub), 1)).astype(BF16)
    for part in range(tm // sub):
        cols = slice(part * sub, (part + 1) * sub)
        before = _dot(hot_all[:, cols].astype(BF16), tri)
        rank_ref[:, cols] = jnp.concatenate(
            [jnp.sum(jnp.where(h[:, cols], before, 0.0), axis=0, keepdims=True) for h in hots], axis=0).astype(I32)
        cnt_ref[part] = jnp.broadcast_to(jnp.sum(hot_all[:, cols], axis=1, keepdims=True), cnt_ref.shape[1:])


def _mix(x2, gi, bi, att, y, wglu, bglu, wout, g1, b1, wr_t, br):
    t, d = x2.shape
    tm = MIX_TILE
    row = lambda i: (i, 0)
    col = lambda i: (0, i)
    const = lambda i: (0, 0)
    return pl.pallas_call(
        _mix_kernel,
        grid=(t // tm,),
        in_specs=[
            pl.BlockSpec((tm, d), row), pl.BlockSpec((1, d), const), pl.BlockSpec((1, d), const),
            pl.BlockSpec((tm, ATT_W), row), pl.BlockSpec((tm, SSM_W), row),
            pl.BlockSpec((SSM_W, SSM_W), const), pl.BlockSpec((1, SSM_W), const),
            pl.BlockSpec((d, d), const), pl.BlockSpec((1, d), const), pl.BlockSpec((1, d), const),
            pl.BlockSpec((N_EXPERTS, d), const), pl.BlockSpec((N_EXPERTS, 1), const),
        ],
        out_specs=[
            pl.BlockSpec((tm, d), row),
            pl.BlockSpec((TOP_K, tm), col), pl.BlockSpec((TOP_K, tm), col), pl.BlockSpec((TOP_K, tm), col),
            pl.BlockSpec((tm // ROW_TILE, N_EXPERTS, 128), lambda i: (i, 0, 0)),
        ],
        out_shape=[
            jax.ShapeDtypeStruct((t, d), F32),
            jax.ShapeDtypeStruct((TOP_K, t), I32), jax.ShapeDtypeStruct((TOP_K, t), F32),
            jax.ShapeDtypeStruct((TOP_K, t), I32),
            jax.ShapeDtypeStruct((t // ROW_TILE, N_EXPERTS, 128), F32),
        ],
        compiler_params=pltpu.CompilerParams(dimension_semantics=("parallel",), vmem_limit_bytes=VMEM_LIMIT),
        name="mix_ln1_router",
    )(x2, gi, bi, att, y, wglu, bglu, wout, g1, b1, wr_t, br)


_HI_BITS = -65536


def _pack_halves(x):
    half = x.shape[1] // 2
    lo = lax.bitcast_convert_type(x[:, :half], I32)
    hi = lax.bitcast_convert_type(x[:, half:], I32)
    return lax.shift_right_logical(lo, 16) | (hi & _HI_BITS)


def _unpack_halves(w):
    lo = lax.bitcast_convert_type(lax.shift_left(w, 16), F32)
    hi = lax.bitcast_convert_type(w & _HI_BITS, F32)
    return lo.astype(BF16), hi.astype(BF16)


def _group_copy(src, src_group, dst, dst_group, sem):
    return pltpu.make_async_copy(src.at[pl.ds(pl.multiple_of(src_group * GROUP, GROUP), GROUP), :],
                                 dst.at[pl.ds(pl.multiple_of(dst_group * GROUP, GROUP), GROUP), :], sem)


def _batched(count, fn):
    def body(q, c):
        fn(q * COPY_BATCH, COPY_BATCH)
        return c

    whole = count // COPY_BATCH
    lax.fori_loop(0, whole, body, 0)
    rest = count - whole * COPY_BATCH
    n, done = COPY_BATCH // 2, whole * COPY_BATCH
    while n >= 1:
        take = (rest // n) % 2

        @pl.when(take == 1)
        def _(n=n, done=done):
            fn(done, n)

        done = done + take * n
        n //= 2


def _dispatch_kernel(ngroups_ref, pad_start_ref, pad_count_ref, n_used_ref, dstg_ref, pos_ref, h1_ref, xs_hbm,
                     sorted_ref, zero_ref, sems, zsem):
    step = pl.program_id(0)
    n_steps = pl.num_programs(0)
    slot = step % 2
    n_blocks = xs_hbm.shape[0] // MOE_BLOCK

    def zero_block(blk):
        return pltpu.make_async_copy(zero_ref, xs_hbm.at[pl.ds(blk * MOE_BLOCK, MOE_BLOCK), :], zsem)

    @pl.when(step == 0)
    def _():
        zero_ref[...] = jnp.zeros(zero_ref.shape, I32)
        for e in range(N_EXPERTS):
            start = pad_start_ref[e]
            count = pad_count_ref[e]

            def fill(j, c):
                _group_copy(zero_ref, 0, xs_hbm, start + j, zsem).start()
                return c

            def drain(j, c):
                _group_copy(zero_ref, 0, xs_hbm, start, zsem).wait()
                return c

            lax.fori_loop(0, count, fill, 0)
            lax.fori_loop(0, count, drain, 0)

        def fill_block(blk, c):
            zero_block(blk).start()
            return c

        def drain_block(blk, c):
            zero_block(blk).wait()
            return c

        lax.fori_loop(n_used_ref[0], n_blocks, fill_block, 0)
        lax.fori_loop(n_used_ref[0], n_blocks, drain_block, 0)

    pos = pos_ref[...]
    rows = lax.broadcasted_iota(I32, (SORT_ROWS, pos.shape[1]), 0)
    place = rows == pos[0:1]
    for k in range(1, TOP_K):
        place = place | (rows == pos[k:k + 1])
    sorted_ref[slot] = _pack_halves(_dot(place.astype(BF16), h1_ref[...].astype(BF16)))

    _batched(ngroups_ref[step],
             lambda j, n: [_group_copy(sorted_ref.at[slot], j + r, xs_hbm, dstg_ref[0, 0, j + r],
                                       sems.at[slot]).start() for r in range(n)])

    def wait_tile(s, count):
        _batched(count, lambda j, n: pltpu.make_async_copy(
            sorted_ref.at[s, pl.ds(0, n * GROUP), :], xs_hbm.at[pl.ds(0, n * GROUP), :], sems.at[s]).wait())

    @pl.when(step > 0)
    def _():
        wait_tile(1 - slot, ngroups_ref[jnp.maximum(step - 1, 0)])

    @pl.when(step == n_steps - 1)
    def _():
        wait_tile(slot, ngroups_ref[step])


def _dispatch(ngroups, pad_start, pad_count, n_used, dstg, pos_t, h1, n_rows):
    t, d = h1.shape
    tm = ROW_TILE
    grid_spec = pltpu.PrefetchScalarGridSpec(
        num_scalar_prefetch=4,
        grid=(t // tm,),
        in_specs=[
            pl.BlockSpec((1, 1, SORT_GROUPS), lambda i, *_: (i, 0, 0), memory_space=pltpu.SMEM),
            pl.BlockSpec((TOP_K, tm), lambda i, *_: (0, i)),
            pl.BlockSpec((tm, d), lambda i, *_: (i, 0)),
        ],
        out_specs=pl.BlockSpec(memory_space=pl.ANY),
        scratch_shapes=[pltpu.VMEM((2, SORT_ROWS, d // 2), I32), pltpu.VMEM((MOE_BLOCK, d // 2), I32),
                        pltpu.SemaphoreType.DMA((2,)), pltpu.SemaphoreType.DMA],
    )
    return pl.pallas_call(
        _dispatch_kernel,
        grid_spec=grid_spec,
        out_shape=jax.ShapeDtypeStruct((n_rows, d // 2), I32),
        compiler_params=pltpu.CompilerParams(dimension_semantics=("arbitrary",), vmem_limit_bytes=VMEM_LIMIT,
                                             has_side_effects=True),
        name="moe_dispatch",
    )(ngroups, pad_start, pad_count, n_used, dstg, pos_t, h1)


def _expert_kernel(be_ref, nb_ref, x_ref, wg_ref, bg_ref, wu_ref, bu_ref, wd_ref, bd_ref, y_ref,
                   wgu_b, wd_b):
    i = pl.program_id(0)
    dff = wg_ref.shape[2]

    @pl.when(i < nb_ref[0])
    def _():
        @pl.when((i == 0) | (be_ref[i] != be_ref[jnp.maximum(i - 1, 0)]))
        def _():
            wgu_b[:, 0:dff] = wg_ref[0].astype(BF16)
            wgu_b[:, dff:] = wu_ref[0].astype(BF16)
            wd_b[...] = wd_ref[0].astype(BF16)

        x_lo, x_hi = _unpack_halves(x_ref[...])
        half = x_lo.shape[1]
        gate_up = _dot(x_lo, wgu_b[0:half, :]) + _dot(x_hi, wgu_b[half:, :])
        gate = gate_up[:, 0:dff] + bg_ref[0]
        up = gate_up[:, dff:] + bu_ref[0]
        gate = jnp.minimum(gate, SWIGLU_LIMIT)
        up = jnp.clip(up, -SWIGLU_LIMIT, SWIGLU_LIMIT)
        act = (up + 1.0) * gate * jax.nn.sigmoid(gate * SWIGLU_ALPHA)
        y = _dot(act.astype(BF16), wd_b[...]) + bd_ref[0]
        y_ref[...] = _pack_halves(y.astype(BF16).astype(F32))

    @pl.when(i >= nb_ref[0])
    def _():
        y_ref[...] = jnp.zeros(y_ref.shape, I32)


def _experts(block_expert, n_used, xs, wg, bg, wu, bu, wd, bd):
    n_rows, packed = xs.shape
    d, dff = wg.shape[1], wg.shape[2]
    assert packed * 2 == d and wd.shape[2] == d
    nb = n_rows // MOE_BLOCK
    last = lambda nu: jnp.maximum(nu[0] - 1, 0)
    blk = lambda i, be, nu: (jnp.minimum(i, last(nu)), 0)
    wsel = lambda i, be, nu: (be[jnp.minimum(i, last(nu))], 0, 0)
    grid_spec = pltpu.PrefetchScalarGridSpec(
        num_scalar_prefetch=2,
        grid=(nb,),
        in_specs=[
            pl.BlockSpec((MOE_BLOCK, packed), blk),
            pl.BlockSpec((1, d, dff), wsel), pl.BlockSpec((1, 1, dff), wsel),
            pl.BlockSpec((1, d, dff), wsel), pl.BlockSpec((1, 1, dff), wsel),
            pl.BlockSpec((1, dff, d), wsel), pl.BlockSpec((1, 1, d), wsel),
        ],
        out_specs=pl.BlockSpec((MOE_BLOCK, packed), lambda i, be, nu: (i, 0)),
        scratch_shapes=[pltpu.VMEM((d, 2 * dff), BF16), pltpu.VMEM((dff, d), BF16)],
    )
    return pl.pallas_call(
        _expert_kernel,
        grid_spec=grid_spec,
        out_shape=jax.ShapeDtypeStruct((n_rows, packed), I32),
        compiler_params=pltpu.CompilerParams(dimension_semantics=("arbitrary",), vmem_limit_bytes=VMEM_LIMIT),
        name="moe_experts",
    )(block_expert, n_used, xs, wg, bg, wu, bu, wd, bd)


def _combine_kernel(ngroups_ref, dstg_ref, dstg_next_ref, pos_ref, gate_ref, h1_ref, ys_hbm, g2_ref, b2_ref,
                    o_ref, buf, sems):
    step = pl.program_id(0)
    n_steps = pl.num_programs(0)
    slot = step % 2

    def fetch(dref, s, j):
        return _group_copy(ys_hbm, dref[0, 0, j], buf.at[s], j, sems.at[s])

    def issue(dref, s, count):
        _batched(count, lambda j, n: [fetch(dref, s, j + r).start() for r in range(n)])

    @pl.when(step == 0)
    def _():
        buf[...] = jnp.zeros(buf.shape, I32)
        issue(dstg_ref, 0, ngroups_ref[0])

    @pl.when(step + 1 < n_steps)
    def _():
        issue(dstg_next_ref, 1 - slot, ngroups_ref[jnp.minimum(step + 1, n_steps - 1)])

    _batched(ngroups_ref[step], lambda j, n: pltpu.make_async_copy(
        ys_hbm.at[pl.ds(0, n * GROUP), :], buf.at[slot, pl.ds(0, n * GROUP), :], sems.at[slot]).wait())

    pos = pos_ref[...]
    gate = gate_ref[...]
    lanes = lax.broadcasted_iota(I32, (pos.shape[0], SORT_ROWS), 1)
    sel = jnp.where(lanes == pos[:, 0:1], gate[:, 0:1], 0.0)
    for k in range(1, TOP_K):
        sel = sel + jnp.where(lanes == pos[:, k:k + 1], gate[:, k:k + 1], 0.0)
    sel_hi = sel.astype(BF16)
    sel_lo = (sel - sel_hi.astype(F32)).astype(BF16)
    y_lo, y_hi = _unpack_halves(buf[slot])
    ffn = jnp.concatenate([_dot(sel_hi, y_lo) + _dot(sel_lo, y_lo), _dot(sel_hi, y_hi) + _dot(sel_lo, y_hi)], axis=1)
    o_ref[...] = _layer_norm(DEEPNORM_ALPHA * h1_ref[...] + ffn, g2_ref[...], b2_ref[...])


def _combine(ngroups, dstg, pos_c, gates, h1, ys, g2, b2):
    t, d = h1.shape
    tm = ROW_TILE
    n_steps = t // tm
    row = lambda i, ng: (i, 0)
    const = lambda i, ng: (0, 0)
    grid_spec = pltpu.PrefetchScalarGridSpec(
        num_scalar_prefetch=1,
        grid=(n_steps,),
        in_specs=[
            pl.BlockSpec((1, 1, SORT_GROUPS), lambda i, ng: (i, 0, 0), memory_space=pltpu.SMEM),
            pl.BlockSpec((1, 1, SORT_GROUPS), lambda i, ng: (jnp.minimum(i + 1, n_steps - 1), 0, 0),
                         memory_space=pltpu.SMEM),
            pl.BlockSpec((tm, TOP_K), row), pl.BlockSpec((tm, TOP_K), row), pl.BlockSpec((tm, d), row),
            pl.BlockSpec(memory_space=pl.ANY),
            pl.BlockSpec((1, d), const), pl.BlockSpec((1, d), const),
        ],
        out_specs=pl.BlockSpec((tm, d), row),
        scratch_shapes=[pltpu.VMEM((2, SORT_ROWS, d // 2), I32), pltpu.SemaphoreType.DMA((2,))],
    )
    return pl.pallas_call(
        _combine_kernel,
        grid_spec=grid_spec,
        out_shape=jax.ShapeDtypeStruct((t, d), F32),
        compiler_params=pltpu.CompilerParams(dimension_semantics=("arbitrary",), vmem_limit_bytes=VMEM_LIMIT),
        name="moe_combine_ln2",
    )(ngroups, dstg, dstg, pos_c, gates, h1, ys, g2, b2)


def kernel(x, meta_tokens, ln_in_g, ln_in_b, rel_bias, w_in, lambda_q1, lambda_k1, lambda_q2, lambda_k2,
           subln_g, a_re, a_im, log_step, b_re, b_im, c_re, c_im, d_skip, w_glu, b_glu, w_out, ln1_g, ln1_b,
           w_router, b_router, w_gate, b_gate, w_up, b_up, w_down, b_down, ln2_g, ln2_b):
    bsz, seq, dm = x.shape
    assert seq % 512 == 0 and w_in.shape[0] == DEPTH == 1
    layer = 0
    row2 = lambda v: v.astype(F32).reshape(1, -1)

    w_in_b = w_in[layer].astype(BF16)
    gi, bi = row2(ln_in_g), row2(ln_in_b)
    q, k, vt, u = _inproj(x, gi, bi, w_in_b, tm=512, kv_tile=KV_TILE)
    meta = jnp.zeros((1, META_PAD, dm), x.dtype).at[0, :N_META].set(meta_tokens.astype(x.dtype))
    _, k_meta, vt_meta, u_meta = _inproj(meta, gi, bi, w_in_b, tm=META_PAD, kv_tile=META_PAD)

    lam_init = 0.8 - 0.6 * math.exp(-0.3 * layer)
    lam = (jnp.exp(jnp.sum(lambda_q1[layer].astype(F32) * lambda_k1[layer].astype(F32)))
           - jnp.exp(jnp.sum(lambda_q2[layer].astype(F32) * lambda_k2[layer].astype(F32))) + lam_init)
    att = _attention(lam.reshape(1), q, k, vt, k_meta[0], vt_meta[0, :, 0], _near_bias(rel_bias),
                     subln_g[layer].astype(F32).reshape(HEAD_W, 1), lam_init=lam_init)

    kpad, pt, qt, a1, a2 = _ssm_tables(a_re[layer], a_im[layer], log_step[layer], b_re[layer], b_im[layer],
                                     c_re[layer], c_im[layer], d_skip[layer])
    n_chunks = seq // SSM_T
    assert (n_chunks * bsz) % 16 == 0
    ug = jnp.transpose(u.reshape(bsz, n_chunks, SSM_T, SSM_G, SSM_CG), (3, 1, 0, 2, 4))
    ug = ug.reshape(SSM_G, n_chunks * bsz, SSM_T * SSM_CG)
    lead = jnp.zeros((8, N_META, SSM_W), BF16).at[0].set(u_meta[0, :N_META])
    ug_meta = jnp.transpose(lead.reshape(8, N_META, SSM_G, SSM_CG), (2, 0, 1, 3)).reshape(SSM_G, 8, N_META * SSM_CG)
    yg = _ssm(ug, ug_meta, kpad, pt, qt, a1, a2, n_chunks=n_chunks, bsz=bsz)
    y_ssm = jnp.transpose(yg.reshape(SSM_G, n_chunks, bsz, SSM_T, SSM_CG), (2, 1, 3, 0, 4))
    y_ssm = y_ssm.reshape(bsz, seq, SSM_W)

    t = bsz * seq
    h1, idx_t, gate_t, rank_t, cnt = _mix(
        x.reshape(t, dm), gi, bi, att.reshape(t, ATT_W), y_ssm.reshape(t, SSM_W),
        w_glu[layer].astype(BF16), row2(b_glu[layer]), w_out[layer].astype(BF16),
        row2(ln1_g[layer]), row2(ln1_b[layer]),
        jnp.transpose(w_router[layer].astype(F32)), b_router[layer].astype(F32).reshape(N_EXPERTS, 1))

    n_tiles = t // ROW_TILE
    experts = jnp.arange(N_EXPERTS, dtype=I32)
    tiles = jnp.arange(n_tiles, dtype=I32)
    tile_cnt = jnp.max(cnt, axis=-1).astype(I32)
    run = (tile_cnt + GROUP - 1) // GROUP * GROUP
    run_off = jnp.sum(jnp.where((experts[:, None] < experts[None, :])[None], run[:, :, None], 0), axis=1)
    run_before = jnp.sum(jnp.where((tiles[:, None] < tiles[None, :])[:, :, None], run[:, None, :], 0), axis=0)
    counts = jnp.sum(run, axis=0)
    padded = (counts + MOE_BLOCK - 1) // MOE_BLOCK * MOE_BLOCK
    padded_end = jnp.sum(jnp.where(experts[:, None] <= experts[None, :], padded[:, None], 0), axis=0)
    padded_start = padded_end - padded
    n_blocks = (t * TOP_K + n_tiles * N_EXPERTS * (GROUP - 1)) // MOE_BLOCK + N_EXPERTS
    block_row0 = jnp.arange(n_blocks, dtype=I32) * MOE_BLOCK
    block_expert = jnp.minimum(jnp.sum((padded_end[None, :] <= block_row0[:, None]).astype(I32), axis=1),
                               N_EXPERTS - 1)
    n_used = (padded_end[-1:] // MOE_BLOCK).astype(I32)
    ngroups = jnp.sum(run, axis=1) // GROUP
    local_row = jnp.arange(SORT_GROUPS, dtype=I32) * GROUP
    owner = jnp.minimum(jnp.sum(((run_off + run)[:, None, :] <= local_row[None, :, None]).astype(I32), axis=-1),
                        N_EXPERTS - 1)
    shift = (padded_start[None, :] + run_before - run_off) // GROUP
    dstg = jnp.sum(jnp.where(owner[..., None] == experts, shift[:, None, :], 0), axis=-1) + local_row // GROUP
    dstg = jnp.where(local_row[None, :] // GROUP < ngroups[:, None], dstg, 0).reshape(n_tiles, 1, SORT_GROUPS)
    run_off_tok = jnp.repeat(run_off, ROW_TILE, axis=0)
    pos_t = jnp.sum(jnp.where(idx_t[..., None] == experts, run_off_tok[None], 0), axis=-1) + rank_t

    xs = _dispatch(ngroups, ((padded_start + counts) // GROUP).astype(I32),
                   ((padded - counts) // GROUP).astype(I32), n_used, dstg, pos_t, h1, n_blocks * MOE_BLOCK)
    b3 = lambda v: v.astype(F32)[:, None, :]
    ys = _experts(block_expert, n_used, xs, w_gate[layer], b3(b_gate[layer]), w_up[layer], b3(b_up[layer]),
                  w_down[layer], b3(b_down[layer]))
    out = _combine(ngroups, dstg, jnp.transpose(pos_t), jnp.transpose(gate_t), h1, ys,
                   row2(ln2_g[layer]), row2(ln2_b[layer]))
    return out.reshape(bsz, seq, dm)
```

```python
import functools
import math

import jax
import jax.numpy as jnp
from jax import lax
from jax.experimental import pallas as pl
from jax.experimental.pallas import tpu as pltpu

F32 = jnp.float32
BF16 = jnp.bfloat16
I32 = jnp.int32

DEPTH = 1
N_META = 16
CHUNK = 64
ATT_HEADS = 4
HEAD_DIM = 64
HEAD_W = 2 * HEAD_DIM
ATT_W = ATT_HEADS * HEAD_W
SSM_W = 512
SSM_CG = 16
SSM_G = SSM_W // SSM_CG
SSM_N = 64
N_BUCKETS = 32
MAX_DISTANCE = 128
N_EXPERTS = 32
TOP_K = 4
SWIGLU_LIMIT = 7.0
SWIGLU_ALPHA = 1.702
LN_EPS = 1e-5
NEG_INF = -1e30
DEEPNORM_ALPHA = (2.0 * DEPTH) ** 0.25
LOG2E = 1.4426950408889634

Q_TILE = 512
KV_TILE = 512
V_ONES = 16
V_ROWS = HEAD_W + V_ONES
META_PAD = 128
SSM_T = 64
MIX_TILE = 512
ROW_TILE = 256
MOE_BLOCK = 512
GROUP = 8
COPY_BATCH = 32
SORT_ROWS = 1280
SORT_GROUPS = SORT_ROWS // GROUP
assert SORT_ROWS >= ROW_TILE * TOP_K + N_EXPERTS * (GROUP - 1)
V7X_VMEM_BYTES = 64 * 1024 * 1024
VMEM_LIMIT = V7X_VMEM_BYTES * 7 // 8


def _layer_norm(x, g, b):
    mu = jnp.mean(x, axis=-1, keepdims=True)
    xc = x - mu
    var = jnp.mean(xc * xc, axis=-1, keepdims=True)
    return xc * lax.rsqrt(var + LN_EPS) * g + b


def _dot(a, b):
    return jnp.dot(a, b, preferred_element_type=F32)


def _dot_nt(a, b, precision=None):
    return lax.dot_general(a, b, (((1,), (1,)), ((), ())), precision=precision,
                           preferred_element_type=F32)


def _inproj_kernel(x_ref, g_ref, b_ref, w_ref, q_ref, k_ref, vt_ref, u_ref, *, kv_tile, q_scale):
    h = _layer_norm(x_ref[0], g_ref[...], b_ref[...]).astype(BF16)
    tm = h.shape[0]
    q_ref[0] = (_dot(h, w_ref[:, 0:ATT_W]) * q_scale).astype(BF16)
    k_ref[0] = _dot(h, w_ref[:, ATT_W:2 * ATT_W]).astype(BF16)
    v = _dot(h, w_ref[:, 2 * ATT_W:3 * ATT_W])
    ones = jnp.ones((V_ONES, kv_tile), BF16)
    for hh in range(ATT_HEADS):
        vt = v[:, hh * HEAD_W:(hh + 1) * HEAD_W].T.astype(BF16)
        for j in range(tm // kv_tile):
            vt_ref[0, hh, j, 0:HEAD_W, :] = vt[:, j * kv_tile:(j + 1) * kv_tile]
            vt_ref[0, hh, j, HEAD_W:V_ROWS, :] = ones
    u_ref[0] = _dot(h, w_ref[:, 3 * ATT_W:]).astype(BF16)


def _inproj(x, g, b, w, *, tm, kv_tile):
    bsz, s, d = x.shape
    q_scale = HEAD_DIM ** -0.5 * LOG2E
    n_cols = w.shape[1]
    row = lambda bi, i: (bi, i, 0)
    return pl.pallas_call(
        functools.partial(_inproj_kernel, kv_tile=kv_tile, q_scale=q_scale),
        grid=(bsz, s // tm),
        in_specs=[
            pl.BlockSpec((1, tm, d), row),
            pl.BlockSpec((1, d), lambda bi, i: (0, 0)),
            pl.BlockSpec((1, d), lambda bi, i: (0, 0)),
            pl.BlockSpec((d, n_cols), lambda bi, i: (0, 0)),
        ],
        out_specs=[
            pl.BlockSpec((1, tm, ATT_W), row),
            pl.BlockSpec((1, tm, ATT_W), row),
            pl.BlockSpec((1, ATT_HEADS, tm // kv_tile, V_ROWS, kv_tile), lambda bi, i: (bi, 0, i, 0, 0)),
            pl.BlockSpec((1, tm, SSM_W), row),
        ],
        out_shape=[
            jax.ShapeDtypeStruct((bsz, s, ATT_W), BF16),
            jax.ShapeDtypeStruct((bsz, s, ATT_W), BF16),
            jax.ShapeDtypeStruct((bsz, ATT_HEADS, s // kv_tile, V_ROWS, kv_tile), BF16),
            jax.ShapeDtypeStruct((bsz, s, SSM_W), BF16),
        ],
        compiler_params=pltpu.CompilerParams(
            dimension_semantics=("parallel", "parallel"), vmem_limit_bytes=VMEM_LIMIT),
        name="inproj",
    )(x, g, b, w)


def _attn_kernel(lam_ref, q_ref, k_ref, vt_ref, km_ref, vtm_ref, bias_ref, g_ref, o_ref,
                 m_ref, acc_ref, qz_ref, sa_ref, sb_ref, sn_ref, *, out_scale):
    i = pl.program_id(2)
    qt = q_ref[0].astype(F32).T
    feat = lax.broadcasted_iota(I32, qt.shape, 0)
    qz_ref[:, 0:Q_TILE] = jnp.where(feat < HEAD_DIM, qt, 0.0).astype(BF16)
    qz_ref[:, Q_TILE:] = jnp.where(feat >= HEAD_DIM, qt, 0.0).astype(BF16)

    m_ref[...] = jnp.full(m_ref.shape, NEG_INF, F32)
    acc_ref[...] = jnp.zeros(acc_ref.shape, F32)

    def scores(kt):
        return _dot(kt, qz_ref[...])

    def k_tile(t):
        return k_ref[0, pl.ds(pl.multiple_of(t * KV_TILE, KV_TILE), KV_TILE), :]

    def absorb(s, pv):
        m_old = m_ref[...]
        m_new = jnp.maximum(m_old, jnp.max(s, axis=0, keepdims=True))
        alpha = jnp.exp2(m_old - m_new)
        p = jnp.exp2(s - m_new).astype(BF16)
        acc_ref[...] = acc_ref[...] * alpha + pv(p)
        m_ref[...] = m_new

    def tile_pv(t):
        return lambda p: _dot(vt_ref[0, 0, t], p)

    n_far = jnp.maximum(i - 1, 0)
    peel = n_far % 2

    sn_ref[...] = scores(km_ref[...])
    sa_ref[...] = scores(k_tile(0))
    absorb(sn_ref[...] + bias_ref[0, 0, 2 * KV_TILE:, :], lambda p: _dot(vtm_ref[0], p))

    @pl.when(peel == 1)
    def _():
        sb_ref[...] = scores(k_tile(1))
        absorb(sa_ref[...], tile_pv(0))
        sa_ref[...] = sb_ref[...]

    def far_pair(j, carry):
        t0 = peel + 2 * j
        s_cur = sa_ref[...]
        sb_ref[...] = scores(k_tile(t0 + 1))
        absorb(s_cur, tile_pv(t0))
        s_cur = sb_ref[...]
        sa_ref[...] = scores(k_tile(t0 + 2))
        absorb(s_cur, tile_pv(t0 + 1))
        return carry

    lax.fori_loop(0, n_far // 2, far_pair, 0)

    sb_ref[...] = scores(k_tile(i))
    absorb(sa_ref[...] + bias_ref[0, 0, 0:KV_TILE, :], tile_pv(n_far))
    absorb(sb_ref[...] + bias_ref[0, 0, KV_TILE:2 * KV_TILE, :], tile_pv(i))

    acc = acc_ref[...]
    lam = lam_ref[0]
    o1 = acc[0:HEAD_W, 0:Q_TILE] / acc[HEAD_W:HEAD_W + 1, 0:Q_TILE]
    o2 = acc[0:HEAD_W, Q_TILE:] / acc[HEAD_W:HEAD_W + 1, Q_TILE:]
    o = o1 - lam * o2
    ms = jnp.mean(o * o, axis=0, keepdims=True)
    o = o * lax.rsqrt(ms + LN_EPS) * g_ref[...] * out_scale
    o_ref[0] = o.T.astype(o_ref.dtype)


def _attention(lam, q, k, vt, k_meta, vt_meta, bias, subln_g, *, lam_init):
    bsz, s, _ = q.shape
    nq = s // Q_TILE
    n_near = META_PAD + 2 * KV_TILE
    grid_spec = pltpu.PrefetchScalarGridSpec(
        num_scalar_prefetch=1,
        grid=(bsz, ATT_HEADS, nq),
        in_specs=[
            pl.BlockSpec((1, Q_TILE, HEAD_W), lambda b, h, i, lam: (b, i, h)),
            pl.BlockSpec((1, s, HEAD_W), lambda b, h, i, lam: (b, 0, h)),
            pl.BlockSpec((1, 1, s // KV_TILE, V_ROWS, KV_TILE), lambda b, h, i, lam: (b, h, 0, 0, 0)),
            pl.BlockSpec((META_PAD, HEAD_W), lambda b, h, i, lam: (0, h)),
            pl.BlockSpec((1, V_ROWS, META_PAD), lambda b, h, i, lam: (h, 0, 0)),
            pl.BlockSpec((1, 1, n_near, 2 * Q_TILE), lambda b, h, i, lam: (h, jnp.minimum(i, 1), 0, 0)),
            pl.BlockSpec((HEAD_W, 1), lambda b, h, i, lam: (0, 0)),
        ],
        out_specs=pl.BlockSpec((1, Q_TILE, HEAD_W), lambda b, h, i, lam: (b, i, h)),
        scratch_shapes=[
            pltpu.VMEM((1, 2 * Q_TILE), F32), pltpu.VMEM((V_ROWS, 2 * Q_TILE), F32),
            pltpu.VMEM((HEAD_W, 2 * Q_TILE), BF16),
            pltpu.VMEM((KV_TILE, 2 * Q_TILE), F32), pltpu.VMEM((KV_TILE, 2 * Q_TILE), F32),
            pltpu.VMEM((META_PAD, 2 * Q_TILE), F32),
        ],
    )
    return pl.pallas_call(
        functools.partial(_attn_kernel, out_scale=1.0 - lam_init),
        grid_spec=grid_spec,
        out_shape=jax.ShapeDtypeStruct((bsz, s, ATT_W), BF16),
        compiler_params=pltpu.CompilerParams(
            dimension_semantics=("parallel", "parallel", "arbitrary"), vmem_limit_bytes=VMEM_LIMIT),
        name="diff_attention",
    )(lam, q, k, vt, k_meta, vt_meta, bias, subln_g)


def _t5_bucket(rel):
    nb = N_BUCKETS // 2
    max_exact = nb // 2
    ret = jnp.where(rel > 0, nb, 0)
    n = jnp.abs(rel)
    n_f = jnp.maximum(n, 1).astype(F32)
    large = max_exact + (jnp.log(n_f / max_exact) / math.log(MAX_DISTANCE / max_exact)
                         * (nb - max_exact)).astype(I32)
    large = jnp.minimum(large, nb - 1)
    return ret + jnp.where(n < max_exact, n, large)


def _near_bias(rel_bias):
    table = rel_bias.astype(F32)
    far = table[N_BUCKETS // 2 - 1]
    c = jnp.arange(Q_TILE, dtype=I32)[None, :]
    r = jnp.arange(KV_TILE, dtype=I32)[:, None]

    shifted = jnp.transpose(table - far) * LOG2E

    def bias_of(offset, rows, cols=Q_TILE):
        rel = jnp.arange(rows, dtype=I32)[:, None] - c[:, :cols] + offset
        hot = _t5_bucket(rel)[None, :, :, None] == jnp.arange(N_BUCKETS, dtype=I32)
        return jnp.sum(jnp.where(hot, shifted[:, None, None, :], 0.0), axis=-1)

    nd = KV_TILE // MAX_DISTANCE
    own_bias = sum(
        (jnp.eye(nd, k=-d, dtype=F32)[None, :, None, :, None]
         * bias_of(d * MAX_DISTANCE, MAX_DISTANCE, MAX_DISTANCE)[:, None, :, None, :]).reshape(
             ATT_HEADS, KV_TILE, Q_TILE)
        for d in (-1, 0, 1))
    own = jnp.where((r // CHUNK <= c // CHUNK)[None], own_bias, NEG_INF)
    near = MAX_DISTANCE
    prev = jnp.zeros((ATT_HEADS, KV_TILE, Q_TILE), F32).at[:, KV_TILE - near:, :near].set(bias_of(-near, near, near))
    rm = jnp.arange(META_PAD, dtype=I32)[:, None]
    meta_ok = (rm < N_META)[None]
    meta0 = jnp.where(meta_ok, jnp.pad(bias_of(-N_META, N_META), ((0, 0), (0, META_PAD - N_META), (0, 0))), NEG_INF)
    meta1 = jnp.where(meta_ok, jnp.zeros((ATT_HEADS, META_PAD, Q_TILE), F32), NEG_INF)
    assert N_META + Q_TILE - (N_META - 1) >= MAX_DISTANCE and KV_TILE >= near and Q_TILE >= near
    v0 = jnp.concatenate([jnp.full_like(prev, NEG_INF), own, meta0], axis=1)
    v1 = jnp.concatenate([prev, own, meta1], axis=1)
    both = jnp.stack([v0, v1], axis=1)
    return jnp.concatenate([both, both], axis=-1)


def _ssm_tables(a_re, a_im, log_step, b_re, b_im, c_re, c_im, d_skip):
    hi = lax.Precision.HIGHEST
    t_len = SSM_T
    step = jnp.exp(log_step.astype(F32))[:, None]
    ar = jnp.minimum(a_re.astype(F32), -1e-4)
    ai = a_im.astype(F32)
    mag = jnp.exp(step * ar)
    ph = step * ai
    abar_re = mag * jnp.cos(ph)
    abar_im = mag * jnp.sin(ph)
    den = ar * ar + ai * ai
    e_re = abar_re - 1.0
    e_im = abar_im
    f_re = (e_re * ar + e_im * ai) / den
    f_im = (e_im * ar - e_re * ai) / den
    br = b_re.astype(F32)
    bi = b_im.astype(F32)
    bb_re = f_re[..., None] * br - f_im[..., None] * bi
    bb_im = f_re[..., None] * bi + f_im[..., None] * br
    tau = jnp.arange(t_len + 1, dtype=F32)[None, :, None]
    pmag = jnp.exp(tau * (step * ar)[:, None, :])
    pph = tau * ph[:, None, :]
    pw_re = pmag * jnp.cos(pph)
    pw_im = pmag * jnp.sin(pph)
    bbt_re = jnp.swapaxes(bb_re, 1, 2)[:, None]
    bbt_im = jnp.swapaxes(bb_im, 1, 2)[:, None]

    def times_bbar(p_re, p_im):
        return (p_re[:, :, None, :] * bbt_re - p_im[:, :, None, :] * bbt_im,
                p_re[:, :, None, :] * bbt_im + p_im[:, :, None, :] * bbt_re)

    cr = c_re.astype(F32)
    ci = c_im.astype(F32)
    bi_re = jnp.swapaxes(bb_re, 1, 2)[:, :, None, :]
    bi_im = jnp.swapaxes(bb_im, 1, 2)[:, :, None, :]
    cb_re = cr[:, None] * bi_re - ci[:, None] * bi_im
    cb_im = cr[:, None] * bi_im + ci[:, None] * bi_re
    kern = jnp.einsum('gikm,gtm->gitk', jnp.concatenate([cb_re, -cb_im], axis=-1),
                      jnp.concatenate([pw_re[:, :t_len], pw_im[:, :t_len]], axis=-1), precision=hi)
    skip = d_skip.astype(F32).reshape(SSM_G, SSM_CG)
    kern = kern.at[:, :, 0, :].add(skip[:, None, :] * jnp.eye(SSM_CG, dtype=F32))
    kflat = kern.reshape(SSM_G, SSM_CG, t_len * SSM_CG)
    kpad = jnp.concatenate([jnp.zeros_like(kflat), kflat], axis=-1)
    back = jnp.arange(t_len - 1, -1, -1, dtype=F32)[None, :, None]
    bmag = jnp.exp(back * (step * ar)[:, None, :])
    bph = back * ph[:, None, :]
    pt_re, pt_im = times_bbar(bmag * jnp.cos(bph), bmag * jnp.sin(bph))
    pt = jnp.concatenate([pt_re, pt_im], axis=-1).reshape(SSM_G, t_len * SSM_CG, 2 * SSM_N)
    up_re = jnp.swapaxes(pw_re[:, 1:], 1, 2)[..., None]
    up_im = jnp.swapaxes(pw_im[:, 1:], 1, 2)[..., None]
    crt = jnp.swapaxes(cr, 1, 2)[:, :, None, :]
    cit = jnp.swapaxes(ci, 1, 2)[:, :, None, :]
    q_re = crt * up_re - cit * up_im
    q_im = crt * up_im + cit * up_re
    qt = jnp.concatenate([q_re, -q_im], axis=1).reshape(SSM_G, 2 * SSM_N, t_len * SSM_CG)
    at_re = pw_re[:, t_len]
    at_im = pw_im[:, t_len]
    a1 = jnp.concatenate([at_re, at_re], axis=-1)
    a2 = jnp.concatenate([-at_im, at_im], axis=-1)
    return kpad, pt.astype(BF16), qt.astype(BF16), a1, a2


def _ssm_state_kernel(u_ref, pt_ref, s_ref):
    s_ref[0] = _dot(u_ref[0], pt_ref[0])


def _ssm_scan_kernel(s_ref, x0_ref, a1_ref, a2_ref, x_ref, *, n_chunks, bsz):
    a1 = a1_ref[...][:, None, :]
    a2 = a2_ref[...][:, None, :]

    def body(c, x):
        rows = pl.ds(c * bsz, bsz)
        x_ref[:, rows, :] = x
        return a1 * x + a2 * pltpu.roll(x, SSM_N, 2) + s_ref[:, rows, :]

    x0 = jnp.broadcast_to(x0_ref[:, 0:1, :], (SSM_G, bsz, 2 * SSM_N))
    lax.fori_loop(0, n_chunks, body, x0)


def _ssm_out_kernel(u_ref, x_ref, kp_ref, qt_ref, y_ref, mt_ref):
    kp = kp_ref[0]
    width = SSM_T * SSM_CG
    lane_tile = 128
    for sub in range(0, lane_tile, SSM_CG):
        shifted = kp if sub == 0 else pltpu.roll(kp, 2 * width - sub, 1)
        for s in range(SSM_T):
            off = (SSM_T - s) * SSM_CG
            if off % lane_tile == sub:
                base = off - sub
                mt_ref[s * SSM_CG:(s + 1) * SSM_CG, :] = shifted[:, base:base + width].astype(BF16)
    x = x_ref[0]
    x_hi = x.astype(BF16)
    x_lo = (x - x_hi.astype(F32)).astype(BF16)
    y = _dot(u_ref[0], mt_ref[...]) + _dot(x_hi, qt_ref[0]) + _dot(x_lo, qt_ref[0])
    y_ref[0] = y.astype(y_ref.dtype)


def _ssm_chunk_state(ug, pt):
    g, r, w = ug.shape
    n2 = 2 * SSM_N
    per_g = lambda gi: (gi, 0, 0)
    return pl.pallas_call(
        _ssm_state_kernel,
        grid=(g,),
        in_specs=[pl.BlockSpec((1, r, w), per_g), pl.BlockSpec((1, w, n2), per_g)],
        out_specs=pl.BlockSpec((1, r, n2), per_g),
        out_shape=jax.ShapeDtypeStruct((g, r, n2), F32),
        compiler_params=pltpu.CompilerParams(dimension_semantics=("parallel",), vmem_limit_bytes=VMEM_LIMIT),
        name="ssm_chunk_state",
    )(ug, pt)


def _ssm(ug, ug_meta, kpad, pt, qt, a1, a2, *, n_chunks, bsz):
    g, r, w = ug.shape
    n2 = 2 * SSM_N
    per_g = lambda gi: (gi, 0, 0)
    params = pltpu.CompilerParams(dimension_semantics=("parallel",), vmem_limit_bytes=VMEM_LIMIT)
    s = _ssm_chunk_state(ug, pt)
    x0 = _ssm_chunk_state(ug_meta, pt[:, w - ug_meta.shape[2]:, :])
    x = pl.pallas_call(
        functools.partial(_ssm_scan_kernel, n_chunks=n_chunks, bsz=bsz),
        out_shape=jax.ShapeDtypeStruct((g, r, n2), F32),
        compiler_params=pltpu.CompilerParams(vmem_limit_bytes=VMEM_LIMIT),
        name="ssm_chunk_scan",
    )(s, x0, a1, a2)
    return pl.pallas_call(
        _ssm_out_kernel,
        grid=(g,),
        in_specs=[pl.BlockSpec((1, r, w), per_g), pl.BlockSpec((1, r, n2), per_g),
                  pl.BlockSpec((1, SSM_CG, 2 * w), per_g), pl.BlockSpec((1, n2, w), per_g)],
        out_specs=pl.BlockSpec((1, r, w), per_g),
        out_shape=jax.ShapeDtypeStruct((g, r, w), BF16),
        scratch_shapes=[pltpu.VMEM((w, w), BF16)],
        compiler_params=params,
        name="ssm_output",
    )(ug, x, kpad, qt)


def _mix_kernel(x_ref, gi_ref, bi_ref, att_ref, y_ref, wglu_ref, bglu_ref, wout_ref, g1_ref, b1_ref,
                wr_ref, br_ref, h1_ref, idx_ref, gate_ref, rank_ref, cnt_ref):
    h0 = _layer_norm(x_ref[...], gi_ref[...], bi_ref[...])
    y = y_ref[...].astype(F32)
    y = y * (0.5 * (1.0 + jnp.tanh(math.sqrt(2.0 / math.pi) * (y + 0.044715 * (y * y * y)))))
    y = y * jax.nn.sigmoid(_dot(y.astype(BF16), wglu_ref[...]) + bglu_ref[...])
    mix = _dot(att_ref[...], wout_ref[0:ATT_W, :]) + _dot(y.astype(BF16), wout_ref[ATT_W:, :])
    h1 = _layer_norm(DEEPNORM_ALPHA * h0 + mix, g1_ref[...], b1_ref[...])
    h1_ref[...] = h1

    h_hi = h1.astype(BF16)
    h_lo = (h1 - h_hi.astype(F32)).astype(BF16)
    w_hi = wr_ref[...].astype(BF16)
    w_lo = (wr_ref[...] - w_hi.astype(F32)).astype(BF16)
    logits = _dot_nt(w_hi, h_hi) + _dot_nt(w_hi, h_lo) + _dot_nt(w_lo, h_hi) + br_ref[...]
    tm = logits.shape[1]
    eidx = lax.broadcasted_iota(I32, logits.shape, 0)
    vals, hots = [], []
    rest = logits
    for _ in range(TOP_K):
        mx = jnp.max(rest, axis=0, keepdims=True)
        first = jnp.min(jnp.where(rest == mx, eidx, N_EXPERTS), axis=0, keepdims=True)
        hot = eidx == first
        vals.append(mx)
        hots.append(hot)
        rest = jnp.where(hot, -jnp.inf, rest)
    exps = [jnp.exp(v - vals[0]) for v in vals]
    denom = exps[0] + exps[1] + exps[2] + exps[3]
    gate_ref[...] = jnp.concatenate([e / denom for e in exps], axis=0)
    idx_ref[...] = jnp.concatenate(
        [jnp.sum(jnp.where(h, eidx, 0), axis=0, keepdims=True) for h in hots], axis=0)

    hot_all = (hots[0] | hots[1] | hots[2] | hots[3]).astype(F32)
    sub = ROW_TILE
    tri = (lax.broadcasted_iota(I32, (sub, sub), 0) < lax.broadcasted_iota(I32, (sub, sub), 1)).astype(BF16)
    for part in range(tm // sub):
        cols = slice(part * sub, (part + 1) * sub)
        before = _dot(hot_all[:, cols].astype(BF16), tri)
        rank_ref[:, cols] = jnp.concatenate(
            [jnp.sum(jnp.where(h[:, cols], before, 0.0), axis=0, keepdims=True) for h in hots], axis=0).astype(I32)
        cnt_ref[part] = jnp.broadcast_to(jnp.sum(hot_all[:, cols], axis=1, keepdims=True), cnt_ref.shape[1:])


def _mix(x2, gi, bi, att, y, wglu, bglu, wout, g1, b1, wr_t, br):
    t, d = x2.shape
    tm = MIX_TILE
    row = lambda i: (i, 0)
    col = lambda i: (0, i)
    const = lambda i: (0, 0)
    return pl.pallas_call(
        _mix_kernel,
        grid=(t // tm,),
        in_specs=[
            pl.BlockSpec((tm, d), row), pl.BlockSpec((1, d), const), pl.BlockSpec((1, d), const),
            pl.BlockSpec((tm, ATT_W), row), pl.BlockSpec((tm, SSM_W), row),
            pl.BlockSpec((SSM_W, SSM_W), const), pl.BlockSpec((1, SSM_W), const),
            pl.BlockSpec((d, d), const), pl.BlockSpec((1, d), const), pl.BlockSpec((1, d), const),
            pl.BlockSpec((N_EXPERTS, d), const), pl.BlockSpec((N_EXPERTS, 1), const),
        ],
        out_specs=[
            pl.BlockSpec((tm, d), row),
            pl.BlockSpec((TOP_K, tm), col), pl.BlockSpec((TOP_K, tm), col), pl.BlockSpec((TOP_K, tm), col),
            pl.BlockSpec((tm // ROW_TILE, N_EXPERTS, 128), lambda i: (i, 0, 0)),
        ],
        out_shape=[
            jax.ShapeDtypeStruct((t, d), F32),
            jax.ShapeDtypeStruct((TOP_K, t), I32), jax.ShapeDtypeStruct((TOP_K, t), F32),
            jax.ShapeDtypeStruct((TOP_K, t), I32),
            jax.ShapeDtypeStruct((t // ROW_TILE, N_EXPERTS, 128), F32),
        ],
        compiler_params=pltpu.CompilerParams(dimension_semantics=("parallel",), vmem_limit_bytes=VMEM_LIMIT),
        name="mix_ln1_router",
    )(x2, gi, bi, att, y, wglu, bglu, wout, g1, b1, wr_t, br)


_HI_BITS = -65536


def _pack_halves(x):
    half = x.shape[1] // 2
    lo = lax.bitcast_convert_type(x[:, :half], I32)
    hi = lax.bitcast_convert_type(x[:, half:], I32)
    return lax.shift_right_logical(lo, 16) | (hi & _HI_BITS)


def _unpack_halves(w):
    lo = lax.bitcast_convert_type(lax.shift_left(w, 16), F32)
    hi = lax.bitcast_convert_type(w & _HI_BITS, F32)
    return lo.astype(BF16), hi.astype(BF16)


def _group_copy(src, src_group, dst, dst_group, sem):
    return pltpu.make_async_copy(src.at[pl.ds(pl.multiple_of(src_group * GROUP, GROUP), GROUP), :],
                                 dst.at[pl.ds(pl.multiple_of(dst_group * GROUP, GROUP), GROUP), :], sem)


def _batched(count, fn):
    def body(q, c):
        fn(q * COPY_BATCH, COPY_BATCH)
        return c

    whole = count // COPY_BATCH
    lax.fori_loop(0, whole, body, 0)
    rest = count - whole * COPY_BATCH
    n, done = COPY_BATCH // 2, whole * COPY_BATCH
    while n >= 1:
        take = (rest // n) % 2

        @pl.when(take == 1)
        def _(n=n, done=done):
            fn(done, n)

        done = done + take * n
        n //= 2


def _dispatch_kernel(ngroups_ref, pad_start_ref, pad_count_ref, n_used_ref, dstg_ref, pos_ref, h1_ref, xs_hbm,
                     sorted_ref, zero_ref, sems, zsem):
    step = pl.program_id(0)
    n_steps = pl.num_programs(0)
    slot = step % 2
    n_blocks = xs_hbm.shape[0] // MOE_BLOCK

    def zero_block(blk):
        return pltpu.make_async_copy(zero_ref, xs_hbm.at[pl.ds(blk * MOE_BLOCK, MOE_BLOCK), :], zsem)

    @pl.when(step == 0)
    def _():
        zero_ref[...] = jnp.zeros(zero_ref.shape, I32)
        for e in range(N_EXPERTS):
            start = pad_start_ref[e]
            count = pad_count_ref[e]

            def fill(j, c):
                _group_copy(zero_ref, 0, xs_hbm, start + j, zsem).start()
                return c

            def drain(j, c):
                _group_copy(zero_ref, 0, xs_hbm, start, zsem).wait()
                return c

            lax.fori_loop(0, count, fill, 0)
            lax.fori_loop(0, count, drain, 0)

        def fill_block(blk, c):
            zero_block(blk).start()
            return c

        def drain_block(blk, c):
            zero_block(blk).wait()
            return c

        lax.fori_loop(n_used_ref[0], n_blocks, fill_block, 0)
        lax.fori_loop(n_used_ref[0], n_blocks, drain_block, 0)

    pos = pos_ref[...]
    rows = lax.broadcasted_iota(I32, (SORT_ROWS, pos.shape[1]), 0)
    place = rows == pos[0:1]
    for k in range(1, TOP_K):
        place = place | (rows == pos[k:k + 1])
    sorted_ref[slot] = _pack_halves(_dot(place.astype(BF16), h1_ref[...].astype(BF16)))

    _batched(ngroups_ref[step],
             lambda j, n: [_group_copy(sorted_ref.at[slot], j + r, xs_hbm, dstg_ref[0, 0, j + r],
                                       sems.at[slot]).start() for r in range(n)])

    def wait_tile(s, count):
        _batched(count, lambda j, n: pltpu.make_async_copy(
            sorted_ref.at[s, pl.ds(0, n * GROUP), :], xs_hbm.at[pl.ds(0, n * GROUP), :], sems.at[s]).wait())

    @pl.when(step > 0)
    def _():
        wait_tile(1 - slot, ngroups_ref[jnp.maximum(step - 1, 0)])

    @pl.when(step == n_steps - 1)
    def _():
        wait_tile(slot, ngroups_ref[step])


def _dispatch(ngroups, pad_start, pad_count, n_used, dstg, pos_t, h1, n_rows):
    t, d = h1.shape
    tm = ROW_TILE
    grid_spec = pltpu.PrefetchScalarGridSpec(
        num_scalar_prefetch=4,
        grid=(t // tm,),
        in_specs=[
            pl.BlockSpec((1, 1, SORT_GROUPS), lambda i, *_: (i, 0, 0), memory_space=pltpu.SMEM),
            pl.BlockSpec((TOP_K, tm), lambda i, *_: (0, i)),
            pl.BlockSpec((tm, d), lambda i, *_: (i, 0)),
        ],
        out_specs=pl.BlockSpec(memory_space=pl.ANY),
        scratch_shapes=[pltpu.VMEM((2, SORT_ROWS, d // 2), I32), pltpu.VMEM((MOE_BLOCK, d // 2), I32),
                        pltpu.SemaphoreType.DMA((2,)), pltpu.SemaphoreType.DMA],
    )
    return pl.pallas_call(
        _dispatch_kernel,
        grid_spec=grid_spec,
        out_shape=jax.ShapeDtypeStruct((n_rows, d // 2), I32),
        compiler_params=pltpu.CompilerParams(dimension_semantics=("arbitrary",), vmem_limit_bytes=VMEM_LIMIT,
                                             has_side_effects=True),
        name="moe_dispatch",
    )(ngroups, pad_start, pad_count, n_used, dstg, pos_t, h1)


def _expert_kernel(be_ref, nb_ref, x_ref, wg_ref, bg_ref, wu_ref, bu_ref, wd_ref, bd_ref, y_ref,
                   wg_b, wu_b, wd_b):
    i = pl.program_id(0)

    @pl.when(i < nb_ref[0])
    def _():
        @pl.when((i == 0) | (be_ref[i] != be_ref[jnp.maximum(i - 1, 0)]))
        def _():
            wg_b[...] = wg_ref[0].astype(BF16)
            wu_b[...] = wu_ref[0].astype(BF16)
            wd_b[...] = wd_ref[0].astype(BF16)

        x_lo, x_hi = _unpack_halves(x_ref[...])
        half = x_lo.shape[1]
        gate = _dot(x_lo, wg_b[0:half, :]) + _dot(x_hi, wg_b[half:, :]) + bg_ref[0]
        up = _dot(x_lo, wu_b[0:half, :]) + _dot(x_hi, wu_b[half:, :]) + bu_ref[0]
        gate = jnp.minimum(gate, SWIGLU_LIMIT)
        up = jnp.clip(up, -SWIGLU_LIMIT, SWIGLU_LIMIT)
        act = (up + 1.0) * gate * jax.nn.sigmoid(gate * SWIGLU_ALPHA)
        y = _dot(act.astype(BF16), wd_b[...]) + bd_ref[0]
        y_ref[...] = _pack_halves(y.astype(BF16).astype(F32))

    @pl.when(i >= nb_ref[0])
    def _():
        y_ref[...] = jnp.zeros(y_ref.shape, I32)


def _experts(block_expert, n_used, xs, wg, bg, wu, bu, wd, bd):
    n_rows, packed = xs.shape
    d, dff = wg.shape[1], wg.shape[2]
    assert packed * 2 == d and wd.shape[2] == d
    nb = n_rows // MOE_BLOCK
    last = lambda nu: jnp.maximum(nu[0] - 1, 0)
    blk = lambda i, be, nu: (jnp.minimum(i, last(nu)), 0)
    wsel = lambda i, be, nu: (be[jnp.minimum(i, last(nu))], 0, 0)
    grid_spec = pltpu.PrefetchScalarGridSpec(
        num_scalar_prefetch=2,
        grid=(nb,),
        in_specs=[
            pl.BlockSpec((MOE_BLOCK, packed), blk),
            pl.BlockSpec((1, d, dff), wsel), pl.BlockSpec((1, 1, dff), wsel),
            pl.BlockSpec((1, d, dff), wsel), pl.BlockSpec((1, 1, dff), wsel),
            pl.BlockSpec((1, dff, d), wsel), pl.BlockSpec((1, 1, d), wsel),
        ],
        out_specs=pl.BlockSpec((MOE_BLOCK, packed), lambda i, be, nu: (i, 0)),
        scratch_shapes=[pltpu.VMEM((d, dff), BF16), pltpu.VMEM((d, dff), BF16), pltpu.VMEM((dff, d), BF16)],
    )
    return pl.pallas_call(
        _expert_kernel,
        grid_spec=grid_spec,
        out_shape=jax.ShapeDtypeStruct((n_rows, packed), I32),
        compiler_params=pltpu.CompilerParams(dimension_semantics=("arbitrary",), vmem_limit_bytes=VMEM_LIMIT),
        name="moe_experts",
    )(block_expert, n_used, xs, wg, bg, wu, bu, wd, bd)


def _combine_kernel(ngroups_ref, dstg_ref, dstg_next_ref, pos_ref, gate_ref, h1_ref, ys_hbm, g2_ref, b2_ref,
                    o_ref, buf, sems):
    step = pl.program_id(0)
    n_steps = pl.num_programs(0)
    slot = step % 2

    def fetch(dref, s, j):
        return _group_copy(ys_hbm, dref[0, 0, j], buf.at[s], j, sems.at[s])

    def issue(dref, s, count):
        _batched(count, lambda j, n: [fetch(dref, s, j + r).start() for r in range(n)])

    @pl.when(step == 0)
    def _():
        buf[...] = jnp.zeros(buf.shape, I32)
        issue(dstg_ref, 0, ngroups_ref[0])

    @pl.when(step + 1 < n_steps)
    def _():
        issue(dstg_next_ref, 1 - slot, ngroups_ref[jnp.minimum(step + 1, n_steps - 1)])

    _batched(ngroups_ref[step], lambda j, n: pltpu.make_async_copy(
        ys_hbm.at[pl.ds(0, n * GROUP), :], buf.at[slot, pl.ds(0, n * GROUP), :], sems.at[slot]).wait())

    pos = pos_ref[...]
    gate = gate_ref[...]
    lanes = lax.broadcasted_iota(I32, (pos.shape[0], SORT_ROWS), 1)
    sel = jnp.where(lanes == pos[:, 0:1], gate[:, 0:1], 0.0)
    for k in range(1, TOP_K):
        sel = sel + jnp.where(lanes == pos[:, k:k + 1], gate[:, k:k + 1], 0.0)
    sel_hi = sel.astype(BF16)
    sel_lo = (sel - sel_hi.astype(F32)).astype(BF16)
    y_lo, y_hi = _unpack_halves(buf[slot])
    ffn = jnp.concatenate([_dot(sel_hi, y_lo) + _dot(sel_lo, y_lo), _dot(sel_hi, y_hi) + _dot(sel_lo, y_hi)], axis=1)
    o_ref[...] = _layer_norm(DEEPNORM_ALPHA * h1_ref[...] + ffn, g2_ref[...], b2_ref[...])


def _combine(ngroups, dstg, pos_c, gates, h1, ys, g2, b2):
    t, d = h1.shape
    tm = ROW_TILE
    n_steps = t // tm
    row = lambda i, ng: (i, 0)
    const = lambda i, ng: (0, 0)
    grid_spec = pltpu.PrefetchScalarGridSpec(
        num_scalar_prefetch=1,
        grid=(n_steps,),
        in_specs=[
            pl.BlockSpec((1, 1, SORT_GROUPS), lambda i, ng: (i, 0, 0), memory_space=pltpu.SMEM),
            pl.BlockSpec((1, 1, SORT_GROUPS), lambda i, ng: (jnp.minimum(i + 1, n_steps - 1), 0, 0),
                         memory_space=pltpu.SMEM),
            pl.BlockSpec((tm, TOP_K), row), pl.BlockSpec((tm, TOP_K), row), pl.BlockSpec((tm, d), row),
            pl.BlockSpec(memory_space=pl.ANY),
            pl.BlockSpec((1, d), const), pl.BlockSpec((1, d), const),
        ],
        out_specs=pl.BlockSpec((tm, d), row),
        scratch_shapes=[pltpu.VMEM((2, SORT_ROWS, d // 2), I32), pltpu.SemaphoreType.DMA((2,))],
    )
    return pl.pallas_call(
        _combine_kernel,
        grid_spec=grid_spec,
        out_shape=jax.ShapeDtypeStruct((t, d), F32),
        compiler_params=pltpu.CompilerParams(dimension_semantics=("arbitrary",), vmem_limit_bytes=VMEM_LIMIT),
        name="moe_combine_ln2",
    )(ngroups, dstg, dstg, pos_c, gates, h1, ys, g2, b2)


def kernel(x, meta_tokens, ln_in_g, ln_in_b, rel_bias, w_in, lambda_q1, lambda_k1, lambda_q2, lambda_k2,
           subln_g, a_re, a_im, log_step, b_re, b_im, c_re, c_im, d_skip, w_glu, b_glu, w_out, ln1_g, ln1_b,
           w_router, b_router, w_gate, b_gate, w_up, b_up, w_down, b_down, ln2_g, ln2_b):
    bsz, seq, dm = x.shape
    assert seq % 512 == 0 and w_in.shape[0] == DEPTH == 1
    layer = 0
    row2 = lambda v: v.astype(F32).reshape(1, -1)

    w_in_b = w_in[layer].astype(BF16)
    gi, bi = row2(ln_in_g), row2(ln_in_b)
    q, k, vt, u = _inproj(x, gi, bi, w_in_b, tm=512, kv_tile=KV_TILE)
    meta = jnp.zeros((1, META_PAD, dm), x.dtype).at[0, :N_META].set(meta_tokens.astype(x.dtype))
    _, k_meta, vt_meta, u_meta = _inproj(meta, gi, bi, w_in_b, tm=META_PAD, kv_tile=META_PAD)

    lam_init = 0.8 - 0.6 * math.exp(-0.3 * layer)
    lam = (jnp.exp(jnp.sum(lambda_q1[layer].astype(F32) * lambda_k1[layer].astype(F32)))
           - jnp.exp(jnp.sum(lambda_q2[layer].astype(F32) * lambda_k2[layer].astype(F32))) + lam_init)
    att = _attention(lam.reshape(1), q, k, vt, k_meta[0], vt_meta[0, :, 0], _near_bias(rel_bias),
                     subln_g[layer].astype(F32).reshape(HEAD_W, 1), lam_init=lam_init)

    kpad, pt, qt, a1, a2 = _ssm_tables(a_re[layer], a_im[layer], log_step[layer], b_re[layer], b_im[layer],
                                     c_re[layer], c_im[layer], d_skip[layer])
    n_chunks = seq // SSM_T
    assert (n_chunks * bsz) % 16 == 0
    ug = jnp.transpose(u.reshape(bsz, n_chunks, SSM_T, SSM_G, SSM_CG), (3, 1, 0, 2, 4))
    ug = ug.reshape(SSM_G, n_chunks * bsz, SSM_T * SSM_CG)
    lead = jnp.zeros((8, N_META, SSM_W), BF16).at[0].set(u_meta[0, :N_META])
    ug_meta = jnp.transpose(lead.reshape(8, N_META, SSM_G, SSM_CG), (2, 0, 1, 3)).reshape(SSM_G, 8, N_META * SSM_CG)
    yg = _ssm(ug, ug_meta, kpad, pt, qt, a1, a2, n_chunks=n_chunks, bsz=bsz)
    y_ssm = jnp.transpose(yg.reshape(SSM_G, n_chunks, bsz, SSM_T, SSM_CG), (2, 1, 3, 0, 4))
    y_ssm = y_ssm.reshape(bsz, seq, SSM_W)

    t = bsz * seq
    h1, idx_t, gate_t, rank_t, cnt = _mix(
        x.reshape(t, dm), gi, bi, att.reshape(t, ATT_W), y_ssm.reshape(t, SSM_W),
        w_glu[layer].astype(BF16), row2(b_glu[layer]), w_out[layer].astype(BF16),
        row2(ln1_g[layer]), row2(ln1_b[layer]),
        jnp.transpose(w_router[layer].astype(F32)), b_router[layer].astype(F32).reshape(N_EXPERTS, 1))

    n_tiles = t // ROW_TILE
    experts = jnp.arange(N_EXPERTS, dtype=I32)
    tiles = jnp.arange(n_tiles, dtype=I32)
    tile_cnt = jnp.max(cnt, axis=-1).astype(I32)
    run = (tile_cnt + GROUP - 1) // GROUP * GROUP
    run_off = jnp.sum(jnp.where((experts[:, None] < experts[None, :])[None], run[:, :, None], 0), axis=1)
    run_before = jnp.sum(jnp.where((tiles[:, None] < tiles[None, :])[:, :, None], run[:, None, :], 0), axis=0)
    counts = jnp.sum(run, axis=0)
    padded = (counts + MOE_BLOCK - 1) // MOE_BLOCK * MOE_BLOCK
    padded_end = jnp.sum(jnp.where(experts[:, None] <= experts[None, :], padded[:, None], 0), axis=0)
    padded_start = padded_end - padded
    n_blocks = (t * TOP_K + n_tiles * N_EXPERTS * (GROUP - 1)) // MOE_BLOCK + N_EXPERTS
    block_row0 = jnp.arange(n_blocks, dtype=I32) * MOE_BLOCK
    block_expert = jnp.minimum(jnp.sum((padded_end[None, :] <= block_row0[:, None]).astype(I32), axis=1),
                               N_EXPERTS - 1)
    n_used = (padded_end[-1:] // MOE_BLOCK).astype(I32)
    ngroups = jnp.sum(run, axis=1) // GROUP
    local_row = jnp.arange(SORT_GROUPS, dtype=I32) * GROUP
    owner = jnp.minimum(jnp.sum(((run_off + run)[:, None, :] <= local_row[None, :, None]).astype(I32), axis=-1),
                        N_EXPERTS - 1)
    shift = (padded_start[None, :] + run_before - run_off) // GROUP
    dstg = jnp.sum(jnp.where(owner[..., None] == experts, shift[:, None, :], 0), axis=-1) + local_row // GROUP
    dstg = jnp.where(local_row[None, :] // GROUP < ngroups[:, None], dstg, 0).reshape(n_tiles, 1, SORT_GROUPS)
    run_off_tok = jnp.repeat(run_off, ROW_TILE, axis=0)
    pos_t = jnp.sum(jnp.where(idx_t[..., None] == experts, run_off_tok[None], 0), axis=-1) + rank_t

    xs = _dispatch(ngroups, ((padded_start + counts) // GROUP).astype(I32),
                   ((padded - counts) // GROUP).astype(I32), n_used, dstg, pos_t, h1, n_blocks * MOE_BLOCK)
    b3 = lambda v: v.astype(F32)[:, None, :]
    ys = _experts(block_expert, n_used, xs, w_gate[layer], b3(b_gate[layer]), w_up[layer], b3(b_up[layer]),
                  w_down[layer], b3(b_down[layer]))
    out = _combine(ngroups, dstg, jnp.transpose(pos_t), jnp.transpose(gate_t), h1, ys,
                   row2(ln2_g[layer]), row2(ln2_b[layer]))
    return out.reshape(bsz, seq, dm)
```
